```python
import math
import jax, jax.numpy as jnp
from jax import lax
import numpy as np

D_MODEL = 1024
BATCH = 8
SEQ = 2048
DEPTH = 4

HEAD_DIM = 64
D_MIX = D_MODEL
A_HEADS = (3 * D_MIX) // (8 * HEAD_DIM)
A_WIDTH = A_HEADS * HEAD_DIM
IDX_HEADS = 4
IDX_DIM = HEAD_DIM
DSA_TOPK_MAX = 256
B_HEADS = D_MIX // (4 * HEAD_DIM)
B_QK_DIM = HEAD_DIM // 2
B_WIDTH = B_HEADS * HEAD_DIM
SUBLN_EPS = 1e-5
C_HEADS = A_HEADS
C_WIDTH = C_HEADS * HEAD_DIM
MOBA_BLOCK = 256
MOBA_TOPK = 3
ROPE_THETA = 500000.0
ROPE_FRACTION = 4
MAX_POS_OFFSET = 1024
Q_BLOCK = 128
MOBA_Q_BLOCK = 32
D_FF = 2816
N_EXPERTS = 8
TOP_K_EXPERTS = 2
D_FF_EXPERT = 2816
EPS = 1e-6
IN_COLS = (A_WIDTH, HEAD_DIM, HEAD_DIM, IDX_HEADS * IDX_DIM, IDX_DIM, IDX_HEADS,
           2 * B_HEADS * B_QK_DIM, 2 * B_HEADS * B_QK_DIM, B_WIDTH,
           C_WIDTH, C_WIDTH, C_WIDTH)
D_IN = sum(IN_COLS)

kernel_name = 'hybrid_dsa_diff_moba_moe_adaln'

F32 = jnp.float32


def _split_points():
    pts, acc = [], 0
    for w in IN_COLS[:-1]:
        acc += w
        pts.append(acc)
    return pts


def rms_norm(x, g, eps=EPS):
    xf = x.astype(F32)
    y = xf * lax.rsqrt(jnp.mean(xf * xf, axis=-1, keepdims=True) + eps)
    return (y * g.astype(F32)).astype(x.dtype)


def rope_tables(positions, dim):
    rot = dim // ROPE_FRACTION
    inv = 1.0 / (ROPE_THETA ** (np.arange(0, rot, 2, dtype=np.float32) / rot))
    ang = positions.astype(F32)[..., None] * jnp.asarray(inv, F32)
    return jnp.cos(ang), jnp.sin(ang)


def partial_rope(x, cos, sin):
    half = cos.shape[-1]
    xf = x.astype(F32)
    x1, x2, xp = xf[..., :half], xf[..., half:2 * half], xf[..., 2 * half:]
    c, s = cos[:, :, None, :], sin[:, :, None, :]
    return jnp.concatenate([x1 * c - x2 * s, x2 * c + x1 * s, xp], axis=-1).astype(x.dtype)


def dsa_attention(q, k, v, q_idx, k_idx, w_idx):
    B, S, H, dh = q.shape
    k_top = min(DSA_TOPK_MAX, S // 4)
    key_pos = jnp.arange(S)
    kf_idx = k_idx.astype(F32)
    gather = jax.vmap(lambda zz, ii: zz[ii])

    def one_block(i):
        t0 = i * Q_BLOCK
        t = t0 + jnp.arange(Q_BLOCK)
        qb = lax.dynamic_slice_in_dim(q, t0, Q_BLOCK, axis=1).astype(F32)
        qib = lax.dynamic_slice_in_dim(q_idx, t0, Q_BLOCK, axis=1).astype(F32)
        wb = lax.dynamic_slice_in_dim(w_idx, t0, Q_BLOCK, axis=1).astype(F32) * IDX_HEADS ** -0.5
        rel = jax.nn.relu(jnp.einsum('bqhd,bsd->bqhs', qib, kf_idx) * IDX_DIM ** -0.5)
        score_idx = jnp.einsum('bqh,bqhs->bqs', wb, rel)
        causal = key_pos[None, :] <= t[:, None]
        score_idx = jnp.where(causal[None], score_idx, -jnp.inf)
        _, sel = lax.top_k(score_idx, k_top)
        valid = sel <= t[None, :, None]
        kg = gather(k, sel).astype(F32)
        vg = gather(v, sel).astype(F32)
        s = jnp.einsum('bqhd,bqkd->bqhk', qb, kg) * dh ** -0.5
        s = jnp.where(valid[:, :, None, :], s, -jnp.inf)
        p = jax.nn.softmax(s, axis=-1)
        return jnp.einsum('bqhk,bqkd->bqhd', p, vg).astype(q.dtype)

    out = lax.map(one_block, jnp.arange(S // Q_BLOCK))
    return jnp.moveaxis(out, 0, 1).reshape(B, S, H * dh)


def diff_attention(q, k, v, lam, g_sub, lam_init):
    B, S, H, _, dq = q.shape
    dv = v.shape[-1]
    key_pos = jnp.arange(S)
    kf = k.astype(F32)
    vf = v.astype(F32)

    def one_block(i):
        t0 = i * Q_BLOCK
        t = t0 + jnp.arange(Q_BLOCK)
        qb = lax.dynamic_slice_in_dim(q, t0, Q_BLOCK, axis=1).astype(F32)
        s = jnp.einsum('bqhcd,bshcd->bhcqs', qb, kf) * dq ** -0.5
        causal = key_pos[None, :] <= t[:, None]
        p = jax.nn.softmax(jnp.where(causal, s, -jnp.inf), axis=-1)
        a = p[:, :, 0] - lam * p[:, :, 1]
        return jnp.einsum('bhqs,bshd->bqhd', a, vf)

    o = lax.map(one_block, jnp.arange(S // Q_BLOCK))
    o = jnp.moveaxis(o, 0, 1).reshape(B, S, H, dv)
    o = rms_norm(o, g_sub, SUBLN_EPS) * (1.0 - lam_init)
    return o.reshape(B, S, H * dv).astype(v.dtype)


def moba_attention(q, k, v):
    B, S, H, dh = q.shape
    nb = -(-S // MOBA_BLOCK)
    n_sel = min(MOBA_TOPK, nb - 1)
    pad = nb * MOBA_BLOCK - S

    def to_blocks(z):
        z = jnp.pad(z, ((0, 0), (0, pad), (0, 0), (0, 0)))
        return z.reshape(B, nb, MOBA_BLOCK, H, dh).transpose(0, 3, 1, 2, 4)

    kb = to_blocks(k)
    vb = to_blocks(v)
    k_mean = jnp.mean(kb.astype(F32), axis=3)
    qh = q.transpose(0, 2, 1, 3)
    scale = dh ** -0.5
    in_block = jnp.arange(MOBA_BLOCK)
    gather = jax.vmap(jax.vmap(lambda zz, ii: zz[ii]))

    def one_block(i):
        t0 = i * MOBA_Q_BLOCK
        t = t0 + jnp.arange(MOBA_Q_BLOCK)
        own = t0 // MOBA_BLOCK
        qb = lax.dynamic_slice_in_dim(qh, t0, MOBA_Q_BLOCK, axis=2).astype(F32)
        k_own = lax.dynamic_index_in_dim(kb, own, axis=2, keepdims=False).astype(F32)
        v_own = lax.dynamic_index_in_dim(vb, own, axis=2, keepdims=False).astype(F32)
        s_own = jnp.einsum('bhqd,bhkd->bhqk', qb, k_own) * scale
        s_own = jnp.where((own * MOBA_BLOCK + in_block)[None, :] <= t[:, None], s_own, -jnp.inf)
        if n_sel == 0:
            p_own = jax.nn.softmax(s_own, axis=-1)
            return jnp.einsum('bhqk,bhkd->bhqd', p_own, v_own).astype(q.dtype)
        n_past = t // MOBA_BLOCK
        gate = jnp.einsum('bhqd,bhnd->bhqn', qb, k_mean)
        gate = jnp.where(jnp.arange(nb)[None, :] < n_past[:, None], gate, -jnp.inf)
        _, sel = lax.top_k(gate, n_sel)
        kg = gather(kb, sel).astype(F32)
        vg = gather(vb, sel).astype(F32)
        s_sel = jnp.einsum('bhqd,bhqnkd->bhqnk', qb, kg) * scale
        sel_ok = jnp.arange(n_sel)[None, :] < n_past[:, None]
        s_sel = jnp.where(sel_ok[None, None, :, :, None], s_sel, -jnp.inf)
        n_keys = n_sel * MOBA_BLOCK
        s_all = jnp.concatenate([s_sel.reshape(B, H, MOBA_Q_BLOCK, n_keys), s_own], axis=-1)
        p = jax.nn.softmax(s_all, axis=-1)
        p_sel = p[..., :n_keys].reshape(B, H, MOBA_Q_BLOCK, n_sel, MOBA_BLOCK)
        p_own = p[..., n_keys:]
        o = jnp.einsum('bhqnk,bhqnkd->bhqd', p_sel, vg) + jnp.einsum('bhqk,bhkd->bhqd', p_own, v_own)
        return o.astype(q.dtype)

    out = lax.map(one_block, jnp.arange(S // MOBA_Q_BLOCK))
    return out.transpose(1, 0, 3, 2, 4).reshape(B, S, H * dh)


def hybrid_mixer(h, cos64, sin64, cos32, sin32, w_in, w_out, lam_vec, g_sub, lam_init):
    B, S, _ = h.shape
    proj = jnp.einsum('bsd,dn->bsn', h, w_in)
    (q_a, k_a, v_a, q_i, k_i, w_i, q_b, k_b, v_b, q_c, k_c, v_c) = jnp.split(proj, _split_points(), axis=-1)
    q_a = partial_rope(q_a.reshape(B, S, A_HEADS, HEAD_DIM), cos64, sin64)
    k_a = partial_rope(k_a.reshape(B, S, 1, HEAD_DIM), cos64, sin64)[:, :, 0]
    q_i = partial_rope(q_i.reshape(B, S, IDX_HEADS, IDX_DIM), cos64, sin64)
    k_i = partial_rope(k_i.reshape(B, S, 1, IDX_DIM), cos64, sin64)[:, :, 0]
    y_a = dsa_attention(q_a, k_a, v_a, q_i, k_i, w_i)
    q_b = partial_rope(q_b.reshape(B, S, 2 * B_HEADS, B_QK_DIM), cos32, sin32).reshape(B, S, B_HEADS, 2, B_QK_DIM)
    k_b = partial_rope(k_b.reshape(B, S, 2 * B_HEADS, B_QK_DIM), cos32, sin32).reshape(B, S, B_HEADS, 2, B_QK_DIM)
    v_b = v_b.reshape(B, S, B_HEADS, HEAD_DIM)
    lf = lam_vec.astype(F32)
    lam = jnp.exp(jnp.sum(lf[0] * lf[1])) - jnp.exp(jnp.sum(lf[2] * lf[3])) + lam_init
    y_b = diff_attention(q_b, k_b, v_b, lam, g_sub, lam_init)
    q_c = partial_rope(q_c.reshape(B, S, C_HEADS, HEAD_DIM), cos64, sin64)
    k_c = partial_rope(k_c.reshape(B, S, C_HEADS, HEAD_DIM), cos64, sin64)
    y_c = moba_attention(q_c, k_c, v_c.reshape(B, S, C_HEADS, HEAD_DIM))
    y = jnp.concatenate([y_a, y_b, y_c], axis=-1)
    return jnp.einsum('bsm,md->bsd', y, w_out)


def swiglu(h, w_gate, w_up, w_down):
    a = jnp.einsum('bsd,df->bsf', h, w_gate)
    u = jnp.einsum('bsd,df->bsf', h, w_up)
    return jnp.einsum('bsf,fd->bsd', jax.nn.silu(a) * u, w_down)


def moe_swiglu(h, w_router, w_gate, w_up, w_down):
    logits = jnp.einsum('bsd,de->bse', h.astype(F32), w_router.astype(F32))
    top_val, top_idx = lax.top_k(logits, TOP_K_EXPERTS)
    top_w = jax.nn.softmax(top_val, axis=-1)
    combine = jnp.sum(jax.nn.one_hot(top_idx, N_EXPERTS, dtype=F32) * top_w[..., None], axis=-2)
    out = jnp.zeros_like(h)
    for e in range(N_EXPERTS):
        out = out + combine[..., e:e + 1].astype(h.dtype) * swiglu(h, w_gate[e], w_up[e], w_down[e])
    return out


def setup_inputs(seed: int = 0) -> dict:
    key = jax.random.key(seed)
    ks = jax.random.split(key, 20)
    n_dense = (DEPTH + 1) // 2
    n_moe = DEPTH // 2

    def nrm(k, shape, scale):
        return jax.random.normal(k, shape, F32) * scale

    x = nrm(ks[0], (BATCH, SEQ, D_MODEL), 1.0)
    c = nrm(ks[1], (BATCH, D_MODEL), 1.0)
    offset = jax.random.randint(ks[2], (BATCH, 1), 0, MAX_POS_OFFSET, dtype=jnp.int32)
    positions = offset + jnp.arange(SEQ, dtype=jnp.int32)[None, :]
    w_in = nrm(ks[3], (DEPTH, D_MODEL, D_IN), D_MODEL ** -0.5)
    w_out = nrm(ks[4], (DEPTH, D_MIX, D_MODEL), D_MIX ** -0.5)
    diff_lambda = nrm(ks[5], (DEPTH, 4, B_QK_DIM), 0.1)
    diff_subln = 1.0 + nrm(ks[6], (DEPTH, HEAD_DIM), 0.01)
    w_ada = nrm(ks[7], (DEPTH, D_MODEL, 6 * D_MODEL), 0.5 * D_MODEL ** -0.5)
    b_ada = nrm(ks[8], (DEPTH, 6 * D_MODEL), 0.01)
    g_attn = 1.0 + nrm(ks[9], (DEPTH, D_MODEL), 0.01)
    g_ffn = 1.0 + nrm(ks[10], (DEPTH, D_MODEL), 0.01)
    w_ff_gate = nrm(ks[11], (n_dense, D_MODEL, D_FF), D_MODEL ** -0.5)
    w_ff_up = nrm(ks[12], (n_dense, D_MODEL, D_FF), D_MODEL ** -0.5)
    w_ff_down = nrm(ks[13], (n_dense, D_FF, D_MODEL), D_FF ** -0.5)
    w_router = nrm(ks[14], (n_moe, D_MODEL, N_EXPERTS), D_MODEL ** -0.5)
    w_exp_gate = nrm(ks[15], (n_moe, N_EXPERTS, D_MODEL, D_FF_EXPERT), D_MODEL ** -0.5)
    w_exp_up = nrm(ks[16], (n_moe, N_EXPERTS, D_MODEL, D_FF_EXPERT), D_MODEL ** -0.5)
    w_exp_down = nrm(ks[17], (n_moe, N_EXPERTS, D_FF_EXPERT, D_MODEL), D_FF_EXPERT ** -0.5)
    g_final = 1.0 + nrm(ks[18], (D_MODEL,), 0.01)
    return {'x': x, 'c': c, 'positions': positions, 'w_in': w_in, 'w_out': w_out,
            'diff_lambda': diff_lambda, 'diff_subln': diff_subln, 'w_ada': w_ada, 'b_ada': b_ada,
            'g_attn': g_attn, 'g_ffn': g_ffn, 'w_ff_gate': w_ff_gate, 'w_ff_up': w_ff_up,
            'w_ff_down': w_ff_down, 'w_router': w_router, 'w_exp_gate': w_exp_gate,
            'w_exp_up': w_exp_up, 'w_exp_down': w_exp_down, 'g_final': g_final}


def reference(x, c, positions, w_in, w_out, diff_lambda, diff_subln, w_ada, b_ada,
              g_attn, g_ffn, w_ff_gate, w_ff_up, w_ff_down, w_router, w_exp_gate,
              w_exp_up, w_exp_down, g_final):
    cos64, sin64 = rope_tables(positions, HEAD_DIM)
    cos32, sin32 = rope_tables(positions, B_QK_DIM)
    c_act = jax.nn.silu(c)
    for layer in range(DEPTH):
        mod = jnp.einsum('bd,dn->bn', c_act, w_ada[layer]) + b_ada[layer]
        sh1, sc1, gt1, sh2, sc2, gt2 = [m[:, None, :] for m in jnp.split(mod, 6, axis=-1)]
        lam_init = 0.8 - 0.6 * math.exp(-0.3 * layer)
        h = rms_norm(x, g_attn[layer]) * (1.0 + sc1) + sh1
        y = hybrid_mixer(h, cos64, sin64, cos32, sin32, w_in[layer], w_out[layer],
                         diff_lambda[layer], diff_subln[layer], lam_init)
        x = x + gt1 * y
        h = rms_norm(x, g_ffn[layer]) * (1.0 + sc2) + sh2
        j = layer // 2
        if layer % 2 == 0:
            f = swiglu(h, w_ff_gate[j], w_ff_up[j], w_ff_down[j])
        else:
            f = moe_swiglu(h, w_router[j], w_exp_gate[j], w_exp_up[j], w_exp_down[j])
        x = x + gt2 * f
    return rms_norm(x, g_final)
```

```python
import functools
import math

import jax
import jax.numpy as jnp
import numpy as np
from jax import lax
from jax.experimental import pallas as pl
from jax.experimental.pallas import tpu as pltpu

F32 = jnp.float32
BF16 = jnp.bfloat16

HEAD_DIM = 64
A_HEADS = 6
IDX_HEADS = 4
B_HEADS = 4
B_QK_DIM = 32
C_HEADS = 6
DSA_TOPK_MAX = 256
MOBA_BLOCK = 256
MOBA_TOPK = 3
ROPE_THETA = 500000.0
ROPE_FRACTION = 4
SUBLN_EPS = 1e-5
EPS = 1e-6
N_EXPERTS = 8

LANES = 128
NEG = -1e30
INT_MIN = -2 ** 31
VMEM_LIMIT = 48 * 1024 * 1024

_G_QA, _G_KKA, _G_VVA, _G_QI, _G_KKI, _G_WI = (0, 384), (384, 512), (512, 640), (640, 896), (896, 1024), (1024, 1152)
_G_QB, _G_KB, _G_VB = (1152, 1408), (1408, 1664), (1664, 1920)
_G_QC, _G_KC, _G_VC = (1920, 2304), (2304, 2688), (2688, 3072)
D_IN_PAD = 3072


def _params(*sem):
    return pltpu.CompilerParams(dimension_semantics=sem, vmem_limit_bytes=VMEM_LIMIT)


def _dot(a, b):
    return jnp.dot(a, b, preferred_element_type=F32)


def _dot_nt(a, b, precision=None):
    return lax.dot_general(a, b, (((1,), (1,)), ((), ())), preferred_element_type=F32, precision=precision)


def _tile_lanes(v, width):
    reps = width // LANES
    return v if reps == 1 else jnp.concatenate([v] * reps, axis=1)


def _adaln_kernel(c_ref, w_ref, b_ref, o_ref):
    c = c_ref[...]
    c_act = c * (1.0 / (1.0 + jnp.exp(-c)))
    o_ref[...] = jnp.dot(c_act, w_ref[...], preferred_element_type=F32,
                         precision=lax.Precision.HIGHEST) + b_ref[...]


def adaln_mod(c, w_ada, b_ada, tn=1536):
    depth, d, n = w_ada.shape
    b = c.shape[0]
    return pl.pallas_call(
        _adaln_kernel,
        grid=(depth, n // tn),
        in_specs=[pl.BlockSpec((b, d), lambda l, j: (0, 0)),
                  pl.BlockSpec((None, d, tn), lambda l, j: (l, 0, j)),
                  pl.BlockSpec((None, 1, tn), lambda l, j: (l, 0, j))],
        out_specs=pl.BlockSpec((None, b, tn), lambda l, j: (l, 0, j)),
        out_shape=jax.ShapeDtypeStruct((depth, b, n), F32),
        compiler_params=_params("parallel", "parallel"),
        name="adaln_mod",
    )(c, w_ada, b_ada.reshape(depth, 1, n))


def _norm_mod(x, g, shift, scale, eps=EPS):
    y = x * lax.rsqrt(jnp.mean(x * x, axis=-1, keepdims=True) + eps)
    return (y * g) * (1.0 + scale) + shift


def _rope_store(acc, o_ref, cos, sa, sb, half):
    for j in range(acc.shape[1] // LANES):
        a = acc[:, j * LANES:(j + 1) * LANES]
        r = a * cos + pltpu.roll(a, half, 1) * sb + pltpu.roll(a, LANES - half, 1) * sa
        o_ref[:, j * LANES:(j + 1) * LANES] = r.astype(o_ref.dtype)


def _inproj_kernel(x_ref, g_ref, mod_ref, c64_ref, sa64_ref, sb64_ref, c32_ref, sa32_ref, sb32_ref, w_ref,
                   qa_ref, kka_ref, vva_ref, qi_ref, kki_ref, wi_ref,
                   qb_ref, kb_ref, vb_ref, qc_ref, kc_ref, vc_ref):
    h = _norm_mod(x_ref[...], g_ref[...], mod_ref[0:1, :], mod_ref[1:2, :]).astype(BF16)
    c64, sa64, sb64 = c64_ref[...], sa64_ref[...], sb64_ref[...]
    c32, sa32, sb32 = c32_ref[...], sa32_ref[...], sb32_ref[...]

    def proj(cols):
        return _dot(h, w_ref[:, cols[0]:cols[1]])

    qk_scale = HEAD_DIM ** -0.5
    _rope_store(proj(_G_QA), qa_ref, c64 * qk_scale, sa64 * qk_scale, sb64 * qk_scale, 8)
    _rope_store(proj(_G_KKA), kka_ref, c64, sa64, sb64, 8)
    vva_ref[...] = proj(_G_VVA).astype(vva_ref.dtype)
    _rope_store(proj(_G_QI), qi_ref, c64, sa64, sb64, 8)
    _rope_store(proj(_G_KKI), kki_ref, c64, sa64, sb64, 8)
    wi_ref[...] = proj(_G_WI) * (IDX_HEADS ** -0.5 * HEAD_DIM ** -0.5)
    b_scale = B_QK_DIM ** -0.5
    _rope_store(proj(_G_QB), qb_ref, c32 * b_scale, sa32 * b_scale, sb32 * b_scale, 4)
    _rope_store(proj(_G_KB), kb_ref, c32, sa32, sb32, 4)
    vb_ref[...] = proj(_G_VB).astype(vb_ref.dtype)
    _rope_store(proj(_G_QC), qc_ref, c64 * qk_scale, sa64 * qk_scale, sb64 * qk_scale, 8)
    _rope_store(proj(_G_KC), kc_ref, c64, sa64, sb64, 8)
    vc_ref[...] = proj(_G_VC).astype(vc_ref.dtype)


def inproj(x, g, mod, tabs, w_pad, seq, tm=512):
    t, d = x.shape
    tiles_per_batch = seq // tm
    row = lambda i: (i, 0)
    widths = [384, 128, 128, 256, 128, 128, 256, 256, 256, 384, 384, 384]
    dtypes = [BF16, BF16, BF16, BF16, BF16, F32, BF16, BF16, BF16, BF16, BF16, BF16]
    return pl.pallas_call(
        _inproj_kernel,
        grid=(t // tm,),
        in_specs=[pl.BlockSpec((tm, d), row),
                  pl.BlockSpec((1, d), lambda i: (0, 0)),
                  pl.BlockSpec((None, 6, d), lambda i: (i // tiles_per_batch, 0, 0))]
                 + [pl.BlockSpec((tm, LANES), row)] * 6
                 + [pl.BlockSpec((d, D_IN_PAD), lambda i: (0, 0))],
        out_specs=[pl.BlockSpec((tm, w), row) for w in widths],
        out_shape=[jax.ShapeDtypeStruct((t, w), dt) for w, dt in zip(widths, dtypes)],
        compiler_params=_params("parallel"),
        name="inproj",
    )(x, g, mod, *tabs, w_pad)


def _softmax_step(s, v, m_ref, l_ref, acc_ref, sl):
    m_prev = m_ref[sl]
    m_new = jnp.maximum(m_prev, jnp.max(s, axis=1, keepdims=True))
    alpha = jnp.exp(m_prev - m_new)
    p = jnp.exp(s - _tile_lanes(m_new, s.shape[1]))
    l_ref[sl] = alpha * l_ref[sl] + jnp.sum(p, axis=1, keepdims=True)
    acc_ref[sl] = alpha * acc_ref[sl] + _dot(p.astype(BF16), v)
    m_ref[sl] = m_new


def _init_stats(m_ref, l_ref, acc_ref):
    m_ref[...] = jnp.full(m_ref.shape, -jnp.inf, F32)
    l_ref[...] = jnp.zeros(l_ref.shape, F32)
    acc_ref[...] = jnp.zeros(acc_ref.shape, F32)


def _stacked_causal(rows, tq):
    assert tq & (tq - 1) == 0
    row = lax.broadcasted_iota(jnp.int32, (rows, tq), 0) & (tq - 1)
    col = lax.broadcasted_iota(jnp.int32, (rows, tq), 1)
    return col <= row


def _key_to_f32(k):
    return lax.bitcast_convert_type(jnp.where(k >= 0, k, k ^ 0x7FFFFFFF), F32)


def _dsa_kernel(qa_ref, qi_ref, wiq_ref, kka_ref, vva_ref, kki_ref, o_ref,
                sc_ref, qs_ref, m_ref, l_ref, acc_ref, *, tq, k_top):
    i = pl.program_id(1)
    nk = i + 1
    t0 = i * tq
    row = lax.broadcasted_iota(jnp.int32, (tq, tq), 0)
    col = lax.broadcasted_iota(jnp.int32, (tq, tq), 1)
    lo = lax.broadcasted_iota(jnp.int32, (tq, LANES), 1) < HEAD_DIM
    zero_b = jnp.zeros((tq, LANES), BF16)

    def stack_heads(q):
        out = []
        for g in range(q.shape[1] // LANES):
            qg = q[:, g * LANES:(g + 1) * LANES]
            out += [jnp.where(lo, qg, zero_b), jnp.where(lo, zero_b, qg)]
        return out

    qa_stack = stack_heads(qa_ref[...])
    for g in range(3):
        qs_ref[g, 0:tq, :] = qa_stack[2 * g]
        qs_ref[g, tq:2 * tq, :] = qa_stack[2 * g + 1]
    qi_stack = jnp.concatenate(stack_heads(qi_ref[...]), axis=0)
    wi = wiq_ref[...]
    w_cols = [wi[:, h:h + 1] for h in range(IDX_HEADS)]

    def chunk(c):
        return pl.ds(pl.multiple_of(c * tq, tq), tq)

    def idx_body(c, carry):
        r = jnp.maximum(_dot_nt(qi_stack, kki_ref[chunk(c), :]), 0.0)
        s = w_cols[0] * r[0:tq]
        for h in range(1, IDX_HEADS):
            s = s + w_cols[h] * r[h * tq:(h + 1) * tq]
        causal = (c * tq + col) <= (t0 + row)
        sc_ref[:, chunk(c)] = jnp.where(causal, s, -jnp.inf)
        return carry

    lax.fori_loop(0, nk, idx_body, 0)

    def count_ge(cand):
        def body(c, acc):
            ge = jnp.where(sc_ref[:, chunk(c)] >= cand, 1.0, 0.0)
            for j in range(tq // LANES):
                acc = acc + ge[:, j * LANES:(j + 1) * LANES]
            return acc
        acc = lax.fori_loop(0, nk, body, jnp.zeros((tq, LANES), F32))
        return jnp.sum(acc, axis=1, keepdims=True)

    kf = float(k_top)
    key0 = jnp.where(count_ge(jnp.zeros((tq, 1), F32)) >= kf, 0, INT_MIN).astype(jnp.int32)

    def bisect(b, key):
        trial = key | lax.shift_left(jnp.int32(1), 30 - b)
        return jnp.where(count_ge(_key_to_f32(trial)) >= kf, trial, key)

    key = lax.fori_loop(0, 31, bisect, key0)
    thr = _key_to_f32(key)
    thr_next = _key_to_f32(key + 1)
    need = kf - count_ge(thr_next)
    all_sel = (t0 + lax.broadcasted_iota(jnp.int32, (tq, 1), 0)) < k_top
    upper = (row <= col).astype(BF16)

    _init_stats(m_ref, l_ref, acc_ref)

    def att_body(c, tie_run):
        s_idx = sc_ref[:, chunk(c)]
        ge = s_idx >= thr
        gt = s_idx >= thr_next
        tie = jnp.where(gt, 0.0, jnp.where(ge, 1.0, 0.0))
        prefix = _dot(tie.astype(BF16), upper) + tie_run
        take = jnp.where(gt, 1.0, jnp.where(prefix <= need, tie, 0.0))
        take = jnp.where(all_sel, 1.0, take)
        take = jnp.where(s_idx > -jnp.inf, take, 0.0)
        sel2 = jnp.concatenate([take, take], axis=0) > 0.5
        kk = kka_ref[chunk(c), :]
        vv = vva_ref[chunk(c), :]
        for g in range(3):
            s = jnp.where(sel2, _dot_nt(qs_ref[g], kk), NEG)
            _softmax_step(s, vv, m_ref, l_ref, acc_ref, g)
        return tie_run + jnp.sum(tie, axis=1, keepdims=True)

    lax.fori_loop(0, nk, att_body, jnp.zeros((tq, 1), F32))

    for g in range(3):
        o = acc_ref[g] / l_ref[g]
        o_ref[:, g * LANES:(g + 1) * LANES] = jnp.where(lo, o[0:tq], o[tq:2 * tq]).astype(o_ref.dtype)


def dsa_attention(qa, qi, wi, kka, vva, kki, batch, seq, tq=256):
    t = qa.shape[0]
    nq = seq // tq
    k_top = min(DSA_TOPK_MAX, seq // 4)
    qrow = lambda b, i: (b * nq + i, 0)
    full = lambda b, i: (b, 0)
    return pl.pallas_call(
        functools.partial(_dsa_kernel, tq=tq, k_top=k_top),
        grid=(batch, nq),
        in_specs=[pl.BlockSpec((tq, 384), qrow), pl.BlockSpec((tq, 256), qrow), pl.BlockSpec((tq, LANES), qrow),
                  pl.BlockSpec((seq, LANES), full), pl.BlockSpec((seq, LANES), full),
                  pl.BlockSpec((seq, LANES), full)],
        out_specs=pl.BlockSpec((tq, 384), qrow),
        out_shape=jax.ShapeDtypeStruct((t, 384), BF16),
        scratch_shapes=[pltpu.VMEM((tq, seq), F32), pltpu.VMEM((3, 2 * tq, LANES), BF16),
                        pltpu.VMEM((3, 2 * tq, LANES), F32), pltpu.VMEM((3, 2 * tq, LANES), F32),
                        pltpu.VMEM((3, 2 * tq, LANES), F32)],
        compiler_params=_params("parallel", "parallel"),
        name="dsa_attention",
    )(qa, qi, wi, kka, vva, kki)


def _diff_kernel(qb_ref, kb_ref, vb_ref, lam_ref, gsub_ref, o_ref,
                 qs_ref, m_ref, l_ref, acc_ref, *, tq, lam_init):
    i = pl.program_id(1)
    lane = lax.broadcasted_iota(jnp.int32, (tq, LANES), 1)
    lo = lane < HEAD_DIM
    zero_b = jnp.zeros((tq, LANES), BF16)
    qb = qb_ref[...]
    for g in range(2):
        qg = qb[:, g * LANES:(g + 1) * LANES]
        for j in range(4):
            qs_ref[g, j * tq:(j + 1) * tq, :] = jnp.where(lane // B_QK_DIM == j, qg, zero_b)
    _init_stats(m_ref, l_ref, acc_ref)

    def chunk(c):
        return pl.ds(pl.multiple_of(c * tq, tq), tq)

    def step(c, masked):
        for g in range(2):
            s = _dot_nt(qs_ref[g], kb_ref[chunk(c), g * LANES:(g + 1) * LANES])
            if masked:
                s = jnp.where(_stacked_causal(4 * tq, tq), s, NEG)
            _softmax_step(s, vb_ref[chunk(c), g * LANES:(g + 1) * LANES], m_ref, l_ref, acc_ref, g)

    def body(c, carry):
        step(c, False)
        return carry

    lax.fori_loop(0, i, body, 0)
    step(i, True)

    lv = lam_ref[...]
    lam = (jnp.exp(jnp.sum(lv[0:1] * lv[1:2], axis=1, keepdims=True))
           - jnp.exp(jnp.sum(lv[2:3] * lv[3:4], axis=1, keepdims=True)) + lam_init)
    gsub = gsub_ref[...]
    for g in range(2):
        p = acc_ref[g] / l_ref[g]
        o = jnp.where(lo, p[0:tq] - lam * p[tq:2 * tq], p[2 * tq:3 * tq] - lam * p[3 * tq:4 * tq])
        o2 = o * o
        ms_lo = jnp.sum(jnp.where(lo, o2, 0.0), axis=1, keepdims=True)
        ms_hi = jnp.sum(jnp.where(lo, 0.0, o2), axis=1, keepdims=True)
        ms = jnp.where(lo, ms_lo, ms_hi) * (1.0 / HEAD_DIM)
        y = (o * lax.rsqrt(ms + SUBLN_EPS) * gsub) * (1.0 - lam_init)
        o_ref[:, g * LANES:(g + 1) * LANES] = y.astype(o_ref.dtype)


def diff_attention(qb, kb, vb, lam_vec, g_sub2, lam_init, batch, seq, tq=256):
    t = qb.shape[0]
    nq = seq // tq
    qrow = lambda b, i: (b * nq + i, 0)
    full = lambda b, i: (b, 0)
    const = lambda b, i: (0, 0)
    return pl.pallas_call(
        functools.partial(_diff_kernel, tq=tq, lam_init=lam_init),
        grid=(batch, nq),
        in_specs=[pl.BlockSpec((tq, 256), qrow), pl.BlockSpec((seq, 256), full), pl.BlockSpec((seq, 256), full),
                  pl.BlockSpec((4, B_QK_DIM), const), pl.BlockSpec((1, LANES), const)],
        out_specs=pl.BlockSpec((tq, 256), qrow),
        out_shape=jax.ShapeDtypeStruct((t, 256), BF16),
        scratch_shapes=[pltpu.VMEM((2, 4 * tq, LANES), BF16), pltpu.VMEM((2, 4 * tq, LANES), F32),
                        pltpu.VMEM((2, 4 * tq, LANES), F32), pltpu.VMEM((2, 4 * tq, LANES), F32)],
        compiler_params=_params("parallel", "parallel"),
        name="diff_attention",
    )(qb, kb, vb, lam_vec, g_sub2)


def _moba_kernel(qc_ref, kc_ref, vc_ref, o_ref, kmean_ref, qs_ref, m_ref, l_ref, acc_ref, *, tq, nb, n_sel):
    i = pl.program_id(1)
    nbp = kmean_ref.shape[0]
    lane = lax.broadcasted_iota(jnp.int32, (tq, LANES), 1)
    lo = lane < HEAD_DIM
    zero_b = jnp.zeros((tq, LANES), BF16)

    @pl.when(i == 0)
    def _():
        kmean_ref[...] = jnp.zeros(kmean_ref.shape, F32)
        for n in range(nb):
            kblk = kc_ref[n * tq:(n + 1) * tq, :].astype(F32)
            kmean_ref[n:n + 1, :] = jnp.mean(kblk, axis=0, keepdims=True)

    qc = qc_ref[...]
    sub = lax.broadcasted_iota(jnp.int32, (nbp, 2 * tq), 0)
    for g in range(3):
        qg = qc[:, g * LANES:(g + 1) * LANES]
        q2 = jnp.concatenate([jnp.where(lo, qg, zero_b), jnp.where(lo, zero_b, qg)], axis=0)
        gt = _dot_nt(kmean_ref[:, g * LANES:(g + 1) * LANES], q2.astype(F32), precision=lax.Precision.HIGHEST)
        past = sub < i
        rows = []
        for n in range(nbp):
            gn = gt[n:n + 1, :]
            beats = jnp.where(sub < n, jnp.where(gt >= gn, 1.0, 0.0), jnp.where(gt > gn, 1.0, 0.0))
            beats = jnp.where(sub == n, 0.0, jnp.where(past, beats, 0.0))
            rank = jnp.sum(beats, axis=0, keepdims=True)
            rows.append(jnp.where(rank < n_sel, 0.0, NEG))
        bias8 = jnp.where(past, jnp.concatenate(rows, axis=0), NEG)
        bias_t = jnp.concatenate([bias8, jnp.zeros((LANES - nbp, 2 * tq), F32)], axis=0)
        qs_ref[g, :, 0:LANES] = q2
        qs_ref[g, :, LANES:2 * LANES] = bias_t.T.astype(BF16)
    _init_stats(m_ref, l_ref, acc_ref)

    def chunk(c):
        return pl.ds(pl.multiple_of(c * tq, tq), tq)

    def body(c, carry):
        hot = jnp.where(lane == c, 1.0, 0.0).astype(BF16)
        for g in range(3):
            k_aug = jnp.concatenate([kc_ref[chunk(c), g * LANES:(g + 1) * LANES], hot], axis=1)
            s = _dot_nt(qs_ref[g], k_aug)
            _softmax_step(s, vc_ref[chunk(c), g * LANES:(g + 1) * LANES], m_ref, l_ref, acc_ref, g)
        return carry

    lax.fori_loop(0, i, body, 0)

    causal2 = _stacked_causal(2 * tq, tq)
    for g in range(3):
        s = _dot_nt(qs_ref[g, :, 0:LANES], kc_ref[chunk(i), g * LANES:(g + 1) * LANES])
        s = jnp.where(causal2, s, NEG)
        _softmax_step(s, vc_ref[chunk(i), g * LANES:(g + 1) * LANES], m_ref, l_ref, acc_ref, g)
        o = acc_ref[g] / l_ref[g]
        o_ref[:, g * LANES:(g + 1) * LANES] = jnp.where(lo, o[0:tq], o[tq:2 * tq]).astype(o_ref.dtype)


def moba_attention(qc, kc, vc, batch, seq):
    tq = MOBA_BLOCK
    t = qc.shape[0]
    nb = seq // tq
    n_sel = min(MOBA_TOPK, nb - 1)
    nbp = 8
    assert seq % tq == 0 and nb <= nbp
    qrow = lambda b, i: (b * nb + i, 0)
    full = lambda b, i: (b, 0)
    return pl.pallas_call(
        functools.partial(_moba_kernel, tq=tq, nb=nb, n_sel=n_sel),
        grid=(batch, nb),
        in_specs=[pl.BlockSpec((tq, 384), qrow), pl.BlockSpec((seq, 384), full), pl.BlockSpec((seq, 384), full)],
        out_specs=pl.BlockSpec((tq, 384), qrow),
        out_shape=jax.ShapeDtypeStruct((t, 384), BF16),
        scratch_shapes=[pltpu.VMEM((nbp, 384), F32), pltpu.VMEM((3, 2 * tq, 2 * LANES), BF16),
                        pltpu.VMEM((3, 2 * tq, LANES), F32), pltpu.VMEM((3, 2 * tq, LANES), F32),
                        pltpu.VMEM((3, 2 * tq, LANES), F32)],
        compiler_params=_params("parallel", "arbitrary"),
        name="moba_attention",
    )(qc, kc, vc)


def _route(h, wr_ref, cw_ref):
    logits = jnp.dot(h, wr_ref[...], preferred_element_type=F32, precision=lax.Precision.HIGHEST)
    lane = lax.broadcasted_iota(jnp.int32, logits.shape, 1)
    lg = jnp.where(lane < N_EXPERTS, logits, -jnp.inf)
    v0 = jnp.max(lg, axis=1, keepdims=True)
    i0 = jnp.min(jnp.where(lg == v0, lane, LANES), axis=1, keepdims=True)
    lg1 = jnp.where(lane == i0, -jnp.inf, lg)
    v1 = jnp.max(lg1, axis=1, keepdims=True)
    i1 = jnp.min(jnp.where(lg1 == v1, lane, LANES), axis=1, keepdims=True)
    e1 = jnp.exp(v1 - v0)
    w0 = 1.0 / (1.0 + e1)
    cw_ref[...] = jnp.where(lane == i0, w0, 0.0) + jnp.where(lane == i1, e1 * w0, 0.0)


def _outproj_kernel(x_ref, ya_ref, yb_ref, yc_ref, wo_ref, g_ref, mod_ref, *rest, with_router):
    if with_router:
        wr_ref, x1_ref, h_ref, cw_ref = rest
    else:
        x1_ref, h_ref = rest
    y = (_dot(ya_ref[...], wo_ref[0:384, :]) + _dot(yb_ref[...], wo_ref[384:640, :])
         + _dot(yc_ref[...], wo_ref[640:1024, :]))
    x1 = x_ref[...] + mod_ref[2:3, :] * y
    x1_ref[...] = x1
    h = _norm_mod(x1, g_ref[...], mod_ref[3:4, :], mod_ref[4:5, :])
    h_ref[...] = h.astype(BF16)
    if with_router:
        _route(h, wr_ref, cw_ref)


def outproj(x, ya, yb, yc, wo, g_ffn, mod, seq, w_router=None, tm=512):
    t, d = x.shape
    tiles_per_batch = seq // tm
    row = lambda i: (i, 0)
    const = lambda i: (0, 0)
    with_router = w_router is not None
    in_specs = [pl.BlockSpec((tm, d), row), pl.BlockSpec((tm, 384), row), pl.BlockSpec((tm, 256), row),
                pl.BlockSpec((tm, 384), row), pl.BlockSpec((d, d), const), pl.BlockSpec((1, d), const),
                pl.BlockSpec((None, 6, d), lambda i: (i // tiles_per_batch, 0, 0))]
    out_specs = [pl.BlockSpec((tm, d), row), pl.BlockSpec((tm, d), row)]
    out_shape = [jax.ShapeDtypeStruct((t, d), F32), jax.ShapeDtypeStruct((t, d), BF16)]
    args = [x, ya, yb, yc, wo, g_ffn, mod]
    if with_router:
        in_specs.append(pl.BlockSpec((d, LANES), const))
        out_specs.append(pl.BlockSpec((tm, LANES), row))
        out_shape.append(jax.ShapeDtypeStruct((t, LANES), F32))
        args.append(w_router)
    return pl.pallas_call(
        functools.partial(_outproj_kernel, with_router=with_router),
        grid=(t // tm,), in_specs=in_specs, out_specs=out_specs, out_shape=out_shape,
        compiler_params=_params("parallel"),
        name="outproj_router" if with_router else "outproj",
    )(*args)


def _swiglu_partial(h, wg_ref, wu_ref, wd_ref):
    a = _dot(h, wg_ref[...])
    u = _dot(h, wu_ref[...])
    act = (a * (1.0 / (1.0 + jnp.exp(-a)))) * u
    return _dot(act.astype(BF16), wd_ref[...])


def _ffn_kernel(x1_ref, h_ref, mod_ref, wg_ref, wu_ref, wd_ref, o_ref, acc_ref):
    f = pl.program_id(1)

    @pl.when(f == 0)
    def _():
        acc_ref[...] = jnp.zeros(acc_ref.shape, F32)

    acc_ref[...] += _swiglu_partial(h_ref[...], wg_ref, wu_ref, wd_ref)

    @pl.when(f == pl.num_programs(1) - 1)
    def _():
        o_ref[...] = x1_ref[...] + mod_ref[5:6, :] * acc_ref[...]


def ffn_dense(x1, h, mod, wg, wu, wd, seq, tm=512, tf=1408):
    t, d = x1.shape
    dff = wg.shape[1]
    tiles_per_batch = seq // tm
    row = lambda i, f: (i, 0)
    return pl.pallas_call(
        _ffn_kernel,
        grid=(t // tm, dff // tf),
        in_specs=[pl.BlockSpec((tm, d), row), pl.BlockSpec((tm, d), row),
                  pl.BlockSpec((None, 6, d), lambda i, f: (i // tiles_per_batch, 0, 0)),
                  pl.BlockSpec((d, tf), lambda i, f: (0, f)), pl.BlockSpec((d, tf), lambda i, f: (0, f)),
                  pl.BlockSpec((tf, d), lambda i, f: (f, 0))],
        out_specs=pl.BlockSpec((tm, d), row),
        out_shape=jax.ShapeDtypeStruct((t, d), F32),
        scratch_shapes=[pltpu.VMEM((tm, d), F32)],
        compiler_params=_params("parallel", "arbitrary"),
        name="ffn_dense",
    )(x1, h, mod, wg, wu, wd)


def _moe_kernel(x1_ref, h_ref, cw_ref, mod_ref, wg_ref, wu_ref, wd_ref, o_ref, acc_ref):
    e = pl.program_id(1)
    f = pl.program_id(2)

    @pl.when((e == 0) & (f == 0))
    def _():
        acc_ref[...] = jnp.zeros(acc_ref.shape, F32)

    cw = cw_ref[...]
    lane = lax.broadcasted_iota(jnp.int32, cw.shape, 1)
    cw_e = jnp.sum(jnp.where(lane == e, cw, 0.0), axis=1, keepdims=True)
    acc_ref[...] += cw_e * _swiglu_partial(h_ref[...], wg_ref, wu_ref, wd_ref)

    @pl.when((e == pl.num_programs(1) - 1) & (f == pl.num_programs(2) - 1))
    def _():
        o_ref[...] = x1_ref[...] + mod_ref[5:6, :] * acc_ref[...]


def ffn_moe(x1, h, cw, mod, wg, wu, wd, seq, tm=512, tf=1408):
    t, d = x1.shape
    n_exp, _, dff = wg.shape
    tiles_per_batch = seq // tm
    row = lambda i, e, f: (i, 0)
    return pl.pallas_call(
        _moe_kernel,
        grid=(t // tm, n_exp, dff // tf),
        in_specs=[pl.BlockSpec((tm, d), row), pl.BlockSpec((tm, d), row), pl.BlockSpec((tm, LANES), row),
                  pl.BlockSpec((None, 6, d), lambda i, e, f: (i // tiles_per_batch, 0, 0)),
                  pl.BlockSpec((None, d, tf), lambda i, e, f: (e, 0, f)),
                  pl.BlockSpec((None, d, tf), lambda i, e, f: (e, 0, f)),
                  pl.BlockSpec((None, tf, d), lambda i, e, f: (e, f, 0))],
        out_specs=pl.BlockSpec((tm, d), row),
        out_shape=jax.ShapeDtypeStruct((t, d), F32),
        scratch_shapes=[pltpu.VMEM((tm, d), F32)],
        compiler_params=_params("parallel", "arbitrary", "arbitrary"),
        name="ffn_moe",
    )(x1, h, cw, mod, wg, wu, wd)


def _final_norm_kernel(x_ref, g_ref, o_ref):
    x = x_ref[...]
    o_ref[...] = (x * lax.rsqrt(jnp.mean(x * x, axis=-1, keepdims=True) + EPS)) * g_ref[...]


def final_norm(x, g, tm=512):
    t, d = x.shape
    return pl.pallas_call(
        _final_norm_kernel,
        grid=(t // tm,),
        in_specs=[pl.BlockSpec((tm, d), lambda i: (i, 0)), pl.BlockSpec((1, d), lambda i: (0, 0))],
        out_specs=pl.BlockSpec((tm, d), lambda i: (i, 0)),
        out_shape=jax.ShapeDtypeStruct((t, d), F32),
        compiler_params=_params("parallel"),
        name="final_norm",
    )(x, g)


def _rope_tables(positions, dim):
    rot = dim // ROPE_FRACTION
    half = rot // 2
    inv = 1.0 / (ROPE_THETA ** (np.arange(0, rot, 2, dtype=np.float32) / rot))
    ang = positions.reshape(-1).astype(F32)[:, None] * jnp.asarray(inv, F32)
    cos, sin = jnp.cos(ang), jnp.sin(ang)
    t = ang.shape[0]
    ones = jnp.ones((t, dim - rot), F32)
    zeros = lambda w: jnp.zeros((t, w), F32)
    reps = LANES // dim
    c = jnp.tile(jnp.concatenate([cos, cos, ones], axis=1), (1, reps))
    sa = jnp.tile(jnp.concatenate([-sin, zeros(dim - half)], axis=1), (1, reps))
    sb = jnp.tile(jnp.concatenate([zeros(half), sin, zeros(dim - rot)], axis=1), (1, reps))
    return c, sa, sb


def _relayout_w_in(w):
    d = w.shape[0]
    pts = np.cumsum([0, 384, 64, 64, 256, 64, 4, 256, 256, 256, 384, 384, 384])
    (q_a, k_a, v_a, q_i, k_i, w_i, q_b, k_b, v_b, q_c, k_c, v_c) = [w[:, pts[j]:pts[j + 1]] for j in range(12)]
    w_i_pad = jnp.concatenate([w_i, jnp.zeros((d, LANES - IDX_HEADS), w.dtype)], axis=1)
    return jnp.concatenate([q_a, k_a, k_a, v_a, v_a, q_i, k_i, k_i, w_i_pad,
                            q_b, k_b, v_b, q_c, k_c, v_c], axis=1).astype(BF16)


def kernel(x, c, positions, w_in, w_out, diff_lambda, diff_subln, w_ada, b_ada, g_attn, g_ffn, w_ff_gate,
           w_ff_up, w_ff_down, w_router, w_exp_gate, w_exp_up, w_exp_down, g_final):
    batch, seq, d = x.shape
    depth = w_in.shape[0]
    t = batch * seq
    tabs = _rope_tables(positions, HEAD_DIM) + _rope_tables(positions, B_QK_DIM)
    mod_all = adaln_mod(c, w_ada, b_ada).reshape(depth, batch, 6, d)
    xf = x.reshape(t, d)
    for layer in range(depth):
        mod = mod_all[layer]
        lam_init = 0.8 - 0.6 * math.exp(-0.3 * layer)
        (qa, kka, vva, qi, kki, wi, qb, kb, vb, qc, kc, vc) = inproj(
            xf, g_attn[layer].reshape(1, d), mod, tabs, _relayout_w_in(w_in[layer]), seq)
        ya = dsa_attention(qa, qi, wi, kka, vva, kki, batch, seq)
        g_sub2 = jnp.tile(diff_subln[layer], 2).reshape(1, LANES)
        yb = diff_attention(qb, kb, vb, diff_lambda[layer], g_sub2, lam_init, batch, seq)
        yc = moba_attention(qc, kc, vc, batch, seq)
        j = layer // 2
        wo = w_out[layer].astype(BF16)
        gf = g_ffn[layer].reshape(1, d)
        if layer % 2 == 0:
            x1, h = outproj(xf, ya, yb, yc, wo, gf, mod, seq)
            xf = ffn_dense(x1, h, mod, w_ff_gate[j].astype(BF16), w_ff_up[j].astype(BF16),
                           w_ff_down[j].astype(BF16), seq)
        else:
            wr = jnp.concatenate([w_router[j], jnp.zeros((d, LANES - N_EXPERTS), F32)], axis=1)
            x1, h, cw = outproj(xf, ya, yb, yc, wo, gf, mod, seq, w_router=wr)
            xf = ffn_moe(x1, h, cw, mod, w_exp_gate[j].astype(BF16), w_exp_up[j].astype(BF16),
                         w_exp_down[j].astype(BF16), seq)
    return final_norm(xf, g_final.reshape(1, d)).reshape(batch, seq, d)
```

```python
import functools
import math

import jax
import jax.numpy as jnp
import numpy as np
from jax import lax
from jax.experimental import pallas as pl
from jax.experimental.pallas import tpu as pltpu

F32 = jnp.float32
BF16 = jnp.bfloat16

HEAD_DIM = 64
A_HEADS = 6
IDX_HEADS = 4
B_HEADS = 4
B_QK_DIM = 32
C_HEADS = 6
DSA_TOPK_MAX = 256
MOBA_BLOCK = 256
MOBA_TOPK = 3
ROPE_THETA = 500000.0
ROPE_FRACTION = 4
SUBLN_EPS = 1e-5
EPS = 1e-6
N_EXPERTS = 8

LANES = 128
NEG = -1e30
INT_MIN = -2 ** 31
VMEM_LIMIT = 48 * 1024 * 1024

_G_QA, _G_KKA, _G_VVA, _G_QI, _G_KKI, _G_WI = (0, 384), (384, 512), (512, 640), (640, 896), (896, 1024), (1024, 1152)
_G_QB, _G_KB, _G_VB = (1152, 1408), (1408, 1664), (1664, 1920)
_G_QC, _G_KC, _G_VC = (1920, 2304), (2304, 2688), (2688, 3072)
D_IN_PAD = 3072


def _params(*sem):
    return pltpu.CompilerParams(dimension_semantics=sem, vmem_limit_bytes=VMEM_LIMIT)


def _dot(a, b):
    return jnp.dot(a, b, preferred_element_type=F32)


def _dot_nt(a, b, precision=None):
    return lax.dot_general(a, b, (((1,), (1,)), ((), ())), preferred_element_type=F32, precision=precision)


def _tile_lanes(v, width):
    reps = width // LANES
    return v if reps == 1 else jnp.concatenate([v] * reps, axis=1)


def _adaln_kernel(c_ref, w_ref, b_ref, o_ref):
    c = c_ref[...]
    c_act = c * (1.0 / (1.0 + jnp.exp(-c)))
    o_ref[...] = jnp.dot(c_act, w_ref[...], preferred_element_type=F32,
                         precision=lax.Precision.HIGHEST) + b_ref[...]


def adaln_mod(c, w_ada, b_ada, tn=1536):
    depth, d, n = w_ada.shape
    b = c.shape[0]
    return pl.pallas_call(
        _adaln_kernel,
        grid=(depth, n // tn),
        in_specs=[pl.BlockSpec((b, d), lambda l, j: (0, 0)),
                  pl.BlockSpec((None, d, tn), lambda l, j: (l, 0, j)),
                  pl.BlockSpec((None, 1, tn), lambda l, j: (l, 0, j))],
        out_specs=pl.BlockSpec((None, b, tn), lambda l, j: (l, 0, j)),
        out_shape=jax.ShapeDtypeStruct((depth, b, n), F32),
        compiler_params=_params("parallel", "parallel"),
        name="adaln_mod",
    )(c, w_ada, b_ada.reshape(depth, 1, n))


def _norm_mod(x, g, shift, scale, eps=EPS):
    y = x * lax.rsqrt(jnp.mean(x * x, axis=-1, keepdims=True) + eps)
    return (y * g) * (1.0 + scale) + shift


def _rope_store(acc, o_ref, cos, sa, sb, half):
    for j in range(acc.shape[1] // LANES):
        a = acc[:, j * LANES:(j + 1) * LANES]
        r = a * cos + pltpu.roll(a, half, 1) * sb + pltpu.roll(a, LANES - half, 1) * sa
        o_ref[:, j * LANES:(j + 1) * LANES] = r.astype(o_ref.dtype)


def _inproj_kernel(x_ref, g_ref, mod_ref, c64_ref, sa64_ref, sb64_ref, c32_ref, sa32_ref, sb32_ref, w_ref,
                   qa_ref, kka_ref, vva_ref, qi_ref, kki_ref, wi_ref,
                   qb_ref, kb_ref, vb_ref, qc_ref, kc_ref, vc_ref):
    h = _norm_mod(x_ref[...], g_ref[...], mod_ref[0:1, :], mod_ref[1:2, :]).astype(BF16)
    c64, sa64, sb64 = c64_ref[...], sa64_ref[...], sb64_ref[...]
    c32, sa32, sb32 = c32_ref[...], sa32_ref[...], sb32_ref[...]

    def proj(cols):
        return _dot(h, w_ref[:, cols[0]:cols[1]])

    qk_scale = HEAD_DIM ** -0.5
    _rope_store(proj(_G_QA), qa_ref, c64 * qk_scale, sa64 * qk_scale, sb64 * qk_scale, 8)
    _rope_store(proj(_G_KKA), kka_ref, c64, sa64, sb64, 8)
    vva_ref[...] = proj(_G_VVA).astype(vva_ref.dtype)
    _rope_store(proj(_G_QI), qi_ref, c64, sa64, sb64, 8)
    _rope_store(proj(_G_KKI), kki_ref, c64, sa64, sb64, 8)
    wi_ref[...] = proj(_G_WI) * (IDX_HEADS ** -0.5 * HEAD_DIM ** -0.5)
    b_scale = B_QK_DIM ** -0.5
    _rope_store(proj(_G_QB), qb_ref, c32 * b_scale, sa32 * b_scale, sb32 * b_scale, 4)
    _rope_store(proj(_G_KB), kb_ref, c32, sa32, sb32, 4)
    vb_ref[...] = proj(_G_VB).astype(vb_ref.dtype)
    _rope_store(proj(_G_QC), qc_ref, c64 * qk_scale, sa64 * qk_scale, sb64 * qk_scale, 8)
    _rope_store(proj(_G_KC), kc_ref, c64, sa64, sb64, 8)
    vc_ref[...] = proj(_G_VC).astype(vc_ref.dtype)


def inproj(x, g, mod, tabs, w_pad, seq, tm=512):
    t, d = x.shape
    tiles_per_batch = seq // tm
    row = lambda i: (i, 0)
    widths = [384, 128, 128, 256, 128, 128, 256, 256, 256, 384, 384, 384]
    dtypes = [BF16, BF16, BF16, BF16, BF16, F32, BF16, BF16, BF16, BF16, BF16, BF16]
    return pl.pallas_call(
        _inproj_kernel,
        grid=(t // tm,),
        in_specs=[pl.BlockSpec((tm, d), row),
                  pl.BlockSpec((1, d), lambda i: (0, 0)),
                  pl.BlockSpec((None, 6, d), lambda i: (i // tiles_per_batch, 0, 0))]
                 + [pl.BlockSpec((tm, LANES), row)] * 6
                 + [pl.BlockSpec((d, D_IN_PAD), lambda i: (0, 0))],
        out_specs=[pl.BlockSpec((tm, w), row) for w in widths],
        out_shape=[jax.ShapeDtypeStruct((t, w), dt) for w, dt in zip(widths, dtypes)],
        compiler_params=_params("parallel"),
        name="inproj",
    )(x, g, mod, *tabs, w_pad)


def _softmax_step(s, v, m_ref, l_ref, acc_ref, sl):
    m_prev = m_ref[sl]
    m_new = jnp.maximum(m_prev, jnp.max(s, axis=1, keepdims=True))
    alpha = jnp.exp(m_prev - m_new)
    p = jnp.exp(s - _tile_lanes(m_new, s.shape[1]))
    l_ref[sl] = alpha * l_ref[sl] + jnp.sum(p, axis=1, keepdims=True)
    acc_ref[sl] = alpha * acc_ref[sl] + _dot(p.astype(BF16), v)
    m_ref[sl] = m_new


def _init_stats(m_ref, l_ref, acc_ref):
    m_ref[...] = jnp.full(m_ref.shape, -jnp.inf, F32)
    l_ref[...] = jnp.zeros(l_ref.shape, F32)
    acc_ref[...] = jnp.zeros(acc_ref.shape, F32)


def _stacked_causal(rows, tq):
    assert tq & (tq - 1) == 0
    row = lax.broadcasted_iota(jnp.int32, (rows, tq), 0) & (tq - 1)
    col = lax.broadcasted_iota(jnp.int32, (rows, tq), 1)
    return col <= row


def _key_to_f32(k):
    return lax.bitcast_convert_type(jnp.where(k >= 0, k, k ^ 0x7FFFFFFF), F32)


def _dsa_kernel(qa_ref, qi_ref, wiq_ref, kka_ref, vva_ref, kki_ref, o_ref,
                sc_ref, qs_ref, m_ref, l_ref, acc_ref, *, tq, k_top):
    i = pl.program_id(1)
    nk = i + 1
    t0 = i * tq
    row = lax.broadcasted_iota(jnp.int32, (tq, tq), 0)
    col = lax.broadcasted_iota(jnp.int32, (tq, tq), 1)
    lo = lax.broadcasted_iota(jnp.int32, (tq, LANES), 1) < HEAD_DIM
    zero_b = jnp.zeros((tq, LANES), BF16)

    def stack_heads(q):
        out = []
        for g in range(q.shape[1] // LANES):
            qg = q[:, g * LANES:(g + 1) * LANES]
            out += [jnp.where(lo, qg, zero_b), jnp.where(lo, zero_b, qg)]
        return out

    qa_stack = stack_heads(qa_ref[...])
    for g in range(3):
        qs_ref[g, 0:tq, :] = qa_stack[2 * g]
        qs_ref[g, tq:2 * tq, :] = qa_stack[2 * g + 1]
    qi_stack = jnp.concatenate(stack_heads(qi_ref[...]), axis=0)
    wi = wiq_ref[...]
    w_cols = [wi[:, h:h + 1] for h in range(IDX_HEADS)]

    def chunk(c):
        return pl.ds(pl.multiple_of(c * tq, tq), tq)

    def idx_body(c, carry):
        r = jnp.maximum(_dot_nt(qi_stack, kki_ref[chunk(c), :]), 0.0)
        s = w_cols[0] * r[0:tq]
        for h in range(1, IDX_HEADS):
            s = s + w_cols[h] * r[h * tq:(h + 1) * tq]
        causal = (c * tq + col) <= (t0 + row)
        sc_ref[:, chunk(c)] = jnp.where(causal, s, -jnp.inf)
        return carry

    lax.fori_loop(0, nk, idx_body, 0)

    def count_ge(cand):
        def body(c, acc):
            ge = jnp.where(sc_ref[:, chunk(c)] >= cand, 1.0, 0.0)
            for j in range(tq // LANES):
                acc = acc + ge[:, j * LANES:(j + 1) * LANES]
            return acc
        acc = lax.fori_loop(0, nk, body, jnp.zeros((tq, LANES), F32))
        return jnp.sum(acc, axis=1, keepdims=True)

    kf = float(k_top)

    def search():
        key0 = jnp.where(count_ge(jnp.zeros((tq, 1), F32)) >= kf, 0, INT_MIN).astype(jnp.int32)

        def bisect(b, key):
            trial = key | lax.shift_left(jnp.int32(1), 30 - b)
            return jnp.where(count_ge(_key_to_f32(trial)) >= kf, trial, key)
        return lax.fori_loop(0, 31, bisect, key0)

    key = lax.cond(t0 + tq <= k_top, lambda: jnp.full((tq, 1), INT_MIN, jnp.int32), search)
    thr = _key_to_f32(key)
    thr_next = _key_to_f32(key + 1)
    need = kf - count_ge(thr_next)
    all_sel = (t0 + lax.broadcasted_iota(jnp.int32, (tq, 1), 0)) < k_top
    upper = (row <= col).astype(BF16)

    _init_stats(m_ref, l_ref, acc_ref)

    def att_body(c, tie_run):
        s_idx = sc_ref[:, chunk(c)]
        ge = s_idx >= thr
        gt = s_idx >= thr_next
        tie = jnp.where(gt, 0.0, jnp.where(ge, 1.0, 0.0))
        prefix = _dot(tie.astype(BF16), upper) + tie_run
        take = jnp.where(gt, 1.0, jnp.where(prefix <= need, tie, 0.0))
        take = jnp.where(all_sel, 1.0, take)
        take = jnp.where(s_idx > -jnp.inf, take, 0.0)
        sel2 = jnp.concatenate([take, take], axis=0) > 0.5
        kk = kka_ref[chunk(c), :]
        vv = vva_ref[chunk(c), :]
        for g in range(3):
            s = jnp.where(sel2, _dot_nt(qs_ref[g], kk), NEG)
            _softmax_step(s, vv, m_ref, l_ref, acc_ref, g)
        return tie_run + jnp.sum(tie, axis=1, keepdims=True)

    lax.fori_loop(0, nk, att_body, jnp.zeros((tq, 1), F32))

    for g in range(3):
        o = acc_ref[g] / l_ref[g]
        o_ref[:, g * LANES:(g + 1) * LANES] = jnp.where(lo, o[0:tq], o[tq:2 * tq]).astype(o_ref.dtype)


def dsa_attention(qa, qi, wi, kka, vva, kki, batch, seq, tq=256):
    t = qa.shape[0]
    nq = seq // tq
    k_top = min(DSA_TOPK_MAX, seq // 4)
    qrow = lambda b, i: (b * nq + i, 0)
    full = lambda b, i: (b, 0)
    return pl.pallas_call(
        functools.partial(_dsa_kernel, tq=tq, k_top=k_top),
        grid=(batch, nq),
        in_specs=[pl.BlockSpec((tq, 384), qrow), pl.BlockSpec((tq, 256), qrow), pl.BlockSpec((tq, LANES), qrow),
                  pl.BlockSpec((seq, LANES), full), pl.BlockSpec((seq, LANES), full),
                  pl.BlockSpec((seq, LANES), full)],
        out_specs=pl.BlockSpec((tq, 384), qrow),
        out_shape=jax.ShapeDtypeStruct((t, 384), BF16),
        scratch_shapes=[pltpu.VMEM((tq, seq), F32), pltpu.VMEM((3, 2 * tq, LANES), BF16),
                        pltpu.VMEM((3, 2 * tq, LANES), F32), pltpu.VMEM((3, 2 * tq, LANES), F32),
                        pltpu.VMEM((3, 2 * tq, LANES), F32)],
        compiler_params=_params("parallel", "parallel"),
        name="dsa_attention",
    )(qa, qi, wi, kka, vva, kki)


def _diff_kernel(qb_ref, kb_ref, vb_ref, lam_ref, gsub_ref, o_ref,
                 qs_ref, m_ref, l_ref, acc_ref, *, tq, lam_init):
    i = pl.program_id(1)
    lane = lax.broadcasted_iota(jnp.int32, (tq, LANES), 1)
    lo = lane < HEAD_DIM
    zero_b = jnp.zeros((tq, LANES), BF16)
    qb = qb_ref[...]
    for g in range(2):
        qg = qb[:, g * LANES:(g + 1) * LANES]
        for j in range(4):
            qs_ref[g, j * tq:(j + 1) * tq, :] = jnp.where(lane // B_QK_DIM == j, qg, zero_b)
    _init_stats(m_ref, l_ref, acc_ref)

    def chunk(c):
        return pl.ds(pl.multiple_of(c * tq, tq), tq)

    def step(c, masked):
        for g in range(2):
            s = _dot_nt(qs_ref[g], kb_ref[chunk(c), g * LANES:(g + 1) * LANES])
            if masked:
                s = jnp.where(_stacked_causal(4 * tq, tq), s, NEG)
            _softmax_step(s, vb_ref[chunk(c), g * LANES:(g + 1) * LANES], m_ref, l_ref, acc_ref, g)

    def body(c, carry):
        step(c, False)
        return carry

    lax.fori_loop(0, i, body, 0)
    step(i, True)

    lv = lam_ref[...]
    lam = (jnp.exp(jnp.sum(lv[0:1] * lv[1:2], axis=1, keepdims=True))
           - jnp.exp(jnp.sum(lv[2:3] * lv[3:4], axis=1, keepdims=True)) + lam_init)
    gsub = gsub_ref[...]
    for g in range(2):
        p = acc_ref[g] / l_ref[g]
        o = jnp.where(lo, p[0:tq] - lam * p[tq:2 * tq], p[2 * tq:3 * tq] - lam * p[3 * tq:4 * tq])
        o2 = o * o
        ms_lo = jnp.sum(jnp.where(lo, o2, 0.0), axis=1, keepdims=True)
        ms_hi = jnp.sum(jnp.where(lo, 0.0, o2), axis=1, keepdims=True)
        ms = jnp.where(lo, ms_lo, ms_hi) * (1.0 / HEAD_DIM)
        y = (o * lax.rsqrt(ms + SUBLN_EPS) * gsub) * (1.0 - lam_init)
        o_ref[:, g * LANES:(g + 1) * LANES] = y.astype(o_ref.dtype)


def diff_attention(qb, kb, vb, lam_vec, g_sub2, lam_init, batch, seq, tq=256):
    t = qb.shape[0]
    nq = seq // tq
    qrow = lambda b, i: (b * nq + i, 0)
    full = lambda b, i: (b, 0)
    const = lambda b, i: (0, 0)
    return pl.pallas_call(
        functools.partial(_diff_kernel, tq=tq, lam_init=lam_init),
        grid=(batch, nq),
        in_specs=[pl.BlockSpec((tq, 256), qrow), pl.BlockSpec((seq, 256), full), pl.BlockSpec((seq, 256), full),
                  pl.BlockSpec((4, B_QK_DIM), const), pl.BlockSpec((1, LANES), const)],
        out_specs=pl.BlockSpec((tq, 256), qrow),
        out_shape=jax.ShapeDtypeStruct((t, 256), BF16),
        scratch_shapes=[pltpu.VMEM((2, 4 * tq, LANES), BF16), pltpu.VMEM((2, 4 * tq, LANES), F32),
                        pltpu.VMEM((2, 4 * tq, LANES), F32), pltpu.VMEM((2, 4 * tq, LANES), F32)],
        compiler_params=_params("parallel", "parallel"),
        name="diff_attention",
    )(qb, kb, vb, lam_vec, g_sub2)


def _moba_kernel(qc_ref, kc_ref, vc_ref, o_ref, kmean_ref, qs_ref, m_ref, l_ref, acc_ref, *, tq, nb, n_sel):
    i = pl.program_id(1)
    nbp = kmean_ref.shape[0]
    lane = lax.broadcasted_iota(jnp.int32, (tq, LANES), 1)
    lo = lane < HEAD_DIM
    zero_b = jnp.zeros((tq, LANES), BF16)

    @pl.when(i == 0)
    def _():
        kmean_ref[...] = jnp.zeros(kmean_ref.shape, F32)
        for n in range(nb):
            kblk = kc_ref[n * tq:(n + 1) * tq, :].astype(F32)
            kmean_ref[n:n + 1, :] = jnp.mean(kblk, axis=0, keepdims=True)

    qc = qc_ref[...]
    sub = lax.broadcasted_iota(jnp.int32, (nbp, 2 * tq), 0)
    for g in range(3):
        qg = qc[:, g * LANES:(g + 1) * LANES]
        q2 = jnp.concatenate([jnp.where(lo, qg, zero_b), jnp.where(lo, zero_b, qg)], axis=0)
        gt = _dot_nt(kmean_ref[:, g * LANES:(g + 1) * LANES], q2.astype(F32), precision=lax.Precision.HIGHEST)
        past = sub < i
        rows = []
        for n in range(nbp):
            gn = gt[n:n + 1, :]
            beats = jnp.where(sub < n, jnp.where(gt >= gn, 1.0, 0.0), jnp.where(gt > gn, 1.0, 0.0))
            beats = jnp.where(sub == n, 0.0, jnp.where(past, beats, 0.0))
            rank = jnp.sum(beats, axis=0, keepdims=True)
            rows.append(jnp.where(rank < n_sel, 0.0, NEG))
        bias8 = jnp.where(past, jnp.concatenate(rows, axis=0), NEG)
        bias_t = jnp.concatenate([bias8, jnp.zeros((LANES - nbp, 2 * tq), F32)], axis=0)
        qs_ref[g, :, 0:LANES] = q2
        qs_ref[g, :, LANES:2 * LANES] = bias_t.T.astype(BF16)
    _init_stats(m_ref, l_ref, acc_ref)

    def chunk(c):
        return pl.ds(pl.multiple_of(c * tq, tq), tq)

    def body(c, carry):
        hot = jnp.where(lane == c, 1.0, 0.0).astype(BF16)
        for g in range(3):
            k_aug = jnp.concatenate([kc_ref[chunk(c), g * LANES:(g + 1) * LANES], hot], axis=1)
            s = _dot_nt(qs_ref[g], k_aug)
            _softmax_step(s, vc_ref[chunk(c), g * LANES:(g + 1) * LANES], m_ref, l_ref, acc_ref, g)
        return carry

    lax.fori_loop(0, i, body, 0)

    causal2 = _stacked_causal(2 * tq, tq)
    for g in range(3):
        s = _dot_nt(qs_ref[g, :, 0:LANES], kc_ref[chunk(i), g * LANES:(g + 1) * LANES])
        s = jnp.where(causal2, s, NEG)
        _softmax_step(s, vc_ref[chunk(i), g * LANES:(g + 1) * LANES], m_ref, l_ref, acc_ref, g)
        o = acc_ref[g] / l_ref[g]
        o_ref[:, g * LANES:(g + 1) * LANES] = jnp.where(lo, o[0:tq], o[tq:2 * tq]).astype(o_ref.dtype)


def moba_attention(qc, kc, vc, batch, seq):
    tq = MOBA_BLOCK
    t = qc.shape[0]
    nb = seq // tq
    n_sel = min(MOBA_TOPK, nb - 1)
    nbp = 8
    assert seq % tq == 0 and nb <= nbp
    qrow = lambda b, i: (b * nb + i, 0)
    full = lambda b, i: (b, 0)
    return pl.pallas_call(
        functools.partial(_moba_kernel, tq=tq, nb=nb, n_sel=n_sel),
        grid=(batch, nb),
        in_specs=[pl.BlockSpec((tq, 384), qrow), pl.BlockSpec((seq, 384), full), pl.BlockSpec((seq, 384), full)],
        out_specs=pl.BlockSpec((tq, 384), qrow),
        out_shape=jax.ShapeDtypeStruct((t, 384), BF16),
        scratch_shapes=[pltpu.VMEM((nbp, 384), F32), pltpu.VMEM((3, 2 * tq, 2 * LANES), BF16),
                        pltpu.VMEM((3, 2 * tq, LANES), F32), pltpu.VMEM((3, 2 * tq, LANES), F32),
                        pltpu.VMEM((3, 2 * tq, LANES), F32)],
        compiler_params=_params("parallel", "arbitrary"),
        name="moba_attention",
    )(qc, kc, vc)


def _route(h, wr_ref, cw_ref):
    logits = jnp.dot(h, wr_ref[...], preferred_element_type=F32, precision=lax.Precision.HIGHEST)
    lane = lax.broadcasted_iota(jnp.int32, logits.shape, 1)
    lg = jnp.where(lane < N_EXPERTS, logits, -jnp.inf)
    v0 = jnp.max(lg, axis=1, keepdims=True)
    i0 = jnp.min(jnp.where(lg == v0, lane, LANES), axis=1, keepdims=True)
    lg1 = jnp.where(lane == i0, -jnp.inf, lg)
    v1 = jnp.max(lg1, axis=1, keepdims=True)
    i1 = jnp.min(jnp.where(lg1 == v1, lane, LANES), axis=1, keepdims=True)
    e1 = jnp.exp(v1 - v0)
    w0 = 1.0 / (1.0 + e1)
    cw_ref[...] = (jnp.where(lane == 0, i0.astype(F32), 0.0) + jnp.where(lane == 1, i1.astype(F32), 0.0)
                   + jnp.where(lane == 2, w0, 0.0) + jnp.where(lane == 3, e1 * w0, 0.0))


def _outproj_kernel(x_ref, ya_ref, yb_ref, yc_ref, wo_ref, g_ref, mod_ref, *rest, with_router):
    if with_router:
        wr_ref, x1_ref, h_ref, cw_ref = rest
    else:
        x1_ref, h_ref = rest
    y = (_dot(ya_ref[...], wo_ref[0:384, :]) + _dot(yb_ref[...], wo_ref[384:640, :])
         + _dot(yc_ref[...], wo_ref[640:1024, :]))
    x1 = x_ref[...] + mod_ref[2:3, :] * y
    x1_ref[...] = x1
    h = _norm_mod(x1, g_ref[...], mod_ref[3:4, :], mod_ref[4:5, :])
    h_ref[...] = h.astype(BF16)
    if with_router:
        _route(h, wr_ref, cw_ref)


def outproj(x, ya, yb, yc, wo, g_ffn, mod, seq, w_router=None, tm=512):
    t, d = x.shape
    tiles_per_batch = seq // tm
    row = lambda i: (i, 0)
    const = lambda i: (0, 0)
    with_router = w_router is not None
    in_specs = [pl.BlockSpec((tm, d), row), pl.BlockSpec((tm, 384), row), pl.BlockSpec((tm, 256), row),
                pl.BlockSpec((tm, 384), row), pl.BlockSpec((d, d), const), pl.BlockSpec((1, d), const),
                pl.BlockSpec((None, 6, d), lambda i: (i // tiles_per_batch, 0, 0))]
    out_specs = [pl.BlockSpec((tm, d), row), pl.BlockSpec((tm, d), row)]
    out_shape = [jax.ShapeDtypeStruct((t, d), F32), jax.ShapeDtypeStruct((t, d), BF16)]
    args = [x, ya, yb, yc, wo, g_ffn, mod]
    if with_router:
        in_specs.append(pl.BlockSpec((d, LANES), const))
        out_specs.append(pl.BlockSpec((tm, LANES), row))
        out_shape.append(jax.ShapeDtypeStruct((t, LANES), F32))
        args.append(w_router)
    return pl.pallas_call(
        functools.partial(_outproj_kernel, with_router=with_router),
        grid=(t // tm,), in_specs=in_specs, out_specs=out_specs, out_shape=out_shape,
        compiler_params=_params("parallel"),
        name="outproj_router" if with_router else "outproj",
    )(*args)


def _swiglu_partial(h, wg_ref, wu_ref, wd_ref):
    a = _dot(h, wg_ref[...])
    u = _dot(h, wu_ref[...])
    act = (a * (1.0 / (1.0 + jnp.exp(-a)))) * u
    return _dot(act.astype(BF16), wd_ref[...])


def _ffn_kernel(x1_ref, h_ref, mod_ref, wg_ref, wu_ref, wd_ref, o_ref, acc_ref):
    f = pl.program_id(1)

    @pl.when(f == 0)
    def _():
        acc_ref[...] = jnp.zeros(acc_ref.shape, F32)

    acc_ref[...] += _swiglu_partial(h_ref[...], wg_ref, wu_ref, wd_ref)

    @pl.when(f == pl.num_programs(1) - 1)
    def _():
        o_ref[...] = x1_ref[...] + mod_ref[5:6, :] * acc_ref[...]


def ffn_dense(x1, h, mod, wg, wu, wd, seq, tm=512, tf=1408):
    t, d = x1.shape
    dff = wg.shape[1]
    tiles_per_batch = seq // tm
    row = lambda i, f: (i, 0)
    return pl.pallas_call(
        _ffn_kernel,
        grid=(t // tm, dff // tf),
        in_specs=[pl.BlockSpec((tm, d), row), pl.BlockSpec((tm, d), row),
                  pl.BlockSpec((None, 6, d), lambda i, f: (i // tiles_per_batch, 0, 0)),
                  pl.BlockSpec((d, tf), lambda i, f: (0, f)), pl.BlockSpec((d, tf), lambda i, f: (0, f)),
                  pl.BlockSpec((tf, d), lambda i, f: (f, 0))],
        out_specs=pl.BlockSpec((tm, d), row),
        out_shape=jax.ShapeDtypeStruct((t, d), F32),
        scratch_shapes=[pltpu.VMEM((tm, d), F32)],
        compiler_params=_params("parallel", "arbitrary"),
        name="ffn_dense",
    )(x1, h, mod, wg, wu, wd)


MOE_TILE = 512
MOE_CHUNK = 256
MOE_TOK = 512


def _forward_fill(values, active):
    n = values.shape[0]
    idx = lax.cummax(jnp.where(active, jnp.arange(n), -1), axis=0)
    return values[jnp.maximum(idx, jnp.argmax(active))]


def _moe_plan(e0, e1, t, n_exp):
    c_max = MOE_TOK // MOE_CHUNK + 1
    ntt = t // MOE_TOK
    experts = jnp.arange(n_exp, dtype=jnp.int32)
    routed = ((e0[:, None] == experts) | (e1[:, None] == experts)).astype(jnp.int32)
    csum = jnp.cumsum(routed, axis=0)
    rank = csum - routed
    counts = csum[-1]
    padded = (counts + MOE_TILE - 1) // MOE_TILE * MOE_TILE
    ends = jnp.cumsum(padded)
    offs = ends - padded
    pos = offs[None, :] + rank
    pos0 = jnp.take_along_axis(pos, e0[:, None], axis=1)[:, 0]
    pos1 = jnp.take_along_axis(pos, e1[:, None], axis=1)[:, 0]
    n_tiles = (2 * t + n_exp * MOE_TILE) // MOE_TILE
    n_valid = ends[-1] // MOE_TILE
    tile_expert = jnp.sum(jnp.arange(n_tiles)[:, None] * MOE_TILE >= ends[None, :], axis=1)
    tile_expert = tile_expert[jnp.minimum(jnp.arange(n_tiles), n_valid - 1)].astype(jnp.int32)
    r_lo = rank[::MOE_TOK]
    r_hi = csum[MOE_TOK - 1::MOE_TOK]
    lo = offs[None, :] + r_lo
    hi = offs[None, :] + r_hi
    first_chunk = lo // MOE_CHUNK
    n_chunks = jnp.where(r_hi > r_lo, (hi - 1) // MOE_CHUNK - first_chunk + 1, 0)
    cs = jnp.arange(c_max)
    active = cs[None, None, :] < n_chunks[:, :, None]
    chunk = first_chunk[:, :, None] + cs[None, None, :]

    def tables(act, chk):
        act, chk = act.reshape(-1), chk.reshape(-1)
        sticky = _forward_fill(chk, act)
        prev = jnp.concatenate([jnp.full((1,), -1, sticky.dtype), sticky[:-1]])
        seen = jnp.concatenate([jnp.zeros((1,), bool), lax.cummax(act.astype(jnp.int32), axis=0)[:-1] > 0])
        first = act & ((sticky != prev) | ~seen)
        return sticky.astype(jnp.int32), act.astype(jnp.int32), first.astype(jnp.int32)

    combine_tabs = tables(active, chunk)
    dispatch_tabs = tables(active.transpose(1, 0, 2), chunk.transpose(1, 0, 2))
    return pos0, pos1, tile_expert, n_valid.astype(jnp.int32).reshape(1), dispatch_tabs, combine_tabs


def _dispatch_kernel(chunk_ref, act_ref, first_ref, info_ref, h_ref, zeros_ref, o_ref):
    del zeros_ref
    e, tt, c = pl.program_id(0), pl.program_id(1), pl.program_id(2)
    s = (e * pl.num_programs(1) + tt) * pl.num_programs(2) + c

    @pl.when(first_ref[s] == 1)
    def _():
        o_ref[...] = jnp.zeros(o_ref.shape, o_ref.dtype)

    @pl.when(act_ref[s] == 1)
    def _():
        info = info_ref[...]
        pos = jnp.where(info[0:1] == e, info[2:3], jnp.where(info[1:2] == e, info[3:4], -1))
        r = lax.broadcasted_iota(jnp.int32, (MOE_CHUNK, MOE_TOK), 0) + chunk_ref[s] * MOE_CHUNK
        onehot = jnp.where(r == pos, 1.0, 0.0).astype(BF16)
        o_ref[...] = (o_ref[...].astype(F32) + _dot(onehot, h_ref[...])).astype(o_ref.dtype)


def moe_dispatch(h, info, tabs, n_exp, n_rows):
    t, d = h.shape
    ntt = t // MOE_TOK
    c_max = MOE_TOK // MOE_CHUNK + 1

    def step(e, tt, c):
        return (e * ntt + tt) * c_max + c

    return pl.pallas_call(
        _dispatch_kernel,
        grid_spec=pltpu.PrefetchScalarGridSpec(
            num_scalar_prefetch=3, grid=(n_exp, ntt, c_max),
            in_specs=[pl.BlockSpec((None, 8, MOE_TOK), lambda e, tt, c, ch, ac, fi: (tt, 0, 0)),
                      pl.BlockSpec((MOE_TOK, d), lambda e, tt, c, ch, ac, fi: (tt, 0)),
                      pl.BlockSpec(memory_space=pl.ANY)],
            out_specs=pl.BlockSpec((MOE_CHUNK, d), lambda e, tt, c, ch, ac, fi: (ch[step(e, tt, c)], 0))),
        out_shape=jax.ShapeDtypeStruct((n_rows, d), BF16),
        input_output_aliases={5: 0},
        compiler_params=_params("arbitrary", "arbitrary", "arbitrary"),
        name="moe_dispatch",
    )(*tabs, info, h, jnp.zeros((n_rows, d), BF16))


def _experts_kernel(te_ref, nv_ref, x_ref, wg_ref, wu_ref, wd_ref, o_ref, acc_ref):
    j, f = pl.program_id(0), pl.program_id(1)
    valid = j < nv_ref[0]

    @pl.when(valid & (f == 0))
    def _():
        acc_ref[...] = jnp.zeros(acc_ref.shape, F32)

    @pl.when(valid)
    def _():
        acc_ref[...] += _swiglu_partial(x_ref[...], wg_ref, wu_ref, wd_ref)

    @pl.when(valid & (f == pl.num_programs(1) - 1))
    def _():
        o_ref[...] = acc_ref[...].astype(o_ref.dtype)


def moe_experts(x_sorted, tile_expert, n_valid, wg, wu, wd, tf=1408):
    n_rows, d = x_sorted.shape
    dff = wg.shape[2]
    nf = dff // tf
    n_tiles = n_rows // MOE_TILE

    def tile(j, nv):
        return jnp.minimum(j, nv[0] - 1)

    def fstep(j, f, nv):
        return jnp.where(j < nv[0], f, nf - 1)

    return pl.pallas_call(
        _experts_kernel,
        grid_spec=pltpu.PrefetchScalarGridSpec(
            num_scalar_prefetch=2, grid=(n_tiles, nf),
            in_specs=[pl.BlockSpec((MOE_TILE, d), lambda j, f, te, nv: (tile(j, nv), 0)),
                      pl.BlockSpec((None, d, tf), lambda j, f, te, nv: (te[j], 0, fstep(j, f, nv))),
                      pl.BlockSpec((None, d, tf), lambda j, f, te, nv: (te[j], 0, fstep(j, f, nv))),
                      pl.BlockSpec((None, tf, d), lambda j, f, te, nv: (te[j], fstep(j, f, nv), 0))],
            out_specs=pl.BlockSpec((MOE_TILE, d), lambda j, f, te, nv: (tile(j, nv), 0)),
            scratch_shapes=[pltpu.VMEM((MOE_TILE, d), F32)]),
        out_shape=jax.ShapeDtypeStruct((n_rows, d), BF16),
        compiler_params=_params("arbitrary", "arbitrary"),
        name="moe_experts",
    )(tile_expert, n_valid, x_sorted, wg, wu, wd)


def _combine_kernel(chunk_ref, act_ref, first_ref, route_ref, y_ref, x1_ref, mod_ref, o_ref, acc_ref):
    tt, e, c = pl.program_id(0), pl.program_id(1), pl.program_id(2)
    s = (tt * pl.num_programs(1) + e) * pl.num_programs(2) + c

    @pl.when((e == 0) & (c == 0))
    def _():
        acc_ref[...] = jnp.zeros(acc_ref.shape, F32)

    @pl.when(act_ref[s] == 1)
    def _():
        rt = route_ref[...]
        ef = e.astype(F32)
        is0, is1 = rt[:, 0:1] == ef, rt[:, 1:2] == ef
        pos = jnp.where(is0, rt[:, 4:5], jnp.where(is1, rt[:, 5:6], -1.0))
        wgt = jnp.where(is0, rt[:, 2:3], jnp.where(is1, rt[:, 3:4], 0.0))
        r = (lax.broadcasted_iota(jnp.int32, (MOE_TOK, MOE_CHUNK), 1) + chunk_ref[s] * MOE_CHUNK).astype(F32)
        onehot = jnp.where(r == pos, 1.0, 0.0).astype(BF16)
        acc_ref[...] += wgt * _dot(onehot, y_ref[...])

    @pl.when((e == pl.num_programs(1) - 1) & (c == pl.num_programs(2) - 1))
    def _():
        o_ref[...] = x1_ref[...] + mod_ref[5:6, :] * acc_ref[...]


def moe_combine(y_sorted, route, x1, mod, tabs, n_exp, seq):
    t, d = x1.shape
    ntt = t // MOE_TOK
    c_max = MOE_TOK // MOE_CHUNK + 1
    tiles_per_batch = seq // MOE_TOK

    def step(tt, e, c):
        return (tt * n_exp + e) * c_max + c

    row = lambda tt, e, c, ch, ac, fi: (tt, 0)
    return pl.pallas_call(
        _combine_kernel,
        grid_spec=pltpu.PrefetchScalarGridSpec(
            num_scalar_prefetch=3, grid=(ntt, n_exp, c_max),
            in_specs=[pl.BlockSpec((MOE_TOK, LANES), row),
                      pl.BlockSpec((MOE_CHUNK, d), lambda tt, e, c, ch, ac, fi: (ch[step(tt, e, c)], 0)),
                      pl.BlockSpec((MOE_TOK, d), row),
                      pl.BlockSpec((None, 6, d), lambda tt, e, c, ch, ac, fi: (tt // tiles_per_batch, 0, 0))],
            out_specs=pl.BlockSpec((MOE_TOK, d), row),
            scratch_shapes=[pltpu.VMEM((MOE_TOK, d), F32)]),
        out_shape=jax.ShapeDtypeStruct((t, d), F32),
        compiler_params=_params("arbitrary", "arbitrary", "arbitrary"),
        name="moe_combine",
    )(*tabs, route, y_sorted, x1, mod)


def ffn_moe(x1, h, route, mod, wg, wu, wd, seq):
    t, d = x1.shape
    n_exp = wg.shape[0]
    e0, e1 = route[:, 0].astype(jnp.int32), route[:, 1].astype(jnp.int32)
    pos0, pos1, tile_expert, n_valid, dispatch_tabs, combine_tabs = _moe_plan(e0, e1, t, n_exp)
    ntt = t // MOE_TOK
    zi = jnp.zeros((t,), jnp.int32)
    info = jnp.stack([e0, e1, pos0, pos1, zi, zi, zi, zi]).reshape(8, ntt, MOE_TOK).transpose(1, 0, 2)
    route = jnp.concatenate([route[:, 0:4], pos0[:, None].astype(F32), pos1[:, None].astype(F32),
                             jnp.zeros((t, LANES - 6), F32)], axis=1)
    n_rows = 2 * t + n_exp * MOE_TILE
    h_sorted = moe_dispatch(h, info, dispatch_tabs, n_exp, n_rows)
    y_sorted = moe_experts(h_sorted, tile_expert, n_valid, wg, wu, wd)
    return moe_combine(y_sorted, route, x1, mod, combine_tabs, n_exp, seq)


def _final_norm_kernel(x_ref, g_ref, o_ref):
    x = x_ref[...]
    o_ref[...] = (x * lax.rsqrt(jnp.mean(x * x, axis=-1, keepdims=True) + EPS)) * g_ref[...]


def final_norm(x, g, tm=512):
    t, d = x.shape
    return pl.pallas_call(
        _final_norm_kernel,
        grid=(t // tm,),
        in_specs=[pl.BlockSpec((tm, d), lambda i: (i, 0)), pl.BlockSpec((1, d), lambda i: (0, 0))],
        out_specs=pl.BlockSpec((tm, d), lambda i: (i, 0)),
        out_shape=jax.ShapeDtypeStruct((t, d), F32),
        compiler_params=_params("parallel"),
        name="final_norm",
    )(x, g)


def _rope_tables(positions, dim):
    rot = dim // ROPE_FRACTION
    half = rot // 2
    inv = 1.0 / (ROPE_THETA ** (np.arange(0, rot, 2, dtype=np.float32) / rot))
    ang = positions.reshape(-1).astype(F32)[:, None] * jnp.asarray(inv, F32)
    cos, sin = jnp.cos(ang), jnp.sin(ang)
    t = ang.shape[0]
    ones = jnp.ones((t, dim - rot), F32)
    zeros = lambda w: jnp.zeros((t, w), F32)
    reps = LANES // dim
    c = jnp.tile(jnp.concatenate([cos, cos, ones], axis=1), (1, reps))
    sa = jnp.tile(jnp.concatenate([-sin, zeros(dim - half)], axis=1), (1, reps))
    sb = jnp.tile(jnp.concatenate([zeros(half), sin, zeros(dim - rot)], axis=1), (1, reps))
    return c, sa, sb


def _relayout_w_in(w):
    d = w.shape[0]
    pts = np.cumsum([0, 384, 64, 64, 256, 64, 4, 256, 256, 256, 384, 384, 384])
    (q_a, k_a, v_a, q_i, k_i, w_i, q_b, k_b, v_b, q_c, k_c, v_c) = [w[:, pts[j]:pts[j + 1]] for j in range(12)]
    w_i_pad = jnp.concatenate([w_i, jnp.zeros((d, LANES - IDX_HEADS), w.dtype)], axis=1)
    return jnp.concatenate([q_a, k_a, k_a, v_a, v_a, q_i, k_i, k_i, w_i_pad,
                            q_b, k_b, v_b, q_c, k_c, v_c], axis=1).astype(BF16)


def kernel(x, c, positions, w_in, w_out, diff_lambda, diff_subln, w_ada, b_ada, g_attn, g_ffn, w_ff_gate,
           w_ff_up, w_ff_down, w_router, w_exp_gate, w_exp_up, w_exp_down, g_final):
    batch, seq, d = x.shape
    depth = w_in.shape[0]
    t = batch * seq
    tabs = _rope_tables(positions, HEAD_DIM) + _rope_tables(positions, B_QK_DIM)
    mod_all = adaln_mod(c, w_ada, b_ada).reshape(depth, batch, 6, d)
    xf = x.reshape(t, d)
    for layer in range(depth):
        mod = mod_all[layer]
        lam_init = 0.8 - 0.6 * math.exp(-0.3 * layer)
        (qa, kka, vva, qi, kki, wi, qb, kb, vb, qc, kc, vc) = inproj(
            xf, g_attn[layer].reshape(1, d), mod, tabs, _relayout_w_in(w_in[layer]), seq)
        ya = dsa_attention(qa, qi, wi, kka, vva, kki, batch, seq)
        g_sub2 = jnp.tile(diff_subln[layer], 2).reshape(1, LANES)
        yb = diff_attention(qb, kb, vb, diff_lambda[layer], g_sub2, lam_init, batch, seq)
        yc = moba_attention(qc, kc, vc, batch, seq)
        j = layer // 2
        wo = w_out[layer].astype(BF16)
        gf = g_ffn[layer].reshape(1, d)
        if layer % 2 == 0:
            x1, h = outproj(xf, ya, yb, yc, wo, gf, mod, seq)
            xf = ffn_dense(x1, h, mod, w_ff_gate[j].astype(BF16), w_ff_up[j].astype(BF16),
                           w_ff_down[j].astype(BF16), seq)
        else:
            wr = jnp.concatenate([w_router[j], jnp.zeros((d, LANES - N_EXPERTS), F32)], axis=1)
            x1, h, cw = outproj(xf, ya, yb, yc, wo, gf, mod, seq, w_router=wr)
            xf = ffn_moe(x1, h, cw, mod, w_exp_gate[j].astype(BF16), w_exp_up[j].astype(BF16),
                         w_exp_down[j].astype(BF16), seq)
    return final_norm(xf, g_final.reshape(1, d)).reshape(batch, seq, d)
```

```python
import functools
import math

import jax
import jax.numpy as jnp
import numpy as np
from jax import lax
from jax.experimental import pallas as pl
from jax.experimental.pallas import tpu as pltpu

F32 = jnp.float32
BF16 = jnp.bfloat16

HEAD_DIM = 64
A_HEADS = 6
IDX_HEADS = 4
B_HEADS = 4
B_QK_DIM = 32
C_HEADS = 6
DSA_TOPK_MAX = 256
MOBA_BLOCK = 256
MOBA_TOPK = 3
ROPE_THETA = 500000.0
ROPE_FRACTION = 4
SUBLN_EPS = 1e-5
EPS = 1e-6
N_EXPERTS = 8

LANES = 128
NEG = -1e30
INT_MIN = -2 ** 31
VMEM_LIMIT = 48 * 1024 * 1024

_G_QA, _G_KKA, _G_VVA, _G_QI, _G_KKI, _G_WI = (0, 384), (384, 512), (512, 640), (640, 896), (896, 1024), (1024, 1152)
_G_QB, _G_KB, _G_VB = (1152, 1408), (1408, 1664), (1664, 1920)
_G_QC, _G_KC, _G_VC = (1920, 2304), (2304, 2688), (2688, 3072)
D_IN_PAD = 3072


def _params(*sem):
    return pltpu.CompilerParams(dimension_semantics=sem, vmem_limit_bytes=VMEM_LIMIT)


def _dot(a, b):
    return jnp.dot(a, b, preferred_element_type=F32)


def _dot_nt(a, b, precision=None):
    return lax.dot_general(a, b, (((1,), (1,)), ((), ())), preferred_element_type=F32, precision=precision)


def _adaln_kernel(c_ref, w_ref, b_ref, o_ref):
    c = c_ref[...]
    c_act = c * (1.0 / (1.0 + jnp.exp(-c)))
    o_ref[...] = jnp.dot(c_act, w_ref[...], preferred_element_type=F32,
                         precision=lax.Precision.HIGHEST) + b_ref[...]


def adaln_mod(c, w_ada, b_ada, tn=1536):
    depth, d, n = w_ada.shape
    b = c.shape[0]
    return pl.pallas_call(
        _adaln_kernel,
        grid=(depth, n // tn),
        in_specs=[pl.BlockSpec((b, d), lambda l, j: (0, 0)),
                  pl.BlockSpec((None, d, tn), lambda l, j: (l, 0, j)),
                  pl.BlockSpec((None, 1, tn), lambda l, j: (l, 0, j))],
        out_specs=pl.BlockSpec((None, b, tn), lambda l, j: (l, 0, j)),
        out_shape=jax.ShapeDtypeStruct((depth, b, n), F32),
        compiler_params=_params("parallel", "parallel"),
        name="adaln_mod",
    )(c, w_ada, b_ada.reshape(depth, 1, n))


def _norm_mod(x, g, shift, scale, eps=EPS):
    y = x * lax.rsqrt(jnp.mean(x * x, axis=-1, keepdims=True) + eps)
    return (y * g) * (1.0 + scale) + shift


def _rope_store(acc, o_ref, cos, sa, sb, half):
    for j in range(acc.shape[1] // LANES):
        a = acc[:, j * LANES:(j + 1) * LANES]
        r = a * cos + pltpu.roll(a, half, 1) * sb + pltpu.roll(a, LANES - half, 1) * sa
        o_ref[:, j * LANES:(j + 1) * LANES] = r.astype(o_ref.dtype)


def _inproj_kernel(x_ref, g_ref, mod_ref, c64_ref, sa64_ref, sb64_ref, c32_ref, sa32_ref, sb32_ref, w_ref,
                   qa_ref, kka_ref, vva_ref, qi_ref, kki_ref, wi_ref,
                   qb_ref, kb_ref, vb_ref, qc_ref, kc_ref, vc_ref):
    h = _norm_mod(x_ref[...], g_ref[...], mod_ref[0:1, :], mod_ref[1:2, :]).astype(BF16)
    c64, sa64, sb64 = c64_ref[...], sa64_ref[...], sb64_ref[...]
    c32, sa32, sb32 = c32_ref[...], sa32_ref[...], sb32_ref[...]

    def proj(cols):
        return _dot(h, w_ref[:, cols[0]:cols[1]])

    qk_scale = HEAD_DIM ** -0.5
    _rope_store(proj(_G_QA), qa_ref, c64 * qk_scale, sa64 * qk_scale, sb64 * qk_scale, 8)
    _rope_store(proj(_G_KKA), kka_ref, c64, sa64, sb64, 8)
    vva_ref[...] = proj(_G_VVA).astype(vva_ref.dtype)
    _rope_store(proj(_G_QI), qi_ref, c64, sa64, sb64, 8)
    _rope_store(proj(_G_KKI), kki_ref, c64, sa64, sb64, 8)
    wi_ref[...] = proj(_G_WI) * (IDX_HEADS ** -0.5 * HEAD_DIM ** -0.5)
    b_scale = B_QK_DIM ** -0.5
    _rope_store(proj(_G_QB), qb_ref, c32 * b_scale, sa32 * b_scale, sb32 * b_scale, 4)
    _rope_store(proj(_G_KB), kb_ref, c32, sa32, sb32, 4)
    vb_ref[...] = proj(_G_VB).astype(vb_ref.dtype)
    _rope_store(proj(_G_QC), qc_ref, c64 * qk_scale, sa64 * qk_scale, sb64 * qk_scale, 8)
    _rope_store(proj(_G_KC), kc_ref, c64, sa64, sb64, 8)
    vc_ref[...] = proj(_G_VC).astype(vc_ref.dtype)


def inproj(x, g, mod, tabs, w_pad, seq, tm=512):
    t, d = x.shape
    tiles_per_batch = seq // tm
    row = lambda i: (i, 0)
    widths = [384, 128, 128, 256, 128, 128, 256, 256, 256, 384, 384, 384]
    dtypes = [BF16, BF16, BF16, BF16, BF16, F32, BF16, BF16, BF16, BF16, BF16, BF16]
    return pl.pallas_call(
        _inproj_kernel,
        grid=(t // tm,),
        in_specs=[pl.BlockSpec((tm, d), row),
                  pl.BlockSpec((1, d), lambda i: (0, 0)),
                  pl.BlockSpec((None, 6, d), lambda i: (i // tiles_per_batch, 0, 0))]
                 + [pl.BlockSpec((tm, LANES), row)] * 6
                 + [pl.BlockSpec((d, D_IN_PAD), lambda i: (0, 0))],
        out_specs=[pl.BlockSpec((tm, w), row) for w in widths],
        out_shape=[jax.ShapeDtypeStruct((t, w), dt) for w, dt in zip(widths, dtypes)],
        compiler_params=_params("parallel"),
        name="inproj",
    )(x, g, mod, *tabs, w_pad)


def _init_stats(m_ref, l_ref, acc_ref):
    m_ref[...] = jnp.full(m_ref.shape, -jnp.inf, F32)
    l_ref[...] = jnp.zeros(l_ref.shape, F32)
    acc_ref[...] = jnp.zeros(acc_ref.shape, F32)


def _key_to_f32(k):
    return lax.bitcast_convert_type(jnp.where(k >= 0, k, k ^ 0x7FFFFFFF), F32)


def dsa_attention(qa, qi, wi, kka, vva, kki, batch, seq, tq=256):
    t = qa.shape[0]
    nq = seq // tq
    k_top = min(DSA_TOPK_MAX, seq // 4)
    qrow = lambda b, i: (b * nq + i, 0)
    full = lambda b, i: (b, 0)
    return pl.pallas_call(
        functools.partial(_dsa_kernel_t, tq=tq, k_top=k_top, seq=seq),
        grid=(batch, nq),
        in_specs=[pl.BlockSpec((tq, 384), qrow), pl.BlockSpec((tq, 256), qrow), pl.BlockSpec((tq, LANES), qrow),
                  pl.BlockSpec((seq, LANES), full), pl.BlockSpec((seq, LANES), full),
                  pl.BlockSpec((seq, LANES), full)],
        out_specs=pl.BlockSpec((tq, 384), qrow),
        out_shape=jax.ShapeDtypeStruct((t, 384), BF16),
        scratch_shapes=[pltpu.VMEM((seq, tq), F32), pltpu.VMEM((1, LANES, seq), BF16),
                        pltpu.VMEM((3, 2 * tq, LANES), BF16),
                        pltpu.VMEM((3, 1, 2 * tq), F32), pltpu.VMEM((3, 1, 2 * tq), F32),
                        pltpu.VMEM((3, LANES, 2 * tq), F32)],
        compiler_params=_params("parallel", "arbitrary"),
        name="dsa_attention",
    )(qa, qi, wi, kka, vva, kki)


def diff_attention(qb, kb, vb, lam_vec, g_sub2, lam_init, batch, seq, tq=256):
    t = qb.shape[0]
    nq = seq // tq
    qrow = lambda b, i: (b * nq + i, 0)
    full = lambda b, i: (b, 0)
    const = lambda b, i: (0, 0)
    return pl.pallas_call(
        functools.partial(_diff_kernel_t, tq=tq, lam_init=lam_init, seq=seq),
        grid=(batch, nq),
        in_specs=[pl.BlockSpec((tq, 256), qrow), pl.BlockSpec((seq, 256), full), pl.BlockSpec((seq, 256), full),
                  pl.BlockSpec((4, B_QK_DIM), const), pl.BlockSpec((1, LANES), const)],
        out_specs=pl.BlockSpec((tq, 256), qrow),
        out_shape=jax.ShapeDtypeStruct((t, 256), BF16),
        scratch_shapes=[pltpu.VMEM((2, LANES, seq), BF16), pltpu.VMEM((2, 4 * tq, LANES), BF16),
                        pltpu.VMEM((2, 1, 4 * tq), F32), pltpu.VMEM((2, 1, 4 * tq), F32),
                        pltpu.VMEM((2, LANES, 4 * tq), F32)],
        compiler_params=_params("parallel", "arbitrary"),
        name="diff_attention",
    )(qb, kb, vb, lam_vec, g_sub2)


def moba_attention(qc, kc, vc, batch, seq):
    tq = MOBA_BLOCK
    t = qc.shape[0]
    nb = seq // tq
    n_sel = min(MOBA_TOPK, nb - 1)
    nbp = 8
    assert seq % tq == 0 and nb <= nbp
    qrow = lambda b, i: (b * nb + i, 0)
    full = lambda b, i: (b, 0)
    return pl.pallas_call(
        functools.partial(_moba_kernel_t, tq=tq, nb=nb, n_sel=n_sel, seq=seq),
        grid=(batch, nb),
        in_specs=[pl.BlockSpec((tq, 384), qrow), pl.BlockSpec((seq, 384), full), pl.BlockSpec((seq, 384), full)],
        out_specs=pl.BlockSpec((tq, 384), qrow),
        out_shape=jax.ShapeDtypeStruct((t, 384), BF16),
        scratch_shapes=[pltpu.VMEM((nbp, 384), F32), pltpu.VMEM((3, LANES, seq), BF16),
                        pltpu.VMEM((3, 2 * tq, LANES), BF16), pltpu.VMEM((3, nbp, 2 * tq), F32),
                        pltpu.VMEM((3, 1, 2 * tq), F32), pltpu.VMEM((3, 1, 2 * tq), F32),
                        pltpu.VMEM((3, LANES, 2 * tq), F32)],
        compiler_params=_params("parallel", "arbitrary"),
        name="moba_attention",
    )(qc, kc, vc)


def _attend_t(k, q_ref, g, v_t, m_ref, l_ref, acc_ref, mask=None):
    for j in range(q_ref.shape[1] // LANES):
        cols = slice(j * LANES, (j + 1) * LANES)
        s = _dot_nt(k, q_ref[g, cols, :])
        if mask is not None:
            s = mask(j, s)
        m_prev = m_ref[g, :, cols]
        m_new = jnp.maximum(m_prev, jnp.max(s, axis=0, keepdims=True))
        alpha = jnp.exp(m_prev - m_new)
        p = jnp.exp(s - m_new)
        l_ref[g, :, cols] = alpha * l_ref[g, :, cols] + jnp.sum(p, axis=0, keepdims=True)
        acc_ref[g, :, cols] = alpha * acc_ref[g, :, cols] + _dot(v_t, p.astype(BF16))
        m_ref[g, :, cols] = m_new


def _causal_mask_t(tk, tq):
    key = lax.broadcasted_iota(jnp.int32, (tk, LANES), 0)
    qry = lax.broadcasted_iota(jnp.int32, (tk, LANES), 1)

    def mask(j, s):
        return jnp.where(key <= qry + (j * LANES) % tq, s, NEG)
    return mask


def _transpose_values(v_ref, vt_ref, seq, tk):
    for g in range(vt_ref.shape[0]):
        for n in range(seq // tk):
            blk = v_ref[n * tk:(n + 1) * tk, g * LANES:(g + 1) * LANES].astype(F32)
            vt_ref[g, :, n * tk:(n + 1) * tk] = blk.T.astype(vt_ref.dtype)


def _head_pair_out(acc_t, l, tq):
    even = acc_t[0:HEAD_DIM, 0:tq] / l[:, 0:tq]
    odd = acc_t[HEAD_DIM:LANES, tq:2 * tq] / l[:, tq:2 * tq]
    return jnp.concatenate([even, odd], axis=0).T


def _dsa_kernel_t(qa_ref, qi_ref, wiq_ref, kka_ref, vva_ref, kki_ref, o_ref,
                  sc_ref, vt_ref, qs_ref, m_ref, l_ref, acc_ref, *, tq, k_top, seq):
    i = pl.program_id(1)
    nk = i + 1
    t0 = i * tq
    lo = lax.broadcasted_iota(jnp.int32, (tq, LANES), 1) < HEAD_DIM
    zero_b = jnp.zeros((tq, LANES), BF16)

    @pl.when(i == 0)
    def _():
        _transpose_values(vva_ref, vt_ref, seq, tq)

    def stack_heads(q):
        out = []
        for g in range(q.shape[1] // LANES):
            qg = q[:, g * LANES:(g + 1) * LANES]
            out += [jnp.where(lo, qg, zero_b), jnp.where(lo, zero_b, qg)]
        return out

    qa_stack = stack_heads(qa_ref[...])
    for g in range(3):
        qs_ref[g, 0:tq, :] = qa_stack[2 * g]
        qs_ref[g, tq:2 * tq, :] = qa_stack[2 * g + 1]
    qi_stack = jnp.concatenate(stack_heads(qi_ref[...]), axis=0)
    wi_t = wiq_ref[...].T

    def rows(c):
        return pl.ds(pl.multiple_of(c * tq, tq), tq)

    key_pos = lax.broadcasted_iota(jnp.int32, (tq, tq), 0)
    qry_pos = lax.broadcasted_iota(jnp.int32, (tq, tq), 1)

    def idx_body(c, carry):
        r = jnp.maximum(_dot_nt(kki_ref[rows(c), :], qi_stack), 0.0)
        s = wi_t[0:1, :] * r[:, 0:tq]
        for h in range(1, IDX_HEADS):
            s = s + wi_t[h:h + 1, :] * r[:, h * tq:(h + 1) * tq]
        causal = (c * tq + key_pos) <= (t0 + qry_pos)
        sc_ref[rows(c), :] = jnp.where(causal, s, -jnp.inf)
        return carry

    lax.fori_loop(0, nk, idx_body, 0)

    def count_ge(cand):
        def body(c, acc):
            ge = jnp.where(sc_ref[rows(c), :] >= cand, 1.0, 0.0)
            return acc + jnp.sum(ge.reshape(tq // 8, 8, tq), axis=0)
        acc = lax.fori_loop(0, nk, body, jnp.zeros((8, tq), F32))
        return jnp.sum(acc, axis=0, keepdims=True)

    kf = float(k_top)

    def search():
        key0 = jnp.where(count_ge(jnp.zeros((1, tq), F32)) >= kf, 0, INT_MIN).astype(jnp.int32)

        def bisect(b, key):
            trial = key | lax.shift_left(jnp.int32(1), 30 - b)
            return jnp.where(count_ge(_key_to_f32(trial)) >= kf, trial, key)
        return lax.fori_loop(0, 31, bisect, key0)

    key = lax.cond(t0 + tq <= k_top, lambda: jnp.full((1, tq), INT_MIN, jnp.int32), search)
    thr = _key_to_f32(key)
    thr_next = _key_to_f32(key + 1)
    need = kf - count_ge(thr_next)
    all_sel = (t0 + lax.broadcasted_iota(jnp.int32, (1, tq), 1)) < k_top
    lower = (qry_pos <= key_pos).astype(BF16)

    _init_stats(m_ref, l_ref, acc_ref)

    def att_body(c, tie_run):
        s_idx = sc_ref[rows(c), :]
        ge = s_idx >= thr
        gt = s_idx >= thr_next
        tie = jnp.where(gt, 0.0, jnp.where(ge, 1.0, 0.0))
        prefix = _dot(lower, tie.astype(BF16)) + tie_run
        take = jnp.where(gt, 1.0, jnp.where(prefix <= need, tie, 0.0))
        take = jnp.where(all_sel, 1.0, take)
        take = jnp.where(s_idx > -jnp.inf, take, 0.0)

        def mask(j, s):
            q0 = (j * LANES) % tq
            return jnp.where(take[:, q0:q0 + LANES] > 0.5, s, NEG)

        kk = kka_ref[rows(c), :]
        vt = vt_ref[0, :, rows(c)]
        for g in range(3):
            _attend_t(kk, qs_ref, g, vt, m_ref, l_ref, acc_ref, mask)
        return tie_run + jnp.sum(tie, axis=0, keepdims=True)

    lax.fori_loop(0, nk, att_body, jnp.zeros((1, tq), F32))

    for g in range(3):
        o_ref[:, g * LANES:(g + 1) * LANES] = _head_pair_out(acc_ref[g], l_ref[g], tq).astype(o_ref.dtype)


def _diff_kernel_t(qb_ref, kb_ref, vb_ref, lam_ref, gsub_ref, o_ref,
                   vt_ref, qs_ref, m_ref, l_ref, acc_ref, *, tq, lam_init, seq):
    i = pl.program_id(1)
    lane = lax.broadcasted_iota(jnp.int32, (tq, LANES), 1)
    zero_b = jnp.zeros((tq, LANES), BF16)

    @pl.when(i == 0)
    def _():
        _transpose_values(vb_ref, vt_ref, seq, tq)

    qb = qb_ref[...]
    for g in range(2):
        qg = qb[:, g * LANES:(g + 1) * LANES]
        for j in range(4):
            qs_ref[g, j * tq:(j + 1) * tq, :] = jnp.where(lane // B_QK_DIM == j, qg, zero_b)
    _init_stats(m_ref, l_ref, acc_ref)

    def rows(c):
        return pl.ds(pl.multiple_of(c * tq, tq), tq)

    def step(c, masked):
        mask = _causal_mask_t(tq, tq) if masked else None
        for g in range(2):
            _attend_t(kb_ref[rows(c), g * LANES:(g + 1) * LANES], qs_ref, g, vt_ref[g, :, rows(c)],
                      m_ref, l_ref, acc_ref, mask)

    def body(c, carry):
        step(c, False)
        return carry

    lax.fori_loop(0, i, body, 0)
    step(i, True)

    lv = lam_ref[...]
    lam = (jnp.exp(jnp.sum(lv[0:1] * lv[1:2], axis=1, keepdims=True))
           - jnp.exp(jnp.sum(lv[2:3] * lv[3:4], axis=1, keepdims=True)) + lam_init)
    gsub = gsub_ref[...]
    for g in range(2):
        acc_t, l = acc_ref[g], l_ref[g]

        def prob(r0, j):
            return acc_t[r0:r0 + HEAD_DIM, j * tq:(j + 1) * tq] / l[:, j * tq:(j + 1) * tq]

        halves = []
        for r0, j in ((0, 0), (HEAD_DIM, 2)):
            o = prob(r0, j) - lam * prob(r0, j + 1)
            ms = jnp.mean(o * o, axis=0, keepdims=True)
            halves.append(o * lax.rsqrt(ms + SUBLN_EPS))
        y = (jnp.concatenate(halves, axis=0).T * gsub) * (1.0 - lam_init)
        o_ref[:, g * LANES:(g + 1) * LANES] = y.astype(o_ref.dtype)


def _moba_kernel_t(qc_ref, kc_ref, vc_ref, o_ref, kmean_ref, vt_ref, qs_ref, bias_ref, m_ref, l_ref, acc_ref,
                   *, tq, nb, n_sel, seq):
    i = pl.program_id(1)
    nbp = kmean_ref.shape[0]
    lane = lax.broadcasted_iota(jnp.int32, (tq, LANES), 1)
    lo = lane < HEAD_DIM
    zero_b = jnp.zeros((tq, LANES), BF16)

    @pl.when(i == 0)
    def _():
        _transpose_values(vc_ref, vt_ref, seq, tq)
        kmean_ref[...] = jnp.zeros(kmean_ref.shape, F32)
        for n in range(nb):
            kblk = kc_ref[n * tq:(n + 1) * tq, :].astype(F32)
            kmean_ref[n:n + 1, :] = jnp.mean(kblk, axis=0, keepdims=True)

    qc = qc_ref[...]
    sub = lax.broadcasted_iota(jnp.int32, (nbp, 2 * tq), 0)
    past = sub < i
    for g in range(3):
        qg = qc[:, g * LANES:(g + 1) * LANES]
        q2 = jnp.concatenate([jnp.where(lo, qg, zero_b), jnp.where(lo, zero_b, qg)], axis=0)
        gt = _dot_nt(kmean_ref[:, g * LANES:(g + 1) * LANES], q2.astype(F32), precision=lax.Precision.HIGHEST)
        rows_ = []
        for n in range(nbp):
            gn = gt[n:n + 1, :]
            beats = jnp.where(sub < n, jnp.where(gt >= gn, 1.0, 0.0), jnp.where(gt > gn, 1.0, 0.0))
            beats = jnp.where(sub == n, 0.0, jnp.where(past, beats, 0.0))
            rank = jnp.sum(beats, axis=0, keepdims=True)
            rows_.append(jnp.where(rank < n_sel, 0.0, NEG))
        bias_ref[g] = jnp.where(past, jnp.concatenate(rows_, axis=0), NEG)
        qs_ref[g] = q2
    _init_stats(m_ref, l_ref, acc_ref)

    def rows(c):
        return pl.ds(pl.multiple_of(c * tq, tq), tq)

    def body(c, carry):
        for g in range(3):
            bias = bias_ref[g, pl.ds(c, 1), :]

            def mask(j, s):
                return s + bias[:, j * LANES:(j + 1) * LANES]

            _attend_t(kc_ref[rows(c), g * LANES:(g + 1) * LANES], qs_ref, g, vt_ref[g, :, rows(c)],
                      m_ref, l_ref, acc_ref, mask)
        return carry

    lax.fori_loop(0, i, body, 0)

    causal = _causal_mask_t(tq, tq)
    for g in range(3):
        _attend_t(kc_ref[rows(i), g * LANES:(g + 1) * LANES], qs_ref, g, vt_ref[g, :, rows(i)],
                  m_ref, l_ref, acc_ref, causal)
        o_ref[:, g * LANES:(g + 1) * LANES] = _head_pair_out(acc_ref[g], l_ref[g], tq).astype(o_ref.dtype)


def _route(h, wr_ref, cw_ref):
    logits = jnp.dot(h, wr_ref[...], preferred_element_type=F32, precision=lax.Precision.HIGHEST)
    lane = lax.broadcasted_iota(jnp.int32, logits.shape, 1)
    lg = jnp.where(lane < N_EXPERTS, logits, -jnp.inf)
    v0 = jnp.max(lg, axis=1, keepdims=True)
    i0 = jnp.min(jnp.where(lg == v0, lane, LANES), axis=1, keepdims=True)
    lg1 = jnp.where(lane == i0, -jnp.inf, lg)
    v1 = jnp.max(lg1, axis=1, keepdims=True)
    i1 = jnp.min(jnp.where(lg1 == v1, lane, LANES), axis=1, keepdims=True)
    e1 = jnp.exp(v1 - v0)
    w0 = 1.0 / (1.0 + e1)
    cw_ref[...] = (jnp.where(lane == 0, i0.astype(F32), 0.0) + jnp.where(lane == 1, i1.astype(F32), 0.0)
                   + jnp.where(lane == 2, w0, 0.0) + jnp.where(lane == 3, e1 * w0, 0.0))


def _outproj_kernel(x_ref, ya_ref, yb_ref, yc_ref, wo_ref, g_ref, mod_ref, *rest, with_router):
    if with_router:
        wr_ref, x1_ref, h_ref, cw_ref = rest
    else:
        x1_ref, h_ref = rest
    y = (_dot(ya_ref[...], wo_ref[0:384, :]) + _dot(yb_ref[...], wo_ref[384:640, :])
         + _dot(yc_ref[...], wo_ref[640:1024, :]))
    x1 = x_ref[...] + mod_ref[2:3, :] * y
    x1_ref[...] = x1
    h = _norm_mod(x1, g_ref[...], mod_ref[3:4, :], mod_ref[4:5, :])
    h_ref[...] = h.astype(BF16)
    if with_router:
        _route(h, wr_ref, cw_ref)


def outproj(x, ya, yb, yc, wo, g_ffn, mod, seq, w_router=None, tm=512):
    t, d = x.shape
    tiles_per_batch = seq // tm
    row = lambda i: (i, 0)
    const = lambda i: (0, 0)
    with_router = w_router is not None
    in_specs = [pl.BlockSpec((tm, d), row), pl.BlockSpec((tm, 384), row), pl.BlockSpec((tm, 256), row),
                pl.BlockSpec((tm, 384), row), pl.BlockSpec((d, d), const), pl.BlockSpec((1, d), const),
                pl.BlockSpec((None, 6, d), lambda i: (i // tiles_per_batch, 0, 0))]
    out_specs = [pl.BlockSpec((tm, d), row), pl.BlockSpec((tm, d), row)]
    out_shape = [jax.ShapeDtypeStruct((t, d), F32), jax.ShapeDtypeStruct((t, d), BF16)]
    args = [x, ya, yb, yc, wo, g_ffn, mod]
    if with_router:
        in_specs.append(pl.BlockSpec((d, LANES), const))
        out_specs.append(pl.BlockSpec((tm, LANES), row))
        out_shape.append(jax.ShapeDtypeStruct((t, LANES), F32))
        args.append(w_router)
    return pl.pallas_call(
        functools.partial(_outproj_kernel, with_router=with_router),
        grid=(t // tm,), in_specs=in_specs, out_specs=out_specs, out_shape=out_shape,
        compiler_params=_params("parallel"),
        name="outproj_router" if with_router else "outproj",
    )(*args)


def _swiglu_partial(h, wg_ref, wu_ref, wd_ref):
    a = _dot(h, wg_ref[...])
    u = _dot(h, wu_ref[...])
    act = (a * (1.0 / (1.0 + jnp.exp(-a)))) * u
    return _dot(act.astype(BF16), wd_ref[...])


def _ffn_kernel(x1_ref, h_ref, mod_ref, wg_ref, wu_ref, wd_ref, o_ref, acc_ref):
    f = pl.program_id(1)

    @pl.when(f == 0)
    def _():
        acc_ref[...] = jnp.zeros(acc_ref.shape, F32)

    acc_ref[...] += _swiglu_partial(h_ref[...], wg_ref, wu_ref, wd_ref)

    @pl.when(f == pl.num_programs(1) - 1)
    def _():
        o_ref[...] = x1_ref[...] + mod_ref[5:6, :] * acc_ref[...]


def ffn_dense(x1, h, mod, wg, wu, wd, seq, tm=512, tf=1408):
    t, d = x1.shape
    dff = wg.shape[1]
    tiles_per_batch = seq // tm
    row = lambda i, f: (i, 0)
    return pl.pallas_call(
        _ffn_kernel,
        grid=(t // tm, dff // tf),
        in_specs=[pl.BlockSpec((tm, d), row), pl.BlockSpec((tm, d), row),
                  pl.BlockSpec((None, 6, d), lambda i, f: (i // tiles_per_batch, 0, 0)),
                  pl.BlockSpec((d, tf), lambda i, f: (0, f)), pl.BlockSpec((d, tf), lambda i, f: (0, f)),
                  pl.BlockSpec((tf, d), lambda i, f: (f, 0))],
        out_specs=pl.BlockSpec((tm, d), row),
        out_shape=jax.ShapeDtypeStruct((t, d), F32),
        scratch_shapes=[pltpu.VMEM((tm, d), F32)],
        compiler_params=_params("parallel", "arbitrary"),
        name="ffn_dense",
    )(x1, h, mod, wg, wu, wd)


MOE_TILE = 512
MOE_CHUNK = 512
MOE_TOK = 1024


def _forward_fill(values, active):
    n = values.shape[0]
    idx = lax.cummax(jnp.where(active, jnp.arange(n), -1), axis=0)
    return values[jnp.maximum(idx, jnp.argmax(active))]


def _moe_plan(e0, e1, t, n_exp):
    c_max = MOE_TOK // MOE_CHUNK + 1
    ntt = t // MOE_TOK
    experts = jnp.arange(n_exp, dtype=jnp.int32)
    routed = ((e0[:, None] == experts) | (e1[:, None] == experts)).astype(jnp.int32)
    csum = jnp.cumsum(routed, axis=0)
    rank = csum - routed
    counts = csum[-1]
    padded = (counts + MOE_TILE - 1) // MOE_TILE * MOE_TILE
    ends = jnp.cumsum(padded)
    offs = ends - padded
    pos = offs[None, :] + rank
    pos0 = jnp.take_along_axis(pos, e0[:, None], axis=1)[:, 0]
    pos1 = jnp.take_along_axis(pos, e1[:, None], axis=1)[:, 0]
    n_tiles = (2 * t + n_exp * MOE_TILE) // MOE_TILE
    n_valid = ends[-1] // MOE_TILE
    tile_expert = jnp.sum(jnp.arange(n_tiles)[:, None] * MOE_TILE >= ends[None, :], axis=1)
    tile_expert = tile_expert[jnp.minimum(jnp.arange(n_tiles), n_valid - 1)].astype(jnp.int32)
    r_lo = rank[::MOE_TOK]
    r_hi = csum[MOE_TOK - 1::MOE_TOK]
    lo = offs[None, :] + r_lo
    hi = offs[None, :] + r_hi
    first_chunk = lo // MOE_CHUNK
    n_chunks = jnp.where(r_hi > r_lo, (hi - 1) // MOE_CHUNK - first_chunk + 1, 0)
    cs = jnp.arange(c_max)
    active = cs[None, None, :] < n_chunks[:, :, None]
    chunk = first_chunk[:, :, None] + cs[None, None, :]

    def tables(act, chk):
        act, chk = act.reshape(-1), chk.reshape(-1)
        sticky = _forward_fill(chk, act)
        prev = jnp.concatenate([jnp.full((1,), -1, sticky.dtype), sticky[:-1]])
        seen = jnp.concatenate([jnp.zeros((1,), bool), lax.cummax(act.astype(jnp.int32), axis=0)[:-1] > 0])
        first = act & ((sticky != prev) | ~seen)
        return sticky.astype(jnp.int32), act.astype(jnp.int32), first.astype(jnp.int32)

    combine_tabs = tables(active, chunk)
    dispatch_tabs = tables(active.transpose(1, 0, 2), chunk.transpose(1, 0, 2))
    return pos0, pos1, tile_expert, n_valid.astype(jnp.int32).reshape(1), dispatch_tabs, combine_tabs


def _dispatch_kernel(chunk_ref, act_ref, first_ref, info_ref, h_ref, zeros_ref, o_ref):
    del zeros_ref
    e, tt, c = pl.program_id(0), pl.program_id(1), pl.program_id(2)
    s = (e * pl.num_programs(1) + tt) * pl.num_programs(2) + c

    @pl.when(first_ref[s] == 1)
    def _():
        o_ref[...] = jnp.zeros(o_ref.shape, o_ref.dtype)

    @pl.when(act_ref[s] == 1)
    def _():
        info = info_ref[...]
        pos = jnp.where(info[0:1] == e, info[2:3], jnp.where(info[1:2] == e, info[3:4], -1))
        r = lax.broadcasted_iota(jnp.int32, (MOE_CHUNK, MOE_TOK), 0) + chunk_ref[s] * MOE_CHUNK
        onehot = jnp.where(r == pos, 1.0, 0.0).astype(BF16)
        o_ref[...] = (o_ref[...].astype(F32) + _dot(onehot, h_ref[...])).astype(o_ref.dtype)


def moe_dispatch(h, info, tabs, n_exp, n_rows):
    t, d = h.shape
    ntt = t // MOE_TOK
    c_max = MOE_TOK // MOE_CHUNK + 1

    def step(e, tt, c):
        return (e * ntt + tt) * c_max + c

    return pl.pallas_call(
        _dispatch_kernel,
        grid_spec=pltpu.PrefetchScalarGridSpec(
            num_scalar_prefetch=3, grid=(n_exp, ntt, c_max),
            in_specs=[pl.BlockSpec((None, 8, MOE_TOK), lambda e, tt, c, ch, ac, fi: (tt, 0, 0)),
                      pl.BlockSpec((MOE_TOK, d), lambda e, tt, c, ch, ac, fi: (tt, 0)),
                      pl.BlockSpec(memory_space=pl.ANY)],
            out_specs=pl.BlockSpec((MOE_CHUNK, d), lambda e, tt, c, ch, ac, fi: (ch[step(e, tt, c)], 0))),
        out_shape=jax.ShapeDtypeStruct((n_rows, d), BF16),
        input_output_aliases={5: 0},
        compiler_params=_params("arbitrary", "arbitrary", "arbitrary"),
        name="moe_dispatch",
    )(*tabs, info, h, jnp.zeros((n_rows, d), BF16))


def _experts_kernel(te_ref, nv_ref, x_ref, wg_ref, wu_ref, wd_ref, o_ref, acc_ref):
    j, f = pl.program_id(0), pl.program_id(1)
    valid = j < nv_ref[0]

    @pl.when(valid & (f == 0))
    def _():
        acc_ref[...] = jnp.zeros(acc_ref.shape, F32)

    @pl.when(valid)
    def _():
        acc_ref[...] += _swiglu_partial(x_ref[...], wg_ref, wu_ref, wd_ref)

    @pl.when(valid & (f == pl.num_programs(1) - 1))
    def _():
        o_ref[...] = acc_ref[...].astype(o_ref.dtype)


def moe_experts(x_sorted, tile_expert, n_valid, wg, wu, wd, tf=1408):
    n_rows, d = x_sorted.shape
    dff = wg.shape[2]
    nf = dff // tf
    n_tiles = n_rows // MOE_TILE

    def tile(j, nv):
        return jnp.minimum(j, nv[0] - 1)

    def fstep(j, f, nv):
        return jnp.where(j < nv[0], f, nf - 1)

    return pl.pallas_call(
        _experts_kernel,
        grid_spec=pltpu.PrefetchScalarGridSpec(
            num_scalar_prefetch=2, grid=(n_tiles, nf),
            in_specs=[pl.BlockSpec((MOE_TILE, d), lambda j, f, te, nv: (tile(j, nv), 0)),
                      pl.BlockSpec((None, d, tf), lambda j, f, te, nv: (te[j], 0, fstep(j, f, nv))),
                      pl.BlockSpec((None, d, tf), lambda j, f, te, nv: (te[j], 0, fstep(j, f, nv))),
                      pl.BlockSpec((None, tf, d), lambda j, f, te, nv: (te[j], fstep(j, f, nv), 0))],
            out_specs=pl.BlockSpec((MOE_TILE, d), lambda j, f, te, nv: (tile(j, nv), 0)),
            scratch_shapes=[pltpu.VMEM((MOE_TILE, d), F32)]),
        out_shape=jax.ShapeDtypeStruct((n_rows, d), BF16),
        compiler_params=_params("arbitrary", "arbitrary"),
        name="moe_experts",
    )(tile_expert, n_valid, x_sorted, wg, wu, wd)


def _combine_kernel(chunk_ref, act_ref, first_ref, route_ref, y_ref, x1_ref, mod_ref, o_ref, acc_ref):
    tt, e, c = pl.program_id(0), pl.program_id(1), pl.program_id(2)
    s = (tt * pl.num_programs(1) + e) * pl.num_programs(2) + c

    @pl.when((e == 0) & (c == 0))
    def _():
        acc_ref[...] = jnp.zeros(acc_ref.shape, F32)

    @pl.when(act_ref[s] == 1)
    def _():
        rt = route_ref[...]
        ef = e.astype(F32)
        is0, is1 = rt[:, 0:1] == ef, rt[:, 1:2] == ef
        pos = jnp.where(is0, rt[:, 4:5], jnp.where(is1, rt[:, 5:6], -1.0))
        wgt = jnp.where(is0, rt[:, 2:3], jnp.where(is1, rt[:, 3:4], 0.0))
        r = (lax.broadcasted_iota(jnp.int32, (MOE_TOK, MOE_CHUNK), 1) + chunk_ref[s] * MOE_CHUNK).astype(F32)
        onehot = jnp.where(r == pos, 1.0, 0.0).astype(BF16)
        acc_ref[...] += wgt * _dot(onehot, y_ref[...])

    @pl.when((e == pl.num_programs(1) - 1) & (c == pl.num_programs(2) - 1))
    def _():
        o_ref[...] = x1_ref[...] + mod_ref[5:6, :] * acc_ref[...]


def moe_combine(y_sorted, route, x1, mod, tabs, n_exp, seq):
    t, d = x1.shape
    ntt = t // MOE_TOK
    c_max = MOE_TOK // MOE_CHUNK + 1
    assert seq % MOE_TOK == 0
    tiles_per_batch = seq // MOE_TOK

    def step(tt, e, c):
        return (tt * n_exp + e) * c_max + c

    row = lambda tt, e, c, ch, ac, fi: (tt, 0)
    return pl.pallas_call(
        _combine_kernel,
        grid_spec=pltpu.PrefetchScalarGridSpec(
            num_scalar_prefetch=3, grid=(ntt, n_exp, c_max),
            in_specs=[pl.BlockSpec((MOE_TOK, LANES), row),
                      pl.BlockSpec((MOE_CHUNK, d), lambda tt, e, c, ch, ac, fi: (ch[step(tt, e, c)], 0)),
                      pl.BlockSpec((MOE_TOK, d), row),
                      pl.BlockSpec((None, 6, d), lambda tt, e, c, ch, ac, fi: (tt // tiles_per_batch, 0, 0))],
            out_specs=pl.BlockSpec((MOE_TOK, d), row),
            scratch_shapes=[pltpu.VMEM((MOE_TOK, d), F32)]),
        out_shape=jax.ShapeDtypeStruct((t, d), F32),
        compiler_params=_params("arbitrary", "arbitrary", "arbitrary"),
        name="moe_combine",
    )(*tabs, route, y_sorted, x1, mod)


def ffn_moe(x1, h, route, mod, wg, wu, wd, seq):
    t, d = x1.shape
    n_exp = wg.shape[0]
    e0, e1 = route[:, 0].astype(jnp.int32), route[:, 1].astype(jnp.int32)
    pos0, pos1, tile_expert, n_valid, dispatch_tabs, combine_tabs = _moe_plan(e0, e1, t, n_exp)
    ntt = t // MOE_TOK
    zi = jnp.zeros((t,), jnp.int32)
    info = jnp.stack([e0, e1, pos0, pos1, zi, zi, zi, zi]).reshape(8, ntt, MOE_TOK).transpose(1, 0, 2)
    route = jnp.concatenate([route[:, 0:4], pos0[:, None].astype(F32), pos1[:, None].astype(F32),
                             jnp.zeros((t, LANES - 6), F32)], axis=1)
    n_rows = 2 * t + n_exp * MOE_TILE
    h_sorted = moe_dispatch(h, info, dispatch_tabs, n_exp, n_rows)
    y_sorted = moe_experts(h_sorted, tile_expert, n_valid, wg, wu, wd)
    return moe_combine(y_sorted, route, x1, mod, combine_tabs, n_exp, seq)


def _final_norm_kernel(x_ref, g_ref, o_ref):
    x = x_ref[...]
    o_ref[...] = (x * lax.rsqrt(jnp.mean(x * x, axis=-1, keepdims=True) + EPS)) * g_ref[...]


def final_norm(x, g, tm=512):
    t, d = x.shape
    return pl.pallas_call(
        _final_norm_kernel,
        grid=(t // tm,),
        in_specs=[pl.BlockSpec((tm, d), lambda i: (i, 0)), pl.BlockSpec((1, d), lambda i: (0, 0))],
        out_specs=pl.BlockSpec((tm, d), lambda i: (i, 0)),
        out_shape=jax.ShapeDtypeStruct((t, d), F32),
        compiler_params=_params("parallel"),
        name="final_norm",
    )(x, g)


def _rope_tables(positions, dim):
    rot = dim // ROPE_FRACTION
    half = rot // 2
    inv = 1.0 / (ROPE_THETA ** (np.arange(0, rot, 2, dtype=np.float32) / rot))
    ang = positions.reshape(-1).astype(F32)[:, None] * jnp.asarray(inv, F32)
    cos, sin = jnp.cos(ang), jnp.sin(ang)
    t = ang.shape[0]
    ones = jnp.ones((t, dim - rot), F32)
    zeros = lambda w: jnp.zeros((t, w), F32)
    reps = LANES // dim
    c = jnp.tile(jnp.concatenate([cos, cos, ones], axis=1), (1, reps))
    sa = jnp.tile(jnp.concatenate([-sin, zeros(dim - half)], axis=1), (1, reps))
    sb = jnp.tile(jnp.concatenate([zeros(half), sin, zeros(dim - rot)], axis=1), (1, reps))
    return c, sa, sb


def _relayout_w_in(w):
    d = w.shape[0]
    pts = np.cumsum([0, 384, 64, 64, 256, 64, 4, 256, 256, 256, 384, 384, 384])
    (q_a, k_a, v_a, q_i, k_i, w_i, q_b, k_b, v_b, q_c, k_c, v_c) = [w[:, pts[j]:pts[j + 1]] for j in range(12)]
    w_i_pad = jnp.concatenate([w_i, jnp.zeros((d, LANES - IDX_HEADS), w.dtype)], axis=1)
    return jnp.concatenate([q_a, k_a, k_a, v_a, v_a, q_i, k_i, k_i, w_i_pad,
                            q_b, k_b, v_b, q_c, k_c, v_c], axis=1).astype(BF16)


def kernel(x, c, positions, w_in, w_out, diff_lambda, diff_subln, w_ada, b_ada, g_attn, g_ffn, w_ff_gate,
           w_ff_up, w_ff_down, w_router, w_exp_gate, w_exp_up, w_exp_down, g_final):
    batch, seq, d = x.shape
    depth = w_in.shape[0]
    t = batch * seq
    tabs = _rope_tables(positions, HEAD_DIM) + _rope_tables(positions, B_QK_DIM)
    mod_all = adaln_mod(c, w_ada, b_ada).reshape(depth, batch, 6, d)
    xf = x.reshape(t, d)
    for layer in range(depth):
        mod = mod_all[layer]
        lam_init = 0.8 - 0.6 * math.exp(-0.3 * layer)
        (qa, kka, vva, qi, kki, wi, qb, kb, vb, qc, kc, vc) = inproj(
            xf, g_attn[layer].reshape(1, d), mod, tabs, _relayout_w_in(w_in[layer]), seq)
        ya = dsa_attention(qa, qi, wi, kka, vva, kki, batch, seq)
        g_sub2 = jnp.tile(diff_subln[layer], 2).reshape(1, LANES)
        yb = diff_attention(qb, kb, vb, diff_lambda[layer], g_sub2, lam_init, batch, seq)
        yc = moba_attention(qc, kc, vc, batch, seq)
        j = layer // 2
        wo = w_out[layer].astype(BF16)
        gf = g_ffn[layer].reshape(1, d)
        if layer % 2 == 0:
            x1, h = outproj(xf, ya, yb, yc, wo, gf, mod, seq)
            xf = ffn_dense(x1, h, mod, w_ff_gate[j].astype(BF16), w_ff_up[j].astype(BF16),
                           w_ff_down[j].astype(BF16), seq)
        else:
            wr = jnp.concatenate([w_router[j], jnp.zeros((d, LANES - N_EXPERTS), F32)], axis=1)
            x1, h, cw = outproj(xf, ya, yb, yc, wo, gf, mod, seq, w_router=wr)
            xf = ffn_moe(x1, h, cw, mod, w_exp_gate[j].astype(BF16), w_exp_up[j].astype(BF16),
                         w_exp_down[j].astype(BF16), seq)
    return final_norm(xf, g_final.reshape(1, d)).reshape(batch, seq, d)
```

```python
import functools
import math

import jax
import jax.numpy as jnp
import numpy as np
from jax import lax
from jax.experimental import pallas as pl
from jax.experimental.pallas import tpu as pltpu
from jax.experimental.pallas import tpu_sc as plsc

F32 = jnp.float32
BF16 = jnp.bfloat16

HEAD_DIM = 64
A_HEADS = 6
IDX_HEADS = 4
B_HEADS = 4
B_QK_DIM = 32
C_HEADS = 6
DSA_TOPK_MAX = 256
MOBA_BLOCK = 256
MOBA_TOPK = 3
ROPE_THETA = 500000.0
ROPE_FRACTION = 4
SUBLN_EPS = 1e-5
EPS = 1e-6
N_EXPERTS = 8

LANES = 128
NEG = -1e30
INT_MIN = -2 ** 31
VMEM_LIMIT = 48 * 1024 * 1024
LOG2E = math.log2(math.e)

_G_QA, _G_KKA, _G_VVA, _G_QI, _G_KKI, _G_WI = (0, 384), (384, 512), (512, 640), (640, 896), (896, 1024), (1024, 1152)
_G_QB, _G_KB, _G_VB = (1152, 1408), (1408, 1664), (1664, 1920)
_G_QC, _G_KC, _G_VC = (1920, 2304), (2304, 2688), (2688, 3072)
D_IN_PAD = 3072


def _params(*sem):
    return pltpu.CompilerParams(dimension_semantics=sem, vmem_limit_bytes=VMEM_LIMIT)


def _dot(a, b):
    return jnp.dot(a, b, preferred_element_type=F32)


def _dot_nt(a, b, precision=None):
    return lax.dot_general(a, b, (((1,), (1,)), ((), ())), preferred_element_type=F32, precision=precision)


def _adaln_kernel(c_ref, w_ref, b_ref, o_ref):
    c = c_ref[...]
    c_act = c * (1.0 / (1.0 + jnp.exp(-c)))
    o_ref[...] = jnp.dot(c_act, w_ref[...], preferred_element_type=F32,
                         precision=lax.Precision.HIGHEST) + b_ref[...]


def adaln_mod(c, w_ada, b_ada, tn=1536):
    depth, d, n = w_ada.shape
    b = c.shape[0]
    return pl.pallas_call(
        _adaln_kernel,
        grid=(depth, n // tn),
        in_specs=[pl.BlockSpec((b, d), lambda l, j: (0, 0)),
                  pl.BlockSpec((None, d, tn), lambda l, j: (l, 0, j)),
                  pl.BlockSpec((None, 1, tn), lambda l, j: (l, 0, j))],
        out_specs=pl.BlockSpec((None, b, tn), lambda l, j: (l, 0, j)),
        out_shape=jax.ShapeDtypeStruct((depth, b, n), F32),
        compiler_params=_params("parallel", "parallel"),
        name="adaln_mod",
    )(c, w_ada, b_ada.reshape(depth, 1, n))


def _norm_mod(x, g, shift, scale, eps=EPS):
    y = x * lax.rsqrt(jnp.mean(x * x, axis=-1, keepdims=True) + eps)
    return (y * g) * (1.0 + scale) + shift


def _rope_store(acc, o_ref, cos, sa, sb, half):
    for j in range(acc.shape[1] // LANES):
        a = acc[:, j * LANES:(j + 1) * LANES]
        r = a * cos + pltpu.roll(a, half, 1) * sb + pltpu.roll(a, LANES - half, 1) * sa
        o_ref[:, j * LANES:(j + 1) * LANES] = r.astype(o_ref.dtype)


def _inproj_kernel(x_ref, g_ref, mod_ref, c64_ref, sa64_ref, sb64_ref, c32_ref, sa32_ref, sb32_ref, w_ref,
                   qa_ref, kka_ref, vva_ref, qi_ref, kki_ref, wi_ref,
                   qb_ref, kb_ref, vb_ref, qc_ref, kc_ref, vc_ref):
    h = _norm_mod(x_ref[...], g_ref[...], mod_ref[0:1, :], mod_ref[1:2, :]).astype(BF16)
    c64, sa64, sb64 = c64_ref[...], sa64_ref[...], sb64_ref[...]
    c32, sa32, sb32 = c32_ref[...], sa32_ref[...], sb32_ref[...]

    def proj(cols):
        return _dot(h, w_ref[:, cols[0]:cols[1]])

    qk_scale = HEAD_DIM ** -0.5 * LOG2E
    _rope_store(proj(_G_QA), qa_ref, c64 * qk_scale, sa64 * qk_scale, sb64 * qk_scale, 8)
    _rope_store(proj(_G_KKA), kka_ref, c64, sa64, sb64, 8)
    vva_ref[...] = proj(_G_VVA).astype(vva_ref.dtype)
    _rope_store(proj(_G_QI), qi_ref, c64, sa64, sb64, 8)
    _rope_store(proj(_G_KKI), kki_ref, c64, sa64, sb64, 8)
    wi_ref[...] = proj(_G_WI) * (IDX_HEADS ** -0.5 * HEAD_DIM ** -0.5)
    b_scale = B_QK_DIM ** -0.5 * LOG2E
    _rope_store(proj(_G_QB), qb_ref, c32 * b_scale, sa32 * b_scale, sb32 * b_scale, 4)
    _rope_store(proj(_G_KB), kb_ref, c32, sa32, sb32, 4)
    vb_ref[...] = proj(_G_VB).astype(vb_ref.dtype)
    _rope_store(proj(_G_QC), qc_ref, c64 * qk_scale, sa64 * qk_scale, sb64 * qk_scale, 8)
    _rope_store(proj(_G_KC), kc_ref, c64, sa64, sb64, 8)
    vc_ref[...] = proj(_G_VC).astype(vc_ref.dtype)


def inproj(x, g, mod, tabs, w_pad, seq, tm=512):
    t, d = x.shape
    tiles_per_batch = seq // tm
    row = lambda i: (i, 0)
    widths = [384, 128, 128, 256, 128, 128, 256, 256, 256, 384, 384, 384]
    dtypes = [BF16, BF16, BF16, BF16, BF16, F32, BF16, BF16, BF16, BF16, BF16, BF16]
    return pl.pallas_call(
        _inproj_kernel,
        grid=(t // tm,),
        in_specs=[pl.BlockSpec((tm, d), row),
                  pl.BlockSpec((1, d), lambda i: (0, 0)),
                  pl.BlockSpec((None, 6, d), lambda i: (i // tiles_per_batch, 0, 0))]
                 + [pl.BlockSpec((tm, LANES), row)] * 6
                 + [pl.BlockSpec((d, D_IN_PAD), lambda i: (0, 0))],
        out_specs=[pl.BlockSpec((tm, w), row) for w in widths],
        out_shape=[jax.ShapeDtypeStruct((t, w), dt) for w, dt in zip(widths, dtypes)],
        compiler_params=_params("parallel"),
        name="inproj",
    )(x, g, mod, *tabs, w_pad)


def _init_stats(m_ref, l_ref, acc_ref):
    m_ref[...] = jnp.full(m_ref.shape, -jnp.inf, F32)
    l_ref[...] = jnp.zeros(l_ref.shape, F32)
    acc_ref[...] = jnp.zeros(acc_ref.shape, F32)


def _key_to_f32(k):
    return lax.bitcast_convert_type(jnp.where(k >= 0, k, k ^ 0x7FFFFFFF), F32)


def dsa_attention(qa, qi, wi, kka, vva, kki, batch, seq, tq=256):
    t = qa.shape[0]
    nq = seq // tq
    k_top = min(DSA_TOPK_MAX, seq // 4)
    qrow = lambda b, i: (b * nq + i, 0)
    full = lambda b, i: (b, 0)
    return pl.pallas_call(
        functools.partial(_dsa_kernel_t, tq=tq, k_top=k_top, seq=seq),
        grid=(batch, nq),
        in_specs=[pl.BlockSpec((tq, 384), qrow), pl.BlockSpec((tq, 256), qrow), pl.BlockSpec((tq, LANES), qrow),
                  pl.BlockSpec((seq, LANES), full), pl.BlockSpec((seq, LANES), full),
                  pl.BlockSpec((seq, LANES), full)],
        out_specs=pl.BlockSpec((tq, 384), qrow),
        out_shape=jax.ShapeDtypeStruct((t, 384), BF16),
        scratch_shapes=[pltpu.VMEM((seq, tq), F32), pltpu.VMEM((1, LANES, seq), BF16),
                        pltpu.VMEM((3, 2 * tq, LANES), BF16),
                        pltpu.VMEM((3, 1, 2 * tq), F32), pltpu.VMEM((3, 1, 2 * tq), F32),
                        pltpu.VMEM((3, LANES, 2 * tq), F32)],
        compiler_params=_params("parallel", "arbitrary"),
        name="dsa_attention",
    )(qa, qi, wi, kka, vva, kki)


def diff_attention(qb, kb, vb, lam_vec, g_sub2, lam_init, batch, seq, tq=256):
    t = qb.shape[0]
    nq = seq // tq
    qrow = lambda b, i: (b * nq + i, 0)
    full = lambda b, i: (b, 0)
    const = lambda b, i: (0, 0)
    return pl.pallas_call(
        functools.partial(_diff_kernel_t, tq=tq, lam_init=lam_init, seq=seq),
        grid=(batch, nq),
        in_specs=[pl.BlockSpec((tq, 256), qrow), pl.BlockSpec((seq, 256), full), pl.BlockSpec((seq, 256), full),
                  pl.BlockSpec((4, B_QK_DIM), const), pl.BlockSpec((1, LANES), const)],
        out_specs=pl.BlockSpec((tq, 256), qrow),
        out_shape=jax.ShapeDtypeStruct((t, 256), BF16),
        scratch_shapes=[pltpu.VMEM((2, LANES, seq), BF16), pltpu.VMEM((2, 4 * tq, LANES), BF16),
                        pltpu.VMEM((2, 1, 4 * tq), F32), pltpu.VMEM((2, 1, 4 * tq), F32),
                        pltpu.VMEM((2, LANES, 4 * tq), F32)],
        compiler_params=_params("parallel", "arbitrary"),
        name="diff_attention",
    )(qb, kb, vb, lam_vec, g_sub2)


def moba_attention(qc, kc, vc, batch, seq):
    tq = MOBA_BLOCK
    t = qc.shape[0]
    nb = seq // tq
    n_sel = min(MOBA_TOPK, nb - 1)
    nbp = 8
    assert seq % tq == 0 and nb <= nbp
    qrow = lambda b, i: (b * nb + i, 0)
    full = lambda b, i: (b, 0)
    return pl.pallas_call(
        functools.partial(_moba_kernel_t, tq=tq, nb=nb, n_sel=n_sel, seq=seq),
        grid=(batch, nb),
        in_specs=[pl.BlockSpec((tq, 384), qrow), pl.BlockSpec((seq, 384), full), pl.BlockSpec((seq, 384), full)],
        out_specs=pl.BlockSpec((tq, 384), qrow),
        out_shape=jax.ShapeDtypeStruct((t, 384), BF16),
        scratch_shapes=[pltpu.VMEM((nbp, 384), F32), pltpu.VMEM((3, LANES, seq), BF16),
                        pltpu.VMEM((3, 2 * tq, LANES), BF16), pltpu.VMEM((3, nbp, 2 * tq), F32),
                        pltpu.VMEM((3, 1, 2 * tq), F32), pltpu.VMEM((3, 1, 2 * tq), F32),
                        pltpu.VMEM((3, LANES, 2 * tq), F32)],
        compiler_params=_params("parallel", "arbitrary"),
        name="moba_attention",
    )(qc, kc, vc)


def _attend_t(k, q_ref, g, v_t, m_ref, l_ref, acc_ref, mask=None):
    for j in range(q_ref.shape[1] // LANES):
        cols = slice(j * LANES, (j + 1) * LANES)
        s = _dot_nt(k, q_ref[g, cols, :])
        if mask is not None:
            s = mask(j, s)
        m_prev = m_ref[g, :, cols]
        m_new = jnp.maximum(m_prev, jnp.max(s, axis=0, keepdims=True))
        alpha = jnp.exp2(m_prev - m_new)
        p = jnp.exp2(s - m_new)
        l_ref[g, :, cols] = alpha * l_ref[g, :, cols] + jnp.sum(p, axis=0, keepdims=True)
        acc_ref[g, :, cols] = alpha * acc_ref[g, :, cols] + _dot(v_t, p.astype(BF16))
        m_ref[g, :, cols] = m_new


def _causal_mask_t(tk, tq):
    key = lax.broadcasted_iota(jnp.int32, (tk, LANES), 0)
    qry = lax.broadcasted_iota(jnp.int32, (tk, LANES), 1)

    def mask(j, s):
        return jnp.where(key <= qry + (j * LANES) % tq, s, NEG)
    return mask


def _transpose_values(v_ref, vt_ref, seq, tk):
    for g in range(vt_ref.shape[0]):
        for n in range(seq // tk):
            blk = v_ref[n * tk:(n + 1) * tk, g * LANES:(g + 1) * LANES].astype(F32)
            vt_ref[g, :, n * tk:(n + 1) * tk] = blk.T.astype(vt_ref.dtype)


def _head_pair_out(acc_t, l, tq):
    even = acc_t[0:HEAD_DIM, 0:tq] / l[:, 0:tq]
    odd = acc_t[HEAD_DIM:LANES, tq:2 * tq] / l[:, tq:2 * tq]
    return jnp.concatenate([even, odd], axis=0).T


def _dsa_kernel_t(qa_ref, qi_ref, wiq_ref, kka_ref, vva_ref, kki_ref, o_ref,
                  sc_ref, vt_ref, qs_ref, m_ref, l_ref, acc_ref, *, tq, k_top, seq):
    i = pl.program_id(1)
    nk = i + 1
    t0 = i * tq
    lo = lax.broadcasted_iota(jnp.int32, (tq, LANES), 1) < HEAD_DIM
    zero_b = jnp.zeros((tq, LANES), BF16)

    @pl.when(i == 0)
    def _():
        _transpose_values(vva_ref, vt_ref, seq, tq)

    def stack_heads(q):
        out = []
        for g in range(q.shape[1] // LANES):
            qg = q[:, g * LANES:(g + 1) * LANES]
            out += [jnp.where(lo, qg, zero_b), jnp.where(lo, zero_b, qg)]
        return out

    qa_stack = stack_heads(qa_ref[...])
    for g in range(3):
        qs_ref[g, 0:tq, :] = qa_stack[2 * g]
        qs_ref[g, tq:2 * tq, :] = qa_stack[2 * g + 1]
    qi_stack = jnp.concatenate(stack_heads(qi_ref[...]), axis=0)
    wi_t = wiq_ref[...].T

    def rows(c):
        return pl.ds(pl.multiple_of(c * tq, tq), tq)

    key_pos = lax.broadcasted_iota(jnp.int32, (tq, tq), 0)
    qry_pos = lax.broadcasted_iota(jnp.int32, (tq, tq), 1)

    def idx_body(c, carry):
        r = jnp.maximum(_dot_nt(kki_ref[rows(c), :], qi_stack), 0.0)
        s = wi_t[0:1, :] * r[:, 0:tq]
        for h in range(1, IDX_HEADS):
            s = s + wi_t[h:h + 1, :] * r[:, h * tq:(h + 1) * tq]
        causal = (c * tq + key_pos) <= (t0 + qry_pos)
        sc_ref[rows(c), :] = jnp.where(causal, s, -jnp.inf)
        return carry

    lax.fori_loop(0, nk, idx_body, 0)

    def count_ge(cand):
        def body(c, acc):
            ge = jnp.where(sc_ref[rows(c), :] >= cand, 1.0, 0.0)
            return acc + jnp.sum(ge.reshape(tq // 8, 8, tq), axis=0)
        acc = lax.fori_loop(0, nk, body, jnp.zeros((8, tq), F32))
        return jnp.sum(acc, axis=0, keepdims=True)

    kf = float(k_top)

    def search():
        key0 = jnp.where(count_ge(jnp.zeros((1, tq), F32)) >= kf, 0, INT_MIN).astype(jnp.int32)

        def bisect(b, key):
            trial = key | lax.shift_left(jnp.int32(1), 30 - b)
            return jnp.where(count_ge(_key_to_f32(trial)) >= kf, trial, key)
        return lax.fori_loop(0, 31, bisect, key0)

    key = lax.cond(t0 + tq <= k_top, lambda: jnp.full((1, tq), INT_MIN, jnp.int32), search)
    thr = _key_to_f32(key)
    thr_next = _key_to_f32(key + 1)
    need = kf - count_ge(thr_next)
    all_sel = (t0 + lax.broadcasted_iota(jnp.int32, (1, tq), 1)) < k_top
    lower = (qry_pos <= key_pos).astype(BF16)

    _init_stats(m_ref, l_ref, acc_ref)

    def att_body(c, tie_run):
        s_idx = sc_ref[rows(c), :]
        ge = s_idx >= thr
        gt = s_idx >= thr_next
        tie = jnp.where(gt, 0.0, jnp.where(ge, 1.0, 0.0))
        prefix = _dot(lower, tie.astype(BF16)) + tie_run
        take = jnp.where(gt, 1.0, jnp.where(prefix <= need, tie, 0.0))
        take = jnp.where(all_sel, 1.0, take)
        take = jnp.where(s_idx > -jnp.inf, take, 0.0)

        def mask(j, s):
            q0 = (j * LANES) % tq
            return jnp.where(take[:, q0:q0 + LANES] > 0.5, s, NEG)

        kk = kka_ref[rows(c), :]
        vt = vt_ref[0, :, rows(c)]
        for g in range(3):
            _attend_t(kk, qs_ref, g, vt, m_ref, l_ref, acc_ref, mask)
        return tie_run + jnp.sum(tie, axis=0, keepdims=True)

    lax.fori_loop(0, nk, att_body, jnp.zeros((1, tq), F32))

    for g in range(3):
        o_ref[:, g * LANES:(g + 1) * LANES] = _head_pair_out(acc_ref[g], l_ref[g], tq).astype(o_ref.dtype)


def _diff_kernel_t(qb_ref, kb_ref, vb_ref, lam_ref, gsub_ref, o_ref,
                   vt_ref, qs_ref, m_ref, l_ref, acc_ref, *, tq, lam_init, seq):
    i = pl.program_id(1)
    lane = lax.broadcasted_iota(jnp.int32, (tq, LANES), 1)
    zero_b = jnp.zeros((tq, LANES), BF16)

    @pl.when(i == 0)
    def _():
        _transpose_values(vb_ref, vt_ref, seq, tq)

    qb = qb_ref[...]
    for g in range(2):
        qg = qb[:, g * LANES:(g + 1) * LANES]
        for j in range(4):
            qs_ref[g, j * tq:(j + 1) * tq, :] = jnp.where(lane // B_QK_DIM == j, qg, zero_b)
    _init_stats(m_ref, l_ref, acc_ref)

    def rows(c):
        return pl.ds(pl.multiple_of(c * tq, tq), tq)

    def step(c, masked):
        mask = _causal_mask_t(tq, tq) if masked else None
        for g in range(2):
            _attend_t(kb_ref[rows(c), g * LANES:(g + 1) * LANES], qs_ref, g, vt_ref[g, :, rows(c)],
                      m_ref, l_ref, acc_ref, mask)

    def body(c, carry):
        step(c, False)
        return carry

    lax.fori_loop(0, i, body, 0)
    step(i, True)

    lv = lam_ref[...]
    lam = (jnp.exp(jnp.sum(lv[0:1] * lv[1:2], axis=1, keepdims=True))
           - jnp.exp(jnp.sum(lv[2:3] * lv[3:4], axis=1, keepdims=True)) + lam_init)
    gsub = gsub_ref[...]
    for g in range(2):
        acc_t, l = acc_ref[g], l_ref[g]

        def prob(r0, j):
            return acc_t[r0:r0 + HEAD_DIM, j * tq:(j + 1) * tq] / l[:, j * tq:(j + 1) * tq]

        halves = []
        for r0, j in ((0, 0), (HEAD_DIM, 2)):
            o = prob(r0, j) - lam * prob(r0, j + 1)
            ms = jnp.mean(o * o, axis=0, keepdims=True)
            halves.append(o * lax.rsqrt(ms + SUBLN_EPS))
        y = (jnp.concatenate(halves, axis=0).T * gsub) * (1.0 - lam_init)
        o_ref[:, g * LANES:(g + 1) * LANES] = y.astype(o_ref.dtype)


def _moba_kernel_t(qc_ref, kc_ref, vc_ref, o_ref, kmean_ref, vt_ref, qs_ref, bias_ref, m_ref, l_ref, acc_ref,
                   *, tq, nb, n_sel, seq):
    i = pl.program_id(1)
    nbp = kmean_ref.shape[0]
    lane = lax.broadcasted_iota(jnp.int32, (tq, LANES), 1)
    lo = lane < HEAD_DIM
    zero_b = jnp.zeros((tq, LANES), BF16)

    @pl.when(i == 0)
    def _():
        _transpose_values(vc_ref, vt_ref, seq, tq)
        kmean_ref[...] = jnp.zeros(kmean_ref.shape, F32)
        for n in range(nb):
            kblk = kc_ref[n * tq:(n + 1) * tq, :].astype(F32)
            kmean_ref[n:n + 1, :] = jnp.mean(kblk, axis=0, keepdims=True)

    qc = qc_ref[...]
    sub = lax.broadcasted_iota(jnp.int32, (nbp, 2 * tq), 0)
    past = sub < i
    for g in range(3):
        qg = qc[:, g * LANES:(g + 1) * LANES]
        q2 = jnp.concatenate([jnp.where(lo, qg, zero_b), jnp.where(lo, zero_b, qg)], axis=0)
        gt = _dot_nt(kmean_ref[:, g * LANES:(g + 1) * LANES], q2.astype(F32), precision=lax.Precision.HIGHEST)
        rows_ = []
        for n in range(nbp):
            gn = gt[n:n + 1, :]
            beats = jnp.where(sub < n, jnp.where(gt >= gn, 1.0, 0.0), jnp.where(gt > gn, 1.0, 0.0))
            beats = jnp.where(sub == n, 0.0, jnp.where(past, beats, 0.0))
            rank = jnp.sum(beats, axis=0, keepdims=True)
            rows_.append(jnp.where(rank < n_sel, 0.0, NEG))
        bias_ref[g] = jnp.where(past, jnp.concatenate(rows_, axis=0), NEG)
        qs_ref[g] = q2
    _init_stats(m_ref, l_ref, acc_ref)

    def rows(c):
        return pl.ds(pl.multiple_of(c * tq, tq), tq)

    def body(c, carry):
        for g in range(3):
            bias = bias_ref[g, pl.ds(c, 1), :]

            def mask(j, s):
                return s + bias[:, j * LANES:(j + 1) * LANES]

            _attend_t(kc_ref[rows(c), g * LANES:(g + 1) * LANES], qs_ref, g, vt_ref[g, :, rows(c)],
                      m_ref, l_ref, acc_ref, mask)
        return carry

    lax.fori_loop(0, i, body, 0)

    causal = _causal_mask_t(tq, tq)
    for g in range(3):
        _attend_t(kc_ref[rows(i), g * LANES:(g + 1) * LANES], qs_ref, g, vt_ref[g, :, rows(i)],
                  m_ref, l_ref, acc_ref, causal)
        o_ref[:, g * LANES:(g + 1) * LANES] = _head_pair_out(acc_ref[g], l_ref[g], tq).astype(o_ref.dtype)


def _route(h, wr_ref, cw_ref):
    logits = jnp.dot(h, wr_ref[...], preferred_element_type=F32, precision=lax.Precision.HIGHEST)
    lane = lax.broadcasted_iota(jnp.int32, logits.shape, 1)
    lg = jnp.where(lane < N_EXPERTS, logits, -jnp.inf)
    v0 = jnp.max(lg, axis=1, keepdims=True)
    i0 = jnp.min(jnp.where(lg == v0, lane, LANES), axis=1, keepdims=True)
    lg1 = jnp.where(lane == i0, -jnp.inf, lg)
    v1 = jnp.max(lg1, axis=1, keepdims=True)
    i1 = jnp.min(jnp.where(lg1 == v1, lane, LANES), axis=1, keepdims=True)
    e1 = jnp.exp(v1 - v0)
    w0 = 1.0 / (1.0 + e1)
    cw_ref[...] = (jnp.where(lane == 0, i0.astype(F32), 0.0) + jnp.where(lane == 1, i1.astype(F32), 0.0)
                   + jnp.where(lane == 2, w0, 0.0) + jnp.where(lane == 3, e1 * w0, 0.0))


def _outproj_kernel(x_ref, ya_ref, yb_ref, yc_ref, wo_ref, g_ref, mod_ref, *rest, with_router):
    if with_router:
        wr_ref, x1_ref, h_ref, cw_ref = rest
    else:
        x1_ref, h_ref = rest
    y = (_dot(ya_ref[...], wo_ref[0:384, :]) + _dot(yb_ref[...], wo_ref[384:640, :])
         + _dot(yc_ref[...], wo_ref[640:1024, :]))
    x1 = x_ref[...] + mod_ref[2:3, :] * y
    x1_ref[...] = x1
    h = _norm_mod(x1, g_ref[...], mod_ref[3:4, :], mod_ref[4:5, :])
    if with_router:
        h_ref[...] = _pack_rows(h)
        _route(h, wr_ref, cw_ref)
    else:
        h_ref[...] = h.astype(BF16)


def outproj(x, ya, yb, yc, wo, g_ffn, mod, seq, w_router=None, tm=512):
    t, d = x.shape
    tiles_per_batch = seq // tm
    row = lambda i: (i, 0)
    const = lambda i: (0, 0)
    with_router = w_router is not None
    in_specs = [pl.BlockSpec((tm, d), row), pl.BlockSpec((tm, 384), row), pl.BlockSpec((tm, 256), row),
                pl.BlockSpec((tm, 384), row), pl.BlockSpec((d, d), const), pl.BlockSpec((1, d), const),
                pl.BlockSpec((None, 6, d), lambda i: (i // tiles_per_batch, 0, 0))]
    out_specs = [pl.BlockSpec((tm, d), row), pl.BlockSpec((tm, d), row)]
    out_shape = [jax.ShapeDtypeStruct((t, d), F32), jax.ShapeDtypeStruct((t, d), BF16)]
    args = [x, ya, yb, yc, wo, g_ffn, mod]
    if with_router:
        in_specs.append(pl.BlockSpec((d, LANES), const))
        out_specs[1] = pl.BlockSpec((tm, d // 2), row)
        out_shape[1] = jax.ShapeDtypeStruct((t, d // 2), jnp.int32)
        out_specs.append(pl.BlockSpec((tm, LANES), row))
        out_shape.append(jax.ShapeDtypeStruct((t, LANES), F32))
        args.append(w_router)
    return pl.pallas_call(
        functools.partial(_outproj_kernel, with_router=with_router),
        grid=(t // tm,), in_specs=in_specs, out_specs=out_specs, out_shape=out_shape,
        compiler_params=_params("parallel"),
        name="outproj_router" if with_router else "outproj",
    )(*args)


def _swiglu_partial(h, wg_ref, wu_ref, wd_ref):
    a = _dot(h, wg_ref[...])
    u = _dot(h, wu_ref[...])
    act = (a * (1.0 / (1.0 + jnp.exp(-a)))) * u
    return _dot(act.astype(BF16), wd_ref[...])


def _ffn_kernel(x1_ref, h_ref, mod_ref, wg_ref, wu_ref, wd_ref, o_ref, acc_ref):
    f = pl.program_id(1)

    @pl.when(f == 0)
    def _():
        acc_ref[...] = jnp.zeros(acc_ref.shape, F32)

    acc_ref[...] += _swiglu_partial(h_ref[...], wg_ref, wu_ref, wd_ref)

    @pl.when(f == pl.num_programs(1) - 1)
    def _():
        o_ref[...] = x1_ref[...] + mod_ref[5:6, :] * acc_ref[...]


def ffn_dense(x1, h, mod, wg, wu, wd, seq, tm=512, tf=1408):
    t, d = x1.shape
    dff = wg.shape[1]
    tiles_per_batch = seq // tm
    row = lambda i, f: (i, 0)
    return pl.pallas_call(
        _ffn_kernel,
        grid=(t // tm, dff // tf),
        in_specs=[pl.BlockSpec((tm, d), row), pl.BlockSpec((tm, d), row),
                  pl.BlockSpec((None, 6, d), lambda i, f: (i // tiles_per_batch, 0, 0)),
                  pl.BlockSpec((d, tf), lambda i, f: (0, f)), pl.BlockSpec((d, tf), lambda i, f: (0, f)),
                  pl.BlockSpec((tf, d), lambda i, f: (f, 0))],
        out_specs=pl.BlockSpec((tm, d), row),
        out_shape=jax.ShapeDtypeStruct((t, d), F32),
        scratch_shapes=[pltpu.VMEM((tm, d), F32)],
        compiler_params=_params("parallel", "arbitrary"),
        name="ffn_dense",
    )(x1, h, mod, wg, wu, wd)


MOE_TILE = 512
SC_CORES, SC_SUBCORES = 2, 16
SC_ROWS = 64
HI16 = -65536


def _pack_rows(x):
    c = x.shape[1] // 2
    bits = lax.bitcast_convert_type(x.astype(jnp.bfloat16).astype(F32), jnp.int32)
    return lax.shift_right_logical(bits[:, :c], jnp.int32(16)) | (bits[:, c:] & jnp.int32(HI16))


def _unpack_rows(w):
    lo = lax.bitcast_convert_type(lax.shift_left(w, jnp.int32(16)), F32)
    hi = lax.bitcast_convert_type(w & jnp.int32(HI16), F32)
    return jnp.concatenate([lo, hi], axis=1)


def sc_gather_rows(table, idx):
    d = table.shape[1]
    b = idx.shape[0]
    per_worker = b // (SC_CORES * SC_SUBCORES)
    assert per_worker * SC_CORES * SC_SUBCORES == b and per_worker % SC_ROWS == 0
    mesh = plsc.VectorSubcoreMesh(core_axis_name="c", subcore_axis_name="s")

    @functools.partial(
        pl.kernel, mesh=mesh, out_type=jax.ShapeDtypeStruct((b, d), table.dtype),
        scratch_types=[pltpu.VMEM((SC_ROWS,), jnp.int32), pltpu.VMEM((SC_ROWS, d), table.dtype),
                       pltpu.SemaphoreType.DMA],
        name="sc_gather_rows")
    def gather(table_hbm, idx_hbm, out_hbm, idx_v, rows_v, sem):
        base = (lax.axis_index("s") * SC_CORES + lax.axis_index("c")) * per_worker

        @pl.loop(0, per_worker // SC_ROWS)
        def _(step):
            off = pl.multiple_of(base + step * SC_ROWS, SC_ROWS)
            pltpu.sync_copy(idx_hbm.at[pl.ds(off, SC_ROWS)], idx_v)
            pltpu.async_copy(table_hbm.at[idx_v], rows_v, sem).wait()
            pltpu.sync_copy(rows_v, out_hbm.at[pl.ds(off, SC_ROWS)])

    return gather(table, idx)


def _moe_layout(e0, e1, t, n_exp):
    experts = jnp.arange(n_exp, dtype=jnp.int32)
    routed = ((e0[:, None] == experts) | (e1[:, None] == experts)).astype(jnp.int32)
    csum = jnp.cumsum(routed, axis=0)
    padded = (csum[-1] + MOE_TILE - 1) // MOE_TILE * MOE_TILE
    ends = jnp.cumsum(padded)
    pos = (ends - padded)[None, :] + csum - routed
    pos0 = jnp.take_along_axis(pos, e0[:, None], axis=1)[:, 0]
    pos1 = jnp.take_along_axis(pos, e1[:, None], axis=1)[:, 0]
    n_rows = 2 * t + n_exp * MOE_TILE
    tok = jnp.arange(t, dtype=jnp.int32)
    src = jnp.zeros((n_rows,), jnp.int32).at[pos0].set(tok).at[pos1].set(tok)
    n_tiles = n_rows // MOE_TILE
    n_valid = ends[-1] // MOE_TILE
    tile_expert = jnp.sum(jnp.arange(n_tiles)[:, None] * MOE_TILE >= ends[None, :], axis=1)
    tile_expert = tile_expert[jnp.minimum(jnp.arange(n_tiles), n_valid - 1)].astype(jnp.int32)
    return pos0, pos1, src, tile_expert, n_valid.astype(jnp.int32).reshape(1)


def _experts_packed_kernel(te_ref, nv_ref, x_ref, wg_ref, wu_ref, wd_ref, o_ref, acc_ref):
    j, f = pl.program_id(0), pl.program_id(1)
    valid = j < nv_ref[0]

    @pl.when(valid & (f == 0))
    def _():
        acc_ref[...] = jnp.zeros(acc_ref.shape, F32)

    @pl.when(valid)
    def _():
        acc_ref[...] += _swiglu_partial(_unpack_rows(x_ref[...]).astype(BF16), wg_ref, wu_ref, wd_ref)

    @pl.when(valid & (f == pl.num_programs(1) - 1))
    def _():
        o_ref[...] = _pack_rows(acc_ref[...])


def moe_experts_packed(x_sorted, tile_expert, n_valid, wg, wu, wd, tf=1408):
    n_rows, half = x_sorted.shape
    d = 2 * half
    nf = wg.shape[2] // tf

    def tile(j, nv):
        return jnp.minimum(j, nv[0] - 1)

    def fstep(j, f, nv):
        return jnp.where(j < nv[0], f, nf - 1)

    return pl.pallas_call(
        _experts_packed_kernel,
        grid_spec=pltpu.PrefetchScalarGridSpec(
            num_scalar_prefetch=2, grid=(n_rows // MOE_TILE, nf),
            in_specs=[pl.BlockSpec((MOE_TILE, half), lambda j, f, te, nv: (tile(j, nv), 0)),
                      pl.BlockSpec((None, d, tf), lambda j, f, te, nv: (te[j], 0, fstep(j, f, nv))),
                      pl.BlockSpec((None, d, tf), lambda j, f, te, nv: (te[j], 0, fstep(j, f, nv))),
                      pl.BlockSpec((None, tf, d), lambda j, f, te, nv: (te[j], fstep(j, f, nv), 0))],
            out_specs=pl.BlockSpec((MOE_TILE, half), lambda j, f, te, nv: (tile(j, nv), 0)),
            scratch_shapes=[pltpu.VMEM((MOE_TILE, d), F32)]),
        out_shape=jax.ShapeDtypeStruct((n_rows, half), jnp.int32),
        compiler_params=_params("arbitrary", "arbitrary"),
        name="moe_experts",
    )(tile_expert, n_valid, x_sorted, wg, wu, wd)


def _combine_packed_kernel(y0_ref, y1_ref, route_ref, x1_ref, mod_ref, o_ref):
    rt = route_ref[...]
    f = rt[:, 2:3] * _unpack_rows(y0_ref[...]) + rt[:, 3:4] * _unpack_rows(y1_ref[...])
    o_ref[...] = x1_ref[...] + mod_ref[5:6, :] * f


def moe_combine_packed(y_pairs, route, x1, mod, seq, tm=512):
    t, d = x1.shape
    nt = t // tm
    tiles_per_batch = seq // tm
    row = lambda i: (i, 0)
    return pl.pallas_call(
        _combine_packed_kernel,
        grid=(nt,),
        in_specs=[pl.BlockSpec((tm, d // 2), row), pl.BlockSpec((tm, d // 2), lambda i: (i + nt, 0)),
                  pl.BlockSpec((tm, LANES), row), pl.BlockSpec((tm, d), row),
                  pl.BlockSpec((None, 6, d), lambda i: (i // tiles_per_batch, 0, 0))],
        out_specs=pl.BlockSpec((tm, d), row),
        out_shape=jax.ShapeDtypeStruct((t, d), F32),
        compiler_params=_params("parallel"),
        name="moe_combine",
    )(y_pairs, y_pairs, route, x1, mod)


def ffn_moe_sc(x1, h_packed, route, mod, wg, wu, wd, seq):
    t = x1.shape[0]
    e0, e1 = route[:, 0].astype(jnp.int32), route[:, 1].astype(jnp.int32)
    pos0, pos1, src, tile_expert, n_valid = _moe_layout(e0, e1, t, wg.shape[0])
    h_sorted = sc_gather_rows(h_packed, src)
    y_sorted = moe_experts_packed(h_sorted, tile_expert, n_valid, wg, wu, wd)
    y_pairs = sc_gather_rows(y_sorted, jnp.concatenate([pos0, pos1]))
    return moe_combine_packed(y_pairs, route, x1, mod, seq)


def _final_norm_kernel(x_ref, g_ref, o_ref):
    x = x_ref[...]
    o_ref[...] = (x * lax.rsqrt(jnp.mean(x * x, axis=-1, keepdims=True) + EPS)) * g_ref[...]


def final_norm(x, g, tm=512):
    t, d = x.shape
    return pl.pallas_call(
        _final_norm_kernel,
        grid=(t // tm,),
        in_specs=[pl.BlockSpec((tm, d), lambda i: (i, 0)), pl.BlockSpec((1, d), lambda i: (0, 0))],
        out_specs=pl.BlockSpec((tm, d), lambda i: (i, 0)),
        out_shape=jax.ShapeDtypeStruct((t, d), F32),
        compiler_params=_params("parallel"),
        name="final_norm",
    )(x, g)


def _rope_tables(positions, dim):
    rot = dim // ROPE_FRACTION
    half = rot // 2
    inv = 1.0 / (ROPE_THETA ** (np.arange(0, rot, 2, dtype=np.float32) / rot))
    ang = positions.reshape(-1).astype(F32)[:, None] * jnp.asarray(inv, F32)
    cos, sin = jnp.cos(ang), jnp.sin(ang)
    t = ang.shape[0]
    ones = jnp.ones((t, dim - rot), F32)
    zeros = lambda w: jnp.zeros((t, w), F32)
    reps = LANES // dim
    c = jnp.tile(jnp.concatenate([cos, cos, ones], axis=1), (1, reps))
    sa = jnp.tile(jnp.concatenate([-sin, zeros(dim - half)], axis=1), (1, reps))
    sb = jnp.tile(jnp.concatenate([zeros(half), sin, zeros(dim - rot)], axis=1), (1, reps))
    return c, sa, sb


def _relayout_w_in(w):
    d = w.shape[0]
    pts = np.cumsum([0, 384, 64, 64, 256, 64, 4, 256, 256, 256, 384, 384, 384])
    (q_a, k_a, v_a, q_i, k_i, w_i, q_b, k_b, v_b, q_c, k_c, v_c) = [w[:, pts[j]:pts[j + 1]] for j in range(12)]
    w_i_pad = jnp.concatenate([w_i, jnp.zeros((d, LANES - IDX_HEADS), w.dtype)], axis=1)
    return jnp.concatenate([q_a, k_a, k_a, v_a, v_a, q_i, k_i, k_i, w_i_pad,
                            q_b, k_b, v_b, q_c, k_c, v_c], axis=1).astype(BF16)


def kernel(x, c, positions, w_in, w_out, diff_lambda, diff_subln, w_ada, b_ada, g_attn, g_ffn, w_ff_gate,
           w_ff_up, w_ff_down, w_router, w_exp_gate, w_exp_up, w_exp_down, g_final):
    batch, seq, d = x.shape
    depth = w_in.shape[0]
    t = batch * seq
    tabs = _rope_tables(positions, HEAD_DIM) + _rope_tables(positions, B_QK_DIM)
    mod_all = adaln_mod(c, w_ada, b_ada).reshape(depth, batch, 6, d)
    xf = x.reshape(t, d)
    for layer in range(depth):
        mod = mod_all[layer]
        lam_init = 0.8 - 0.6 * math.exp(-0.3 * layer)
        (qa, kka, vva, qi, kki, wi, qb, kb, vb, qc, kc, vc) = inproj(
            xf, g_attn[layer].reshape(1, d), mod, tabs, _relayout_w_in(w_in[layer]), seq)
        ya = dsa_attention(qa, qi, wi, kka, vva, kki, batch, seq)
        g_sub2 = jnp.tile(diff_subln[layer], 2).reshape(1, LANES)
        yb = diff_attention(qb, kb, vb, diff_lambda[layer], g_sub2, lam_init, batch, seq)
        yc = moba_attention(qc, kc, vc, batch, seq)
        j = layer // 2
        wo = w_out[layer].astype(BF16)
        gf = g_ffn[layer].reshape(1, d)
        if layer % 2 == 0:
            x1, h = outproj(xf, ya, yb, yc, wo, gf, mod, seq)
            xf = ffn_dense(x1, h, mod, w_ff_gate[j].astype(BF16), w_ff_up[j].astype(BF16),
                           w_ff_down[j].astype(BF16), seq)
        else:
            wr = jnp.concatenate([w_router[j], jnp.zeros((d, LANES - N_EXPERTS), F32)], axis=1)
            x1, h_packed, route = outproj(xf, ya, yb, yc, wo, gf, mod, seq, w_router=wr)
            xf = ffn_moe_sc(x1, h_packed, route, mod, w_exp_gate[j].astype(BF16), w_exp_up[j].astype(BF16),
                            w_exp_down[j].astype(BF16), seq)
    return final_norm(xf, g_final.reshape(1, d)).reshape(batch, seq, d)
```

```python
import functools
import math

import jax
import jax.numpy as jnp
import numpy as np
from jax import lax
from jax.experimental import pallas as pl
from jax.experimental.pallas import tpu as pltpu
from jax.experimental.pallas import tpu_sc as plsc

F32 = jnp.float32
BF16 = jnp.bfloat16

HEAD_DIM = 64
A_HEADS = 6
IDX_HEADS = 4
B_HEADS = 4
B_QK_DIM = 32
C_HEADS = 6
DSA_TOPK_MAX = 256
MOBA_BLOCK = 256
MOBA_TOPK = 3
ROPE_THETA = 500000.0
ROPE_FRACTION = 4
SUBLN_EPS = 1e-5
EPS = 1e-6
N_EXPERTS = 8

LANES = 128
NEG = -1e30
INT_MIN = -2 ** 31
VMEM_LIMIT = 48 * 1024 * 1024
LOG2E = math.log2(math.e)

_G_QA, _G_KKA, _G_VVA, _G_QI, _G_KKI, _G_WI = (0, 384), (384, 512), (512, 640), (640, 896), (896, 1024), (1024, 1152)
_G_QB, _G_KB, _G_VB = (1152, 1408), (1408, 1664), (1664, 1920)
_G_QC, _G_KC, _G_VC = (1920, 2304), (2304, 2688), (2688, 3072)
D_IN_PAD = 3072


def _params(*sem):
    return pltpu.CompilerParams(dimension_semantics=sem, vmem_limit_bytes=VMEM_LIMIT)


def _dot(a, b):
    return jnp.dot(a, b, preferred_element_type=F32)


def _dot_nt(a, b, precision=None):
    return lax.dot_general(a, b, (((1,), (1,)), ((), ())), preferred_element_type=F32, precision=precision)


def _adaln_kernel(c_ref, w_ref, b_ref, o_ref):
    c = c_ref[...]
    c_act = c * (1.0 / (1.0 + jnp.exp(-c)))
    o_ref[...] = jnp.dot(c_act, w_ref[...], preferred_element_type=F32,
                         precision=lax.Precision.HIGHEST) + b_ref[...]


def adaln_mod(c, w_ada, b_ada, tn=1536):
    depth, d, n = w_ada.shape
    b = c.shape[0]
    return pl.pallas_call(
        _adaln_kernel,
        grid=(depth, n // tn),
        in_specs=[pl.BlockSpec((b, d), lambda l, j: (0, 0)),
                  pl.BlockSpec((None, d, tn), lambda l, j: (l, 0, j)),
                  pl.BlockSpec((None, 1, tn), lambda l, j: (l, 0, j))],
        out_specs=pl.BlockSpec((None, b, tn), lambda l, j: (l, 0, j)),
        out_shape=jax.ShapeDtypeStruct((depth, b, n), F32),
        compiler_params=_params("parallel", "parallel"),
        name="adaln_mod",
    )(c, w_ada, b_ada.reshape(depth, 1, n))


def _norm_mod(x, g, shift, scale, eps=EPS):
    y = x * lax.rsqrt(jnp.mean(x * x, axis=-1, keepdims=True) + eps)
    return (y * g) * (1.0 + scale) + shift


def _rope_store(acc, o_ref, cos, sa, sb, half):
    for j in range(acc.shape[1] // LANES):
        a = acc[:, j * LANES:(j + 1) * LANES]
        r = a * cos + pltpu.roll(a, half, 1) * sb + pltpu.roll(a, LANES - half, 1) * sa
        o_ref[:, j * LANES:(j + 1) * LANES] = r.astype(o_ref.dtype)


def _inproj_kernel(x_ref, g_ref, mod_ref, c64_ref, sa64_ref, sb64_ref, c32_ref, sa32_ref, sb32_ref, w_ref,
                   qa_ref, kka_ref, vva_ref, qi_ref, kki_ref, wi_ref,
                   qb_ref, kb_ref, vb_ref, qc_ref, kc_ref, vc_ref):
    h = _norm_mod(x_ref[...], g_ref[...], mod_ref[0:1, :], mod_ref[1:2, :]).astype(BF16)
    c64, sa64, sb64 = c64_ref[...], sa64_ref[...], sb64_ref[...]
    c32, sa32, sb32 = c32_ref[...], sa32_ref[...], sb32_ref[...]

    def proj(cols):
        return _dot(h, w_ref[:, cols[0]:cols[1]])

    qk_scale = HEAD_DIM ** -0.5 * LOG2E
    _rope_store(proj(_G_QA), qa_ref, c64 * qk_scale, sa64 * qk_scale, sb64 * qk_scale, 8)
    _rope_store(proj(_G_KKA), kka_ref, c64, sa64, sb64, 8)
    vva_ref[...] = proj(_G_VVA).astype(vva_ref.dtype)
    _rope_store(proj(_G_QI), qi_ref, c64, sa64, sb64, 8)
    _rope_store(proj(_G_KKI), kki_ref, c64, sa64, sb64, 8)
    wi_ref[...] = proj(_G_WI) * (IDX_HEADS ** -0.5 * HEAD_DIM ** -0.5)
    b_scale = B_QK_DIM ** -0.5 * LOG2E
    _rope_store(proj(_G_QB), qb_ref, c32 * b_scale, sa32 * b_scale, sb32 * b_scale, 4)
    _rope_store(proj(_G_KB), kb_ref, c32, sa32, sb32, 4)
    vb_ref[...] = proj(_G_VB).astype(vb_ref.dtype)
    _rope_store(proj(_G_QC), qc_ref, c64 * qk_scale, sa64 * qk_scale, sb64 * qk_scale, 8)
    _rope_store(proj(_G_KC), kc_ref, c64, sa64, sb64, 8)
    vc_ref[...] = proj(_G_VC).astype(vc_ref.dtype)


def inproj(x, g, mod, tabs, w_pad, seq, tm=512):
    t, d = x.shape
    tiles_per_batch = seq // tm
    row = lambda i: (i, 0)
    widths = [384, 128, 128, 256, 128, 128, 256, 256, 256, 384, 384, 384]
    dtypes = [BF16, BF16, BF16, BF16, BF16, F32, BF16, BF16, BF16, BF16, BF16, BF16]
    return pl.pallas_call(
        _inproj_kernel,
        grid=(t // tm,),
        in_specs=[pl.BlockSpec((tm, d), row),
                  pl.BlockSpec((1, d), lambda i: (0, 0)),
                  pl.BlockSpec((None, 6, d), lambda i: (i // tiles_per_batch, 0, 0))]
                 + [pl.BlockSpec((tm, LANES), row)] * 6
                 + [pl.BlockSpec((d, D_IN_PAD), lambda i: (0, 0))],
        out_specs=[pl.BlockSpec((tm, w), row) for w in widths],
        out_shape=[jax.ShapeDtypeStruct((t, w), dt) for w, dt in zip(widths, dtypes)],
        compiler_params=_params("parallel"),
        name="inproj",
    )(x, g, mod, *tabs, w_pad)


def _init_stats(m_ref, l_ref, acc_ref):
    m_ref[...] = jnp.full(m_ref.shape, -jnp.inf, F32)
    l_ref[...] = jnp.zeros(l_ref.shape, F32)
    acc_ref[...] = jnp.zeros(acc_ref.shape, F32)


def _key_to_f32(k):
    return lax.bitcast_convert_type(jnp.where(k >= 0, k, k ^ 0x7FFFFFFF), F32)


def dsa_attention(qa, qi, wi, kka, vva, kki, batch, seq, tq=256):
    t = qa.shape[0]
    nq = seq // tq
    k_top = min(DSA_TOPK_MAX, seq // 4)
    qrow = lambda b, i: (b * nq + i, 0)
    full = lambda b, i: (b, 0)
    return pl.pallas_call(
        functools.partial(_dsa_kernel_t, tq=tq, k_top=k_top, seq=seq),
        grid=(batch, nq),
        in_specs=[pl.BlockSpec((tq, 384), qrow), pl.BlockSpec((tq, 256), qrow), pl.BlockSpec((tq, LANES), qrow),
                  pl.BlockSpec((seq, LANES), full), pl.BlockSpec((seq, LANES), full),
                  pl.BlockSpec((seq, LANES), full)],
        out_specs=pl.BlockSpec((tq, 384), qrow),
        out_shape=jax.ShapeDtypeStruct((t, 384), BF16),
        scratch_shapes=[pltpu.VMEM((seq, tq), F32), pltpu.VMEM((1, LANES, seq), BF16),
                        pltpu.VMEM((3, 2 * tq, LANES), BF16),
                        pltpu.VMEM((3, 1, 2 * tq), F32), pltpu.VMEM((3, 1, 2 * tq), F32),
                        pltpu.VMEM((3, LANES, 2 * tq), F32)],
        compiler_params=_params("parallel", "arbitrary"),
        name="dsa_attention",
    )(qa, qi, wi, kka, vva, kki)


def diff_attention(qb, kb, vb, lam_vec, g_sub2, lam_init, batch, seq, tq=256):
    t = qb.shape[0]
    nq = seq // tq
    qrow = lambda b, i: (b * nq + i, 0)
    full = lambda b, i: (b, 0)
    const = lambda b, i: (0, 0)
    return pl.pallas_call(
        functools.partial(_diff_kernel_t, tq=tq, lam_init=lam_init, seq=seq),
        grid=(batch, nq),
        in_specs=[pl.BlockSpec((tq, 256), qrow), pl.BlockSpec((seq, 256), full), pl.BlockSpec((seq, 256), full),
                  pl.BlockSpec((4, B_QK_DIM), const), pl.BlockSpec((1, LANES), const)],
        out_specs=pl.BlockSpec((tq, 256), qrow),
        out_shape=jax.ShapeDtypeStruct((t, 256), BF16),
        scratch_shapes=[pltpu.VMEM((2, LANES, seq), BF16), pltpu.VMEM((2, 4 * tq, LANES), BF16),
                        pltpu.VMEM((2, 1, 4 * tq), F32), pltpu.VMEM((2, 1, 4 * tq), F32),
                        pltpu.VMEM((2, LANES, 4 * tq), F32)],
        compiler_params=_params("parallel", "arbitrary"),
        name="diff_attention",
    )(qb, kb, vb, lam_vec, g_sub2)


def moba_attention(qc, kc, vc, batch, seq):
    tq = MOBA_BLOCK
    t = qc.shape[0]
    nb = seq // tq
    n_sel = min(MOBA_TOPK, nb - 1)
    nbp = 8
    assert seq % tq == 0 and nb <= nbp
    qrow = lambda b, i: (b * nb + i, 0)
    full = lambda b, i: (b, 0)
    return pl.pallas_call(
        functools.partial(_moba_kernel_t, tq=tq, nb=nb, n_sel=n_sel, seq=seq),
        grid=(batch, nb),
        in_specs=[pl.BlockSpec((tq, 384), qrow), pl.BlockSpec((seq, 384), full), pl.BlockSpec((seq, 384), full)],
        out_specs=pl.BlockSpec((tq, 384), qrow),
        out_shape=jax.ShapeDtypeStruct((t, 384), BF16),
        scratch_shapes=[pltpu.VMEM((nbp, 384), F32), pltpu.VMEM((3, LANES, seq), BF16),
                        pltpu.VMEM((3, 2 * tq, LANES), BF16), pltpu.VMEM((3, nbp, 2 * tq), F32),
                        pltpu.VMEM((3, 1, 2 * tq), F32), pltpu.VMEM((3, 1, 2 * tq), F32),
                        pltpu.VMEM((3, LANES, 2 * tq), F32)],
        compiler_params=_params("parallel", "arbitrary"),
        name="moba_attention",
    )(qc, kc, vc)


def _attend_t(k, q_ref, g, v_t, m_ref, l_ref, acc_ref, mask=None):
    for j in range(q_ref.shape[1] // LANES):
        cols = slice(j * LANES, (j + 1) * LANES)
        s = _dot_nt(k, q_ref[g, cols, :])
        if mask is not None:
            s = mask(j, s)
        m_prev = m_ref[g, :, cols]
        m_new = jnp.maximum(m_prev, jnp.max(s, axis=0, keepdims=True))
        alpha = jnp.exp2(m_prev - m_new)
        p = jnp.exp2(s - m_new)
        l_ref[g, :, cols] = alpha * l_ref[g, :, cols] + jnp.sum(p, axis=0, keepdims=True)
        acc_ref[g, :, cols] = alpha * acc_ref[g, :, cols] + _dot(v_t, p.astype(BF16))
        m_ref[g, :, cols] = m_new


def _causal_mask_t(tk, tq):
    key = lax.broadcasted_iota(jnp.int32, (tk, LANES), 0)
    qry = lax.broadcasted_iota(jnp.int32, (tk, LANES), 1)

    def mask(j, s):
        return jnp.where(key <= qry + (j * LANES) % tq, s, NEG)
    return mask


def _transpose_values(v_ref, vt_ref, seq, tk):
    for g in range(vt_ref.shape[0]):
        for n in range(seq // tk):
            blk = v_ref[n * tk:(n + 1) * tk, g * LANES:(g + 1) * LANES].astype(F32)
            vt_ref[g, :, n * tk:(n + 1) * tk] = blk.T.astype(vt_ref.dtype)


def _head_pair_out(acc_t, l, tq):
    even = acc_t[0:HEAD_DIM, 0:tq] / l[:, 0:tq]
    odd = acc_t[HEAD_DIM:LANES, tq:2 * tq] / l[:, tq:2 * tq]
    return jnp.concatenate([even, odd], axis=0).T


def _dsa_kernel_t(qa_ref, qi_ref, wiq_ref, kka_ref, vva_ref, kki_ref, o_ref,
                  sc_ref, vt_ref, qs_ref, m_ref, l_ref, acc_ref, *, tq, k_top, seq):
    i = pl.program_id(1)
    nk = i + 1
    t0 = i * tq
    lo = lax.broadcasted_iota(jnp.int32, (tq, LANES), 1) < HEAD_DIM
    zero_b = jnp.zeros((tq, LANES), BF16)

    @pl.when(i == 0)
    def _():
        _transpose_values(vva_ref, vt_ref, seq, tq)

    def stack_heads(q):
        out = []
        for g in range(q.shape[1] // LANES):
            qg = q[:, g * LANES:(g + 1) * LANES]
            out += [jnp.where(lo, qg, zero_b), jnp.where(lo, zero_b, qg)]
        return out

    qa_stack = stack_heads(qa_ref[...])
    for g in range(3):
        qs_ref[g, 0:tq, :] = qa_stack[2 * g]
        qs_ref[g, tq:2 * tq, :] = qa_stack[2 * g + 1]
    qi_stack = jnp.concatenate(stack_heads(qi_ref[...]), axis=0)
    wi_t = wiq_ref[...].T

    def rows(c):
        return pl.ds(pl.multiple_of(c * tq, tq), tq)

    key_pos = lax.broadcasted_iota(jnp.int32, (tq, tq), 0)
    qry_pos = lax.broadcasted_iota(jnp.int32, (tq, tq), 1)

    def idx_body(c, carry):
        r = jnp.maximum(_dot_nt(kki_ref[rows(c), :], qi_stack), 0.0)
        s = wi_t[0:1, :] * r[:, 0:tq]
        for h in range(1, IDX_HEADS):
            s = s + wi_t[h:h + 1, :] * r[:, h * tq:(h + 1) * tq]
        causal = (c * tq + key_pos) <= (t0 + qry_pos)
        sc_ref[rows(c), :] = jnp.where(causal, s, -jnp.inf)
        return carry

    lax.fori_loop(0, nk, idx_body, 0)

    def count_ge(cand):
        def body(c, acc):
            ge = jnp.where(sc_ref[rows(c), :] >= cand, 1.0, 0.0)
            return acc + jnp.sum(ge.reshape(tq // 8, 8, tq), axis=0)
        acc = lax.fori_loop(0, nk, body, jnp.zeros((8, tq), F32))
        return jnp.sum(acc, axis=0, keepdims=True)

    kf = float(k_top)

    def search():
        key0 = jnp.where(count_ge(jnp.zeros((1, tq), F32)) >= kf, 0, INT_MIN).astype(jnp.int32)

        def bisect(b, key):
            trial = key | lax.shift_left(jnp.int32(1), 30 - b)
            return jnp.where(count_ge(_key_to_f32(trial)) >= kf, trial, key)
        return lax.fori_loop(0, 31, bisect, key0)

    key = lax.cond(t0 + tq <= k_top, lambda: jnp.full((1, tq), INT_MIN, jnp.int32), search)
    thr = _key_to_f32(key)
    thr_next = _key_to_f32(key + 1)
    need = kf - count_ge(thr_next)
    all_sel = (t0 + lax.broadcasted_iota(jnp.int32, (1, tq), 1)) < k_top
    lower = (qry_pos <= key_pos).astype(BF16)

    _init_stats(m_ref, l_ref, acc_ref)

    def att_body(c, tie_run):
        s_idx = sc_ref[rows(c), :]
        ge = s_idx >= thr
        gt = s_idx >= thr_next
        tie = jnp.where(gt, 0.0, jnp.where(ge, 1.0, 0.0))
        prefix = _dot(lower, tie.astype(BF16)) + tie_run
        take = jnp.where(gt, 1.0, jnp.where(prefix <= need, tie, 0.0))
        take = jnp.where(all_sel, 1.0, take)
        take = jnp.where(s_idx > -jnp.inf, take, 0.0)

        def mask(j, s):
            q0 = (j * LANES) % tq
            return jnp.where(take[:, q0:q0 + LANES] > 0.5, s, NEG)

        kk = kka_ref[rows(c), :]
        vt = vt_ref[0, :, rows(c)]
        for g in range(3):
            _attend_t(kk, qs_ref, g, vt, m_ref, l_ref, acc_ref, mask)
        return tie_run + jnp.sum(tie, axis=0, keepdims=True)

    lax.fori_loop(0, nk, att_body, jnp.zeros((1, tq), F32))

    for g in range(3):
        o_ref[:, g * LANES:(g + 1) * LANES] = _head_pair_out(acc_ref[g], l_ref[g], tq).astype(o_ref.dtype)


def _diff_kernel_t(qb_ref, kb_ref, vb_ref, lam_ref, gsub_ref, o_ref,
                   vt_ref, qs_ref, m_ref, l_ref, acc_ref, *, tq, lam_init, seq):
    i = pl.program_id(1)
    lane = lax.broadcasted_iota(jnp.int32, (tq, LANES), 1)
    zero_b = jnp.zeros((tq, LANES), BF16)

    @pl.when(i == 0)
    def _():
        _transpose_values(vb_ref, vt_ref, seq, tq)

    qb = qb_ref[...]
    for g in range(2):
        qg = qb[:, g * LANES:(g + 1) * LANES]
        for j in range(4):
            qs_ref[g, j * tq:(j + 1) * tq, :] = jnp.where(lane // B_QK_DIM == j, qg, zero_b)
    _init_stats(m_ref, l_ref, acc_ref)

    def rows(c):
        return pl.ds(pl.multiple_of(c * tq, tq), tq)

    def step(c, masked):
        mask = _causal_mask_t(tq, tq) if masked else None
        for g in range(2):
            _attend_t(kb_ref[rows(c), g * LANES:(g + 1) * LANES], qs_ref, g, vt_ref[g, :, rows(c)],
                      m_ref, l_ref, acc_ref, mask)

    def body(c, carry):
        step(c, False)
        return carry

    lax.fori_loop(0, i, body, 0)
    step(i, True)

    lv = lam_ref[...]
    lam = (jnp.exp(jnp.sum(lv[0:1] * lv[1:2], axis=1, keepdims=True))
           - jnp.exp(jnp.sum(lv[2:3] * lv[3:4], axis=1, keepdims=True)) + lam_init)
    gsub = gsub_ref[...]
    for g in range(2):
        acc_t, l = acc_ref[g], l_ref[g]

        def prob(r0, j):
            return acc_t[r0:r0 + HEAD_DIM, j * tq:(j + 1) * tq] / l[:, j * tq:(j + 1) * tq]

        halves = []
        for r0, j in ((0, 0), (HEAD_DIM, 2)):
            o = prob(r0, j) - lam * prob(r0, j + 1)
            ms = jnp.mean(o * o, axis=0, keepdims=True)
            halves.append(o * lax.rsqrt(ms + SUBLN_EPS))
        y = (jnp.concatenate(halves, axis=0).T * gsub) * (1.0 - lam_init)
        o_ref[:, g * LANES:(g + 1) * LANES] = y.astype(o_ref.dtype)


def _moba_kernel_t(qc_ref, kc_ref, vc_ref, o_ref, kmean_ref, vt_ref, qs_ref, bias_ref, m_ref, l_ref, acc_ref,
                   *, tq, nb, n_sel, seq):
    i = pl.program_id(1)
    nbp = kmean_ref.shape[0]
    lane = lax.broadcasted_iota(jnp.int32, (tq, LANES), 1)
    lo = lane < HEAD_DIM
    zero_b = jnp.zeros((tq, LANES), BF16)

    @pl.when(i == 0)
    def _():
        _transpose_values(vc_ref, vt_ref, seq, tq)
        kmean_ref[...] = jnp.zeros(kmean_ref.shape, F32)
        for n in range(nb):
            kblk = kc_ref[n * tq:(n + 1) * tq, :].astype(F32)
            kmean_ref[n:n + 1, :] = jnp.mean(kblk, axis=0, keepdims=True)

    qc = qc_ref[...]
    sub = lax.broadcasted_iota(jnp.int32, (nbp, 2 * tq), 0)
    past = sub < i
    for g in range(3):
        qg = qc[:, g * LANES:(g + 1) * LANES]
        q2 = jnp.concatenate([jnp.where(lo, qg, zero_b), jnp.where(lo, zero_b, qg)], axis=0)
        gt = _dot_nt(kmean_ref[:, g * LANES:(g + 1) * LANES], q2.astype(F32), precision=lax.Precision.HIGHEST)
        rows_ = []
        for n in range(nbp):
            gn = gt[n:n + 1, :]
            beats = jnp.where(sub < n, jnp.where(gt >= gn, 1.0, 0.0), jnp.where(gt > gn, 1.0, 0.0))
            beats = jnp.where(sub == n, 0.0, jnp.where(past, beats, 0.0))
            rank = jnp.sum(beats, axis=0, keepdims=True)
            rows_.append(jnp.where(rank < n_sel, 0.0, NEG))
        bias_ref[g] = jnp.where(past, jnp.concatenate(rows_, axis=0), NEG)
        qs_ref[g] = q2
    _init_stats(m_ref, l_ref, acc_ref)

    def rows(c):
        return pl.ds(pl.multiple_of(c * tq, tq), tq)

    def body(c, carry):
        for g in range(3):
            bias = bias_ref[g, pl.ds(c, 1), :]

            def mask(j, s):
                return s + bias[:, j * LANES:(j + 1) * LANES]

            _attend_t(kc_ref[rows(c), g * LANES:(g + 1) * LANES], qs_ref, g, vt_ref[g, :, rows(c)],
                      m_ref, l_ref, acc_ref, mask)
        return carry

    lax.fori_loop(0, i, body, 0)

    causal = _causal_mask_t(tq, tq)
    for g in range(3):
        _attend_t(kc_ref[rows(i), g * LANES:(g + 1) * LANES], qs_ref, g, vt_ref[g, :, rows(i)],
                  m_ref, l_ref, acc_ref, causal)
        o_ref[:, g * LANES:(g + 1) * LANES] = _head_pair_out(acc_ref[g], l_ref[g], tq).astype(o_ref.dtype)


def _route(h, wr_ref, cw_ref):
    hi = h.astype(BF16)
    lo = (h - hi.astype(F32)).astype(BF16)
    logits = _dot(hi, wr_ref[0]) + (_dot(lo, wr_ref[0]) + _dot(hi, wr_ref[1]))
    lane = lax.broadcasted_iota(jnp.int32, logits.shape, 1)
    lg = jnp.where(lane < N_EXPERTS, logits, -jnp.inf)
    v0 = jnp.max(lg, axis=1, keepdims=True)
    i0 = jnp.min(jnp.where(lg == v0, lane, LANES), axis=1, keepdims=True)
    lg1 = jnp.where(lane == i0, -jnp.inf, lg)
    v1 = jnp.max(lg1, axis=1, keepdims=True)
    i1 = jnp.min(jnp.where(lg1 == v1, lane, LANES), axis=1, keepdims=True)
    e1 = jnp.exp(v1 - v0)
    w0 = 1.0 / (1.0 + e1)
    cw_ref[...] = (jnp.where(lane == 0, i0.astype(F32), 0.0) + jnp.where(lane == 1, i1.astype(F32), 0.0)
                   + jnp.where(lane == 2, w0, 0.0) + jnp.where(lane == 3, e1 * w0, 0.0))


def _outproj_kernel(x_ref, ya_ref, yb_ref, yc_ref, wo_ref, g_ref, mod_ref, *rest, with_router):
    if with_router:
        wr_ref, x1_ref, h_ref, cw_ref = rest
    else:
        x1_ref, h_ref = rest
    y = (_dot(ya_ref[...], wo_ref[0:384, :]) + _dot(yb_ref[...], wo_ref[384:640, :])
         + _dot(yc_ref[...], wo_ref[640:1024, :]))
    x1 = x_ref[...] + mod_ref[2:3, :] * y
    x1_ref[...] = x1
    h = _norm_mod(x1, g_ref[...], mod_ref[3:4, :], mod_ref[4:5, :])
    if with_router:
        h_ref[...] = _pack_rows(h)
        _route(h, wr_ref, cw_ref)
    else:
        h_ref[...] = h.astype(BF16)


def outproj(x, ya, yb, yc, wo, g_ffn, mod, seq, w_router=None, tm=512):
    t, d = x.shape
    tiles_per_batch = seq // tm
    row = lambda i: (i, 0)
    const = lambda i: (0, 0)
    with_router = w_router is not None
    in_specs = [pl.BlockSpec((tm, d), row), pl.BlockSpec((tm, 384), row), pl.BlockSpec((tm, 256), row),
                pl.BlockSpec((tm, 384), row), pl.BlockSpec((d, d), const), pl.BlockSpec((1, d), const),
                pl.BlockSpec((None, 6, d), lambda i: (i // tiles_per_batch, 0, 0))]
    out_specs = [pl.BlockSpec((tm, d), row), pl.BlockSpec((tm, d), row)]
    out_shape = [jax.ShapeDtypeStruct((t, d), F32), jax.ShapeDtypeStruct((t, d), BF16)]
    args = [x, ya, yb, yc, wo, g_ffn, mod]
    if with_router:
        in_specs.append(pl.BlockSpec((2, d, LANES), lambda i: (0, 0, 0)))
        out_specs[1] = pl.BlockSpec((tm, d // 2), row)
        out_shape[1] = jax.ShapeDtypeStruct((t, d // 2), jnp.int32)
        out_specs.append(pl.BlockSpec((tm, LANES), row))
        out_shape.append(jax.ShapeDtypeStruct((t, LANES), F32))
        args.append(w_router)
    return pl.pallas_call(
        functools.partial(_outproj_kernel, with_router=with_router),
        grid=(t // tm,), in_specs=in_specs, out_specs=out_specs, out_shape=out_shape,
        compiler_params=_params("parallel"),
        name="outproj_router" if with_router else "outproj",
    )(*args)


def _swiglu_partial(h, wg_ref, wu_ref, wd_ref):
    a = _dot(h, wg_ref[...])
    u = _dot(h, wu_ref[...])
    act = (a * (1.0 / (1.0 + jnp.exp(-a)))) * u
    return _dot(act.astype(BF16), wd_ref[...])


def _ffn_kernel(x1_ref, h_ref, mod_ref, wg_ref, wu_ref, wd_ref, o_ref, acc_ref):
    f = pl.program_id(1)

    @pl.when(f == 0)
    def _():
        acc_ref[...] = jnp.zeros(acc_ref.shape, F32)

    acc_ref[...] += _swiglu_partial(h_ref[...], wg_ref, wu_ref, wd_ref)

    @pl.when(f == pl.num_programs(1) - 1)
    def _():
        o_ref[...] = x1_ref[...] + mod_ref[5:6, :] * acc_ref[...]


def ffn_dense(x1, h, mod, wg, wu, wd, seq, tm=512, tf=1408):
    t, d = x1.shape
    dff = wg.shape[1]
    tiles_per_batch = seq // tm
    row = lambda i, f: (i, 0)
    return pl.pallas_call(
        _ffn_kernel,
        grid=(t // tm, dff // tf),
        in_specs=[pl.BlockSpec((tm, d), row), pl.BlockSpec((tm, d), row),
                  pl.BlockSpec((None, 6, d), lambda i, f: (i // tiles_per_batch, 0, 0)),
                  pl.BlockSpec((d, tf), lambda i, f: (0, f)), pl.BlockSpec((d, tf), lambda i, f: (0, f)),
                  pl.BlockSpec((tf, d), lambda i, f: (f, 0))],
        out_specs=pl.BlockSpec((tm, d), row),
        out_shape=jax.ShapeDtypeStruct((t, d), F32),
        scratch_shapes=[pltpu.VMEM((tm, d), F32)],
        compiler_params=_params("parallel", "arbitrary"),
        name="ffn_dense",
    )(x1, h, mod, wg, wu, wd)


MOE_TILE = 512
SC_CORES, SC_SUBCORES = 2, 16
SC_ROWS = 64
HI16 = -65536


def _pack_rows(x):
    c = x.shape[1] // 2
    bits = lax.bitcast_convert_type(x.astype(jnp.bfloat16).astype(F32), jnp.int32)
    return lax.shift_right_logical(bits[:, :c], jnp.int32(16)) | (bits[:, c:] & jnp.int32(HI16))


def _unpack_rows(w):
    lo = lax.bitcast_convert_type(lax.shift_left(w, jnp.int32(16)), F32)
    hi = lax.bitcast_convert_type(w & jnp.int32(HI16), F32)
    return jnp.concatenate([lo, hi], axis=1)


def sc_gather_rows(table, idx):
    d = table.shape[1]
    b = idx.shape[0]
    per_worker = b // (SC_CORES * SC_SUBCORES)
    assert per_worker * SC_CORES * SC_SUBCORES == b and per_worker % (2 * SC_ROWS) == 0
    mesh = plsc.VectorSubcoreMesh(core_axis_name="c", subcore_axis_name="s")
    idx_buf = pltpu.VMEM((SC_ROWS,), jnp.int32)
    row_buf = pltpu.VMEM((SC_ROWS, d), table.dtype)

    @functools.partial(
        pl.kernel, mesh=mesh, out_type=jax.ShapeDtypeStruct((b, d), table.dtype),
        scratch_types=[idx_buf, idx_buf, row_buf, row_buf] + [pltpu.SemaphoreType.DMA] * 4,
        name="sc_gather_rows")
    def gather(table_hbm, idx_hbm, out_hbm, idx0, idx1, rows0, rows1, sem_g0, sem_g1, sem_w0, sem_w1):
        base = (lax.axis_index("s") * SC_CORES + lax.axis_index("c")) * per_worker

        @pl.loop(0, per_worker // (2 * SC_ROWS))
        def _(pair):
            off0 = pl.multiple_of(base + pair * (2 * SC_ROWS), SC_ROWS)
            off1 = pl.multiple_of(off0 + SC_ROWS, SC_ROWS)
            pltpu.sync_copy(idx_hbm.at[pl.ds(off0, SC_ROWS)], idx0)
            pltpu.sync_copy(idx_hbm.at[pl.ds(off1, SC_ROWS)], idx1)
            gather0 = pltpu.async_copy(table_hbm.at[idx0], rows0, sem_g0)
            gather1 = pltpu.async_copy(table_hbm.at[idx1], rows1, sem_g1)
            gather0.wait()
            write0 = pltpu.async_copy(rows0, out_hbm.at[pl.ds(off0, SC_ROWS)], sem_w0)
            gather1.wait()
            write1 = pltpu.async_copy(rows1, out_hbm.at[pl.ds(off1, SC_ROWS)], sem_w1)
            write0.wait()
            write1.wait()

    return gather(table, idx)


def _moe_layout(e0, e1, t, n_exp):
    experts = jnp.arange(n_exp, dtype=jnp.int32)
    routed = ((e0[:, None] == experts) | (e1[:, None] == experts)).astype(jnp.int32)
    csum = jnp.cumsum(routed, axis=0)
    padded = (csum[-1] + MOE_TILE - 1) // MOE_TILE * MOE_TILE
    ends = jnp.cumsum(padded)
    pos = (ends - padded)[None, :] + csum - routed
    pos0 = jnp.take_along_axis(pos, e0[:, None], axis=1)[:, 0]
    pos1 = jnp.take_along_axis(pos, e1[:, None], axis=1)[:, 0]
    n_rows = 2 * t + n_exp * MOE_TILE
    tok = jnp.arange(t, dtype=jnp.int32)
    src = jnp.zeros((n_rows,), jnp.int32).at[jnp.concatenate([pos0, pos1])].set(
        jnp.concatenate([tok, tok]), unique_indices=True)
    n_tiles = n_rows // MOE_TILE
    n_valid = ends[-1] // MOE_TILE
    tile_expert = jnp.sum(jnp.arange(n_tiles)[:, None] * MOE_TILE >= ends[None, :], axis=1)
    tile_expert = tile_expert[jnp.minimum(jnp.arange(n_tiles), n_valid - 1)].astype(jnp.int32)
    return pos0, pos1, src, tile_expert, n_valid.astype(jnp.int32).reshape(1)


def _experts_packed_kernel(te_ref, nv_ref, x_ref, wg_ref, wu_ref, wd_ref, o_ref, acc_ref):
    j, f = pl.program_id(0), pl.program_id(1)
    valid = j < nv_ref[0]

    @pl.when(valid & (f == 0))
    def _():
        acc_ref[...] = jnp.zeros(acc_ref.shape, F32)

    @pl.when(valid)
    def _():
        acc_ref[...] += _swiglu_partial(_unpack_rows(x_ref[...]).astype(BF16), wg_ref, wu_ref, wd_ref)

    @pl.when(valid & (f == pl.num_programs(1) - 1))
    def _():
        o_ref[...] = _pack_rows(acc_ref[...])


def moe_experts_packed(x_sorted, tile_expert, n_valid, wg, wu, wd, tf=1408):
    n_rows, half = x_sorted.shape
    d = 2 * half
    nf = wg.shape[2] // tf

    def tile(j, nv):
        return jnp.minimum(j, nv[0] - 1)

    def fstep(j, f, nv):
        return jnp.where(j < nv[0], f, nf - 1)

    return pl.pallas_call(
        _experts_packed_kernel,
        grid_spec=pltpu.PrefetchScalarGridSpec(
            num_scalar_prefetch=2, grid=(n_rows // MOE_TILE, nf),
            in_specs=[pl.BlockSpec((MOE_TILE, half), lambda j, f, te, nv: (tile(j, nv), 0)),
                      pl.BlockSpec((None, d, tf), lambda j, f, te, nv: (te[j], 0, fstep(j, f, nv))),
                      pl.BlockSpec((None, d, tf), lambda j, f, te, nv: (te[j], 0, fstep(j, f, nv))),
                      pl.BlockSpec((None, tf, d), lambda j, f, te, nv: (te[j], fstep(j, f, nv), 0))],
            out_specs=pl.BlockSpec((MOE_TILE, half), lambda j, f, te, nv: (tile(j, nv), 0)),
            scratch_shapes=[pltpu.VMEM((MOE_TILE, d), F32)]),
        out_shape=jax.ShapeDtypeStruct((n_rows, half), jnp.int32),
        compiler_params=_params("arbitrary", "arbitrary"),
        name="moe_experts",
    )(tile_expert, n_valid, x_sorted, wg, wu, wd)


def _combine_packed_kernel(y0_ref, y1_ref, route_ref, x1_ref, mod_ref, o_ref):
    rt = route_ref[...]
    f = rt[:, 2:3] * _unpack_rows(y0_ref[...]) + rt[:, 3:4] * _unpack_rows(y1_ref[...])
    o_ref[...] = x1_ref[...] + mod_ref[5:6, :] * f


def moe_combine_packed(y_pairs, route, x1, mod, seq, tm=512):
    t, d = x1.shape
    nt = t // tm
    tiles_per_batch = seq // tm
    row = lambda i: (i, 0)
    return pl.pallas_call(
        _combine_packed_kernel,
        grid=(nt,),
        in_specs=[pl.BlockSpec((tm, d // 2), row), pl.BlockSpec((tm, d // 2), lambda i: (i + nt, 0)),
                  pl.BlockSpec((tm, LANES), row), pl.BlockSpec((tm, d), row),
                  pl.BlockSpec((None, 6, d), lambda i: (i // tiles_per_batch, 0, 0))],
        out_specs=pl.BlockSpec((tm, d), row),
        out_shape=jax.ShapeDtypeStruct((t, d), F32),
        compiler_params=_params("parallel"),
        name="moe_combine",
    )(y_pairs, y_pairs, route, x1, mod)


def ffn_moe_sc(x1, h_packed, route, mod, wg, wu, wd, seq):
    t = x1.shape[0]
    e0, e1 = route[:, 0].astype(jnp.int32), route[:, 1].astype(jnp.int32)
    pos0, pos1, src, tile_expert, n_valid = _moe_layout(e0, e1, t, wg.shape[0])
    h_sorted = sc_gather_rows(h_packed, src)
    y_sorted = moe_experts_packed(h_sorted, tile_expert, n_valid, wg, wu, wd)
    y_pairs = sc_gather_rows(y_sorted, jnp.concatenate([pos0, pos1]))
    return moe_combine_packed(y_pairs, route, x1, mod, seq)


def _final_norm_kernel(x_ref, g_ref, o_ref):
    x = x_ref[...]
    o_ref[...] = (x * lax.rsqrt(jnp.mean(x * x, axis=-1, keepdims=True) + EPS)) * g_ref[...]


def final_norm(x, g, tm=512):
    t, d = x.shape
    return pl.pallas_call(
        _final_norm_kernel,
        grid=(t // tm,),
        in_specs=[pl.BlockSpec((tm, d), lambda i: (i, 0)), pl.BlockSpec((1, d), lambda i: (0, 0))],
        out_specs=pl.BlockSpec((tm, d), lambda i: (i, 0)),
        out_shape=jax.ShapeDtypeStruct((t, d), F32),
        compiler_params=_params("parallel"),
        name="final_norm",
    )(x, g)


def _rope_tables(positions, dim):
    rot = dim // ROPE_FRACTION
    half = rot // 2
    inv = 1.0 / (ROPE_THETA ** (np.arange(0, rot, 2, dtype=np.float32) / rot))
    ang = positions.reshape(-1).astype(F32)[:, None] * jnp.asarray(inv, F32)
    cos, sin = jnp.cos(ang), jnp.sin(ang)
    t = ang.shape[0]
    ones = jnp.ones((t, dim - rot), F32)
    zeros = lambda w: jnp.zeros((t, w), F32)
    reps = LANES // dim
    c = jnp.tile(jnp.concatenate([cos, cos, ones], axis=1), (1, reps))
    sa = jnp.tile(jnp.concatenate([-sin, zeros(dim - half)], axis=1), (1, reps))
    sb = jnp.tile(jnp.concatenate([zeros(half), sin, zeros(dim - rot)], axis=1), (1, reps))
    return c, sa, sb


def _relayout_w_in(w):
    d = w.shape[0]
    pts = np.cumsum([0, 384, 64, 64, 256, 64, 4, 256, 256, 256, 384, 384, 384])
    (q_a, k_a, v_a, q_i, k_i, w_i, q_b, k_b, v_b, q_c, k_c, v_c) = [w[:, pts[j]:pts[j + 1]] for j in range(12)]
    w_i_pad = jnp.concatenate([w_i, jnp.zeros((d, LANES - IDX_HEADS), w.dtype)], axis=1)
    return jnp.concatenate([q_a, k_a, k_a, v_a, v_a, q_i, k_i, k_i, w_i_pad,
                            q_b, k_b, v_b, q_c, k_c, v_c], axis=1).astype(BF16)


def kernel(x, c, positions, w_in, w_out, diff_lambda, diff_subln, w_ada, b_ada, g_attn, g_ffn, w_ff_gate,
           w_ff_up, w_ff_down, w_router, w_exp_gate, w_exp_up, w_exp_down, g_final):
    batch, seq, d = x.shape
    depth = w_in.shape[0]
    t = batch * seq
    tabs = _rope_tables(positions, HEAD_DIM) + _rope_tables(positions, B_QK_DIM)
    mod_all = adaln_mod(c, w_ada, b_ada).reshape(depth, batch, 6, d)
    xf = x.reshape(t, d)
    for layer in range(depth):
        mod = mod_all[layer]
        lam_init = 0.8 - 0.6 * math.exp(-0.3 * layer)
        (qa, kka, vva, qi, kki, wi, qb, kb, vb, qc, kc, vc) = inproj(
            xf, g_attn[layer].reshape(1, d), mod, tabs, _relayout_w_in(w_in[layer]), seq)
        ya = dsa_attention(qa, qi, wi, kka, vva, kki, batch, seq)
        g_sub2 = jnp.tile(diff_subln[layer], 2).reshape(1, LANES)
        yb = diff_attention(qb, kb, vb, diff_lambda[layer], g_sub2, lam_init, batch, seq)
        yc = moba_attention(qc, kc, vc, batch, seq)
        j = layer // 2
        wo = w_out[layer].astype(BF16)
        gf = g_ffn[layer].reshape(1, d)
        if layer % 2 == 0:
            x1, h = outproj(xf, ya, yb, yc, wo, gf, mod, seq)
            xf = ffn_dense(x1, h, mod, w_ff_gate[j].astype(BF16), w_ff_up[j].astype(BF16),
                           w_ff_down[j].astype(BF16), seq)
        else:
            wr = jnp.concatenate([w_router[j], jnp.zeros((d, LANES - N_EXPERTS), F32)], axis=1)
            wr_hi = wr.astype(BF16)
            wr = jnp.stack([wr_hi, (wr - wr_hi.astype(F32)).astype(BF16)])
            x1, h_packed, route = outproj(xf, ya, yb, yc, wo, gf, mod, seq, w_router=wr)
            xf = ffn_moe_sc(x1, h_packed, route, mod, w_exp_gate[j].astype(BF16), w_exp_up[j].astype(BF16),
                            w_exp_down[j].astype(BF16), seq)
    return final_norm(xf, g_final.reshape(1, d)).reshape(batch, seq, d)
```

```python
import functools
import math

import jax
import jax.numpy as jnp
import numpy as np
from jax import lax
from jax.experimental import pallas as pl
from jax.experimental.pallas import tpu as pltpu
from jax.experimental.pallas import tpu_sc as plsc

F32 = jnp.float32
BF16 = jnp.bfloat16

HEAD_DIM = 64
A_HEADS = 6
IDX_HEADS = 4
B_HEADS = 4
B_QK_DIM = 32
C_HEADS = 6
DSA_TOPK_MAX = 256
MOBA_BLOCK = 256
MOBA_TOPK = 3
ROPE_THETA = 500000.0
ROPE_FRACTION = 4
SUBLN_EPS = 1e-5
EPS = 1e-6
N_EXPERTS = 8

LANES = 128
NEG = -1e30
INT_MIN = -2 ** 31
MIN_NORMAL_KEY = 0x00800000
MIN_NORMAL_F32 = float(np.float32(2.0 ** -126))
VMEM_LIMIT = 48 * 1024 * 1024
LOG2E = math.log2(math.e)

_G_QA, _G_KKA, _G_VVA, _G_QI, _G_KKI, _G_WI = (0, 384), (384, 512), (512, 640), (640, 896), (896, 1024), (1024, 1152)
_G_QB, _G_KB, _G_VB = (1152, 1408), (1408, 1664), (1664, 1920)
_G_QC, _G_KC, _G_VC = (1920, 2304), (2304, 2688), (2688, 3072)
D_IN_PAD = 3072


def _params(*sem):
    return pltpu.CompilerParams(dimension_semantics=sem, vmem_limit_bytes=VMEM_LIMIT)


def _dot(a, b):
    return jnp.dot(a, b, preferred_element_type=F32)


def _dot_nt(a, b, precision=None):
    return lax.dot_general(a, b, (((1,), (1,)), ((), ())), preferred_element_type=F32, precision=precision)


def _adaln_kernel(c_ref, w_ref, b_ref, o_ref):
    c = c_ref[...]
    c_act = c * (1.0 / (1.0 + jnp.exp(-c)))
    o_ref[...] = jnp.dot(c_act, w_ref[...], preferred_element_type=F32,
                         precision=lax.Precision.HIGHEST) + b_ref[...]


def adaln_mod(c, w_ada, b_ada, tn=1536):
    depth, d, n = w_ada.shape
    b = c.shape[0]
    return pl.pallas_call(
        _adaln_kernel,
        grid=(depth, n // tn),
        in_specs=[pl.BlockSpec((b, d), lambda l, j: (0, 0)),
                  pl.BlockSpec((None, d, tn), lambda l, j: (l, 0, j)),
                  pl.BlockSpec((None, 1, tn), lambda l, j: (l, 0, j))],
        out_specs=pl.BlockSpec((None, b, tn), lambda l, j: (l, 0, j)),
        out_shape=jax.ShapeDtypeStruct((depth, b, n), F32),
        compiler_params=_params("parallel", "parallel"),
        name="adaln_mod",
    )(c, w_ada, b_ada.reshape(depth, 1, n))


def _norm_mod(x, g, shift, scale, eps=EPS):
    y = x * lax.rsqrt(jnp.mean(x * x, axis=-1, keepdims=True) + eps)
    return (y * g) * (1.0 + scale) + shift


def _rope_store(acc, o_ref, cos, sa, sb, half):
    for j in range(acc.shape[1] // LANES):
        a = acc[:, j * LANES:(j + 1) * LANES]
        r = a * cos + pltpu.roll(a, half, 1) * sb + pltpu.roll(a, LANES - half, 1) * sa
        o_ref[:, j * LANES:(j + 1) * LANES] = r.astype(o_ref.dtype)


def _inproj_kernel(x_ref, g_ref, mod_ref, c64_ref, sa64_ref, sb64_ref, c32_ref, sa32_ref, sb32_ref, w_ref,
                   qa_ref, kka_ref, vva_ref, qi_ref, kki_ref, wi_ref,
                   qb_ref, kb_ref, vb_ref, qc_ref, kc_ref, vc_ref):
    h = _norm_mod(x_ref[...], g_ref[...], mod_ref[0:1, :], mod_ref[1:2, :]).astype(BF16)
    c64, sa64, sb64 = c64_ref[...], sa64_ref[...], sb64_ref[...]
    c32, sa32, sb32 = c32_ref[...], sa32_ref[...], sb32_ref[...]

    def proj(cols):
        return _dot(h, w_ref[:, cols[0]:cols[1]])

    qk_scale = HEAD_DIM ** -0.5 * LOG2E
    _rope_store(proj(_G_QA), qa_ref, c64 * qk_scale, sa64 * qk_scale, sb64 * qk_scale, 8)
    _rope_store(proj(_G_KKA), kka_ref, c64, sa64, sb64, 8)
    vva_ref[...] = proj(_G_VVA).astype(vva_ref.dtype)
    _rope_store(proj(_G_QI), qi_ref, c64, sa64, sb64, 8)
    _rope_store(proj(_G_KKI), kki_ref, c64, sa64, sb64, 8)
    wi_ref[...] = proj(_G_WI) * (IDX_HEADS ** -0.5 * HEAD_DIM ** -0.5)
    b_scale = B_QK_DIM ** -0.5 * LOG2E
    _rope_store(proj(_G_QB), qb_ref, c32 * b_scale, sa32 * b_scale, sb32 * b_scale, 4)
    _rope_store(proj(_G_KB), kb_ref, c32, sa32, sb32, 4)
    vb_ref[...] = proj(_G_VB).astype(vb_ref.dtype)
    _rope_store(proj(_G_QC), qc_ref, c64 * qk_scale, sa64 * qk_scale, sb64 * qk_scale, 8)
    _rope_store(proj(_G_KC), kc_ref, c64, sa64, sb64, 8)
    vc_ref[...] = proj(_G_VC).astype(vc_ref.dtype)


def inproj(x, g, mod, tabs, w_pad, seq, tm=512):
    t, d = x.shape
    tiles_per_batch = seq // tm
    row = lambda i: (i, 0)
    widths = [384, 128, 128, 256, 128, 128, 256, 256, 256, 384, 384, 384]
    dtypes = [BF16, BF16, BF16, BF16, BF16, F32, BF16, BF16, BF16, BF16, BF16, BF16]
    return pl.pallas_call(
        _inproj_kernel,
        grid=(t // tm,),
        in_specs=[pl.BlockSpec((tm, d), row),
                  pl.BlockSpec((1, d), lambda i: (0, 0)),
                  pl.BlockSpec((None, 6, d), lambda i: (i // tiles_per_batch, 0, 0))]
                 + [pl.BlockSpec((tm, LANES), row)] * 6
                 + [pl.BlockSpec((d, D_IN_PAD), lambda i: (0, 0))],
        out_specs=[pl.BlockSpec((tm, w), row) for w in widths],
        out_shape=[jax.ShapeDtypeStruct((t, w), dt) for w, dt in zip(widths, dtypes)],
        compiler_params=_params("parallel"),
        name="inproj",
    )(x, g, mod, *tabs, w_pad)


def _init_stats(m_ref, l_ref, acc_ref):
    m_ref[...] = jnp.full(m_ref.shape, -jnp.inf, F32)
    l_ref[...] = jnp.zeros(l_ref.shape, F32)
    acc_ref[...] = jnp.zeros(acc_ref.shape, F32)


def _key_to_f32(k):
    return lax.bitcast_convert_type(jnp.where(k >= 0, k, k ^ 0x7FFFFFFF), F32)


def dsa_attention(qa, qi, wi, kka, vva, kki, batch, seq, tq=256):
    t = qa.shape[0]
    nq = seq // tq
    k_top = min(DSA_TOPK_MAX, seq // 4)
    qrow = lambda b, i: (b * nq + i, 0)
    full = lambda b, i: (b, 0)
    return pl.pallas_call(
        functools.partial(_dsa_kernel_t, tq=tq, k_top=k_top, seq=seq),
        grid=(batch, nq),
        in_specs=[pl.BlockSpec((tq, 384), qrow), pl.BlockSpec((tq, 256), qrow), pl.BlockSpec((tq, LANES), qrow),
                  pl.BlockSpec((seq, LANES), full), pl.BlockSpec((seq, LANES), full),
                  pl.BlockSpec((seq, LANES), full)],
        out_specs=pl.BlockSpec((tq, 384), qrow),
        out_shape=jax.ShapeDtypeStruct((t, 384), BF16),
        scratch_shapes=[pltpu.VMEM((seq, tq), F32), pltpu.VMEM((seq, tq), jnp.int16),
                        pltpu.VMEM((seq, tq), jnp.int16), pltpu.VMEM((1, LANES, seq), BF16),
                        pltpu.VMEM((3, 2 * tq, LANES), BF16),
                        pltpu.VMEM((3, 1, 2 * tq), F32), pltpu.VMEM((3, 1, 2 * tq), F32),
                        pltpu.VMEM((3, LANES, 2 * tq), F32)],
        compiler_params=_params("parallel", "arbitrary"),
        name="dsa_attention",
    )(qa, qi, wi, kka, vva, kki)


def diff_attention(qb, kb, vb, lam_vec, g_sub2, lam_init, batch, seq, tq=256):
    t = qb.shape[0]
    nq = seq // tq
    qrow = lambda b, i: (b * nq + i, 0)
    full = lambda b, i: (b, 0)
    const = lambda b, i: (0, 0)
    return pl.pallas_call(
        functools.partial(_diff_kernel_t, tq=tq, lam_init=lam_init, seq=seq),
        grid=(batch, nq),
        in_specs=[pl.BlockSpec((tq, 256), qrow), pl.BlockSpec((seq, 256), full), pl.BlockSpec((seq, 256), full),
                  pl.BlockSpec((4, B_QK_DIM), const), pl.BlockSpec((1, LANES), const)],
        out_specs=pl.BlockSpec((tq, 256), qrow),
        out_shape=jax.ShapeDtypeStruct((t, 256), BF16),
        scratch_shapes=[pltpu.VMEM((2, LANES, seq), BF16), pltpu.VMEM((2, 4 * tq, LANES), BF16),
                        pltpu.VMEM((2, 1, 4 * tq), F32), pltpu.VMEM((2, 1, 4 * tq), F32),
                        pltpu.VMEM((2, LANES, 4 * tq), F32)],
        compiler_params=_params("parallel", "arbitrary"),
        name="diff_attention",
    )(qb, kb, vb, lam_vec, g_sub2)


def moba_attention(qc, kc, vc, batch, seq):
    tq = MOBA_BLOCK
    t = qc.shape[0]
    nb = seq // tq
    n_sel = min(MOBA_TOPK, nb - 1)
    nbp = 8
    assert seq % tq == 0 and nb <= nbp
    qrow = lambda b, i: (b * nb + i, 0)
    full = lambda b, i: (b, 0)
    return pl.pallas_call(
        functools.partial(_moba_kernel_t, tq=tq, nb=nb, n_sel=n_sel, seq=seq),
        grid=(batch, nb),
        in_specs=[pl.BlockSpec((tq, 384), qrow), pl.BlockSpec((seq, 384), full), pl.BlockSpec((seq, 384), full)],
        out_specs=pl.BlockSpec((tq, 384), qrow),
        out_shape=jax.ShapeDtypeStruct((t, 384), BF16),
        scratch_shapes=[pltpu.VMEM((nbp, 384), F32), pltpu.VMEM((3, LANES, seq), BF16),
                        pltpu.VMEM((3, 2 * tq, LANES), BF16), pltpu.VMEM((3, nbp, 2 * tq), F32),
                        pltpu.VMEM((3, 1, 2 * tq), F32), pltpu.VMEM((3, 1, 2 * tq), F32),
                        pltpu.VMEM((3, LANES, 2 * tq), F32)],
        compiler_params=_params("parallel", "arbitrary"),
        name="moba_attention",
    )(qc, kc, vc)


def _attend_t(k, q_ref, g, v_t, m_ref, l_ref, acc_ref, mask=None):
    for j in range(q_ref.shape[1] // LANES):
        cols = slice(j * LANES, (j + 1) * LANES)
        s = _dot_nt(k, q_ref[g, cols, :])
        if mask is not None:
            s = mask(j, s)
        m_prev = m_ref[g, :, cols]
        m_new = jnp.maximum(m_prev, jnp.max(s, axis=0, keepdims=True))
        alpha = jnp.exp2(m_prev - m_new)
        p = jnp.exp2(s - m_new)
        l_ref[g, :, cols] = alpha * l_ref[g, :, cols] + jnp.sum(p, axis=0, keepdims=True)
        acc_ref[g, :, cols] = alpha * acc_ref[g, :, cols] + _dot(v_t, p.astype(BF16))
        m_ref[g, :, cols] = m_new


def _causal_mask_t(tk, tq):
    key = lax.broadcasted_iota(jnp.int32, (tk, LANES), 0)
    qry = lax.broadcasted_iota(jnp.int32, (tk, LANES), 1)

    def mask(j, s):
        return jnp.where(key <= qry + (j * LANES) % tq, s, NEG)
    return mask


def _transpose_values(v_ref, vt_ref, seq, tk):
    for g in range(vt_ref.shape[0]):
        for n in range(seq // tk):
            blk = v_ref[n * tk:(n + 1) * tk, g * LANES:(g + 1) * LANES].astype(F32)
            vt_ref[g, :, n * tk:(n + 1) * tk] = blk.T.astype(vt_ref.dtype)


def _head_pair_out(acc_t, l, tq):
    even = acc_t[0:HEAD_DIM, 0:tq] / l[:, 0:tq]
    odd = acc_t[HEAD_DIM:LANES, tq:2 * tq] / l[:, tq:2 * tq]
    return jnp.concatenate([even, odd], axis=0).T


def _dsa_kernel_t(qa_ref, qi_ref, wiq_ref, kka_ref, vva_ref, kki_ref, o_ref,
                  sc_ref, hi_ref, lo_ref, vt_ref, qs_ref, m_ref, l_ref, acc_ref, *, tq, k_top, seq):
    i = pl.program_id(1)
    nk = i + 1
    t0 = i * tq
    lo = lax.broadcasted_iota(jnp.int32, (tq, LANES), 1) < HEAD_DIM
    zero_b = jnp.zeros((tq, LANES), BF16)

    @pl.when(i == 0)
    def _():
        _transpose_values(vva_ref, vt_ref, seq, tq)

    def stack_heads(q):
        out = []
        for g in range(q.shape[1] // LANES):
            qg = q[:, g * LANES:(g + 1) * LANES]
            out += [jnp.where(lo, qg, zero_b), jnp.where(lo, zero_b, qg)]
        return out

    qa_stack = stack_heads(qa_ref[...])
    for g in range(3):
        qs_ref[g, 0:tq, :] = qa_stack[2 * g]
        qs_ref[g, tq:2 * tq, :] = qa_stack[2 * g + 1]
    qi_stack = jnp.concatenate(stack_heads(qi_ref[...]), axis=0)
    wi_t = wiq_ref[...].T

    def rows(c):
        return pl.ds(pl.multiple_of(c * tq, tq), tq)

    key_pos = lax.broadcasted_iota(jnp.int32, (tq, tq), 0)
    qry_pos = lax.broadcasted_iota(jnp.int32, (tq, tq), 1)

    def idx_body(c, carry):
        r = jnp.maximum(_dot_nt(kki_ref[rows(c), :], qi_stack), 0.0)
        s = wi_t[0:1, :] * r[:, 0:tq]
        for h in range(1, IDX_HEADS):
            s = s + wi_t[h:h + 1, :] * r[:, h * tq:(h + 1) * tq]
        causal = (c * tq + key_pos) <= (t0 + qry_pos)
        s = jnp.where(causal, s, -jnp.inf)
        s = jnp.where(jnp.abs(s) < MIN_NORMAL_F32, 0.0, s)
        sc_ref[rows(c), :] = s
        bits = lax.bitcast_convert_type(s, jnp.int32)
        key = jnp.where(bits >= 0, bits, bits ^ 0x7FFFFFFF)
        hi_ref[rows(c), :] = lax.shift_right_arithmetic(key, jnp.int32(16)).astype(jnp.int16)
        lo_ref[rows(c), :] = ((key & 0xFFFF) - 32768).astype(jnp.int16)
        return carry

    lax.fori_loop(0, nk, idx_body, 0)

    def count16(ref, cand, strict):
        c16 = cand.astype(jnp.int16)

        def body(c, acc):
            x = ref[rows(c), :]
            hit = jnp.where((x > c16) if strict else (x >= c16), jnp.int16(1), jnp.int16(0))
            for r in range(tq // 16):
                acc = acc + hit[r * 16:(r + 1) * 16]
            return acc
        acc = lax.fori_loop(0, nk, body, jnp.zeros((16, tq), jnp.int16))
        return jnp.sum(acc.astype(F32), axis=0, keepdims=True)

    def search16(ref, k_need):
        v0 = jnp.where(count16(ref, jnp.zeros((1, tq), jnp.int32), False) >= k_need, 0, -32768).astype(jnp.int32)

        def bisect(b, v):
            trial = v | lax.shift_left(jnp.int32(1), 14 - b)
            return jnp.where(count16(ref, trial, False) >= k_need, trial, v)
        return lax.fori_loop(0, 15, bisect, v0)

    kf = float(k_top)

    def search():
        hi_k = search16(hi_ref, kf)
        above = count16(hi_ref, hi_k, True)
        hi16 = hi_k.astype(jnp.int16)

        def keep_bucket(c, carry):
            lo_ref[rows(c), :] = jnp.where(hi_ref[rows(c), :] == hi16, lo_ref[rows(c), :], jnp.int16(-32768))
            return carry
        lax.fori_loop(0, nk, keep_bucket, 0)
        lo_k = search16(lo_ref, kf - above)
        n_gt = above + count16(lo_ref, lo_k, True)
        return lax.shift_left(hi_k, jnp.int32(16)) | (lo_k + 32768), kf - n_gt

    key, need = lax.cond(t0 + tq <= k_top,
                         lambda: (jnp.full((1, tq), INT_MIN, jnp.int32), jnp.full((1, tq), kf, F32)), search)
    thr = _key_to_f32(key)
    thr_next = _key_to_f32(jnp.where(key == 0, MIN_NORMAL_KEY, key + 1))
    all_sel = (t0 + lax.broadcasted_iota(jnp.int32, (1, tq), 1)) < k_top
    lower = (qry_pos <= key_pos).astype(BF16)

    _init_stats(m_ref, l_ref, acc_ref)

    def att_body(c, tie_run):
        s_idx = sc_ref[rows(c), :]
        ge = s_idx >= thr
        gt = s_idx >= thr_next
        tie = jnp.where(gt, 0.0, jnp.where(ge, 1.0, 0.0))
        prefix = _dot(lower, tie.astype(BF16)) + tie_run
        take = jnp.where(gt, 1.0, jnp.where(prefix <= need, tie, 0.0))
        take = jnp.where(all_sel, 1.0, take)
        take = jnp.where(s_idx > -jnp.inf, take, 0.0)

        def mask(j, s):
            q0 = (j * LANES) % tq
            return jnp.where(take[:, q0:q0 + LANES] > 0.5, s, NEG)

        kk = kka_ref[rows(c), :]
        vt = vt_ref[0, :, rows(c)]
        for g in range(3):
            _attend_t(kk, qs_ref, g, vt, m_ref, l_ref, acc_ref, mask)
        return tie_run + jnp.sum(tie, axis=0, keepdims=True)

    lax.fori_loop(0, nk, att_body, jnp.zeros((1, tq), F32))

    for g in range(3):
        o_ref[:, g * LANES:(g + 1) * LANES] = _head_pair_out(acc_ref[g], l_ref[g], tq).astype(o_ref.dtype)


def _diff_kernel_t(qb_ref, kb_ref, vb_ref, lam_ref, gsub_ref, o_ref,
                   vt_ref, qs_ref, m_ref, l_ref, acc_ref, *, tq, lam_init, seq):
    i = pl.program_id(1)
    lane = lax.broadcasted_iota(jnp.int32, (tq, LANES), 1)
    zero_b = jnp.zeros((tq, LANES), BF16)

    @pl.when(i == 0)
    def _():
        _transpose_values(vb_ref, vt_ref, seq, tq)

    qb = qb_ref[...]
    for g in range(2):
        qg = qb[:, g * LANES:(g + 1) * LANES]
        for j in range(4):
            qs_ref[g, j * tq:(j + 1) * tq, :] = jnp.where(lane // B_QK_DIM == j, qg, zero_b)
    _init_stats(m_ref, l_ref, acc_ref)

    def rows(c):
        return pl.ds(pl.multiple_of(c * tq, tq), tq)

    def step(c, masked):
        mask = _causal_mask_t(tq, tq) if masked else None
        for g in range(2):
            _attend_t(kb_ref[rows(c), g * LANES:(g + 1) * LANES], qs_ref, g, vt_ref[g, :, rows(c)],
                      m_ref, l_ref, acc_ref, mask)

    def body(c, carry):
        step(c, False)
        return carry

    lax.fori_loop(0, i, body, 0)
    step(i, True)

    lv = lam_ref[...]
    lam = (jnp.exp(jnp.sum(lv[0:1] * lv[1:2], axis=1, keepdims=True))
           - jnp.exp(jnp.sum(lv[2:3] * lv[3:4], axis=1, keepdims=True)) + lam_init)
    gsub = gsub_ref[...]
    for g in range(2):
        acc_t, l = acc_ref[g], l_ref[g]

        def prob(r0, j):
            return acc_t[r0:r0 + HEAD_DIM, j * tq:(j + 1) * tq] / l[:, j * tq:(j + 1) * tq]

        halves = []
        for r0, j in ((0, 0), (HEAD_DIM, 2)):
            o = prob(r0, j) - lam * prob(r0, j + 1)
            ms = jnp.mean(o * o, axis=0, keepdims=True)
            halves.append(o * lax.rsqrt(ms + SUBLN_EPS))
        y = (jnp.concatenate(halves, axis=0).T * gsub) * (1.0 - lam_init)
        o_ref[:, g * LANES:(g + 1) * LANES] = y.astype(o_ref.dtype)


def _moba_kernel_t(qc_ref, kc_ref, vc_ref, o_ref, kmean_ref, vt_ref, qs_ref, bias_ref, m_ref, l_ref, acc_ref,
                   *, tq, nb, n_sel, seq):
    i = pl.program_id(1)
    nbp = kmean_ref.shape[0]
    lane = lax.broadcasted_iota(jnp.int32, (tq, LANES), 1)
    lo = lane < HEAD_DIM
    zero_b = jnp.zeros((tq, LANES), BF16)

    @pl.when(i == 0)
    def _():
        _transpose_values(vc_ref, vt_ref, seq, tq)
        kmean_ref[...] = jnp.zeros(kmean_ref.shape, F32)
        for n in range(nb):
            kblk = kc_ref[n * tq:(n + 1) * tq, :].astype(F32)
            kmean_ref[n:n + 1, :] = jnp.mean(kblk, axis=0, keepdims=True)

    qc = qc_ref[...]
    sub = lax.broadcasted_iota(jnp.int32, (nbp, 2 * tq), 0)
    past = sub < i
    for g in range(3):
        qg = qc[:, g * LANES:(g + 1) * LANES]
        q2 = jnp.concatenate([jnp.where(lo, qg, zero_b), jnp.where(lo, zero_b, qg)], axis=0)
        gt = _dot_nt(kmean_ref[:, g * LANES:(g + 1) * LANES], q2.astype(F32), precision=lax.Precision.HIGHEST)
        rows_ = []
        for n in range(nbp):
            gn = gt[n:n + 1, :]
            beats = jnp.where(sub < n, jnp.where(gt >= gn, 1.0, 0.0), jnp.where(gt > gn, 1.0, 0.0))
            beats = jnp.where(sub == n, 0.0, jnp.where(past, beats, 0.0))
            rank = jnp.sum(beats, axis=0, keepdims=True)
            rows_.append(jnp.where(rank < n_sel, 0.0, NEG))
        bias_ref[g] = jnp.where(past, jnp.concatenate(rows_, axis=0), NEG)
        qs_ref[g] = q2
    _init_stats(m_ref, l_ref, acc_ref)

    def rows(c):
        return pl.ds(pl.multiple_of(c * tq, tq), tq)

    def body(c, carry):
        for g in range(3):
            bias = bias_ref[g, pl.ds(c, 1), :]

            def mask(j, s):
                return s + bias[:, j * LANES:(j + 1) * LANES]

            _attend_t(kc_ref[rows(c), g * LANES:(g + 1) * LANES], qs_ref, g, vt_ref[g, :, rows(c)],
                      m_ref, l_ref, acc_ref, mask)
        return carry

    lax.fori_loop(0, i, body, 0)

    causal = _causal_mask_t(tq, tq)
    for g in range(3):
        _attend_t(kc_ref[rows(i), g * LANES:(g + 1) * LANES], qs_ref, g, vt_ref[g, :, rows(i)],
                  m_ref, l_ref, acc_ref, causal)
        o_ref[:, g * LANES:(g + 1) * LANES] = _head_pair_out(acc_ref[g], l_ref[g], tq).astype(o_ref.dtype)


def _route(h, wr_ref, cw_ref):
    hi = h.astype(BF16)
    lo = (h - hi.astype(F32)).astype(BF16)
    logits = _dot(hi, wr_ref[0]) + (_dot(lo, wr_ref[0]) + _dot(hi, wr_ref[1]))
    lane = lax.broadcasted_iota(jnp.int32, logits.shape, 1)
    lg = jnp.where(lane < N_EXPERTS, logits, -jnp.inf)
    v0 = jnp.max(lg, axis=1, keepdims=True)
    i0 = jnp.min(jnp.where(lg == v0, lane, LANES), axis=1, keepdims=True)
    lg1 = jnp.where(lane == i0, -jnp.inf, lg)
    v1 = jnp.max(lg1, axis=1, keepdims=True)
    i1 = jnp.min(jnp.where(lg1 == v1, lane, LANES), axis=1, keepdims=True)
    e1 = jnp.exp(v1 - v0)
    w0 = 1.0 / (1.0 + e1)
    cw_ref[...] = (jnp.where(lane == 0, i0.astype(F32), 0.0) + jnp.where(lane == 1, i1.astype(F32), 0.0)
                   + jnp.where(lane == 2, w0, 0.0) + jnp.where(lane == 3, e1 * w0, 0.0))


def _outproj_kernel(x_ref, ya_ref, yb_ref, yc_ref, wo_ref, g_ref, mod_ref, *rest, with_router):
    if with_router:
        wr_ref, x1_ref, h_ref, cw_ref = rest
    else:
        x1_ref, h_ref = rest
    y = (_dot(ya_ref[...], wo_ref[0:384, :]) + _dot(yb_ref[...], wo_ref[384:640, :])
         + _dot(yc_ref[...], wo_ref[640:1024, :]))
    x1 = x_ref[...] + mod_ref[2:3, :] * y
    x1_ref[...] = x1
    h = _norm_mod(x1, g_ref[...], mod_ref[3:4, :], mod_ref[4:5, :])
    if with_router:
        h_ref[...] = _pack_rows(h)
        _route(h, wr_ref, cw_ref)
    else:
        h_ref[...] = h.astype(BF16)


def outproj(x, ya, yb, yc, wo, g_ffn, mod, seq, w_router=None, tm=512):
    t, d = x.shape
    tiles_per_batch = seq // tm
    row = lambda i: (i, 0)
    const = lambda i: (0, 0)
    with_router = w_router is not None
    in_specs = [pl.BlockSpec((tm, d), row), pl.BlockSpec((tm, 384), row), pl.BlockSpec((tm, 256), row),
                pl.BlockSpec((tm, 384), row), pl.BlockSpec((d, d), const), pl.BlockSpec((1, d), const),
                pl.BlockSpec((None, 6, d), lambda i: (i // tiles_per_batch, 0, 0))]
    out_specs = [pl.BlockSpec((tm, d), row), pl.BlockSpec((tm, d), row)]
    out_shape = [jax.ShapeDtypeStruct((t, d), F32), jax.ShapeDtypeStruct((t, d), BF16)]
    args = [x, ya, yb, yc, wo, g_ffn, mod]
    if with_router:
        in_specs.append(pl.BlockSpec((2, d, LANES), lambda i: (0, 0, 0)))
        out_specs[1] = pl.BlockSpec((tm, d // 2), row)
        out_shape[1] = jax.ShapeDtypeStruct((t, d // 2), jnp.int32)
        out_specs.append(pl.BlockSpec((tm, LANES), row))
        out_shape.append(jax.ShapeDtypeStruct((t, LANES), F32))
        args.append(w_router)
    return pl.pallas_call(
        functools.partial(_outproj_kernel, with_router=with_router),
        grid=(t // tm,), in_specs=in_specs, out_specs=out_specs, out_shape=out_shape,
        compiler_params=_params("parallel"),
        name="outproj_router" if with_router else "outproj",
    )(*args)


def _swiglu_partial(h, wg_ref, wu_ref, wd_ref):
    a = _dot(h, wg_ref[...])
    u = _dot(h, wu_ref[...])
    act = (a * (1.0 / (1.0 + jnp.exp(-a)))) * u
    return _dot(act.astype(BF16), wd_ref[...])


def _ffn_kernel(x1_ref, h_ref, mod_ref, wg_ref, wu_ref, wd_ref, o_ref, acc_ref):
    f = pl.program_id(1)

    @pl.when(f == 0)
    def _():
        acc_ref[...] = jnp.zeros(acc_ref.shape, F32)

    acc_ref[...] += _swiglu_partial(h_ref[...], wg_ref, wu_ref, wd_ref)

    @pl.when(f == pl.num_programs(1) - 1)
    def _():
        o_ref[...] = x1_ref[...] + mod_ref[5:6, :] * acc_ref[...]


def ffn_dense(x1, h, mod, wg, wu, wd, seq, tm=512, tf=1408):
    t, d = x1.shape
    dff = wg.shape[1]
    tiles_per_batch = seq // tm
    row = lambda i, f: (i, 0)
    return pl.pallas_call(
        _ffn_kernel,
        grid=(t // tm, dff // tf),
        in_specs=[pl.BlockSpec((tm, d), row), pl.BlockSpec((tm, d), row),
                  pl.BlockSpec((None, 6, d), lambda i, f: (i // tiles_per_batch, 0, 0)),
                  pl.BlockSpec((d, tf), lambda i, f: (0, f)), pl.BlockSpec((d, tf), lambda i, f: (0, f)),
                  pl.BlockSpec((tf, d), lambda i, f: (f, 0))],
        out_specs=pl.BlockSpec((tm, d), row),
        out_shape=jax.ShapeDtypeStruct((t, d), F32),
        scratch_shapes=[pltpu.VMEM((tm, d), F32)],
        compiler_params=_params("parallel", "arbitrary"),
        name="ffn_dense",
    )(x1, h, mod, wg, wu, wd)


MOE_TILE = 512
SC_CORES, SC_SUBCORES = 2, 16
SC_ROWS = 64
HI16 = -65536


def _pack_rows(x):
    c = x.shape[1] // 2
    bits = lax.bitcast_convert_type(x.astype(jnp.bfloat16).astype(F32), jnp.int32)
    return lax.shift_right_logical(bits[:, :c], jnp.int32(16)) | (bits[:, c:] & jnp.int32(HI16))


def _unpack_rows(w):
    lo = lax.bitcast_convert_type(lax.shift_left(w, jnp.int32(16)), F32)
    hi = lax.bitcast_convert_type(w & jnp.int32(HI16), F32)
    return jnp.concatenate([lo, hi], axis=1)


def sc_gather_rows(table, idx):
    d = table.shape[1]
    b = idx.shape[0]
    per_worker = b // (SC_CORES * SC_SUBCORES)
    assert per_worker * SC_CORES * SC_SUBCORES == b and per_worker % (2 * SC_ROWS) == 0
    mesh = plsc.VectorSubcoreMesh(core_axis_name="c", subcore_axis_name="s")
    idx_buf = pltpu.VMEM((SC_ROWS,), jnp.int32)
    row_buf = pltpu.VMEM((SC_ROWS, d), table.dtype)

    @functools.partial(
        pl.kernel, mesh=mesh, out_type=jax.ShapeDtypeStruct((b, d), table.dtype),
        scratch_types=[idx_buf, idx_buf, row_buf, row_buf] + [pltpu.SemaphoreType.DMA] * 4,
        name="sc_gather_rows")
    def gather(table_hbm, idx_hbm, out_hbm, idx0, idx1, rows0, rows1, sem_g0, sem_g1, sem_w0, sem_w1):
        base = (lax.axis_index("s") * SC_CORES + lax.axis_index("c")) * per_worker

        @pl.loop(0, per_worker // (2 * SC_ROWS))
        def _(pair):
            off0 = pl.multiple_of(base + pair * (2 * SC_ROWS), SC_ROWS)
            off1 = pl.multiple_of(off0 + SC_ROWS, SC_ROWS)
            pltpu.sync_copy(idx_hbm.at[pl.ds(off0, SC_ROWS)], idx0)
            pltpu.sync_copy(idx_hbm.at[pl.ds(off1, SC_ROWS)], idx1)
            gather0 = pltpu.async_copy(table_hbm.at[idx0], rows0, sem_g0)
            gather1 = pltpu.async_copy(table_hbm.at[idx1], rows1, sem_g1)
            gather0.wait()
            write0 = pltpu.async_copy(rows0, out_hbm.at[pl.ds(off0, SC_ROWS)], sem_w0)
            gather1.wait()
            write1 = pltpu.async_copy(rows1, out_hbm.at[pl.ds(off1, SC_ROWS)], sem_w1)
            write0.wait()
            write1.wait()

    return gather(table, idx)


def _moe_layout(e0, e1, t, n_exp):
    experts = jnp.arange(n_exp, dtype=jnp.int32)
    routed = ((e0[:, None] == experts) | (e1[:, None] == experts)).astype(jnp.int32)
    csum = jnp.cumsum(routed, axis=0)
    padded = (csum[-1] + MOE_TILE - 1) // MOE_TILE * MOE_TILE
    ends = jnp.cumsum(padded)
    pos = (ends - padded)[None, :] + csum - routed
    pos0 = jnp.take_along_axis(pos, e0[:, None], axis=1)[:, 0]
    pos1 = jnp.take_along_axis(pos, e1[:, None], axis=1)[:, 0]
    n_rows = 2 * t + n_exp * MOE_TILE
    tok = jnp.arange(t, dtype=jnp.int32)
    src = jnp.zeros((n_rows,), jnp.int32).at[jnp.concatenate([pos0, pos1])].set(
        jnp.concatenate([tok, tok]), unique_indices=True)
    n_tiles = n_rows // MOE_TILE
    n_valid = ends[-1] // MOE_TILE
    tile_expert = jnp.sum(jnp.arange(n_tiles)[:, None] * MOE_TILE >= ends[None, :], axis=1)
    tile_expert = tile_expert[jnp.minimum(jnp.arange(n_tiles), n_valid - 1)].astype(jnp.int32)
    return pos0, pos1, src, tile_expert, n_valid.astype(jnp.int32).reshape(1)


def _experts_packed_kernel(te_ref, nv_ref, x_ref, wg_ref, wu_ref, wd_ref, o_ref, acc_ref):
    j, f = pl.program_id(0), pl.program_id(1)
    valid = j < nv_ref[0]

    @pl.when(valid & (f == 0))
    def _():
        acc_ref[...] = jnp.zeros(acc_ref.shape, F32)

    @pl.when(valid)
    def _():
        acc_ref[...] += _swiglu_partial(_unpack_rows(x_ref[...]).astype(BF16), wg_ref, wu_ref, wd_ref)

    @pl.when(valid & (f == pl.num_programs(1) - 1))
    def _():
        o_ref[...] = _pack_rows(acc_ref[...])


def moe_experts_packed(x_sorted, tile_expert, n_valid, wg, wu, wd, tf=1408):
    n_rows, half = x_sorted.shape
    d = 2 * half
    nf = wg.shape[2] // tf

    def tile(j, nv):
        return jnp.minimum(j, nv[0] - 1)

    def fstep(j, f, nv):
        return jnp.where(j < nv[0], f, nf - 1)

    return pl.pallas_call(
        _experts_packed_kernel,
        grid_spec=pltpu.PrefetchScalarGridSpec(
            num_scalar_prefetch=2, grid=(n_rows // MOE_TILE, nf),
            in_specs=[pl.BlockSpec((MOE_TILE, half), lambda j, f, te, nv: (tile(j, nv), 0)),
                      pl.BlockSpec((None, d, tf), lambda j, f, te, nv: (te[j], 0, fstep(j, f, nv))),
                      pl.BlockSpec((None, d, tf), lambda j, f, te, nv: (te[j], 0, fstep(j, f, nv))),
                      pl.BlockSpec((None, tf, d), lambda j, f, te, nv: (te[j], fstep(j, f, nv), 0))],
            out_specs=pl.BlockSpec((MOE_TILE, half), lambda j, f, te, nv: (tile(j, nv), 0)),
            scratch_shapes=[pltpu.VMEM((MOE_TILE, d), F32)]),
        out_shape=jax.ShapeDtypeStruct((n_rows, half), jnp.int32),
        compiler_params=_params("arbitrary", "arbitrary"),
        name="moe_experts",
    )(tile_expert, n_valid, x_sorted, wg, wu, wd)


def _combine_packed_kernel(y0_ref, y1_ref, route_ref, x1_ref, mod_ref, o_ref):
    rt = route_ref[...]
    f = rt[:, 2:3] * _unpack_rows(y0_ref[...]) + rt[:, 3:4] * _unpack_rows(y1_ref[...])
    o_ref[...] = x1_ref[...] + mod_ref[5:6, :] * f


def moe_combine_packed(y_pairs, route, x1, mod, seq, tm=512):
    t, d = x1.shape
    nt = t // tm
    tiles_per_batch = seq // tm
    row = lambda i: (i, 0)
    return pl.pallas_call(
        _combine_packed_kernel,
        grid=(nt,),
        in_specs=[pl.BlockSpec((tm, d // 2), row), pl.BlockSpec((tm, d // 2), lambda i: (i + nt, 0)),
                  pl.BlockSpec((tm, LANES), row), pl.BlockSpec((tm, d), row),
                  pl.BlockSpec((None, 6, d), lambda i: (i // tiles_per_batch, 0, 0))],
        out_specs=pl.BlockSpec((tm, d), row),
        out_shape=jax.ShapeDtypeStruct((t, d), F32),
        compiler_params=_params("parallel"),
        name="moe_combine",
    )(y_pairs, y_pairs, route, x1, mod)


def ffn_moe_sc(x1, h_packed, route, mod, wg, wu, wd, seq):
    t = x1.shape[0]
    e0, e1 = route[:, 0].astype(jnp.int32), route[:, 1].astype(jnp.int32)
    pos0, pos1, src, tile_expert, n_valid = _moe_layout(e0, e1, t, wg.shape[0])
    h_sorted = sc_gather_rows(h_packed, src)
    y_sorted = moe_experts_packed(h_sorted, tile_expert, n_valid, wg, wu, wd)
    y_pairs = sc_gather_rows(y_sorted, jnp.concatenate([pos0, pos1]))
    return moe_combine_packed(y_pairs, route, x1, mod, seq)


def _final_norm_kernel(x_ref, g_ref, o_ref):
    x = x_ref[...]
    o_ref[...] = (x * lax.rsqrt(jnp.mean(x * x, axis=-1, keepdims=True) + EPS)) * g_ref[...]


def final_norm(x, g, tm=512):
    t, d = x.shape
    return pl.pallas_call(
        _final_norm_kernel,
        grid=(t // tm,),
        in_specs=[pl.BlockSpec((tm, d), lambda i: (i, 0)), pl.BlockSpec((1, d), lambda i: (0, 0))],
        out_specs=pl.BlockSpec((tm, d), lambda i: (i, 0)),
        out_shape=jax.ShapeDtypeStruct((t, d), F32),
        compiler_params=_params("parallel"),
        name="final_norm",
    )(x, g)


def _rope_tables(positions, dim):
    rot = dim // ROPE_FRACTION
    half = rot // 2
    inv = 1.0 / (ROPE_THETA ** (np.arange(0, rot, 2, dtype=np.float32) / rot))
    ang = positions.reshape(-1).astype(F32)[:, None] * jnp.asarray(inv, F32)
    cos, sin = jnp.cos(ang), jnp.sin(ang)
    t = ang.shape[0]
    ones = jnp.ones((t, dim - rot), F32)
    zeros = lambda w: jnp.zeros((t, w), F32)
    reps = LANES // dim
    c = jnp.tile(jnp.concatenate([cos, cos, ones], axis=1), (1, reps))
    sa = jnp.tile(jnp.concatenate([-sin, zeros(dim - half)], axis=1), (1, reps))
    sb = jnp.tile(jnp.concatenate([zeros(half), sin, zeros(dim - rot)], axis=1), (1, reps))
    return c, sa, sb


def _relayout_w_in(w):
    d = w.shape[0]
    pts = np.cumsum([0, 384, 64, 64, 256, 64, 4, 256, 256, 256, 384, 384, 384])
    (q_a, k_a, v_a, q_i, k_i, w_i, q_b, k_b, v_b, q_c, k_c, v_c) = [w[:, pts[j]:pts[j + 1]] for j in range(12)]
    w_i_pad = jnp.concatenate([w_i, jnp.zeros((d, LANES - IDX_HEADS), w.dtype)], axis=1)
    return jnp.concatenate([q_a, k_a, k_a, v_a, v_a, q_i, k_i, k_i, w_i_pad,
                            q_b, k_b, v_b, q_c, k_c, v_c], axis=1).astype(BF16)


def kernel(x, c, positions, w_in, w_out, diff_lambda, diff_subln, w_ada, b_ada, g_attn, g_ffn, w_ff_gate,
           w_ff_up, w_ff_down, w_router, w_exp_gate, w_exp_up, w_exp_down, g_final):
    batch, seq, d = x.shape
    depth = w_in.shape[0]
    t = batch * seq
    tabs = _rope_tables(positions, HEAD_DIM) + _rope_tables(positions, B_QK_DIM)
    mod_all = adaln_mod(c, w_ada, b_ada).reshape(depth, batch, 6, d)
    xf = x.reshape(t, d)
    for layer in range(depth):
        mod = mod_all[layer]
        lam_init = 0.8 - 0.6 * math.exp(-0.3 * layer)
        (qa, kka, vva, qi, kki, wi, qb, kb, vb, qc, kc, vc) = inproj(
            xf, g_attn[layer].reshape(1, d), mod, tabs, _relayout_w_in(w_in[layer]), seq)
        ya = dsa_attention(qa, qi, wi, kka, vva, kki, batch, seq)
        g_sub2 = jnp.tile(diff_subln[layer], 2).reshape(1, LANES)
        yb = diff_attention(qb, kb, vb, diff_lambda[layer], g_sub2, lam_init, batch, seq)
        yc = moba_attention(qc, kc, vc, batch, seq)
        j = layer // 2
        wo = w_out[layer].astype(BF16)
        gf = g_ffn[layer].reshape(1, d)
        if layer % 2 == 0:
            x1, h = outproj(xf, ya, yb, yc, wo, gf, mod, seq)
            xf = ffn_dense(x1, h, mod, w_ff_gate[j].astype(BF16), w_ff_up[j].astype(BF16),
                           w_ff_down[j].astype(BF16), seq)
        else:
            wr = jnp.concatenate([w_router[j], jnp.zeros((d, LANES - N_EXPERTS), F32)], axis=1)
            wr_hi = wr.astype(BF16)
            wr = jnp.stack([wr_hi, (wr - wr_hi.astype(F32)).astype(BF16)])
            x1, h_packed, route = outproj(xf, ya, yb, yc, wo, gf, mod, seq, w_router=wr)
            xf = ffn_moe_sc(x1, h_packed, route, mod, w_exp_gate[j].astype(BF16), w_exp_up[j].astype(BF16),
                            w_exp_down[j].astype(BF16), seq)
    return final_norm(xf, g_final.reshape(1, d)).reshape(batch, seq, d)
```

```python
import functools
import math

import jax
import jax.numpy as jnp
import numpy as np
from jax import lax
from jax.experimental import pallas as pl
from jax.experimental.pallas import tpu as pltpu
from jax.experimental.pallas import tpu_sc as plsc

F32 = jnp.float32
BF16 = jnp.bfloat16

HEAD_DIM = 64
A_HEADS = 6
IDX_HEADS = 4
B_HEADS = 4
B_QK_DIM = 32
C_HEADS = 6
DSA_TOPK_MAX = 256
MOBA_BLOCK = 256
MOBA_TOPK = 3
ROPE_THETA = 500000.0
ROPE_FRACTION = 4
SUBLN_EPS = 1e-5
EPS = 1e-6
N_EXPERTS = 8

LANES = 128
NEG = -1e30
INT_MIN = -2 ** 31
MIN_NORMAL_KEY = 0x00800000
MIN_NORMAL_F32 = float(np.float32(2.0 ** -126))
VMEM_LIMIT = 48 * 1024 * 1024
LOG2E = math.log2(math.e)

_G_QA, _G_KKA, _G_VVA, _G_QI, _G_KKI, _G_WI = (0, 384), (384, 512), (512, 640), (640, 896), (896, 1024), (1024, 1152)
_G_QB, _G_KB, _G_VB = (1152, 1408), (1408, 1664), (1664, 1920)
_G_QC, _G_KC, _G_VC = (1920, 2304), (2304, 2688), (2688, 3072)
D_IN_PAD = 3072


def _params(*sem):
    return pltpu.CompilerParams(dimension_semantics=sem, vmem_limit_bytes=VMEM_LIMIT)


def _dot(a, b):
    return jnp.dot(a, b, preferred_element_type=F32)


def _dot_nt(a, b, precision=None):
    return lax.dot_general(a, b, (((1,), (1,)), ((), ())), preferred_element_type=F32, precision=precision)


def _adaln_kernel(c_ref, w_ref, b_ref, o_ref):
    c = c_ref[...]
    c_act = c * (1.0 / (1.0 + jnp.exp(-c)))
    o_ref[...] = jnp.dot(c_act, w_ref[...], preferred_element_type=F32,
                         precision=lax.Precision.HIGHEST) + b_ref[...]


def adaln_mod(c, w_ada, b_ada, tn=1536):
    depth, d, n = w_ada.shape
    b = c.shape[0]
    return pl.pallas_call(
        _adaln_kernel,
        grid=(depth, n // tn),
        in_specs=[pl.BlockSpec((b, d), lambda l, j: (0, 0)),
                  pl.BlockSpec((None, d, tn), lambda l, j: (l, 0, j)),
                  pl.BlockSpec((None, 1, tn), lambda l, j: (l, 0, j))],
        out_specs=pl.BlockSpec((None, b, tn), lambda l, j: (l, 0, j)),
        out_shape=jax.ShapeDtypeStruct((depth, b, n), F32),
        compiler_params=_params("parallel", "parallel"),
        name="adaln_mod",
    )(c, w_ada, b_ada.reshape(depth, 1, n))


def _norm_mod(x, g, shift, scale, eps=EPS):
    y = x * lax.rsqrt(jnp.mean(x * x, axis=-1, keepdims=True) + eps)
    return (y * g) * (1.0 + scale) + shift


def _rope_store(acc, o_ref, cos, sa, sb, half):
    for j in range(acc.shape[1] // LANES):
        a = acc[:, j * LANES:(j + 1) * LANES]
        r = a * cos + pltpu.roll(a, half, 1) * sb + pltpu.roll(a, LANES - half, 1) * sa
        o_ref[:, j * LANES:(j + 1) * LANES] = r.astype(o_ref.dtype)


def _inproj_kernel(x_ref, g_ref, mod_ref, c64_ref, sa64_ref, sb64_ref, c32_ref, sa32_ref, sb32_ref, w_ref,
                   qa_ref, kka_ref, vva_ref, qi_ref, kki_ref, wi_ref,
                   qb_ref, kb_ref, vb_ref, qc_ref, kc_ref, vc_ref):
    h = _norm_mod(x_ref[...], g_ref[...], mod_ref[0:1, :], mod_ref[1:2, :]).astype(BF16)
    c64, sa64, sb64 = c64_ref[...], sa64_ref[...], sb64_ref[...]
    c32, sa32, sb32 = c32_ref[...], sa32_ref[...], sb32_ref[...]

    def proj(cols):
        return _dot(h, w_ref[:, cols[0]:cols[1]])

    qk_scale = HEAD_DIM ** -0.5 * LOG2E
    _rope_store(proj(_G_QA), qa_ref, c64 * qk_scale, sa64 * qk_scale, sb64 * qk_scale, 8)
    _rope_store(proj(_G_KKA), kka_ref, c64, sa64, sb64, 8)
    vva_ref[...] = proj(_G_VVA).astype(vva_ref.dtype)
    _rope_store(proj(_G_QI), qi_ref, c64, sa64, sb64, 8)
    _rope_store(proj(_G_KKI), kki_ref, c64, sa64, sb64, 8)
    wi_ref[...] = proj(_G_WI) * (IDX_HEADS ** -0.5 * HEAD_DIM ** -0.5)
    b_scale = B_QK_DIM ** -0.5 * LOG2E
    _rope_store(proj(_G_QB), qb_ref, c32 * b_scale, sa32 * b_scale, sb32 * b_scale, 4)
    _rope_store(proj(_G_KB), kb_ref, c32, sa32, sb32, 4)
    vb_ref[...] = proj(_G_VB).astype(vb_ref.dtype)
    _rope_store(proj(_G_QC), qc_ref, c64 * qk_scale, sa64 * qk_scale, sb64 * qk_scale, 8)
    _rope_store(proj(_G_KC), kc_ref, c64, sa64, sb64, 8)
    vc_ref[...] = proj(_G_VC).astype(vc_ref.dtype)


def inproj(x, g, mod, tabs, w_pad, seq, tm=512):
    t, d = x.shape
    tiles_per_batch = seq // tm
    row = lambda i: (i, 0)
    widths = [384, 128, 128, 256, 128, 128, 256, 256, 256, 384, 384, 384]
    dtypes = [BF16, BF16, BF16, BF16, BF16, F32, BF16, BF16, BF16, BF16, BF16, BF16]
    return pl.pallas_call(
        _inproj_kernel,
        grid=(t // tm,),
        in_specs=[pl.BlockSpec((tm, d), row),
                  pl.BlockSpec((1, d), lambda i: (0, 0)),
                  pl.BlockSpec((None, 6, d), lambda i: (i // tiles_per_batch, 0, 0))]
                 + [pl.BlockSpec((tm, LANES), row)] * 6
                 + [pl.BlockSpec((d, D_IN_PAD), lambda i: (0, 0))],
        out_specs=[pl.BlockSpec((tm, w), row) for w in widths],
        out_shape=[jax.ShapeDtypeStruct((t, w), dt) for w, dt in zip(widths, dtypes)],
        compiler_params=_params("parallel"),
        name="inproj",
    )(x, g, mod, *tabs, w_pad)


def _init_stats(m_ref, l_ref, acc_ref):
    m_ref[...] = jnp.full(m_ref.shape, -jnp.inf, F32)
    l_ref[...] = jnp.zeros(l_ref.shape, F32)
    acc_ref[...] = jnp.zeros(acc_ref.shape, F32)


def _key_to_f32(k):
    return lax.bitcast_convert_type(jnp.where(k >= 0, k, k ^ 0x7FFFFFFF), F32)


def dsa_attention(qa, qi, wi, kka, vva, kki, batch, seq, tq=256):
    t = qa.shape[0]
    nq = seq // tq
    k_top = min(DSA_TOPK_MAX, seq // 4)
    qrow = lambda b, i: (b * nq + i, 0)
    full = lambda b, i: (b, 0)
    return pl.pallas_call(
        functools.partial(_dsa_kernel_t, tq=tq, k_top=k_top, seq=seq),
        grid=(batch, nq),
        in_specs=[pl.BlockSpec((tq, 384), qrow), pl.BlockSpec((tq, 256), qrow), pl.BlockSpec((tq, LANES), qrow),
                  pl.BlockSpec((seq, LANES), full), pl.BlockSpec((seq, LANES), full),
                  pl.BlockSpec((seq, LANES), full)],
        out_specs=pl.BlockSpec((tq, 384), qrow),
        out_shape=jax.ShapeDtypeStruct((t, 384), BF16),
        scratch_shapes=[pltpu.VMEM((seq, tq), F32), pltpu.VMEM((seq, tq), jnp.int16),
                        pltpu.VMEM((seq, tq), jnp.int16), pltpu.VMEM((1, LANES, seq), BF16),
                        pltpu.VMEM((3, 2 * tq, LANES), BF16),
                        pltpu.VMEM((3, 1, 2 * tq), F32), pltpu.VMEM((3, 1, 2 * tq), F32),
                        pltpu.VMEM((3, LANES, 2 * tq), F32)],
        compiler_params=_params("parallel", "arbitrary"),
        name="dsa_attention",
    )(qa, qi, wi, kka, vva, kki)


def diff_attention(qb, kb, vb, lam_vec, g_sub2, lam_init, batch, seq, tq=256):
    t = qb.shape[0]
    nq = seq // tq
    qrow = lambda b, i: (b * nq + i, 0)
    full = lambda b, i: (b, 0)
    const = lambda b, i: (0, 0)
    return pl.pallas_call(
        functools.partial(_diff_kernel_t, tq=tq, lam_init=lam_init, seq=seq),
        grid=(batch, nq),
        in_specs=[pl.BlockSpec((tq, 256), qrow), pl.BlockSpec((seq, 256), full), pl.BlockSpec((seq, 256), full),
                  pl.BlockSpec((4, B_QK_DIM), const), pl.BlockSpec((1, LANES), const)],
        out_specs=pl.BlockSpec((tq, 256), qrow),
        out_shape=jax.ShapeDtypeStruct((t, 256), BF16),
        scratch_shapes=[pltpu.VMEM((2, LANES, seq), BF16), pltpu.VMEM((2, 4 * tq, LANES), BF16),
                        pltpu.VMEM((2, 1, 4 * tq), F32), pltpu.VMEM((2, 1, 4 * tq), F32),
                        pltpu.VMEM((2, LANES, 4 * tq), F32)],
        compiler_params=_params("parallel", "arbitrary"),
        name="diff_attention",
    )(qb, kb, vb, lam_vec, g_sub2)


def moba_attention(qc, kc, vc, batch, seq):
    tq = MOBA_BLOCK
    t = qc.shape[0]
    nb = seq // tq
    n_sel = min(MOBA_TOPK, nb - 1)
    nbp = 8
    assert seq % tq == 0 and nb <= nbp
    qrow = lambda b, i: (b * nb + i, 0)
    full = lambda b, i: (b, 0)
    return pl.pallas_call(
        functools.partial(_moba_kernel_t, tq=tq, nb=nb, n_sel=n_sel, seq=seq),
        grid=(batch, nb),
        in_specs=[pl.BlockSpec((tq, 384), qrow), pl.BlockSpec((seq, 384), full), pl.BlockSpec((seq, 384), full)],
        out_specs=pl.BlockSpec((tq, 384), qrow),
        out_shape=jax.ShapeDtypeStruct((t, 384), BF16),
        scratch_shapes=[pltpu.VMEM((nbp, 384), F32), pltpu.VMEM((3, LANES, seq), BF16),
                        pltpu.VMEM((3, 2 * tq, LANES), BF16), pltpu.VMEM((3, nbp, 2 * tq), F32),
                        pltpu.VMEM((3, 1, 2 * tq), F32), pltpu.VMEM((3, 1, 2 * tq), F32),
                        pltpu.VMEM((3, LANES, 2 * tq), F32)],
        compiler_params=_params("parallel", "arbitrary"),
        name="moba_attention",
    )(qc, kc, vc)


def _attend_t(k, q_ref, g, v_t, m_ref, l_ref, acc_ref, mask=None):
    for j in range(q_ref.shape[1] // LANES):
        cols = slice(j * LANES, (j + 1) * LANES)
        s = _dot_nt(k, q_ref[g, cols, :])
        if mask is not None:
            s = mask(j, s)
        m_prev = m_ref[g, :, cols]
        m_new = jnp.maximum(m_prev, jnp.max(s, axis=0, keepdims=True))
        alpha = jnp.exp2(m_prev - m_new)
        p = jnp.exp2(s - m_new)
        l_ref[g, :, cols] = alpha * l_ref[g, :, cols] + jnp.sum(p, axis=0, keepdims=True)
        acc_ref[g, :, cols] = alpha * acc_ref[g, :, cols] + _dot(v_t, p.astype(BF16))
        m_ref[g, :, cols] = m_new


def _causal_mask_t(tk, tq):
    key = lax.broadcasted_iota(jnp.int32, (tk, LANES), 0)
    qry = lax.broadcasted_iota(jnp.int32, (tk, LANES), 1)

    def mask(j, s):
        return jnp.where(key <= qry + (j * LANES) % tq, s, NEG)
    return mask


def _transpose_values(v_ref, vt_ref, seq, tk):
    for g in range(vt_ref.shape[0]):
        for n in range(seq // tk):
            blk = v_ref[n * tk:(n + 1) * tk, g * LANES:(g + 1) * LANES].astype(F32)
            vt_ref[g, :, n * tk:(n + 1) * tk] = blk.T.astype(vt_ref.dtype)


def _head_pair_out(acc_t, l, tq):
    even = acc_t[0:HEAD_DIM, 0:tq] / l[:, 0:tq]
    odd = acc_t[HEAD_DIM:LANES, tq:2 * tq] / l[:, tq:2 * tq]
    return jnp.concatenate([even, odd], axis=0).T


def _dsa_kernel_t(qa_ref, qi_ref, wiq_ref, kka_ref, vva_ref, kki_ref, o_ref,
                  sc_ref, hi_ref, lo_ref, vt_ref, qs_ref, m_ref, l_ref, acc_ref, *, tq, k_top, seq):
    i = pl.program_id(1)
    nk = i + 1
    t0 = i * tq
    lo = lax.broadcasted_iota(jnp.int32, (tq, LANES), 1) < HEAD_DIM
    zero_b = jnp.zeros((tq, LANES), BF16)

    @pl.when(i == 0)
    def _():
        _transpose_values(vva_ref, vt_ref, seq, tq)

    def stack_heads(q):
        out = []
        for g in range(q.shape[1] // LANES):
            qg = q[:, g * LANES:(g + 1) * LANES]
            out += [jnp.where(lo, qg, zero_b), jnp.where(lo, zero_b, qg)]
        return out

    qa_stack = stack_heads(qa_ref[...])
    for g in range(3):
        qs_ref[g, 0:tq, :] = qa_stack[2 * g]
        qs_ref[g, tq:2 * tq, :] = qa_stack[2 * g + 1]
    qi_stack = jnp.concatenate(stack_heads(qi_ref[...]), axis=0)
    wi_t = wiq_ref[...].T

    def rows(c):
        return pl.ds(pl.multiple_of(c * tq, tq), tq)

    key_pos = lax.broadcasted_iota(jnp.int32, (tq, tq), 0)
    qry_pos = lax.broadcasted_iota(jnp.int32, (tq, tq), 1)

    def idx_body(c, carry):
        r = jnp.maximum(_dot_nt(kki_ref[rows(c), :], qi_stack), 0.0)
        s = wi_t[0:1, :] * r[:, 0:tq]
        for h in range(1, IDX_HEADS):
            s = s + wi_t[h:h + 1, :] * r[:, h * tq:(h + 1) * tq]
        causal = (c * tq + key_pos) <= (t0 + qry_pos)
        s = jnp.where(causal, s, -jnp.inf)
        s = jnp.where(jnp.abs(s) < MIN_NORMAL_F32, 0.0, s)
        sc_ref[rows(c), :] = s
        bits = lax.bitcast_convert_type(s, jnp.int32)
        key = jnp.where(bits >= 0, bits, bits ^ 0x7FFFFFFF)
        hi_ref[rows(c), :] = lax.shift_right_arithmetic(key, jnp.int32(16)).astype(jnp.int16)
        lo_ref[rows(c), :] = ((key & 0xFFFF) - 32768).astype(jnp.int16)
        return carry

    lax.fori_loop(0, nk, idx_body, 0)

    def count16(ref, cand, strict):
        c16 = cand.astype(jnp.int16)

        def body(c, acc):
            x = ref[rows(c), :]
            hit = jnp.where((x > c16) if strict else (x >= c16), jnp.int16(1), jnp.int16(0))
            for r in range(tq // 16):
                acc = acc + hit[r * 16:(r + 1) * 16]
            return acc
        acc = lax.fori_loop(0, nk, body, jnp.zeros((16, tq), jnp.int16))
        return jnp.sum(acc.astype(F32), axis=0, keepdims=True)

    def search16(ref, k_need):
        v0 = jnp.where(count16(ref, jnp.zeros((1, tq), jnp.int32), False) >= k_need, 0, -32768).astype(jnp.int32)

        def bisect(b, v):
            trial = v | lax.shift_left(jnp.int32(1), 14 - b)
            return jnp.where(count16(ref, trial, False) >= k_need, trial, v)
        return lax.fori_loop(0, 15, bisect, v0)

    kf = float(k_top)

    def search():
        hi_k = search16(hi_ref, kf)
        above = count16(hi_ref, hi_k, True)
        hi16 = hi_k.astype(jnp.int16)

        def keep_bucket(c, carry):
            lo_ref[rows(c), :] = jnp.where(hi_ref[rows(c), :] == hi16, lo_ref[rows(c), :], jnp.int16(-32768))
            return carry
        lax.fori_loop(0, nk, keep_bucket, 0)
        lo_k = search16(lo_ref, kf - above)
        n_gt = above + count16(lo_ref, lo_k, True)
        return lax.shift_left(hi_k, jnp.int32(16)) | (lo_k + 32768), kf - n_gt

    key, need = lax.cond(t0 + tq <= k_top,
                         lambda: (jnp.full((1, tq), INT_MIN, jnp.int32), jnp.full((1, tq), kf, F32)), search)
    thr = _key_to_f32(key)
    thr_next = _key_to_f32(jnp.where(key == 0, MIN_NORMAL_KEY, key + 1))
    all_sel = (t0 + lax.broadcasted_iota(jnp.int32, (1, tq), 1)) < k_top
    lower = (qry_pos <= key_pos).astype(BF16)

    _init_stats(m_ref, l_ref, acc_ref)

    def att_body(c, tie_run):
        s_idx = sc_ref[rows(c), :]
        ge = s_idx >= thr
        gt = s_idx >= thr_next
        tie = jnp.where(gt, 0.0, jnp.where(ge, 1.0, 0.0))
        prefix = _dot(lower, tie.astype(BF16)) + tie_run
        take = jnp.where(gt, 1.0, jnp.where(prefix <= need, tie, 0.0))
        take = jnp.where(all_sel, 1.0, take)
        take = jnp.where(s_idx > -jnp.inf, take, 0.0)

        def mask(j, s):
            q0 = (j * LANES) % tq
            return jnp.where(take[:, q0:q0 + LANES] > 0.5, s, NEG)

        kk = kka_ref[rows(c), :]
        vt = vt_ref[0, :, rows(c)]
        for g in range(3):
            _attend_t(kk, qs_ref, g, vt, m_ref, l_ref, acc_ref, mask)
        return tie_run + jnp.sum(tie, axis=0, keepdims=True)

    lax.fori_loop(0, nk, att_body, jnp.zeros((1, tq), F32))

    for g in range(3):
        o_ref[:, g * LANES:(g + 1) * LANES] = _head_pair_out(acc_ref[g], l_ref[g], tq).astype(o_ref.dtype)


def _diff_kernel_t(qb_ref, kb_ref, vb_ref, lam_ref, gsub_ref, o_ref,
                   vt_ref, qs_ref, m_ref, l_ref, acc_ref, *, tq, lam_init, seq):
    i = pl.program_id(1)
    lane = lax.broadcasted_iota(jnp.int32, (tq, LANES), 1)
    zero_b = jnp.zeros((tq, LANES), BF16)

    @pl.when(i == 0)
    def _():
        _transpose_values(vb_ref, vt_ref, seq, tq)

    qb = qb_ref[...]
    for g in range(2):
        qg = qb[:, g * LANES:(g + 1) * LANES]
        for j in range(4):
            qs_ref[g, j * tq:(j + 1) * tq, :] = jnp.where(lane // B_QK_DIM == j, qg, zero_b)
    _init_stats(m_ref, l_ref, acc_ref)

    def rows(c):
        return pl.ds(pl.multiple_of(c * tq, tq), tq)

    def step(c, masked):
        mask = _causal_mask_t(tq, tq) if masked else None
        for g in range(2):
            _attend_t(kb_ref[rows(c), g * LANES:(g + 1) * LANES], qs_ref, g, vt_ref[g, :, rows(c)],
                      m_ref, l_ref, acc_ref, mask)

    def body(c, carry):
        step(c, False)
        return carry

    lax.fori_loop(0, i, body, 0)
    step(i, True)

    lv = lam_ref[...]
    lam = (jnp.exp(jnp.sum(lv[0:1] * lv[1:2], axis=1, keepdims=True))
           - jnp.exp(jnp.sum(lv[2:3] * lv[3:4], axis=1, keepdims=True)) + lam_init)
    gsub = gsub_ref[...]
    for g in range(2):
        acc_t, l = acc_ref[g], l_ref[g]

        def prob(r0, j):
            return acc_t[r0:r0 + HEAD_DIM, j * tq:(j + 1) * tq] / l[:, j * tq:(j + 1) * tq]

        halves = []
        for r0, j in ((0, 0), (HEAD_DIM, 2)):
            o = prob(r0, j) - lam * prob(r0, j + 1)
            ms = jnp.mean(o * o, axis=0, keepdims=True)
            halves.append(o * lax.rsqrt(ms + SUBLN_EPS))
        y = (jnp.concatenate(halves, axis=0).T * gsub) * (1.0 - lam_init)
        o_ref[:, g * LANES:(g + 1) * LANES] = y.astype(o_ref.dtype)


def _moba_kernel_t(qc_ref, kc_ref, vc_ref, o_ref, kmean_ref, vt_ref, qs_ref, bias_ref, m_ref, l_ref, acc_ref,
                   *, tq, nb, n_sel, seq):
    i = pl.program_id(1)
    nbp = kmean_ref.shape[0]
    lane = lax.broadcasted_iota(jnp.int32, (tq, LANES), 1)
    lo = lane < HEAD_DIM
    zero_b = jnp.zeros((tq, LANES), BF16)

    @pl.when(i == 0)
    def _():
        _transpose_values(vc_ref, vt_ref, seq, tq)
        kmean_ref[...] = jnp.zeros(kmean_ref.shape, F32)
        for n in range(nb):
            kblk = kc_ref[n * tq:(n + 1) * tq, :].astype(F32)
            kmean_ref[n:n + 1, :] = jnp.mean(kblk, axis=0, keepdims=True)

    qc = qc_ref[...]
    sub = lax.broadcasted_iota(jnp.int32, (nbp, 2 * tq), 0)
    past = sub < i
    for g in range(3):
        qg = qc[:, g * LANES:(g + 1) * LANES]
        q2 = jnp.concatenate([jnp.where(lo, qg, zero_b), jnp.where(lo, zero_b, qg)], axis=0)
        km = kmean_ref[:, g * LANES:(g + 1) * LANES]
        km_hi = km.astype(BF16)
        gt = _dot_nt(km_hi, q2) + _dot_nt((km - km_hi.astype(F32)).astype(BF16), q2)
        rows_ = []
        for n in range(nbp):
            gn = gt[n:n + 1, :]
            beats = jnp.where(sub < n, jnp.where(gt >= gn, 1.0, 0.0), jnp.where(gt > gn, 1.0, 0.0))
            beats = jnp.where(sub == n, 0.0, jnp.where(past, beats, 0.0))
            rank = jnp.sum(beats, axis=0, keepdims=True)
            rows_.append(jnp.where(rank < n_sel, 0.0, NEG))
        bias_ref[g] = jnp.where(past, jnp.concatenate(rows_, axis=0), NEG)
        qs_ref[g] = q2
    _init_stats(m_ref, l_ref, acc_ref)

    def rows(c):
        return pl.ds(pl.multiple_of(c * tq, tq), tq)

    def body(c, carry):
        for g in range(3):
            bias = bias_ref[g, pl.ds(c, 1), :]

            def mask(j, s):
                return s + bias[:, j * LANES:(j + 1) * LANES]

            _attend_t(kc_ref[rows(c), g * LANES:(g + 1) * LANES], qs_ref, g, vt_ref[g, :, rows(c)],
                      m_ref, l_ref, acc_ref, mask)
        return carry

    lax.fori_loop(0, i, body, 0)

    causal = _causal_mask_t(tq, tq)
    for g in range(3):
        _attend_t(kc_ref[rows(i), g * LANES:(g + 1) * LANES], qs_ref, g, vt_ref[g, :, rows(i)],
                  m_ref, l_ref, acc_ref, causal)
        o_ref[:, g * LANES:(g + 1) * LANES] = _head_pair_out(acc_ref[g], l_ref[g], tq).astype(o_ref.dtype)


def _route(h, wr_ref, cw_ref):
    hi = h.astype(BF16)
    lo = (h - hi.astype(F32)).astype(BF16)
    logits = _dot(hi, wr_ref[0]) + (_dot(lo, wr_ref[0]) + _dot(hi, wr_ref[1]))
    lane = lax.broadcasted_iota(jnp.int32, logits.shape, 1)
    lg = jnp.where(lane < N_EXPERTS, logits, -jnp.inf)
    v0 = jnp.max(lg, axis=1, keepdims=True)
    i0 = jnp.min(jnp.where(lg == v0, lane, LANES), axis=1, keepdims=True)
    lg1 = jnp.where(lane == i0, -jnp.inf, lg)
    v1 = jnp.max(lg1, axis=1, keepdims=True)
    i1 = jnp.min(jnp.where(lg1 == v1, lane, LANES), axis=1, keepdims=True)
    e1 = jnp.exp(v1 - v0)
    w0 = 1.0 / (1.0 + e1)
    cw_ref[...] = (jnp.where(lane == 0, i0.astype(F32), 0.0) + jnp.where(lane == 1, i1.astype(F32), 0.0)
                   + jnp.where(lane == 2, w0, 0.0) + jnp.where(lane == 3, e1 * w0, 0.0))


def _outproj_kernel(x_ref, ya_ref, yb_ref, yc_ref, wo_ref, g_ref, mod_ref, *rest, with_router):
    if with_router:
        wr_ref, x1_ref, h_ref, cw_ref = rest
    else:
        x1_ref, h_ref = rest
    y = (_dot(ya_ref[...], wo_ref[0:384, :]) + _dot(yb_ref[...], wo_ref[384:640, :])
         + _dot(yc_ref[...], wo_ref[640:1024, :]))
    x1 = x_ref[...] + mod_ref[2:3, :] * y
    x1_ref[...] = x1
    h = _norm_mod(x1, g_ref[...], mod_ref[3:4, :], mod_ref[4:5, :])
    if with_router:
        h_ref[...] = _pack_rows(h)
        _route(h, wr_ref, cw_ref)
    else:
        h_ref[...] = h.astype(BF16)


def outproj(x, ya, yb, yc, wo, g_ffn, mod, seq, w_router=None, tm=512):
    t, d = x.shape
    tiles_per_batch = seq // tm
    row = lambda i: (i, 0)
    const = lambda i: (0, 0)
    with_router = w_router is not None
    in_specs = [pl.BlockSpec((tm, d), row), pl.BlockSpec((tm, 384), row), pl.BlockSpec((tm, 256), row),
                pl.BlockSpec((tm, 384), row), pl.BlockSpec((d, d), const), pl.BlockSpec((1, d), const),
                pl.BlockSpec((None, 6, d), lambda i: (i // tiles_per_batch, 0, 0))]
    out_specs = [pl.BlockSpec((tm, d), row), pl.BlockSpec((tm, d), row)]
    out_shape = [jax.ShapeDtypeStruct((t, d), F32), jax.ShapeDtypeStruct((t, d), BF16)]
    args = [x, ya, yb, yc, wo, g_ffn, mod]
    if with_router:
        in_specs.append(pl.BlockSpec((2, d, LANES), lambda i: (0, 0, 0)))
        out_specs[1] = pl.BlockSpec((tm, d // 2), row)
        out_shape[1] = jax.ShapeDtypeStruct((t, d // 2), jnp.int32)
        out_specs.append(pl.BlockSpec((tm, LANES), row))
        out_shape.append(jax.ShapeDtypeStruct((t, LANES), F32))
        args.append(w_router)
    return pl.pallas_call(
        functools.partial(_outproj_kernel, with_router=with_router),
        grid=(t // tm,), in_specs=in_specs, out_specs=out_specs, out_shape=out_shape,
        compiler_params=_params("parallel"),
        name="outproj_router" if with_router else "outproj",
    )(*args)


def _swiglu_partial(h, wg_ref, wu_ref, wd_ref):
    a = _dot(h, wg_ref[...])
    u = _dot(h, wu_ref[...])
    act = (a * (1.0 / (1.0 + jnp.exp(-a)))) * u
    return _dot(act.astype(BF16), wd_ref[...])


def _ffn_kernel(x1_ref, h_ref, mod_ref, wg_ref, wu_ref, wd_ref, o_ref, acc_ref):
    f = pl.program_id(1)

    @pl.when(f == 0)
    def _():
        acc_ref[...] = jnp.zeros(acc_ref.shape, F32)

    acc_ref[...] += _swiglu_partial(h_ref[...], wg_ref, wu_ref, wd_ref)

    @pl.when(f == pl.num_programs(1) - 1)
    def _():
        o_ref[...] = x1_ref[...] + mod_ref[5:6, :] * acc_ref[...]


def ffn_dense(x1, h, mod, wg, wu, wd, seq, tm=512, tf=1408):
    t, d = x1.shape
    dff = wg.shape[1]
    tiles_per_batch = seq // tm
    row = lambda i, f: (i, 0)
    return pl.pallas_call(
        _ffn_kernel,
        grid=(t // tm, dff // tf),
        in_specs=[pl.BlockSpec((tm, d), row), pl.BlockSpec((tm, d), row),
                  pl.BlockSpec((None, 6, d), lambda i, f: (i // tiles_per_batch, 0, 0)),
                  pl.BlockSpec((d, tf), lambda i, f: (0, f)), pl.BlockSpec((d, tf), lambda i, f: (0, f)),
                  pl.BlockSpec((tf, d), lambda i, f: (f, 0))],
        out_specs=pl.BlockSpec((tm, d), row),
        out_shape=jax.ShapeDtypeStruct((t, d), F32),
        scratch_shapes=[pltpu.VMEM((tm, d), F32)],
        compiler_params=_params("parallel", "arbitrary"),
        name="ffn_dense",
    )(x1, h, mod, wg, wu, wd)


MOE_TILE = 512
MOE_PARTS = 3
SC_CORES, SC_SUBCORES = 2, 16
SC_ROWS = 64
HI16 = -65536


def _pack_rows(x):
    c = x.shape[1] // 2
    bits = lax.bitcast_convert_type(x.astype(jnp.bfloat16).astype(F32), jnp.int32)
    return lax.shift_right_logical(bits[:, :c], jnp.int32(16)) | (bits[:, c:] & jnp.int32(HI16))


def _unpack_rows(w):
    lo = lax.bitcast_convert_type(lax.shift_left(w, jnp.int32(16)), F32)
    hi = lax.bitcast_convert_type(w & jnp.int32(HI16), F32)
    return jnp.concatenate([lo, hi], axis=1)


def sc_gather_rows(table, idx):
    d = table.shape[1]
    b = idx.shape[0]
    per_worker = b // (SC_CORES * SC_SUBCORES)
    assert per_worker * SC_CORES * SC_SUBCORES == b and per_worker % (2 * SC_ROWS) == 0
    mesh = plsc.VectorSubcoreMesh(core_axis_name="c", subcore_axis_name="s")
    idx_buf = pltpu.VMEM((SC_ROWS,), jnp.int32)
    row_buf = pltpu.VMEM((SC_ROWS, d), table.dtype)

    @functools.partial(
        pl.kernel, mesh=mesh, out_type=jax.ShapeDtypeStruct((b, d), table.dtype),
        scratch_types=[idx_buf, idx_buf, row_buf, row_buf] + [pltpu.SemaphoreType.DMA] * 4,
        name="sc_gather_rows")
    def gather(table_hbm, idx_hbm, out_hbm, idx0, idx1, rows0, rows1, sem_g0, sem_g1, sem_w0, sem_w1):
        base = (lax.axis_index("s") * SC_CORES + lax.axis_index("c")) * per_worker

        @pl.loop(0, per_worker // (2 * SC_ROWS))
        def _(pair):
            off0 = pl.multiple_of(base + pair * (2 * SC_ROWS), SC_ROWS)
            off1 = pl.multiple_of(off0 + SC_ROWS, SC_ROWS)
            pltpu.sync_copy(idx_hbm.at[pl.ds(off0, SC_ROWS)], idx0)
            pltpu.sync_copy(idx_hbm.at[pl.ds(off1, SC_ROWS)], idx1)
            gather0 = pltpu.async_copy(table_hbm.at[idx0], rows0, sem_g0)
            gather1 = pltpu.async_copy(table_hbm.at[idx1], rows1, sem_g1)
            gather0.wait()
            write0 = pltpu.async_copy(rows0, out_hbm.at[pl.ds(off0, SC_ROWS)], sem_w0)
            gather1.wait()
            write1 = pltpu.async_copy(rows1, out_hbm.at[pl.ds(off1, SC_ROWS)], sem_w1)
            write0.wait()
            write1.wait()

    return gather(table, idx)


def _moe_layout(e0, e1, t, n_exp):
    experts = jnp.arange(n_exp, dtype=jnp.int32)
    routed = ((e0[:, None] == experts) | (e1[:, None] == experts)).astype(jnp.int32)
    csum = jnp.cumsum(routed, axis=0)
    padded = (csum[-1] + MOE_TILE - 1) // MOE_TILE * MOE_TILE
    ends = jnp.cumsum(padded)
    pos = (ends - padded)[None, :] + csum - routed
    pos0 = jnp.take_along_axis(pos, e0[:, None], axis=1)[:, 0]
    pos1 = jnp.take_along_axis(pos, e1[:, None], axis=1)[:, 0]
    n_rows = 2 * t + n_exp * MOE_TILE
    tok = jnp.arange(t, dtype=jnp.int32)
    src = jnp.zeros((n_rows,), jnp.int32).at[jnp.concatenate([pos0, pos1])].set(
        jnp.concatenate([tok, tok]), unique_indices=True)
    n_tiles = n_rows // MOE_TILE
    n_valid = ends[-1] // MOE_TILE
    tile_expert = jnp.sum(jnp.arange(n_tiles)[:, None] * MOE_TILE >= ends[None, :], axis=1)
    tile_expert = tile_expert[jnp.minimum(jnp.arange(n_tiles), n_valid - 1)].astype(jnp.int32)
    return pos0, pos1, src, tile_expert, n_valid.astype(jnp.int32).reshape(1)


def _experts_packed_kernel(te_ref, nv_ref, x_ref, wg_ref, wu_ref, wd_ref, *rest):
    o_ref, acc_ref = rest[-2:]
    j, f = pl.program_id(0), pl.program_id(1)
    valid = j < nv_ref[0]

    @pl.when(valid & (f == 0))
    def _():
        acc_ref[...] = jnp.zeros(acc_ref.shape, F32)

    @pl.when(valid)
    def _():
        acc_ref[...] += _swiglu_partial(_unpack_rows(x_ref[...]).astype(BF16), wg_ref, wu_ref, wd_ref)

    @pl.when(valid & (f == pl.num_programs(1) - 1))
    def _():
        o_ref[...] = _pack_rows(acc_ref[...])


def moe_experts_packed(x_part, tile_expert, n_valid, wg, wu, wd, y_prev, part, n_rows, tf=1408):
    rows_part, half = x_part.shape
    d = 2 * half
    nf = wg.shape[2] // tf
    tiles_part = rows_part // MOE_TILE

    def tile(j, nv):
        return jnp.maximum(jnp.minimum(j, nv[0] - 1), 0)

    def fstep(j, f, nv):
        return jnp.where(j < nv[0], f, nf - 1)

    in_specs = [pl.BlockSpec((MOE_TILE, half), lambda j, f, te, nv: (tile(j, nv), 0)),
                pl.BlockSpec((None, d, tf), lambda j, f, te, nv: (te[j], 0, fstep(j, f, nv))),
                pl.BlockSpec((None, d, tf), lambda j, f, te, nv: (te[j], 0, fstep(j, f, nv))),
                pl.BlockSpec((None, tf, d), lambda j, f, te, nv: (te[j], fstep(j, f, nv), 0))]
    args = [tile_expert, n_valid, x_part, wg, wu, wd]
    aliases = {}
    if y_prev is not None:
        in_specs.append(pl.BlockSpec(memory_space=pl.ANY))
        args.append(y_prev)
        aliases = {len(args) - 1: 0}
    return pl.pallas_call(
        _experts_packed_kernel,
        grid_spec=pltpu.PrefetchScalarGridSpec(
            num_scalar_prefetch=2, grid=(tiles_part, nf), in_specs=in_specs,
            out_specs=pl.BlockSpec((MOE_TILE, half), lambda j, f, te, nv: (part * tiles_part + tile(j, nv), 0)),
            scratch_shapes=[pltpu.VMEM((MOE_TILE, d), F32)]),
        out_shape=jax.ShapeDtypeStruct((n_rows, half), jnp.int32),
        input_output_aliases=aliases,
        compiler_params=_params("arbitrary", "arbitrary"),
        name="moe_experts",
    )(*args)


def _combine_packed_kernel(y0_ref, y1_ref, route_ref, x1_ref, mod_ref, o_ref):
    rt = route_ref[...]
    f = rt[:, 2:3] * _unpack_rows(y0_ref[...]) + rt[:, 3:4] * _unpack_rows(y1_ref[...])
    o_ref[...] = x1_ref[...] + mod_ref[5:6, :] * f


def moe_combine_packed(y_pairs, route, x1, mod, seq, tm=512):
    t, d = x1.shape
    nt = t // tm
    tiles_per_batch = seq // tm
    row = lambda i: (i, 0)
    return pl.pallas_call(
        _combine_packed_kernel,
        grid=(nt,),
        in_specs=[pl.BlockSpec((tm, d // 2), row), pl.BlockSpec((tm, d // 2), lambda i: (i + nt, 0)),
                  pl.BlockSpec((tm, LANES), row), pl.BlockSpec((tm, d), row),
                  pl.BlockSpec((None, 6, d), lambda i: (i // tiles_per_batch, 0, 0))],
        out_specs=pl.BlockSpec((tm, d), row),
        out_shape=jax.ShapeDtypeStruct((t, d), F32),
        compiler_params=_params("parallel"),
        name="moe_combine",
    )(y_pairs, y_pairs, route, x1, mod)


def ffn_moe_sc(x1, h_packed, route, mod, wg, wu, wd, seq):
    t = x1.shape[0]
    e0, e1 = route[:, 0].astype(jnp.int32), route[:, 1].astype(jnp.int32)
    pos0, pos1, src, tile_expert, n_valid = _moe_layout(e0, e1, t, wg.shape[0])
    n_rows = src.shape[0]
    rows_part = n_rows // MOE_PARTS
    tiles_part = rows_part // MOE_TILE
    assert rows_part * MOE_PARTS == n_rows and tiles_part * MOE_TILE == rows_part
    h_parts = [sc_gather_rows(h_packed, src[p * rows_part:(p + 1) * rows_part]) for p in range(MOE_PARTS)]
    y_sorted = None
    for p in range(MOE_PARTS):
        n_valid_part = jnp.clip(n_valid - p * tiles_part, 0, tiles_part)
        y_sorted = moe_experts_packed(h_parts[p], tile_expert[p * tiles_part:(p + 1) * tiles_part], n_valid_part,
                                      wg, wu, wd, y_sorted, p, n_rows)
    y_pairs = sc_gather_rows(y_sorted, jnp.concatenate([pos0, pos1]))
    return moe_combine_packed(y_pairs, route, x1, mod, seq)


def _final_norm_kernel(x_ref, g_ref, o_ref):
    x = x_ref[...]
    o_ref[...] = (x * lax.rsqrt(jnp.mean(x * x, axis=-1, keepdims=True) + EPS)) * g_ref[...]


def final_norm(x, g, tm=512):
    t, d = x.shape
    return pl.pallas_call(
        _final_norm_kernel,
        grid=(t // tm,),
        in_specs=[pl.BlockSpec((tm, d), lambda i: (i, 0)), pl.BlockSpec((1, d), lambda i: (0, 0))],
        out_specs=pl.BlockSpec((tm, d), lambda i: (i, 0)),
        out_shape=jax.ShapeDtypeStruct((t, d), F32),
        compiler_params=_params("parallel"),
        name="final_norm",
    )(x, g)


def _rope_tables(positions, dim):
    rot = dim // ROPE_FRACTION
    half = rot // 2
    inv = 1.0 / (ROPE_THETA ** (np.arange(0, rot, 2, dtype=np.float32) / rot))
    ang = positions.reshape(-1).astype(F32)[:, None] * jnp.asarray(inv, F32)
    cos, sin = jnp.cos(ang), jnp.sin(ang)
    t = ang.shape[0]
    ones = jnp.ones((t, dim - rot), F32)
    zeros = lambda w: jnp.zeros((t, w), F32)
    reps = LANES // dim
    c = jnp.tile(jnp.concatenate([cos, cos, ones], axis=1), (1, reps))
    sa = jnp.tile(jnp.concatenate([-sin, zeros(dim - half)], axis=1), (1, reps))
    sb = jnp.tile(jnp.concatenate([zeros(half), sin, zeros(dim - rot)], axis=1), (1, reps))
    return c, sa, sb


def _relayout_w_in(w):
    d = w.shape[0]
    pts = np.cumsum([0, 384, 64, 64, 256, 64, 4, 256, 256, 256, 384, 384, 384])
    (q_a, k_a, v_a, q_i, k_i, w_i, q_b, k_b, v_b, q_c, k_c, v_c) = [w[:, pts[j]:pts[j + 1]] for j in range(12)]
    w_i_pad = jnp.concatenate([w_i, jnp.zeros((d, LANES - IDX_HEADS), w.dtype)], axis=1)
    return jnp.concatenate([q_a, k_a, k_a, v_a, v_a, q_i, k_i, k_i, w_i_pad,
                            q_b, k_b, v_b, q_c, k_c, v_c], axis=1).astype(BF16)


def kernel(x, c, positions, w_in, w_out, diff_lambda, diff_subln, w_ada, b_ada, g_attn, g_ffn, w_ff_gate,
           w_ff_up, w_ff_down, w_router, w_exp_gate, w_exp_up, w_exp_down, g_final):
    batch, seq, d = x.shape
    depth = w_in.shape[0]
    t = batch * seq
    tabs = _rope_tables(positions, HEAD_DIM) + _rope_tables(positions, B_QK_DIM)
    mod_all = adaln_mod(c, w_ada, b_ada).reshape(depth, batch, 6, d)
    xf = x.reshape(t, d)
    for layer in range(depth):
        mod = mod_all[layer]
        lam_init = 0.8 - 0.6 * math.exp(-0.3 * layer)
        (qa, kka, vva, qi, kki, wi, qb, kb, vb, qc, kc, vc) = inproj(
            xf, g_attn[layer].reshape(1, d), mod, tabs, _relayout_w_in(w_in[layer]), seq)
        ya = dsa_attention(qa, qi, wi, kka, vva, kki, batch, seq)
        g_sub2 = jnp.tile(diff_subln[layer], 2).reshape(1, LANES)
        yb = diff_attention(qb, kb, vb, diff_lambda[layer], g_sub2, lam_init, batch, seq)
        yc = moba_attention(qc, kc, vc, batch, seq)
        j = layer // 2
        wo = w_out[layer].astype(BF16)
        gf = g_ffn[layer].reshape(1, d)
        if layer % 2 == 0:
            x1, h = outproj(xf, ya, yb, yc, wo, gf, mod, seq)
            xf = ffn_dense(x1, h, mod, w_ff_gate[j].astype(BF16), w_ff_up[j].astype(BF16),
                           w_ff_down[j].astype(BF16), seq)
        else:
            wr = jnp.concatenate([w_router[j], jnp.zeros((d, LANES - N_EXPERTS), F32)], axis=1)
            wr_hi = wr.astype(BF16)
            wr = jnp.stack([wr_hi, (wr - wr_hi.astype(F32)).astype(BF16)])
            x1, h_packed, route = outproj(xf, ya, yb, yc, wo, gf, mod, seq, w_router=wr)
            xf = ffn_moe_sc(x1, h_packed, route, mod, w_exp_gate[j].astype(BF16), w_exp_up[j].astype(BF16),
                            w_exp_down[j].astype(BF16), seq)
    return final_norm(xf, g_final.reshape(1, d)).reshape(batch, seq, d)
```

```python
import functools
import math

import jax
import jax.numpy as jnp
import numpy as np
from jax import lax
from jax.experimental import pallas as pl
from jax.experimental.pallas import tpu as pltpu
from jax.experimental.pallas import tpu_sc as plsc

F32 = jnp.float32
BF16 = jnp.bfloat16

HEAD_DIM = 64
A_HEADS = 6
IDX_HEADS = 4
B_HEADS = 4
B_QK_DIM = 32
C_HEADS = 6
DSA_TOPK_MAX = 256
MOBA_BLOCK = 256
MOBA_TOPK = 3
ROPE_THETA = 500000.0
ROPE_FRACTION = 4
SUBLN_EPS = 1e-5
EPS = 1e-6
N_EXPERTS = 8

LANES = 128
NEG = -1e30
INT_MIN = -2 ** 31
MIN_NORMAL_KEY = 0x00800000
MIN_NORMAL_F32 = float(np.float32(2.0 ** -126))
VMEM_LIMIT = 48 * 1024 * 1024
LOG2E = math.log2(math.e)

_G_QA, _G_KKA, _G_VVA, _G_QI, _G_KKI, _G_WI = (0, 384), (384, 512), (512, 640), (640, 896), (896, 1024), (1024, 1152)
_G_QB, _G_KB, _G_VB = (1152, 1408), (1408, 1664), (1664, 1920)
_G_QC, _G_KC, _G_VC = (1920, 2304), (2304, 2688), (2688, 3072)
D_IN_PAD = 3072


def _params(*sem):
    return pltpu.CompilerParams(dimension_semantics=sem, vmem_limit_bytes=VMEM_LIMIT)


def _dot(a, b):
    return jnp.dot(a, b, preferred_element_type=F32)


def _dot_nt(a, b, precision=None):
    return lax.dot_general(a, b, (((1,), (1,)), ((), ())), preferred_element_type=F32, precision=precision)


def _adaln_kernel(c_ref, w_ref, b_ref, o_ref):
    c = c_ref[...]
    c_act = c * (1.0 / (1.0 + jnp.exp(-c)))
    o_ref[...] = jnp.dot(c_act, w_ref[...], preferred_element_type=F32,
                         precision=lax.Precision.HIGHEST) + b_ref[...]


def adaln_mod(c, w_ada, b_ada, tn=1536):
    depth, d, n = w_ada.shape
    b = c.shape[0]
    return pl.pallas_call(
        _adaln_kernel,
        grid=(depth, n // tn),
        in_specs=[pl.BlockSpec((b, d), lambda l, j: (0, 0)),
                  pl.BlockSpec((None, d, tn), lambda l, j: (l, 0, j)),
                  pl.BlockSpec((None, 1, tn), lambda l, j: (l, 0, j))],
        out_specs=pl.BlockSpec((None, b, tn), lambda l, j: (l, 0, j)),
        out_shape=jax.ShapeDtypeStruct((depth, b, n), F32),
        compiler_params=_params("parallel", "parallel"),
        name="adaln_mod",
    )(c, w_ada, b_ada.reshape(depth, 1, n))


def _norm_mod(x, g, shift, scale, eps=EPS):
    y = x * lax.rsqrt(jnp.mean(x * x, axis=-1, keepdims=True) + eps)
    return (y * g) * (1.0 + scale) + shift


def _rope_store(acc, o_ref, cos, sa, sb, half):
    for j in range(acc.shape[1] // LANES):
        a = acc[:, j * LANES:(j + 1) * LANES]
        r = a * cos + pltpu.roll(a, half, 1) * sb + pltpu.roll(a, LANES - half, 1) * sa
        o_ref[:, j * LANES:(j + 1) * LANES] = r.astype(o_ref.dtype)


def _inproj_kernel(x_ref, g_ref, mod_ref, c64_ref, sa64_ref, sb64_ref, c32_ref, sa32_ref, sb32_ref, w_ref,
                   qa_ref, kka_ref, vva_ref, qi_ref, kki_ref, wi_ref,
                   qb_ref, kb_ref, vb_ref, qc_ref, kc_ref, vc_ref):
    h = _norm_mod(x_ref[...], g_ref[...], mod_ref[0:1, :], mod_ref[1:2, :]).astype(BF16)
    c64, sa64, sb64 = c64_ref[...], sa64_ref[...], sb64_ref[...]
    c32, sa32, sb32 = c32_ref[...], sa32_ref[...], sb32_ref[...]

    def proj(cols):
        return _dot(h, w_ref[:, cols[0]:cols[1]])

    qk_scale = HEAD_DIM ** -0.5 * LOG2E
    _rope_store(proj(_G_QA), qa_ref, c64 * qk_scale, sa64 * qk_scale, sb64 * qk_scale, 8)
    _rope_store(proj(_G_KKA), kka_ref, c64, sa64, sb64, 8)
    vva_ref[...] = proj(_G_VVA).astype(vva_ref.dtype)
    _rope_store(proj(_G_QI), qi_ref, c64, sa64, sb64, 8)
    _rope_store(proj(_G_KKI), kki_ref, c64, sa64, sb64, 8)
    wi_ref[...] = proj(_G_WI) * (IDX_HEADS ** -0.5 * HEAD_DIM ** -0.5)
    b_scale = B_QK_DIM ** -0.5 * LOG2E
    _rope_store(proj(_G_QB), qb_ref, c32 * b_scale, sa32 * b_scale, sb32 * b_scale, 4)
    _rope_store(proj(_G_KB), kb_ref, c32, sa32, sb32, 4)
    vb_ref[...] = proj(_G_VB).astype(vb_ref.dtype)
    _rope_store(proj(_G_QC), qc_ref, c64 * qk_scale, sa64 * qk_scale, sb64 * qk_scale, 8)
    _rope_store(proj(_G_KC), kc_ref, c64, sa64, sb64, 8)
    vc_ref[...] = proj(_G_VC).astype(vc_ref.dtype)


def inproj(x, g, mod, tabs, w_pad, layer, seq, tm=512):
    t, d = x.shape
    tiles_per_batch = seq // tm
    row = lambda i: (i, 0)
    widths = [384, 128, 128, 256, 128, 128, 256, 256, 256, 384, 384, 384]
    dtypes = [BF16, BF16, BF16, BF16, BF16, F32, BF16, BF16, BF16, BF16, BF16, BF16]
    return pl.pallas_call(
        _inproj_kernel,
        grid=(t // tm,),
        in_specs=[pl.BlockSpec((tm, d), row),
                  pl.BlockSpec((1, d), lambda i: (0, 0)),
                  pl.BlockSpec((None, 6, d), lambda i: (i // tiles_per_batch, 0, 0))]
                 + [pl.BlockSpec((tm, LANES), row)] * 6
                 + [pl.BlockSpec((None, d, D_IN_PAD), lambda i: (layer, 0, 0))],
        out_specs=[pl.BlockSpec((tm, w), row) for w in widths],
        out_shape=[jax.ShapeDtypeStruct((t, w), dt) for w, dt in zip(widths, dtypes)],
        compiler_params=_params("parallel"),
        name="inproj",
    )(x, g, mod, *tabs, w_pad)


def _init_stats(m_ref, l_ref, acc_ref):
    m_ref[...] = jnp.full(m_ref.shape, -jnp.inf, F32)
    l_ref[...] = jnp.zeros(l_ref.shape, F32)
    acc_ref[...] = jnp.zeros(acc_ref.shape, F32)


def _key_to_f32(k):
    return lax.bitcast_convert_type(jnp.where(k >= 0, k, k ^ 0x7FFFFFFF), F32)


def dsa_attention(qa, qi, wi, kka, vva, kki, batch, seq, tq=256):
    t = qa.shape[0]
    nq = seq // tq
    k_top = min(DSA_TOPK_MAX, seq // 4)
    qrow = lambda b, i: (b * nq + i, 0)
    full = lambda b, i: (b, 0)
    return pl.pallas_call(
        functools.partial(_dsa_kernel_t, tq=tq, k_top=k_top, seq=seq),
        grid=(batch, nq),
        in_specs=[pl.BlockSpec((tq, 384), qrow), pl.BlockSpec((tq, 256), qrow), pl.BlockSpec((tq, LANES), qrow),
                  pl.BlockSpec((seq, LANES), full), pl.BlockSpec((seq, LANES), full),
                  pl.BlockSpec((seq, LANES), full)],
        out_specs=pl.BlockSpec((tq, 384), qrow),
        out_shape=jax.ShapeDtypeStruct((t, 384), BF16),
        scratch_shapes=[pltpu.VMEM((seq, tq), F32), pltpu.VMEM((seq, tq), jnp.int16),
                        pltpu.VMEM((seq, tq), jnp.int16), pltpu.VMEM((1, LANES, seq), BF16),
                        pltpu.VMEM((3, 2 * tq, LANES), BF16),
                        pltpu.VMEM((3, 1, 2 * tq), F32), pltpu.VMEM((3, 1, 2 * tq), F32),
                        pltpu.VMEM((3, LANES, 2 * tq), F32)],
        compiler_params=_params("parallel", "arbitrary"),
        name="dsa_attention",
    )(qa, qi, wi, kka, vva, kki)


def diff_attention(qb, kb, vb, lam_vec, g_sub2, lam_init, batch, seq, tq=256):
    t = qb.shape[0]
    nq = seq // tq
    qrow = lambda b, i: (b * nq + i, 0)
    full = lambda b, i: (b, 0)
    const = lambda b, i: (0, 0)
    return pl.pallas_call(
        functools.partial(_diff_kernel_t, tq=tq, lam_init=lam_init, seq=seq),
        grid=(batch, nq),
        in_specs=[pl.BlockSpec((tq, 256), qrow), pl.BlockSpec((seq, 256), full), pl.BlockSpec((seq, 256), full),
                  pl.BlockSpec((4, B_QK_DIM), const), pl.BlockSpec((1, LANES), const)],
        out_specs=pl.BlockSpec((tq, 256), qrow),
        out_shape=jax.ShapeDtypeStruct((t, 256), BF16),
        scratch_shapes=[pltpu.VMEM((2, LANES, seq), BF16), pltpu.VMEM((2, 4 * tq, LANES), BF16),
                        pltpu.VMEM((2, 1, 4 * tq), F32), pltpu.VMEM((2, 1, 4 * tq), F32),
                        pltpu.VMEM((2, LANES, 4 * tq), F32)],
        compiler_params=_params("parallel", "arbitrary"),
        name="diff_attention",
    )(qb, kb, vb, lam_vec, g_sub2)


def moba_attention(qc, kc, vc, batch, seq):
    tq = MOBA_BLOCK
    t = qc.shape[0]
    nb = seq // tq
    n_sel = min(MOBA_TOPK, nb - 1)
    nbp = 8
    assert seq % tq == 0 and nb <= nbp
    qrow = lambda b, i: (b * nb + i, 0)
    full = lambda b, i: (b, 0)
    return pl.pallas_call(
        functools.partial(_moba_kernel_t, tq=tq, nb=nb, n_sel=n_sel, seq=seq),
        grid=(batch, nb),
        in_specs=[pl.BlockSpec((tq, 384), qrow), pl.BlockSpec((seq, 384), full), pl.BlockSpec((seq, 384), full)],
        out_specs=pl.BlockSpec((tq, 384), qrow),
        out_shape=jax.ShapeDtypeStruct((t, 384), BF16),
        scratch_shapes=[pltpu.VMEM((nbp, 384), F32), pltpu.VMEM((3, LANES, seq), BF16),
                        pltpu.VMEM((3, 2 * tq, LANES), BF16), pltpu.VMEM((3, nbp, 2 * tq), F32),
                        pltpu.VMEM((3, 1, 2 * tq), F32), pltpu.VMEM((3, 1, 2 * tq), F32),
                        pltpu.VMEM((3, LANES, 2 * tq), F32)],
        compiler_params=_params("parallel", "arbitrary"),
        name="moba_attention",
    )(qc, kc, vc)


def _attend_t(k, q_ref, g, v_t, m_ref, l_ref, acc_ref, mask=None):
    for j in range(q_ref.shape[1] // LANES):
        cols = slice(j * LANES, (j + 1) * LANES)
        s = _dot_nt(k, q_ref[g, cols, :])
        if mask is not None:
            s = mask(j, s)
        m_prev = m_ref[g, :, cols]
        m_new = jnp.maximum(m_prev, jnp.max(s, axis=0, keepdims=True))
        alpha = jnp.exp2(m_prev - m_new)
        p = jnp.exp2(s - m_new)
        l_ref[g, :, cols] = alpha * l_ref[g, :, cols] + jnp.sum(p, axis=0, keepdims=True)
        acc_ref[g, :, cols] = alpha * acc_ref[g, :, cols] + _dot(v_t, p.astype(BF16))
        m_ref[g, :, cols] = m_new


def _causal_mask_t(tk, tq):
    key = lax.broadcasted_iota(jnp.int32, (tk, LANES), 0)
    qry = lax.broadcasted_iota(jnp.int32, (tk, LANES), 1)

    def mask(j, s):
        return jnp.where(key <= qry + (j * LANES) % tq, s, NEG)
    return mask


def _transpose_values(v_ref, vt_ref, seq, tk):
    for g in range(vt_ref.shape[0]):
        for n in range(seq // tk):
            blk = v_ref[n * tk:(n + 1) * tk, g * LANES:(g + 1) * LANES].astype(F32)
            vt_ref[g, :, n * tk:(n + 1) * tk] = blk.T.astype(vt_ref.dtype)


def _head_pair_out(acc_t, l, tq):
    even = acc_t[0:HEAD_DIM, 0:tq] / l[:, 0:tq]
    odd = acc_t[HEAD_DIM:LANES, tq:2 * tq] / l[:, tq:2 * tq]
    return jnp.concatenate([even, odd], axis=0).T


def _dsa_kernel_t(qa_ref, qi_ref, wiq_ref, kka_ref, vva_ref, kki_ref, o_ref,
                  sc_ref, hi_ref, lo_ref, vt_ref, qs_ref, m_ref, l_ref, acc_ref, *, tq, k_top, seq):
    i = pl.program_id(1)
    nk = i + 1
    t0 = i * tq
    lo = lax.broadcasted_iota(jnp.int32, (tq, LANES), 1) < HEAD_DIM
    zero_b = jnp.zeros((tq, LANES), BF16)

    @pl.when(i == 0)
    def _():
        _transpose_values(vva_ref, vt_ref, seq, tq)

    def stack_heads(q):
        out = []
        for g in range(q.shape[1] // LANES):
            qg = q[:, g * LANES:(g + 1) * LANES]
            out += [jnp.where(lo, qg, zero_b), jnp.where(lo, zero_b, qg)]
        return out

    qa_stack = stack_heads(qa_ref[...])
    for g in range(3):
        qs_ref[g, 0:tq, :] = qa_stack[2 * g]
        qs_ref[g, tq:2 * tq, :] = qa_stack[2 * g + 1]
    qi_stack = jnp.concatenate(stack_heads(qi_ref[...]), axis=0)
    wi_t = wiq_ref[...].T

    def rows(c):
        return pl.ds(pl.multiple_of(c * tq, tq), tq)

    key_pos = lax.broadcasted_iota(jnp.int32, (tq, tq), 0)
    qry_pos = lax.broadcasted_iota(jnp.int32, (tq, tq), 1)

    def idx_body(c, carry):
        r = jnp.maximum(_dot_nt(kki_ref[rows(c), :], qi_stack), 0.0)
        s = wi_t[0:1, :] * r[:, 0:tq]
        for h in range(1, IDX_HEADS):
            s = s + wi_t[h:h + 1, :] * r[:, h * tq:(h + 1) * tq]
        causal = (c * tq + key_pos) <= (t0 + qry_pos)
        s = jnp.where(causal, s, -jnp.inf)
        s = jnp.where(jnp.abs(s) < MIN_NORMAL_F32, 0.0, s)
        sc_ref[rows(c), :] = s
        bits = lax.bitcast_convert_type(s, jnp.int32)
        key = jnp.where(bits >= 0, bits, bits ^ 0x7FFFFFFF)
        hi_ref[rows(c), :] = lax.shift_right_arithmetic(key, jnp.int32(16)).astype(jnp.int16)
        lo_ref[rows(c), :] = ((key & 0xFFFF) - 32768).astype(jnp.int16)
        return carry

    lax.fori_loop(0, nk, idx_body, 0)

    def count16(ref, cand, strict):
        c16 = cand.astype(jnp.int16)

        def body(c, acc):
            x = ref[rows(c), :]
            hit = jnp.where((x > c16) if strict else (x >= c16), jnp.int16(1), jnp.int16(0))
            for r in range(tq // 16):
                acc = acc + hit[r * 16:(r + 1) * 16]
            return acc
        acc = lax.fori_loop(0, nk, body, jnp.zeros((16, tq), jnp.int16))
        return jnp.sum(acc.astype(F32), axis=0, keepdims=True)

    def search16(ref, k_need):
        v0 = jnp.where(count16(ref, jnp.zeros((1, tq), jnp.int32), False) >= k_need, 0, -32768).astype(jnp.int32)

        def bisect(b, v):
            trial = v | lax.shift_left(jnp.int32(1), 14 - b)
            return jnp.where(count16(ref, trial, False) >= k_need, trial, v)
        return lax.fori_loop(0, 15, bisect, v0)

    kf = float(k_top)

    def search():
        hi_k = search16(hi_ref, kf)
        above = count16(hi_ref, hi_k, True)
        hi16 = hi_k.astype(jnp.int16)

        def keep_bucket(c, carry):
            lo_ref[rows(c), :] = jnp.where(hi_ref[rows(c), :] == hi16, lo_ref[rows(c), :], jnp.int16(-32768))
            return carry
        lax.fori_loop(0, nk, keep_bucket, 0)
        lo_k = search16(lo_ref, kf - above)
        n_gt = above + count16(lo_ref, lo_k, True)
        return lax.shift_left(hi_k, jnp.int32(16)) | (lo_k + 32768), kf - n_gt

    key, need = lax.cond(t0 + tq <= k_top,
                         lambda: (jnp.full((1, tq), INT_MIN, jnp.int32), jnp.full((1, tq), kf, F32)), search)
    thr = _key_to_f32(key)
    thr_next = _key_to_f32(jnp.where(key == 0, MIN_NORMAL_KEY, key + 1))
    all_sel = (t0 + lax.broadcasted_iota(jnp.int32, (1, tq), 1)) < k_top
    lower = (qry_pos <= key_pos).astype(BF16)

    _init_stats(m_ref, l_ref, acc_ref)

    def att_body(c, tie_run):
        s_idx = sc_ref[rows(c), :]
        ge = s_idx >= thr
        gt = s_idx >= thr_next
        tie = jnp.where(gt, 0.0, jnp.where(ge, 1.0, 0.0))
        prefix = _dot(lower, tie.astype(BF16)) + tie_run
        take = jnp.where(gt, 1.0, jnp.where(prefix <= need, tie, 0.0))
        take = jnp.where(all_sel, 1.0, take)
        take = jnp.where(s_idx > -jnp.inf, take, 0.0)

        def mask(j, s):
            q0 = (j * LANES) % tq
            return jnp.where(take[:, q0:q0 + LANES] > 0.5, s, NEG)

        kk = kka_ref[rows(c), :]
        vt = vt_ref[0, :, rows(c)]
        for g in range(3):
            _attend_t(kk, qs_ref, g, vt, m_ref, l_ref, acc_ref, mask)
        return tie_run + jnp.sum(tie, axis=0, keepdims=True)

    lax.fori_loop(0, nk, att_body, jnp.zeros((1, tq), F32))

    for g in range(3):
        o_ref[:, g * LANES:(g + 1) * LANES] = _head_pair_out(acc_ref[g], l_ref[g], tq).astype(o_ref.dtype)


def _diff_kernel_t(qb_ref, kb_ref, vb_ref, lam_ref, gsub_ref, o_ref,
                   vt_ref, qs_ref, m_ref, l_ref, acc_ref, *, tq, lam_init, seq):
    i = pl.program_id(1)
    lane = lax.broadcasted_iota(jnp.int32, (tq, LANES), 1)
    zero_b = jnp.zeros((tq, LANES), BF16)

    @pl.when(i == 0)
    def _():
        _transpose_values(vb_ref, vt_ref, seq, tq)

    qb = qb_ref[...]
    for g in range(2):
        qg = qb[:, g * LANES:(g + 1) * LANES]
        for j in range(4):
            qs_ref[g, j * tq:(j + 1) * tq, :] = jnp.where(lane // B_QK_DIM == j, qg, zero_b)
    _init_stats(m_ref, l_ref, acc_ref)

    def rows(c):
        return pl.ds(pl.multiple_of(c * tq, tq), tq)

    def step(c, masked):
        mask = _causal_mask_t(tq, tq) if masked else None
        for g in range(2):
            _attend_t(kb_ref[rows(c), g * LANES:(g + 1) * LANES], qs_ref, g, vt_ref[g, :, rows(c)],
                      m_ref, l_ref, acc_ref, mask)

    def body(c, carry):
        step(c, False)
        return carry

    lax.fori_loop(0, i, body, 0)
    step(i, True)

    lv = lam_ref[...]
    lam = (jnp.exp(jnp.sum(lv[0:1] * lv[1:2], axis=1, keepdims=True))
           - jnp.exp(jnp.sum(lv[2:3] * lv[3:4], axis=1, keepdims=True)) + lam_init)
    gsub = gsub_ref[...]
    for g in range(2):
        acc_t, l = acc_ref[g], l_ref[g]

        def prob(r0, j):
            return acc_t[r0:r0 + HEAD_DIM, j * tq:(j + 1) * tq] / l[:, j * tq:(j + 1) * tq]

        halves = []
        for r0, j in ((0, 0), (HEAD_DIM, 2)):
            o = prob(r0, j) - lam * prob(r0, j + 1)
            ms = jnp.mean(o * o, axis=0, keepdims=True)
            halves.append(o * lax.rsqrt(ms + SUBLN_EPS))
        y = (jnp.concatenate(halves, axis=0).T * gsub) * (1.0 - lam_init)
        o_ref[:, g * LANES:(g + 1) * LANES] = y.astype(o_ref.dtype)


def _moba_kernel_t(qc_ref, kc_ref, vc_ref, o_ref, kmean_ref, vt_ref, qs_ref, bias_ref, m_ref, l_ref, acc_ref,
                   *, tq, nb, n_sel, seq):
    i = pl.program_id(1)
    nbp = kmean_ref.shape[0]
    lane = lax.broadcasted_iota(jnp.int32, (tq, LANES), 1)
    lo = lane < HEAD_DIM
    zero_b = jnp.zeros((tq, LANES), BF16)

    @pl.when(i == 0)
    def _():
        _transpose_values(vc_ref, vt_ref, seq, tq)
        kmean_ref[...] = jnp.zeros(kmean_ref.shape, F32)
        for n in range(nb):
            kblk = kc_ref[n * tq:(n + 1) * tq, :].astype(F32)
            kmean_ref[n:n + 1, :] = jnp.mean(kblk, axis=0, keepdims=True)

    qc = qc_ref[...]
    sub = lax.broadcasted_iota(jnp.int32, (nbp, 2 * tq), 0)
    past = sub < i
    for g in range(3):
        qg = qc[:, g * LANES:(g + 1) * LANES]
        q2 = jnp.concatenate([jnp.where(lo, qg, zero_b), jnp.where(lo, zero_b, qg)], axis=0)
        km = kmean_ref[:, g * LANES:(g + 1) * LANES]
        km_hi = km.astype(BF16)
        gt = _dot_nt(km_hi, q2) + _dot_nt((km - km_hi.astype(F32)).astype(BF16), q2)
        rows_ = []
        for n in range(nbp):
            gn = gt[n:n + 1, :]
            beats = jnp.where(sub < n, jnp.where(gt >= gn, 1.0, 0.0), jnp.where(gt > gn, 1.0, 0.0))
            beats = jnp.where(sub == n, 0.0, jnp.where(past, beats, 0.0))
            rank = jnp.sum(beats, axis=0, keepdims=True)
            rows_.append(jnp.where(rank < n_sel, 0.0, NEG))
        bias_ref[g] = jnp.where(past, jnp.concatenate(rows_, axis=0), NEG)
        qs_ref[g] = q2
    _init_stats(m_ref, l_ref, acc_ref)

    def rows(c):
        return pl.ds(pl.multiple_of(c * tq, tq), tq)

    def body(c, carry):
        for g in range(3):
            bias = bias_ref[g, pl.ds(c, 1), :]

            def mask(j, s):
                return s + bias[:, j * LANES:(j + 1) * LANES]

            _attend_t(kc_ref[rows(c), g * LANES:(g + 1) * LANES], qs_ref, g, vt_ref[g, :, rows(c)],
                      m_ref, l_ref, acc_ref, mask)
        return carry

    lax.fori_loop(0, i, body, 0)

    causal = _causal_mask_t(tq, tq)
    for g in range(3):
        _attend_t(kc_ref[rows(i), g * LANES:(g + 1) * LANES], qs_ref, g, vt_ref[g, :, rows(i)],
                  m_ref, l_ref, acc_ref, causal)
        o_ref[:, g * LANES:(g + 1) * LANES] = _head_pair_out(acc_ref[g], l_ref[g], tq).astype(o_ref.dtype)


def _route(h, wr_ref, cw_ref):
    hi = h.astype(BF16)
    lo = (h - hi.astype(F32)).astype(BF16)
    logits = _dot(hi, wr_ref[0]) + (_dot(lo, wr_ref[0]) + _dot(hi, wr_ref[1]))
    lane = lax.broadcasted_iota(jnp.int32, logits.shape, 1)
    lg = jnp.where(lane < N_EXPERTS, logits, -jnp.inf)
    v0 = jnp.max(lg, axis=1, keepdims=True)
    i0 = jnp.min(jnp.where(lg == v0, lane, LANES), axis=1, keepdims=True)
    lg1 = jnp.where(lane == i0, -jnp.inf, lg)
    v1 = jnp.max(lg1, axis=1, keepdims=True)
    i1 = jnp.min(jnp.where(lg1 == v1, lane, LANES), axis=1, keepdims=True)
    e1 = jnp.exp(v1 - v0)
    w0 = 1.0 / (1.0 + e1)
    cw_ref[...] = (jnp.where(lane == 0, i0.astype(F32), 0.0) + jnp.where(lane == 1, i1.astype(F32), 0.0)
                   + jnp.where(lane == 2, w0, 0.0) + jnp.where(lane == 3, e1 * w0, 0.0))


def _outproj_kernel(x_ref, ya_ref, yb_ref, yc_ref, wo_ref, g_ref, mod_ref, *rest, with_router):
    if with_router:
        wr_ref, x1_ref, h_ref, cw_ref = rest
    else:
        x1_ref, h_ref = rest
    y = (_dot(ya_ref[...], wo_ref[0:384, :]) + _dot(yb_ref[...], wo_ref[384:640, :])
         + _dot(yc_ref[...], wo_ref[640:1024, :]))
    x1 = x_ref[...] + mod_ref[2:3, :] * y
    x1_ref[...] = x1
    h = _norm_mod(x1, g_ref[...], mod_ref[3:4, :], mod_ref[4:5, :])
    if with_router:
        h_ref[...] = _pack_rows(h)
        _route(h, wr_ref, cw_ref)
    else:
        h_ref[...] = h.astype(BF16)


def outproj(x, ya, yb, yc, wo, layer, g_ffn, mod, seq, w_router=None, tm=512):
    t, d = x.shape
    tiles_per_batch = seq // tm
    row = lambda i: (i, 0)
    const = lambda i: (0, 0)
    with_router = w_router is not None
    in_specs = [pl.BlockSpec((tm, d), row), pl.BlockSpec((tm, 384), row), pl.BlockSpec((tm, 256), row),
                pl.BlockSpec((tm, 384), row), pl.BlockSpec((None, d, d), lambda i: (layer, 0, 0)),
                pl.BlockSpec((1, d), const),
                pl.BlockSpec((None, 6, d), lambda i: (i // tiles_per_batch, 0, 0))]
    out_specs = [pl.BlockSpec((tm, d), row), pl.BlockSpec((tm, d), row)]
    out_shape = [jax.ShapeDtypeStruct((t, d), F32), jax.ShapeDtypeStruct((t, d), BF16)]
    args = [x, ya, yb, yc, wo, g_ffn, mod]
    if with_router:
        in_specs.append(pl.BlockSpec((2, d, LANES), lambda i: (0, 0, 0)))
        out_specs[1] = pl.BlockSpec((tm, d // 2), row)
        out_shape[1] = jax.ShapeDtypeStruct((t, d // 2), jnp.int32)
        out_specs.append(pl.BlockSpec((tm, LANES), row))
        out_shape.append(jax.ShapeDtypeStruct((t, LANES), F32))
        args.append(w_router)
    return pl.pallas_call(
        functools.partial(_outproj_kernel, with_router=with_router),
        grid=(t // tm,), in_specs=in_specs, out_specs=out_specs, out_shape=out_shape,
        compiler_params=_params("parallel"),
        name="outproj_router" if with_router else "outproj",
    )(*args)


def _swiglu_partial(h, wg_ref, wu_ref, wd_ref):
    a = _dot(h, wg_ref[...])
    u = _dot(h, wu_ref[...])
    act = (a * (1.0 / (1.0 + jnp.exp(-a)))) * u
    return _dot(act.astype(BF16), wd_ref[...])


def _ffn_kernel(x1_ref, h_ref, mod_ref, wg_ref, wu_ref, wd_ref, o_ref, acc_ref):
    f = pl.program_id(1)

    @pl.when(f == 0)
    def _():
        acc_ref[...] = jnp.zeros(acc_ref.shape, F32)

    acc_ref[...] += _swiglu_partial(h_ref[...], wg_ref, wu_ref, wd_ref)

    @pl.when(f == pl.num_programs(1) - 1)
    def _():
        o_ref[...] = x1_ref[...] + mod_ref[5:6, :] * acc_ref[...]


def ffn_dense(x1, h, mod, wg, wu, wd, seq, tm=512, tf=1408):
    t, d = x1.shape
    dff = wg.shape[1]
    tiles_per_batch = seq // tm
    row = lambda i, f: (i, 0)
    return pl.pallas_call(
        _ffn_kernel,
        grid=(t // tm, dff // tf),
        in_specs=[pl.BlockSpec((tm, d), row), pl.BlockSpec((tm, d), row),
                  pl.BlockSpec((None, 6, d), lambda i, f: (i // tiles_per_batch, 0, 0)),
                  pl.BlockSpec((d, tf), lambda i, f: (0, f)), pl.BlockSpec((d, tf), lambda i, f: (0, f)),
                  pl.BlockSpec((tf, d), lambda i, f: (f, 0))],
        out_specs=pl.BlockSpec((tm, d), row),
        out_shape=jax.ShapeDtypeStruct((t, d), F32),
        scratch_shapes=[pltpu.VMEM((tm, d), F32)],
        compiler_params=_params("parallel", "arbitrary"),
        name="ffn_dense",
    )(x1, h, mod, wg, wu, wd)


MOE_TILE = 512
MOE_PARTS = 3
SC_CORES, SC_SUBCORES = 2, 16
SC_ROWS = 64
HI16 = -65536


def _pack_rows(x):
    c = x.shape[1] // 2
    bits = lax.bitcast_convert_type(x.astype(jnp.bfloat16).astype(F32), jnp.int32)
    return lax.shift_right_logical(bits[:, :c], jnp.int32(16)) | (bits[:, c:] & jnp.int32(HI16))


def _unpack_rows(w):
    lo = lax.bitcast_convert_type(lax.shift_left(w, jnp.int32(16)), F32)
    hi = lax.bitcast_convert_type(w & jnp.int32(HI16), F32)
    return jnp.concatenate([lo, hi], axis=1)


def sc_gather_rows(table, idx):
    d = table.shape[1]
    b = idx.shape[0]
    per_worker = b // (SC_CORES * SC_SUBCORES)
    assert per_worker * SC_CORES * SC_SUBCORES == b and per_worker % (2 * SC_ROWS) == 0
    mesh = plsc.VectorSubcoreMesh(core_axis_name="c", subcore_axis_name="s")
    idx_buf = pltpu.VMEM((SC_ROWS,), jnp.int32)
    row_buf = pltpu.VMEM((SC_ROWS, d), table.dtype)

    @functools.partial(
        pl.kernel, mesh=mesh, out_type=jax.ShapeDtypeStruct((b, d), table.dtype),
        scratch_types=[idx_buf, idx_buf, row_buf, row_buf] + [pltpu.SemaphoreType.DMA] * 4,
        name="sc_gather_rows")
    def gather(table_hbm, idx_hbm, out_hbm, idx0, idx1, rows0, rows1, sem_g0, sem_g1, sem_w0, sem_w1):
        base = (lax.axis_index("s") * SC_CORES + lax.axis_index("c")) * per_worker

        @pl.loop(0, per_worker // (2 * SC_ROWS))
        def _(pair):
            off0 = pl.multiple_of(base + pair * (2 * SC_ROWS), SC_ROWS)
            off1 = pl.multiple_of(off0 + SC_ROWS, SC_ROWS)
            pltpu.sync_copy(idx_hbm.at[pl.ds(off0, SC_ROWS)], idx0)
            pltpu.sync_copy(idx_hbm.at[pl.ds(off1, SC_ROWS)], idx1)
            gather0 = pltpu.async_copy(table_hbm.at[idx0], rows0, sem_g0)
            gather1 = pltpu.async_copy(table_hbm.at[idx1], rows1, sem_g1)
            gather0.wait()
            write0 = pltpu.async_copy(rows0, out_hbm.at[pl.ds(off0, SC_ROWS)], sem_w0)
            gather1.wait()
            write1 = pltpu.async_copy(rows1, out_hbm.at[pl.ds(off1, SC_ROWS)], sem_w1)
            write0.wait()
            write1.wait()

    return gather(table, idx)


def _moe_layout(e0, e1, t, n_exp):
    experts = jnp.arange(n_exp, dtype=jnp.int32)
    routed = ((e0[:, None] == experts) | (e1[:, None] == experts)).astype(jnp.int32)
    csum = jnp.cumsum(routed, axis=0)
    padded = (csum[-1] + MOE_TILE - 1) // MOE_TILE * MOE_TILE
    ends = jnp.cumsum(padded)
    pos = (ends - padded)[None, :] + csum - routed
    pos0 = jnp.take_along_axis(pos, e0[:, None], axis=1)[:, 0]
    pos1 = jnp.take_along_axis(pos, e1[:, None], axis=1)[:, 0]
    n_rows = 2 * t + n_exp * MOE_TILE
    tok = jnp.arange(t, dtype=jnp.int32)
    src = jnp.zeros((n_rows,), jnp.int32).at[jnp.concatenate([pos0, pos1])].set(
        jnp.concatenate([tok, tok]), unique_indices=True)
    n_tiles = n_rows // MOE_TILE
    n_valid = ends[-1] // MOE_TILE
    tile_expert = jnp.sum(jnp.arange(n_tiles)[:, None] * MOE_TILE >= ends[None, :], axis=1)
    tile_expert = tile_expert[jnp.minimum(jnp.arange(n_tiles), n_valid - 1)].astype(jnp.int32)
    return pos0, pos1, src, tile_expert, n_valid.astype(jnp.int32).reshape(1)


def _experts_packed_kernel(te_ref, nv_ref, x_ref, wg_ref, wu_ref, wd_ref, *rest):
    o_ref, acc_ref = rest[-2:]
    j, f = pl.program_id(0), pl.program_id(1)
    valid = j < nv_ref[0]

    @pl.when(valid & (f == 0))
    def _():
        acc_ref[...] = jnp.zeros(acc_ref.shape, F32)

    @pl.when(valid)
    def _():
        acc_ref[...] += _swiglu_partial(_unpack_rows(x_ref[...]).astype(BF16), wg_ref, wu_ref, wd_ref)

    @pl.when(valid & (f == pl.num_programs(1) - 1))
    def _():
        o_ref[...] = _pack_rows(acc_ref[...])


def moe_experts_packed(x_part, tile_expert, n_valid, wg, wu, wd, y_prev, part, n_rows, tf=1408):
    rows_part, half = x_part.shape
    d = 2 * half
    nf = wg.shape[2] // tf
    tiles_part = rows_part // MOE_TILE

    def tile(j, nv):
        return jnp.maximum(jnp.minimum(j, nv[0] - 1), 0)

    def fstep(j, f, nv):
        return jnp.where(j < nv[0], f, nf - 1)

    in_specs = [pl.BlockSpec((MOE_TILE, half), lambda j, f, te, nv: (tile(j, nv), 0)),
                pl.BlockSpec((None, d, tf), lambda j, f, te, nv: (te[j], 0, fstep(j, f, nv))),
                pl.BlockSpec((None, d, tf), lambda j, f, te, nv: (te[j], 0, fstep(j, f, nv))),
                pl.BlockSpec((None, tf, d), lambda j, f, te, nv: (te[j], fstep(j, f, nv), 0))]
    args = [tile_expert, n_valid, x_part, wg, wu, wd]
    aliases = {}
    if y_prev is not None:
        in_specs.append(pl.BlockSpec(memory_space=pl.ANY))
        args.append(y_prev)
        aliases = {len(args) - 1: 0}
    return pl.pallas_call(
        _experts_packed_kernel,
        grid_spec=pltpu.PrefetchScalarGridSpec(
            num_scalar_prefetch=2, grid=(tiles_part, nf), in_specs=in_specs,
            out_specs=pl.BlockSpec((MOE_TILE, half), lambda j, f, te, nv: (part * tiles_part + tile(j, nv), 0)),
            scratch_shapes=[pltpu.VMEM((MOE_TILE, d), F32)]),
        out_shape=jax.ShapeDtypeStruct((n_rows, half), jnp.int32),
        input_output_aliases=aliases,
        compiler_params=_params("arbitrary", "arbitrary"),
        name="moe_experts",
    )(*args)


def _combine_packed_kernel(y0_ref, y1_ref, route_ref, x1_ref, mod_ref, o_ref):
    rt = route_ref[...]
    f = rt[:, 2:3] * _unpack_rows(y0_ref[...]) + rt[:, 3:4] * _unpack_rows(y1_ref[...])
    o_ref[...] = x1_ref[...] + mod_ref[5:6, :] * f


def moe_combine_packed(y_pairs, route, x1, mod, seq, tm=512):
    t, d = x1.shape
    nt = t // tm
    tiles_per_batch = seq // tm
    row = lambda i: (i, 0)
    return pl.pallas_call(
        _combine_packed_kernel,
        grid=(nt,),
        in_specs=[pl.BlockSpec((tm, d // 2), row), pl.BlockSpec((tm, d // 2), lambda i: (i + nt, 0)),
                  pl.BlockSpec((tm, LANES), row), pl.BlockSpec((tm, d), row),
                  pl.BlockSpec((None, 6, d), lambda i: (i // tiles_per_batch, 0, 0))],
        out_specs=pl.BlockSpec((tm, d), row),
        out_shape=jax.ShapeDtypeStruct((t, d), F32),
        compiler_params=_params("parallel"),
        name="moe_combine",
    )(y_pairs, y_pairs, route, x1, mod)


def ffn_moe_sc(x1, h_packed, route, mod, wg, wu, wd, seq):
    t = x1.shape[0]
    e0, e1 = route[:, 0].astype(jnp.int32), route[:, 1].astype(jnp.int32)
    pos0, pos1, src, tile_expert, n_valid = _moe_layout(e0, e1, t, wg.shape[0])
    n_rows = src.shape[0]
    rows_part = n_rows // MOE_PARTS
    tiles_part = rows_part // MOE_TILE
    assert rows_part * MOE_PARTS == n_rows and tiles_part * MOE_TILE == rows_part
    h_parts = [sc_gather_rows(h_packed, src[p * rows_part:(p + 1) * rows_part]) for p in range(MOE_PARTS)]
    y_sorted = None
    for p in range(MOE_PARTS):
        n_valid_part = jnp.clip(n_valid - p * tiles_part, 0, tiles_part)
        y_sorted = moe_experts_packed(h_parts[p], tile_expert[p * tiles_part:(p + 1) * tiles_part], n_valid_part,
                                      wg, wu, wd, y_sorted, p, n_rows)
    y_pairs = sc_gather_rows(y_sorted, jnp.concatenate([pos0, pos1]))
    return moe_combine_packed(y_pairs, route, x1, mod, seq)


def _final_norm_kernel(x_ref, g_ref, o_ref):
    x = x_ref[...]
    o_ref[...] = (x * lax.rsqrt(jnp.mean(x * x, axis=-1, keepdims=True) + EPS)) * g_ref[...]


def final_norm(x, g, tm=512):
    t, d = x.shape
    return pl.pallas_call(
        _final_norm_kernel,
        grid=(t // tm,),
        in_specs=[pl.BlockSpec((tm, d), lambda i: (i, 0)), pl.BlockSpec((1, d), lambda i: (0, 0))],
        out_specs=pl.BlockSpec((tm, d), lambda i: (i, 0)),
        out_shape=jax.ShapeDtypeStruct((t, d), F32),
        compiler_params=_params("parallel"),
        name="final_norm",
    )(x, g)


def _cast_kernel(w_ref, o_ref):
    o_ref[...] = w_ref[...].astype(o_ref.dtype)


def cast_layer_bf16(w, layer):
    _, e, a, b = w.shape
    ta = a // 2
    return pl.pallas_call(
        _cast_kernel,
        grid=(e, a // ta),
        in_specs=[pl.BlockSpec((None, None, ta, b), lambda i, r: (layer, i, r, 0))],
        out_specs=pl.BlockSpec((None, ta, b), lambda i, r: (i, r, 0)),
        out_shape=jax.ShapeDtypeStruct((e, a, b), BF16),
        compiler_params=_params("parallel", "parallel"),
        name="cast_bf16",
    )(w)


def _rope_tables(positions, dim):
    rot = dim // ROPE_FRACTION
    half = rot // 2
    inv = 1.0 / (ROPE_THETA ** (np.arange(0, rot, 2, dtype=np.float32) / rot))
    ang = positions.reshape(-1).astype(F32)[:, None] * jnp.asarray(inv, F32)
    cos, sin = jnp.cos(ang), jnp.sin(ang)
    t = ang.shape[0]
    ones = jnp.ones((t, dim - rot), F32)
    zeros = lambda w: jnp.zeros((t, w), F32)
    reps = LANES // dim
    c = jnp.tile(jnp.concatenate([cos, cos, ones], axis=1), (1, reps))
    sa = jnp.tile(jnp.concatenate([-sin, zeros(dim - half)], axis=1), (1, reps))
    sb = jnp.tile(jnp.concatenate([zeros(half), sin, zeros(dim - rot)], axis=1), (1, reps))
    return c, sa, sb


def _relayout_w_in(w):
    pts = np.cumsum([0, 384, 64, 64, 256, 64, 4, 256, 256, 256, 384, 384, 384])
    (q_a, k_a, v_a, q_i, k_i, w_i, q_b, k_b, v_b, q_c, k_c, v_c) = [w[..., pts[j]:pts[j + 1]] for j in range(12)]
    w_i_pad = jnp.concatenate([w_i, jnp.zeros(w.shape[:-1] + (LANES - IDX_HEADS,), w.dtype)], axis=-1)
    return jnp.concatenate([q_a, k_a, k_a, v_a, v_a, q_i, k_i, k_i, w_i_pad,
                            q_b, k_b, v_b, q_c, k_c, v_c], axis=-1).astype(BF16)


def kernel(x, c, positions, w_in, w_out, diff_lambda, diff_subln, w_ada, b_ada, g_attn, g_ffn, w_ff_gate,
           w_ff_up, w_ff_down, w_router, w_exp_gate, w_exp_up, w_exp_down, g_final):
    batch, seq, d = x.shape
    depth = w_in.shape[0]
    t = batch * seq
    tabs = _rope_tables(positions, HEAD_DIM) + _rope_tables(positions, B_QK_DIM)
    mod_all = adaln_mod(c, w_ada, b_ada).reshape(depth, batch, 6, d)
    w_in_pad = _relayout_w_in(w_in)
    wo = w_out.astype(BF16)
    xf = x.reshape(t, d)
    for layer in range(depth):
        mod = mod_all[layer]
        lam_init = 0.8 - 0.6 * math.exp(-0.3 * layer)
        (qa, kka, vva, qi, kki, wi, qb, kb, vb, qc, kc, vc) = inproj(
            xf, g_attn[layer].reshape(1, d), mod, tabs, w_in_pad, layer, seq)
        ya = dsa_attention(qa, qi, wi, kka, vva, kki, batch, seq)
        g_sub2 = jnp.tile(diff_subln[layer], 2).reshape(1, LANES)
        yb = diff_attention(qb, kb, vb, diff_lambda[layer], g_sub2, lam_init, batch, seq)
        yc = moba_attention(qc, kc, vc, batch, seq)
        j = layer // 2
        gf = g_ffn[layer].reshape(1, d)
        if layer % 2 == 0:
            x1, h = outproj(xf, ya, yb, yc, wo, layer, gf, mod, seq)
            wg, wu, wd = (cast_layer_bf16(w[:, None], j)[0] for w in (w_ff_gate, w_ff_up, w_ff_down))
            xf = ffn_dense(x1, h, mod, wg, wu, wd, seq)
        else:
            wr = jnp.concatenate([w_router[j], jnp.zeros((d, LANES - N_EXPERTS), F32)], axis=1)
            wr_hi = wr.astype(BF16)
            wr = jnp.stack([wr_hi, (wr - wr_hi.astype(F32)).astype(BF16)])
            x1, h_packed, route = outproj(xf, ya, yb, yc, wo, layer, gf, mod, seq, w_router=wr)
            wg, wu, wd = (cast_layer_bf16(w, j) for w in (w_exp_gate, w_exp_up, w_exp_down))
            xf = ffn_moe_sc(x1, h_packed, route, mod, wg, wu, wd, seq)
    return final_norm(xf, g_final.reshape(1, d)).reshape(batch, seq, d)
```

```python
import functools
import math

import jax
import jax.numpy as jnp
import numpy as np
from jax import lax
from jax.experimental import pallas as pl
from jax.experimental.pallas import tpu as pltpu
from jax.experimental.pallas import tpu_sc as plsc

F32 = jnp.float32
BF16 = jnp.bfloat16

HEAD_DIM = 64
A_HEADS = 6
IDX_HEADS = 4
B_HEADS = 4
B_QK_DIM = 32
C_HEADS = 6
A_WIDTH, IDX_WIDTH = A_HEADS * HEAD_DIM, IDX_HEADS * HEAD_DIM
B_WIDTH, C_WIDTH = B_HEADS * HEAD_DIM, C_HEADS * HEAD_DIM
DSA_TOPK_MAX = 256
MOBA_BLOCK = 256
MOBA_TOPK = 3
ROPE_THETA = 500000.0
ROPE_FRACTION = 4
SUBLN_EPS = 1e-5
EPS = 1e-6
N_EXPERTS = 8

LANES = 128
NEG = -1e30
INT_MIN = -2 ** 31
MIN_NORMAL_KEY = 0x00800000
MIN_NORMAL_F32 = float(np.float32(2.0 ** -126))
VMEM_LIMIT = 48 * 1024 * 1024
LOG2E = math.log2(math.e)

_G_QA, _G_KKA, _G_VVA, _G_QI, _G_KKI, _G_WI = (0, 384), (384, 512), (512, 640), (640, 896), (896, 1024), (1024, 1152)
_G_QB, _G_KB, _G_VB = (1152, 1408), (1408, 1664), (1664, 1920)
_G_QC, _G_KC, _G_VC = (1920, 2304), (2304, 2688), (2688, 3072)
D_IN_PAD = 3072


def _params(*sem):
    return pltpu.CompilerParams(dimension_semantics=sem, vmem_limit_bytes=VMEM_LIMIT)


def _dot(a, b):
    return jnp.dot(a, b, preferred_element_type=F32)


def _dot_nt(a, b):
    return lax.dot_general(a, b, (((1,), (1,)), ((), ())), preferred_element_type=F32)


def _adaln_kernel(c_ref, w_ref, b_ref, o_ref):
    c = c_ref[...]
    c_act = c * (1.0 / (1.0 + jnp.exp(-c)))
    o_ref[...] = jnp.dot(c_act, w_ref[...], preferred_element_type=F32,
                         precision=lax.Precision.HIGHEST) + b_ref[...]


def adaln_mod(c, w_ada, b_ada, tn=1536):
    depth, d, n = w_ada.shape
    b = c.shape[0]
    return pl.pallas_call(
        _adaln_kernel,
        grid=(depth, n // tn),
        in_specs=[pl.BlockSpec((b, d), lambda l, j: (0, 0)),
                  pl.BlockSpec((None, d, tn), lambda l, j: (l, 0, j)),
                  pl.BlockSpec((None, 1, tn), lambda l, j: (l, 0, j))],
        out_specs=pl.BlockSpec((None, b, tn), lambda l, j: (l, 0, j)),
        out_shape=jax.ShapeDtypeStruct((depth, b, n), F32),
        compiler_params=_params("parallel", "parallel"),
        name="adaln_mod",
    )(c, w_ada, b_ada.reshape(depth, 1, n))


def _norm_mod(x, g, shift, scale, eps=EPS):
    y = x * lax.rsqrt(jnp.mean(x * x, axis=-1, keepdims=True) + eps)
    return (y * g) * (1.0 + scale) + shift


def _rope_store(acc, o_ref, cos, sa, sb, half):
    for j in range(acc.shape[1] // LANES):
        a = acc[:, j * LANES:(j + 1) * LANES]
        r = a * cos + pltpu.roll(a, half, 1) * sb + pltpu.roll(a, LANES - half, 1) * sa
        o_ref[:, j * LANES:(j + 1) * LANES] = r.astype(o_ref.dtype)


def _inproj_kernel(x_ref, g_ref, mod_ref, c64_ref, sa64_ref, sb64_ref, c32_ref, sa32_ref, sb32_ref, w_ref,
                   qa_ref, kka_ref, vva_ref, qi_ref, kki_ref, wi_ref,
                   qb_ref, kb_ref, vb_ref, qc_ref, kc_ref, vc_ref):
    h = _norm_mod(x_ref[...], g_ref[...], mod_ref[0:1, :], mod_ref[1:2, :]).astype(BF16)
    c64, sa64, sb64 = c64_ref[...], sa64_ref[...], sb64_ref[...]
    c32, sa32, sb32 = c32_ref[...], sa32_ref[...], sb32_ref[...]

    def proj(cols):
        return _dot(h, w_ref[:, cols[0]:cols[1]])

    qk_scale = HEAD_DIM ** -0.5 * LOG2E
    _rope_store(proj(_G_QA), qa_ref, c64 * qk_scale, sa64 * qk_scale, sb64 * qk_scale, 8)
    _rope_store(proj(_G_KKA), kka_ref, c64, sa64, sb64, 8)
    vva_ref[...] = proj(_G_VVA).astype(vva_ref.dtype)
    _rope_store(proj(_G_QI), qi_ref, c64, sa64, sb64, 8)
    _rope_store(proj(_G_KKI), kki_ref, c64, sa64, sb64, 8)
    wi_ref[...] = proj(_G_WI) * (IDX_HEADS ** -0.5 * HEAD_DIM ** -0.5)
    b_scale = B_QK_DIM ** -0.5 * LOG2E
    _rope_store(proj(_G_QB), qb_ref, c32 * b_scale, sa32 * b_scale, sb32 * b_scale, 4)
    _rope_store(proj(_G_KB), kb_ref, c32, sa32, sb32, 4)
    vb_ref[...] = proj(_G_VB).astype(vb_ref.dtype)
    _rope_store(proj(_G_QC), qc_ref, c64 * qk_scale, sa64 * qk_scale, sb64 * qk_scale, 8)
    _rope_store(proj(_G_KC), kc_ref, c64, sa64, sb64, 8)
    vc_ref[...] = proj(_G_VC).astype(vc_ref.dtype)


def inproj(x, g, mod, tabs, w_pad, layer, seq, tm=512):
    t, d = x.shape
    tiles_per_batch = seq // tm
    row = lambda i: (i, 0)
    widths = [A_WIDTH, LANES, LANES, IDX_WIDTH, LANES, LANES, B_WIDTH, B_WIDTH, B_WIDTH, C_WIDTH, C_WIDTH, C_WIDTH]
    dtypes = [BF16, BF16, BF16, BF16, BF16, F32, BF16, BF16, BF16, BF16, BF16, BF16]
    return pl.pallas_call(
        _inproj_kernel,
        grid=(t // tm,),
        in_specs=[pl.BlockSpec((tm, d), row),
                  pl.BlockSpec((1, d), lambda i: (0, 0)),
                  pl.BlockSpec((None, 6, d), lambda i: (i // tiles_per_batch, 0, 0))]
                 + [pl.BlockSpec((tm, LANES), row)] * 6
                 + [pl.BlockSpec((None, d, D_IN_PAD), lambda i: (layer, 0, 0))],
        out_specs=[pl.BlockSpec((tm, w), row) for w in widths],
        out_shape=[jax.ShapeDtypeStruct((t, w), dt) for w, dt in zip(widths, dtypes)],
        compiler_params=_params("parallel"),
        name="inproj",
    )(x, g, mod, *tabs, w_pad)


def _init_stats(m_ref, l_ref, acc_ref):
    m_ref[...] = jnp.full(m_ref.shape, -jnp.inf, F32)
    l_ref[...] = jnp.zeros(l_ref.shape, F32)
    acc_ref[...] = jnp.zeros(acc_ref.shape, F32)


def _key_to_f32(k):
    return lax.bitcast_convert_type(jnp.where(k >= 0, k, k ^ 0x7FFFFFFF), F32)


def dsa_attention(qa, qi, wi, kka, vva, kki, batch, seq, tq=256):
    t = qa.shape[0]
    nq = seq // tq
    k_top = min(DSA_TOPK_MAX, seq // 4)
    qrow = lambda b, i: (b * nq + i, 0)
    full = lambda b, i: (b, 0)
    return pl.pallas_call(
        functools.partial(_dsa_kernel_t, tq=tq, k_top=k_top, seq=seq),
        grid=(batch, nq),
        in_specs=[pl.BlockSpec((tq, A_WIDTH), qrow), pl.BlockSpec((tq, IDX_WIDTH), qrow),
                  pl.BlockSpec((tq, LANES), qrow),
                  pl.BlockSpec((seq, LANES), full), pl.BlockSpec((seq, LANES), full),
                  pl.BlockSpec((seq, LANES), full)],
        out_specs=pl.BlockSpec((tq, A_WIDTH), qrow),
        out_shape=jax.ShapeDtypeStruct((t, A_WIDTH), BF16),
        scratch_shapes=[pltpu.VMEM((seq, tq), F32), pltpu.VMEM((seq, tq), jnp.int16),
                        pltpu.VMEM((seq, tq), jnp.int16), pltpu.VMEM((1, LANES, seq), BF16),
                        pltpu.VMEM((3, 2 * tq, LANES), BF16),
                        pltpu.VMEM((3, 1, 2 * tq), F32), pltpu.VMEM((3, 1, 2 * tq), F32),
                        pltpu.VMEM((3, LANES, 2 * tq), F32)],
        compiler_params=_params("parallel", "arbitrary"),
        name="dsa_attention",
    )(qa, qi, wi, kka, vva, kki)


def diff_attention(qb, kb, vb, lam_vec, g_sub2, lam_init, batch, seq, tq=256):
    t = qb.shape[0]
    nq = seq // tq
    qrow = lambda b, i: (b * nq + i, 0)
    full = lambda b, i: (b, 0)
    const = lambda b, i: (0, 0)
    return pl.pallas_call(
        functools.partial(_diff_kernel_t, tq=tq, lam_init=lam_init, seq=seq),
        grid=(batch, nq),
        in_specs=[pl.BlockSpec((tq, B_WIDTH), qrow), pl.BlockSpec((seq, B_WIDTH), full),
                  pl.BlockSpec((seq, B_WIDTH), full),
                  pl.BlockSpec((4, B_QK_DIM), const), pl.BlockSpec((1, LANES), const)],
        out_specs=pl.BlockSpec((tq, B_WIDTH), qrow),
        out_shape=jax.ShapeDtypeStruct((t, B_WIDTH), BF16),
        scratch_shapes=[pltpu.VMEM((2, LANES, seq), BF16), pltpu.VMEM((2, 4 * tq, LANES), BF16),
                        pltpu.VMEM((2, 1, 4 * tq), F32), pltpu.VMEM((2, 1, 4 * tq), F32),
                        pltpu.VMEM((2, LANES, 4 * tq), F32)],
        compiler_params=_params("parallel", "arbitrary"),
        name="diff_attention",
    )(qb, kb, vb, lam_vec, g_sub2)


def moba_attention(qc, kc, vc, batch, seq):
    tq = MOBA_BLOCK
    t = qc.shape[0]
    nb = seq // tq
    n_sel = min(MOBA_TOPK, nb - 1)
    nbp = 8
    assert seq % tq == 0 and nb <= nbp
    qrow = lambda b, i: (b * nb + i, 0)
    full = lambda b, i: (b, 0)
    return pl.pallas_call(
        functools.partial(_moba_kernel_t, tq=tq, nb=nb, n_sel=n_sel, seq=seq),
        grid=(batch, nb),
        in_specs=[pl.BlockSpec((tq, C_WIDTH), qrow), pl.BlockSpec((seq, C_WIDTH), full),
                  pl.BlockSpec((seq, C_WIDTH), full)],
        out_specs=pl.BlockSpec((tq, C_WIDTH), qrow),
        out_shape=jax.ShapeDtypeStruct((t, C_WIDTH), BF16),
        scratch_shapes=[pltpu.VMEM((nbp, C_WIDTH), F32), pltpu.VMEM((3, LANES, seq), BF16),
                        pltpu.VMEM((3, 2 * tq, LANES), BF16), pltpu.VMEM((3, nbp, 2 * tq), F32),
                        pltpu.VMEM((3, 1, 2 * tq), F32), pltpu.VMEM((3, 1, 2 * tq), F32),
                        pltpu.VMEM((3, LANES, 2 * tq), F32)],
        compiler_params=_params("parallel", "arbitrary"),
        name="moba_attention",
    )(qc, kc, vc)


def _attend_t(k, q_ref, g, v_t, m_ref, l_ref, acc_ref, mask=None):
    for j in range(q_ref.shape[1] // LANES):
        cols = slice(j * LANES, (j + 1) * LANES)
        s = _dot_nt(k, q_ref[g, cols, :])
        if mask is not None:
            s = mask(j, s)
        m_prev = m_ref[g, :, cols]
        m_new = jnp.maximum(m_prev, jnp.max(s, axis=0, keepdims=True))
        alpha = jnp.exp2(m_prev - m_new)
        p = jnp.exp2(s - m_new)
        l_ref[g, :, cols] = alpha * l_ref[g, :, cols] + jnp.sum(p, axis=0, keepdims=True)
        acc_ref[g, :, cols] = alpha * acc_ref[g, :, cols] + _dot(v_t, p.astype(BF16))
        m_ref[g, :, cols] = m_new


def _causal_mask_t(tk, tq):
    key = lax.broadcasted_iota(jnp.int32, (tk, LANES), 0)
    qry = lax.broadcasted_iota(jnp.int32, (tk, LANES), 1)

    def mask(j, s):
        return jnp.where(key <= qry + (j * LANES) % tq, s, NEG)
    return mask


def _transpose_values(v_ref, vt_ref, seq, tk):
    for g in range(vt_ref.shape[0]):
        for n in range(seq // tk):
            blk = v_ref[n * tk:(n + 1) * tk, g * LANES:(g + 1) * LANES].astype(F32)
            vt_ref[g, :, n * tk:(n + 1) * tk] = blk.T.astype(vt_ref.dtype)


def _head_pair_out(acc_t, l, tq):
    even = acc_t[0:HEAD_DIM, 0:tq] / l[:, 0:tq]
    odd = acc_t[HEAD_DIM:LANES, tq:2 * tq] / l[:, tq:2 * tq]
    return jnp.concatenate([even, odd], axis=0).T


def _dsa_kernel_t(qa_ref, qi_ref, wiq_ref, kka_ref, vva_ref, kki_ref, o_ref,
                  sc_ref, hi_ref, lo_ref, vt_ref, qs_ref, m_ref, l_ref, acc_ref, *, tq, k_top, seq):
    i = pl.program_id(1)
    nk = i + 1
    t0 = i * tq
    lo = lax.broadcasted_iota(jnp.int32, (tq, LANES), 1) < HEAD_DIM
    zero_b = jnp.zeros((tq, LANES), BF16)

    @pl.when(i == 0)
    def _():
        _transpose_values(vva_ref, vt_ref, seq, tq)

    def stack_heads(q):
        out = []
        for g in range(q.shape[1] // LANES):
            qg = q[:, g * LANES:(g + 1) * LANES]
            out += [jnp.where(lo, qg, zero_b), jnp.where(lo, zero_b, qg)]
        return out

    qa_stack = stack_heads(qa_ref[...])
    for g in range(3):
        qs_ref[g, 0:tq, :] = qa_stack[2 * g]
        qs_ref[g, tq:2 * tq, :] = qa_stack[2 * g + 1]
    qi_stack = jnp.concatenate(stack_heads(qi_ref[...]), axis=0)
    wi_t = wiq_ref[...].T

    def rows(c):
        return pl.ds(pl.multiple_of(c * tq, tq), tq)

    key_pos = lax.broadcasted_iota(jnp.int32, (tq, tq), 0)
    qry_pos = lax.broadcasted_iota(jnp.int32, (tq, tq), 1)

    def idx_body(c, carry):
        r = jnp.maximum(_dot_nt(kki_ref[rows(c), :], qi_stack), 0.0)
        s = wi_t[0:1, :] * r[:, 0:tq]
        for h in range(1, IDX_HEADS):
            s = s + wi_t[h:h + 1, :] * r[:, h * tq:(h + 1) * tq]
        causal = (c * tq + key_pos) <= (t0 + qry_pos)
        s = jnp.where(causal, s, -jnp.inf)
        s = jnp.where(jnp.abs(s) < MIN_NORMAL_F32, 0.0, s)
        sc_ref[rows(c), :] = s
        bits = lax.bitcast_convert_type(s, jnp.int32)
        key = jnp.where(bits >= 0, bits, bits ^ 0x7FFFFFFF)
        hi_ref[rows(c), :] = lax.shift_right_arithmetic(key, jnp.int32(16)).astype(jnp.int16)
        lo_ref[rows(c), :] = ((key & 0xFFFF) - 32768).astype(jnp.int16)
        return carry

    lax.fori_loop(0, nk, idx_body, 0)

    def count16(ref, cand, strict):
        c16 = cand.astype(jnp.int16)

        def body(c, acc):
            x = ref[rows(c), :]
            hit = jnp.where((x > c16) if strict else (x >= c16), jnp.int16(1), jnp.int16(0))
            for r in range(tq // 16):
                acc = acc + hit[r * 16:(r + 1) * 16]
            return acc
        acc = lax.fori_loop(0, nk, body, jnp.zeros((16, tq), jnp.int16))
        return jnp.sum(acc.astype(F32), axis=0, keepdims=True)

    def search16(ref, k_need):
        v0 = jnp.where(count16(ref, jnp.zeros((1, tq), jnp.int32), False) >= k_need, 0, -32768).astype(jnp.int32)

        def bisect(b, v):
            trial = v | lax.shift_left(jnp.int32(1), 14 - b)
            return jnp.where(count16(ref, trial, False) >= k_need, trial, v)
        return lax.fori_loop(0, 15, bisect, v0)

    kf = float(k_top)

    def search():
        hi_k = search16(hi_ref, kf)
        above = count16(hi_ref, hi_k, True)
        hi16 = hi_k.astype(jnp.int16)

        def keep_bucket(c, carry):
            lo_ref[rows(c), :] = jnp.where(hi_ref[rows(c), :] == hi16, lo_ref[rows(c), :], jnp.int16(-32768))
            return carry
        lax.fori_loop(0, nk, keep_bucket, 0)
        lo_k = search16(lo_ref, kf - above)
        n_gt = above + count16(lo_ref, lo_k, True)
        return lax.shift_left(hi_k, jnp.int32(16)) | (lo_k + 32768), kf - n_gt

    key, need = lax.cond(t0 + tq <= k_top,
                         lambda: (jnp.full((1, tq), INT_MIN, jnp.int32), jnp.full((1, tq), kf, F32)), search)
    thr = _key_to_f32(key)
    thr_next = _key_to_f32(jnp.where(key == 0, MIN_NORMAL_KEY, key + 1))
    all_sel = (t0 + lax.broadcasted_iota(jnp.int32, (1, tq), 1)) < k_top
    lower = (qry_pos <= key_pos).astype(BF16)

    _init_stats(m_ref, l_ref, acc_ref)

    def att_body(c, tie_run):
        s_idx = sc_ref[rows(c), :]
        ge = s_idx >= thr
        gt = s_idx >= thr_next
        tie = jnp.where(gt, 0.0, jnp.where(ge, 1.0, 0.0))
        prefix = _dot(lower, tie.astype(BF16)) + tie_run
        take = jnp.where(gt, 1.0, jnp.where(prefix <= need, tie, 0.0))
        take = jnp.where(all_sel, 1.0, take)
        take = jnp.where(s_idx > -jnp.inf, take, 0.0)

        def mask(j, s):
            q0 = (j * LANES) % tq
            return jnp.where(take[:, q0:q0 + LANES] > 0.5, s, NEG)

        kk = kka_ref[rows(c), :]
        vt = vt_ref[0, :, rows(c)]
        for g in range(3):
            _attend_t(kk, qs_ref, g, vt, m_ref, l_ref, acc_ref, mask)
        return tie_run + jnp.sum(tie, axis=0, keepdims=True)

    lax.fori_loop(0, nk, att_body, jnp.zeros((1, tq), F32))

    for g in range(3):
        o_ref[:, g * LANES:(g + 1) * LANES] = _head_pair_out(acc_ref[g], l_ref[g], tq).astype(o_ref.dtype)


def _diff_kernel_t(qb_ref, kb_ref, vb_ref, lam_ref, gsub_ref, o_ref,
                   vt_ref, qs_ref, m_ref, l_ref, acc_ref, *, tq, lam_init, seq):
    i = pl.program_id(1)
    lane = lax.broadcasted_iota(jnp.int32, (tq, LANES), 1)
    zero_b = jnp.zeros((tq, LANES), BF16)

    @pl.when(i == 0)
    def _():
        _transpose_values(vb_ref, vt_ref, seq, tq)

    qb = qb_ref[...]
    for g in range(2):
        qg = qb[:, g * LANES:(g + 1) * LANES]
        for j in range(4):
            qs_ref[g, j * tq:(j + 1) * tq, :] = jnp.where(lane // B_QK_DIM == j, qg, zero_b)
    _init_stats(m_ref, l_ref, acc_ref)

    def rows(c):
        return pl.ds(pl.multiple_of(c * tq, tq), tq)

    def step(c, masked):
        mask = _causal_mask_t(tq, tq) if masked else None
        for g in range(2):
            _attend_t(kb_ref[rows(c), g * LANES:(g + 1) * LANES], qs_ref, g, vt_ref[g, :, rows(c)],
                      m_ref, l_ref, acc_ref, mask)

    def body(c, carry):
        step(c, False)
        return carry

    lax.fori_loop(0, i, body, 0)
    step(i, True)

    lv = lam_ref[...]
    lam = (jnp.exp(jnp.sum(lv[0:1] * lv[1:2], axis=1, keepdims=True))
           - jnp.exp(jnp.sum(lv[2:3] * lv[3:4], axis=1, keepdims=True)) + lam_init)
    gsub = gsub_ref[...]
    for g in range(2):
        acc_t, l = acc_ref[g], l_ref[g]

        def prob(r0, j):
            return acc_t[r0:r0 + HEAD_DIM, j * tq:(j + 1) * tq] / l[:, j * tq:(j + 1) * tq]

        halves = []
        for r0, j in ((0, 0), (HEAD_DIM, 2)):
            o = prob(r0, j) - lam * prob(r0, j + 1)
            ms = jnp.mean(o * o, axis=0, keepdims=True)
            halves.append(o * lax.rsqrt(ms + SUBLN_EPS))
        y = (jnp.concatenate(halves, axis=0).T * gsub) * (1.0 - lam_init)
        o_ref[:, g * LANES:(g + 1) * LANES] = y.astype(o_ref.dtype)


def _moba_kernel_t(qc_ref, kc_ref, vc_ref, o_ref, kmean_ref, vt_ref, qs_ref, bias_ref, m_ref, l_ref, acc_ref,
                   *, tq, nb, n_sel, seq):
    i = pl.program_id(1)
    nbp = kmean_ref.shape[0]
    lane = lax.broadcasted_iota(jnp.int32, (tq, LANES), 1)
    lo = lane < HEAD_DIM
    zero_b = jnp.zeros((tq, LANES), BF16)

    @pl.when(i == 0)
    def _():
        _transpose_values(vc_ref, vt_ref, seq, tq)
        kmean_ref[...] = jnp.zeros(kmean_ref.shape, F32)
        for n in range(nb):
            kblk = kc_ref[n * tq:(n + 1) * tq, :].astype(F32)
            kmean_ref[n:n + 1, :] = jnp.mean(kblk, axis=0, keepdims=True)

    qc = qc_ref[...]
    sub = lax.broadcasted_iota(jnp.int32, (nbp, 2 * tq), 0)
    past = sub < i
    for g in range(3):
        qg = qc[:, g * LANES:(g + 1) * LANES]
        q2 = jnp.concatenate([jnp.where(lo, qg, zero_b), jnp.where(lo, zero_b, qg)], axis=0)
        km = kmean_ref[:, g * LANES:(g + 1) * LANES]
        km_hi = km.astype(BF16)
        gt = _dot_nt(km_hi, q2) + _dot_nt((km - km_hi.astype(F32)).astype(BF16), q2)
        rows_ = []
        for n in range(nbp):
            gn = gt[n:n + 1, :]
            beats = jnp.where(sub < n, jnp.where(gt >= gn, 1.0, 0.0), jnp.where(gt > gn, 1.0, 0.0))
            beats = jnp.where(sub == n, 0.0, jnp.where(past, beats, 0.0))
            rank = jnp.sum(beats, axis=0, keepdims=True)
            rows_.append(jnp.where(rank < n_sel, 0.0, NEG))
        bias_ref[g] = jnp.where(past, jnp.concatenate(rows_, axis=0), NEG)
        qs_ref[g] = q2
    _init_stats(m_ref, l_ref, acc_ref)

    def rows(c):
        return pl.ds(pl.multiple_of(c * tq, tq), tq)

    def body(c, carry):
        for g in range(3):
            bias = bias_ref[g, pl.ds(c, 1), :]

            def mask(j, s):
                return s + bias[:, j * LANES:(j + 1) * LANES]

            _attend_t(kc_ref[rows(c), g * LANES:(g + 1) * LANES], qs_ref, g, vt_ref[g, :, rows(c)],
                      m_ref, l_ref, acc_ref, mask)
        return carry

    lax.fori_loop(0, i, body, 0)

    causal = _causal_mask_t(tq, tq)
    for g in range(3):
        _attend_t(kc_ref[rows(i), g * LANES:(g + 1) * LANES], qs_ref, g, vt_ref[g, :, rows(i)],
                  m_ref, l_ref, acc_ref, causal)
        o_ref[:, g * LANES:(g + 1) * LANES] = _head_pair_out(acc_ref[g], l_ref[g], tq).astype(o_ref.dtype)


def _route(h, wr_ref, cw_ref):
    hi = h.astype(BF16)
    lo = (h - hi.astype(F32)).astype(BF16)
    logits = _dot(hi, wr_ref[0]) + (_dot(lo, wr_ref[0]) + _dot(hi, wr_ref[1]))
    lane = lax.broadcasted_iota(jnp.int32, logits.shape, 1)
    lg = jnp.where(lane < N_EXPERTS, logits, -jnp.inf)
    v0 = jnp.max(lg, axis=1, keepdims=True)
    i0 = jnp.min(jnp.where(lg == v0, lane, LANES), axis=1, keepdims=True)
    lg1 = jnp.where(lane == i0, -jnp.inf, lg)
    v1 = jnp.max(lg1, axis=1, keepdims=True)
    i1 = jnp.min(jnp.where(lg1 == v1, lane, LANES), axis=1, keepdims=True)
    e1 = jnp.exp(v1 - v0)
    w0 = 1.0 / (1.0 + e1)
    cw_ref[...] = (jnp.where(lane == 0, i0.astype(F32), 0.0) + jnp.where(lane == 1, i1.astype(F32), 0.0)
                   + jnp.where(lane == 2, w0, 0.0) + jnp.where(lane == 3, e1 * w0, 0.0))


def _outproj_kernel(x_ref, ya_ref, yb_ref, yc_ref, wo_ref, g_ref, mod_ref, *rest, with_router):
    if with_router:
        wr_ref, x1_ref, h_ref, cw_ref = rest
    else:
        x1_ref, h_ref = rest
    ab = A_WIDTH + B_WIDTH
    y = (_dot(ya_ref[...], wo_ref[0:A_WIDTH, :]) + _dot(yb_ref[...], wo_ref[A_WIDTH:ab, :])
         + _dot(yc_ref[...], wo_ref[ab:ab + C_WIDTH, :]))
    x1 = x_ref[...] + mod_ref[2:3, :] * y
    x1_ref[...] = x1
    h = _norm_mod(x1, g_ref[...], mod_ref[3:4, :], mod_ref[4:5, :])
    if with_router:
        h_ref[...] = _pack_rows(h)
        _route(h, wr_ref, cw_ref)
    else:
        h_ref[...] = h.astype(BF16)


def outproj(x, ya, yb, yc, wo, layer, g_ffn, mod, seq, w_router=None, tm=512):
    t, d = x.shape
    tiles_per_batch = seq // tm
    row = lambda i: (i, 0)
    const = lambda i: (0, 0)
    with_router = w_router is not None
    in_specs = [pl.BlockSpec((tm, d), row), pl.BlockSpec((tm, A_WIDTH), row), pl.BlockSpec((tm, B_WIDTH), row),
                pl.BlockSpec((tm, C_WIDTH), row), pl.BlockSpec((None, d, d), lambda i: (layer, 0, 0)),
                pl.BlockSpec((1, d), const),
                pl.BlockSpec((None, 6, d), lambda i: (i // tiles_per_batch, 0, 0))]
    out_specs = [pl.BlockSpec((tm, d), row), pl.BlockSpec((tm, d), row)]
    out_shape = [jax.ShapeDtypeStruct((t, d), F32), jax.ShapeDtypeStruct((t, d), BF16)]
    args = [x, ya, yb, yc, wo, g_ffn, mod]
    if with_router:
        in_specs.append(pl.BlockSpec((2, d, LANES), lambda i: (0, 0, 0)))
        out_specs[1] = pl.BlockSpec((tm, d // 2), row)
        out_shape[1] = jax.ShapeDtypeStruct((t, d // 2), jnp.int32)
        out_specs.append(pl.BlockSpec((tm, LANES), row))
        out_shape.append(jax.ShapeDtypeStruct((t, LANES), F32))
        args.append(w_router)
    return pl.pallas_call(
        functools.partial(_outproj_kernel, with_router=with_router),
        grid=(t // tm,), in_specs=in_specs, out_specs=out_specs, out_shape=out_shape,
        compiler_params=_params("parallel"),
        name="outproj_router" if with_router else "outproj",
    )(*args)


def _swiglu_partial(h, wg_ref, wu_ref, wd_ref):
    a = _dot(h, wg_ref[...])
    u = _dot(h, wu_ref[...])
    act = (a * (1.0 / (1.0 + jnp.exp(-a)))) * u
    return _dot(act.astype(BF16), wd_ref[...])


def _ffn_kernel(x1_ref, h_ref, mod_ref, wg_ref, wu_ref, wd_ref, o_ref, acc_ref):
    f = pl.program_id(1)

    @pl.when(f == 0)
    def _():
        acc_ref[...] = jnp.zeros(acc_ref.shape, F32)

    acc_ref[...] += _swiglu_partial(h_ref[...], wg_ref, wu_ref, wd_ref)

    @pl.when(f == pl.num_programs(1) - 1)
    def _():
        o_ref[...] = x1_ref[...] + mod_ref[5:6, :] * acc_ref[...]


def ffn_dense(x1, h, mod, wg, wu, wd, seq, tm=512, tf=1408):
    t, d = x1.shape
    dff = wg.shape[1]
    tiles_per_batch = seq // tm
    row = lambda i, f: (i, 0)
    return pl.pallas_call(
        _ffn_kernel,
        grid=(t // tm, dff // tf),
        in_specs=[pl.BlockSpec((tm, d), row), pl.BlockSpec((tm, d), row),
                  pl.BlockSpec((None, 6, d), lambda i, f: (i // tiles_per_batch, 0, 0)),
                  pl.BlockSpec((d, tf), lambda i, f: (0, f)), pl.BlockSpec((d, tf), lambda i, f: (0, f)),
                  pl.BlockSpec((tf, d), lambda i, f: (f, 0))],
        out_specs=pl.BlockSpec((tm, d), row),
        out_shape=jax.ShapeDtypeStruct((t, d), F32),
        scratch_shapes=[pltpu.VMEM((tm, d), F32)],
        compiler_params=_params("parallel", "arbitrary"),
        name="ffn_dense",
    )(x1, h, mod, wg, wu, wd)


MOE_TILE = 512
MOE_PARTS = 3
SC_CORES, SC_SUBCORES = 2, 16
SC_ROWS = 64
HI16 = -65536


def _pack_rows(x):
    c = x.shape[1] // 2
    bits = lax.bitcast_convert_type(x.astype(jnp.bfloat16).astype(F32), jnp.int32)
    return lax.shift_right_logical(bits[:, :c], jnp.int32(16)) | (bits[:, c:] & jnp.int32(HI16))


def _unpack_rows(w):
    lo = lax.bitcast_convert_type(lax.shift_left(w, jnp.int32(16)), F32)
    hi = lax.bitcast_convert_type(w & jnp.int32(HI16), F32)
    return jnp.concatenate([lo, hi], axis=1)


def sc_gather_rows(table, idx):
    d = table.shape[1]
    b = idx.shape[0]
    per_worker = b // (SC_CORES * SC_SUBCORES)
    assert per_worker * SC_CORES * SC_SUBCORES == b and per_worker % (2 * SC_ROWS) == 0
    mesh = plsc.VectorSubcoreMesh(core_axis_name="c", subcore_axis_name="s")
    idx_buf = pltpu.VMEM((SC_ROWS,), jnp.int32)
    row_buf = pltpu.VMEM((SC_ROWS, d), table.dtype)

    @functools.partial(
        pl.kernel, mesh=mesh, out_type=jax.ShapeDtypeStruct((b, d), table.dtype),
        scratch_types=[idx_buf, idx_buf, row_buf, row_buf] + [pltpu.SemaphoreType.DMA] * 4,
        name="sc_gather_rows")
    def gather(table_hbm, idx_hbm, out_hbm, idx0, idx1, rows0, rows1, sem_g0, sem_g1, sem_w0, sem_w1):
        base = (lax.axis_index("s") * SC_CORES + lax.axis_index("c")) * per_worker

        @pl.loop(0, per_worker // (2 * SC_ROWS))
        def _(pair):
            off0 = pl.multiple_of(base + pair * (2 * SC_ROWS), SC_ROWS)
            off1 = pl.multiple_of(off0 + SC_ROWS, SC_ROWS)
            pltpu.sync_copy(idx_hbm.at[pl.ds(off0, SC_ROWS)], idx0)
            pltpu.sync_copy(idx_hbm.at[pl.ds(off1, SC_ROWS)], idx1)
            gather0 = pltpu.async_copy(table_hbm.at[idx0], rows0, sem_g0)
            gather1 = pltpu.async_copy(table_hbm.at[idx1], rows1, sem_g1)
            gather0.wait()
            write0 = pltpu.async_copy(rows0, out_hbm.at[pl.ds(off0, SC_ROWS)], sem_w0)
            gather1.wait()
            write1 = pltpu.async_copy(rows1, out_hbm.at[pl.ds(off1, SC_ROWS)], sem_w1)
            write0.wait()
            write1.wait()

    return gather(table, idx)


def _moe_layout(e0, e1, t, n_exp):
    experts = jnp.arange(n_exp, dtype=jnp.int32)
    routed = ((e0[:, None] == experts) | (e1[:, None] == experts)).astype(jnp.int32)
    csum = jnp.cumsum(routed, axis=0)
    padded = (csum[-1] + MOE_TILE - 1) // MOE_TILE * MOE_TILE
    ends = jnp.cumsum(padded)
    pos = (ends - padded)[None, :] + csum - routed
    pos0 = jnp.take_along_axis(pos, e0[:, None], axis=1)[:, 0]
    pos1 = jnp.take_along_axis(pos, e1[:, None], axis=1)[:, 0]
    n_rows = 2 * t + n_exp * MOE_TILE
    tok = jnp.arange(t, dtype=jnp.int32)
    src = jnp.zeros((n_rows,), jnp.int32).at[jnp.concatenate([pos0, pos1])].set(
        jnp.concatenate([tok, tok]), unique_indices=True, mode="promise_in_bounds")
    n_tiles = n_rows // MOE_TILE
    n_valid = ends[-1] // MOE_TILE
    tile_expert = jnp.sum(jnp.arange(n_tiles)[:, None] * MOE_TILE >= ends[None, :], axis=1)
    tile_expert = tile_expert[jnp.minimum(jnp.arange(n_tiles), n_valid - 1)].astype(jnp.int32)
    return pos0, pos1, src, tile_expert, n_valid.astype(jnp.int32).reshape(1)


def _experts_packed_kernel(te_ref, nv_ref, x_ref, wg_ref, wu_ref, wd_ref, *rest):
    o_ref, acc_ref = rest[-2:]
    j, f = pl.program_id(0), pl.program_id(1)
    valid = j < nv_ref[0]

    @pl.when(valid & (f == 0))
    def _():
        acc_ref[...] = jnp.zeros(acc_ref.shape, F32)

    @pl.when(valid)
    def _():
        acc_ref[...] += _swiglu_partial(_unpack_rows(x_ref[...]).astype(BF16), wg_ref, wu_ref, wd_ref)

    @pl.when(valid & (f == pl.num_programs(1) - 1))
    def _():
        o_ref[...] = _pack_rows(acc_ref[...])


def moe_experts_packed(x_part, tile_expert, n_valid, wg, wu, wd, y_prev, part, n_rows, tf=1408):
    rows_part, half = x_part.shape
    d = 2 * half
    nf = wg.shape[2] // tf
    tiles_part = rows_part // MOE_TILE

    def tile(j, nv):
        return jnp.maximum(jnp.minimum(j, nv[0] - 1), 0)

    def fstep(j, f, nv):
        return jnp.where(j < nv[0], f, nf - 1)

    in_specs = [pl.BlockSpec((MOE_TILE, half), lambda j, f, te, nv: (tile(j, nv), 0)),
                pl.BlockSpec((None, d, tf), lambda j, f, te, nv: (te[j], 0, fstep(j, f, nv))),
                pl.BlockSpec((None, d, tf), lambda j, f, te, nv: (te[j], 0, fstep(j, f, nv))),
                pl.BlockSpec((None, tf, d), lambda j, f, te, nv: (te[j], fstep(j, f, nv), 0))]
    args = [tile_expert, n_valid, x_part, wg, wu, wd]
    aliases = {}
    if y_prev is not None:
        in_specs.append(pl.BlockSpec(memory_space=pl.ANY))
        args.append(y_prev)
        aliases = {len(args) - 1: 0}
    return pl.pallas_call(
        _experts_packed_kernel,
        grid_spec=pltpu.PrefetchScalarGridSpec(
            num_scalar_prefetch=2, grid=(tiles_part, nf), in_specs=in_specs,
            out_specs=pl.BlockSpec((MOE_TILE, half), lambda j, f, te, nv: (part * tiles_part + tile(j, nv), 0)),
            scratch_shapes=[pltpu.VMEM((MOE_TILE, d), F32)]),
        out_shape=jax.ShapeDtypeStruct((n_rows, half), jnp.int32),
        input_output_aliases=aliases,
        compiler_params=_params("arbitrary", "arbitrary"),
        name="moe_experts",
    )(*args)


def _combine_packed_kernel(y0_ref, y1_ref, route_ref, x1_ref, mod_ref, o_ref):
    rt = route_ref[...]
    f = rt[:, 2:3] * _unpack_rows(y0_ref[...]) + rt[:, 3:4] * _unpack_rows(y1_ref[...])
    o_ref[...] = x1_ref[...] + mod_ref[5:6, :] * f


def moe_combine_packed(y_pairs, route, x1, mod, seq, tm=512):
    t, d = x1.shape
    nt = t // tm
    tiles_per_batch = seq // tm
    row = lambda i: (i, 0)
    return pl.pallas_call(
        _combine_packed_kernel,
        grid=(nt,),
        in_specs=[pl.BlockSpec((tm, d // 2), row), pl.BlockSpec((tm, d // 2), lambda i: (i + nt, 0)),
                  pl.BlockSpec((tm, LANES), row), pl.BlockSpec((tm, d), row),
                  pl.BlockSpec((None, 6, d), lambda i: (i // tiles_per_batch, 0, 0))],
        out_specs=pl.BlockSpec((tm, d), row),
        out_shape=jax.ShapeDtypeStruct((t, d), F32),
        compiler_params=_params("parallel"),
        name="moe_combine",
    )(y_pairs, y_pairs, route, x1, mod)


def ffn_moe_sc(x1, h_packed, route, mod, wg, wu, wd, seq):
    t = x1.shape[0]
    e0, e1 = route[:, 0].astype(jnp.int32), route[:, 1].astype(jnp.int32)
    pos0, pos1, src, tile_expert, n_valid = _moe_layout(e0, e1, t, wg.shape[0])
    n_rows = src.shape[0]
    rows_part = n_rows // MOE_PARTS
    tiles_part = rows_part // MOE_TILE
    assert rows_part * MOE_PARTS == n_rows and tiles_part * MOE_TILE == rows_part
    h_parts = [sc_gather_rows(h_packed, src[p * rows_part:(p + 1) * rows_part]) for p in range(MOE_PARTS)]
    y_sorted = None
    for p in range(MOE_PARTS):
        n_valid_part = jnp.clip(n_valid - p * tiles_part, 0, tiles_part)
        y_sorted = moe_experts_packed(h_parts[p], tile_expert[p * tiles_part:(p + 1) * tiles_part], n_valid_part,
                                      wg, wu, wd, y_sorted, p, n_rows)
    y_pairs = sc_gather_rows(y_sorted, jnp.concatenate([pos0, pos1]))
    return moe_combine_packed(y_pairs, route, x1, mod, seq)


def _final_norm_kernel(x_ref, g_ref, o_ref):
    x = x_ref[...]
    o_ref[...] = (x * lax.rsqrt(jnp.mean(x * x, axis=-1, keepdims=True) + EPS)) * g_ref[...]


def final_norm(x, g, tm=512):
    t, d = x.shape
    return pl.pallas_call(
        _final_norm_kernel,
        grid=(t // tm,),
        in_specs=[pl.BlockSpec((tm, d), lambda i: (i, 0)), pl.BlockSpec((1, d), lambda i: (0, 0))],
        out_specs=pl.BlockSpec((tm, d), lambda i: (i, 0)),
        out_shape=jax.ShapeDtypeStruct((t, d), F32),
        compiler_params=_params("parallel"),
        name="final_norm",
    )(x, g)


def _cast_kernel(w_ref, o_ref):
    o_ref[...] = w_ref[...].astype(o_ref.dtype)


def cast_layer_bf16(w, layer):
    _, e, a, b = w.shape
    ta = a // 2
    return pl.pallas_call(
        _cast_kernel,
        grid=(e, a // ta),
        in_specs=[pl.BlockSpec((None, None, ta, b), lambda i, r: (layer, i, r, 0))],
        out_specs=pl.BlockSpec((None, ta, b), lambda i, r: (i, r, 0)),
        out_shape=jax.ShapeDtypeStruct((e, a, b), BF16),
        compiler_params=_params("parallel", "parallel"),
        name="cast_bf16",
    )(w)


def _rope_tables(positions, dim):
    rot = dim // ROPE_FRACTION
    half = rot // 2
    inv = 1.0 / (ROPE_THETA ** (np.arange(0, rot, 2, dtype=np.float32) / rot))
    ang = positions.reshape(-1).astype(F32)[:, None] * jnp.asarray(inv, F32)
    cos, sin = jnp.cos(ang), jnp.sin(ang)
    t = ang.shape[0]
    ones = jnp.ones((t, dim - rot), F32)
    zeros = lambda w: jnp.zeros((t, w), F32)
    reps = LANES // dim
    c = jnp.tile(jnp.concatenate([cos, cos, ones], axis=1), (1, reps))
    sa = jnp.tile(jnp.concatenate([-sin, zeros(dim - half)], axis=1), (1, reps))
    sb = jnp.tile(jnp.concatenate([zeros(half), sin, zeros(dim - rot)], axis=1), (1, reps))
    return c, sa, sb


def _relayout_w_in(w):
    pts = np.cumsum([0, 384, 64, 64, 256, 64, 4, 256, 256, 256, 384, 384, 384])
    (q_a, k_a, v_a, q_i, k_i, w_i, q_b, k_b, v_b, q_c, k_c, v_c) = [w[..., pts[j]:pts[j + 1]] for j in range(12)]
    w_i_pad = jnp.concatenate([w_i, jnp.zeros(w.shape[:-1] + (LANES - IDX_HEADS,), w.dtype)], axis=-1)
    return jnp.concatenate([q_a, k_a, k_a, v_a, v_a, q_i, k_i, k_i, w_i_pad,
                            q_b, k_b, v_b, q_c, k_c, v_c], axis=-1).astype(BF16)


def kernel(x, c, positions, w_in, w_out, diff_lambda, diff_subln, w_ada, b_ada, g_attn, g_ffn, w_ff_gate,
           w_ff_up, w_ff_down, w_router, w_exp_gate, w_exp_up, w_exp_down, g_final):
    batch, seq, d = x.shape
    depth = w_in.shape[0]
    t = batch * seq
    tabs = _rope_tables(positions, HEAD_DIM) + _rope_tables(positions, B_QK_DIM)
    mod_all = adaln_mod(c, w_ada, b_ada).reshape(depth, batch, 6, d)
    w_in_pad = _relayout_w_in(w_in)
    wo = w_out.astype(BF16)
    xf = x.reshape(t, d)
    for layer in range(depth):
        mod = mod_all[layer]
        lam_init = 0.8 - 0.6 * math.exp(-0.3 * layer)
        (qa, kka, vva, qi, kki, wi, qb, kb, vb, qc, kc, vc) = inproj(
            xf, g_attn[layer].reshape(1, d), mod, tabs, w_in_pad, layer, seq)
        ya = dsa_attention(qa, qi, wi, kka, vva, kki, batch, seq)
        g_sub2 = jnp.tile(diff_subln[layer], 2).reshape(1, LANES)
        yb = diff_attention(qb, kb, vb, diff_lambda[layer], g_sub2, lam_init, batch, seq)
        yc = moba_attention(qc, kc, vc, batch, seq)
        j = layer // 2
        gf = g_ffn[layer].reshape(1, d)
        if layer % 2 == 0:
            x1, h = outproj(xf, ya, yb, yc, wo, layer, gf, mod, seq)
            wg, wu, wd = (cast_layer_bf16(w[:, None], j)[0] for w in (w_ff_gate, w_ff_up, w_ff_down))
            xf = ffn_dense(x1, h, mod, wg, wu, wd, seq)
        else:
            wr = jnp.concatenate([w_router[j], jnp.zeros((d, LANES - N_EXPERTS), F32)], axis=1)
            wr_hi = wr.astype(BF16)
            wr = jnp.stack([wr_hi, (wr - wr_hi.astype(F32)).astype(BF16)])
            x1, h_packed, route = outproj(xf, ya, yb, yc, wo, layer, gf, mod, seq, w_router=wr)
            wg, wu, wd = (cast_layer_bf16(w, j) for w in (w_exp_gate, w_exp_up, w_exp_down))
            xf = ffn_moe_sc(x1, h_packed, route, mod, wg, wu, wd, seq)
    return final_norm(xf, g_final.reshape(1, d)).reshape(batch, seq, d)
```

```python
import functools
import math

import jax
import jax.numpy as jnp
import numpy as np
from jax import lax
from jax.experimental import pallas as pl
from jax.experimental.pallas import tpu as pltpu
from jax.experimental.pallas import tpu_sc as plsc

F32 = jnp.float32
BF16 = jnp.bfloat16

HEAD_DIM = 64
A_HEADS = 6
IDX_HEADS = 4
B_HEADS = 4
B_QK_DIM = 32
C_HEADS = 6
A_WIDTH, IDX_WIDTH = A_HEADS * HEAD_DIM, IDX_HEADS * HEAD_DIM
B_WIDTH, C_WIDTH = B_HEADS * HEAD_DIM, C_HEADS * HEAD_DIM
DSA_TOPK_MAX = 256
MOBA_BLOCK = 256
MOBA_TOPK = 3
ROPE_THETA = 500000.0
ROPE_FRACTION = 4
SUBLN_EPS = 1e-5
EPS = 1e-6
N_EXPERTS = 8

LANES = 128
NEG = -1e30
INT_MIN = -2 ** 31
MIN_NORMAL_KEY = 0x00800000
MIN_NORMAL_F32 = float(np.float32(2.0 ** -126))
VMEM_LIMIT = 48 * 1024 * 1024
LOG2E = math.log2(math.e)

_G_QA, _G_KKA, _G_VVA, _G_QI, _G_KKI, _G_WI = (0, 384), (384, 512), (512, 640), (640, 896), (896, 1024), (1024, 1152)
_G_QB, _G_KB, _G_VB = (1152, 1408), (1408, 1664), (1664, 1920)
_G_QC, _G_KC, _G_VC = (1920, 2304), (2304, 2688), (2688, 3072)
D_IN_PAD = 3072


def _params(*sem):
    return pltpu.CompilerParams(dimension_semantics=sem, vmem_limit_bytes=VMEM_LIMIT)


def _dot(a, b):
    return jnp.dot(a, b, preferred_element_type=F32)


def _dot_nt(a, b):
    return lax.dot_general(a, b, (((1,), (1,)), ((), ())), preferred_element_type=F32)


def _adaln_kernel(c_ref, w_ref, b_ref, o_ref):
    c = c_ref[...]
    c_act = c * (1.0 / (1.0 + jnp.exp(-c)))
    o_ref[...] = jnp.dot(c_act, w_ref[...], preferred_element_type=F32,
                         precision=lax.Precision.HIGHEST) + b_ref[...]


def adaln_mod(c, w_ada, b_ada, tn=1536):
    depth, d, n = w_ada.shape
    b = c.shape[0]
    return pl.pallas_call(
        _adaln_kernel,
        grid=(depth, n // tn),
        in_specs=[pl.BlockSpec((b, d), lambda l, j: (0, 0)),
                  pl.BlockSpec((None, d, tn), lambda l, j: (l, 0, j)),
                  pl.BlockSpec((None, 1, tn), lambda l, j: (l, 0, j))],
        out_specs=pl.BlockSpec((None, b, tn), lambda l, j: (l, 0, j)),
        out_shape=jax.ShapeDtypeStruct((depth, b, n), F32),
        compiler_params=_params("parallel", "parallel"),
        name="adaln_mod",
    )(c, w_ada, b_ada.reshape(depth, 1, n))


def _norm_mod(x, g, shift, scale, eps=EPS):
    y = x * lax.rsqrt(jnp.mean(x * x, axis=-1, keepdims=True) + eps)
    return (y * g) * (1.0 + scale) + shift


def _rope_store(acc, o_ref, cos, sa, sb, half):
    for j in range(acc.shape[1] // LANES):
        a = acc[:, j * LANES:(j + 1) * LANES]
        r = a * cos + pltpu.roll(a, half, 1) * sb + pltpu.roll(a, LANES - half, 1) * sa
        o_ref[:, j * LANES:(j + 1) * LANES] = r.astype(o_ref.dtype)


def _inproj_kernel(x_ref, g_ref, mod_ref, c64_ref, sa64_ref, sb64_ref, c32_ref, sa32_ref, sb32_ref, w_ref,
                   qa_ref, kka_ref, vva_ref, qi_ref, kki_ref, wi_ref,
                   qb_ref, kb_ref, vb_ref, qc_ref, kc_ref, vc_ref):
    h = _norm_mod(x_ref[...], g_ref[...], mod_ref[0:1, :], mod_ref[1:2, :]).astype(BF16)
    c64, sa64, sb64 = c64_ref[...], sa64_ref[...], sb64_ref[...]
    c32, sa32, sb32 = c32_ref[...], sa32_ref[...], sb32_ref[...]

    def proj(cols):
        return _dot(h, w_ref[:, cols[0]:cols[1]])

    qk_scale = HEAD_DIM ** -0.5 * LOG2E
    _rope_store(proj(_G_QA), qa_ref, c64 * qk_scale, sa64 * qk_scale, sb64 * qk_scale, 8)
    _rope_store(proj(_G_KKA), kka_ref, c64, sa64, sb64, 8)
    vva_ref[...] = proj(_G_VVA).astype(vva_ref.dtype)
    _rope_store(proj(_G_QI), qi_ref, c64, sa64, sb64, 8)
    _rope_store(proj(_G_KKI), kki_ref, c64, sa64, sb64, 8)
    wi_ref[...] = proj(_G_WI) * (IDX_HEADS ** -0.5 * HEAD_DIM ** -0.5)
    b_scale = B_QK_DIM ** -0.5 * LOG2E
    _rope_store(proj(_G_QB), qb_ref, c32 * b_scale, sa32 * b_scale, sb32 * b_scale, 4)
    _rope_store(proj(_G_KB), kb_ref, c32, sa32, sb32, 4)
    vb_ref[...] = proj(_G_VB).astype(vb_ref.dtype)
    _rope_store(proj(_G_QC), qc_ref, c64 * qk_scale, sa64 * qk_scale, sb64 * qk_scale, 8)
    _rope_store(proj(_G_KC), kc_ref, c64, sa64, sb64, 8)
    vc_ref[...] = proj(_G_VC).astype(vc_ref.dtype)


def inproj(x, g, mod, tabs, w_pad, layer, seq, tm=512):
    t, d = x.shape
    tiles_per_batch = seq // tm
    row = lambda i: (i, 0)
    widths = [A_WIDTH, LANES, LANES, IDX_WIDTH, LANES, LANES, B_WIDTH, B_WIDTH, B_WIDTH, C_WIDTH, C_WIDTH, C_WIDTH]
    dtypes = [BF16, BF16, BF16, BF16, BF16, F32, BF16, BF16, BF16, BF16, BF16, BF16]
    return pl.pallas_call(
        _inproj_kernel,
        grid=(t // tm,),
        in_specs=[pl.BlockSpec((tm, d), row),
                  pl.BlockSpec((1, d), lambda i: (0, 0)),
                  pl.BlockSpec((None, 6, d), lambda i: (i // tiles_per_batch, 0, 0))]
                 + [pl.BlockSpec((tm, LANES), row)] * 6
                 + [pl.BlockSpec((None, d, D_IN_PAD), lambda i: (layer, 0, 0))],
        out_specs=[pl.BlockSpec((tm, w), row) for w in widths],
        out_shape=[jax.ShapeDtypeStruct((t, w), dt) for w, dt in zip(widths, dtypes)],
        compiler_params=_params("parallel"),
        name="inproj",
    )(x, g, mod, *tabs, w_pad)


def _init_stats(m_ref, l_ref, acc_ref):
    m_ref[...] = jnp.full(m_ref.shape, -jnp.inf, F32)
    l_ref[...] = jnp.zeros(l_ref.shape, F32)
    acc_ref[...] = jnp.zeros(acc_ref.shape, F32)


def _key_to_f32(k):
    return lax.bitcast_convert_type(jnp.where(k >= 0, k, k ^ 0x7FFFFFFF), F32)


def dsa_attention(qa, qi, wi, kka, vva, kki, batch, seq, tq=256):
    t = qa.shape[0]
    nq = seq // tq
    k_top = min(DSA_TOPK_MAX, seq // 4)
    qrow = lambda b, i: (b * nq + i, 0)
    full = lambda b, i: (b, 0)
    return pl.pallas_call(
        functools.partial(_dsa_kernel_t, tq=tq, k_top=k_top, seq=seq),
        grid=(batch, nq),
        in_specs=[pl.BlockSpec((tq, A_WIDTH), qrow), pl.BlockSpec((tq, IDX_WIDTH), qrow),
                  pl.BlockSpec((tq, LANES), qrow),
                  pl.BlockSpec((seq, LANES), full), pl.BlockSpec((seq, LANES), full),
                  pl.BlockSpec((seq, LANES), full)],
        out_specs=pl.BlockSpec((tq, A_WIDTH), qrow),
        out_shape=jax.ShapeDtypeStruct((t, A_WIDTH), BF16),
        scratch_shapes=[pltpu.VMEM((seq, tq), F32), pltpu.VMEM((seq, tq), jnp.int16),
                        pltpu.VMEM((seq, tq), jnp.int16), pltpu.VMEM((1, LANES, seq), BF16),
                        pltpu.VMEM((3, 2 * tq, LANES), BF16),
                        pltpu.VMEM((3, 1, 2 * tq), F32), pltpu.VMEM((3, 1, 2 * tq), F32),
                        pltpu.VMEM((3, LANES, 2 * tq), F32)],
        compiler_params=_params("parallel", "arbitrary"),
        name="dsa_attention",
    )(qa, qi, wi, kka, vva, kki)


def diff_attention(qb, kb, vb, lam_vec, g_sub2, lam_init, batch, seq, tq=256):
    t = qb.shape[0]
    nq = seq // tq
    qrow = lambda b, i: (b * nq + i, 0)
    full = lambda b, i: (b, 0)
    const = lambda b, i: (0, 0)
    return pl.pallas_call(
        functools.partial(_diff_kernel_t, tq=tq, lam_init=lam_init, seq=seq),
        grid=(batch, nq),
        in_specs=[pl.BlockSpec((tq, B_WIDTH), qrow), pl.BlockSpec((seq, B_WIDTH), full),
                  pl.BlockSpec((seq, B_WIDTH), full),
                  pl.BlockSpec((4, B_QK_DIM), const), pl.BlockSpec((1, LANES), const)],
        out_specs=pl.BlockSpec((tq, B_WIDTH), qrow),
        out_shape=jax.ShapeDtypeStruct((t, B_WIDTH), BF16),
        scratch_shapes=[pltpu.VMEM((2, LANES, seq), BF16), pltpu.VMEM((2, 4 * tq, LANES), BF16),
                        pltpu.VMEM((2, 1, 4 * tq), F32), pltpu.VMEM((2, 1, 4 * tq), F32),
                        pltpu.VMEM((2, LANES, 4 * tq), F32)],
        compiler_params=_params("parallel", "arbitrary"),
        name="diff_attention",
    )(qb, kb, vb, lam_vec, g_sub2)


def moba_attention(qc, kc, vc, batch, seq):
    tq = MOBA_BLOCK
    t = qc.shape[0]
    nb = seq // tq
    n_sel = min(MOBA_TOPK, nb - 1)
    nbp = 8
    assert seq % tq == 0 and nb <= nbp
    qrow = lambda b, i: (b * nb + i, 0)
    full = lambda b, i: (b, 0)
    return pl.pallas_call(
        functools.partial(_moba_kernel_t, tq=tq, nb=nb, n_sel=n_sel, seq=seq),
        grid=(batch, nb),
        in_specs=[pl.BlockSpec((tq, C_WIDTH), qrow), pl.BlockSpec((seq, C_WIDTH), full),
                  pl.BlockSpec((seq, C_WIDTH), full)],
        out_specs=pl.BlockSpec((tq, C_WIDTH), qrow),
        out_shape=jax.ShapeDtypeStruct((t, C_WIDTH), BF16),
        scratch_shapes=[pltpu.VMEM((nbp, C_WIDTH), F32), pltpu.VMEM((3, LANES, seq), BF16),
                        pltpu.VMEM((3, 2 * tq, LANES), BF16), pltpu.VMEM((3, nbp, 2 * tq), F32),
                        pltpu.VMEM((3, 1, 2 * tq), F32), pltpu.VMEM((3, 1, 2 * tq), F32),
                        pltpu.VMEM((3, LANES, 2 * tq), F32)],
        compiler_params=_params("parallel", "arbitrary"),
        name="moba_attention",
    )(qc, kc, vc)


def _attend_t(k, q_ref, g, v_t, m_ref, l_ref, acc_ref, mask=None):
    for j in range(q_ref.shape[1] // LANES):
        cols = slice(j * LANES, (j + 1) * LANES)
        s = _dot_nt(k, q_ref[g, cols, :])
        if mask is not None:
            s = mask(j, s)
        m_prev = m_ref[g, :, cols]
        m_new = jnp.maximum(m_prev, jnp.max(s, axis=0, keepdims=True))
        alpha = jnp.exp2(m_prev - m_new)
        p = jnp.exp2(s - m_new)
        l_ref[g, :, cols] = alpha * l_ref[g, :, cols] + jnp.sum(p, axis=0, keepdims=True)
        acc_ref[g, :, cols] = alpha * acc_ref[g, :, cols] + _dot(v_t, p.astype(BF16))
        m_ref[g, :, cols] = m_new


def _causal_mask_t(tk, tq):
    key = lax.broadcasted_iota(jnp.int32, (tk, LANES), 0)
    qry = lax.broadcasted_iota(jnp.int32, (tk, LANES), 1)

    def mask(j, s):
        return jnp.where(key <= qry + (j * LANES) % tq, s, NEG)
    return mask


def _transpose_values(v_ref, vt_ref, seq, tk):
    for g in range(vt_ref.shape[0]):
        for n in range(seq // tk):
            blk = v_ref[n * tk:(n + 1) * tk, g * LANES:(g + 1) * LANES].astype(F32)
            vt_ref[g, :, n * tk:(n + 1) * tk] = blk.T.astype(vt_ref.dtype)


def _head_pair_out(acc_t, l, tq):
    even = acc_t[0:HEAD_DIM, 0:tq] / l[:, 0:tq]
    odd = acc_t[HEAD_DIM:LANES, tq:2 * tq] / l[:, tq:2 * tq]
    return jnp.concatenate([even, odd], axis=0).T


def _dsa_kernel_t(qa_ref, qi_ref, wiq_ref, kka_ref, vva_ref, kki_ref, o_ref,
                  sc_ref, hi_ref, lo_ref, vt_ref, qs_ref, m_ref, l_ref, acc_ref, *, tq, k_top, seq):
    i = pl.program_id(1)
    nk = i + 1
    t0 = i * tq
    lo = lax.broadcasted_iota(jnp.int32, (tq, LANES), 1) < HEAD_DIM
    zero_b = jnp.zeros((tq, LANES), BF16)

    @pl.when(i == 0)
    def _():
        _transpose_values(vva_ref, vt_ref, seq, tq)

    def stack_heads(q):
        out = []
        for g in range(q.shape[1] // LANES):
            qg = q[:, g * LANES:(g + 1) * LANES]
            out += [jnp.where(lo, qg, zero_b), jnp.where(lo, zero_b, qg)]
        return out

    qa_stack = stack_heads(qa_ref[...])
    for g in range(3):
        qs_ref[g, 0:tq, :] = qa_stack[2 * g]
        qs_ref[g, tq:2 * tq, :] = qa_stack[2 * g + 1]
    qi_stack = jnp.concatenate(stack_heads(qi_ref[...]), axis=0)
    wi_t = wiq_ref[...].T

    def rows(c):
        return pl.ds(pl.multiple_of(c * tq, tq), tq)

    key_pos = lax.broadcasted_iota(jnp.int32, (tq, tq), 0)
    qry_pos = lax.broadcasted_iota(jnp.int32, (tq, tq), 1)

    def idx_body(c, carry):
        r = jnp.maximum(_dot_nt(kki_ref[rows(c), :], qi_stack), 0.0)
        s = wi_t[0:1, :] * r[:, 0:tq]
        for h in range(1, IDX_HEADS):
            s = s + wi_t[h:h + 1, :] * r[:, h * tq:(h + 1) * tq]
        causal = (c * tq + key_pos) <= (t0 + qry_pos)
        s = jnp.where(causal, s, -jnp.inf)
        s = jnp.where(jnp.abs(s) < MIN_NORMAL_F32, 0.0, s)
        sc_ref[rows(c), :] = s
        bits = lax.bitcast_convert_type(s, jnp.int32)
        key = jnp.where(bits >= 0, bits, bits ^ 0x7FFFFFFF)
        hi_ref[rows(c), :] = lax.shift_right_arithmetic(key, jnp.int32(16)).astype(jnp.int16)
        lo_ref[rows(c), :] = ((key & 0xFFFF) - 32768).astype(jnp.int16)
        return carry

    lax.fori_loop(0, nk, idx_body, 0)

    def count16(ref, cand, strict):
        c16 = cand.astype(jnp.int16)

        def body(c, acc):
            x = ref[rows(c), :]
            hit = jnp.where((x > c16) if strict else (x >= c16), jnp.int16(1), jnp.int16(0))
            for r in range(tq // 16):
                acc = acc + hit[r * 16:(r + 1) * 16]
            return acc
        acc = lax.fori_loop(0, nk, body, jnp.zeros((16, tq), jnp.int16))
        return jnp.sum(acc.astype(F32), axis=0, keepdims=True)

    def search16(ref, k_need):
        v0 = jnp.where(count16(ref, jnp.zeros((1, tq), jnp.int32), False) >= k_need, 0, -32768).astype(jnp.int32)

        def bisect(b, v):
            trial = v | lax.shift_left(jnp.int32(1), 14 - b)
            return jnp.where(count16(ref, trial, False) >= k_need, trial, v)
        return lax.fori_loop(0, 15, bisect, v0)

    kf = float(k_top)

    def search():
        hi_k = search16(hi_ref, kf)
        above = count16(hi_ref, hi_k, True)
        hi16 = hi_k.astype(jnp.int16)

        def keep_bucket(c, carry):
            lo_ref[rows(c), :] = jnp.where(hi_ref[rows(c), :] == hi16, lo_ref[rows(c), :], jnp.int16(-32768))
            return carry
        lax.fori_loop(0, nk, keep_bucket, 0)
        lo_k = search16(lo_ref, kf - above)
        n_gt = above + count16(lo_ref, lo_k, True)
        return lax.shift_left(hi_k, jnp.int32(16)) | (lo_k + 32768), kf - n_gt

    key, need = lax.cond(t0 + tq <= k_top,
                         lambda: (jnp.full((1, tq), INT_MIN, jnp.int32), jnp.full((1, tq), kf, F32)), search)
    thr = _key_to_f32(key)
    thr_next = _key_to_f32(jnp.where(key == 0, MIN_NORMAL_KEY, key + 1))
    all_sel = (t0 + lax.broadcasted_iota(jnp.int32, (1, tq), 1)) < k_top
    lower = (qry_pos <= key_pos).astype(BF16)

    _init_stats(m_ref, l_ref, acc_ref)

    def att_body(c, tie_run):
        s_idx = sc_ref[rows(c), :]
        ge = s_idx >= thr
        gt = s_idx >= thr_next
        tie = jnp.where(gt, 0.0, jnp.where(ge, 1.0, 0.0))
        prefix = _dot(lower, tie.astype(BF16)) + tie_run
        take = jnp.where(gt, 1.0, jnp.where(prefix <= need, tie, 0.0))
        take = jnp.where(all_sel, 1.0, take)
        take = jnp.where(s_idx > -jnp.inf, take, 0.0)

        def mask(j, s):
            q0 = (j * LANES) % tq
            return jnp.where(take[:, q0:q0 + LANES] > 0.5, s, NEG)

        kk = kka_ref[rows(c), :]
        vt = vt_ref[0, :, rows(c)]
        for g in range(3):
            _attend_t(kk, qs_ref, g, vt, m_ref, l_ref, acc_ref, mask)
        return tie_run + jnp.sum(tie, axis=0, keepdims=True)

    lax.fori_loop(0, nk, att_body, jnp.zeros((1, tq), F32))

    for g in range(3):
        o_ref[:, g * LANES:(g + 1) * LANES] = _head_pair_out(acc_ref[g], l_ref[g], tq).astype(o_ref.dtype)


def _diff_kernel_t(qb_ref, kb_ref, vb_ref, lam_ref, gsub_ref, o_ref,
                   vt_ref, qs_ref, m_ref, l_ref, acc_ref, *, tq, lam_init, seq):
    i = pl.program_id(1)
    lane = lax.broadcasted_iota(jnp.int32, (tq, LANES), 1)
    zero_b = jnp.zeros((tq, LANES), BF16)

    @pl.when(i == 0)
    def _():
        _transpose_values(vb_ref, vt_ref, seq, tq)

    qb = qb_ref[...]
    for g in range(2):
        qg = qb[:, g * LANES:(g + 1) * LANES]
        for j in range(4):
            qs_ref[g, j * tq:(j + 1) * tq, :] = jnp.where(lane // B_QK_DIM == j, qg, zero_b)
    _init_stats(m_ref, l_ref, acc_ref)

    def rows(c):
        return pl.ds(pl.multiple_of(c * tq, tq), tq)

    def step(c, masked):
        mask = _causal_mask_t(tq, tq) if masked else None
        for g in range(2):
            _attend_t(kb_ref[rows(c), g * LANES:(g + 1) * LANES], qs_ref, g, vt_ref[g, :, rows(c)],
                      m_ref, l_ref, acc_ref, mask)

    def body(c, carry):
        step(c, False)
        return carry

    lax.fori_loop(0, i, body, 0)
    step(i, True)

    lv = lam_ref[...]
    lam = (jnp.exp(jnp.sum(lv[0:1] * lv[1:2], axis=1, keepdims=True))
           - jnp.exp(jnp.sum(lv[2:3] * lv[3:4], axis=1, keepdims=True)) + lam_init)
    gsub = gsub_ref[...]
    for g in range(2):
        acc_t, l = acc_ref[g], l_ref[g]

        def prob(r0, j):
            return acc_t[r0:r0 + HEAD_DIM, j * tq:(j + 1) * tq] / l[:, j * tq:(j + 1) * tq]

        halves = []
        for r0, j in ((0, 0), (HEAD_DIM, 2)):
            o = prob(r0, j) - lam * prob(r0, j + 1)
            ms = jnp.mean(o * o, axis=0, keepdims=True)
            halves.append(o * lax.rsqrt(ms + SUBLN_EPS))
        y = (jnp.concatenate(halves, axis=0).T * gsub) * (1.0 - lam_init)
        o_ref[:, g * LANES:(g + 1) * LANES] = y.astype(o_ref.dtype)


def _moba_kernel_t(qc_ref, kc_ref, vc_ref, o_ref, kmean_ref, vt_ref, qs_ref, bias_ref, m_ref, l_ref, acc_ref,
                   *, tq, nb, n_sel, seq):
    i = pl.program_id(1)
    nbp = kmean_ref.shape[0]
    lane = lax.broadcasted_iota(jnp.int32, (tq, LANES), 1)
    lo = lane < HEAD_DIM
    zero_b = jnp.zeros((tq, LANES), BF16)

    @pl.when(i == 0)
    def _():
        _transpose_values(vc_ref, vt_ref, seq, tq)
        kmean_ref[...] = jnp.zeros(kmean_ref.shape, F32)
        for n in range(nb):
            kblk = kc_ref[n * tq:(n + 1) * tq, :].astype(F32)
            kmean_ref[n:n + 1, :] = jnp.mean(kblk, axis=0, keepdims=True)

    qc = qc_ref[...]
    sub = lax.broadcasted_iota(jnp.int32, (nbp, 2 * tq), 0)
    past = sub < i
    for g in range(3):
        qg = qc[:, g * LANES:(g + 1) * LANES]
        q2 = jnp.concatenate([jnp.where(lo, qg, zero_b), jnp.where(lo, zero_b, qg)], axis=0)
        km = kmean_ref[:, g * LANES:(g + 1) * LANES]
        km_hi = km.astype(BF16)
        gt = _dot_nt(km_hi, q2) + _dot_nt((km - km_hi.astype(F32)).astype(BF16), q2)
        rows_ = []
        for n in range(nbp):
            gn = gt[n:n + 1, :]
            beats = jnp.where(sub < n, jnp.where(gt >= gn, 1.0, 0.0), jnp.where(gt > gn, 1.0, 0.0))
            beats = jnp.where(sub == n, 0.0, jnp.where(past, beats, 0.0))
            rank = jnp.sum(beats, axis=0, keepdims=True)
            rows_.append(jnp.where(rank < n_sel, 0.0, NEG))
        bias_ref[g] = jnp.where(past, jnp.concatenate(rows_, axis=0), NEG)
        qs_ref[g] = q2
    _init_stats(m_ref, l_ref, acc_ref)

    def rows(c):
        return pl.ds(pl.multiple_of(c * tq, tq), tq)

    def body(c, carry):
        for g in range(3):
            bias = bias_ref[g, pl.ds(c, 1), :]

            def mask(j, s):
                return s + bias[:, j * LANES:(j + 1) * LANES]

            _attend_t(kc_ref[rows(c), g * LANES:(g + 1) * LANES], qs_ref, g, vt_ref[g, :, rows(c)],
                      m_ref, l_ref, acc_ref, mask)
        return carry

    lax.fori_loop(0, i, body, 0)

    causal = _causal_mask_t(tq, tq)
    for g in range(3):
        _attend_t(kc_ref[rows(i), g * LANES:(g + 1) * LANES], qs_ref, g, vt_ref[g, :, rows(i)],
                  m_ref, l_ref, acc_ref, causal)
        o_ref[:, g * LANES:(g + 1) * LANES] = _head_pair_out(acc_ref[g], l_ref[g], tq).astype(o_ref.dtype)


def _route(h, wr_ref, cw_ref):
    hi = h.astype(BF16)
    lo = (h - hi.astype(F32)).astype(BF16)
    logits = _dot(hi, wr_ref[0]) + (_dot(lo, wr_ref[0]) + _dot(hi, wr_ref[1]))
    lane = lax.broadcasted_iota(jnp.int32, logits.shape, 1)
    lg = jnp.where(lane < N_EXPERTS, logits, -jnp.inf)
    v0 = jnp.max(lg, axis=1, keepdims=True)
    i0 = jnp.min(jnp.where(lg == v0, lane, LANES), axis=1, keepdims=True)
    lg1 = jnp.where(lane == i0, -jnp.inf, lg)
    v1 = jnp.max(lg1, axis=1, keepdims=True)
    i1 = jnp.min(jnp.where(lg1 == v1, lane, LANES), axis=1, keepdims=True)
    e1 = jnp.exp(v1 - v0)
    w0 = 1.0 / (1.0 + e1)
    cw_ref[...] = (jnp.where(lane == 0, i0.astype(F32), 0.0) + jnp.where(lane == 1, i1.astype(F32), 0.0)
                   + jnp.where(lane == 2, w0, 0.0) + jnp.where(lane == 3, e1 * w0, 0.0))


def _outproj_kernel(x_ref, ya_ref, yb_ref, yc_ref, wo_ref, g_ref, mod_ref, *rest, with_router):
    if with_router:
        wr_ref, x1_ref, h_ref, cw_ref = rest
    else:
        x1_ref, h_ref = rest
    ab = A_WIDTH + B_WIDTH
    y = (_dot(ya_ref[...], wo_ref[0:A_WIDTH, :]) + _dot(yb_ref[...], wo_ref[A_WIDTH:ab, :])
         + _dot(yc_ref[...], wo_ref[ab:ab + C_WIDTH, :]))
    x1 = x_ref[...] + mod_ref[2:3, :] * y
    x1_ref[...] = x1
    h = _norm_mod(x1, g_ref[...], mod_ref[3:4, :], mod_ref[4:5, :])
    if with_router:
        h_ref[...] = _pack_rows(h)
        _route(h, wr_ref, cw_ref)
    else:
        h_ref[...] = h.astype(BF16)


def outproj(x, ya, yb, yc, wo, layer, g_ffn, mod, seq, w_router=None, tm=512):
    t, d = x.shape
    tiles_per_batch = seq // tm
    row = lambda i: (i, 0)
    const = lambda i: (0, 0)
    with_router = w_router is not None
    in_specs = [pl.BlockSpec((tm, d), row), pl.BlockSpec((tm, A_WIDTH), row), pl.BlockSpec((tm, B_WIDTH), row),
                pl.BlockSpec((tm, C_WIDTH), row), pl.BlockSpec((None, d, d), lambda i: (layer, 0, 0)),
                pl.BlockSpec((1, d), const),
                pl.BlockSpec((None, 6, d), lambda i: (i // tiles_per_batch, 0, 0))]
    out_specs = [pl.BlockSpec((tm, d), row), pl.BlockSpec((tm, d), row)]
    out_shape = [jax.ShapeDtypeStruct((t, d), F32), jax.ShapeDtypeStruct((t, d), BF16)]
    args = [x, ya, yb, yc, wo, g_ffn, mod]
    if with_router:
        in_specs.append(pl.BlockSpec((2, d, LANES), lambda i: (0, 0, 0)))
        out_specs[1] = pl.BlockSpec((tm, d // 2), row)
        out_shape[1] = jax.ShapeDtypeStruct((t, d // 2), jnp.int32)
        out_specs.append(pl.BlockSpec((tm, LANES), row))
        out_shape.append(jax.ShapeDtypeStruct((t, LANES), F32))
        args.append(w_router)
    return pl.pallas_call(
        functools.partial(_outproj_kernel, with_router=with_router),
        grid=(t // tm,), in_specs=in_specs, out_specs=out_specs, out_shape=out_shape,
        compiler_params=_params("parallel"),
        name="outproj_router" if with_router else "outproj",
    )(*args)


def _swiglu_partial(h, wg_ref, wu_ref, wd_ref):
    a = _dot(h, wg_ref[...])
    u = _dot(h, wu_ref[...])
    act = (a * (1.0 / (1.0 + jnp.exp(-a)))) * u
    return _dot(act.astype(BF16), wd_ref[...])


def _ffn_kernel(x1_ref, h_ref, mod_ref, wg_ref, wu_ref, wd_ref, o_ref, acc_ref):
    f = pl.program_id(1)

    @pl.when(f == 0)
    def _():
        acc_ref[...] = jnp.zeros(acc_ref.shape, F32)

    acc_ref[...] += _swiglu_partial(h_ref[...], wg_ref, wu_ref, wd_ref)

    @pl.when(f == pl.num_programs(1) - 1)
    def _():
        o_ref[...] = x1_ref[...] + mod_ref[5:6, :] * acc_ref[...]


def ffn_dense(x1, h, mod, wg, wu, wd, seq, tm=512, tf=1408):
    t, d = x1.shape
    dff = wg.shape[1]
    tiles_per_batch = seq // tm
    row = lambda i, f: (i, 0)
    return pl.pallas_call(
        _ffn_kernel,
        grid=(t // tm, dff // tf),
        in_specs=[pl.BlockSpec((tm, d), row), pl.BlockSpec((tm, d), row),
                  pl.BlockSpec((None, 6, d), lambda i, f: (i // tiles_per_batch, 0, 0)),
                  pl.BlockSpec((d, tf), lambda i, f: (0, f)), pl.BlockSpec((d, tf), lambda i, f: (0, f)),
                  pl.BlockSpec((tf, d), lambda i, f: (f, 0))],
        out_specs=pl.BlockSpec((tm, d), row),
        out_shape=jax.ShapeDtypeStruct((t, d), F32),
        scratch_shapes=[pltpu.VMEM((tm, d), F32)],
        compiler_params=_params("parallel", "arbitrary"),
        name="ffn_dense",
    )(x1, h, mod, wg, wu, wd)


MOE_TILE = 512
MOE_PARTS = 3
SC_CORES, SC_SUBCORES = 2, 16
SC_ROWS = 64
HI16 = -65536


def _pack_rows(x):
    c = x.shape[1] // 2
    bits = lax.bitcast_convert_type(x.astype(jnp.bfloat16).astype(F32), jnp.int32)
    return lax.shift_right_logical(bits[:, :c], jnp.int32(16)) | (bits[:, c:] & jnp.int32(HI16))


def _unpack_rows(w):
    lo = lax.bitcast_convert_type(lax.shift_left(w, jnp.int32(16)), F32)
    hi = lax.bitcast_convert_type(w & jnp.int32(HI16), F32)
    return jnp.concatenate([lo, hi], axis=1)


def sc_gather_rows(table, idx):
    d = table.shape[1]
    b = idx.shape[0]
    per_worker = b // (SC_CORES * SC_SUBCORES)
    assert per_worker * SC_CORES * SC_SUBCORES == b and per_worker % (2 * SC_ROWS) == 0
    mesh = plsc.VectorSubcoreMesh(core_axis_name="c", subcore_axis_name="s")
    idx_buf = pltpu.VMEM((SC_ROWS,), jnp.int32)
    row_buf = pltpu.VMEM((SC_ROWS, d), table.dtype)

    @functools.partial(
        pl.kernel, mesh=mesh, out_type=jax.ShapeDtypeStruct((b, d), table.dtype),
        scratch_types=[idx_buf, idx_buf, row_buf, row_buf] + [pltpu.SemaphoreType.DMA] * 4,
        name="sc_gather_rows")
    def gather(table_hbm, idx_hbm, out_hbm, idx0, idx1, rows0, rows1, sem_g0, sem_g1, sem_w0, sem_w1):
        base = (lax.axis_index("s") * SC_CORES + lax.axis_index("c")) * per_worker

        @pl.loop(0, per_worker // (2 * SC_ROWS))
        def _(pair):
            off0 = pl.multiple_of(base + pair * (2 * SC_ROWS), SC_ROWS)
            off1 = pl.multiple_of(off0 + SC_ROWS, SC_ROWS)
            pltpu.sync_copy(idx_hbm.at[pl.ds(off0, SC_ROWS)], idx0)
            pltpu.sync_copy(idx_hbm.at[pl.ds(off1, SC_ROWS)], idx1)
            gather0 = pltpu.async_copy(table_hbm.at[idx0], rows0, sem_g0)
            gather1 = pltpu.async_copy(table_hbm.at[idx1], rows1, sem_g1)
            gather0.wait()
            write0 = pltpu.async_copy(rows0, out_hbm.at[pl.ds(off0, SC_ROWS)], sem_w0)
            gather1.wait()
            write1 = pltpu.async_copy(rows1, out_hbm.at[pl.ds(off1, SC_ROWS)], sem_w1)
            write0.wait()
            write1.wait()

    return gather(table, idx)


def _moe_layout(e0, e1, t, n_exp):
    experts = jnp.arange(n_exp, dtype=jnp.int32)
    routed = ((e0[:, None] == experts) | (e1[:, None] == experts)).astype(jnp.int32)
    csum = jnp.cumsum(routed, axis=0)
    padded = (csum[-1] + MOE_TILE - 1) // MOE_TILE * MOE_TILE
    ends = jnp.cumsum(padded)
    pos = (ends - padded)[None, :] + csum - routed
    pos0 = jnp.take_along_axis(pos, e0[:, None], axis=1)[:, 0]
    pos1 = jnp.take_along_axis(pos, e1[:, None], axis=1)[:, 0]
    n_rows = 2 * t + n_exp * MOE_TILE
    tok = jnp.arange(t, dtype=jnp.int32)
    src = jnp.zeros((n_rows,), jnp.int32).at[jnp.concatenate([pos0, pos1])].set(
        jnp.concatenate([tok, tok]), unique_indices=True, mode="promise_in_bounds")
    n_tiles = n_rows // MOE_TILE
    n_valid = ends[-1] // MOE_TILE
    tile_expert = jnp.sum(jnp.arange(n_tiles)[:, None] * MOE_TILE >= ends[None, :], axis=1)
    tile_expert = tile_expert[jnp.minimum(jnp.arange(n_tiles), n_valid - 1)].astype(jnp.int32)
    return pos0, pos1, src, tile_expert, n_valid.astype(jnp.int32).reshape(1)


def _experts_packed_kernel(te_ref, nv_ref, x_ref, wg_ref, wu_ref, wd_ref, *rest):
    o_ref, acc_ref = rest[-2:]
    j, f = pl.program_id(0), pl.program_id(1)
    valid = j < nv_ref[0]

    @pl.when(valid & (f == 0))
    def _():
        acc_ref[...] = jnp.zeros(acc_ref.shape, F32)

    @pl.when(valid)
    def _():
        acc_ref[...] += _swiglu_partial(_unpack_rows(x_ref[...]).astype(BF16), wg_ref, wu_ref, wd_ref)

    @pl.when(valid & (f == pl.num_programs(1) - 1))
    def _():
        o_ref[...] = _pack_rows(acc_ref[...])


def moe_experts_packed(x_part, tile_expert, n_valid, wg, wu, wd, y_prev, part, n_rows, tf=1408):
    rows_part, half = x_part.shape
    d = 2 * half
    nf = wg.shape[2] // tf
    tiles_part = rows_part // MOE_TILE

    def tile(j, nv):
        return jnp.maximum(jnp.minimum(j, nv[0] - 1), 0)

    def fstep(j, f, nv):
        return jnp.where(j < nv[0], f, nf - 1)

    in_specs = [pl.BlockSpec((MOE_TILE, half), lambda j, f, te, nv: (tile(j, nv), 0)),
                pl.BlockSpec((None, d, tf), lambda j, f, te, nv: (te[j], 0, fstep(j, f, nv))),
                pl.BlockSpec((None, d, tf), lambda j, f, te, nv: (te[j], 0, fstep(j, f, nv))),
                pl.BlockSpec((None, tf, d), lambda j, f, te, nv: (te[j], fstep(j, f, nv), 0))]
    args = [tile_expert, n_valid, x_part, wg, wu, wd]
    aliases = {}
    if y_prev is not None:
        in_specs.append(pl.BlockSpec(memory_space=pl.ANY))
        args.append(y_prev)
        aliases = {len(args) - 1: 0}
    return pl.pallas_call(
        _experts_packed_kernel,
        grid_spec=pltpu.PrefetchScalarGridSpec(
            num_scalar_prefetch=2, grid=(tiles_part, nf), in_specs=in_specs,
            out_specs=pl.BlockSpec((MOE_TILE, half), lambda j, f, te, nv: (part * tiles_part + tile(j, nv), 0)),
            scratch_shapes=[pltpu.VMEM((MOE_TILE, d), F32)]),
        out_shape=jax.ShapeDtypeStruct((n_rows, half), jnp.int32),
        input_output_aliases=aliases,
        compiler_params=_params("arbitrary", "arbitrary"),
        name="moe_experts",
    )(*args)


def _combine_packed_kernel(y0_ref, y1_ref, route_ref, x1_ref, mod_ref, o_ref):
    rt = route_ref[...]
    f = rt[:, 2:3] * _unpack_rows(y0_ref[...]) + rt[:, 3:4] * _unpack_rows(y1_ref[...])
    o_ref[...] = x1_ref[...] + mod_ref[5:6, :] * f


def moe_combine_packed(y_pairs, route, x1, mod, seq, tm=512):
    t, d = x1.shape
    nt = t // tm
    tiles_per_batch = seq // tm
    row = lambda i: (i, 0)
    return pl.pallas_call(
        _combine_packed_kernel,
        grid=(nt,),
        in_specs=[pl.BlockSpec((tm, d // 2), row), pl.BlockSpec((tm, d // 2), lambda i: (i + nt, 0)),
                  pl.BlockSpec((tm, LANES), row), pl.BlockSpec((tm, d), row),
                  pl.BlockSpec((None, 6, d), lambda i: (i // tiles_per_batch, 0, 0))],
        out_specs=pl.BlockSpec((tm, d), row),
        out_shape=jax.ShapeDtypeStruct((t, d), F32),
        compiler_params=_params("parallel"),
        name="moe_combine",
    )(y_pairs, y_pairs, route, x1, mod)


def ffn_moe_sc(x1, h_packed, route, mod, wg, wu, wd, seq):
    t = x1.shape[0]
    e0, e1 = route[:, 0].astype(jnp.int32), route[:, 1].astype(jnp.int32)
    pos0, pos1, src, tile_expert, n_valid = _moe_layout(e0, e1, t, wg.shape[0])
    n_rows = src.shape[0]
    rows_part = n_rows // MOE_PARTS
    tiles_part = rows_part // MOE_TILE
    assert rows_part * MOE_PARTS == n_rows and tiles_part * MOE_TILE == rows_part
    h_parts = [sc_gather_rows(h_packed, src[p * rows_part:(p + 1) * rows_part]) for p in range(MOE_PARTS)]
    y_sorted = None
    for p in range(MOE_PARTS):
        n_valid_part = jnp.clip(n_valid - p * tiles_part, 0, tiles_part)
        y_sorted = moe_experts_packed(h_parts[p], tile_expert[p * tiles_part:(p + 1) * tiles_part], n_valid_part,
                                      wg, wu, wd, y_sorted, p, n_rows)
    y_pairs = sc_gather_rows(y_sorted, jnp.concatenate([pos0, pos1]))
    return moe_combine_packed(y_pairs, route, x1, mod, seq)


def _final_norm_kernel(x_ref, g_ref, o_ref):
    x = x_ref[...]
    o_ref[...] = (x * lax.rsqrt(jnp.mean(x * x, axis=-1, keepdims=True) + EPS)) * g_ref[...]


def final_norm(x, g, tm=512):
    t, d = x.shape
    return pl.pallas_call(
        _final_norm_kernel,
        grid=(t // tm,),
        in_specs=[pl.BlockSpec((tm, d), lambda i: (i, 0)), pl.BlockSpec((1, d), lambda i: (0, 0))],
        out_specs=pl.BlockSpec((tm, d), lambda i: (i, 0)),
        out_shape=jax.ShapeDtypeStruct((t, d), F32),
        compiler_params=_params("parallel"),
        name="final_norm",
    )(x, g)


def _cast_kernel(w_ref, o_ref):
    o_ref[...] = w_ref[...].astype(o_ref.dtype)


def cast_layer_bf16(w, layer):
    _, e, a, b = w.shape
    ta = a // 8
    return pl.pallas_call(
        _cast_kernel,
        grid=(e, a // ta),
        in_specs=[pl.BlockSpec((None, None, ta, b), lambda i, r: (layer, i, r, 0))],
        out_specs=pl.BlockSpec((None, ta, b), lambda i, r: (i, r, 0)),
        out_shape=jax.ShapeDtypeStruct((e, a, b), BF16),
        compiler_params=_params("parallel", "parallel"),
        name="cast_bf16",
    )(w)


def _rope_tables(positions, dim):
    rot = dim // ROPE_FRACTION
    half = rot // 2
    inv = 1.0 / (ROPE_THETA ** (np.arange(0, rot, 2, dtype=np.float32) / rot))
    ang = positions.reshape(-1).astype(F32)[:, None] * jnp.asarray(inv, F32)
    cos, sin = jnp.cos(ang), jnp.sin(ang)
    t = ang.shape[0]
    ones = jnp.ones((t, dim - rot), F32)
    zeros = lambda w: jnp.zeros((t, w), F32)
    reps = LANES // dim
    c = jnp.tile(jnp.concatenate([cos, cos, ones], axis=1), (1, reps))
    sa = jnp.tile(jnp.concatenate([-sin, zeros(dim - half)], axis=1), (1, reps))
    sb = jnp.tile(jnp.concatenate([zeros(half), sin, zeros(dim - rot)], axis=1), (1, reps))
    return c, sa, sb


def _relayout_w_in(w):
    pts = np.cumsum([0, 384, 64, 64, 256, 64, 4, 256, 256, 256, 384, 384, 384])
    (q_a, k_a, v_a, q_i, k_i, w_i, q_b, k_b, v_b, q_c, k_c, v_c) = [w[..., pts[j]:pts[j + 1]] for j in range(12)]
    w_i_pad = jnp.concatenate([w_i, jnp.zeros(w.shape[:-1] + (LANES - IDX_HEADS,), w.dtype)], axis=-1)
    return jnp.concatenate([q_a, k_a, k_a, v_a, v_a, q_i, k_i, k_i, w_i_pad,
                            q_b, k_b, v_b, q_c, k_c, v_c], axis=-1).astype(BF16)


def kernel(x, c, positions, w_in, w_out, diff_lambda, diff_subln, w_ada, b_ada, g_attn, g_ffn, w_ff_gate,
           w_ff_up, w_ff_down, w_router, w_exp_gate, w_exp_up, w_exp_down, g_final):
    batch, seq, d = x.shape
    depth = w_in.shape[0]
    t = batch * seq
    tabs = _rope_tables(positions, HEAD_DIM) + _rope_tables(positions, B_QK_DIM)
    mod_all = adaln_mod(c, w_ada, b_ada).reshape(depth, batch, 6, d)
    w_in_pad = _relayout_w_in(w_in)
    wo = w_out.astype(BF16)
    xf = x.reshape(t, d)
    for layer in range(depth):
        mod = mod_all[layer]
        lam_init = 0.8 - 0.6 * math.exp(-0.3 * layer)
        (qa, kka, vva, qi, kki, wi, qb, kb, vb, qc, kc, vc) = inproj(
            xf, g_attn[layer].reshape(1, d), mod, tabs, w_in_pad, layer, seq)
        ya = dsa_attention(qa, qi, wi, kka, vva, kki, batch, seq)
        g_sub2 = jnp.tile(diff_subln[layer], 2).reshape(1, LANES)
        yb = diff_attention(qb, kb, vb, diff_lambda[layer], g_sub2, lam_init, batch, seq)
        yc = moba_attention(qc, kc, vc, batch, seq)
        j = layer // 2
        gf = g_ffn[layer].reshape(1, d)
        if layer % 2 == 0:
            x1, h = outproj(xf, ya, yb, yc, wo, layer, gf, mod, seq)
            wg, wu, wd = (cast_layer_bf16(w[:, None], j)[0] for w in (w_ff_gate, w_ff_up, w_ff_down))
            xf = ffn_dense(x1, h, mod, wg, wu, wd, seq)
        else:
            wr = jnp.concatenate([w_router[j], jnp.zeros((d, LANES - N_EXPERTS), F32)], axis=1)
            wr_hi = wr.astype(BF16)
            wr = jnp.stack([wr_hi, (wr - wr_hi.astype(F32)).astype(BF16)])
            x1, h_packed, route = outproj(xf, ya, yb, yc, wo, layer, gf, mod, seq, w_router=wr)
            wg, wu, wd = (cast_layer_bf16(w, j) for w in (w_exp_gate, w_exp_up, w_exp_down))
            xf = ffn_moe_sc(x1, h_packed, route, mod, wg, wu, wd, seq)
    return final_norm(xf, g_final.reshape(1, d)).reshape(batch, seq, d)
```

```python
import functools
import math

import jax
import jax.numpy as jnp
import numpy as np
from jax import lax
from jax.experimental import pallas as pl
from jax.experimental.pallas import tpu as pltpu
from jax.experimental.pallas import tpu_sc as plsc

F32 = jnp.float32
BF16 = jnp.bfloat16

HEAD_DIM = 64
A_HEADS = 6
IDX_HEADS = 4
B_HEADS = 4
B_QK_DIM = 32
C_HEADS = 6
A_WIDTH, IDX_WIDTH = A_HEADS * HEAD_DIM, IDX_HEADS * HEAD_DIM
B_WIDTH, C_WIDTH = B_HEADS * HEAD_DIM, C_HEADS * HEAD_DIM
DSA_TOPK_MAX = 256
MOBA_BLOCK = 256
MOBA_TOPK = 3
ROPE_THETA = 500000.0
ROPE_FRACTION = 4
SUBLN_EPS = 1e-5
EPS = 1e-6
N_EXPERTS = 8

LANES = 128
NEG = -1e30
INT_MIN = -2 ** 31
MIN_NORMAL_KEY = 0x00800000
MIN_NORMAL_F32 = float(np.float32(2.0 ** -126))
VMEM_LIMIT = 48 * 1024 * 1024
LOG2E = math.log2(math.e)

_G_QA, _G_KKA, _G_VVA, _G_QI, _G_KKI, _G_WI = (0, 384), (384, 512), (512, 640), (640, 896), (896, 1024), (1024, 1152)
_G_QB, _G_KB, _G_VB = (1152, 1408), (1408, 1664), (1664, 1920)
_G_QC, _G_KC, _G_VC = (1920, 2304), (2304, 2688), (2688, 3072)
D_IN_PAD = 3072


def _params(*sem):
    return pltpu.CompilerParams(dimension_semantics=sem, vmem_limit_bytes=VMEM_LIMIT)


def _dot(a, b):
    return jnp.dot(a, b, preferred_element_type=F32)


def _dot_nt(a, b):
    return lax.dot_general(a, b, (((1,), (1,)), ((), ())), preferred_element_type=F32)


def _adaln_kernel(c_ref, w_ref, b_ref, o_ref):
    c = c_ref[...]
    c_act = c * (1.0 / (1.0 + jnp.exp(-c)))
    o_ref[...] = jnp.dot(c_act, w_ref[...], preferred_element_type=F32,
                         precision=lax.Precision.HIGHEST) + b_ref[...]


def adaln_mod(c, w_ada, b_ada, tn=1536):
    depth, d, n = w_ada.shape
    b = c.shape[0]
    return pl.pallas_call(
        _adaln_kernel,
        grid=(depth, n // tn),
        in_specs=[pl.BlockSpec((b, d), lambda l, j: (0, 0)),
                  pl.BlockSpec((None, d, tn), lambda l, j: (l, 0, j)),
                  pl.BlockSpec((None, 1, tn), lambda l, j: (l, 0, j))],
        out_specs=pl.BlockSpec((None, b, tn), lambda l, j: (l, 0, j)),
        out_shape=jax.ShapeDtypeStruct((depth, b, n), F32),
        compiler_params=_params("parallel", "parallel"),
        name="adaln_mod",
    )(c, w_ada, b_ada.reshape(depth, 1, n))


def _norm_mod(x, g, shift, scale, eps=EPS):
    y = x * lax.rsqrt(jnp.mean(x * x, axis=-1, keepdims=True) + eps)
    return (y * g) * (1.0 + scale) + shift


def _rope_store(acc, o_ref, cos, sa, sb, half):
    for j in range(acc.shape[1] // LANES):
        a = acc[:, j * LANES:(j + 1) * LANES]
        r = a * cos + pltpu.roll(a, half, 1) * sb + pltpu.roll(a, LANES - half, 1) * sa
        o_ref[:, j * LANES:(j + 1) * LANES] = r.astype(o_ref.dtype)


def _inproj_kernel(x_ref, g_ref, mod_ref, c64_ref, sa64_ref, sb64_ref, c32_ref, sa32_ref, sb32_ref, w_ref,
                   qa_ref, kka_ref, vva_ref, qi_ref, kki_ref, wi_ref,
                   qb_ref, kb_ref, vb_ref, qc_ref, kc_ref, vc_ref):
    h = _norm_mod(x_ref[...], g_ref[...], mod_ref[0:1, :], mod_ref[1:2, :]).astype(BF16)
    c64, sa64, sb64 = c64_ref[...], sa64_ref[...], sb64_ref[...]
    c32, sa32, sb32 = c32_ref[...], sa32_ref[...], sb32_ref[...]

    def proj(cols):
        return _dot(h, w_ref[:, cols[0]:cols[1]])

    qk_scale = HEAD_DIM ** -0.5 * LOG2E
    _rope_store(proj(_G_QA), qa_ref, c64 * qk_scale, sa64 * qk_scale, sb64 * qk_scale, 8)
    _rope_store(proj(_G_KKA), kka_ref, c64, sa64, sb64, 8)
    vva_ref[...] = proj(_G_VVA).astype(vva_ref.dtype)
    _rope_store(proj(_G_QI), qi_ref, c64, sa64, sb64, 8)
    _rope_store(proj(_G_KKI), kki_ref, c64, sa64, sb64, 8)
    wi_ref[...] = proj(_G_WI) * (IDX_HEADS ** -0.5 * HEAD_DIM ** -0.5)
    b_scale = B_QK_DIM ** -0.5 * LOG2E
    _rope_store(proj(_G_QB), qb_ref, c32 * b_scale, sa32 * b_scale, sb32 * b_scale, 4)
    _rope_store(proj(_G_KB), kb_ref, c32, sa32, sb32, 4)
    vb_ref[...] = proj(_G_VB).astype(vb_ref.dtype)
    _rope_store(proj(_G_QC), qc_ref, c64 * qk_scale, sa64 * qk_scale, sb64 * qk_scale, 8)
    _rope_store(proj(_G_KC), kc_ref, c64, sa64, sb64, 8)
    vc_ref[...] = proj(_G_VC).astype(vc_ref.dtype)


def inproj(x, g, mod, tabs, w_pad, layer, seq, tm=512):
    t, d = x.shape
    tiles_per_batch = seq // tm
    row = lambda i: (i, 0)
    widths = [A_WIDTH, LANES, LANES, IDX_WIDTH, LANES, LANES, B_WIDTH, B_WIDTH, B_WIDTH, C_WIDTH, C_WIDTH, C_WIDTH]
    dtypes = [BF16, BF16, BF16, BF16, BF16, F32, BF16, BF16, BF16, BF16, BF16, BF16]
    return pl.pallas_call(
        _inproj_kernel,
        grid=(t // tm,),
        in_specs=[pl.BlockSpec((tm, d), row),
                  pl.BlockSpec((1, d), lambda i: (0, 0)),
                  pl.BlockSpec((None, 6, d), lambda i: (i // tiles_per_batch, 0, 0))]
                 + [pl.BlockSpec((tm, LANES), row)] * 6
                 + [pl.BlockSpec((None, d, D_IN_PAD), lambda i: (layer, 0, 0))],
        out_specs=[pl.BlockSpec((tm, w), row) for w in widths],
        out_shape=[jax.ShapeDtypeStruct((t, w), dt) for w, dt in zip(widths, dtypes)],
        compiler_params=_params("parallel"),
        name="inproj",
    )(x, g, mod, *tabs, w_pad)


def _init_stats(m_ref, l_ref, acc_ref):
    m_ref[...] = jnp.full(m_ref.shape, -jnp.inf, F32)
    l_ref[...] = jnp.zeros(l_ref.shape, F32)
    acc_ref[...] = jnp.zeros(acc_ref.shape, F32)


def _key_to_f32(k):
    return lax.bitcast_convert_type(jnp.where(k >= 0, k, k ^ 0x7FFFFFFF), F32)


def dsa_attention(qa, qi, wi, kka, vva, kki, batch, seq, tq=256):
    t = qa.shape[0]
    nq = seq // tq
    k_top = min(DSA_TOPK_MAX, seq // 4)
    qrow = lambda b, i: (b * nq + i, 0)
    full = lambda b, i: (b, 0)
    return pl.pallas_call(
        functools.partial(_dsa_kernel_t, tq=tq, k_top=k_top, seq=seq),
        grid=(batch, nq),
        in_specs=[pl.BlockSpec((tq, A_WIDTH), qrow), pl.BlockSpec((tq, IDX_WIDTH), qrow),
                  pl.BlockSpec((tq, LANES), qrow),
                  pl.BlockSpec((seq, LANES), full), pl.BlockSpec((seq, LANES), full),
                  pl.BlockSpec((seq, LANES), full)],
        out_specs=pl.BlockSpec((tq, A_WIDTH), qrow),
        out_shape=jax.ShapeDtypeStruct((t, A_WIDTH), BF16),
        scratch_shapes=[pltpu.VMEM((seq, tq), F32), pltpu.VMEM((seq, tq), jnp.int16),
                        pltpu.VMEM((seq, tq), jnp.int16), pltpu.VMEM((1, LANES, seq), BF16),
                        pltpu.VMEM((3, 2 * tq, LANES), BF16),
                        pltpu.VMEM((3, 1, 2 * tq), F32), pltpu.VMEM((3, 1, 2 * tq), F32),
                        pltpu.VMEM((3, LANES, 2 * tq), F32)],
        compiler_params=_params("parallel", "arbitrary"),
        name="dsa_attention",
    )(qa, qi, wi, kka, vva, kki)


def diff_attention(qb, kb, vb, lam_vec, g_sub2, lam_init, batch, seq, tq=256):
    t = qb.shape[0]
    nq = seq // tq
    qrow = lambda b, i: (b * nq + i, 0)
    full = lambda b, i: (b, 0)
    const = lambda b, i: (0, 0)
    return pl.pallas_call(
        functools.partial(_diff_kernel_t, tq=tq, lam_init=lam_init, seq=seq),
        grid=(batch, nq),
        in_specs=[pl.BlockSpec((tq, B_WIDTH), qrow), pl.BlockSpec((seq, B_WIDTH), full),
                  pl.BlockSpec((seq, B_WIDTH), full),
                  pl.BlockSpec((4, B_QK_DIM), const), pl.BlockSpec((1, LANES), const)],
        out_specs=pl.BlockSpec((tq, B_WIDTH), qrow),
        out_shape=jax.ShapeDtypeStruct((t, B_WIDTH), BF16),
        scratch_shapes=[pltpu.VMEM((2, LANES, seq), BF16), pltpu.VMEM((2, 4 * tq, LANES), BF16),
                        pltpu.VMEM((2, 1, 4 * tq), F32), pltpu.VMEM((2, 1, 4 * tq), F32),
                        pltpu.VMEM((2, LANES, 4 * tq), F32)],
        compiler_params=_params("parallel", "arbitrary"),
        name="diff_attention",
    )(qb, kb, vb, lam_vec, g_sub2)


def moba_attention(qc, kc, vc, batch, seq):
    tq = MOBA_BLOCK
    t = qc.shape[0]
    nb = seq // tq
    n_sel = min(MOBA_TOPK, nb - 1)
    nbp = 8
    assert seq % tq == 0 and nb <= nbp
    qrow = lambda b, i: (b * nb + i, 0)
    full = lambda b, i: (b, 0)
    return pl.pallas_call(
        functools.partial(_moba_kernel_t, tq=tq, nb=nb, n_sel=n_sel, seq=seq),
        grid=(batch, nb),
        in_specs=[pl.BlockSpec((tq, C_WIDTH), qrow), pl.BlockSpec((seq, C_WIDTH), full),
                  pl.BlockSpec((seq, C_WIDTH), full)],
        out_specs=pl.BlockSpec((tq, C_WIDTH), qrow),
        out_shape=jax.ShapeDtypeStruct((t, C_WIDTH), BF16),
        scratch_shapes=[pltpu.VMEM((nbp, C_WIDTH), F32), pltpu.VMEM((3, LANES, seq), BF16),
                        pltpu.VMEM((3, 2 * tq, LANES), BF16), pltpu.VMEM((3, nbp, 2 * tq), F32),
                        pltpu.VMEM((3, 1, 2 * tq), F32), pltpu.VMEM((3, 1, 2 * tq), F32),
                        pltpu.VMEM((3, LANES, 2 * tq), F32)],
        compiler_params=_params("parallel", "arbitrary"),
        name="moba_attention",
    )(qc, kc, vc)


def _attend_t(k, q_ref, g, v_t, m_ref, l_ref, acc_ref, mask=None):
    for j in range(q_ref.shape[1] // LANES):
        cols = slice(j * LANES, (j + 1) * LANES)
        s = _dot_nt(k, q_ref[g, cols, :])
        if mask is not None:
            s = mask(j, s)
        m_prev = m_ref[g, :, cols]
        m_new = jnp.maximum(m_prev, jnp.max(s, axis=0, keepdims=True))
        alpha = jnp.exp2(m_prev - m_new)
        p = jnp.exp2(s - m_new)
        l_ref[g, :, cols] = alpha * l_ref[g, :, cols] + jnp.sum(p, axis=0, keepdims=True)
        acc_ref[g, :, cols] = alpha * acc_ref[g, :, cols] + _dot(v_t, p.astype(BF16))
        m_ref[g, :, cols] = m_new


def _causal_mask_t(tk, tq):
    key = lax.broadcasted_iota(jnp.int32, (tk, LANES), 0)
    qry = lax.broadcasted_iota(jnp.int32, (tk, LANES), 1)

    def mask(j, s):
        return jnp.where(key <= qry + (j * LANES) % tq, s, NEG)
    return mask


def _transpose_values(v_ref, vt_ref, seq, tk):
    for g in range(vt_ref.shape[0]):
        for n in range(seq // tk):
            blk = v_ref[n * tk:(n + 1) * tk, g * LANES:(g + 1) * LANES].astype(F32)
            vt_ref[g, :, n * tk:(n + 1) * tk] = blk.T.astype(vt_ref.dtype)


def _head_pair_out(acc_t, l, tq):
    even = acc_t[0:HEAD_DIM, 0:tq] / l[:, 0:tq]
    odd = acc_t[HEAD_DIM:LANES, tq:2 * tq] / l[:, tq:2 * tq]
    return jnp.concatenate([even, odd], axis=0).T


def _dsa_kernel_t(qa_ref, qi_ref, wiq_ref, kka_ref, vva_ref, kki_ref, o_ref,
                  sc_ref, hi_ref, lo_ref, vt_ref, qs_ref, m_ref, l_ref, acc_ref, *, tq, k_top, seq):
    i = pl.program_id(1)
    nk = i + 1
    t0 = i * tq
    lo = lax.broadcasted_iota(jnp.int32, (tq, LANES), 1) < HEAD_DIM
    zero_b = jnp.zeros((tq, LANES), BF16)

    @pl.when(i == 0)
    def _():
        _transpose_values(vva_ref, vt_ref, seq, tq)

    def stack_heads(q):
        out = []
        for g in range(q.shape[1] // LANES):
            qg = q[:, g * LANES:(g + 1) * LANES]
            out += [jnp.where(lo, qg, zero_b), jnp.where(lo, zero_b, qg)]
        return out

    qa_stack = stack_heads(qa_ref[...])
    for g in range(3):
        qs_ref[g, 0:tq, :] = qa_stack[2 * g]
        qs_ref[g, tq:2 * tq, :] = qa_stack[2 * g + 1]
    qi_stack = jnp.concatenate(stack_heads(qi_ref[...]), axis=0)
    wi_t = wiq_ref[...].T

    def rows(c):
        return pl.ds(pl.multiple_of(c * tq, tq), tq)

    key_pos = lax.broadcasted_iota(jnp.int32, (tq, tq), 0)
    qry_pos = lax.broadcasted_iota(jnp.int32, (tq, tq), 1)

    def idx_body(c, carry):
        r = jnp.maximum(_dot_nt(kki_ref[rows(c), :], qi_stack), 0.0)
        s = wi_t[0:1, :] * r[:, 0:tq]
        for h in range(1, IDX_HEADS):
            s = s + wi_t[h:h + 1, :] * r[:, h * tq:(h + 1) * tq]
        causal = (c * tq + key_pos) <= (t0 + qry_pos)
        s = jnp.where(causal, s, -jnp.inf)
        s = jnp.where(jnp.abs(s) < MIN_NORMAL_F32, 0.0, s)
        sc_ref[rows(c), :] = s
        bits = lax.bitcast_convert_type(s, jnp.int32)
        key = jnp.where(bits >= 0, bits, bits ^ 0x7FFFFFFF)
        hi_ref[rows(c), :] = lax.shift_right_arithmetic(key, jnp.int32(16)).astype(jnp.int16)
        lo_ref[rows(c), :] = ((key & 0xFFFF) - 32768).astype(jnp.int16)
        return carry

    lax.fori_loop(0, nk, idx_body, 0)

    def count16(ref, cand, strict):
        c16 = cand.astype(jnp.int16)

        def body(c, acc):
            x = ref[rows(c), :]
            hit = jnp.where((x > c16) if strict else (x >= c16), jnp.int16(1), jnp.int16(0))
            for r in range(tq // 16):
                acc = acc + hit[r * 16:(r + 1) * 16]
            return acc
        acc = lax.fori_loop(0, nk, body, jnp.zeros((16, tq), jnp.int16))
        return jnp.sum(acc.astype(F32), axis=0, keepdims=True)

    def search16(ref, k_need):
        v0 = jnp.where(count16(ref, jnp.zeros((1, tq), jnp.int32), False) >= k_need, 0, -32768).astype(jnp.int32)

        def bisect(b, v):
            trial = v | lax.shift_left(jnp.int32(1), 14 - b)
            return jnp.where(count16(ref, trial, False) >= k_need, trial, v)
        return lax.fori_loop(0, 15, bisect, v0)

    kf = float(k_top)

    def search():
        hi_k = search16(hi_ref, kf)
        above = count16(hi_ref, hi_k, True)
        hi16 = hi_k.astype(jnp.int16)

        def keep_bucket(c, carry):
            lo_ref[rows(c), :] = jnp.where(hi_ref[rows(c), :] == hi16, lo_ref[rows(c), :], jnp.int16(-32768))
            return carry
        lax.fori_loop(0, nk, keep_bucket, 0)
        lo_k = search16(lo_ref, kf - above)
        n_gt = above + count16(lo_ref, lo_k, True)
        return lax.shift_left(hi_k, jnp.int32(16)) | (lo_k + 32768), kf - n_gt

    key, need = lax.cond(t0 + tq <= k_top,
                         lambda: (jnp.full((1, tq), INT_MIN, jnp.int32), jnp.full((1, tq), kf, F32)), search)
    thr = _key_to_f32(key)
    thr_next = _key_to_f32(jnp.where(key == 0, MIN_NORMAL_KEY, key + 1))
    all_sel = (t0 + lax.broadcasted_iota(jnp.int32, (1, tq), 1)) < k_top
    lower = (qry_pos <= key_pos).astype(BF16)

    _init_stats(m_ref, l_ref, acc_ref)

    def att_body(c, tie_run):
        s_idx = sc_ref[rows(c), :]
        ge = s_idx >= thr
        gt = s_idx >= thr_next
        tie = jnp.where(gt, 0.0, jnp.where(ge, 1.0, 0.0))
        prefix = _dot(lower, tie.astype(BF16)) + tie_run
        take = jnp.where(gt, 1.0, jnp.where(prefix <= need, tie, 0.0))
        take = jnp.where(all_sel, 1.0, take)
        take = jnp.where(s_idx > -jnp.inf, take, 0.0)

        def mask(j, s):
            q0 = (j * LANES) % tq
            return jnp.where(take[:, q0:q0 + LANES] > 0.5, s, NEG)

        kk = kka_ref[rows(c), :]
        vt = vt_ref[0, :, rows(c)]
        for g in range(3):
            _attend_t(kk, qs_ref, g, vt, m_ref, l_ref, acc_ref, mask)
        return tie_run + jnp.sum(tie, axis=0, keepdims=True)

    lax.fori_loop(0, nk, att_body, jnp.zeros((1, tq), F32))

    for g in range(3):
        o_ref[:, g * LANES:(g + 1) * LANES] = _head_pair_out(acc_ref[g], l_ref[g], tq).astype(o_ref.dtype)


def _diff_kernel_t(qb_ref, kb_ref, vb_ref, lam_ref, gsub_ref, o_ref,
                   vt_ref, qs_ref, m_ref, l_ref, acc_ref, *, tq, lam_init, seq):
    i = pl.program_id(1)
    lane = lax.broadcasted_iota(jnp.int32, (tq, LANES), 1)
    zero_b = jnp.zeros((tq, LANES), BF16)

    @pl.when(i == 0)
    def _():
        _transpose_values(vb_ref, vt_ref, seq, tq)

    qb = qb_ref[...]
    for g in range(2):
        qg = qb[:, g * LANES:(g + 1) * LANES]
        for j in range(4):
            qs_ref[g, j * tq:(j + 1) * tq, :] = jnp.where(lane // B_QK_DIM == j, qg, zero_b)
    _init_stats(m_ref, l_ref, acc_ref)

    def rows(c):
        return pl.ds(pl.multiple_of(c * tq, tq), tq)

    def step(c, masked):
        mask = _causal_mask_t(tq, tq) if masked else None
        for g in range(2):
            _attend_t(kb_ref[rows(c), g * LANES:(g + 1) * LANES], qs_ref, g, vt_ref[g, :, rows(c)],
                      m_ref, l_ref, acc_ref, mask)

    def body(c, carry):
        step(c, False)
        return carry

    lax.fori_loop(0, i, body, 0)
    step(i, True)

    lv = lam_ref[...]
    lam = (jnp.exp(jnp.sum(lv[0:1] * lv[1:2], axis=1, keepdims=True))
           - jnp.exp(jnp.sum(lv[2:3] * lv[3:4], axis=1, keepdims=True)) + lam_init)
    gsub = gsub_ref[...]
    for g in range(2):
        acc_t, l = acc_ref[g], l_ref[g]

        def prob(r0, j):
            return acc_t[r0:r0 + HEAD_DIM, j * tq:(j + 1) * tq] / l[:, j * tq:(j + 1) * tq]

        halves = []
        for r0, j in ((0, 0), (HEAD_DIM, 2)):
            o = prob(r0, j) - lam * prob(r0, j + 1)
            ms = jnp.mean(o * o, axis=0, keepdims=True)
            halves.append(o * lax.rsqrt(ms + SUBLN_EPS))
        y = (jnp.concatenate(halves, axis=0).T * gsub) * (1.0 - lam_init)
        o_ref[:, g * LANES:(g + 1) * LANES] = y.astype(o_ref.dtype)


def _moba_kernel_t(qc_ref, kc_ref, vc_ref, o_ref, kmean_ref, vt_ref, qs_ref, bias_ref, m_ref, l_ref, acc_ref,
                   *, tq, nb, n_sel, seq):
    i = pl.program_id(1)
    nbp = kmean_ref.shape[0]
    lane = lax.broadcasted_iota(jnp.int32, (tq, LANES), 1)
    lo = lane < HEAD_DIM
    zero_b = jnp.zeros((tq, LANES), BF16)

    @pl.when(i == 0)
    def _():
        _transpose_values(vc_ref, vt_ref, seq, tq)
        kmean_ref[...] = jnp.zeros(kmean_ref.shape, F32)
        for n in range(nb):
            kblk = kc_ref[n * tq:(n + 1) * tq, :].astype(F32)
            kmean_ref[n:n + 1, :] = jnp.mean(kblk, axis=0, keepdims=True)

    qc = qc_ref[...]
    sub = lax.broadcasted_iota(jnp.int32, (nbp, 2 * tq), 0)
    past = sub < i
    for g in range(3):
        qg = qc[:, g * LANES:(g + 1) * LANES]
        q2 = jnp.concatenate([jnp.where(lo, qg, zero_b), jnp.where(lo, zero_b, qg)], axis=0)
        km = kmean_ref[:, g * LANES:(g + 1) * LANES]
        km_hi = km.astype(BF16)
        gt = _dot_nt(km_hi, q2) + _dot_nt((km - km_hi.astype(F32)).astype(BF16), q2)
        rows_ = []
        for n in range(nbp):
            gn = gt[n:n + 1, :]
            beats = jnp.where(sub < n, jnp.where(gt >= gn, 1.0, 0.0), jnp.where(gt > gn, 1.0, 0.0))
            beats = jnp.where(sub == n, 0.0, jnp.where(past, beats, 0.0))
            rank = jnp.sum(beats, axis=0, keepdims=True)
            rows_.append(jnp.where(rank < n_sel, 0.0, NEG))
        bias_ref[g] = jnp.where(past, jnp.concatenate(rows_, axis=0), NEG)
        qs_ref[g] = q2
    _init_stats(m_ref, l_ref, acc_ref)

    def rows(c):
        return pl.ds(pl.multiple_of(c * tq, tq), tq)

    def body(c, carry):
        for g in range(3):
            bias = bias_ref[g, pl.ds(c, 1), :]

            def mask(j, s):
                return s + bias[:, j * LANES:(j + 1) * LANES]

            _attend_t(kc_ref[rows(c), g * LANES:(g + 1) * LANES], qs_ref, g, vt_ref[g, :, rows(c)],
                      m_ref, l_ref, acc_ref, mask)
        return carry

    lax.fori_loop(0, i, body, 0)

    causal = _causal_mask_t(tq, tq)
    for g in range(3):
        _attend_t(kc_ref[rows(i), g * LANES:(g + 1) * LANES], qs_ref, g, vt_ref[g, :, rows(i)],
                  m_ref, l_ref, acc_ref, causal)
        o_ref[:, g * LANES:(g + 1) * LANES] = _head_pair_out(acc_ref[g], l_ref[g], tq).astype(o_ref.dtype)


def _route(h, wr_ref, cw_ref):
    hi = h.astype(BF16)
    lo = (h - hi.astype(F32)).astype(BF16)
    logits = _dot(hi, wr_ref[0]) + (_dot(lo, wr_ref[0]) + _dot(hi, wr_ref[1]))
    lane = lax.broadcasted_iota(jnp.int32, logits.shape, 1)
    lg = jnp.where(lane < N_EXPERTS, logits, -jnp.inf)
    v0 = jnp.max(lg, axis=1, keepdims=True)
    i0 = jnp.min(jnp.where(lg == v0, lane, LANES), axis=1, keepdims=True)
    lg1 = jnp.where(lane == i0, -jnp.inf, lg)
    v1 = jnp.max(lg1, axis=1, keepdims=True)
    i1 = jnp.min(jnp.where(lg1 == v1, lane, LANES), axis=1, keepdims=True)
    e1 = jnp.exp(v1 - v0)
    w0 = 1.0 / (1.0 + e1)
    cw_ref[...] = (jnp.where(lane == 0, i0.astype(F32), 0.0) + jnp.where(lane == 1, i1.astype(F32), 0.0)
                   + jnp.where(lane == 2, w0, 0.0) + jnp.where(lane == 3, e1 * w0, 0.0))


def _outproj_kernel(x_ref, ya_ref, yb_ref, yc_ref, wo_ref, g_ref, mod_ref, *rest, with_router):
    if with_router:
        wr_ref, x1_ref, h_ref, cw_ref = rest
    else:
        x1_ref, h_ref = rest
    ab = A_WIDTH + B_WIDTH
    y = (_dot(ya_ref[...], wo_ref[0:A_WIDTH, :]) + _dot(yb_ref[...], wo_ref[A_WIDTH:ab, :])
         + _dot(yc_ref[...], wo_ref[ab:ab + C_WIDTH, :]))
    x1 = x_ref[...] + mod_ref[2:3, :] * y
    x1_ref[...] = x1
    h = _norm_mod(x1, g_ref[...], mod_ref[3:4, :], mod_ref[4:5, :])
    if with_router:
        h_ref[...] = _pack_rows(h)
        _route(h, wr_ref, cw_ref)
    else:
        h_ref[...] = h.astype(BF16)


def outproj(x, ya, yb, yc, wo, layer, g_ffn, mod, seq, w_router=None, tm=512):
    t, d = x.shape
    tiles_per_batch = seq // tm
    row = lambda i: (i, 0)
    const = lambda i: (0, 0)
    with_router = w_router is not None
    in_specs = [pl.BlockSpec((tm, d), row), pl.BlockSpec((tm, A_WIDTH), row), pl.BlockSpec((tm, B_WIDTH), row),
                pl.BlockSpec((tm, C_WIDTH), row), pl.BlockSpec((None, d, d), lambda i: (layer, 0, 0)),
                pl.BlockSpec((1, d), const),
                pl.BlockSpec((None, 6, d), lambda i: (i // tiles_per_batch, 0, 0))]
    out_specs = [pl.BlockSpec((tm, d), row), pl.BlockSpec((tm, d), row)]
    out_shape = [jax.ShapeDtypeStruct((t, d), F32), jax.ShapeDtypeStruct((t, d), BF16)]
    args = [x, ya, yb, yc, wo, g_ffn, mod]
    if with_router:
        in_specs.append(pl.BlockSpec((2, d, LANES), lambda i: (0, 0, 0)))
        out_specs[1] = pl.BlockSpec((tm, d // 2), row)
        out_shape[1] = jax.ShapeDtypeStruct((t, d // 2), jnp.int32)
        out_specs.append(pl.BlockSpec((tm, LANES), row))
        out_shape.append(jax.ShapeDtypeStruct((t, LANES), F32))
        args.append(w_router)
    return pl.pallas_call(
        functools.partial(_outproj_kernel, with_router=with_router),
        grid=(t // tm,), in_specs=in_specs, out_specs=out_specs, out_shape=out_shape,
        compiler_params=_params("parallel"),
        name="outproj_router" if with_router else "outproj",
    )(*args)


def _swiglu_partial(h, wg_ref, wu_ref, wd_ref):
    a = _dot(h, wg_ref[...])
    u = _dot(h, wu_ref[...])
    act = (a * (1.0 / (1.0 + jnp.exp(-a)))) * u
    return _dot(act.astype(BF16), wd_ref[...])


def _ffn_kernel(x1_ref, h_ref, mod_ref, wg_ref, wu_ref, wd_ref, o_ref, acc_ref):
    f = pl.program_id(1)

    @pl.when(f == 0)
    def _():
        acc_ref[...] = jnp.zeros(acc_ref.shape, F32)

    acc_ref[...] += _swiglu_partial(h_ref[...], wg_ref, wu_ref, wd_ref)

    @pl.when(f == pl.num_programs(1) - 1)
    def _():
        o_ref[...] = x1_ref[...] + mod_ref[5:6, :] * acc_ref[...]


def ffn_dense(x1, h, mod, wg, wu, wd, seq, tm=512, tf=1408):
    t, d = x1.shape
    dff = wg.shape[1]
    tiles_per_batch = seq // tm
    row = lambda i, f: (i, 0)
    return pl.pallas_call(
        _ffn_kernel,
        grid=(t // tm, dff // tf),
        in_specs=[pl.BlockSpec((tm, d), row), pl.BlockSpec((tm, d), row),
                  pl.BlockSpec((None, 6, d), lambda i, f: (i // tiles_per_batch, 0, 0)),
                  pl.BlockSpec((d, tf), lambda i, f: (0, f)), pl.BlockSpec((d, tf), lambda i, f: (0, f)),
                  pl.BlockSpec((tf, d), lambda i, f: (f, 0))],
        out_specs=pl.BlockSpec((tm, d), row),
        out_shape=jax.ShapeDtypeStruct((t, d), F32),
        scratch_shapes=[pltpu.VMEM((tm, d), F32)],
        compiler_params=_params("parallel", "arbitrary"),
        name="ffn_dense",
    )(x1, h, mod, wg, wu, wd)


MOE_TILE = 512
MOE_PARTS = 3
SC_CORES, SC_SUBCORES = 2, 16
SC_ROWS = 64
HI16 = -65536


def _pack_rows(x):
    c = x.shape[1] // 2
    bits = lax.bitcast_convert_type(x.astype(jnp.bfloat16).astype(F32), jnp.int32)
    return lax.shift_right_logical(bits[:, :c], jnp.int32(16)) | (bits[:, c:] & jnp.int32(HI16))


def _unpack_rows(w):
    lo = lax.bitcast_convert_type(lax.shift_left(w, jnp.int32(16)), F32)
    hi = lax.bitcast_convert_type(w & jnp.int32(HI16), F32)
    return jnp.concatenate([lo, hi], axis=1)


def sc_gather_rows(table, idx):
    d = table.shape[1]
    b = idx.shape[0]
    per_worker = b // (SC_CORES * SC_SUBCORES)
    assert per_worker * SC_CORES * SC_SUBCORES == b and per_worker % (2 * SC_ROWS) == 0
    mesh = plsc.VectorSubcoreMesh(core_axis_name="c", subcore_axis_name="s")
    idx_buf = pltpu.VMEM((SC_ROWS,), jnp.int32)
    row_buf = pltpu.VMEM((SC_ROWS, d), table.dtype)

    @functools.partial(
        pl.kernel, mesh=mesh, out_type=jax.ShapeDtypeStruct((b, d), table.dtype),
        scratch_types=[idx_buf, idx_buf, row_buf, row_buf] + [pltpu.SemaphoreType.DMA] * 4,
        name="sc_gather_rows")
    def gather(table_hbm, idx_hbm, out_hbm, idx0, idx1, rows0, rows1, sem_g0, sem_g1, sem_w0, sem_w1):
        base = (lax.axis_index("s") * SC_CORES + lax.axis_index("c")) * per_worker

        @pl.loop(0, per_worker // (2 * SC_ROWS))
        def _(pair):
            off0 = pl.multiple_of(base + pair * (2 * SC_ROWS), SC_ROWS)
            off1 = pl.multiple_of(off0 + SC_ROWS, SC_ROWS)
            pltpu.sync_copy(idx_hbm.at[pl.ds(off0, SC_ROWS)], idx0)
            pltpu.sync_copy(idx_hbm.at[pl.ds(off1, SC_ROWS)], idx1)
            gather0 = pltpu.async_copy(table_hbm.at[idx0], rows0, sem_g0)
            gather1 = pltpu.async_copy(table_hbm.at[idx1], rows1, sem_g1)
            gather0.wait()
            write0 = pltpu.async_copy(rows0, out_hbm.at[pl.ds(off0, SC_ROWS)], sem_w0)
            gather1.wait()
            write1 = pltpu.async_copy(rows1, out_hbm.at[pl.ds(off1, SC_ROWS)], sem_w1)
            write0.wait()
            write1.wait()

    return gather(table, idx)


def _moe_layout(e0, e1, t, n_exp):
    experts = jnp.arange(n_exp, dtype=jnp.int32)
    routed = ((e0[:, None] == experts) | (e1[:, None] == experts)).astype(jnp.int32)
    csum = jnp.cumsum(routed, axis=0)
    padded = (csum[-1] + MOE_TILE - 1) // MOE_TILE * MOE_TILE
    ends = jnp.cumsum(padded)
    pos = (ends - padded)[None, :] + csum - routed
    pos0 = jnp.take_along_axis(pos, e0[:, None], axis=1)[:, 0]
    pos1 = jnp.take_along_axis(pos, e1[:, None], axis=1)[:, 0]
    n_rows = 2 * t + n_exp * MOE_TILE
    tok = jnp.arange(t, dtype=jnp.int32)
    src = jnp.zeros((n_rows,), jnp.int32).at[jnp.concatenate([pos0, pos1])].set(
        jnp.concatenate([tok, tok]), unique_indices=True, mode="promise_in_bounds")
    n_tiles = n_rows // MOE_TILE
    n_valid = ends[-1] // MOE_TILE
    tile_expert = jnp.sum(jnp.arange(n_tiles)[:, None] * MOE_TILE >= ends[None, :], axis=1)
    tile_expert = tile_expert[jnp.minimum(jnp.arange(n_tiles), n_valid - 1)].astype(jnp.int32)
    return pos0, pos1, src, tile_expert, n_valid.astype(jnp.int32).reshape(1)


W_ROWS_IN = 128
W_ROWS_DOWN = 352
FF_SPLIT = 2


def _experts_packed_kernel(te_ref, nv_ref, first_ref, x_ref, wg_hbm, wu_hbm, wd_hbm, *rest, layer):
    o_ref, wg_b, wu_b, wd_b, st_in, st_dn, sems = rest[-7:]
    j = pl.program_id(0)
    valid = j < nv_ref[0]

    @pl.when(valid & (first_ref[j] == 1))
    def _():
        e = te_ref[j]
        plan = []
        for w_hbm, w_b, rb, st, sem0 in ((wg_hbm, wg_b, W_ROWS_IN, st_in, 0), (wu_hbm, wu_b, W_ROWS_IN, st_in, 0),
                                         (wd_hbm, wd_b, W_ROWS_DOWN, st_dn, 2)):
            assert w_b.shape[0] % rb == 0
            plan += [(w_hbm, w_b, r0, rb, st, sem0) for r0 in range(0, w_b.shape[0], rb)]
        uses = {0: 0, 2: 0}
        copies = []
        for w_hbm, w_b, r0, rb, st, sem0 in plan:
            slot = uses[sem0] % 2
            uses[sem0] += 1
            copies.append((pltpu.make_async_copy(w_hbm.at[layer, e, pl.ds(r0, rb), :], st.at[slot],
                                                 sems.at[sem0 + slot]), st, slot, w_b, r0, rb))
        copies[0][0].start()
        for b, (copy, st, slot, w_b, r0, rb) in enumerate(copies):
            if b + 1 < len(copies):
                copies[b + 1][0].start()
            copy.wait()
            w_b[r0:r0 + rb, :] = st[slot].astype(BF16)

    @pl.when(valid)
    def _():
        x = _unpack_rows(x_ref[...]).astype(BF16)
        tf = wg_b.shape[1] // FF_SPLIT
        acc = None
        for f in range(FF_SPLIT):
            cols = slice(f * tf, (f + 1) * tf)
            a = _dot(x, wg_b[:, cols])
            u = _dot(x, wu_b[:, cols])
            act = (a * (1.0 / (1.0 + jnp.exp(-a)))) * u
            part = _dot(act.astype(BF16), wd_b[cols, :])
            acc = part if acc is None else acc + part
        o_ref[...] = _pack_rows(acc)


def moe_experts_packed(x_part, tile_expert, n_valid, w_gate, w_up, w_down, layer, y_prev, part, n_rows):
    rows_part, half = x_part.shape
    d = 2 * half
    dff = w_gate.shape[3]
    tiles_part = rows_part // MOE_TILE
    first = jnp.concatenate([jnp.ones((1,), jnp.int32),
                             (tile_expert[1:] != tile_expert[:-1]).astype(jnp.int32)])

    def tile(j, nv):
        return jnp.maximum(jnp.minimum(j, nv[0] - 1), 0)

    hbm = pl.BlockSpec(memory_space=pl.ANY)
    in_specs = [pl.BlockSpec((MOE_TILE, half), lambda j, te, nv, fi: (tile(j, nv), 0)), hbm, hbm, hbm]
    args = [tile_expert, n_valid, first, x_part, w_gate, w_up, w_down]
    aliases = {}
    if y_prev is not None:
        in_specs.append(hbm)
        args.append(y_prev)
        aliases = {len(args) - 1: 0}
    return pl.pallas_call(
        functools.partial(_experts_packed_kernel, layer=layer),
        grid_spec=pltpu.PrefetchScalarGridSpec(
            num_scalar_prefetch=3, grid=(tiles_part,), in_specs=in_specs,
            out_specs=pl.BlockSpec((MOE_TILE, half), lambda j, te, nv, fi: (part * tiles_part + tile(j, nv), 0)),
            scratch_shapes=[pltpu.VMEM((d, dff), BF16), pltpu.VMEM((d, dff), BF16), pltpu.VMEM((dff, d), BF16),
                            pltpu.VMEM((2, W_ROWS_IN, dff), F32), pltpu.VMEM((2, W_ROWS_DOWN, d), F32),
                            pltpu.SemaphoreType.DMA((4,))]),
        out_shape=jax.ShapeDtypeStruct((n_rows, half), jnp.int32),
        input_output_aliases=aliases,
        compiler_params=_params("arbitrary"),
        name="moe_experts",
    )(*args)


def _combine_packed_kernel(y0_ref, y1_ref, route_ref, x1_ref, mod_ref, o_ref):
    rt = route_ref[...]
    f = rt[:, 2:3] * _unpack_rows(y0_ref[...]) + rt[:, 3:4] * _unpack_rows(y1_ref[...])
    o_ref[...] = x1_ref[...] + mod_ref[5:6, :] * f


def moe_combine_packed(y_pairs, route, x1, mod, seq, tm=512):
    t, d = x1.shape
    nt = t // tm
    tiles_per_batch = seq // tm
    row = lambda i: (i, 0)
    return pl.pallas_call(
        _combine_packed_kernel,
        grid=(nt,),
        in_specs=[pl.BlockSpec((tm, d // 2), row), pl.BlockSpec((tm, d // 2), lambda i: (i + nt, 0)),
                  pl.BlockSpec((tm, LANES), row), pl.BlockSpec((tm, d), row),
                  pl.BlockSpec((None, 6, d), lambda i: (i // tiles_per_batch, 0, 0))],
        out_specs=pl.BlockSpec((tm, d), row),
        out_shape=jax.ShapeDtypeStruct((t, d), F32),
        compiler_params=_params("parallel"),
        name="moe_combine",
    )(y_pairs, y_pairs, route, x1, mod)


def ffn_moe_sc(x1, h_packed, route, mod, w_gate, w_up, w_down, layer, seq):
    t = x1.shape[0]
    e0, e1 = route[:, 0].astype(jnp.int32), route[:, 1].astype(jnp.int32)
    pos0, pos1, src, tile_expert, n_valid = _moe_layout(e0, e1, t, w_gate.shape[1])
    n_rows = src.shape[0]
    rows_part = n_rows // MOE_PARTS
    tiles_part = rows_part // MOE_TILE
    assert rows_part * MOE_PARTS == n_rows and tiles_part * MOE_TILE == rows_part
    h_parts = [sc_gather_rows(h_packed, src[p * rows_part:(p + 1) * rows_part]) for p in range(MOE_PARTS)]
    y_sorted = None
    for p in range(MOE_PARTS):
        n_valid_part = jnp.clip(n_valid - p * tiles_part, 0, tiles_part)
        y_sorted = moe_experts_packed(h_parts[p], tile_expert[p * tiles_part:(p + 1) * tiles_part], n_valid_part,
                                      w_gate, w_up, w_down, layer, y_sorted, p, n_rows)
    y_pairs = sc_gather_rows(y_sorted, jnp.concatenate([pos0, pos1]))
    return moe_combine_packed(y_pairs, route, x1, mod, seq)


def _final_norm_kernel(x_ref, g_ref, o_ref):
    x = x_ref[...]
    o_ref[...] = (x * lax.rsqrt(jnp.mean(x * x, axis=-1, keepdims=True) + EPS)) * g_ref[...]


def final_norm(x, g, tm=512):
    t, d = x.shape
    return pl.pallas_call(
        _final_norm_kernel,
        grid=(t // tm,),
        in_specs=[pl.BlockSpec((tm, d), lambda i: (i, 0)), pl.BlockSpec((1, d), lambda i: (0, 0))],
        out_specs=pl.BlockSpec((tm, d), lambda i: (i, 0)),
        out_shape=jax.ShapeDtypeStruct((t, d), F32),
        compiler_params=_params("parallel"),
        name="final_norm",
    )(x, g)


def _cast_kernel(w_ref, o_ref):
    o_ref[...] = w_ref[...].astype(o_ref.dtype)


def cast_layer_bf16(w, layer):
    _, e, a, b = w.shape
    ta = a // 2
    return pl.pallas_call(
        _cast_kernel,
        grid=(e, a // ta),
        in_specs=[pl.BlockSpec((None, None, ta, b), lambda i, r: (layer, i, r, 0))],
        out_specs=pl.BlockSpec((None, ta, b), lambda i, r: (i, r, 0)),
        out_shape=jax.ShapeDtypeStruct((e, a, b), BF16),
        compiler_params=_params("parallel", "parallel"),
        name="cast_bf16",
    )(w)


def _rope_tables(positions, dim):
    rot = dim // ROPE_FRACTION
    half = rot // 2
    inv = 1.0 / (ROPE_THETA ** (np.arange(0, rot, 2, dtype=np.float32) / rot))
    ang = positions.reshape(-1).astype(F32)[:, None] * jnp.asarray(inv, F32)
    cos, sin = jnp.cos(ang), jnp.sin(ang)
    t = ang.shape[0]
    ones = jnp.ones((t, dim - rot), F32)
    zeros = lambda w: jnp.zeros((t, w), F32)
    reps = LANES // dim
    c = jnp.tile(jnp.concatenate([cos, cos, ones], axis=1), (1, reps))
    sa = jnp.tile(jnp.concatenate([-sin, zeros(dim - half)], axis=1), (1, reps))
    sb = jnp.tile(jnp.concatenate([zeros(half), sin, zeros(dim - rot)], axis=1), (1, reps))
    return c, sa, sb


def _relayout_w_in(w):
    pts = np.cumsum([0, 384, 64, 64, 256, 64, 4, 256, 256, 256, 384, 384, 384])
    (q_a, k_a, v_a, q_i, k_i, w_i, q_b, k_b, v_b, q_c, k_c, v_c) = [w[..., pts[j]:pts[j + 1]] for j in range(12)]
    w_i_pad = jnp.concatenate([w_i, jnp.zeros(w.shape[:-1] + (LANES - IDX_HEADS,), w.dtype)], axis=-1)
    return jnp.concatenate([q_a, k_a, k_a, v_a, v_a, q_i, k_i, k_i, w_i_pad,
                            q_b, k_b, v_b, q_c, k_c, v_c], axis=-1).astype(BF16)


def kernel(x, c, positions, w_in, w_out, diff_lambda, diff_subln, w_ada, b_ada, g_attn, g_ffn, w_ff_gate,
           w_ff_up, w_ff_down, w_router, w_exp_gate, w_exp_up, w_exp_down, g_final):
    batch, seq, d = x.shape
    depth = w_in.shape[0]
    t = batch * seq
    tabs = _rope_tables(positions, HEAD_DIM) + _rope_tables(positions, B_QK_DIM)
    mod_all = adaln_mod(c, w_ada, b_ada).reshape(depth, batch, 6, d)
    w_in_pad = _relayout_w_in(w_in)
    wo = w_out.astype(BF16)
    xf = x.reshape(t, d)
    for layer in range(depth):
        mod = mod_all[layer]
        lam_init = 0.8 - 0.6 * math.exp(-0.3 * layer)
        (qa, kka, vva, qi, kki, wi, qb, kb, vb, qc, kc, vc) = inproj(
            xf, g_attn[layer].reshape(1, d), mod, tabs, w_in_pad, layer, seq)
        ya = dsa_attention(qa, qi, wi, kka, vva, kki, batch, seq)
        g_sub2 = jnp.tile(diff_subln[layer], 2).reshape(1, LANES)
        yb = diff_attention(qb, kb, vb, diff_lambda[layer], g_sub2, lam_init, batch, seq)
        yc = moba_attention(qc, kc, vc, batch, seq)
        j = layer // 2
        gf = g_ffn[layer].reshape(1, d)
        if layer % 2 == 0:
            x1, h = outproj(xf, ya, yb, yc, wo, layer, gf, mod, seq)
            wg, wu, wd = (cast_layer_bf16(w[:, None], j)[0] for w in (w_ff_gate, w_ff_up, w_ff_down))
            xf = ffn_dense(x1, h, mod, wg, wu, wd, seq)
        else:
            wr = jnp.concatenate([w_router[j], jnp.zeros((d, LANES - N_EXPERTS), F32)], axis=1)
            wr_hi = wr.astype(BF16)
            wr = jnp.stack([wr_hi, (wr - wr_hi.astype(F32)).astype(BF16)])
            x1, h_packed, route = outproj(xf, ya, yb, yc, wo, layer, gf, mod, seq, w_router=wr)
            xf = ffn_moe_sc(x1, h_packed, route, mod, w_exp_gate, w_exp_up, w_exp_down, j, seq)
    return final_norm(xf, g_final.reshape(1, d)).reshape(batch, seq, d)
```

```python
import functools
import math

import jax
import jax.numpy as jnp
import numpy as np
from jax import lax
from jax.experimental import pallas as pl
from jax.experimental.pallas import tpu as pltpu
from jax.experimental.pallas import tpu_sc as plsc

F32 = jnp.float32
BF16 = jnp.bfloat16

HEAD_DIM = 64
A_HEADS = 6
IDX_HEADS = 4
B_HEADS = 4
B_QK_DIM = 32
C_HEADS = 6
A_WIDTH, IDX_WIDTH = A_HEADS * HEAD_DIM, IDX_HEADS * HEAD_DIM
B_WIDTH, C_WIDTH = B_HEADS * HEAD_DIM, C_HEADS * HEAD_DIM
DSA_TOPK_MAX = 256
MOBA_BLOCK = 256
MOBA_TOPK = 3
ROPE_THETA = 500000.0
ROPE_FRACTION = 4
SUBLN_EPS = 1e-5
EPS = 1e-6
N_EXPERTS = 8

LANES = 128
NEG = -1e30
INT_MIN = -2 ** 31
MIN_NORMAL_KEY = 0x00800000
MIN_NORMAL_F32 = float(np.float32(2.0 ** -126))
VMEM_LIMIT = 48 * 1024 * 1024
LOG2E = math.log2(math.e)

_G_QA, _G_KKA, _G_VVA, _G_QI, _G_KKI, _G_WI = (0, 384), (384, 512), (512, 640), (640, 896), (896, 1024), (1024, 1152)
_G_QB, _G_KB, _G_VB = (1152, 1408), (1408, 1664), (1664, 1920)
_G_QC, _G_KC, _G_VC = (1920, 2304), (2304, 2688), (2688, 3072)
D_IN_PAD = 3072


def _params(*sem):
    return pltpu.CompilerParams(dimension_semantics=sem, vmem_limit_bytes=VMEM_LIMIT)


def _dot(a, b):
    return jnp.dot(a, b, preferred_element_type=F32)


def _dot_nt(a, b):
    return lax.dot_general(a, b, (((1,), (1,)), ((), ())), preferred_element_type=F32)


def _adaln_kernel(c_ref, w_ref, b_ref, o_ref):
    c = c_ref[...]
    c_act = c * (1.0 / (1.0 + jnp.exp(-c)))
    o_ref[...] = jnp.dot(c_act, w_ref[...], preferred_element_type=F32,
                         precision=lax.Precision.HIGHEST) + b_ref[...]


def adaln_mod(c, w_ada, b_ada, tn=1536):
    depth, d, n = w_ada.shape
    b = c.shape[0]
    return pl.pallas_call(
        _adaln_kernel,
        grid=(depth, n // tn),
        in_specs=[pl.BlockSpec((b, d), lambda l, j: (0, 0)),
                  pl.BlockSpec((None, d, tn), lambda l, j: (l, 0, j)),
                  pl.BlockSpec((None, 1, tn), lambda l, j: (l, 0, j))],
        out_specs=pl.BlockSpec((None, b, tn), lambda l, j: (l, 0, j)),
        out_shape=jax.ShapeDtypeStruct((depth, b, n), F32),
        compiler_params=_params("parallel", "parallel"),
        name="adaln_mod",
    )(c, w_ada, b_ada.reshape(depth, 1, n))


def _norm_mod(x, g, shift, scale, eps=EPS):
    y = x * lax.rsqrt(jnp.mean(x * x, axis=-1, keepdims=True) + eps)
    return (y * g) * (1.0 + scale) + shift


def _rope_store(acc, o_ref, cos, sa, sb, half):
    for j in range(acc.shape[1] // LANES):
        a = acc[:, j * LANES:(j + 1) * LANES]
        r = a * cos + pltpu.roll(a, half, 1) * sb + pltpu.roll(a, LANES - half, 1) * sa
        o_ref[:, j * LANES:(j + 1) * LANES] = r.astype(o_ref.dtype)


def _inproj_kernel(x_ref, g_ref, mod_ref, c64_ref, sa64_ref, sb64_ref, c32_ref, sa32_ref, sb32_ref, w_ref,
                   qa_ref, kka_ref, vva_ref, qi_ref, kki_ref, wi_ref,
                   qb_ref, kb_ref, vb_ref, qc_ref, kc_ref, vc_ref):
    h = _norm_mod(x_ref[...], g_ref[...], mod_ref[0:1, :], mod_ref[1:2, :]).astype(BF16)
    c64, sa64, sb64 = c64_ref[...], sa64_ref[...], sb64_ref[...]
    c32, sa32, sb32 = c32_ref[...], sa32_ref[...], sb32_ref[...]

    def proj(cols):
        return _dot(h, w_ref[:, cols[0]:cols[1]])

    qk_scale = HEAD_DIM ** -0.5 * LOG2E
    _rope_store(proj(_G_QA), qa_ref, c64 * qk_scale, sa64 * qk_scale, sb64 * qk_scale, 8)
    _rope_store(proj(_G_KKA), kka_ref, c64, sa64, sb64, 8)
    vva_ref[...] = proj(_G_VVA).astype(vva_ref.dtype)
    _rope_store(proj(_G_QI), qi_ref, c64, sa64, sb64, 8)
    _rope_store(proj(_G_KKI), kki_ref, c64, sa64, sb64, 8)
    wi_ref[...] = proj(_G_WI) * (IDX_HEADS ** -0.5 * HEAD_DIM ** -0.5)
    b_scale = B_QK_DIM ** -0.5 * LOG2E
    _rope_store(proj(_G_QB), qb_ref, c32 * b_scale, sa32 * b_scale, sb32 * b_scale, 4)
    _rope_store(proj(_G_KB), kb_ref, c32, sa32, sb32, 4)
    vb_ref[...] = proj(_G_VB).astype(vb_ref.dtype)
    _rope_store(proj(_G_QC), qc_ref, c64 * qk_scale, sa64 * qk_scale, sb64 * qk_scale, 8)
    _rope_store(proj(_G_KC), kc_ref, c64, sa64, sb64, 8)
    vc_ref[...] = proj(_G_VC).astype(vc_ref.dtype)


def inproj(x, g, mod, tabs, w_pad, layer, seq, tm=512):
    t, d = x.shape
    tiles_per_batch = seq // tm
    row = lambda i: (i, 0)
    widths = [A_WIDTH, LANES, LANES, IDX_WIDTH, LANES, LANES, B_WIDTH, B_WIDTH, B_WIDTH, C_WIDTH, C_WIDTH, C_WIDTH]
    dtypes = [BF16, BF16, BF16, BF16, BF16, F32, BF16, BF16, BF16, BF16, BF16, BF16]
    return pl.pallas_call(
        _inproj_kernel,
        grid=(t // tm,),
        in_specs=[pl.BlockSpec((tm, d), row),
                  pl.BlockSpec((1, d), lambda i: (0, 0)),
                  pl.BlockSpec((None, 6, d), lambda i: (i // tiles_per_batch, 0, 0))]
                 + [pl.BlockSpec((tm, LANES), row)] * 6
                 + [pl.BlockSpec((None, d, D_IN_PAD), lambda i: (layer, 0, 0))],
        out_specs=[pl.BlockSpec((tm, w), row) for w in widths],
        out_shape=[jax.ShapeDtypeStruct((t, w), dt) for w, dt in zip(widths, dtypes)],
        compiler_params=_params("parallel"),
        name="inproj",
    )(x, g, mod, *tabs, w_pad)


def _init_stats(m_ref, l_ref, acc_ref):
    m_ref[...] = jnp.full(m_ref.shape, -jnp.inf, F32)
    l_ref[...] = jnp.zeros(l_ref.shape, F32)
    acc_ref[...] = jnp.zeros(acc_ref.shape, F32)


def _key_to_f32(k):
    return lax.bitcast_convert_type(jnp.where(k >= 0, k, k ^ 0x7FFFFFFF), F32)


def dsa_attention(qa, qi, wi, kka, vva, kki, batch, seq, tq=256):
    t = qa.shape[0]
    nq = seq // tq
    k_top = min(DSA_TOPK_MAX, seq // 4)
    qrow = lambda b, i: (b * nq + i, 0)
    full = lambda b, i: (b, 0)
    return pl.pallas_call(
        functools.partial(_dsa_kernel_t, tq=tq, k_top=k_top, seq=seq),
        grid=(batch, nq),
        in_specs=[pl.BlockSpec((tq, A_WIDTH), qrow), pl.BlockSpec((tq, IDX_WIDTH), qrow),
                  pl.BlockSpec((tq, LANES), qrow),
                  pl.BlockSpec((seq, LANES), full), pl.BlockSpec((seq, LANES), full),
                  pl.BlockSpec((seq, LANES), full)],
        out_specs=pl.BlockSpec((tq, A_WIDTH), qrow),
        out_shape=jax.ShapeDtypeStruct((t, A_WIDTH), BF16),
        scratch_shapes=[pltpu.VMEM((seq, tq), F32), pltpu.VMEM((seq, tq), jnp.int16),
                        pltpu.VMEM((seq, tq), jnp.int16), pltpu.VMEM((1, LANES, seq), BF16),
                        pltpu.VMEM((3, 2 * tq, LANES), BF16),
                        pltpu.VMEM((3, 1, 2 * tq), F32), pltpu.VMEM((3, 1, 2 * tq), F32),
                        pltpu.VMEM((3, LANES, 2 * tq), F32)],
        compiler_params=_params("parallel", "arbitrary"),
        name="dsa_attention",
    )(qa, qi, wi, kka, vva, kki)


def diff_attention(qb, kb, vb, lam_vec, g_sub2, lam_init, batch, seq, tq=256):
    t = qb.shape[0]
    nq = seq // tq
    qrow = lambda b, i: (b * nq + i, 0)
    full = lambda b, i: (b, 0)
    const = lambda b, i: (0, 0)
    return pl.pallas_call(
        functools.partial(_diff_kernel_t, tq=tq, lam_init=lam_init, seq=seq),
        grid=(batch, nq),
        in_specs=[pl.BlockSpec((tq, B_WIDTH), qrow), pl.BlockSpec((seq, B_WIDTH), full),
                  pl.BlockSpec((seq, B_WIDTH), full),
                  pl.BlockSpec((4, B_QK_DIM), const), pl.BlockSpec((1, LANES), const)],
        out_specs=pl.BlockSpec((tq, B_WIDTH), qrow),
        out_shape=jax.ShapeDtypeStruct((t, B_WIDTH), BF16),
        scratch_shapes=[pltpu.VMEM((2, LANES, seq), BF16), pltpu.VMEM((2, 4 * tq, LANES), BF16),
                        pltpu.VMEM((2, 1, 4 * tq), F32), pltpu.VMEM((2, 1, 4 * tq), F32),
                        pltpu.VMEM((2, LANES, 4 * tq), F32)],
        compiler_params=_params("parallel", "arbitrary"),
        name="diff_attention",
    )(qb, kb, vb, lam_vec, g_sub2)


def moba_attention(qc, kc, vc, batch, seq):
    tq = MOBA_BLOCK
    t = qc.shape[0]
    nb = seq // tq
    n_sel = min(MOBA_TOPK, nb - 1)
    nbp = 8
    assert seq % tq == 0 and nb <= nbp
    qrow = lambda b, i: (b * nb + i, 0)
    full = lambda b, i: (b, 0)
    return pl.pallas_call(
        functools.partial(_moba_kernel_t, tq=tq, nb=nb, n_sel=n_sel, seq=seq),
        grid=(batch, nb),
        in_specs=[pl.BlockSpec((tq, C_WIDTH), qrow), pl.BlockSpec((seq, C_WIDTH), full),
                  pl.BlockSpec((seq, C_WIDTH), full)],
        out_specs=pl.BlockSpec((tq, C_WIDTH), qrow),
        out_shape=jax.ShapeDtypeStruct((t, C_WIDTH), BF16),
        scratch_shapes=[pltpu.VMEM((nbp, C_WIDTH), F32), pltpu.VMEM((3, LANES, seq), BF16),
                        pltpu.VMEM((3, 2 * tq, LANES), BF16), pltpu.VMEM((3, nbp, 2 * tq), F32),
                        pltpu.VMEM((3, 1, 2 * tq), F32), pltpu.VMEM((3, 1, 2 * tq), F32),
                        pltpu.VMEM((3, LANES, 2 * tq), F32)],
        compiler_params=_params("parallel", "arbitrary"),
        name="moba_attention",
    )(qc, kc, vc)


def _attend_t(k, q_ref, g, v_t, m_ref, l_ref, acc_ref, mask=None):
    for j in range(q_ref.shape[1] // LANES):
        cols = slice(j * LANES, (j + 1) * LANES)
        s = _dot_nt(k, q_ref[g, cols, :])
        if mask is not None:
            s = mask(j, s)
        m_prev = m_ref[g, :, cols]
        m_new = jnp.maximum(m_prev, jnp.max(s, axis=0, keepdims=True))
        alpha = jnp.exp2(m_prev - m_new)
        p = jnp.exp2(s - m_new)
        l_ref[g, :, cols] = alpha * l_ref[g, :, cols] + jnp.sum(p, axis=0, keepdims=True)
        acc_ref[g, :, cols] = alpha * acc_ref[g, :, cols] + _dot(v_t, p.astype(BF16))
        m_ref[g, :, cols] = m_new


def _causal_mask_t(tk, tq):
    key = lax.broadcasted_iota(jnp.int32, (tk, LANES), 0)
    qry = lax.broadcasted_iota(jnp.int32, (tk, LANES), 1)

    def mask(j, s):
        return jnp.where(key <= qry + (j * LANES) % tq, s, NEG)
    return mask


def _transpose_values(v_ref, vt_ref, seq, tk):
    for g in range(vt_ref.shape[0]):
        for n in range(seq // tk):
            blk = v_ref[n * tk:(n + 1) * tk, g * LANES:(g + 1) * LANES].astype(F32)
            vt_ref[g, :, n * tk:(n + 1) * tk] = blk.T.astype(vt_ref.dtype)


def _head_pair_out(acc_t, l, tq):
    even = acc_t[0:HEAD_DIM, 0:tq] / l[:, 0:tq]
    odd = acc_t[HEAD_DIM:LANES, tq:2 * tq] / l[:, tq:2 * tq]
    return jnp.concatenate([even, odd], axis=0).T


def _dsa_kernel_t(qa_ref, qi_ref, wiq_ref, kka_ref, vva_ref, kki_ref, o_ref,
                  sc_ref, hi_ref, lo_ref, vt_ref, qs_ref, m_ref, l_ref, acc_ref, *, tq, k_top, seq):
    i = pl.program_id(1)
    nk = i + 1
    t0 = i * tq
    lo = lax.broadcasted_iota(jnp.int32, (tq, LANES), 1) < HEAD_DIM
    zero_b = jnp.zeros((tq, LANES), BF16)

    @pl.when(i == 0)
    def _():
        _transpose_values(vva_ref, vt_ref, seq, tq)

    def stack_heads(q):
        out = []
        for g in range(q.shape[1] // LANES):
            qg = q[:, g * LANES:(g + 1) * LANES]
            out += [jnp.where(lo, qg, zero_b), jnp.where(lo, zero_b, qg)]
        return out

    qa_stack = stack_heads(qa_ref[...])
    for g in range(3):
        qs_ref[g, 0:tq, :] = qa_stack[2 * g]
        qs_ref[g, tq:2 * tq, :] = qa_stack[2 * g + 1]
    qi_stack = jnp.concatenate(stack_heads(qi_ref[...]), axis=0)
    wi_t = wiq_ref[...].T

    def rows(c):
        return pl.ds(pl.multiple_of(c * tq, tq), tq)

    key_pos = lax.broadcasted_iota(jnp.int32, (tq, tq), 0)
    qry_pos = lax.broadcasted_iota(jnp.int32, (tq, tq), 1)

    def idx_body(c, carry):
        r = jnp.maximum(_dot_nt(kki_ref[rows(c), :], qi_stack), 0.0)
        s = wi_t[0:1, :] * r[:, 0:tq]
        for h in range(1, IDX_HEADS):
            s = s + wi_t[h:h + 1, :] * r[:, h * tq:(h + 1) * tq]
        causal = (c * tq + key_pos) <= (t0 + qry_pos)
        s = jnp.where(causal, s, -jnp.inf)
        s = jnp.where(jnp.abs(s) < MIN_NORMAL_F32, 0.0, s)
        sc_ref[rows(c), :] = s
        bits = lax.bitcast_convert_type(s, jnp.int32)
        key = jnp.where(bits >= 0, bits, bits ^ 0x7FFFFFFF)
        hi_ref[rows(c), :] = lax.shift_right_arithmetic(key, jnp.int32(16)).astype(jnp.int16)
        lo_ref[rows(c), :] = ((key & 0xFFFF) - 32768).astype(jnp.int16)
        return carry

    lax.fori_loop(0, nk, idx_body, 0)

    def count16(ref, cand, strict):
        c16 = cand.astype(jnp.int16)

        def body(c, acc):
            x = ref[rows(c), :]
            hit = jnp.where((x > c16) if strict else (x >= c16), jnp.int16(1), jnp.int16(0))
            for r in range(tq // 16):
                acc = acc + hit[r * 16:(r + 1) * 16]
            return acc
        acc = lax.fori_loop(0, nk, body, jnp.zeros((16, tq), jnp.int16))
        return jnp.sum(acc.astype(F32), axis=0, keepdims=True)

    def search16(ref, k_need):
        v0 = jnp.where(count16(ref, jnp.zeros((1, tq), jnp.int32), False) >= k_need, 0, -32768).astype(jnp.int32)

        def bisect(b, v):
            trial = v | lax.shift_left(jnp.int32(1), 14 - b)
            return jnp.where(count16(ref, trial, False) >= k_need, trial, v)
        return lax.fori_loop(0, 15, bisect, v0)

    kf = float(k_top)

    def search():
        hi_k = search16(hi_ref, kf)
        above = count16(hi_ref, hi_k, True)
        hi16 = hi_k.astype(jnp.int16)

        def keep_bucket(c, carry):
            lo_ref[rows(c), :] = jnp.where(hi_ref[rows(c), :] == hi16, lo_ref[rows(c), :], jnp.int16(-32768))
            return carry
        lax.fori_loop(0, nk, keep_bucket, 0)
        lo_k = search16(lo_ref, kf - above)
        n_gt = above + count16(lo_ref, lo_k, True)
        return lax.shift_left(hi_k, jnp.int32(16)) | (lo_k + 32768), kf - n_gt

    key, need = lax.cond(t0 + tq <= k_top,
                         lambda: (jnp.full((1, tq), INT_MIN, jnp.int32), jnp.full((1, tq), kf, F32)), search)
    thr = _key_to_f32(key)
    thr_next = _key_to_f32(jnp.where(key == 0, MIN_NORMAL_KEY, key + 1))
    all_sel = (t0 + lax.broadcasted_iota(jnp.int32, (1, tq), 1)) < k_top
    lower = (qry_pos <= key_pos).astype(BF16)

    _init_stats(m_ref, l_ref, acc_ref)

    def att_body(c, tie_run):
        s_idx = sc_ref[rows(c), :]
        ge = s_idx >= thr
        gt = s_idx >= thr_next
        tie = jnp.where(gt, 0.0, jnp.where(ge, 1.0, 0.0))
        prefix = _dot(lower, tie.astype(BF16)) + tie_run
        take = jnp.where(gt, 1.0, jnp.where(prefix <= need, tie, 0.0))
        take = jnp.where(all_sel, 1.0, take)
        take = jnp.where(s_idx > -jnp.inf, take, 0.0)

        def mask(j, s):
            q0 = (j * LANES) % tq
            return jnp.where(take[:, q0:q0 + LANES] > 0.5, s, NEG)

        kk = kka_ref[rows(c), :]
        vt = vt_ref[0, :, rows(c)]
        for g in range(3):
            _attend_t(kk, qs_ref, g, vt, m_ref, l_ref, acc_ref, mask)
        return tie_run + jnp.sum(tie, axis=0, keepdims=True)

    lax.fori_loop(0, nk, att_body, jnp.zeros((1, tq), F32))

    for g in range(3):
        o_ref[:, g * LANES:(g + 1) * LANES] = _head_pair_out(acc_ref[g], l_ref[g], tq).astype(o_ref.dtype)


def _diff_kernel_t(qb_ref, kb_ref, vb_ref, lam_ref, gsub_ref, o_ref,
                   vt_ref, qs_ref, m_ref, l_ref, acc_ref, *, tq, lam_init, seq):
    i = pl.program_id(1)
    lane = lax.broadcasted_iota(jnp.int32, (tq, LANES), 1)
    zero_b = jnp.zeros((tq, LANES), BF16)

    @pl.when(i == 0)
    def _():
        _transpose_values(vb_ref, vt_ref, seq, tq)

    qb = qb_ref[...]
    for g in range(2):
        qg = qb[:, g * LANES:(g + 1) * LANES]
        for j in range(4):
            qs_ref[g, j * tq:(j + 1) * tq, :] = jnp.where(lane // B_QK_DIM == j, qg, zero_b)
    _init_stats(m_ref, l_ref, acc_ref)

    def rows(c):
        return pl.ds(pl.multiple_of(c * tq, tq), tq)

    def step(c, masked):
        mask = _causal_mask_t(tq, tq) if masked else None
        for g in range(2):
            _attend_t(kb_ref[rows(c), g * LANES:(g + 1) * LANES], qs_ref, g, vt_ref[g, :, rows(c)],
                      m_ref, l_ref, acc_ref, mask)

    def body(c, carry):
        step(c, False)
        return carry

    lax.fori_loop(0, i, body, 0)
    step(i, True)

    lv = lam_ref[...]
    lam = (jnp.exp(jnp.sum(lv[0:1] * lv[1:2], axis=1, keepdims=True))
           - jnp.exp(jnp.sum(lv[2:3] * lv[3:4], axis=1, keepdims=True)) + lam_init)
    gsub = gsub_ref[...]
    for g in range(2):
        acc_t, l = acc_ref[g], l_ref[g]

        def prob(r0, j):
            return acc_t[r0:r0 + HEAD_DIM, j * tq:(j + 1) * tq] / l[:, j * tq:(j + 1) * tq]

        halves = []
        for r0, j in ((0, 0), (HEAD_DIM, 2)):
            o = prob(r0, j) - lam * prob(r0, j + 1)
            ms = jnp.mean(o * o, axis=0, keepdims=True)
            halves.append(o * lax.rsqrt(ms + SUBLN_EPS))
        y = (jnp.concatenate(halves, axis=0).T * gsub) * (1.0 - lam_init)
        o_ref[:, g * LANES:(g + 1) * LANES] = y.astype(o_ref.dtype)


def _moba_kernel_t(qc_ref, kc_ref, vc_ref, o_ref, kmean_ref, vt_ref, qs_ref, bias_ref, m_ref, l_ref, acc_ref,
                   *, tq, nb, n_sel, seq):
    i = pl.program_id(1)
    nbp = kmean_ref.shape[0]
    lane = lax.broadcasted_iota(jnp.int32, (tq, LANES), 1)
    lo = lane < HEAD_DIM
    zero_b = jnp.zeros((tq, LANES), BF16)

    @pl.when(i == 0)
    def _():
        _transpose_values(vc_ref, vt_ref, seq, tq)
        kmean_ref[...] = jnp.zeros(kmean_ref.shape, F32)
        for n in range(nb):
            kblk = kc_ref[n * tq:(n + 1) * tq, :].astype(F32)
            kmean_ref[n:n + 1, :] = jnp.mean(kblk, axis=0, keepdims=True)

    qc = qc_ref[...]
    sub = lax.broadcasted_iota(jnp.int32, (nbp, 2 * tq), 0)
    past = sub < i
    for g in range(3):
        qg = qc[:, g * LANES:(g + 1) * LANES]
        q2 = jnp.concatenate([jnp.where(lo, qg, zero_b), jnp.where(lo, zero_b, qg)], axis=0)
        km = kmean_ref[:, g * LANES:(g + 1) * LANES]
        km_hi = km.astype(BF16)
        gt = _dot_nt(km_hi, q2) + _dot_nt((km - km_hi.astype(F32)).astype(BF16), q2)
        rows_ = []
        for n in range(nbp):
            gn = gt[n:n + 1, :]
            beats = jnp.where(sub < n, jnp.where(gt >= gn, 1.0, 0.0), jnp.where(gt > gn, 1.0, 0.0))
            beats = jnp.where(sub == n, 0.0, jnp.where(past, beats, 0.0))
            rank = jnp.sum(beats, axis=0, keepdims=True)
            rows_.append(jnp.where(rank < n_sel, 0.0, NEG))
        bias_ref[g] = jnp.where(past, jnp.concatenate(rows_, axis=0), NEG)
        qs_ref[g] = q2
    _init_stats(m_ref, l_ref, acc_ref)

    def rows(c):
        return pl.ds(pl.multiple_of(c * tq, tq), tq)

    def body(c, carry):
        for g in range(3):
            bias = bias_ref[g, pl.ds(c, 1), :]

            def mask(j, s):
                return s + bias[:, j * LANES:(j + 1) * LANES]

            _attend_t(kc_ref[rows(c), g * LANES:(g + 1) * LANES], qs_ref, g, vt_ref[g, :, rows(c)],
                      m_ref, l_ref, acc_ref, mask)
        return carry

    lax.fori_loop(0, i, body, 0)

    causal = _causal_mask_t(tq, tq)
    for g in range(3):
        _attend_t(kc_ref[rows(i), g * LANES:(g + 1) * LANES], qs_ref, g, vt_ref[g, :, rows(i)],
                  m_ref, l_ref, acc_ref, causal)
        o_ref[:, g * LANES:(g + 1) * LANES] = _head_pair_out(acc_ref[g], l_ref[g], tq).astype(o_ref.dtype)


def _route(h, wr_ref, cw_ref):
    hi = h.astype(BF16)
    lo = (h - hi.astype(F32)).astype(BF16)
    logits = _dot(hi, wr_ref[0]) + (_dot(lo, wr_ref[0]) + _dot(hi, wr_ref[1]))
    lane = lax.broadcasted_iota(jnp.int32, logits.shape, 1)
    lg = jnp.where(lane < N_EXPERTS, logits, -jnp.inf)
    v0 = jnp.max(lg, axis=1, keepdims=True)
    i0 = jnp.min(jnp.where(lg == v0, lane, LANES), axis=1, keepdims=True)
    lg1 = jnp.where(lane == i0, -jnp.inf, lg)
    v1 = jnp.max(lg1, axis=1, keepdims=True)
    i1 = jnp.min(jnp.where(lg1 == v1, lane, LANES), axis=1, keepdims=True)
    e1 = jnp.exp(v1 - v0)
    w0 = 1.0 / (1.0 + e1)
    cw_ref[...] = (jnp.where(lane == 0, i0.astype(F32), 0.0) + jnp.where(lane == 1, i1.astype(F32), 0.0)
                   + jnp.where(lane == 2, w0, 0.0) + jnp.where(lane == 3, e1 * w0, 0.0))


def _outproj_kernel(x_ref, ya_ref, yb_ref, yc_ref, wo_ref, g_ref, mod_ref, *rest, with_router):
    if with_router:
        wr_ref, x1_ref, h_ref, cw_ref = rest
    else:
        x1_ref, h_ref = rest
    ab = A_WIDTH + B_WIDTH
    y = (_dot(ya_ref[...], wo_ref[0:A_WIDTH, :]) + _dot(yb_ref[...], wo_ref[A_WIDTH:ab, :])
         + _dot(yc_ref[...], wo_ref[ab:ab + C_WIDTH, :]))
    x1 = x_ref[...] + mod_ref[2:3, :] * y
    x1_ref[...] = x1
    h = _norm_mod(x1, g_ref[...], mod_ref[3:4, :], mod_ref[4:5, :])
    if with_router:
        h_ref[...] = _pack_rows(h)
        _route(h, wr_ref, cw_ref)
    else:
        h_ref[...] = h.astype(BF16)


def outproj(x, ya, yb, yc, wo, layer, g_ffn, mod, seq, w_router=None, tm=512):
    t, d = x.shape
    tiles_per_batch = seq // tm
    row = lambda i: (i, 0)
    const = lambda i: (0, 0)
    with_router = w_router is not None
    in_specs = [pl.BlockSpec((tm, d), row), pl.BlockSpec((tm, A_WIDTH), row), pl.BlockSpec((tm, B_WIDTH), row),
                pl.BlockSpec((tm, C_WIDTH), row), pl.BlockSpec((None, d, d), lambda i: (layer, 0, 0)),
                pl.BlockSpec((1, d), const),
                pl.BlockSpec((None, 6, d), lambda i: (i // tiles_per_batch, 0, 0))]
    out_specs = [pl.BlockSpec((tm, d), row), pl.BlockSpec((tm, d), row)]
    out_shape = [jax.ShapeDtypeStruct((t, d), F32), jax.ShapeDtypeStruct((t, d), BF16)]
    args = [x, ya, yb, yc, wo, g_ffn, mod]
    if with_router:
        in_specs.append(pl.BlockSpec((2, d, LANES), lambda i: (0, 0, 0)))
        out_specs[1] = pl.BlockSpec((tm, d // 2), row)
        out_shape[1] = jax.ShapeDtypeStruct((t, d // 2), jnp.int32)
        out_specs.append(pl.BlockSpec((tm, LANES), row))
        out_shape.append(jax.ShapeDtypeStruct((t, LANES), F32))
        args.append(w_router)
    return pl.pallas_call(
        functools.partial(_outproj_kernel, with_router=with_router),
        grid=(t // tm,), in_specs=in_specs, out_specs=out_specs, out_shape=out_shape,
        compiler_params=_params("parallel"),
        name="outproj_router" if with_router else "outproj",
    )(*args)


def _swiglu_partial(h, wg_ref, wu_ref, wd_ref):
    a = _dot(h, wg_ref[...])
    u = _dot(h, wu_ref[...])
    act = (a * (1.0 / (1.0 + jnp.exp(-a)))) * u
    return _dot(act.astype(BF16), wd_ref[...])


def _ffn_kernel(x1_ref, h_ref, mod_ref, wg_ref, wu_ref, wd_ref, o_ref, acc_ref):
    f = pl.program_id(1)

    @pl.when(f == 0)
    def _():
        acc_ref[...] = jnp.zeros(acc_ref.shape, F32)

    acc_ref[...] += _swiglu_partial(h_ref[...], wg_ref, wu_ref, wd_ref)

    @pl.when(f == pl.num_programs(1) - 1)
    def _():
        o_ref[...] = x1_ref[...] + mod_ref[5:6, :] * acc_ref[...]


def ffn_dense(x1, h, mod, wg, wu, wd, seq, tm=512, tf=1408):
    t, d = x1.shape
    dff = wg.shape[1]
    tiles_per_batch = seq // tm
    row = lambda i, f: (i, 0)
    return pl.pallas_call(
        _ffn_kernel,
        grid=(t // tm, dff // tf),
        in_specs=[pl.BlockSpec((tm, d), row), pl.BlockSpec((tm, d), row),
                  pl.BlockSpec((None, 6, d), lambda i, f: (i // tiles_per_batch, 0, 0)),
                  pl.BlockSpec((d, tf), lambda i, f: (0, f)), pl.BlockSpec((d, tf), lambda i, f: (0, f)),
                  pl.BlockSpec((tf, d), lambda i, f: (f, 0))],
        out_specs=pl.BlockSpec((tm, d), row),
        out_shape=jax.ShapeDtypeStruct((t, d), F32),
        scratch_shapes=[pltpu.VMEM((tm, d), F32)],
        compiler_params=_params("parallel", "arbitrary"),
        name="ffn_dense",
    )(x1, h, mod, wg, wu, wd)


MOE_TILE = 512
MOE_PARTS = 3
SC_CORES, SC_SUBCORES = 2, 16
SC_ROWS = 64
HI16 = -65536


def _pack_rows(x):
    c = x.shape[1] // 2
    bits = lax.bitcast_convert_type(x.astype(jnp.bfloat16).astype(F32), jnp.int32)
    return lax.shift_right_logical(bits[:, :c], jnp.int32(16)) | (bits[:, c:] & jnp.int32(HI16))


def _unpack_rows(w):
    lo = lax.bitcast_convert_type(lax.shift_left(w, jnp.int32(16)), F32)
    hi = lax.bitcast_convert_type(w & jnp.int32(HI16), F32)
    return jnp.concatenate([lo, hi], axis=1)


def sc_gather_rows(table, idx):
    d = table.shape[1]
    b = idx.shape[0]
    per_worker = b // (SC_CORES * SC_SUBCORES)
    assert per_worker * SC_CORES * SC_SUBCORES == b and per_worker % (2 * SC_ROWS) == 0
    mesh = plsc.VectorSubcoreMesh(core_axis_name="c", subcore_axis_name="s")
    idx_buf = pltpu.VMEM((SC_ROWS,), jnp.int32)
    row_buf = pltpu.VMEM((SC_ROWS, d), table.dtype)

    @functools.partial(
        pl.kernel, mesh=mesh, out_type=jax.ShapeDtypeStruct((b, d), table.dtype),
        scratch_types=[idx_buf, idx_buf, row_buf, row_buf] + [pltpu.SemaphoreType.DMA] * 4,
        name="sc_gather_rows")
    def gather(table_hbm, idx_hbm, out_hbm, idx0, idx1, rows0, rows1, sem_g0, sem_g1, sem_w0, sem_w1):
        base = (lax.axis_index("s") * SC_CORES + lax.axis_index("c")) * per_worker

        @pl.loop(0, per_worker // (2 * SC_ROWS))
        def _(pair):
            off0 = pl.multiple_of(base + pair * (2 * SC_ROWS), SC_ROWS)
            off1 = pl.multiple_of(off0 + SC_ROWS, SC_ROWS)
            pltpu.sync_copy(idx_hbm.at[pl.ds(off0, SC_ROWS)], idx0)
            pltpu.sync_copy(idx_hbm.at[pl.ds(off1, SC_ROWS)], idx1)
            gather0 = pltpu.async_copy(table_hbm.at[idx0], rows0, sem_g0)
            gather1 = pltpu.async_copy(table_hbm.at[idx1], rows1, sem_g1)
            gather0.wait()
            write0 = pltpu.async_copy(rows0, out_hbm.at[pl.ds(off0, SC_ROWS)], sem_w0)
            gather1.wait()
            write1 = pltpu.async_copy(rows1, out_hbm.at[pl.ds(off1, SC_ROWS)], sem_w1)
            write0.wait()
            write1.wait()

    return gather(table, idx)


def _moe_layout(e0, e1, t, n_exp):
    experts = jnp.arange(n_exp, dtype=jnp.int32)
    routed = ((e0[:, None] == experts) | (e1[:, None] == experts)).astype(jnp.int32)
    csum = jnp.cumsum(routed, axis=0)
    padded = (csum[-1] + MOE_TILE - 1) // MOE_TILE * MOE_TILE
    ends = jnp.cumsum(padded)
    pos = (ends - padded)[None, :] + csum - routed
    pos0 = jnp.take_along_axis(pos, e0[:, None], axis=1)[:, 0]
    pos1 = jnp.take_along_axis(pos, e1[:, None], axis=1)[:, 0]
    n_rows = 2 * t + n_exp * MOE_TILE
    tok = jnp.arange(t, dtype=jnp.int32)
    src = jnp.zeros((n_rows,), jnp.int32).at[jnp.concatenate([pos0, pos1])].set(
        jnp.concatenate([tok, tok]), unique_indices=True, mode="promise_in_bounds")
    n_tiles = n_rows // MOE_TILE
    n_valid = ends[-1] // MOE_TILE
    tile_expert = jnp.sum(jnp.arange(n_tiles)[:, None] * MOE_TILE >= ends[None, :], axis=1)
    tile_expert = tile_expert[jnp.minimum(jnp.arange(n_tiles), n_valid - 1)].astype(jnp.int32)
    return pos0, pos1, src, tile_expert, n_valid.astype(jnp.int32).reshape(1)


W_ROWS_IN = 128
W_ROWS_DOWN = 352
W_SLOTS = 4
FF_SPLIT = 2


def _experts_packed_kernel(te_ref, nv_ref, first_ref, x_ref, wg_hbm, wu_hbm, wd_hbm, *rest, layer):
    o_ref, wg_b, wu_b, wd_b, st_in, st_dn, sems = rest[-7:]
    j = pl.program_id(0)
    valid = j < nv_ref[0]

    @pl.when(valid & (first_ref[j] == 1))
    def _():
        e = te_ref[j]
        plan = []
        for w_hbm, w_b, rb, st, sem0 in ((wg_hbm, wg_b, W_ROWS_IN, st_in, 0), (wu_hbm, wu_b, W_ROWS_IN, st_in, 0),
                                         (wd_hbm, wd_b, W_ROWS_DOWN, st_dn, W_SLOTS)):
            assert w_b.shape[0] % rb == 0
            plan += [(w_hbm, w_b, r0, rb, st, sem0) for r0 in range(0, w_b.shape[0], rb)]
        uses = {0: 0, W_SLOTS: 0}
        copies = []
        for w_hbm, w_b, r0, rb, st, sem0 in plan:
            slot = uses[sem0] % W_SLOTS
            uses[sem0] += 1
            copies.append((pltpu.make_async_copy(w_hbm.at[layer, e, pl.ds(r0, rb), :], st.at[slot],
                                                 sems.at[sem0 + slot]), st, slot, w_b, r0, rb))
        ahead = W_SLOTS - 1
        for copy in copies[:ahead]:
            copy[0].start()
        for b, (copy, st, slot, w_b, r0, rb) in enumerate(copies):
            if b + ahead < len(copies):
                copies[b + ahead][0].start()
            copy.wait()
            w_b[r0:r0 + rb, :] = st[slot].astype(BF16)

    @pl.when(valid)
    def _():
        x = _unpack_rows(x_ref[...]).astype(BF16)
        tf = wg_b.shape[1] // FF_SPLIT
        acc = None
        for f in range(FF_SPLIT):
            cols = slice(f * tf, (f + 1) * tf)
            a = _dot(x, wg_b[:, cols])
            u = _dot(x, wu_b[:, cols])
            act = (a * (1.0 / (1.0 + jnp.exp(-a)))) * u
            part = _dot(act.astype(BF16), wd_b[cols, :])
            acc = part if acc is None else acc + part
        o_ref[...] = _pack_rows(acc)


def moe_experts_packed(x_part, tile_expert, n_valid, w_gate, w_up, w_down, layer, y_prev, part, n_rows):
    rows_part, half = x_part.shape
    d = 2 * half
    dff = w_gate.shape[3]
    tiles_part = rows_part // MOE_TILE
    first = jnp.concatenate([jnp.ones((1,), jnp.int32),
                             (tile_expert[1:] != tile_expert[:-1]).astype(jnp.int32)])

    def tile(j, nv):
        return jnp.maximum(jnp.minimum(j, nv[0] - 1), 0)

    hbm = pl.BlockSpec(memory_space=pl.ANY)
    in_specs = [pl.BlockSpec((MOE_TILE, half), lambda j, te, nv, fi: (tile(j, nv), 0)), hbm, hbm, hbm]
    args = [tile_expert, n_valid, first, x_part, w_gate, w_up, w_down]
    aliases = {}
    if y_prev is not None:
        in_specs.append(hbm)
        args.append(y_prev)
        aliases = {len(args) - 1: 0}
    return pl.pallas_call(
        functools.partial(_experts_packed_kernel, layer=layer),
        grid_spec=pltpu.PrefetchScalarGridSpec(
            num_scalar_prefetch=3, grid=(tiles_part,), in_specs=in_specs,
            out_specs=pl.BlockSpec((MOE_TILE, half), lambda j, te, nv, fi: (part * tiles_part + tile(j, nv), 0)),
            scratch_shapes=[pltpu.VMEM((d, dff), BF16), pltpu.VMEM((d, dff), BF16), pltpu.VMEM((dff, d), BF16),
                            pltpu.VMEM((W_SLOTS, W_ROWS_IN, dff), F32), pltpu.VMEM((W_SLOTS, W_ROWS_DOWN, d), F32),
                            pltpu.SemaphoreType.DMA((2 * W_SLOTS,))]),
        out_shape=jax.ShapeDtypeStruct((n_rows, half), jnp.int32),
        input_output_aliases=aliases,
        compiler_params=_params("arbitrary"),
        name="moe_experts",
    )(*args)


def _combine_packed_kernel(y0_ref, y1_ref, route_ref, x1_ref, mod_ref, o_ref):
    rt = route_ref[...]
    f = rt[:, 2:3] * _unpack_rows(y0_ref[...]) + rt[:, 3:4] * _unpack_rows(y1_ref[...])
    o_ref[...] = x1_ref[...] + mod_ref[5:6, :] * f


def moe_combine_packed(y_pairs, route, x1, mod, seq, tm=512):
    t, d = x1.shape
    nt = t // tm
    tiles_per_batch = seq // tm
    row = lambda i: (i, 0)
    return pl.pallas_call(
        _combine_packed_kernel,
        grid=(nt,),
        in_specs=[pl.BlockSpec((tm, d // 2), row), pl.BlockSpec((tm, d // 2), lambda i: (i + nt, 0)),
                  pl.BlockSpec((tm, LANES), row), pl.BlockSpec((tm, d), row),
                  pl.BlockSpec((None, 6, d), lambda i: (i // tiles_per_batch, 0, 0))],
        out_specs=pl.BlockSpec((tm, d), row),
        out_shape=jax.ShapeDtypeStruct((t, d), F32),
        compiler_params=_params("parallel"),
        name="moe_combine",
    )(y_pairs, y_pairs, route, x1, mod)


def ffn_moe_sc(x1, h_packed, route, mod, w_gate, w_up, w_down, layer, seq):
    t = x1.shape[0]
    e0, e1 = route[:, 0].astype(jnp.int32), route[:, 1].astype(jnp.int32)
    pos0, pos1, src, tile_expert, n_valid = _moe_layout(e0, e1, t, w_gate.shape[1])
    n_rows = src.shape[0]
    rows_part = n_rows // MOE_PARTS
    tiles_part = rows_part // MOE_TILE
    assert rows_part * MOE_PARTS == n_rows and tiles_part * MOE_TILE == rows_part
    h_parts = [sc_gather_rows(h_packed, src[p * rows_part:(p + 1) * rows_part]) for p in range(MOE_PARTS)]
    y_sorted = None
    for p in range(MOE_PARTS):
        n_valid_part = jnp.clip(n_valid - p * tiles_part, 0, tiles_part)
        y_sorted = moe_experts_packed(h_parts[p], tile_expert[p * tiles_part:(p + 1) * tiles_part], n_valid_part,
                                      w_gate, w_up, w_down, layer, y_sorted, p, n_rows)
    y_pairs = sc_gather_rows(y_sorted, jnp.concatenate([pos0, pos1]))
    return moe_combine_packed(y_pairs, route, x1, mod, seq)


def _final_norm_kernel(x_ref, g_ref, o_ref):
    x = x_ref[...]
    o_ref[...] = (x * lax.rsqrt(jnp.mean(x * x, axis=-1, keepdims=True) + EPS)) * g_ref[...]


def final_norm(x, g, tm=512):
    t, d = x.shape
    return pl.pallas_call(
        _final_norm_kernel,
        grid=(t // tm,),
        in_specs=[pl.BlockSpec((tm, d), lambda i: (i, 0)), pl.BlockSpec((1, d), lambda i: (0, 0))],
        out_specs=pl.BlockSpec((tm, d), lambda i: (i, 0)),
        out_shape=jax.ShapeDtypeStruct((t, d), F32),
        compiler_params=_params("parallel"),
        name="final_norm",
    )(x, g)


def _cast_kernel(w_ref, o_ref):
    o_ref[...] = w_ref[...].astype(o_ref.dtype)


def cast_layer_bf16(w, layer):
    _, e, a, b = w.shape
    ta = a // 2
    return pl.pallas_call(
        _cast_kernel,
        grid=(e, a // ta),
        in_specs=[pl.BlockSpec((None, None, ta, b), lambda i, r: (layer, i, r, 0))],
        out_specs=pl.BlockSpec((None, ta, b), lambda i, r: (i, r, 0)),
        out_shape=jax.ShapeDtypeStruct((e, a, b), BF16),
        compiler_params=_params("parallel", "parallel"),
        name="cast_bf16",
    )(w)


def _rope_tables(positions, dim):
    rot = dim // ROPE_FRACTION
    half = rot // 2
    inv = 1.0 / (ROPE_THETA ** (np.arange(0, rot, 2, dtype=np.float32) / rot))
    ang = positions.reshape(-1).astype(F32)[:, None] * jnp.asarray(inv, F32)
    cos, sin = jnp.cos(ang), jnp.sin(ang)
    t = ang.shape[0]
    ones = jnp.ones((t, dim - rot), F32)
    zeros = lambda w: jnp.zeros((t, w), F32)
    reps = LANES // dim
    c = jnp.tile(jnp.concatenate([cos, cos, ones], axis=1), (1, reps))
    sa = jnp.tile(jnp.concatenate([-sin, zeros(dim - half)], axis=1), (1, reps))
    sb = jnp.tile(jnp.concatenate([zeros(half), sin, zeros(dim - rot)], axis=1), (1, reps))
    return c, sa, sb


def _relayout_w_in(w):
    pts = np.cumsum([0, 384, 64, 64, 256, 64, 4, 256, 256, 256, 384, 384, 384])
    (q_a, k_a, v_a, q_i, k_i, w_i, q_b, k_b, v_b, q_c, k_c, v_c) = [w[..., pts[j]:pts[j + 1]] for j in range(12)]
    w_i_pad = jnp.concatenate([w_i, jnp.zeros(w.shape[:-1] + (LANES - IDX_HEADS,), w.dtype)], axis=-1)
    return jnp.concatenate([q_a, k_a, k_a, v_a, v_a, q_i, k_i, k_i, w_i_pad,
                            q_b, k_b, v_b, q_c, k_c, v_c], axis=-1).astype(BF16)


def kernel(x, c, positions, w_in, w_out, diff_lambda, diff_subln, w_ada, b_ada, g_attn, g_ffn, w_ff_gate,
           w_ff_up, w_ff_down, w_router, w_exp_gate, w_exp_up, w_exp_down, g_final):
    batch, seq, d = x.shape
    depth = w_in.shape[0]
    t = batch * seq
    tabs = _rope_tables(positions, HEAD_DIM) + _rope_tables(positions, B_QK_DIM)
    mod_all = adaln_mod(c, w_ada, b_ada).reshape(depth, batch, 6, d)
    w_in_pad = _relayout_w_in(w_in)
    wo = w_out.astype(BF16)
    xf = x.reshape(t, d)
    for layer in range(depth):
        mod = mod_all[layer]
        lam_init = 0.8 - 0.6 * math.exp(-0.3 * layer)
        (qa, kka, vva, qi, kki, wi, qb, kb, vb, qc, kc, vc) = inproj(
            xf, g_attn[layer].reshape(1, d), mod, tabs, w_in_pad, layer, seq)
        ya = dsa_attention(qa, qi, wi, kka, vva, kki, batch, seq)
        g_sub2 = jnp.tile(diff_subln[layer], 2).reshape(1, LANES)
        yb = diff_attention(qb, kb, vb, diff_lambda[layer], g_sub2, lam_init, batch, seq)
        yc = moba_attention(qc, kc, vc, batch, seq)
        j = layer // 2
        gf = g_ffn[layer].reshape(1, d)
        if layer % 2 == 0:
            x1, h = outproj(xf, ya, yb, yc, wo, layer, gf, mod, seq)
            wg, wu, wd = (cast_layer_bf16(w[:, None], j)[0] for w in (w_ff_gate, w_ff_up, w_ff_down))
            xf = ffn_dense(x1, h, mod, wg, wu, wd, seq)
        else:
            wr = jnp.concatenate([w_router[j], jnp.zeros((d, LANES - N_EXPERTS), F32)], axis=1)
            wr_hi = wr.astype(BF16)
            wr = jnp.stack([wr_hi, (wr - wr_hi.astype(F32)).astype(BF16)])
            x1, h_packed, route = outproj(xf, ya, yb, yc, wo, layer, gf, mod, seq, w_router=wr)
            xf = ffn_moe_sc(x1, h_packed, route, mod, w_exp_gate, w_exp_up, w_exp_down, j, seq)
    return final_norm(xf, g_final.reshape(1, d)).reshape(batch, seq, d)
```

```python
import functools
import math

import jax
import jax.numpy as jnp
import numpy as np
from jax import lax
from jax.experimental import pallas as pl
from jax.experimental.pallas import tpu as pltpu
from jax.experimental.pallas import tpu_sc as plsc

F32 = jnp.float32
BF16 = jnp.bfloat16

HEAD_DIM = 64
A_HEADS = 6
IDX_HEADS = 4
B_HEADS = 4
B_QK_DIM = 32
C_HEADS = 6
A_WIDTH, IDX_WIDTH = A_HEADS * HEAD_DIM, IDX_HEADS * HEAD_DIM
B_WIDTH, C_WIDTH = B_HEADS * HEAD_DIM, C_HEADS * HEAD_DIM
DSA_TOPK_MAX = 256
MOBA_BLOCK = 256
MOBA_TOPK = 3
ROPE_THETA = 500000.0
ROPE_FRACTION = 4
SUBLN_EPS = 1e-5
EPS = 1e-6
N_EXPERTS = 8

LANES = 128
NEG = -1e30
INT_MIN = -2 ** 31
MIN_NORMAL_KEY = 0x00800000
MIN_NORMAL_F32 = float(np.float32(2.0 ** -126))
VMEM_LIMIT = 48 * 1024 * 1024
LOG2E = math.log2(math.e)

_G_QA, _G_KKA, _G_VVA, _G_QI, _G_KKI, _G_WI = (0, 384), (384, 512), (512, 640), (640, 896), (896, 1024), (1024, 1152)
_G_QB, _G_KB, _G_VB = (1152, 1408), (1408, 1664), (1664, 1920)
_G_QC, _G_KC, _G_VC = (1920, 2304), (2304, 2688), (2688, 3072)
D_IN_PAD = 3072


def _params(*sem):
    return pltpu.CompilerParams(dimension_semantics=sem, vmem_limit_bytes=VMEM_LIMIT)


def _dot(a, b):
    return jnp.dot(a, b, preferred_element_type=F32)


def _dot_nt(a, b):
    return lax.dot_general(a, b, (((1,), (1,)), ((), ())), preferred_element_type=F32)


def _adaln_kernel(c_ref, w_ref, b_ref, o_ref):
    c = c_ref[...]
    c_act = c * (1.0 / (1.0 + jnp.exp(-c)))
    o_ref[...] = jnp.dot(c_act, w_ref[...], preferred_element_type=F32,
                         precision=lax.Precision.HIGHEST) + b_ref[...]


def adaln_mod(c, w_ada, b_ada, tn=1536):
    depth, d, n = w_ada.shape
    b = c.shape[0]
    return pl.pallas_call(
        _adaln_kernel,
        grid=(depth, n // tn),
        in_specs=[pl.BlockSpec((b, d), lambda l, j: (0, 0)),
                  pl.BlockSpec((None, d, tn), lambda l, j: (l, 0, j)),
                  pl.BlockSpec((None, 1, tn), lambda l, j: (l, 0, j))],
        out_specs=pl.BlockSpec((None, b, tn), lambda l, j: (l, 0, j)),
        out_shape=jax.ShapeDtypeStruct((depth, b, n), F32),
        compiler_params=_params("parallel", "parallel"),
        name="adaln_mod",
    )(c, w_ada, b_ada.reshape(depth, 1, n))


def _norm_mod(x, g, shift, scale, eps=EPS):
    y = x * lax.rsqrt(jnp.mean(x * x, axis=-1, keepdims=True) + eps)
    return (y * g) * (1.0 + scale) + shift


def _rope_store(acc, o_ref, cos, sa, sb, half):
    for j in range(acc.shape[1] // LANES):
        a = acc[:, j * LANES:(j + 1) * LANES]
        r = a * cos + pltpu.roll(a, half, 1) * sb + pltpu.roll(a, LANES - half, 1) * sa
        o_ref[:, j * LANES:(j + 1) * LANES] = r.astype(o_ref.dtype)


def _inproj_kernel(x_ref, g_ref, mod_ref, c64_ref, sa64_ref, sb64_ref, c32_ref, sa32_ref, sb32_ref, w_ref,
                   qa_ref, kka_ref, vva_ref, qi_ref, kki_ref, wi_ref,
                   qb_ref, kb_ref, vb_ref, qc_ref, kc_ref, vc_ref):
    h = _norm_mod(x_ref[...], g_ref[...], mod_ref[0:1, :], mod_ref[1:2, :]).astype(BF16)
    c64, sa64, sb64 = c64_ref[...], sa64_ref[...], sb64_ref[...]
    c32, sa32, sb32 = c32_ref[...], sa32_ref[...], sb32_ref[...]

    def proj(cols):
        return _dot(h, w_ref[:, cols[0]:cols[1]])

    qk_scale = HEAD_DIM ** -0.5 * LOG2E
    _rope_store(proj(_G_QA), qa_ref, c64 * qk_scale, sa64 * qk_scale, sb64 * qk_scale, 8)
    _rope_store(proj(_G_KKA), kka_ref, c64, sa64, sb64, 8)
    vva_ref[...] = proj(_G_VVA).astype(vva_ref.dtype)
    _rope_store(proj(_G_QI), qi_ref, c64, sa64, sb64, 8)
    _rope_store(proj(_G_KKI), kki_ref, c64, sa64, sb64, 8)
    wi_ref[...] = proj(_G_WI) * (IDX_HEADS ** -0.5 * HEAD_DIM ** -0.5)
    b_scale = B_QK_DIM ** -0.5 * LOG2E
    _rope_store(proj(_G_QB), qb_ref, c32 * b_scale, sa32 * b_scale, sb32 * b_scale, 4)
    _rope_store(proj(_G_KB), kb_ref, c32, sa32, sb32, 4)
    vb_ref[...] = proj(_G_VB).astype(vb_ref.dtype)
    _rope_store(proj(_G_QC), qc_ref, c64 * qk_scale, sa64 * qk_scale, sb64 * qk_scale, 8)
    _rope_store(proj(_G_KC), kc_ref, c64, sa64, sb64, 8)
    vc_ref[...] = proj(_G_VC).astype(vc_ref.dtype)


def inproj(x, g, mod, tabs, w_pad, layer, seq, tm=512):
    t, d = x.shape
    tiles_per_batch = seq // tm
    row = lambda i: (i, 0)
    widths = [A_WIDTH, LANES, LANES, IDX_WIDTH, LANES, LANES, B_WIDTH, B_WIDTH, B_WIDTH, C_WIDTH, C_WIDTH, C_WIDTH]
    dtypes = [BF16, BF16, BF16, BF16, BF16, F32, BF16, BF16, BF16, BF16, BF16, BF16]
    return pl.pallas_call(
        _inproj_kernel,
        grid=(t // tm,),
        in_specs=[pl.BlockSpec((tm, d), row),
                  pl.BlockSpec((1, d), lambda i: (0, 0)),
                  pl.BlockSpec((None, 6, d), lambda i: (i // tiles_per_batch, 0, 0))]
                 + [pl.BlockSpec((tm, LANES), row)] * 6
                 + [pl.BlockSpec((None, d, D_IN_PAD), lambda i: (layer, 0, 0))],
        out_specs=[pl.BlockSpec((tm, w), row) for w in widths],
        out_shape=[jax.ShapeDtypeStruct((t, w), dt) for w, dt in zip(widths, dtypes)],
        compiler_params=_params("parallel"),
        name="inproj",
    )(x, g, mod, *tabs, w_pad)


def _init_stats(m_ref, l_ref, acc_ref):
    m_ref[...] = jnp.full(m_ref.shape, -jnp.inf, F32)
    l_ref[...] = jnp.zeros(l_ref.shape, F32)
    acc_ref[...] = jnp.zeros(acc_ref.shape, F32)


def _key_to_f32(k):
    return lax.bitcast_convert_type(jnp.where(k >= 0, k, k ^ 0x7FFFFFFF), F32)


def dsa_attention(qa, qi, wi, kka, vva, kki, batch, seq, tq=256):
    t = qa.shape[0]
    nq = seq // tq
    k_top = min(DSA_TOPK_MAX, seq // 4)
    qrow = lambda b, i: (b * nq + i, 0)
    full = lambda b, i: (b, 0)
    return pl.pallas_call(
        functools.partial(_dsa_kernel_t, tq=tq, k_top=k_top, seq=seq),
        grid=(batch, nq),
        in_specs=[pl.BlockSpec((tq, A_WIDTH), qrow), pl.BlockSpec((tq, IDX_WIDTH), qrow),
                  pl.BlockSpec((tq, LANES), qrow),
                  pl.BlockSpec((seq, LANES), full), pl.BlockSpec((seq, LANES), full),
                  pl.BlockSpec((seq, LANES), full)],
        out_specs=pl.BlockSpec((tq, A_WIDTH), qrow),
        out_shape=jax.ShapeDtypeStruct((t, A_WIDTH), BF16),
        scratch_shapes=[pltpu.VMEM((seq, tq), F32), pltpu.VMEM((seq, tq), jnp.int16),
                        pltpu.VMEM((seq, tq), jnp.int16), pltpu.VMEM((1, LANES, seq), BF16),
                        pltpu.VMEM((3, 2 * tq, LANES), BF16),
                        pltpu.VMEM((3, 1, 2 * tq), F32), pltpu.VMEM((3, 1, 2 * tq), F32),
                        pltpu.VMEM((3, LANES, 2 * tq), F32)],
        compiler_params=_params("parallel", "arbitrary"),
        name="dsa_attention",
    )(qa, qi, wi, kka, vva, kki)


def diff_attention(qb, kb, vb, lam_vec, g_sub2, lam_init, batch, seq, tq=256):
    t = qb.shape[0]
    nq = seq // tq
    qrow = lambda b, i: (b * nq + i, 0)
    full = lambda b, i: (b, 0)
    const = lambda b, i: (0, 0)
    return pl.pallas_call(
        functools.partial(_diff_kernel_t, tq=tq, lam_init=lam_init, seq=seq),
        grid=(batch, nq),
        in_specs=[pl.BlockSpec((tq, B_WIDTH), qrow), pl.BlockSpec((seq, B_WIDTH), full),
                  pl.BlockSpec((seq, B_WIDTH), full),
                  pl.BlockSpec((4, B_QK_DIM), const), pl.BlockSpec((1, LANES), const)],
        out_specs=pl.BlockSpec((tq, B_WIDTH), qrow),
        out_shape=jax.ShapeDtypeStruct((t, B_WIDTH), BF16),
        scratch_shapes=[pltpu.VMEM((2, LANES, seq), BF16), pltpu.VMEM((2, 4 * tq, LANES), BF16),
                        pltpu.VMEM((2, 1, 4 * tq), F32), pltpu.VMEM((2, 1, 4 * tq), F32),
                        pltpu.VMEM((2, LANES, 4 * tq), F32)],
        compiler_params=_params("parallel", "arbitrary"),
        name="diff_attention",
    )(qb, kb, vb, lam_vec, g_sub2)


def moba_attention(qc, kc, vc, batch, seq):
    tq = MOBA_BLOCK
    t = qc.shape[0]
    nb = seq // tq
    n_sel = min(MOBA_TOPK, nb - 1)
    nbp = 8
    assert seq % tq == 0 and nb <= nbp
    qrow = lambda b, i: (b * nb + i, 0)
    full = lambda b, i: (b, 0)
    return pl.pallas_call(
        functools.partial(_moba_kernel_t, tq=tq, nb=nb, n_sel=n_sel, seq=seq),
        grid=(batch, nb),
        in_specs=[pl.BlockSpec((tq, C_WIDTH), qrow), pl.BlockSpec((seq, C_WIDTH), full),
                  pl.BlockSpec((seq, C_WIDTH), full)],
        out_specs=pl.BlockSpec((tq, C_WIDTH), qrow),
        out_shape=jax.ShapeDtypeStruct((t, C_WIDTH), BF16),
        scratch_shapes=[pltpu.VMEM((nbp, C_WIDTH), F32), pltpu.VMEM((3, LANES, seq), BF16),
                        pltpu.VMEM((3, 2 * tq, LANES), BF16), pltpu.VMEM((3, nbp, 2 * tq), F32),
                        pltpu.VMEM((3, 1, 2 * tq), F32), pltpu.VMEM((3, 1, 2 * tq), F32),
                        pltpu.VMEM((3, LANES, 2 * tq), F32)],
        compiler_params=_params("parallel", "arbitrary"),
        name="moba_attention",
    )(qc, kc, vc)


def _attend_t(k, q_ref, g, v_t, m_ref, l_ref, acc_ref, mask=None):
    for j in range(q_ref.shape[1] // LANES):
        cols = slice(j * LANES, (j + 1) * LANES)
        s = _dot_nt(k, q_ref[g, cols, :])
        if mask is not None:
            s = mask(j, s)
        m_prev = m_ref[g, :, cols]
        m_new = jnp.maximum(m_prev, jnp.max(s, axis=0, keepdims=True))
        alpha = jnp.exp2(m_prev - m_new)
        p = jnp.exp2(s - m_new)
        l_ref[g, :, cols] = alpha * l_ref[g, :, cols] + jnp.sum(p, axis=0, keepdims=True)
        acc_ref[g, :, cols] = alpha * acc_ref[g, :, cols] + _dot(v_t, p.astype(BF16))
        m_ref[g, :, cols] = m_new


def _causal_mask_t(tk, tq):
    key = lax.broadcasted_iota(jnp.int32, (tk, LANES), 0)
    qry = lax.broadcasted_iota(jnp.int32, (tk, LANES), 1)

    def mask(j, s):
        return jnp.where(key <= qry + (j * LANES) % tq, s, NEG)
    return mask


def _transpose_values(v_ref, vt_ref, seq, tk):
    for g in range(vt_ref.shape[0]):
        for n in range(seq // tk):
            blk = v_ref[n * tk:(n + 1) * tk, g * LANES:(g + 1) * LANES].astype(F32)
            vt_ref[g, :, n * tk:(n + 1) * tk] = blk.T.astype(vt_ref.dtype)


def _head_pair_out(acc_t, l, tq):
    even = acc_t[0:HEAD_DIM, 0:tq] / l[:, 0:tq]
    odd = acc_t[HEAD_DIM:LANES, tq:2 * tq] / l[:, tq:2 * tq]
    return jnp.concatenate([even, odd], axis=0).T


def _dsa_kernel_t(qa_ref, qi_ref, wiq_ref, kka_ref, vva_ref, kki_ref, o_ref,
                  sc_ref, hi_ref, lo_ref, vt_ref, qs_ref, m_ref, l_ref, acc_ref, *, tq, k_top, seq):
    i = pl.program_id(1)
    nk = i + 1
    t0 = i * tq
    lo = lax.broadcasted_iota(jnp.int32, (tq, LANES), 1) < HEAD_DIM
    zero_b = jnp.zeros((tq, LANES), BF16)

    @pl.when(i == 0)
    def _():
        _transpose_values(vva_ref, vt_ref, seq, tq)

    def stack_heads(q):
        out = []
        for g in range(q.shape[1] // LANES):
            qg = q[:, g * LANES:(g + 1) * LANES]
            out += [jnp.where(lo, qg, zero_b), jnp.where(lo, zero_b, qg)]
        return out

    qa_stack = stack_heads(qa_ref[...])
    for g in range(3):
        qs_ref[g, 0:tq, :] = qa_stack[2 * g]
        qs_ref[g, tq:2 * tq, :] = qa_stack[2 * g + 1]
    qi_stack = jnp.concatenate(stack_heads(qi_ref[...]), axis=0)
    wi_t = wiq_ref[...].T

    def rows(c):
        return pl.ds(pl.multiple_of(c * tq, tq), tq)

    key_pos = lax.broadcasted_iota(jnp.int32, (tq, tq), 0)
    qry_pos = lax.broadcasted_iota(jnp.int32, (tq, tq), 1)

    def idx_body(c, carry):
        r = jnp.maximum(_dot_nt(kki_ref[rows(c), :], qi_stack), 0.0)
        s = wi_t[0:1, :] * r[:, 0:tq]
        for h in range(1, IDX_HEADS):
            s = s + wi_t[h:h + 1, :] * r[:, h * tq:(h + 1) * tq]
        causal = (c * tq + key_pos) <= (t0 + qry_pos)
        s = jnp.where(causal, s, -jnp.inf)
        s = jnp.where(jnp.abs(s) < MIN_NORMAL_F32, 0.0, s)
        sc_ref[rows(c), :] = s
        bits = lax.bitcast_convert_type(s, jnp.int32)
        key = jnp.where(bits >= 0, bits, bits ^ 0x7FFFFFFF)
        hi_ref[rows(c), :] = lax.shift_right_arithmetic(key, jnp.int32(16)).astype(jnp.int16)
        lo_ref[rows(c), :] = ((key & 0xFFFF) - 32768).astype(jnp.int16)
        return carry

    lax.fori_loop(0, nk, idx_body, 0)

    def count16(ref, cand, strict):
        c16 = cand.astype(jnp.int16)

        def body(c, acc):
            x = ref[rows(c), :]
            hit = jnp.where((x > c16) if strict else (x >= c16), jnp.int16(1), jnp.int16(0))
            for r in range(tq // 16):
                acc = acc + hit[r * 16:(r + 1) * 16]
            return acc
        acc = lax.fori_loop(0, nk, body, jnp.zeros((16, tq), jnp.int16))
        return jnp.sum(acc.astype(F32), axis=0, keepdims=True)

    def search16(ref, k_need):
        v0 = jnp.where(count16(ref, jnp.zeros((1, tq), jnp.int32), False) >= k_need, 0, -32768).astype(jnp.int32)

        def bisect(b, v):
            trial = v | lax.shift_left(jnp.int32(1), 14 - b)
            return jnp.where(count16(ref, trial, False) >= k_need, trial, v)
        return lax.fori_loop(0, 15, bisect, v0)

    kf = float(k_top)

    def search():
        hi_k = search16(hi_ref, kf)
        above = count16(hi_ref, hi_k, True)
        hi16 = hi_k.astype(jnp.int16)

        def keep_bucket(c, carry):
            lo_ref[rows(c), :] = jnp.where(hi_ref[rows(c), :] == hi16, lo_ref[rows(c), :], jnp.int16(-32768))
            return carry
        lax.fori_loop(0, nk, keep_bucket, 0)
        lo_k = search16(lo_ref, kf - above)
        n_gt = above + count16(lo_ref, lo_k, True)
        return lax.shift_left(hi_k, jnp.int32(16)) | (lo_k + 32768), kf - n_gt

    key, need = lax.cond(t0 + tq <= k_top,
                         lambda: (jnp.full((1, tq), INT_MIN, jnp.int32), jnp.full((1, tq), kf, F32)), search)
    thr = _key_to_f32(key)
    thr_next = _key_to_f32(jnp.where(key == 0, MIN_NORMAL_KEY, key + 1))
    all_sel = (t0 + lax.broadcasted_iota(jnp.int32, (1, tq), 1)) < k_top
    lower = (qry_pos <= key_pos).astype(BF16)

    _init_stats(m_ref, l_ref, acc_ref)

    def att_body(c, tie_run):
        s_idx = sc_ref[rows(c), :]
        ge = s_idx >= thr
        gt = s_idx >= thr_next
        tie = jnp.where(gt, 0.0, jnp.where(ge, 1.0, 0.0))
        prefix = _dot(lower, tie.astype(BF16)) + tie_run
        take = jnp.where(gt, 1.0, jnp.where(prefix <= need, tie, 0.0))
        take = jnp.where(all_sel, 1.0, take)
        take = jnp.where(s_idx > -jnp.inf, take, 0.0)

        def mask(j, s):
            q0 = (j * LANES) % tq
            return jnp.where(take[:, q0:q0 + LANES] > 0.5, s, NEG)

        kk = kka_ref[rows(c), :]
        vt = vt_ref[0, :, rows(c)]
        for g in range(3):
            _attend_t(kk, qs_ref, g, vt, m_ref, l_ref, acc_ref, mask)
        return tie_run + jnp.sum(tie, axis=0, keepdims=True)

    lax.fori_loop(0, nk, att_body, jnp.zeros((1, tq), F32))

    for g in range(3):
        o_ref[:, g * LANES:(g + 1) * LANES] = _head_pair_out(acc_ref[g], l_ref[g], tq).astype(o_ref.dtype)


def _diff_kernel_t(qb_ref, kb_ref, vb_ref, lam_ref, gsub_ref, o_ref,
                   vt_ref, qs_ref, m_ref, l_ref, acc_ref, *, tq, lam_init, seq):
    i = pl.program_id(1)
    lane = lax.broadcasted_iota(jnp.int32, (tq, LANES), 1)
    zero_b = jnp.zeros((tq, LANES), BF16)

    @pl.when(i == 0)
    def _():
        _transpose_values(vb_ref, vt_ref, seq, tq)

    qb = qb_ref[...]
    for g in range(2):
        qg = qb[:, g * LANES:(g + 1) * LANES]
        for j in range(4):
            qs_ref[g, j * tq:(j + 1) * tq, :] = jnp.where(lane // B_QK_DIM == j, qg, zero_b)
    _init_stats(m_ref, l_ref, acc_ref)

    def rows(c):
        return pl.ds(pl.multiple_of(c * tq, tq), tq)

    def step(c, masked):
        mask = _causal_mask_t(tq, tq) if masked else None
        for g in range(2):
            _attend_t(kb_ref[rows(c), g * LANES:(g + 1) * LANES], qs_ref, g, vt_ref[g, :, rows(c)],
                      m_ref, l_ref, acc_ref, mask)

    def body(c, carry):
        step(c, False)
        return carry

    lax.fori_loop(0, i, body, 0)
    step(i, True)

    lv = lam_ref[...]
    lam = (jnp.exp(jnp.sum(lv[0:1] * lv[1:2], axis=1, keepdims=True))
           - jnp.exp(jnp.sum(lv[2:3] * lv[3:4], axis=1, keepdims=True)) + lam_init)
    gsub = gsub_ref[...]
    for g in range(2):
        acc_t, l = acc_ref[g], l_ref[g]

        def prob(r0, j):
            return acc_t[r0:r0 + HEAD_DIM, j * tq:(j + 1) * tq] / l[:, j * tq:(j + 1) * tq]

        halves = []
        for r0, j in ((0, 0), (HEAD_DIM, 2)):
            o = prob(r0, j) - lam * prob(r0, j + 1)
            ms = jnp.mean(o * o, axis=0, keepdims=True)
            halves.append(o * lax.rsqrt(ms + SUBLN_EPS))
        y = (jnp.concatenate(halves, axis=0).T * gsub) * (1.0 - lam_init)
        o_ref[:, g * LANES:(g + 1) * LANES] = y.astype(o_ref.dtype)


def _moba_kernel_t(qc_ref, kc_ref, vc_ref, o_ref, kmean_ref, vt_ref, qs_ref, bias_ref, m_ref, l_ref, acc_ref,
                   *, tq, nb, n_sel, seq):
    i = pl.program_id(1)
    nbp = kmean_ref.shape[0]
    lane = lax.broadcasted_iota(jnp.int32, (tq, LANES), 1)
    lo = lane < HEAD_DIM
    zero_b = jnp.zeros((tq, LANES), BF16)

    @pl.when(i == 0)
    def _():
        _transpose_values(vc_ref, vt_ref, seq, tq)
        kmean_ref[...] = jnp.zeros(kmean_ref.shape, F32)
        for n in range(nb):
            kblk = kc_ref[n * tq:(n + 1) * tq, :].astype(F32)
            kmean_ref[n:n + 1, :] = jnp.mean(kblk, axis=0, keepdims=True)

    qc = qc_ref[...]
    sub = lax.broadcasted_iota(jnp.int32, (nbp, 2 * tq), 0)
    past = sub < i
    for g in range(3):
        qg = qc[:, g * LANES:(g + 1) * LANES]
        q2 = jnp.concatenate([jnp.where(lo, qg, zero_b), jnp.where(lo, zero_b, qg)], axis=0)
        km = kmean_ref[:, g * LANES:(g + 1) * LANES]
        km_hi = km.astype(BF16)
        gt = _dot_nt(km_hi, q2) + _dot_nt((km - km_hi.astype(F32)).astype(BF16), q2)
        rows_ = []
        for n in range(nbp):
            gn = gt[n:n + 1, :]
            beats = jnp.where(sub < n, jnp.where(gt >= gn, 1.0, 0.0), jnp.where(gt > gn, 1.0, 0.0))
            beats = jnp.where(sub == n, 0.0, jnp.where(past, beats, 0.0))
            rank = jnp.sum(beats, axis=0, keepdims=True)
            rows_.append(jnp.where(rank < n_sel, 0.0, NEG))
        bias_ref[g] = jnp.where(past, jnp.concatenate(rows_, axis=0), NEG)
        qs_ref[g] = q2
    _init_stats(m_ref, l_ref, acc_ref)

    def rows(c):
        return pl.ds(pl.multiple_of(c * tq, tq), tq)

    def body(c, carry):
        for g in range(3):
            bias = bias_ref[g, pl.ds(c, 1), :]

            def mask(j, s):
                return s + bias[:, j * LANES:(j + 1) * LANES]

            _attend_t(kc_ref[rows(c), g * LANES:(g + 1) * LANES], qs_ref, g, vt_ref[g, :, rows(c)],
                      m_ref, l_ref, acc_ref, mask)
        return carry

    lax.fori_loop(0, i, body, 0)

    causal = _causal_mask_t(tq, tq)
    for g in range(3):
        _attend_t(kc_ref[rows(i), g * LANES:(g + 1) * LANES], qs_ref, g, vt_ref[g, :, rows(i)],
                  m_ref, l_ref, acc_ref, causal)
        o_ref[:, g * LANES:(g + 1) * LANES] = _head_pair_out(acc_ref[g], l_ref[g], tq).astype(o_ref.dtype)


def _route(h, wr_ref, cw_ref):
    hi = h.astype(BF16)
    lo = (h - hi.astype(F32)).astype(BF16)
    logits = _dot(hi, wr_ref[0]) + (_dot(lo, wr_ref[0]) + _dot(hi, wr_ref[1]))
    lane = lax.broadcasted_iota(jnp.int32, logits.shape, 1)
    lg = jnp.where(lane < N_EXPERTS, logits, -jnp.inf)
    v0 = jnp.max(lg, axis=1, keepdims=True)
    i0 = jnp.min(jnp.where(lg == v0, lane, LANES), axis=1, keepdims=True)
    lg1 = jnp.where(lane == i0, -jnp.inf, lg)
    v1 = jnp.max(lg1, axis=1, keepdims=True)
    i1 = jnp.min(jnp.where(lg1 == v1, lane, LANES), axis=1, keepdims=True)
    e1 = jnp.exp(v1 - v0)
    w0 = 1.0 / (1.0 + e1)
    cw_ref[...] = (jnp.where(lane == 0, i0.astype(F32), 0.0) + jnp.where(lane == 1, i1.astype(F32), 0.0)
                   + jnp.where(lane == 2, w0, 0.0) + jnp.where(lane == 3, e1 * w0, 0.0))


def _outproj_kernel(x_ref, ya_ref, yb_ref, yc_ref, wo_ref, g_ref, mod_ref, *rest, with_router):
    if with_router:
        wr_ref, x1_ref, h_ref, cw_ref = rest
    else:
        x1_ref, h_ref = rest
    ab = A_WIDTH + B_WIDTH
    y = (_dot(ya_ref[...], wo_ref[0:A_WIDTH, :]) + _dot(yb_ref[...], wo_ref[A_WIDTH:ab, :])
         + _dot(yc_ref[...], wo_ref[ab:ab + C_WIDTH, :]))
    x1 = x_ref[...] + mod_ref[2:3, :] * y
    x1_ref[...] = x1
    h = _norm_mod(x1, g_ref[...], mod_ref[3:4, :], mod_ref[4:5, :])
    if with_router:
        h_ref[...] = _pack_rows(h)
        _route(h, wr_ref, cw_ref)
    else:
        h_ref[...] = h.astype(BF16)


def outproj(x, ya, yb, yc, wo, layer, g_ffn, mod, seq, w_router=None, tm=512):
    t, d = x.shape
    tiles_per_batch = seq // tm
    row = lambda i: (i, 0)
    const = lambda i: (0, 0)
    with_router = w_router is not None
    in_specs = [pl.BlockSpec((tm, d), row), pl.BlockSpec((tm, A_WIDTH), row), pl.BlockSpec((tm, B_WIDTH), row),
                pl.BlockSpec((tm, C_WIDTH), row), pl.BlockSpec((None, d, d), lambda i: (layer, 0, 0)),
                pl.BlockSpec((1, d), const),
                pl.BlockSpec((None, 6, d), lambda i: (i // tiles_per_batch, 0, 0))]
    out_specs = [pl.BlockSpec((tm, d), row), pl.BlockSpec((tm, d), row)]
    out_shape = [jax.ShapeDtypeStruct((t, d), F32), jax.ShapeDtypeStruct((t, d), BF16)]
    args = [x, ya, yb, yc, wo, g_ffn, mod]
    if with_router:
        in_specs.append(pl.BlockSpec((2, d, LANES), lambda i: (0, 0, 0)))
        out_specs[1] = pl.BlockSpec((tm, d // 2), row)
        out_shape[1] = jax.ShapeDtypeStruct((t, d // 2), jnp.int32)
        out_specs.append(pl.BlockSpec((tm, LANES), row))
        out_shape.append(jax.ShapeDtypeStruct((t, LANES), F32))
        args.append(w_router)
    return pl.pallas_call(
        functools.partial(_outproj_kernel, with_router=with_router),
        grid=(t // tm,), in_specs=in_specs, out_specs=out_specs, out_shape=out_shape,
        compiler_params=_params("parallel"),
        name="outproj_router" if with_router else "outproj",
    )(*args)


W_ROWS_IN = 128
W_ROWS_DOWN = 352
FF_SPLIT = 2


def _load_weights_bf16(layer, e, w_hbms, w_bs, st_in, st_dn, sems):
    slots = st_in.shape[0]
    plan = []
    for w_hbm, w_b, rb, st, sem0 in ((w_hbms[0], w_bs[0], W_ROWS_IN, st_in, 0), (w_hbms[1], w_bs[1], W_ROWS_IN, st_in, 0),
                                     (w_hbms[2], w_bs[2], W_ROWS_DOWN, st_dn, slots)):
        assert w_b.shape[0] % rb == 0
        plan += [(w_hbm, w_b, r0, rb, st, sem0) for r0 in range(0, w_b.shape[0], rb)]
    uses = {0: 0, slots: 0}
    copies = []
    for w_hbm, w_b, r0, rb, st, sem0 in plan:
        slot = uses[sem0] % slots
        uses[sem0] += 1
        copies.append((pltpu.make_async_copy(w_hbm.at[layer, e, pl.ds(r0, rb), :], st.at[slot],
                                             sems.at[sem0 + slot]), st, slot, w_b, r0, rb))
    ahead = slots - 1
    for copy in copies[:ahead]:
        copy[0].start()
    for b, (copy, st, slot, w_b, r0, rb) in enumerate(copies):
        if b + ahead < len(copies):
            copies[b + ahead][0].start()
        copy.wait()
        w_b[r0:r0 + rb, :] = st[slot].astype(BF16)


def _swiglu_resident(x, wg_b, wu_b, wd_b):
    tf = wg_b.shape[1] // FF_SPLIT
    acc = None
    for f in range(FF_SPLIT):
        cols = slice(f * tf, (f + 1) * tf)
        a = _dot(x, wg_b[:, cols])
        u = _dot(x, wu_b[:, cols])
        act = (a * (1.0 / (1.0 + jnp.exp(-a)))) * u
        part = _dot(act.astype(BF16), wd_b[cols, :])
        acc = part if acc is None else acc + part
    return acc


def _weight_scratch(d, dff, slots):
    return [pltpu.VMEM((d, dff), BF16), pltpu.VMEM((d, dff), BF16), pltpu.VMEM((dff, d), BF16),
            pltpu.VMEM((slots, W_ROWS_IN, dff), F32), pltpu.VMEM((slots, W_ROWS_DOWN, d), F32),
            pltpu.SemaphoreType.DMA((2 * slots,))]


def _ffn_kernel(x1_ref, h_ref, mod_ref, wg_hbm, wu_hbm, wd_hbm, o_ref, wg_b, wu_b, wd_b, st_in, st_dn, sems, *, layer):
    @pl.when(pl.program_id(0) == 0)
    def _():
        _load_weights_bf16(layer, 0, (wg_hbm, wu_hbm, wd_hbm), (wg_b, wu_b, wd_b), st_in, st_dn, sems)

    o_ref[...] = x1_ref[...] + mod_ref[5:6, :] * _swiglu_resident(h_ref[...], wg_b, wu_b, wd_b)


def ffn_dense(x1, h, mod, w_gate, w_up, w_down, layer, seq, tm=512):
    t, d = x1.shape
    tiles_per_batch = seq // tm
    row = lambda i: (i, 0)
    hbm = pl.BlockSpec(memory_space=pl.ANY)
    return pl.pallas_call(
        functools.partial(_ffn_kernel, layer=layer),
        grid=(t // tm,),
        in_specs=[pl.BlockSpec((tm, d), row), pl.BlockSpec((tm, d), row),
                  pl.BlockSpec((None, 6, d), lambda i: (i // tiles_per_batch, 0, 0)), hbm, hbm, hbm],
        out_specs=pl.BlockSpec((tm, d), row),
        out_shape=jax.ShapeDtypeStruct((t, d), F32),
        scratch_shapes=_weight_scratch(d, w_gate.shape[3], 2),
        compiler_params=_params("arbitrary"),
        name="ffn_dense",
    )(x1, h, mod, w_gate, w_up, w_down)


MOE_TILE = 512
MOE_PARTS = 3
SC_CORES, SC_SUBCORES = 2, 16
SC_ROWS = 64
HI16 = -65536


def _pack_rows(x):
    c = x.shape[1] // 2
    bits = lax.bitcast_convert_type(x.astype(jnp.bfloat16).astype(F32), jnp.int32)
    return lax.shift_right_logical(bits[:, :c], jnp.int32(16)) | (bits[:, c:] & jnp.int32(HI16))


def _unpack_rows(w):
    lo = lax.bitcast_convert_type(lax.shift_left(w, jnp.int32(16)), F32)
    hi = lax.bitcast_convert_type(w & jnp.int32(HI16), F32)
    return jnp.concatenate([lo, hi], axis=1)


def sc_gather_rows(table, idx):
    d = table.shape[1]
    b = idx.shape[0]
    per_worker = b // (SC_CORES * SC_SUBCORES)
    assert per_worker * SC_CORES * SC_SUBCORES == b and per_worker % (2 * SC_ROWS) == 0
    mesh = plsc.VectorSubcoreMesh(core_axis_name="c", subcore_axis_name="s")
    idx_buf = pltpu.VMEM((SC_ROWS,), jnp.int32)
    row_buf = pltpu.VMEM((SC_ROWS, d), table.dtype)

    @functools.partial(
        pl.kernel, mesh=mesh, out_type=jax.ShapeDtypeStruct((b, d), table.dtype),
        scratch_types=[idx_buf, idx_buf, row_buf, row_buf] + [pltpu.SemaphoreType.DMA] * 4,
        name="sc_gather_rows")
    def gather(table_hbm, idx_hbm, out_hbm, idx0, idx1, rows0, rows1, sem_g0, sem_g1, sem_w0, sem_w1):
        base = (lax.axis_index("s") * SC_CORES + lax.axis_index("c")) * per_worker

        @pl.loop(0, per_worker // (2 * SC_ROWS))
        def _(pair):
            off0 = pl.multiple_of(base + pair * (2 * SC_ROWS), SC_ROWS)
            off1 = pl.multiple_of(off0 + SC_ROWS, SC_ROWS)
            pltpu.sync_copy(idx_hbm.at[pl.ds(off0, SC_ROWS)], idx0)
            pltpu.sync_copy(idx_hbm.at[pl.ds(off1, SC_ROWS)], idx1)
            gather0 = pltpu.async_copy(table_hbm.at[idx0], rows0, sem_g0)
            gather1 = pltpu.async_copy(table_hbm.at[idx1], rows1, sem_g1)
            gather0.wait()
            write0 = pltpu.async_copy(rows0, out_hbm.at[pl.ds(off0, SC_ROWS)], sem_w0)
            gather1.wait()
            write1 = pltpu.async_copy(rows1, out_hbm.at[pl.ds(off1, SC_ROWS)], sem_w1)
            write0.wait()
            write1.wait()

    return gather(table, idx)


def _moe_layout(e0, e1, t, n_exp):
    experts = jnp.arange(n_exp, dtype=jnp.int32)
    routed = ((e0[:, None] == experts) | (e1[:, None] == experts)).astype(jnp.int32)
    csum = jnp.cumsum(routed, axis=0)
    padded = (csum[-1] + MOE_TILE - 1) // MOE_TILE * MOE_TILE
    ends = jnp.cumsum(padded)
    pos = (ends - padded)[None, :] + csum - routed
    pos0 = jnp.take_along_axis(pos, e0[:, None], axis=1)[:, 0]
    pos1 = jnp.take_along_axis(pos, e1[:, None], axis=1)[:, 0]
    n_rows = 2 * t + n_exp * MOE_TILE
    tok = jnp.arange(t, dtype=jnp.int32)
    src = jnp.zeros((n_rows,), jnp.int32).at[jnp.concatenate([pos0, pos1])].set(
        jnp.concatenate([tok, tok]), unique_indices=True, mode="promise_in_bounds")
    n_tiles = n_rows // MOE_TILE
    n_valid = ends[-1] // MOE_TILE
    tile_expert = jnp.sum(jnp.arange(n_tiles)[:, None] * MOE_TILE >= ends[None, :], axis=1)
    tile_expert = tile_expert[jnp.minimum(jnp.arange(n_tiles), n_valid - 1)].astype(jnp.int32)
    return pos0, pos1, src, tile_expert, n_valid.astype(jnp.int32).reshape(1)


W_SLOTS = 4


def _experts_packed_kernel(te_ref, nv_ref, first_ref, x_ref, wg_hbm, wu_hbm, wd_hbm, *rest, layer):
    o_ref, wg_b, wu_b, wd_b, st_in, st_dn, sems = rest[-7:]
    j = pl.program_id(0)
    valid = j < nv_ref[0]

    @pl.when(valid & (first_ref[j] == 1))
    def _():
        _load_weights_bf16(layer, te_ref[j], (wg_hbm, wu_hbm, wd_hbm), (wg_b, wu_b, wd_b), st_in, st_dn, sems)

    @pl.when(valid)
    def _():
        o_ref[...] = _pack_rows(_swiglu_resident(_unpack_rows(x_ref[...]).astype(BF16), wg_b, wu_b, wd_b))


def moe_experts_packed(x_part, tile_expert, n_valid, w_gate, w_up, w_down, layer, y_prev, part, n_rows):
    rows_part, half = x_part.shape
    d = 2 * half
    dff = w_gate.shape[3]
    tiles_part = rows_part // MOE_TILE
    first = jnp.concatenate([jnp.ones((1,), jnp.int32),
                             (tile_expert[1:] != tile_expert[:-1]).astype(jnp.int32)])

    def tile(j, nv):
        return jnp.maximum(jnp.minimum(j, nv[0] - 1), 0)

    hbm = pl.BlockSpec(memory_space=pl.ANY)
    in_specs = [pl.BlockSpec((MOE_TILE, half), lambda j, te, nv, fi: (tile(j, nv), 0)), hbm, hbm, hbm]
    args = [tile_expert, n_valid, first, x_part, w_gate, w_up, w_down]
    aliases = {}
    if y_prev is not None:
        in_specs.append(hbm)
        args.append(y_prev)
        aliases = {len(args) - 1: 0}
    return pl.pallas_call(
        functools.partial(_experts_packed_kernel, layer=layer),
        grid_spec=pltpu.PrefetchScalarGridSpec(
            num_scalar_prefetch=3, grid=(tiles_part,), in_specs=in_specs,
            out_specs=pl.BlockSpec((MOE_TILE, half), lambda j, te, nv, fi: (part * tiles_part + tile(j, nv), 0)),
            scratch_shapes=_weight_scratch(d, dff, W_SLOTS)),
        out_shape=jax.ShapeDtypeStruct((n_rows, half), jnp.int32),
        input_output_aliases=aliases,
        compiler_params=_params("arbitrary"),
        name="moe_experts",
    )(*args)


def _combine_packed_kernel(y0_ref, y1_ref, route_ref, x1_ref, mod_ref, o_ref):
    rt = route_ref[...]
    f = rt[:, 2:3] * _unpack_rows(y0_ref[...]) + rt[:, 3:4] * _unpack_rows(y1_ref[...])
    o_ref[...] = x1_ref[...] + mod_ref[5:6, :] * f


def moe_combine_packed(y_pairs, route, x1, mod, seq, tm=512):
    t, d = x1.shape
    nt = t // tm
    tiles_per_batch = seq // tm
    row = lambda i: (i, 0)
    return pl.pallas_call(
        _combine_packed_kernel,
        grid=(nt,),
        in_specs=[pl.BlockSpec((tm, d // 2), row), pl.BlockSpec((tm, d // 2), lambda i: (i + nt, 0)),
                  pl.BlockSpec((tm, LANES), row), pl.BlockSpec((tm, d), row),
                  pl.BlockSpec((None, 6, d), lambda i: (i // tiles_per_batch, 0, 0))],
        out_specs=pl.BlockSpec((tm, d), row),
        out_shape=jax.ShapeDtypeStruct((t, d), F32),
        compiler_params=_params("parallel"),
        name="moe_combine",
    )(y_pairs, y_pairs, route, x1, mod)


def ffn_moe_sc(x1, h_packed, route, mod, w_gate, w_up, w_down, layer, seq):
    t = x1.shape[0]
    e0, e1 = route[:, 0].astype(jnp.int32), route[:, 1].astype(jnp.int32)
    pos0, pos1, src, tile_expert, n_valid = _moe_layout(e0, e1, t, w_gate.shape[1])
    n_rows = src.shape[0]
    rows_part = n_rows // MOE_PARTS
    tiles_part = rows_part // MOE_TILE
    assert rows_part * MOE_PARTS == n_rows and tiles_part * MOE_TILE == rows_part
    h_parts = [sc_gather_rows(h_packed, src[p * rows_part:(p + 1) * rows_part]) for p in range(MOE_PARTS)]
    y_sorted = None
    for p in range(MOE_PARTS):
        n_valid_part = jnp.clip(n_valid - p * tiles_part, 0, tiles_part)
        y_sorted = moe_experts_packed(h_parts[p], tile_expert[p * tiles_part:(p + 1) * tiles_part], n_valid_part,
                                      w_gate, w_up, w_down, layer, y_sorted, p, n_rows)
    y_pairs = sc_gather_rows(y_sorted, jnp.concatenate([pos0, pos1]))
    return moe_combine_packed(y_pairs, route, x1, mod, seq)


def _final_norm_kernel(x_ref, g_ref, o_ref):
    x = x_ref[...]
    o_ref[...] = (x * lax.rsqrt(jnp.mean(x * x, axis=-1, keepdims=True) + EPS)) * g_ref[...]


def final_norm(x, g, tm=512):
    t, d = x.shape
    return pl.pallas_call(
        _final_norm_kernel,
        grid=(t // tm,),
        in_specs=[pl.BlockSpec((tm, d), lambda i: (i, 0)), pl.BlockSpec((1, d), lambda i: (0, 0))],
        out_specs=pl.BlockSpec((tm, d), lambda i: (i, 0)),
        out_shape=jax.ShapeDtypeStruct((t, d), F32),
        compiler_params=_params("parallel"),
        name="final_norm",
    )(x, g)


def _rope_tables(positions, dim):
    rot = dim // ROPE_FRACTION
    half = rot // 2
    inv = 1.0 / (ROPE_THETA ** (np.arange(0, rot, 2, dtype=np.float32) / rot))
    ang = positions.reshape(-1).astype(F32)[:, None] * jnp.asarray(inv, F32)
    cos, sin = jnp.cos(ang), jnp.sin(ang)
    t = ang.shape[0]
    ones = jnp.ones((t, dim - rot), F32)
    zeros = lambda w: jnp.zeros((t, w), F32)
    reps = LANES // dim
    c = jnp.tile(jnp.concatenate([cos, cos, ones], axis=1), (1, reps))
    sa = jnp.tile(jnp.concatenate([-sin, zeros(dim - half)], axis=1), (1, reps))
    sb = jnp.tile(jnp.concatenate([zeros(half), sin, zeros(dim - rot)], axis=1), (1, reps))
    return c, sa, sb


def _relayout_w_in(w):
    pts = np.cumsum([0, 384, 64, 64, 256, 64, 4, 256, 256, 256, 384, 384, 384])
    (q_a, k_a, v_a, q_i, k_i, w_i, q_b, k_b, v_b, q_c, k_c, v_c) = [w[..., pts[j]:pts[j + 1]] for j in range(12)]
    w_i_pad = jnp.concatenate([w_i, jnp.zeros(w.shape[:-1] + (LANES - IDX_HEADS,), w.dtype)], axis=-1)
    return jnp.concatenate([q_a, k_a, k_a, v_a, v_a, q_i, k_i, k_i, w_i_pad,
                            q_b, k_b, v_b, q_c, k_c, v_c], axis=-1).astype(BF16)


def kernel(x, c, positions, w_in, w_out, diff_lambda, diff_subln, w_ada, b_ada, g_attn, g_ffn, w_ff_gate,
           w_ff_up, w_ff_down, w_router, w_exp_gate, w_exp_up, w_exp_down, g_final):
    batch, seq, d = x.shape
    depth = w_in.shape[0]
    t = batch * seq
    tabs = _rope_tables(positions, HEAD_DIM) + _rope_tables(positions, B_QK_DIM)
    mod_all = adaln_mod(c, w_ada, b_ada).reshape(depth, batch, 6, d)
    w_in_pad = _relayout_w_in(w_in)
    wo = w_out.astype(BF16)
    xf = x.reshape(t, d)
    for layer in range(depth):
        mod = mod_all[layer]
        lam_init = 0.8 - 0.6 * math.exp(-0.3 * layer)
        (qa, kka, vva, qi, kki, wi, qb, kb, vb, qc, kc, vc) = inproj(
            xf, g_attn[layer].reshape(1, d), mod, tabs, w_in_pad, layer, seq)
        ya = dsa_attention(qa, qi, wi, kka, vva, kki, batch, seq)
        g_sub2 = jnp.tile(diff_subln[layer], 2).reshape(1, LANES)
        yb = diff_attention(qb, kb, vb, diff_lambda[layer], g_sub2, lam_init, batch, seq)
        yc = moba_attention(qc, kc, vc, batch, seq)
        j = layer // 2
        gf = g_ffn[layer].reshape(1, d)
        if layer % 2 == 0:
            x1, h = outproj(xf, ya, yb, yc, wo, layer, gf, mod, seq)
            xf = ffn_dense(x1, h, mod, w_ff_gate[:, None], w_ff_up[:, None], w_ff_down[:, None], j, seq)
        else:
            wr = jnp.concatenate([w_router[j], jnp.zeros((d, LANES - N_EXPERTS), F32)], axis=1)
            wr_hi = wr.astype(BF16)
            wr = jnp.stack([wr_hi, (wr - wr_hi.astype(F32)).astype(BF16)])
            x1, h_packed, route = outproj(xf, ya, yb, yc, wo, layer, gf, mod, seq, w_router=wr)
            xf = ffn_moe_sc(x1, h_packed, route, mod, w_exp_gate, w_exp_up, w_exp_down, j, seq)
    return final_norm(xf, g_final.reshape(1, d)).reshape(batch, seq, d)
```

```python
import functools
import math

import jax
import jax.numpy as jnp
import numpy as np
from jax import lax
from jax.experimental import pallas as pl
from jax.experimental.pallas import tpu as pltpu
from jax.experimental.pallas import tpu_sc as plsc

F32 = jnp.float32
BF16 = jnp.bfloat16

HEAD_DIM = 64
A_HEADS = 6
IDX_HEADS = 4
B_HEADS = 4
B_QK_DIM = 32
C_HEADS = 6
A_WIDTH, IDX_WIDTH = A_HEADS * HEAD_DIM, IDX_HEADS * HEAD_DIM
B_WIDTH, C_WIDTH = B_HEADS * HEAD_DIM, C_HEADS * HEAD_DIM
DSA_TOPK_MAX = 256
MOBA_BLOCK = 256
MOBA_TOPK = 3
ROPE_THETA = 500000.0
ROPE_FRACTION = 4
SUBLN_EPS = 1e-5
EPS = 1e-6
N_EXPERTS = 8

LANES = 128
NEG = -1e30
INT_MIN = -2 ** 31
MIN_NORMAL_KEY = 0x00800000
MIN_NORMAL_F32 = float(np.float32(2.0 ** -126))
VMEM_LIMIT = 48 * 1024 * 1024
LOG2E = math.log2(math.e)

_G_QA, _G_KKA, _G_VVA, _G_QI, _G_KKI, _G_WI = (0, 384), (384, 512), (512, 640), (640, 896), (896, 1024), (1024, 1152)
_G_QB, _G_KB, _G_VB = (1152, 1408), (1408, 1664), (1664, 1920)
_G_QC, _G_KC, _G_VC = (1920, 2304), (2304, 2688), (2688, 3072)
D_IN_PAD = 3072


def _params(*sem):
    return pltpu.CompilerParams(dimension_semantics=sem, vmem_limit_bytes=VMEM_LIMIT)


def _dot(a, b):
    return jnp.dot(a, b, preferred_element_type=F32)


def _dot_nt(a, b):
    return lax.dot_general(a, b, (((1,), (1,)), ((), ())), preferred_element_type=F32)


def _adaln_kernel(c_ref, w_ref, b_ref, o_ref):
    c = c_ref[...]
    c_act = c * (1.0 / (1.0 + jnp.exp(-c)))
    o_ref[...] = jnp.dot(c_act, w_ref[...], preferred_element_type=F32,
                         precision=lax.Precision.HIGHEST) + b_ref[...]


def adaln_mod(c, w_ada, b_ada, tn=1536):
    depth, d, n = w_ada.shape
    b = c.shape[0]
    return pl.pallas_call(
        _adaln_kernel,
        grid=(depth, n // tn),
        in_specs=[pl.BlockSpec((b, d), lambda l, j: (0, 0)),
                  pl.BlockSpec((None, d, tn), lambda l, j: (l, 0, j)),
                  pl.BlockSpec((None, 1, tn), lambda l, j: (l, 0, j))],
        out_specs=pl.BlockSpec((None, b, tn), lambda l, j: (l, 0, j)),
        out_shape=jax.ShapeDtypeStruct((depth, b, n), F32),
        compiler_params=_params("parallel", "parallel"),
        name="adaln_mod",
    )(c, w_ada, b_ada.reshape(depth, 1, n))


def _norm_mod(x, g, shift, scale, eps=EPS):
    y = x * lax.rsqrt(jnp.mean(x * x, axis=-1, keepdims=True) + eps)
    return (y * g) * (1.0 + scale) + shift


def _rope_store(acc, o_ref, cos, sa, sb, half):
    for j in range(acc.shape[1] // LANES):
        a = acc[:, j * LANES:(j + 1) * LANES]
        r = a * cos + pltpu.roll(a, half, 1) * sb + pltpu.roll(a, LANES - half, 1) * sa
        o_ref[:, j * LANES:(j + 1) * LANES] = r.astype(o_ref.dtype)


def _inproj_kernel(x_ref, g_ref, mod_ref, c64_ref, sa64_ref, sb64_ref, c32_ref, sa32_ref, sb32_ref, w_ref,
                   qa_ref, kka_ref, vva_ref, qi_ref, kki_ref, wi_ref,
                   qb_ref, kb_ref, vb_ref, qc_ref, kc_ref, vc_ref):
    h = _norm_mod(x_ref[...], g_ref[...], mod_ref[0:1, :], mod_ref[1:2, :]).astype(BF16)
    c64, sa64, sb64 = c64_ref[...], sa64_ref[...], sb64_ref[...]
    c32, sa32, sb32 = c32_ref[...], sa32_ref[...], sb32_ref[...]

    def proj(cols):
        return _dot(h, w_ref[:, cols[0]:cols[1]])

    qk_scale = HEAD_DIM ** -0.5 * LOG2E
    _rope_store(proj(_G_QA), qa_ref, c64 * qk_scale, sa64 * qk_scale, sb64 * qk_scale, 8)
    _rope_store(proj(_G_KKA), kka_ref, c64, sa64, sb64, 8)
    vva_ref[...] = proj(_G_VVA).astype(vva_ref.dtype)
    _rope_store(proj(_G_QI), qi_ref, c64, sa64, sb64, 8)
    _rope_store(proj(_G_KKI), kki_ref, c64, sa64, sb64, 8)
    wi_ref[...] = proj(_G_WI) * (IDX_HEADS ** -0.5 * HEAD_DIM ** -0.5)
    b_scale = B_QK_DIM ** -0.5 * LOG2E
    _rope_store(proj(_G_QB), qb_ref, c32 * b_scale, sa32 * b_scale, sb32 * b_scale, 4)
    _rope_store(proj(_G_KB), kb_ref, c32, sa32, sb32, 4)
    vb_ref[...] = proj(_G_VB).astype(vb_ref.dtype)
    _rope_store(proj(_G_QC), qc_ref, c64 * qk_scale, sa64 * qk_scale, sb64 * qk_scale, 8)
    _rope_store(proj(_G_KC), kc_ref, c64, sa64, sb64, 8)
    vc_ref[...] = proj(_G_VC).astype(vc_ref.dtype)


def inproj(x, g, mod, tabs, w_pad, layer, seq, tm=512):
    t, d = x.shape
    tiles_per_batch = seq // tm
    row = lambda i: (i, 0)
    widths = [A_WIDTH, LANES, LANES, IDX_WIDTH, LANES, LANES, B_WIDTH, B_WIDTH, B_WIDTH, C_WIDTH, C_WIDTH, C_WIDTH]
    dtypes = [BF16, BF16, BF16, BF16, BF16, F32, BF16, BF16, BF16, BF16, BF16, BF16]
    return pl.pallas_call(
        _inproj_kernel,
        grid=(t // tm,),
        in_specs=[pl.BlockSpec((tm, d), row),
                  pl.BlockSpec((1, d), lambda i: (0, 0)),
                  pl.BlockSpec((None, 6, d), lambda i: (i // tiles_per_batch, 0, 0))]
                 + [pl.BlockSpec((tm, LANES), row)] * 6
                 + [pl.BlockSpec((None, d, D_IN_PAD), lambda i: (layer, 0, 0))],
        out_specs=[pl.BlockSpec((tm, w), row) for w in widths],
        out_shape=[jax.ShapeDtypeStruct((t, w), dt) for w, dt in zip(widths, dtypes)],
        compiler_params=_params("parallel"),
        name="inproj",
    )(x, g, mod, *tabs, w_pad)


def _init_stats(m_ref, l_ref, acc_ref):
    m_ref[...] = jnp.full(m_ref.shape, -jnp.inf, F32)
    l_ref[...] = jnp.zeros(l_ref.shape, F32)
    acc_ref[...] = jnp.zeros(acc_ref.shape, F32)


def _key_to_f32(k):
    return lax.bitcast_convert_type(jnp.where(k >= 0, k, k ^ 0x7FFFFFFF), F32)


def dsa_attention(qa, qi, wi, kka, vva, kki, batch, seq, tq=256):
    t = qa.shape[0]
    nq = seq // tq
    k_top = min(DSA_TOPK_MAX, seq // 4)
    qrow = lambda b, i: (b * nq + i, 0)
    full = lambda b, i: (b, 0)
    return pl.pallas_call(
        functools.partial(_dsa_kernel_t, tq=tq, k_top=k_top, seq=seq),
        grid=(batch, nq),
        in_specs=[pl.BlockSpec((tq, A_WIDTH), qrow), pl.BlockSpec((tq, IDX_WIDTH), qrow),
                  pl.BlockSpec((tq, LANES), qrow),
                  pl.BlockSpec((seq, LANES), full), pl.BlockSpec((seq, LANES), full),
                  pl.BlockSpec((seq, LANES), full)],
        out_specs=pl.BlockSpec((tq, A_WIDTH), qrow),
        out_shape=jax.ShapeDtypeStruct((t, A_WIDTH), BF16),
        scratch_shapes=[pltpu.VMEM((seq, tq), F32), pltpu.VMEM((seq, tq), jnp.int16),
                        pltpu.VMEM((seq, tq), jnp.int16), pltpu.VMEM((1, LANES, seq), BF16),
                        pltpu.VMEM((3, 2 * tq, LANES), BF16),
                        pltpu.VMEM((3, 1, 2 * tq), F32), pltpu.VMEM((3, 1, 2 * tq), F32),
                        pltpu.VMEM((3, LANES, 2 * tq), F32)],
        compiler_params=_params("parallel", "arbitrary"),
        name="dsa_attention",
    )(qa, qi, wi, kka, vva, kki)


def diff_attention(qb, kb, vb, lam_vec, g_sub2, lam_init, batch, seq, tq=256):
    t = qb.shape[0]
    nq = seq // tq
    qrow = lambda b, i: (b * nq + i, 0)
    full = lambda b, i: (b, 0)
    const = lambda b, i: (0, 0)
    return pl.pallas_call(
        functools.partial(_diff_kernel_t, tq=tq, lam_init=lam_init, seq=seq),
        grid=(batch, nq),
        in_specs=[pl.BlockSpec((tq, B_WIDTH), qrow), pl.BlockSpec((seq, B_WIDTH), full),
                  pl.BlockSpec((seq, B_WIDTH), full),
                  pl.BlockSpec((4, B_QK_DIM), const), pl.BlockSpec((1, LANES), const)],
        out_specs=pl.BlockSpec((tq, B_WIDTH), qrow),
        out_shape=jax.ShapeDtypeStruct((t, B_WIDTH), BF16),
        scratch_shapes=[pltpu.VMEM((2, LANES, seq), BF16), pltpu.VMEM((2, 4 * tq, LANES), BF16),
                        pltpu.VMEM((2, 1, 4 * tq), F32), pltpu.VMEM((2, 1, 4 * tq), F32),
                        pltpu.VMEM((2, LANES, 4 * tq), F32)],
        compiler_params=_params("parallel", "arbitrary"),
        name="diff_attention",
    )(qb, kb, vb, lam_vec, g_sub2)


def moba_attention(qc, kc, vc, batch, seq):
    tq = MOBA_BLOCK
    t = qc.shape[0]
    nb = seq // tq
    n_sel = min(MOBA_TOPK, nb - 1)
    nbp = 8
    assert seq % tq == 0 and nb <= nbp
    qrow = lambda b, i: (b * nb + i, 0)
    full = lambda b, i: (b, 0)
    return pl.pallas_call(
        functools.partial(_moba_kernel_t, tq=tq, nb=nb, n_sel=n_sel, seq=seq),
        grid=(batch, nb),
        in_specs=[pl.BlockSpec((tq, C_WIDTH), qrow), pl.BlockSpec((seq, C_WIDTH), full),
                  pl.BlockSpec((seq, C_WIDTH), full)],
        out_specs=pl.BlockSpec((tq, C_WIDTH), qrow),
        out_shape=jax.ShapeDtypeStruct((t, C_WIDTH), BF16),
        scratch_shapes=[pltpu.VMEM((nbp, C_WIDTH), F32), pltpu.VMEM((3, LANES, seq), BF16),
                        pltpu.VMEM((3, 2 * tq, LANES), BF16), pltpu.VMEM((3, nbp, 2 * tq), F32),
                        pltpu.VMEM((3, 1, 2 * tq), F32), pltpu.VMEM((3, 1, 2 * tq), F32),
                        pltpu.VMEM((3, LANES, 2 * tq), F32)],
        compiler_params=_params("parallel", "arbitrary"),
        name="moba_attention",
    )(qc, kc, vc)


def _attend_t(k, q_ref, g, v_t, m_ref, l_ref, acc_ref, mask=None):
    for j in range(q_ref.shape[1] // LANES):
        cols = slice(j * LANES, (j + 1) * LANES)
        s = _dot_nt(k, q_ref[g, cols, :])
        if mask is not None:
            s = mask(j, s)
        m_prev = m_ref[g, :, cols]
        m_new = jnp.maximum(m_prev, jnp.max(s, axis=0, keepdims=True))
        alpha = jnp.exp2(m_prev - m_new)
        p = jnp.exp2(s - m_new)
        l_ref[g, :, cols] = alpha * l_ref[g, :, cols] + jnp.sum(p, axis=0, keepdims=True)
        acc_ref[g, :, cols] = alpha * acc_ref[g, :, cols] + _dot(v_t, p.astype(BF16))
        m_ref[g, :, cols] = m_new


def _causal_mask_t(tk, tq):
    key = lax.broadcasted_iota(jnp.int32, (tk, LANES), 0)
    qry = lax.broadcasted_iota(jnp.int32, (tk, LANES), 1)

    def mask(j, s):
        return jnp.where(key <= qry + (j * LANES) % tq, s, NEG)
    return mask


def _transpose_values(v_ref, vt_ref, seq, tk):
    for g in range(vt_ref.shape[0]):
        for n in range(seq // tk):
            blk = v_ref[n * tk:(n + 1) * tk, g * LANES:(g + 1) * LANES].astype(F32)
            vt_ref[g, :, n * tk:(n + 1) * tk] = blk.T.astype(vt_ref.dtype)


def _head_pair_out(acc_t, l, tq):
    even = acc_t[0:HEAD_DIM, 0:tq] / l[:, 0:tq]
    odd = acc_t[HEAD_DIM:LANES, tq:2 * tq] / l[:, tq:2 * tq]
    return jnp.concatenate([even, odd], axis=0).T


def _dsa_kernel_t(qa_ref, qi_ref, wiq_ref, kka_ref, vva_ref, kki_ref, o_ref,
                  sc_ref, hi_ref, lo_ref, vt_ref, qs_ref, m_ref, l_ref, acc_ref, *, tq, k_top, seq):
    i = pl.program_id(1)
    nk = i + 1
    t0 = i * tq
    lo = lax.broadcasted_iota(jnp.int32, (tq, LANES), 1) < HEAD_DIM
    zero_b = jnp.zeros((tq, LANES), BF16)

    @pl.when(i == 0)
    def _():
        _transpose_values(vva_ref, vt_ref, seq, tq)

    def stack_heads(q):
        out = []
        for g in range(q.shape[1] // LANES):
            qg = q[:, g * LANES:(g + 1) * LANES]
            out += [jnp.where(lo, qg, zero_b), jnp.where(lo, zero_b, qg)]
        return out

    qa_stack = stack_heads(qa_ref[...])
    for g in range(3):
        qs_ref[g, 0:tq, :] = qa_stack[2 * g]
        qs_ref[g, tq:2 * tq, :] = qa_stack[2 * g + 1]
    qi_stack = jnp.concatenate(stack_heads(qi_ref[...]), axis=0)
    wi_t = wiq_ref[...].T

    def rows(c):
        return pl.ds(pl.multiple_of(c * tq, tq), tq)

    key_pos = lax.broadcasted_iota(jnp.int32, (tq, tq), 0)
    qry_pos = lax.broadcasted_iota(jnp.int32, (tq, tq), 1)

    def idx_body(c, carry):
        r = jnp.maximum(_dot_nt(kki_ref[rows(c), :], qi_stack), 0.0)
        s = wi_t[0:1, :] * r[:, 0:tq]
        for h in range(1, IDX_HEADS):
            s = s + wi_t[h:h + 1, :] * r[:, h * tq:(h + 1) * tq]
        causal = (c * tq + key_pos) <= (t0 + qry_pos)
        s = jnp.where(causal, s, -jnp.inf)
        s = jnp.where(jnp.abs(s) < MIN_NORMAL_F32, 0.0, s)
        sc_ref[rows(c), :] = s
        bits = lax.bitcast_convert_type(s, jnp.int32)
        key = jnp.where(bits >= 0, bits, bits ^ 0x7FFFFFFF)
        hi_ref[rows(c), :] = lax.shift_right_arithmetic(key, jnp.int32(16)).astype(jnp.int16)
        lo_ref[rows(c), :] = ((key & 0xFFFF) - 32768).astype(jnp.int16)
        return carry

    lax.fori_loop(0, nk, idx_body, 0)

    def count16(ref, cand, strict):
        c16 = cand.astype(jnp.int16)

        def body(c, acc):
            x = ref[rows(c), :]
            hit = jnp.where((x > c16) if strict else (x >= c16), jnp.int16(1), jnp.int16(0))
            for r in range(tq // 16):
                acc = acc + hit[r * 16:(r + 1) * 16]
            return acc
        acc = lax.fori_loop(0, nk, body, jnp.zeros((16, tq), jnp.int16))
        return jnp.sum(acc.astype(F32), axis=0, keepdims=True)

    def search16(ref, k_need):
        v0 = jnp.where(count16(ref, jnp.zeros((1, tq), jnp.int32), False) >= k_need, 0, -32768).astype(jnp.int32)

        def bisect(b, v):
            trial = v | lax.shift_left(jnp.int32(1), 14 - b)
            return jnp.where(count16(ref, trial, False) >= k_need, trial, v)
        return lax.fori_loop(0, 15, bisect, v0)

    kf = float(k_top)

    def search():
        hi_k = search16(hi_ref, kf)
        above = count16(hi_ref, hi_k, True)
        hi16 = hi_k.astype(jnp.int16)

        def keep_bucket(c, carry):
            lo_ref[rows(c), :] = jnp.where(hi_ref[rows(c), :] == hi16, lo_ref[rows(c), :], jnp.int16(-32768))
            return carry
        lax.fori_loop(0, nk, keep_bucket, 0)
        lo_k = search16(lo_ref, kf - above)
        n_gt = above + count16(lo_ref, lo_k, True)
        return lax.shift_left(hi_k, jnp.int32(16)) | (lo_k + 32768), kf - n_gt

    key, need = lax.cond(t0 + tq <= k_top,
                         lambda: (jnp.full((1, tq), INT_MIN, jnp.int32), jnp.full((1, tq), kf, F32)), search)
    thr = _key_to_f32(key)
    thr_next = _key_to_f32(jnp.where(key == 0, MIN_NORMAL_KEY, key + 1))
    all_sel = (t0 + lax.broadcasted_iota(jnp.int32, (1, tq), 1)) < k_top
    lower = (qry_pos <= key_pos).astype(BF16)

    _init_stats(m_ref, l_ref, acc_ref)

    def att_body(c, tie_run):
        s_idx = sc_ref[rows(c), :]
        ge = s_idx >= thr
        gt = s_idx >= thr_next
        tie = jnp.where(gt, 0.0, jnp.where(ge, 1.0, 0.0))
        prefix = _dot(lower, tie.astype(BF16)) + tie_run
        take = jnp.where(gt, 1.0, jnp.where(prefix <= need, tie, 0.0))
        take = jnp.where(all_sel, 1.0, take)
        take = jnp.where(s_idx > -jnp.inf, take, 0.0)

        def mask(j, s):
            q0 = (j * LANES) % tq
            return jnp.where(take[:, q0:q0 + LANES] > 0.5, s, NEG)

        kk = kka_ref[rows(c), :]
        vt = vt_ref[0, :, rows(c)]
        for g in range(3):
            _attend_t(kk, qs_ref, g, vt, m_ref, l_ref, acc_ref, mask)
        return tie_run + jnp.sum(tie, axis=0, keepdims=True)

    lax.fori_loop(0, nk, att_body, jnp.zeros((1, tq), F32))

    for g in range(3):
        o_ref[:, g * LANES:(g + 1) * LANES] = _head_pair_out(acc_ref[g], l_ref[g], tq).astype(o_ref.dtype)


def _diff_kernel_t(qb_ref, kb_ref, vb_ref, lam_ref, gsub_ref, o_ref,
                   vt_ref, qs_ref, m_ref, l_ref, acc_ref, *, tq, lam_init, seq):
    i = pl.program_id(1)
    lane = lax.broadcasted_iota(jnp.int32, (tq, LANES), 1)
    zero_b = jnp.zeros((tq, LANES), BF16)

    @pl.when(i == 0)
    def _():
        _transpose_values(vb_ref, vt_ref, seq, tq)

    qb = qb_ref[...]
    for g in range(2):
        qg = qb[:, g * LANES:(g + 1) * LANES]
        for j in range(4):
            qs_ref[g, j * tq:(j + 1) * tq, :] = jnp.where(lane // B_QK_DIM == j, qg, zero_b)
    _init_stats(m_ref, l_ref, acc_ref)

    def rows(c):
        return pl.ds(pl.multiple_of(c * tq, tq), tq)

    def step(c, masked):
        mask = _causal_mask_t(tq, tq) if masked else None
        for g in range(2):
            _attend_t(kb_ref[rows(c), g * LANES:(g + 1) * LANES], qs_ref, g, vt_ref[g, :, rows(c)],
                      m_ref, l_ref, acc_ref, mask)

    def body(c, carry):
        step(c, False)
        return carry

    lax.fori_loop(0, i, body, 0)
    step(i, True)

    lv = lam_ref[...]
    lam = (jnp.exp(jnp.sum(lv[0:1] * lv[1:2], axis=1, keepdims=True))
           - jnp.exp(jnp.sum(lv[2:3] * lv[3:4], axis=1, keepdims=True)) + lam_init)
    gsub = gsub_ref[...]
    for g in range(2):
        acc_t, l = acc_ref[g], l_ref[g]

        def prob(r0, j):
            return acc_t[r0:r0 + HEAD_DIM, j * tq:(j + 1) * tq] / l[:, j * tq:(j + 1) * tq]

        halves = []
        for r0, j in ((0, 0), (HEAD_DIM, 2)):
            o = prob(r0, j) - lam * prob(r0, j + 1)
            ms = jnp.mean(o * o, axis=0, keepdims=True)
            halves.append(o * lax.rsqrt(ms + SUBLN_EPS))
        y = (jnp.concatenate(halves, axis=0).T * gsub) * (1.0 - lam_init)
        o_ref[:, g * LANES:(g + 1) * LANES] = y.astype(o_ref.dtype)


def _moba_kernel_t(qc_ref, kc_ref, vc_ref, o_ref, kmean_ref, vt_ref, qs_ref, bias_ref, m_ref, l_ref, acc_ref,
                   *, tq, nb, n_sel, seq):
    i = pl.program_id(1)
    nbp = kmean_ref.shape[0]
    lane = lax.broadcasted_iota(jnp.int32, (tq, LANES), 1)
    lo = lane < HEAD_DIM
    zero_b = jnp.zeros((tq, LANES), BF16)

    @pl.when(i == 0)
    def _():
        _transpose_values(vc_ref, vt_ref, seq, tq)
        kmean_ref[...] = jnp.zeros(kmean_ref.shape, F32)
        for n in range(nb):
            kblk = kc_ref[n * tq:(n + 1) * tq, :].astype(F32)
            kmean_ref[n:n + 1, :] = jnp.mean(kblk, axis=0, keepdims=True)

    qc = qc_ref[...]
    sub = lax.broadcasted_iota(jnp.int32, (nbp, 2 * tq), 0)
    past = sub < i
    for g in range(3):
        qg = qc[:, g * LANES:(g + 1) * LANES]
        q2 = jnp.concatenate([jnp.where(lo, qg, zero_b), jnp.where(lo, zero_b, qg)], axis=0)
        km = kmean_ref[:, g * LANES:(g + 1) * LANES]
        km_hi = km.astype(BF16)
        gt = _dot_nt(km_hi, q2) + _dot_nt((km - km_hi.astype(F32)).astype(BF16), q2)
        rows_ = []
        for n in range(nbp):
            gn = gt[n:n + 1, :]
            beats = jnp.where(sub < n, jnp.where(gt >= gn, 1.0, 0.0), jnp.where(gt > gn, 1.0, 0.0))
            beats = jnp.where(sub == n, 0.0, jnp.where(past, beats, 0.0))
            rank = jnp.sum(beats, axis=0, keepdims=True)
            rows_.append(jnp.where(rank < n_sel, 0.0, NEG))
        bias_ref[g] = jnp.where(past, jnp.concatenate(rows_, axis=0), NEG)
        qs_ref[g] = q2
    _init_stats(m_ref, l_ref, acc_ref)

    def rows(c):
        return pl.ds(pl.multiple_of(c * tq, tq), tq)

    def body(c, carry):
        for g in range(3):
            bias = bias_ref[g, pl.ds(c, 1), :]

            def mask(j, s):
                return s + bias[:, j * LANES:(j + 1) * LANES]

            _attend_t(kc_ref[rows(c), g * LANES:(g + 1) * LANES], qs_ref, g, vt_ref[g, :, rows(c)],
                      m_ref, l_ref, acc_ref, mask)
        return carry

    lax.fori_loop(0, i, body, 0)

    causal = _causal_mask_t(tq, tq)
    for g in range(3):
        _attend_t(kc_ref[rows(i), g * LANES:(g + 1) * LANES], qs_ref, g, vt_ref[g, :, rows(i)],
                  m_ref, l_ref, acc_ref, causal)
        o_ref[:, g * LANES:(g + 1) * LANES] = _head_pair_out(acc_ref[g], l_ref[g], tq).astype(o_ref.dtype)


def _route(h, wr_ref, cw_ref):
    hi = h.astype(BF16)
    lo = (h - hi.astype(F32)).astype(BF16)
    logits = _dot(hi, wr_ref[0]) + (_dot(lo, wr_ref[0]) + _dot(hi, wr_ref[1]))
    lane = lax.broadcasted_iota(jnp.int32, logits.shape, 1)
    lg = jnp.where(lane < N_EXPERTS, logits, -jnp.inf)
    v0 = jnp.max(lg, axis=1, keepdims=True)
    i0 = jnp.min(jnp.where(lg == v0, lane, LANES), axis=1, keepdims=True)
    lg1 = jnp.where(lane == i0, -jnp.inf, lg)
    v1 = jnp.max(lg1, axis=1, keepdims=True)
    i1 = jnp.min(jnp.where(lg1 == v1, lane, LANES), axis=1, keepdims=True)
    e1 = jnp.exp(v1 - v0)
    w0 = 1.0 / (1.0 + e1)
    cw_ref[...] = (jnp.where(lane == 0, i0.astype(F32), 0.0) + jnp.where(lane == 1, i1.astype(F32), 0.0)
                   + jnp.where(lane == 2, w0, 0.0) + jnp.where(lane == 3, e1 * w0, 0.0))


def _mixer_out(x_ref, ya_ref, yb_ref, yc_ref, wo_ref, g_ref, mod_ref):
    ab = A_WIDTH + B_WIDTH
    y = (_dot(ya_ref[...], wo_ref[0:A_WIDTH, :]) + _dot(yb_ref[...], wo_ref[A_WIDTH:ab, :])
         + _dot(yc_ref[...], wo_ref[ab:ab + C_WIDTH, :]))
    x1 = x_ref[...] + mod_ref[2:3, :] * y
    return x1, _norm_mod(x1, g_ref[...], mod_ref[3:4, :], mod_ref[4:5, :])


def _mixer_out_specs(tm, d, layer, tiles_per_batch):
    row = lambda i: (i, 0)
    return [pl.BlockSpec((tm, d), row), pl.BlockSpec((tm, A_WIDTH), row), pl.BlockSpec((tm, B_WIDTH), row),
            pl.BlockSpec((tm, C_WIDTH), row), pl.BlockSpec((None, d, d), lambda i: (layer, 0, 0)),
            pl.BlockSpec((1, d), lambda i: (0, 0)),
            pl.BlockSpec((None, 6, d), lambda i: (i // tiles_per_batch, 0, 0))]


def _outproj_kernel(x_ref, ya_ref, yb_ref, yc_ref, wo_ref, g_ref, mod_ref, wr_ref, x1_ref, h_ref, cw_ref):
    x1, h = _mixer_out(x_ref, ya_ref, yb_ref, yc_ref, wo_ref, g_ref, mod_ref)
    x1_ref[...] = x1
    h_ref[...] = _pack_rows(h)
    _route(h, wr_ref, cw_ref)


def outproj_router(x, ya, yb, yc, wo, layer, g_ffn, mod, seq, w_router, tm=512):
    t, d = x.shape
    row = lambda i: (i, 0)
    return pl.pallas_call(
        _outproj_kernel,
        grid=(t // tm,),
        in_specs=_mixer_out_specs(tm, d, layer, seq // tm) + [pl.BlockSpec((2, d, LANES), lambda i: (0, 0, 0))],
        out_specs=[pl.BlockSpec((tm, d), row), pl.BlockSpec((tm, d // 2), row), pl.BlockSpec((tm, LANES), row)],
        out_shape=[jax.ShapeDtypeStruct((t, d), F32), jax.ShapeDtypeStruct((t, d // 2), jnp.int32),
                   jax.ShapeDtypeStruct((t, LANES), F32)],
        compiler_params=_params("parallel"),
        name="outproj_router",
    )(x, ya, yb, yc, wo, g_ffn, mod, w_router)


W_ROWS_IN = 128
W_ROWS_DOWN = 352
FF_SPLIT = 2


def _load_weights_bf16(layer, e, w_hbms, w_bs, st_in, st_dn, sems):
    slots = st_in.shape[0]
    plan = []
    for w_hbm, w_b, rb, st, sem0 in ((w_hbms[0], w_bs[0], W_ROWS_IN, st_in, 0), (w_hbms[1], w_bs[1], W_ROWS_IN, st_in, 0),
                                     (w_hbms[2], w_bs[2], W_ROWS_DOWN, st_dn, slots)):
        assert w_b.shape[0] % rb == 0
        plan += [(w_hbm, w_b, r0, rb, st, sem0) for r0 in range(0, w_b.shape[0], rb)]
    uses = {0: 0, slots: 0}
    copies = []
    for w_hbm, w_b, r0, rb, st, sem0 in plan:
        slot = uses[sem0] % slots
        uses[sem0] += 1
        copies.append((pltpu.make_async_copy(w_hbm.at[layer, e, pl.ds(r0, rb), :], st.at[slot],
                                             sems.at[sem0 + slot]), st, slot, w_b, r0, rb))
    ahead = slots - 1
    for copy in copies[:ahead]:
        copy[0].start()
    for b, (copy, st, slot, w_b, r0, rb) in enumerate(copies):
        if b + ahead < len(copies):
            copies[b + ahead][0].start()
        copy.wait()
        w_b[r0:r0 + rb, :] = st[slot].astype(BF16)


def _swiglu_resident(x, wg_b, wu_b, wd_b):
    tf = wg_b.shape[1] // FF_SPLIT
    acc = None
    for f in range(FF_SPLIT):
        cols = slice(f * tf, (f + 1) * tf)
        a = _dot(x, wg_b[:, cols])
        u = _dot(x, wu_b[:, cols])
        act = (a * (1.0 / (1.0 + jnp.exp(-a)))) * u
        part = _dot(act.astype(BF16), wd_b[cols, :])
        acc = part if acc is None else acc + part
    return acc


def _weight_scratch(d, dff, slots):
    return [pltpu.VMEM((d, dff), BF16), pltpu.VMEM((d, dff), BF16), pltpu.VMEM((dff, d), BF16),
            pltpu.VMEM((slots, W_ROWS_IN, dff), F32), pltpu.VMEM((slots, W_ROWS_DOWN, d), F32),
            pltpu.SemaphoreType.DMA((2 * slots,))]


def _ffn_kernel(x_ref, ya_ref, yb_ref, yc_ref, wo_ref, g_ref, mod_ref, wg_hbm, wu_hbm, wd_hbm, o_ref,
                wg_b, wu_b, wd_b, st_in, st_dn, sems, *, layer):
    @pl.when(pl.program_id(0) == 0)
    def _():
        _load_weights_bf16(layer, 0, (wg_hbm, wu_hbm, wd_hbm), (wg_b, wu_b, wd_b), st_in, st_dn, sems)

    x1, h = _mixer_out(x_ref, ya_ref, yb_ref, yc_ref, wo_ref, g_ref, mod_ref)
    o_ref[...] = x1 + mod_ref[5:6, :] * _swiglu_resident(h.astype(BF16), wg_b, wu_b, wd_b)


def outproj_ffn_dense(x, ya, yb, yc, wo, layer, g_ffn, mod, w_gate, w_up, w_down, ff_layer, seq, tm=512):
    t, d = x.shape
    hbm = pl.BlockSpec(memory_space=pl.ANY)
    return pl.pallas_call(
        functools.partial(_ffn_kernel, layer=ff_layer),
        grid=(t // tm,),
        in_specs=_mixer_out_specs(tm, d, layer, seq // tm) + [hbm, hbm, hbm],
        out_specs=pl.BlockSpec((tm, d), lambda i: (i, 0)),
        out_shape=jax.ShapeDtypeStruct((t, d), F32),
        scratch_shapes=_weight_scratch(d, w_gate.shape[3], 2),
        compiler_params=_params("arbitrary"),
        name="ffn_dense",
    )(x, ya, yb, yc, wo, g_ffn, mod, w_gate, w_up, w_down)


MOE_TILE = 512
MOE_PARTS = 3
SC_CORES, SC_SUBCORES = 2, 16
SC_ROWS = 64
HI16 = -65536


def _pack_rows(x):
    c = x.shape[1] // 2
    bits = lax.bitcast_convert_type(x.astype(jnp.bfloat16).astype(F32), jnp.int32)
    return lax.shift_right_logical(bits[:, :c], jnp.int32(16)) | (bits[:, c:] & jnp.int32(HI16))


def _unpack_rows(w):
    lo = lax.bitcast_convert_type(lax.shift_left(w, jnp.int32(16)), F32)
    hi = lax.bitcast_convert_type(w & jnp.int32(HI16), F32)
    return jnp.concatenate([lo, hi], axis=1)


def sc_gather_rows(table, idx):
    d = table.shape[1]
    b = idx.shape[0]
    per_worker = b // (SC_CORES * SC_SUBCORES)
    assert per_worker * SC_CORES * SC_SUBCORES == b and per_worker % (2 * SC_ROWS) == 0
    mesh = plsc.VectorSubcoreMesh(core_axis_name="c", subcore_axis_name="s")
    idx_buf = pltpu.VMEM((SC_ROWS,), jnp.int32)
    row_buf = pltpu.VMEM((SC_ROWS, d), table.dtype)

    @functools.partial(
        pl.kernel, mesh=mesh, out_type=jax.ShapeDtypeStruct((b, d), table.dtype),
        scratch_types=[idx_buf, idx_buf, row_buf, row_buf] + [pltpu.SemaphoreType.DMA] * 4,
        name="sc_gather_rows")
    def gather(table_hbm, idx_hbm, out_hbm, idx0, idx1, rows0, rows1, sem_g0, sem_g1, sem_w0, sem_w1):
        base = (lax.axis_index("s") * SC_CORES + lax.axis_index("c")) * per_worker

        @pl.loop(0, per_worker // (2 * SC_ROWS))
        def _(pair):
            off0 = pl.multiple_of(base + pair * (2 * SC_ROWS), SC_ROWS)
            off1 = pl.multiple_of(off0 + SC_ROWS, SC_ROWS)
            pltpu.sync_copy(idx_hbm.at[pl.ds(off0, SC_ROWS)], idx0)
            pltpu.sync_copy(idx_hbm.at[pl.ds(off1, SC_ROWS)], idx1)
            gather0 = pltpu.async_copy(table_hbm.at[idx0], rows0, sem_g0)
            gather1 = pltpu.async_copy(table_hbm.at[idx1], rows1, sem_g1)
            gather0.wait()
            write0 = pltpu.async_copy(rows0, out_hbm.at[pl.ds(off0, SC_ROWS)], sem_w0)
            gather1.wait()
            write1 = pltpu.async_copy(rows1, out_hbm.at[pl.ds(off1, SC_ROWS)], sem_w1)
            write0.wait()
            write1.wait()

    return gather(table, idx)


def _moe_layout(e0, e1, t, n_exp):
    experts = jnp.arange(n_exp, dtype=jnp.int32)
    routed = ((e0[:, None] == experts) | (e1[:, None] == experts)).astype(jnp.int32)
    csum = jnp.cumsum(routed, axis=0)
    padded = (csum[-1] + MOE_TILE - 1) // MOE_TILE * MOE_TILE
    ends = jnp.cumsum(padded)
    pos = (ends - padded)[None, :] + csum - routed
    pos0 = jnp.take_along_axis(pos, e0[:, None], axis=1)[:, 0]
    pos1 = jnp.take_along_axis(pos, e1[:, None], axis=1)[:, 0]
    n_rows = 2 * t + n_exp * MOE_TILE
    tok = jnp.arange(t, dtype=jnp.int32)
    src = jnp.zeros((n_rows,), jnp.int32).at[jnp.concatenate([pos0, pos1])].set(
        jnp.concatenate([tok, tok]), unique_indices=True, mode="promise_in_bounds")
    n_tiles = n_rows // MOE_TILE
    n_valid = ends[-1] // MOE_TILE
    tile_expert = jnp.sum(jnp.arange(n_tiles)[:, None] * MOE_TILE >= ends[None, :], axis=1)
    tile_expert = tile_expert[jnp.minimum(jnp.arange(n_tiles), n_valid - 1)].astype(jnp.int32)
    return pos0, pos1, src, tile_expert, n_valid.astype(jnp.int32).reshape(1)


W_SLOTS = 4


def _experts_packed_kernel(te_ref, nv_ref, first_ref, x_ref, wg_hbm, wu_hbm, wd_hbm, *rest, layer):
    o_ref, wg_b, wu_b, wd_b, st_in, st_dn, sems = rest[-7:]
    j = pl.program_id(0)
    valid = j < nv_ref[0]

    @pl.when(valid & (first_ref[j] == 1))
    def _():
        _load_weights_bf16(layer, te_ref[j], (wg_hbm, wu_hbm, wd_hbm), (wg_b, wu_b, wd_b), st_in, st_dn, sems)

    @pl.when(valid)
    def _():
        o_ref[...] = _pack_rows(_swiglu_resident(_unpack_rows(x_ref[...]).astype(BF16), wg_b, wu_b, wd_b))


def moe_experts_packed(x_part, tile_expert, n_valid, w_gate, w_up, w_down, layer, y_prev, part, n_rows):
    rows_part, half = x_part.shape
    d = 2 * half
    dff = w_gate.shape[3]
    tiles_part = rows_part // MOE_TILE
    first = jnp.concatenate([jnp.ones((1,), jnp.int32),
                             (tile_expert[1:] != tile_expert[:-1]).astype(jnp.int32)])

    def tile(j, nv):
        return jnp.maximum(jnp.minimum(j, nv[0] - 1), 0)

    hbm = pl.BlockSpec(memory_space=pl.ANY)
    in_specs = [pl.BlockSpec((MOE_TILE, half), lambda j, te, nv, fi: (tile(j, nv), 0)), hbm, hbm, hbm]
    args = [tile_expert, n_valid, first, x_part, w_gate, w_up, w_down]
    aliases = {}
    if y_prev is not None:
        in_specs.append(hbm)
        args.append(y_prev)
        aliases = {len(args) - 1: 0}
    return pl.pallas_call(
        functools.partial(_experts_packed_kernel, layer=layer),
        grid_spec=pltpu.PrefetchScalarGridSpec(
            num_scalar_prefetch=3, grid=(tiles_part,), in_specs=in_specs,
            out_specs=pl.BlockSpec((MOE_TILE, half), lambda j, te, nv, fi: (part * tiles_part + tile(j, nv), 0)),
            scratch_shapes=_weight_scratch(d, dff, W_SLOTS)),
        out_shape=jax.ShapeDtypeStruct((n_rows, half), jnp.int32),
        input_output_aliases=aliases,
        compiler_params=_params("arbitrary"),
        name="moe_experts",
    )(*args)


def _combine_packed_kernel(y0_ref, y1_ref, route_ref, x1_ref, mod_ref, o_ref):
    rt = route_ref[...]
    f = rt[:, 2:3] * _unpack_rows(y0_ref[...]) + rt[:, 3:4] * _unpack_rows(y1_ref[...])
    o_ref[...] = x1_ref[...] + mod_ref[5:6, :] * f


def moe_combine_packed(y_pairs, route, x1, mod, seq, tm=512):
    t, d = x1.shape
    nt = t // tm
    tiles_per_batch = seq // tm
    row = lambda i: (i, 0)
    return pl.pallas_call(
        _combine_packed_kernel,
        grid=(nt,),
        in_specs=[pl.BlockSpec((tm, d // 2), row), pl.BlockSpec((tm, d // 2), lambda i: (i + nt, 0)),
                  pl.BlockSpec((tm, LANES), row), pl.BlockSpec((tm, d), row),
                  pl.BlockSpec((None, 6, d), lambda i: (i // tiles_per_batch, 0, 0))],
        out_specs=pl.BlockSpec((tm, d), row),
        out_shape=jax.ShapeDtypeStruct((t, d), F32),
        compiler_params=_params("parallel"),
        name="moe_combine",
    )(y_pairs, y_pairs, route, x1, mod)


def ffn_moe_sc(x1, h_packed, route, mod, w_gate, w_up, w_down, layer, seq):
    t = x1.shape[0]
    e0, e1 = route[:, 0].astype(jnp.int32), route[:, 1].astype(jnp.int32)
    pos0, pos1, src, tile_expert, n_valid = _moe_layout(e0, e1, t, w_gate.shape[1])
    n_rows = src.shape[0]
    rows_part = n_rows // MOE_PARTS
    tiles_part = rows_part // MOE_TILE
    assert rows_part * MOE_PARTS == n_rows and tiles_part * MOE_TILE == rows_part
    h_parts = [sc_gather_rows(h_packed, src[p * rows_part:(p + 1) * rows_part]) for p in range(MOE_PARTS)]
    y_sorted = None
    for p in range(MOE_PARTS):
        n_valid_part = jnp.clip(n_valid - p * tiles_part, 0, tiles_part)
        y_sorted = moe_experts_packed(h_parts[p], tile_expert[p * tiles_part:(p + 1) * tiles_part], n_valid_part,
                                      w_gate, w_up, w_down, layer, y_sorted, p, n_rows)
    y_pairs = sc_gather_rows(y_sorted, jnp.concatenate([pos0, pos1]))
    return moe_combine_packed(y_pairs, route, x1, mod, seq)


def _final_norm_kernel(x_ref, g_ref, o_ref):
    x = x_ref[...]
    o_ref[...] = (x * lax.rsqrt(jnp.mean(x * x, axis=-1, keepdims=True) + EPS)) * g_ref[...]


def final_norm(x, g, tm=512):
    t, d = x.shape
    return pl.pallas_call(
        _final_norm_kernel,
        grid=(t // tm,),
        in_specs=[pl.BlockSpec((tm, d), lambda i: (i, 0)), pl.BlockSpec((1, d), lambda i: (0, 0))],
        out_specs=pl.BlockSpec((tm, d), lambda i: (i, 0)),
        out_shape=jax.ShapeDtypeStruct((t, d), F32),
        compiler_params=_params("parallel"),
        name="final_norm",
    )(x, g)


def _rope_tables(positions, dim):
    rot = dim // ROPE_FRACTION
    half = rot // 2
    inv = 1.0 / (ROPE_THETA ** (np.arange(0, rot, 2, dtype=np.float32) / rot))
    ang = positions.reshape(-1).astype(F32)[:, None] * jnp.asarray(inv, F32)
    cos, sin = jnp.cos(ang), jnp.sin(ang)
    t = ang.shape[0]
    ones = jnp.ones((t, dim - rot), F32)
    zeros = lambda w: jnp.zeros((t, w), F32)
    reps = LANES // dim
    c = jnp.tile(jnp.concatenate([cos, cos, ones], axis=1), (1, reps))
    sa = jnp.tile(jnp.concatenate([-sin, zeros(dim - half)], axis=1), (1, reps))
    sb = jnp.tile(jnp.concatenate([zeros(half), sin, zeros(dim - rot)], axis=1), (1, reps))
    return c, sa, sb


def _relayout_w_in(w):
    pts = np.cumsum([0, 384, 64, 64, 256, 64, 4, 256, 256, 256, 384, 384, 384])
    (q_a, k_a, v_a, q_i, k_i, w_i, q_b, k_b, v_b, q_c, k_c, v_c) = [w[..., pts[j]:pts[j + 1]] for j in range(12)]
    w_i_pad = jnp.concatenate([w_i, jnp.zeros(w.shape[:-1] + (LANES - IDX_HEADS,), w.dtype)], axis=-1)
    return jnp.concatenate([q_a, k_a, k_a, v_a, v_a, q_i, k_i, k_i, w_i_pad,
                            q_b, k_b, v_b, q_c, k_c, v_c], axis=-1).astype(BF16)


def kernel(x, c, positions, w_in, w_out, diff_lambda, diff_subln, w_ada, b_ada, g_attn, g_ffn, w_ff_gate,
           w_ff_up, w_ff_down, w_router, w_exp_gate, w_exp_up, w_exp_down, g_final):
    batch, seq, d = x.shape
    depth = w_in.shape[0]
    t = batch * seq
    tabs = _rope_tables(positions, HEAD_DIM) + _rope_tables(positions, B_QK_DIM)
    mod_all = adaln_mod(c, w_ada, b_ada).reshape(depth, batch, 6, d)
    w_in_pad = _relayout_w_in(w_in)
    wo = w_out.astype(BF16)
    xf = x.reshape(t, d)
    for layer in range(depth):
        mod = mod_all[layer]
        lam_init = 0.8 - 0.6 * math.exp(-0.3 * layer)
        (qa, kka, vva, qi, kki, wi, qb, kb, vb, qc, kc, vc) = inproj(
            xf, g_attn[layer].reshape(1, d), mod, tabs, w_in_pad, layer, seq)
        ya = dsa_attention(qa, qi, wi, kka, vva, kki, batch, seq)
        g_sub2 = jnp.tile(diff_subln[layer], 2).reshape(1, LANES)
        yb = diff_attention(qb, kb, vb, diff_lambda[layer], g_sub2, lam_init, batch, seq)
        yc = moba_attention(qc, kc, vc, batch, seq)
        j = layer // 2
        gf = g_ffn[layer].reshape(1, d)
        if layer % 2 == 0:
            xf = outproj_ffn_dense(xf, ya, yb, yc, wo, layer, gf, mod,
                                   w_ff_gate[:, None], w_ff_up[:, None], w_ff_down[:, None], j, seq)
        else:
            wr = jnp.concatenate([w_router[j], jnp.zeros((d, LANES - N_EXPERTS), F32)], axis=1)
            wr_hi = wr.astype(BF16)
            wr = jnp.stack([wr_hi, (wr - wr_hi.astype(F32)).astype(BF16)])
            x1, h_packed, route = outproj_router(xf, ya, yb, yc, wo, layer, gf, mod, seq, wr)
            xf = ffn_moe_sc(x1, h_packed, route, mod, w_exp_gate, w_exp_up, w_exp_down, j, seq)
    return final_norm(xf, g_final.reshape(1, d)).reshape(batch, seq, d)
```

```python
import functools
import math

import jax
import jax.numpy as jnp
import numpy as np
from jax import lax
from jax.experimental import pallas as pl
from jax.experimental.pallas import tpu as pltpu
from jax.experimental.pallas import tpu_sc as plsc

F32 = jnp.float32
BF16 = jnp.bfloat16

HEAD_DIM = 64
A_HEADS = 6
IDX_HEADS = 4
B_HEADS = 4
B_QK_DIM = 32
C_HEADS = 6
A_WIDTH, IDX_WIDTH = A_HEADS * HEAD_DIM, IDX_HEADS * HEAD_DIM
B_WIDTH, C_WIDTH = B_HEADS * HEAD_DIM, C_HEADS * HEAD_DIM
DSA_TOPK_MAX = 256
MOBA_BLOCK = 256
MOBA_TOPK = 3
ROPE_THETA = 500000.0
ROPE_FRACTION = 4
SUBLN_EPS = 1e-5
EPS = 1e-6
N_EXPERTS = 8

LANES = 128
NEG = -1e30
INT_MIN = -2 ** 31
MIN_NORMAL_KEY = 0x00800000
MIN_NORMAL_F32 = float(np.float32(2.0 ** -126))
VMEM_LIMIT = 48 * 1024 * 1024
LOG2E = math.log2(math.e)

_G_QA, _G_KKA, _G_VVA, _G_QI, _G_KKI, _G_WI = (0, 384), (384, 512), (512, 640), (640, 896), (896, 1024), (1024, 1152)
_G_QB, _G_KB, _G_VB = (1152, 1408), (1408, 1664), (1664, 1920)
_G_QC, _G_KC, _G_VC = (1920, 2304), (2304, 2688), (2688, 3072)
D_IN_PAD = 3072


def _params(*sem):
    return pltpu.CompilerParams(dimension_semantics=sem, vmem_limit_bytes=VMEM_LIMIT)


def _dot(a, b):
    return jnp.dot(a, b, preferred_element_type=F32)


def _dot_nt(a, b):
    return lax.dot_general(a, b, (((1,), (1,)), ((), ())), preferred_element_type=F32)


def _adaln_kernel(c_ref, w_ref, b_ref, o_ref):
    c = c_ref[...]
    c_act = c * (1.0 / (1.0 + jnp.exp(-c)))
    o_ref[...] = jnp.dot(c_act, w_ref[...], preferred_element_type=F32,
                         precision=lax.Precision.HIGHEST) + b_ref[...]


def adaln_mod(c, w_ada, b_ada, tn=1536):
    depth, d, n = w_ada.shape
    b = c.shape[0]
    return pl.pallas_call(
        _adaln_kernel,
        grid=(depth, n // tn),
        in_specs=[pl.BlockSpec((b, d), lambda l, j: (0, 0)),
                  pl.BlockSpec((None, d, tn), lambda l, j: (l, 0, j)),
                  pl.BlockSpec((None, 1, tn), lambda l, j: (l, 0, j))],
        out_specs=pl.BlockSpec((None, b, tn), lambda l, j: (l, 0, j)),
        out_shape=jax.ShapeDtypeStruct((depth, b, n), F32),
        compiler_params=_params("parallel", "parallel"),
        name="adaln_mod",
    )(c, w_ada, b_ada.reshape(depth, 1, n))


def _norm_mod(x, g, shift, scale, eps=EPS):
    y = x * lax.rsqrt(jnp.mean(x * x, axis=-1, keepdims=True) + eps)
    return (y * g) * (1.0 + scale) + shift


def _rope_store(acc, o_ref, cos, sa, sb, half):
    for j in range(acc.shape[1] // LANES):
        a = acc[:, j * LANES:(j + 1) * LANES]
        r = a * cos + pltpu.roll(a, half, 1) * sb + pltpu.roll(a, LANES - half, 1) * sa
        o_ref[:, j * LANES:(j + 1) * LANES] = r.astype(o_ref.dtype)


def _inproj_kernel(x_ref, g_ref, mod_ref, c64_ref, sa64_ref, sb64_ref, c32_ref, sa32_ref, sb32_ref, w_ref,
                   qa_ref, kka_ref, vva_ref, qi_ref, kki_ref, wi_ref,
                   qb_ref, kb_ref, vb_ref, qc_ref, kc_ref, vc_ref):
    h = _norm_mod(x_ref[...], g_ref[...], mod_ref[0:1, :], mod_ref[1:2, :]).astype(BF16)
    c64, sa64, sb64 = c64_ref[...], sa64_ref[...], sb64_ref[...]
    c32, sa32, sb32 = c32_ref[...], sa32_ref[...], sb32_ref[...]

    def proj(cols):
        return _dot(h, w_ref[:, cols[0]:cols[1]])

    qk_scale = HEAD_DIM ** -0.5 * LOG2E
    _rope_store(proj(_G_QA), qa_ref, c64 * qk_scale, sa64 * qk_scale, sb64 * qk_scale, 8)
    _rope_store(proj(_G_KKA), kka_ref, c64, sa64, sb64, 8)
    vva_ref[...] = proj(_G_VVA).astype(vva_ref.dtype)
    _rope_store(proj(_G_QI), qi_ref, c64, sa64, sb64, 8)
    _rope_store(proj(_G_KKI), kki_ref, c64, sa64, sb64, 8)
    wi_ref[...] = proj(_G_WI) * (IDX_HEADS ** -0.5 * HEAD_DIM ** -0.5)
    b_scale = B_QK_DIM ** -0.5 * LOG2E
    _rope_store(proj(_G_QB), qb_ref, c32 * b_scale, sa32 * b_scale, sb32 * b_scale, 4)
    _rope_store(proj(_G_KB), kb_ref, c32, sa32, sb32, 4)
    vb_ref[...] = proj(_G_VB).astype(vb_ref.dtype)
    _rope_store(proj(_G_QC), qc_ref, c64 * qk_scale, sa64 * qk_scale, sb64 * qk_scale, 8)
    _rope_store(proj(_G_KC), kc_ref, c64, sa64, sb64, 8)
    vc_ref[...] = proj(_G_VC).astype(vc_ref.dtype)


def inproj(x, g, mod, tabs, w_pad, layer, seq, tm=512):
    t, d = x.shape
    tiles_per_batch = seq // tm
    row = lambda i: (i, 0)
    widths = [A_WIDTH, LANES, LANES, IDX_WIDTH, LANES, LANES, B_WIDTH, B_WIDTH, B_WIDTH, C_WIDTH, C_WIDTH, C_WIDTH]
    dtypes = [BF16, BF16, BF16, BF16, BF16, F32, BF16, BF16, BF16, BF16, BF16, BF16]
    return pl.pallas_call(
        _inproj_kernel,
        grid=(t // tm,),
        in_specs=[pl.BlockSpec((tm, d), row),
                  pl.BlockSpec((1, d), lambda i: (0, 0)),
                  pl.BlockSpec((None, 6, d), lambda i: (i // tiles_per_batch, 0, 0))]
                 + [pl.BlockSpec((tm, LANES), row)] * 6
                 + [pl.BlockSpec((None, d, D_IN_PAD), lambda i: (layer, 0, 0))],
        out_specs=[pl.BlockSpec((tm, w), row) for w in widths],
        out_shape=[jax.ShapeDtypeStruct((t, w), dt) for w, dt in zip(widths, dtypes)],
        compiler_params=_params("parallel"),
        name="inproj",
    )(x, g, mod, *tabs, w_pad)


def _init_stats(m_ref, l_ref, acc_ref):
    m_ref[...] = jnp.full(m_ref.shape, -jnp.inf, F32)
    l_ref[...] = jnp.zeros(l_ref.shape, F32)
    acc_ref[...] = jnp.zeros(acc_ref.shape, F32)


def _key_to_f32(k):
    return lax.bitcast_convert_type(jnp.where(k >= 0, k, k ^ 0x7FFFFFFF), F32)


def dsa_attention(qa, qi, wi, kka, vva, kki, batch, seq, tq=256):
    t = qa.shape[0]
    nq = seq // tq
    k_top = min(DSA_TOPK_MAX, seq // 4)
    qrow = lambda b, i: (b * nq + i, 0)
    full = lambda b, i: (b, 0)
    return pl.pallas_call(
        functools.partial(_dsa_kernel_t, tq=tq, k_top=k_top, seq=seq),
        grid=(batch, nq),
        in_specs=[pl.BlockSpec((tq, A_WIDTH), qrow), pl.BlockSpec((tq, IDX_WIDTH), qrow),
                  pl.BlockSpec((tq, LANES), qrow),
                  pl.BlockSpec((seq, LANES), full), pl.BlockSpec((seq, LANES), full),
                  pl.BlockSpec((seq, LANES), full)],
        out_specs=pl.BlockSpec((tq, A_WIDTH), qrow),
        out_shape=jax.ShapeDtypeStruct((t, A_WIDTH), BF16),
        scratch_shapes=[pltpu.VMEM((seq, tq), F32), pltpu.VMEM((seq, tq), jnp.int16),
                        pltpu.VMEM((seq, tq), jnp.int16), pltpu.VMEM((1, LANES, seq), BF16),
                        pltpu.VMEM((3, 2 * tq, LANES), BF16),
                        pltpu.VMEM((3, 1, 2 * tq), F32), pltpu.VMEM((3, 1, 2 * tq), F32),
                        pltpu.VMEM((3, LANES, 2 * tq), F32)],
        compiler_params=_params("parallel", "arbitrary"),
        name="dsa_attention",
    )(qa, qi, wi, kka, vva, kki)


def diff_attention(qb, kb, vb, lam_vec, g_sub2, lam_init, batch, seq, tq=256):
    t = qb.shape[0]
    nq = seq // tq
    qrow = lambda b, i: (b * nq + i, 0)
    full = lambda b, i: (b, 0)
    const = lambda b, i: (0, 0)
    return pl.pallas_call(
        functools.partial(_diff_kernel_t, tq=tq, lam_init=lam_init, seq=seq),
        grid=(batch, nq),
        in_specs=[pl.BlockSpec((tq, B_WIDTH), qrow), pl.BlockSpec((seq, B_WIDTH), full),
                  pl.BlockSpec((seq, B_WIDTH), full),
                  pl.BlockSpec((4, B_QK_DIM), const), pl.BlockSpec((1, LANES), const)],
        out_specs=pl.BlockSpec((tq, B_WIDTH), qrow),
        out_shape=jax.ShapeDtypeStruct((t, B_WIDTH), BF16),
        scratch_shapes=[pltpu.VMEM((2, LANES, seq), BF16), pltpu.VMEM((2, 4 * tq, LANES), BF16),
                        pltpu.VMEM((2, 1, 4 * tq), F32), pltpu.VMEM((2, 1, 4 * tq), F32),
                        pltpu.VMEM((2, LANES, 4 * tq), F32)],
        compiler_params=_params("parallel", "arbitrary"),
        name="diff_attention",
    )(qb, kb, vb, lam_vec, g_sub2)


def moba_attention(qc, kc, vc, batch, seq):
    tq = MOBA_BLOCK
    t = qc.shape[0]
    nb = seq // tq
    n_sel = min(MOBA_TOPK, nb - 1)
    nbp = 8
    assert seq % tq == 0 and nb <= nbp
    qrow = lambda b, i: (b * nb + i, 0)
    full = lambda b, i: (b, 0)
    return pl.pallas_call(
        functools.partial(_moba_kernel_t, tq=tq, nb=nb, n_sel=n_sel, seq=seq),
        grid=(batch, nb),
        in_specs=[pl.BlockSpec((tq, C_WIDTH), qrow), pl.BlockSpec((seq, C_WIDTH), full),
                  pl.BlockSpec((seq, C_WIDTH), full)],
        out_specs=pl.BlockSpec((tq, C_WIDTH), qrow),
        out_shape=jax.ShapeDtypeStruct((t, C_WIDTH), BF16),
        scratch_shapes=[pltpu.VMEM((nbp, C_WIDTH), F32), pltpu.VMEM((3, LANES, seq), BF16),
                        pltpu.VMEM((3, 2 * tq, LANES), BF16), pltpu.VMEM((3, nbp, 2 * tq), F32),
                        pltpu.VMEM((3, 1, 2 * tq), F32), pltpu.VMEM((3, 1, 2 * tq), F32),
                        pltpu.VMEM((3, LANES, 2 * tq), F32)],
        compiler_params=_params("parallel", "arbitrary"),
        name="moba_attention",
    )(qc, kc, vc)


def _attend_t(k, q_ref, g, v_t, m_ref, l_ref, acc_ref, mask=None):
    for j in range(q_ref.shape[1] // LANES):
        cols = slice(j * LANES, (j + 1) * LANES)
        s = _dot_nt(k, q_ref[g, cols, :])
        if mask is not None:
            s = mask(j, s)
        m_prev = m_ref[g, :, cols]
        m_new = jnp.maximum(m_prev, jnp.max(s, axis=0, keepdims=True))
        alpha = jnp.exp2(m_prev - m_new)
        p = jnp.exp2(s - m_new)
        l_ref[g, :, cols] = alpha * l_ref[g, :, cols] + jnp.sum(p, axis=0, keepdims=True)
        acc_ref[g, :, cols] = alpha * acc_ref[g, :, cols] + _dot(v_t, p.astype(BF16))
        m_ref[g, :, cols] = m_new


def _causal_mask_t(tk, tq):
    key = lax.broadcasted_iota(jnp.int32, (tk, LANES), 0)
    qry = lax.broadcasted_iota(jnp.int32, (tk, LANES), 1)

    def mask(j, s):
        return jnp.where(key <= qry + (j * LANES) % tq, s, NEG)
    return mask


def _transpose_values(v_ref, vt_ref, seq, tk):
    for g in range(vt_ref.shape[0]):
        for n in range(seq // tk):
            blk = v_ref[n * tk:(n + 1) * tk, g * LANES:(g + 1) * LANES].astype(F32)
            vt_ref[g, :, n * tk:(n + 1) * tk] = blk.T.astype(vt_ref.dtype)


def _head_pair_out(acc_t, l, tq):
    even = acc_t[0:HEAD_DIM, 0:tq] / l[:, 0:tq]
    odd = acc_t[HEAD_DIM:LANES, tq:2 * tq] / l[:, tq:2 * tq]
    return jnp.concatenate([even, odd], axis=0).T


def _dsa_kernel_t(qa_ref, qi_ref, wiq_ref, kka_ref, vva_ref, kki_ref, o_ref,
                  sc_ref, hi_ref, lo_ref, vt_ref, qs_ref, m_ref, l_ref, acc_ref, *, tq, k_top, seq):
    i = pl.program_id(1)
    nk = i + 1
    t0 = i * tq
    lo = lax.broadcasted_iota(jnp.int32, (tq, LANES), 1) < HEAD_DIM
    zero_b = jnp.zeros((tq, LANES), BF16)

    @pl.when(i == 0)
    def _():
        _transpose_values(vva_ref, vt_ref, seq, tq)

    def stack_heads(q):
        out = []
        for g in range(q.shape[1] // LANES):
            qg = q[:, g * LANES:(g + 1) * LANES]
            out += [jnp.where(lo, qg, zero_b), jnp.where(lo, zero_b, qg)]
        return out

    qa_stack = stack_heads(qa_ref[...])
    for g in range(3):
        qs_ref[g, 0:tq, :] = qa_stack[2 * g]
        qs_ref[g, tq:2 * tq, :] = qa_stack[2 * g + 1]
    qi_stack = jnp.concatenate(stack_heads(qi_ref[...]), axis=0)
    wi_t = wiq_ref[...].T

    def rows(c):
        return pl.ds(pl.multiple_of(c * tq, tq), tq)

    key_pos = lax.broadcasted_iota(jnp.int32, (tq, tq), 0)
    qry_pos = lax.broadcasted_iota(jnp.int32, (tq, tq), 1)

    def idx_body(c, carry):
        r = jnp.maximum(_dot_nt(kki_ref[rows(c), :], qi_stack), 0.0)
        s = wi_t[0:1, :] * r[:, 0:tq]
        for h in range(1, IDX_HEADS):
            s = s + wi_t[h:h + 1, :] * r[:, h * tq:(h + 1) * tq]
        causal = (c * tq + key_pos) <= (t0 + qry_pos)
        s = jnp.where(causal, s, -jnp.inf)
        s = jnp.where(jnp.abs(s) < MIN_NORMAL_F32, 0.0, s)
        sc_ref[rows(c), :] = s
        bits = lax.bitcast_convert_type(s, jnp.int32)
        key = jnp.where(bits >= 0, bits, bits ^ 0x7FFFFFFF)
        hi_ref[rows(c), :] = lax.shift_right_arithmetic(key, jnp.int32(16)).astype(jnp.int16)
        lo_ref[rows(c), :] = ((key & 0xFFFF) - 32768).astype(jnp.int16)
        return carry

    lax.fori_loop(0, nk, idx_body, 0)

    def count16(ref, cand, strict):
        c16 = cand.astype(jnp.int16)

        def body(c, acc):
            x = ref[rows(c), :]
            hit = jnp.where((x > c16) if strict else (x >= c16), jnp.int16(1), jnp.int16(0))
            for r in range(tq // 16):
                acc = acc + hit[r * 16:(r + 1) * 16]
            return acc
        acc = lax.fori_loop(0, nk, body, jnp.zeros((16, tq), jnp.int16))
        return jnp.sum(acc.astype(F32), axis=0, keepdims=True)

    def search16(ref, k_need):
        v0 = jnp.where(count16(ref, jnp.zeros((1, tq), jnp.int32), False) >= k_need, 0, -32768).astype(jnp.int32)

        def bisect(b, v):
            trial = v | lax.shift_left(jnp.int32(1), 14 - b)
            return jnp.where(count16(ref, trial, False) >= k_need, trial, v)
        return lax.fori_loop(0, 15, bisect, v0)

    kf = float(k_top)

    def search():
        hi_k = search16(hi_ref, kf)
        above = count16(hi_ref, hi_k, True)
        hi16 = hi_k.astype(jnp.int16)

        def keep_bucket(c, carry):
            lo_ref[rows(c), :] = jnp.where(hi_ref[rows(c), :] == hi16, lo_ref[rows(c), :], jnp.int16(-32768))
            return carry
        lax.fori_loop(0, nk, keep_bucket, 0)
        lo_k = search16(lo_ref, kf - above)
        n_gt = above + count16(lo_ref, lo_k, True)
        return lax.shift_left(hi_k, jnp.int32(16)) | (lo_k + 32768), kf - n_gt

    key, need = lax.cond(t0 + tq <= k_top,
                         lambda: (jnp.full((1, tq), INT_MIN, jnp.int32), jnp.full((1, tq), kf, F32)), search)
    thr = _key_to_f32(key)
    thr_next = _key_to_f32(jnp.where(key == 0, MIN_NORMAL_KEY, key + 1))
    all_sel = (t0 + lax.broadcasted_iota(jnp.int32, (1, tq), 1)) < k_top
    lower = (qry_pos <= key_pos).astype(BF16)

    _init_stats(m_ref, l_ref, acc_ref)

    def att_body(c, tie_run):
        s_idx = sc_ref[rows(c), :]
        ge = s_idx >= thr
        gt = s_idx >= thr_next
        tie = jnp.where(gt, 0.0, jnp.where(ge, 1.0, 0.0))
        prefix = _dot(lower, tie.astype(BF16)) + tie_run
        take = jnp.where(gt, 1.0, jnp.where(prefix <= need, tie, 0.0))
        take = jnp.where(all_sel, 1.0, take)
        take = jnp.where(s_idx > -jnp.inf, take, 0.0)

        def mask(j, s):
            q0 = (j * LANES) % tq
            return jnp.where(take[:, q0:q0 + LANES] > 0.5, s, NEG)

        kk = kka_ref[rows(c), :]
        vt = vt_ref[0, :, rows(c)]
        for g in range(3):
            _attend_t(kk, qs_ref, g, vt, m_ref, l_ref, acc_ref, mask)
        return tie_run + jnp.sum(tie, axis=0, keepdims=True)

    lax.fori_loop(0, nk, att_body, jnp.zeros((1, tq), F32))

    for g in range(3):
        o_ref[:, g * LANES:(g + 1) * LANES] = _head_pair_out(acc_ref[g], l_ref[g], tq).astype(o_ref.dtype)


def _diff_kernel_t(qb_ref, kb_ref, vb_ref, lam_ref, gsub_ref, o_ref,
                   vt_ref, qs_ref, m_ref, l_ref, acc_ref, *, tq, lam_init, seq):
    i = pl.program_id(1)
    lane = lax.broadcasted_iota(jnp.int32, (tq, LANES), 1)
    zero_b = jnp.zeros((tq, LANES), BF16)

    @pl.when(i == 0)
    def _():
        _transpose_values(vb_ref, vt_ref, seq, tq)

    qb = qb_ref[...]
    for g in range(2):
        qg = qb[:, g * LANES:(g + 1) * LANES]
        for j in range(4):
            qs_ref[g, j * tq:(j + 1) * tq, :] = jnp.where(lane // B_QK_DIM == j, qg, zero_b)
    _init_stats(m_ref, l_ref, acc_ref)

    def rows(c):
        return pl.ds(pl.multiple_of(c * tq, tq), tq)

    def step(c, masked):
        mask = _causal_mask_t(tq, tq) if masked else None
        for g in range(2):
            _attend_t(kb_ref[rows(c), g * LANES:(g + 1) * LANES], qs_ref, g, vt_ref[g, :, rows(c)],
                      m_ref, l_ref, acc_ref, mask)

    def body(c, carry):
        step(c, False)
        return carry

    lax.fori_loop(0, i, body, 0)
    step(i, True)

    lv = lam_ref[...]
    lam = (jnp.exp(jnp.sum(lv[0:1] * lv[1:2], axis=1, keepdims=True))
           - jnp.exp(jnp.sum(lv[2:3] * lv[3:4], axis=1, keepdims=True)) + lam_init)
    gsub = gsub_ref[...]
    for g in range(2):
        acc_t, l = acc_ref[g], l_ref[g]

        def prob(r0, j):
            return acc_t[r0:r0 + HEAD_DIM, j * tq:(j + 1) * tq] / l[:, j * tq:(j + 1) * tq]

        halves = []
        for r0, j in ((0, 0), (HEAD_DIM, 2)):
            o = prob(r0, j) - lam * prob(r0, j + 1)
            ms = jnp.mean(o * o, axis=0, keepdims=True)
            halves.append(o * lax.rsqrt(ms + SUBLN_EPS))
        y = (jnp.concatenate(halves, axis=0).T * gsub) * (1.0 - lam_init)
        o_ref[:, g * LANES:(g + 1) * LANES] = y.astype(o_ref.dtype)


def _moba_kernel_t(qc_ref, kc_ref, vc_ref, o_ref, kmean_ref, vt_ref, qs_ref, bias_ref, m_ref, l_ref, acc_ref,
                   *, tq, nb, n_sel, seq):
    i = pl.program_id(1)
    nbp = kmean_ref.shape[0]
    lane = lax.broadcasted_iota(jnp.int32, (tq, LANES), 1)
    lo = lane < HEAD_DIM
    zero_b = jnp.zeros((tq, LANES), BF16)

    @pl.when(i == 0)
    def _():
        _transpose_values(vc_ref, vt_ref, seq, tq)
        kmean_ref[...] = jnp.zeros(kmean_ref.shape, F32)
        for n in range(nb):
            kblk = kc_ref[n * tq:(n + 1) * tq, :].astype(F32)
            kmean_ref[n:n + 1, :] = jnp.mean(kblk, axis=0, keepdims=True)

    qc = qc_ref[...]
    sub = lax.broadcasted_iota(jnp.int32, (nbp, 2 * tq), 0)
    past = sub < i
    for g in range(3):
        qg = qc[:, g * LANES:(g + 1) * LANES]
        q2 = jnp.concatenate([jnp.where(lo, qg, zero_b), jnp.where(lo, zero_b, qg)], axis=0)
        km = kmean_ref[:, g * LANES:(g + 1) * LANES]
        km_hi = km.astype(BF16)
        gt = _dot_nt(km_hi, q2) + _dot_nt((km - km_hi.astype(F32)).astype(BF16), q2)
        rows_ = []
        for n in range(nbp):
            gn = gt[n:n + 1, :]
            beats = jnp.where(sub < n, jnp.where(gt >= gn, 1.0, 0.0), jnp.where(gt > gn, 1.0, 0.0))
            beats = jnp.where(sub == n, 0.0, jnp.where(past, beats, 0.0))
            rank = jnp.sum(beats, axis=0, keepdims=True)
            rows_.append(jnp.where(rank < n_sel, 0.0, NEG))
        bias_ref[g] = jnp.where(past, jnp.concatenate(rows_, axis=0), NEG)
        qs_ref[g] = q2
    _init_stats(m_ref, l_ref, acc_ref)

    def rows(c):
        return pl.ds(pl.multiple_of(c * tq, tq), tq)

    def body(c, carry):
        for g in range(3):
            bias = bias_ref[g, pl.ds(c, 1), :]

            def mask(j, s):
                return s + bias[:, j * LANES:(j + 1) * LANES]

            _attend_t(kc_ref[rows(c), g * LANES:(g + 1) * LANES], qs_ref, g, vt_ref[g, :, rows(c)],
                      m_ref, l_ref, acc_ref, mask)
        return carry

    lax.fori_loop(0, i, body, 0)

    causal = _causal_mask_t(tq, tq)
    for g in range(3):
        _attend_t(kc_ref[rows(i), g * LANES:(g + 1) * LANES], qs_ref, g, vt_ref[g, :, rows(i)],
                  m_ref, l_ref, acc_ref, causal)
        o_ref[:, g * LANES:(g + 1) * LANES] = _head_pair_out(acc_ref[g], l_ref[g], tq).astype(o_ref.dtype)


def _route(h, wr_ref, cw_ref, cnt_ref):
    hi = h.astype(BF16)
    lo = (h - hi.astype(F32)).astype(BF16)
    logits = _dot(hi, wr_ref[0]) + (_dot(lo, wr_ref[0]) + _dot(hi, wr_ref[1]))
    lane = lax.broadcasted_iota(jnp.int32, logits.shape, 1)
    lg = jnp.where(lane < N_EXPERTS, logits, -jnp.inf)
    v0 = jnp.max(lg, axis=1, keepdims=True)
    i0 = jnp.min(jnp.where(lg == v0, lane, LANES), axis=1, keepdims=True)
    lg1 = jnp.where(lane == i0, -jnp.inf, lg)
    v1 = jnp.max(lg1, axis=1, keepdims=True)
    i1 = jnp.min(jnp.where(lg1 == v1, lane, LANES), axis=1, keepdims=True)
    e1 = jnp.exp(v1 - v0)
    w0 = 1.0 / (1.0 + e1)
    tm = logits.shape[0]
    routed = jnp.where(lane == i0, 1.0, 0.0) + jnp.where(lane == i1, 1.0, 0.0)
    earlier_rows = (lax.broadcasted_iota(jnp.int32, (tm, tm), 1)
                    < lax.broadcasted_iota(jnp.int32, (tm, tm), 0)).astype(BF16)

    @pl.when(pl.program_id(0) == 0)
    def _():
        cnt_ref[...] = jnp.zeros(cnt_ref.shape, F32)

    before = _dot(earlier_rows, routed.astype(BF16)) + cnt_ref[0:1, :]
    r0 = jnp.sum(jnp.where(lane == i0, before, 0.0), axis=1, keepdims=True)
    r1 = jnp.sum(jnp.where(lane == i1, before, 0.0), axis=1, keepdims=True)
    cnt_ref[0:1, :] = cnt_ref[0:1, :] + jnp.sum(routed, axis=0, keepdims=True)
    cw_ref[...] = (jnp.where(lane == 0, i0.astype(F32), 0.0) + jnp.where(lane == 1, i1.astype(F32), 0.0)
                   + jnp.where(lane == 2, w0, 0.0) + jnp.where(lane == 3, e1 * w0, 0.0)
                   + jnp.where(lane == 4, r0, 0.0) + jnp.where(lane == 5, r1, 0.0))


def _mixer_out(x_ref, ya_ref, yb_ref, yc_ref, wo_ref, g_ref, mod_ref):
    ab = A_WIDTH + B_WIDTH
    y = (_dot(ya_ref[...], wo_ref[0:A_WIDTH, :]) + _dot(yb_ref[...], wo_ref[A_WIDTH:ab, :])
         + _dot(yc_ref[...], wo_ref[ab:ab + C_WIDTH, :]))
    x1 = x_ref[...] + mod_ref[2:3, :] * y
    return x1, _norm_mod(x1, g_ref[...], mod_ref[3:4, :], mod_ref[4:5, :])


def _mixer_out_specs(tm, d, layer, tiles_per_batch):
    row = lambda i: (i, 0)
    return [pl.BlockSpec((tm, d), row), pl.BlockSpec((tm, A_WIDTH), row), pl.BlockSpec((tm, B_WIDTH), row),
            pl.BlockSpec((tm, C_WIDTH), row), pl.BlockSpec((None, d, d), lambda i: (layer, 0, 0)),
            pl.BlockSpec((1, d), lambda i: (0, 0)),
            pl.BlockSpec((None, 6, d), lambda i: (i // tiles_per_batch, 0, 0))]


def _outproj_kernel(x_ref, ya_ref, yb_ref, yc_ref, wo_ref, g_ref, mod_ref, wr_ref, x1_ref, h_ref, cw_ref, cnt_ref):
    x1, h = _mixer_out(x_ref, ya_ref, yb_ref, yc_ref, wo_ref, g_ref, mod_ref)
    x1_ref[...] = x1
    h_ref[...] = _pack_rows(h)
    _route(h, wr_ref, cw_ref, cnt_ref)


def outproj_router(x, ya, yb, yc, wo, layer, g_ffn, mod, seq, w_router, tm=512):
    t, d = x.shape
    row = lambda i: (i, 0)
    return pl.pallas_call(
        _outproj_kernel,
        grid=(t // tm,),
        in_specs=_mixer_out_specs(tm, d, layer, seq // tm) + [pl.BlockSpec((2, d, LANES), lambda i: (0, 0, 0))],
        out_specs=[pl.BlockSpec((tm, d), row), pl.BlockSpec((tm, d // 2), row), pl.BlockSpec((tm, LANES), row),
                   pl.BlockSpec((8, LANES), lambda i: (0, 0))],
        out_shape=[jax.ShapeDtypeStruct((t, d), F32), jax.ShapeDtypeStruct((t, d // 2), jnp.int32),
                   jax.ShapeDtypeStruct((t, LANES), F32), jax.ShapeDtypeStruct((8, LANES), F32)],
        compiler_params=_params("arbitrary"),
        name="outproj_router",
    )(x, ya, yb, yc, wo, g_ffn, mod, w_router)


W_ROWS_IN = 128
W_ROWS_DOWN = 352
FF_SPLIT = 2


def _load_weights_bf16(layer, e, w_hbms, w_bs, st_in, st_dn, sems):
    slots = st_in.shape[0]
    plan = []
    for w_hbm, w_b, rb, st, sem0 in ((w_hbms[0], w_bs[0], W_ROWS_IN, st_in, 0), (w_hbms[1], w_bs[1], W_ROWS_IN, st_in, 0),
                                     (w_hbms[2], w_bs[2], W_ROWS_DOWN, st_dn, slots)):
        assert w_b.shape[0] % rb == 0
        plan += [(w_hbm, w_b, r0, rb, st, sem0) for r0 in range(0, w_b.shape[0], rb)]
    uses = {0: 0, slots: 0}
    copies = []
    for w_hbm, w_b, r0, rb, st, sem0 in plan:
        slot = uses[sem0] % slots
        uses[sem0] += 1
        copies.append((pltpu.make_async_copy(w_hbm.at[layer, e, pl.ds(r0, rb), :], st.at[slot],
                                             sems.at[sem0 + slot]), st, slot, w_b, r0, rb))
    ahead = slots - 1
    for copy in copies[:ahead]:
        copy[0].start()
    for b, (copy, st, slot, w_b, r0, rb) in enumerate(copies):
        if b + ahead < len(copies):
            copies[b + ahead][0].start()
        copy.wait()
        w_b[r0:r0 + rb, :] = st[slot].astype(BF16)


def _swiglu_resident(x, wg_b, wu_b, wd_b):
    tf = wg_b.shape[1] // FF_SPLIT
    acc = None
    for f in range(FF_SPLIT):
        cols = slice(f * tf, (f + 1) * tf)
        a = _dot(x, wg_b[:, cols])
        u = _dot(x, wu_b[:, cols])
        act = (a * (1.0 / (1.0 + jnp.exp(-a)))) * u
        part = _dot(act.astype(BF16), wd_b[cols, :])
        acc = part if acc is None else acc + part
    return acc


def _weight_scratch(d, dff, slots):
    return [pltpu.VMEM((d, dff), BF16), pltpu.VMEM((d, dff), BF16), pltpu.VMEM((dff, d), BF16),
            pltpu.VMEM((slots, W_ROWS_IN, dff), F32), pltpu.VMEM((slots, W_ROWS_DOWN, d), F32),
            pltpu.SemaphoreType.DMA((2 * slots,))]


def _ffn_kernel(x_ref, ya_ref, yb_ref, yc_ref, wo_ref, g_ref, mod_ref, wg_hbm, wu_hbm, wd_hbm, o_ref,
                wg_b, wu_b, wd_b, st_in, st_dn, sems, *, layer):
    @pl.when(pl.program_id(0) == 0)
    def _():
        _load_weights_bf16(layer, 0, (wg_hbm, wu_hbm, wd_hbm), (wg_b, wu_b, wd_b), st_in, st_dn, sems)

    x1, h = _mixer_out(x_ref, ya_ref, yb_ref, yc_ref, wo_ref, g_ref, mod_ref)
    o_ref[...] = x1 + mod_ref[5:6, :] * _swiglu_resident(h.astype(BF16), wg_b, wu_b, wd_b)


def outproj_ffn_dense(x, ya, yb, yc, wo, layer, g_ffn, mod, w_gate, w_up, w_down, ff_layer, seq, tm=512):
    t, d = x.shape
    hbm = pl.BlockSpec(memory_space=pl.ANY)
    return pl.pallas_call(
        functools.partial(_ffn_kernel, layer=ff_layer),
        grid=(t // tm,),
        in_specs=_mixer_out_specs(tm, d, layer, seq // tm) + [hbm, hbm, hbm],
        out_specs=pl.BlockSpec((tm, d), lambda i: (i, 0)),
        out_shape=jax.ShapeDtypeStruct((t, d), F32),
        scratch_shapes=_weight_scratch(d, w_gate.shape[3], 2),
        compiler_params=_params("arbitrary"),
        name="ffn_dense",
    )(x, ya, yb, yc, wo, g_ffn, mod, w_gate, w_up, w_down)


MOE_TILE = 512
MOE_PARTS = 3
SC_CORES, SC_SUBCORES = 2, 16
SC_ROWS = 64
HI16 = -65536


def _pack_rows(x):
    c = x.shape[1] // 2
    bits = lax.bitcast_convert_type(x.astype(jnp.bfloat16).astype(F32), jnp.int32)
    return lax.shift_right_logical(bits[:, :c], jnp.int32(16)) | (bits[:, c:] & jnp.int32(HI16))


def _unpack_rows(w):
    lo = lax.bitcast_convert_type(lax.shift_left(w, jnp.int32(16)), F32)
    hi = lax.bitcast_convert_type(w & jnp.int32(HI16), F32)
    return jnp.concatenate([lo, hi], axis=1)


def sc_gather_rows(table, idx):
    d = table.shape[1]
    b = idx.shape[0]
    per_worker = b // (SC_CORES * SC_SUBCORES)
    assert per_worker * SC_CORES * SC_SUBCORES == b and per_worker % (2 * SC_ROWS) == 0
    mesh = plsc.VectorSubcoreMesh(core_axis_name="c", subcore_axis_name="s")
    idx_buf = pltpu.VMEM((SC_ROWS,), jnp.int32)
    row_buf = pltpu.VMEM((SC_ROWS, d), table.dtype)

    @functools.partial(
        pl.kernel, mesh=mesh, out_type=jax.ShapeDtypeStruct((b, d), table.dtype),
        scratch_types=[idx_buf, idx_buf, row_buf, row_buf] + [pltpu.SemaphoreType.DMA] * 4,
        name="sc_gather_rows")
    def gather(table_hbm, idx_hbm, out_hbm, idx0, idx1, rows0, rows1, sem_g0, sem_g1, sem_w0, sem_w1):
        base = (lax.axis_index("s") * SC_CORES + lax.axis_index("c")) * per_worker

        @pl.loop(0, per_worker // (2 * SC_ROWS))
        def _(pair):
            off0 = pl.multiple_of(base + pair * (2 * SC_ROWS), SC_ROWS)
            off1 = pl.multiple_of(off0 + SC_ROWS, SC_ROWS)
            pltpu.sync_copy(idx_hbm.at[pl.ds(off0, SC_ROWS)], idx0)
            pltpu.sync_copy(idx_hbm.at[pl.ds(off1, SC_ROWS)], idx1)
            gather0 = pltpu.async_copy(table_hbm.at[idx0], rows0, sem_g0)
            gather1 = pltpu.async_copy(table_hbm.at[idx1], rows1, sem_g1)
            gather0.wait()
            write0 = pltpu.async_copy(rows0, out_hbm.at[pl.ds(off0, SC_ROWS)], sem_w0)
            gather1.wait()
            write1 = pltpu.async_copy(rows1, out_hbm.at[pl.ds(off1, SC_ROWS)], sem_w1)
            write0.wait()
            write1.wait()

    return gather(table, idx)


def _moe_layout(route, counts, n_exp):
    t = route.shape[0]
    e0, e1, rank0, rank1 = (route[:, k].astype(jnp.int32) for k in (0, 1, 4, 5))
    padded = (counts + MOE_TILE - 1) // MOE_TILE * MOE_TILE
    ends = jnp.cumsum(padded)
    offs = ends - padded
    pos0 = offs[e0] + rank0
    pos1 = offs[e1] + rank1
    n_rows = 2 * t + n_exp * MOE_TILE
    tok = jnp.arange(t, dtype=jnp.int32)
    src = jnp.zeros((n_rows,), jnp.int32).at[jnp.concatenate([pos0, pos1])].set(
        jnp.concatenate([tok, tok]), unique_indices=True, mode="promise_in_bounds")
    n_tiles = n_rows // MOE_TILE
    n_valid = ends[-1] // MOE_TILE
    tile_expert = jnp.sum(jnp.arange(n_tiles)[:, None] * MOE_TILE >= ends[None, :], axis=1)
    tile_expert = tile_expert[jnp.minimum(jnp.arange(n_tiles), n_valid - 1)].astype(jnp.int32)
    return pos0, pos1, src, tile_expert, n_valid.astype(jnp.int32).reshape(1)


W_SLOTS = 4


def _experts_packed_kernel(te_ref, nv_ref, first_ref, x_ref, wg_hbm, wu_hbm, wd_hbm, *rest, layer):
    o_ref, wg_b, wu_b, wd_b, st_in, st_dn, sems = rest[-7:]
    j = pl.program_id(0)
    valid = j < nv_ref[0]

    @pl.when(valid & (first_ref[j] == 1))
    def _():
        _load_weights_bf16(layer, te_ref[j], (wg_hbm, wu_hbm, wd_hbm), (wg_b, wu_b, wd_b), st_in, st_dn, sems)

    @pl.when(valid)
    def _():
        o_ref[...] = _pack_rows(_swiglu_resident(_unpack_rows(x_ref[...]).astype(BF16), wg_b, wu_b, wd_b))


def moe_experts_packed(x_part, tile_expert, n_valid, w_gate, w_up, w_down, layer, y_prev, part, n_rows):
    rows_part, half = x_part.shape
    d = 2 * half
    dff = w_gate.shape[3]
    tiles_part = rows_part // MOE_TILE
    first = jnp.concatenate([jnp.ones((1,), jnp.int32),
                             (tile_expert[1:] != tile_expert[:-1]).astype(jnp.int32)])

    def tile(j, nv):
        return jnp.maximum(jnp.minimum(j, nv[0] - 1), 0)

    hbm = pl.BlockSpec(memory_space=pl.ANY)
    in_specs = [pl.BlockSpec((MOE_TILE, half), lambda j, te, nv, fi: (tile(j, nv), 0)), hbm, hbm, hbm]
    args = [tile_expert, n_valid, first, x_part, w_gate, w_up, w_down]
    aliases = {}
    if y_prev is not None:
        in_specs.append(hbm)
        args.append(y_prev)
        aliases = {len(args) - 1: 0}
    return pl.pallas_call(
        functools.partial(_experts_packed_kernel, layer=layer),
        grid_spec=pltpu.PrefetchScalarGridSpec(
            num_scalar_prefetch=3, grid=(tiles_part,), in_specs=in_specs,
            out_specs=pl.BlockSpec((MOE_TILE, half), lambda j, te, nv, fi: (part * tiles_part + tile(j, nv), 0)),
            scratch_shapes=_weight_scratch(d, dff, W_SLOTS)),
        out_shape=jax.ShapeDtypeStruct((n_rows, half), jnp.int32),
        input_output_aliases=aliases,
        compiler_params=_params("arbitrary"),
        name="moe_experts",
    )(*args)


def _combine_packed_kernel(y0_ref, y1_ref, route_ref, x1_ref, mod_ref, o_ref):
    rt = route_ref[...]
    f = rt[:, 2:3] * _unpack_rows(y0_ref[...]) + rt[:, 3:4] * _unpack_rows(y1_ref[...])
    o_ref[...] = x1_ref[...] + mod_ref[5:6, :] * f


def moe_combine_packed(y_pairs, route, x1, mod, seq, tm=512):
    t, d = x1.shape
    nt = t // tm
    tiles_per_batch = seq // tm
    row = lambda i: (i, 0)
    return pl.pallas_call(
        _combine_packed_kernel,
        grid=(nt,),
        in_specs=[pl.BlockSpec((tm, d // 2), row), pl.BlockSpec((tm, d // 2), lambda i: (i + nt, 0)),
                  pl.BlockSpec((tm, LANES), row), pl.BlockSpec((tm, d), row),
                  pl.BlockSpec((None, 6, d), lambda i: (i // tiles_per_batch, 0, 0))],
        out_specs=pl.BlockSpec((tm, d), row),
        out_shape=jax.ShapeDtypeStruct((t, d), F32),
        compiler_params=_params("parallel"),
        name="moe_combine",
    )(y_pairs, y_pairs, route, x1, mod)


def ffn_moe_sc(x1, h_packed, route, counts, mod, w_gate, w_up, w_down, layer, seq):
    n_exp = w_gate.shape[1]
    pos0, pos1, src, tile_expert, n_valid = _moe_layout(route, counts[0, :n_exp].astype(jnp.int32), n_exp)
    n_rows = src.shape[0]
    rows_part = n_rows // MOE_PARTS
    tiles_part = rows_part // MOE_TILE
    assert rows_part * MOE_PARTS == n_rows and tiles_part * MOE_TILE == rows_part
    h_parts = [sc_gather_rows(h_packed, src[p * rows_part:(p + 1) * rows_part]) for p in range(MOE_PARTS)]
    y_sorted = None
    for p in range(MOE_PARTS):
        n_valid_part = jnp.clip(n_valid - p * tiles_part, 0, tiles_part)
        y_sorted = moe_experts_packed(h_parts[p], tile_expert[p * tiles_part:(p + 1) * tiles_part], n_valid_part,
                                      w_gate, w_up, w_down, layer, y_sorted, p, n_rows)
    y_pairs = sc_gather_rows(y_sorted, jnp.concatenate([pos0, pos1]))
    return moe_combine_packed(y_pairs, route, x1, mod, seq)


def _final_norm_kernel(x_ref, g_ref, o_ref):
    x = x_ref[...]
    o_ref[...] = (x * lax.rsqrt(jnp.mean(x * x, axis=-1, keepdims=True) + EPS)) * g_ref[...]


def final_norm(x, g, tm=512):
    t, d = x.shape
    return pl.pallas_call(
        _final_norm_kernel,
        grid=(t // tm,),
        in_specs=[pl.BlockSpec((tm, d), lambda i: (i, 0)), pl.BlockSpec((1, d), lambda i: (0, 0))],
        out_specs=pl.BlockSpec((tm, d), lambda i: (i, 0)),
        out_shape=jax.ShapeDtypeStruct((t, d), F32),
        compiler_params=_params("parallel"),
        name="final_norm",
    )(x, g)


def _rope_tables(positions, dim):
    rot = dim // ROPE_FRACTION
    half = rot // 2
    inv = 1.0 / (ROPE_THETA ** (np.arange(0, rot, 2, dtype=np.float32) / rot))
    ang = positions.reshape(-1).astype(F32)[:, None] * jnp.asarray(inv, F32)
    cos, sin = jnp.cos(ang), jnp.sin(ang)
    t = ang.shape[0]
    ones = jnp.ones((t, dim - rot), F32)
    zeros = lambda w: jnp.zeros((t, w), F32)
    reps = LANES // dim
    c = jnp.tile(jnp.concatenate([cos, cos, ones], axis=1), (1, reps))
    sa = jnp.tile(jnp.concatenate([-sin, zeros(dim - half)], axis=1), (1, reps))
    sb = jnp.tile(jnp.concatenate([zeros(half), sin, zeros(dim - rot)], axis=1), (1, reps))
    return c, sa, sb


def _relayout_w_in(w):
    pts = np.cumsum([0, 384, 64, 64, 256, 64, 4, 256, 256, 256, 384, 384, 384])
    (q_a, k_a, v_a, q_i, k_i, w_i, q_b, k_b, v_b, q_c, k_c, v_c) = [w[..., pts[j]:pts[j + 1]] for j in range(12)]
    w_i_pad = jnp.concatenate([w_i, jnp.zeros(w.shape[:-1] + (LANES - IDX_HEADS,), w.dtype)], axis=-1)
    return jnp.concatenate([q_a, k_a, k_a, v_a, v_a, q_i, k_i, k_i, w_i_pad,
                            q_b, k_b, v_b, q_c, k_c, v_c], axis=-1).astype(BF16)


def kernel(x, c, positions, w_in, w_out, diff_lambda, diff_subln, w_ada, b_ada, g_attn, g_ffn, w_ff_gate,
           w_ff_up, w_ff_down, w_router, w_exp_gate, w_exp_up, w_exp_down, g_final):
    batch, seq, d = x.shape
    depth = w_in.shape[0]
    t = batch * seq
    tabs = _rope_tables(positions, HEAD_DIM) + _rope_tables(positions, B_QK_DIM)
    mod_all = adaln_mod(c, w_ada, b_ada).reshape(depth, batch, 6, d)
    w_in_pad = _relayout_w_in(w_in)
    wo = w_out.astype(BF16)
    xf = x.reshape(t, d)
    for layer in range(depth):
        mod = mod_all[layer]
        lam_init = 0.8 - 0.6 * math.exp(-0.3 * layer)
        (qa, kka, vva, qi, kki, wi, qb, kb, vb, qc, kc, vc) = inproj(
            xf, g_attn[layer].reshape(1, d), mod, tabs, w_in_pad, layer, seq)
        ya = dsa_attention(qa, qi, wi, kka, vva, kki, batch, seq)
        g_sub2 = jnp.tile(diff_subln[layer], 2).reshape(1, LANES)
        yb = diff_attention(qb, kb, vb, diff_lambda[layer], g_sub2, lam_init, batch, seq)
        yc = moba_attention(qc, kc, vc, batch, seq)
        j = layer // 2
        gf = g_ffn[layer].reshape(1, d)
        if layer % 2 == 0:
            xf = outproj_ffn_dense(xf, ya, yb, yc, wo, layer, gf, mod,
                                   w_ff_gate[:, None], w_ff_up[:, None], w_ff_down[:, None], j, seq)
        else:
            wr = jnp.concatenate([w_router[j], jnp.zeros((d, LANES - N_EXPERTS), F32)], axis=1)
            wr_hi = wr.astype(BF16)
            wr = jnp.stack([wr_hi, (wr - wr_hi.astype(F32)).astype(BF16)])
            x1, h_packed, route, counts = outproj_router(xf, ya, yb, yc, wo, layer, gf, mod, seq, wr)
            xf = ffn_moe_sc(x1, h_packed, route, counts, mod, w_exp_gate, w_exp_up, w_exp_down, j, seq)
    return final_norm(xf, g_final.reshape(1, d)).reshape(batch, seq, d)
```

```python
import functools
import math

import jax
import jax.numpy as jnp
import numpy as np
from jax import lax
from jax.experimental import pallas as pl
from jax.experimental.pallas import tpu as pltpu
from jax.experimental.pallas import tpu_sc as plsc

F32 = jnp.float32
BF16 = jnp.bfloat16

HEAD_DIM = 64
A_HEADS = 6
IDX_HEADS = 4
B_HEADS = 4
B_QK_DIM = 32
C_HEADS = 6
A_WIDTH, IDX_WIDTH = A_HEADS * HEAD_DIM, IDX_HEADS * HEAD_DIM
B_WIDTH, C_WIDTH = B_HEADS * HEAD_DIM, C_HEADS * HEAD_DIM
DSA_TOPK_MAX = 256
MOBA_BLOCK = 256
MOBA_TOPK = 3
ROPE_THETA = 500000.0
ROPE_FRACTION = 4
SUBLN_EPS = 1e-5
EPS = 1e-6
N_EXPERTS = 8

LANES = 128
NEG = -1e30
INT_MIN = -2 ** 31
MIN_NORMAL_KEY = 0x00800000
MIN_NORMAL_F32 = float(np.float32(2.0 ** -126))
VMEM_LIMIT = 48 * 1024 * 1024
LOG2E = math.log2(math.e)

_G_QA, _G_KKA, _G_VVA, _G_QI, _G_KKI, _G_WI = (0, 384), (384, 512), (512, 640), (640, 896), (896, 1024), (1024, 1152)
_G_QB, _G_KB, _G_VB = (1152, 1408), (1408, 1664), (1664, 1920)
_G_QC, _G_KC, _G_VC = (1920, 2304), (2304, 2688), (2688, 3072)
D_IN_PAD = 3072


def _params(*sem):
    return pltpu.CompilerParams(dimension_semantics=sem, vmem_limit_bytes=VMEM_LIMIT)


def _dot(a, b):
    return jnp.dot(a, b, preferred_element_type=F32)


def _dot_nt(a, b):
    return lax.dot_general(a, b, (((1,), (1,)), ((), ())), preferred_element_type=F32)


def _adaln_kernel(c_ref, w_ref, b_ref, o_ref):
    c = c_ref[...]
    c_act = c * (1.0 / (1.0 + jnp.exp(-c)))
    o_ref[...] = jnp.dot(c_act, w_ref[...], preferred_element_type=F32,
                         precision=lax.Precision.HIGHEST) + b_ref[...]


def adaln_mod(c, w_ada, b_ada, tn=1536):
    depth, d, n = w_ada.shape
    b = c.shape[0]
    return pl.pallas_call(
        _adaln_kernel,
        grid=(depth, n // tn),
        in_specs=[pl.BlockSpec((b, d), lambda l, j: (0, 0)),
                  pl.BlockSpec((None, d, tn), lambda l, j: (l, 0, j)),
                  pl.BlockSpec((None, 1, tn), lambda l, j: (l, 0, j))],
        out_specs=pl.BlockSpec((None, b, tn), lambda l, j: (l, 0, j)),
        out_shape=jax.ShapeDtypeStruct((depth, b, n), F32),
        compiler_params=_params("parallel", "parallel"),
        name="adaln_mod",
    )(c, w_ada, b_ada.reshape(depth, 1, n))


def _norm_mod(x, g, shift, scale, eps=EPS):
    y = x * lax.rsqrt(jnp.mean(x * x, axis=-1, keepdims=True) + eps)
    return (y * g) * (1.0 + scale) + shift


def _rope_store(acc, o_ref, cos, sa, sb, half):
    for j in range(acc.shape[1] // LANES):
        a = acc[:, j * LANES:(j + 1) * LANES]
        r = a * cos + pltpu.roll(a, half, 1) * sb + pltpu.roll(a, LANES - half, 1) * sa
        o_ref[:, j * LANES:(j + 1) * LANES] = r.astype(o_ref.dtype)


def _inproj_kernel(x_ref, g_ref, mod_ref, c64_ref, sa64_ref, sb64_ref, c32_ref, sa32_ref, sb32_ref, w_ref,
                   qa_ref, kka_ref, vva_ref, qi_ref, kki_ref, wi_ref,
                   qb_ref, kb_ref, vb_ref, qc_ref, kc_ref, vc_ref):
    h = _norm_mod(x_ref[...], g_ref[...], mod_ref[0:1, :], mod_ref[1:2, :]).astype(BF16)
    c64, sa64, sb64 = c64_ref[...], sa64_ref[...], sb64_ref[...]
    c32, sa32, sb32 = c32_ref[...], sa32_ref[...], sb32_ref[...]

    def proj(cols):
        return _dot(h, w_ref[:, cols[0]:cols[1]])

    qk_scale = HEAD_DIM ** -0.5 * LOG2E
    _rope_store(proj(_G_QA), qa_ref, c64 * qk_scale, sa64 * qk_scale, sb64 * qk_scale, 8)
    _rope_store(proj(_G_KKA), kka_ref, c64, sa64, sb64, 8)
    vva_ref[...] = proj(_G_VVA).astype(vva_ref.dtype)
    _rope_store(proj(_G_QI), qi_ref, c64, sa64, sb64, 8)
    _rope_store(proj(_G_KKI), kki_ref, c64, sa64, sb64, 8)
    wi_ref[...] = proj(_G_WI) * (IDX_HEADS ** -0.5 * HEAD_DIM ** -0.5)
    b_scale = B_QK_DIM ** -0.5 * LOG2E
    _rope_store(proj(_G_QB), qb_ref, c32 * b_scale, sa32 * b_scale, sb32 * b_scale, 4)
    _rope_store(proj(_G_KB), kb_ref, c32, sa32, sb32, 4)
    vb_ref[...] = proj(_G_VB).astype(vb_ref.dtype)
    _rope_store(proj(_G_QC), qc_ref, c64 * qk_scale, sa64 * qk_scale, sb64 * qk_scale, 8)
    _rope_store(proj(_G_KC), kc_ref, c64, sa64, sb64, 8)
    vc_ref[...] = proj(_G_VC).astype(vc_ref.dtype)


def inproj(x, g, mod, tabs, w_pad, layer, seq, tm=512):
    t, d = x.shape
    tiles_per_batch = seq // tm
    row = lambda i: (i, 0)
    widths = [A_WIDTH, LANES, LANES, IDX_WIDTH, LANES, LANES, B_WIDTH, B_WIDTH, B_WIDTH, C_WIDTH, C_WIDTH, C_WIDTH]
    dtypes = [BF16, BF16, BF16, BF16, BF16, F32, BF16, BF16, BF16, BF16, BF16, BF16]
    return pl.pallas_call(
        _inproj_kernel,
        grid=(t // tm,),
        in_specs=[pl.BlockSpec((tm, d), row),
                  pl.BlockSpec((1, d), lambda i: (0, 0)),
                  pl.BlockSpec((None, 6, d), lambda i: (i // tiles_per_batch, 0, 0))]
                 + [pl.BlockSpec((tm, LANES), row)] * 6
                 + [pl.BlockSpec((None, d, D_IN_PAD), lambda i: (layer, 0, 0))],
        out_specs=[pl.BlockSpec((tm, w), row) for w in widths],
        out_shape=[jax.ShapeDtypeStruct((t, w), dt) for w, dt in zip(widths, dtypes)],
        compiler_params=_params("parallel"),
        name="inproj",
    )(x, g, mod, *tabs, w_pad)


def _init_stats(m_ref, l_ref, acc_ref):
    m_ref[...] = jnp.full(m_ref.shape, -jnp.inf, F32)
    l_ref[...] = jnp.zeros(l_ref.shape, F32)
    acc_ref[...] = jnp.zeros(acc_ref.shape, F32)


def _key_to_f32(k):
    return lax.bitcast_convert_type(jnp.where(k >= 0, k, k ^ 0x7FFFFFFF), F32)


def dsa_attention(qa, qi, wi, kka, vva, kki, batch, seq, tq=256):
    t = qa.shape[0]
    nq = seq // tq
    k_top = min(DSA_TOPK_MAX, seq // 4)
    qrow = lambda b, i: (b * nq + i, 0)
    full = lambda b, i: (b, 0)
    return pl.pallas_call(
        functools.partial(_dsa_kernel_t, tq=tq, k_top=k_top, seq=seq),
        grid=(batch, nq),
        in_specs=[pl.BlockSpec((tq, A_WIDTH), qrow), pl.BlockSpec((tq, IDX_WIDTH), qrow),
                  pl.BlockSpec((tq, LANES), qrow),
                  pl.BlockSpec((seq, LANES), full), pl.BlockSpec((seq, LANES), full),
                  pl.BlockSpec((seq, LANES), full)],
        out_specs=pl.BlockSpec((tq, A_WIDTH), qrow),
        out_shape=jax.ShapeDtypeStruct((t, A_WIDTH), BF16),
        scratch_shapes=[pltpu.VMEM((seq, tq), F32), pltpu.VMEM((seq, tq), jnp.int16),
                        pltpu.VMEM((seq, tq), jnp.int16), pltpu.VMEM((1, LANES, seq), BF16),
                        pltpu.VMEM((3, 2 * tq, LANES), BF16),
                        pltpu.VMEM((3, 1, 2 * tq), F32), pltpu.VMEM((3, 1, 2 * tq), F32),
                        pltpu.VMEM((3, LANES, 2 * tq), F32)],
        compiler_params=_params("parallel", "arbitrary"),
        name="dsa_attention",
    )(qa, qi, wi, kka, vva, kki)


def diff_attention(qb, kb, vb, lam_vec, g_sub2, lam_init, batch, seq, tq=256):
    t = qb.shape[0]
    nq = seq // tq
    qrow = lambda b, i: (b * nq + i, 0)
    full = lambda b, i: (b, 0)
    const = lambda b, i: (0, 0)
    return pl.pallas_call(
        functools.partial(_diff_kernel_t, tq=tq, lam_init=lam_init, seq=seq),
        grid=(batch, nq),
        in_specs=[pl.BlockSpec((tq, B_WIDTH), qrow), pl.BlockSpec((seq, B_WIDTH), full),
                  pl.BlockSpec((seq, B_WIDTH), full),
                  pl.BlockSpec((4, B_QK_DIM), const), pl.BlockSpec((1, LANES), const)],
        out_specs=pl.BlockSpec((tq, B_WIDTH), qrow),
        out_shape=jax.ShapeDtypeStruct((t, B_WIDTH), BF16),
        scratch_shapes=[pltpu.VMEM((2, LANES, seq), BF16), pltpu.VMEM((2, 4 * tq, LANES), BF16),
                        pltpu.VMEM((2, 1, 4 * tq), F32), pltpu.VMEM((2, 1, 4 * tq), F32),
                        pltpu.VMEM((2, LANES, 4 * tq), F32)],
        compiler_params=_params("parallel", "arbitrary"),
        name="diff_attention",
    )(qb, kb, vb, lam_vec, g_sub2)


def moba_attention(qc, kc, vc, batch, seq):
    tq = MOBA_BLOCK
    t = qc.shape[0]
    nb = seq // tq
    n_sel = min(MOBA_TOPK, nb - 1)
    nbp = 8
    assert seq % tq == 0 and nb <= nbp
    qrow = lambda b, i: (b * nb + i, 0)
    full = lambda b, i: (b, 0)
    return pl.pallas_call(
        functools.partial(_moba_kernel_t, tq=tq, nb=nb, n_sel=n_sel, seq=seq),
        grid=(batch, nb),
        in_specs=[pl.BlockSpec((tq, C_WIDTH), qrow), pl.BlockSpec((seq, C_WIDTH), full),
                  pl.BlockSpec((seq, C_WIDTH), full)],
        out_specs=pl.BlockSpec((tq, C_WIDTH), qrow),
        out_shape=jax.ShapeDtypeStruct((t, C_WIDTH), BF16),
        scratch_shapes=[pltpu.VMEM((nbp, C_WIDTH), F32), pltpu.VMEM((3, LANES, seq), BF16),
                        pltpu.VMEM((3, 2 * tq, LANES), BF16), pltpu.VMEM((3, nbp, 2 * tq), F32),
                        pltpu.VMEM((3, 1, 2 * tq), F32), pltpu.VMEM((3, 1, 2 * tq), F32),
                        pltpu.VMEM((3, LANES, 2 * tq), F32)],
        compiler_params=_params("parallel", "arbitrary"),
        name="moba_attention",
    )(qc, kc, vc)


def _attend_t(k, q_ref, g, v_t, m_ref, l_ref, acc_ref, mask=None):
    for j in range(q_ref.shape[1] // LANES):
        cols = slice(j * LANES, (j + 1) * LANES)
        s = _dot_nt(k, q_ref[g, cols, :])
        if mask is not None:
            s = mask(j, s)
        m_prev = m_ref[g, :, cols]
        m_new = jnp.maximum(m_prev, jnp.max(s, axis=0, keepdims=True))
        alpha = jnp.exp2(m_prev - m_new)
        p = jnp.exp2(s - m_new)
        l_ref[g, :, cols] = alpha * l_ref[g, :, cols] + jnp.sum(p, axis=0, keepdims=True)
        acc_ref[g, :, cols] = alpha * acc_ref[g, :, cols] + _dot(v_t, p.astype(BF16))
        m_ref[g, :, cols] = m_new


def _causal_mask_t(tk, tq):
    key = lax.broadcasted_iota(jnp.int32, (tk, LANES), 0)
    qry = lax.broadcasted_iota(jnp.int32, (tk, LANES), 1)

    def mask(j, s):
        return jnp.where(key <= qry + (j * LANES) % tq, s, NEG)
    return mask


def _transpose_values(v_ref, vt_ref, seq, tk):
    for g in range(vt_ref.shape[0]):
        for n in range(seq // tk):
            blk = v_ref[n * tk:(n + 1) * tk, g * LANES:(g + 1) * LANES].astype(F32)
            vt_ref[g, :, n * tk:(n + 1) * tk] = blk.T.astype(vt_ref.dtype)


def _head_pair_out(acc_t, l, tq):
    even = acc_t[0:HEAD_DIM, 0:tq] / l[:, 0:tq]
    odd = acc_t[HEAD_DIM:LANES, tq:2 * tq] / l[:, tq:2 * tq]
    return jnp.concatenate([even, odd], axis=0).T


def _dsa_kernel_t(qa_ref, qi_ref, wiq_ref, kka_ref, vva_ref, kki_ref, o_ref,
                  sc_ref, hi_ref, lo_ref, vt_ref, qs_ref, m_ref, l_ref, acc_ref, *, tq, k_top, seq):
    i = pl.program_id(1)
    nk = i + 1
    t0 = i * tq
    lo = lax.broadcasted_iota(jnp.int32, (tq, LANES), 1) < HEAD_DIM
    zero_b = jnp.zeros((tq, LANES), BF16)

    @pl.when(i == 0)
    def _():
        _transpose_values(vva_ref, vt_ref, seq, tq)

    def stack_heads(q):
        out = []
        for g in range(q.shape[1] // LANES):
            qg = q[:, g * LANES:(g + 1) * LANES]
            out += [jnp.where(lo, qg, zero_b), jnp.where(lo, zero_b, qg)]
        return out

    qa_stack = stack_heads(qa_ref[...])
    for g in range(3):
        qs_ref[g, 0:tq, :] = qa_stack[2 * g]
        qs_ref[g, tq:2 * tq, :] = qa_stack[2 * g + 1]
    qi_stack = jnp.concatenate(stack_heads(qi_ref[...]), axis=0)
    wi_t = wiq_ref[...].T

    def rows(c):
        return pl.ds(pl.multiple_of(c * tq, tq), tq)

    key_pos = lax.broadcasted_iota(jnp.int32, (tq, tq), 0)
    qry_pos = lax.broadcasted_iota(jnp.int32, (tq, tq), 1)

    def idx_body(c, carry):
        r = jnp.maximum(_dot_nt(kki_ref[rows(c), :], qi_stack), 0.0)
        s = wi_t[0:1, :] * r[:, 0:tq]
        for h in range(1, IDX_HEADS):
            s = s + wi_t[h:h + 1, :] * r[:, h * tq:(h + 1) * tq]
        causal = (c * tq + key_pos) <= (t0 + qry_pos)
        s = jnp.where(causal, s, -jnp.inf)
        s = jnp.where(jnp.abs(s) < MIN_NORMAL_F32, 0.0, s)
        sc_ref[rows(c), :] = s
        bits = lax.bitcast_convert_type(s, jnp.int32)
        key = jnp.where(bits >= 0, bits, bits ^ 0x7FFFFFFF)
        hi_ref[rows(c), :] = lax.shift_right_arithmetic(key, jnp.int32(16)).astype(jnp.int16)
        lo_ref[rows(c), :] = ((key & 0xFFFF) - 32768).astype(jnp.int16)
        return carry

    lax.fori_loop(0, nk, idx_body, 0)

    def count16(ref, cand, strict):
        c16 = cand.astype(jnp.int16)

        def body(c, acc):
            x = ref[rows(c), :]
            hit = jnp.where((x > c16) if strict else (x >= c16), jnp.int16(1), jnp.int16(0))
            for r in range(tq // 16):
                acc = acc + hit[r * 16:(r + 1) * 16]
            return acc
        acc = lax.fori_loop(0, nk, body, jnp.zeros((16, tq), jnp.int16))
        return jnp.sum(acc.astype(F32), axis=0, keepdims=True)

    def search16(ref, k_need):
        v0 = jnp.where(count16(ref, jnp.zeros((1, tq), jnp.int32), False) >= k_need, 0, -32768).astype(jnp.int32)

        def bisect(b, v):
            trial = v | lax.shift_left(jnp.int32(1), 14 - b)
            return jnp.where(count16(ref, trial, False) >= k_need, trial, v)
        return lax.fori_loop(0, 15, bisect, v0)

    kf = float(k_top)

    def search():
        hi_k = search16(hi_ref, kf)
        above = count16(hi_ref, hi_k, True)
        hi16 = hi_k.astype(jnp.int16)

        def keep_bucket(c, carry):
            lo_ref[rows(c), :] = jnp.where(hi_ref[rows(c), :] == hi16, lo_ref[rows(c), :], jnp.int16(-32768))
            return carry
        lax.fori_loop(0, nk, keep_bucket, 0)
        lo_k = search16(lo_ref, kf - above)
        n_gt = above + count16(lo_ref, lo_k, True)
        return lax.shift_left(hi_k, jnp.int32(16)) | (lo_k + 32768), kf - n_gt

    key, need = lax.cond(t0 + tq <= k_top,
                         lambda: (jnp.full((1, tq), INT_MIN, jnp.int32), jnp.full((1, tq), kf, F32)), search)
    thr = _key_to_f32(key)
    thr_next = _key_to_f32(jnp.where(key == 0, MIN_NORMAL_KEY, key + 1))
    all_sel = (t0 + lax.broadcasted_iota(jnp.int32, (1, tq), 1)) < k_top
    lower = (qry_pos <= key_pos).astype(BF16)

    _init_stats(m_ref, l_ref, acc_ref)

    def att_body(c, tie_run):
        s_idx = sc_ref[rows(c), :]
        ge = s_idx >= thr
        gt = s_idx >= thr_next
        tie = jnp.where(gt, 0.0, jnp.where(ge, 1.0, 0.0))
        prefix = _dot(lower, tie.astype(BF16)) + tie_run
        take = jnp.where(gt, 1.0, jnp.where(prefix <= need, tie, 0.0))
        take = jnp.where(all_sel, 1.0, take)
        take = jnp.where(s_idx > -jnp.inf, take, 0.0)

        def mask(j, s):
            q0 = (j * LANES) % tq
            return jnp.where(take[:, q0:q0 + LANES] > 0.5, s, NEG)

        kk = kka_ref[rows(c), :]
        vt = vt_ref[0, :, rows(c)]
        for g in range(3):
            _attend_t(kk, qs_ref, g, vt, m_ref, l_ref, acc_ref, mask)
        return tie_run + jnp.sum(tie, axis=0, keepdims=True)

    lax.fori_loop(0, nk, att_body, jnp.zeros((1, tq), F32))

    for g in range(3):
        o_ref[:, g * LANES:(g + 1) * LANES] = _head_pair_out(acc_ref[g], l_ref[g], tq).astype(o_ref.dtype)


def _diff_kernel_t(qb_ref, kb_ref, vb_ref, lam_ref, gsub_ref, o_ref,
                   vt_ref, qs_ref, m_ref, l_ref, acc_ref, *, tq, lam_init, seq):
    i = pl.program_id(1)
    lane = lax.broadcasted_iota(jnp.int32, (tq, LANES), 1)
    zero_b = jnp.zeros((tq, LANES), BF16)

    @pl.when(i == 0)
    def _():
        _transpose_values(vb_ref, vt_ref, seq, tq)

    qb = qb_ref[...]
    for g in range(2):
        qg = qb[:, g * LANES:(g + 1) * LANES]
        for j in range(4):
            qs_ref[g, j * tq:(j + 1) * tq, :] = jnp.where(lane // B_QK_DIM == j, qg, zero_b)
    _init_stats(m_ref, l_ref, acc_ref)

    def rows(c):
        return pl.ds(pl.multiple_of(c * tq, tq), tq)

    def step(c, masked):
        mask = _causal_mask_t(tq, tq) if masked else None
        for g in range(2):
            _attend_t(kb_ref[rows(c), g * LANES:(g + 1) * LANES], qs_ref, g, vt_ref[g, :, rows(c)],
                      m_ref, l_ref, acc_ref, mask)

    def body(c, carry):
        step(c, False)
        return carry

    lax.fori_loop(0, i, body, 0)
    step(i, True)

    lv = lam_ref[...]
    lam = (jnp.exp(jnp.sum(lv[0:1] * lv[1:2], axis=1, keepdims=True))
           - jnp.exp(jnp.sum(lv[2:3] * lv[3:4], axis=1, keepdims=True)) + lam_init)
    gsub = gsub_ref[...]
    for g in range(2):
        acc_t, l = acc_ref[g], l_ref[g]

        def prob(r0, j):
            return acc_t[r0:r0 + HEAD_DIM, j * tq:(j + 1) * tq] / l[:, j * tq:(j + 1) * tq]

        halves = []
        for r0, j in ((0, 0), (HEAD_DIM, 2)):
            o = prob(r0, j) - lam * prob(r0, j + 1)
            ms = jnp.mean(o * o, axis=0, keepdims=True)
            halves.append(o * lax.rsqrt(ms + SUBLN_EPS))
        y = (jnp.concatenate(halves, axis=0).T * gsub) * (1.0 - lam_init)
        o_ref[:, g * LANES:(g + 1) * LANES] = y.astype(o_ref.dtype)


def _moba_kernel_t(qc_ref, kc_ref, vc_ref, o_ref, kmean_ref, vt_ref, qs_ref, bias_ref, m_ref, l_ref, acc_ref,
                   *, tq, nb, n_sel, seq):
    i = pl.program_id(1)
    nbp = kmean_ref.shape[0]
    lane = lax.broadcasted_iota(jnp.int32, (tq, LANES), 1)
    lo = lane < HEAD_DIM
    zero_b = jnp.zeros((tq, LANES), BF16)

    @pl.when(i == 0)
    def _():
        _transpose_values(vc_ref, vt_ref, seq, tq)
        kmean_ref[...] = jnp.zeros(kmean_ref.shape, F32)
        for n in range(nb):
            kblk = kc_ref[n * tq:(n + 1) * tq, :].astype(F32)
            kmean_ref[n:n + 1, :] = jnp.mean(kblk, axis=0, keepdims=True)

    qc = qc_ref[...]
    sub = lax.broadcasted_iota(jnp.int32, (nbp, 2 * tq), 0)
    past = sub < i
    for g in range(3):
        qg = qc[:, g * LANES:(g + 1) * LANES]
        q2 = jnp.concatenate([jnp.where(lo, qg, zero_b), jnp.where(lo, zero_b, qg)], axis=0)
        km = kmean_ref[:, g * LANES:(g + 1) * LANES]
        km_hi = km.astype(BF16)
        gt = _dot_nt(km_hi, q2) + _dot_nt((km - km_hi.astype(F32)).astype(BF16), q2)
        rows_ = []
        for n in range(nbp):
            gn = gt[n:n + 1, :]
            beats = jnp.where(sub < n, jnp.where(gt >= gn, 1.0, 0.0), jnp.where(gt > gn, 1.0, 0.0))
            beats = jnp.where(sub == n, 0.0, jnp.where(past, beats, 0.0))
            rank = jnp.sum(beats, axis=0, keepdims=True)
            rows_.append(jnp.where(rank < n_sel, 0.0, NEG))
        bias_ref[g] = jnp.where(past, jnp.concatenate(rows_, axis=0), NEG)
        qs_ref[g] = q2
    _init_stats(m_ref, l_ref, acc_ref)

    def rows(c):
        return pl.ds(pl.multiple_of(c * tq, tq), tq)

    def body(c, carry):
        for g in range(3):
            bias = bias_ref[g, pl.ds(c, 1), :]

            def mask(j, s):
                return s + bias[:, j * LANES:(j + 1) * LANES]

            _attend_t(kc_ref[rows(c), g * LANES:(g + 1) * LANES], qs_ref, g, vt_ref[g, :, rows(c)],
                      m_ref, l_ref, acc_ref, mask)
        return carry

    lax.fori_loop(0, i, body, 0)

    causal = _causal_mask_t(tq, tq)
    for g in range(3):
        _attend_t(kc_ref[rows(i), g * LANES:(g + 1) * LANES], qs_ref, g, vt_ref[g, :, rows(i)],
                  m_ref, l_ref, acc_ref, causal)
        o_ref[:, g * LANES:(g + 1) * LANES] = _head_pair_out(acc_ref[g], l_ref[g], tq).astype(o_ref.dtype)


def _route(h, wr_ref, cw_ref, meta_ref, cnt_ref):
    hi = h.astype(BF16)
    lo = (h - hi.astype(F32)).astype(BF16)
    logits = _dot(hi, wr_ref[0]) + (_dot(lo, wr_ref[0]) + _dot(hi, wr_ref[1]))
    lane = lax.broadcasted_iota(jnp.int32, logits.shape, 1)
    lg = jnp.where(lane < N_EXPERTS, logits, -jnp.inf)
    v0 = jnp.max(lg, axis=1, keepdims=True)
    i0 = jnp.min(jnp.where(lg == v0, lane, LANES), axis=1, keepdims=True)
    lg1 = jnp.where(lane == i0, -jnp.inf, lg)
    v1 = jnp.max(lg1, axis=1, keepdims=True)
    i1 = jnp.min(jnp.where(lg1 == v1, lane, LANES), axis=1, keepdims=True)
    e1 = jnp.exp(v1 - v0)
    w0 = 1.0 / (1.0 + e1)
    tm = logits.shape[0]
    routed = jnp.where(lane == i0, 1.0, 0.0) + jnp.where(lane == i1, 1.0, 0.0)
    earlier_rows = (lax.broadcasted_iota(jnp.int32, (tm, tm), 1)
                    < lax.broadcasted_iota(jnp.int32, (tm, tm), 0)).astype(BF16)

    @pl.when(pl.program_id(0) == 0)
    def _():
        cnt_ref[...] = jnp.zeros(cnt_ref.shape, F32)

    before = _dot(earlier_rows, routed.astype(BF16)) + cnt_ref[0:1, :]
    r0 = jnp.sum(jnp.where(lane == i0, before, 0.0), axis=1, keepdims=True)
    r1 = jnp.sum(jnp.where(lane == i1, before, 0.0), axis=1, keepdims=True)
    cnt_ref[0:1, :] = cnt_ref[0:1, :] + jnp.sum(routed, axis=0, keepdims=True)
    cw = (jnp.where(lane == 0, i0.astype(F32), 0.0) + jnp.where(lane == 1, i1.astype(F32), 0.0)
          + jnp.where(lane == 2, w0, 0.0) + jnp.where(lane == 3, e1 * w0, 0.0)
          + jnp.where(lane == 4, r0, 0.0) + jnp.where(lane == 5, r1, 0.0))
    cw_ref[...] = cw
    meta_ref[...] = cw.T[0:8, :].astype(jnp.int32)


def _mixer_out(x_ref, ya_ref, yb_ref, yc_ref, wo_ref, g_ref, mod_ref):
    ab = A_WIDTH + B_WIDTH
    y = (_dot(ya_ref[...], wo_ref[0:A_WIDTH, :]) + _dot(yb_ref[...], wo_ref[A_WIDTH:ab, :])
         + _dot(yc_ref[...], wo_ref[ab:ab + C_WIDTH, :]))
    x1 = x_ref[...] + mod_ref[2:3, :] * y
    return x1, _norm_mod(x1, g_ref[...], mod_ref[3:4, :], mod_ref[4:5, :])


def _mixer_out_specs(tm, d, layer, tiles_per_batch):
    row = lambda i: (i, 0)
    return [pl.BlockSpec((tm, d), row), pl.BlockSpec((tm, A_WIDTH), row), pl.BlockSpec((tm, B_WIDTH), row),
            pl.BlockSpec((tm, C_WIDTH), row), pl.BlockSpec((None, d, d), lambda i: (layer, 0, 0)),
            pl.BlockSpec((1, d), lambda i: (0, 0)),
            pl.BlockSpec((None, 6, d), lambda i: (i // tiles_per_batch, 0, 0))]


def _outproj_kernel(x_ref, ya_ref, yb_ref, yc_ref, wo_ref, g_ref, mod_ref, wr_ref,
                    x1_ref, h_ref, cw_ref, meta_ref, cnt_ref):
    x1, h = _mixer_out(x_ref, ya_ref, yb_ref, yc_ref, wo_ref, g_ref, mod_ref)
    x1_ref[...] = x1
    h_ref[...] = _pack_rows(h)
    _route(h, wr_ref, cw_ref, meta_ref, cnt_ref)


def outproj_router(x, ya, yb, yc, wo, layer, g_ffn, mod, seq, w_router, tm=512):
    t, d = x.shape
    row = lambda i: (i, 0)
    return pl.pallas_call(
        _outproj_kernel,
        grid=(t // tm,),
        in_specs=_mixer_out_specs(tm, d, layer, seq // tm) + [pl.BlockSpec((2, d, LANES), lambda i: (0, 0, 0))],
        out_specs=[pl.BlockSpec((tm, d), row), pl.BlockSpec((tm, d // 2), row), pl.BlockSpec((tm, LANES), row),
                   pl.BlockSpec((8, tm), lambda i: (0, i)), pl.BlockSpec((8, LANES), lambda i: (0, 0))],
        out_shape=[jax.ShapeDtypeStruct((t, d), F32), jax.ShapeDtypeStruct((t, d // 2), jnp.int32),
                   jax.ShapeDtypeStruct((t, LANES), F32), jax.ShapeDtypeStruct((8, t), jnp.int32),
                   jax.ShapeDtypeStruct((8, LANES), F32)],
        compiler_params=_params("arbitrary"),
        name="outproj_router",
    )(x, ya, yb, yc, wo, g_ffn, mod, w_router)


W_ROWS_IN = 128
W_ROWS_DOWN = 352
FF_SPLIT = 2


def _load_weights_bf16(layer, e, w_hbms, w_bs, st_in, st_dn, sems):
    slots = st_in.shape[0]
    plan = []
    for w_hbm, w_b, rb, st, sem0 in ((w_hbms[0], w_bs[0], W_ROWS_IN, st_in, 0), (w_hbms[1], w_bs[1], W_ROWS_IN, st_in, 0),
                                     (w_hbms[2], w_bs[2], W_ROWS_DOWN, st_dn, slots)):
        assert w_b.shape[0] % rb == 0
        plan += [(w_hbm, w_b, r0, rb, st, sem0) for r0 in range(0, w_b.shape[0], rb)]
    uses = {0: 0, slots: 0}
    copies = []
    for w_hbm, w_b, r0, rb, st, sem0 in plan:
        slot = uses[sem0] % slots
        uses[sem0] += 1
        copies.append((pltpu.make_async_copy(w_hbm.at[layer, e, pl.ds(r0, rb), :], st.at[slot],
                                             sems.at[sem0 + slot]), st, slot, w_b, r0, rb))
    ahead = slots - 1
    for copy in copies[:ahead]:
        copy[0].start()
    for b, (copy, st, slot, w_b, r0, rb) in enumerate(copies):
        if b + ahead < len(copies):
            copies[b + ahead][0].start()
        copy.wait()
        w_b[r0:r0 + rb, :] = st[slot].astype(BF16)


def _swiglu_resident(x, wg_b, wu_b, wd_b):
    tf = wg_b.shape[1] // FF_SPLIT
    acc = None
    for f in range(FF_SPLIT):
        cols = slice(f * tf, (f + 1) * tf)
        a = _dot(x, wg_b[:, cols])
        u = _dot(x, wu_b[:, cols])
        act = (a * (1.0 / (1.0 + jnp.exp(-a)))) * u
        part = _dot(act.astype(BF16), wd_b[cols, :])
        acc = part if acc is None else acc + part
    return acc


def _weight_scratch(d, dff, slots):
    return [pltpu.VMEM((d, dff), BF16), pltpu.VMEM((d, dff), BF16), pltpu.VMEM((dff, d), BF16),
            pltpu.VMEM((slots, W_ROWS_IN, dff), F32), pltpu.VMEM((slots, W_ROWS_DOWN, d), F32),
            pltpu.SemaphoreType.DMA((2 * slots,))]


def _ffn_kernel(x_ref, ya_ref, yb_ref, yc_ref, wo_ref, g_ref, mod_ref, wg_hbm, wu_hbm, wd_hbm, o_ref,
                wg_b, wu_b, wd_b, st_in, st_dn, sems, *, layer):
    @pl.when(pl.program_id(0) == 0)
    def _():
        _load_weights_bf16(layer, 0, (wg_hbm, wu_hbm, wd_hbm), (wg_b, wu_b, wd_b), st_in, st_dn, sems)

    x1, h = _mixer_out(x_ref, ya_ref, yb_ref, yc_ref, wo_ref, g_ref, mod_ref)
    o_ref[...] = x1 + mod_ref[5:6, :] * _swiglu_resident(h.astype(BF16), wg_b, wu_b, wd_b)


def outproj_ffn_dense(x, ya, yb, yc, wo, layer, g_ffn, mod, w_gate, w_up, w_down, ff_layer, seq, tm=512):
    t, d = x.shape
    hbm = pl.BlockSpec(memory_space=pl.ANY)
    return pl.pallas_call(
        functools.partial(_ffn_kernel, layer=ff_layer),
        grid=(t // tm,),
        in_specs=_mixer_out_specs(tm, d, layer, seq // tm) + [hbm, hbm, hbm],
        out_specs=pl.BlockSpec((tm, d), lambda i: (i, 0)),
        out_shape=jax.ShapeDtypeStruct((t, d), F32),
        scratch_shapes=_weight_scratch(d, w_gate.shape[3], 2),
        compiler_params=_params("arbitrary"),
        name="ffn_dense",
    )(x, ya, yb, yc, wo, g_ffn, mod, w_gate, w_up, w_down)


MOE_TILE = 512
MOE_PARTS = 3
SC_CORES, SC_SUBCORES = 2, 16
SC_ROWS = 64
HI16 = -65536


def _pack_rows(x):
    c = x.shape[1] // 2
    bits = lax.bitcast_convert_type(x.astype(jnp.bfloat16).astype(F32), jnp.int32)
    return lax.shift_right_logical(bits[:, :c], jnp.int32(16)) | (bits[:, c:] & jnp.int32(HI16))


def _unpack_rows(w):
    lo = lax.bitcast_convert_type(lax.shift_left(w, jnp.int32(16)), F32)
    hi = lax.bitcast_convert_type(w & jnp.int32(HI16), F32)
    return jnp.concatenate([lo, hi], axis=1)


def sc_gather_rows(table, idx):
    d = table.shape[1]
    b = idx.shape[0]
    per_worker = b // (SC_CORES * SC_SUBCORES)
    assert per_worker * SC_CORES * SC_SUBCORES == b and per_worker % (2 * SC_ROWS) == 0
    mesh = plsc.VectorSubcoreMesh(core_axis_name="c", subcore_axis_name="s")
    idx_buf = pltpu.VMEM((SC_ROWS,), jnp.int32)
    row_buf = pltpu.VMEM((SC_ROWS, d), table.dtype)

    @functools.partial(
        pl.kernel, mesh=mesh, out_type=jax.ShapeDtypeStruct((b, d), table.dtype),
        scratch_types=[idx_buf, idx_buf, row_buf, row_buf] + [pltpu.SemaphoreType.DMA] * 4,
        name="sc_gather_rows")
    def gather(table_hbm, idx_hbm, out_hbm, idx0, idx1, rows0, rows1, sem_g0, sem_g1, sem_w0, sem_w1):
        base = (lax.axis_index("s") * SC_CORES + lax.axis_index("c")) * per_worker

        @pl.loop(0, per_worker // (2 * SC_ROWS))
        def _(pair):
            off0 = pl.multiple_of(base + pair * (2 * SC_ROWS), SC_ROWS)
            off1 = pl.multiple_of(off0 + SC_ROWS, SC_ROWS)
            pltpu.sync_copy(idx_hbm.at[pl.ds(off0, SC_ROWS)], idx0)
            pltpu.sync_copy(idx_hbm.at[pl.ds(off1, SC_ROWS)], idx1)
            gather0 = pltpu.async_copy(table_hbm.at[idx0], rows0, sem_g0)
            gather1 = pltpu.async_copy(table_hbm.at[idx1], rows1, sem_g1)
            gather0.wait()
            write0 = pltpu.async_copy(rows0, out_hbm.at[pl.ds(off0, SC_ROWS)], sem_w0)
            gather1.wait()
            write1 = pltpu.async_copy(rows1, out_hbm.at[pl.ds(off1, SC_ROWS)], sem_w1)
            write0.wait()
            write1.wait()

    return gather(table, idx)


def _moe_layout(meta, counts, n_exp):
    t = meta.shape[1]
    e0, e1, rank0, rank1 = meta[0], meta[1], meta[4], meta[5]
    padded = (counts + MOE_TILE - 1) // MOE_TILE * MOE_TILE
    ends = jnp.cumsum(padded)
    offs = ends - padded
    pos0 = offs[e0] + rank0
    pos1 = offs[e1] + rank1
    n_rows = 2 * t + n_exp * MOE_TILE
    tok = jnp.arange(t, dtype=jnp.int32)
    src = jnp.zeros((n_rows,), jnp.int32).at[jnp.concatenate([pos0, pos1])].set(
        jnp.concatenate([tok, tok]), unique_indices=True, mode="promise_in_bounds")
    n_tiles = n_rows // MOE_TILE
    n_valid = ends[-1] // MOE_TILE
    tile_expert = jnp.sum(jnp.arange(n_tiles)[:, None] * MOE_TILE >= ends[None, :], axis=1)
    tile_expert = tile_expert[jnp.minimum(jnp.arange(n_tiles), n_valid - 1)].astype(jnp.int32)
    return pos0, pos1, src, tile_expert, n_valid.astype(jnp.int32).reshape(1)


W_SLOTS = 4


def _experts_packed_kernel(te_ref, nv_ref, first_ref, x_ref, wg_hbm, wu_hbm, wd_hbm, *rest, layer):
    o_ref, wg_b, wu_b, wd_b, st_in, st_dn, sems = rest[-7:]
    j = pl.program_id(0)
    valid = j < nv_ref[0]

    @pl.when(valid & (first_ref[j] == 1))
    def _():
        _load_weights_bf16(layer, te_ref[j], (wg_hbm, wu_hbm, wd_hbm), (wg_b, wu_b, wd_b), st_in, st_dn, sems)

    @pl.when(valid)
    def _():
        o_ref[...] = _pack_rows(_swiglu_resident(_unpack_rows(x_ref[...]).astype(BF16), wg_b, wu_b, wd_b))


def moe_experts_packed(x_part, tile_expert, n_valid, w_gate, w_up, w_down, layer, y_prev, part, n_rows):
    rows_part, half = x_part.shape
    d = 2 * half
    dff = w_gate.shape[3]
    tiles_part = rows_part // MOE_TILE
    first = jnp.concatenate([jnp.ones((1,), jnp.int32),
                             (tile_expert[1:] != tile_expert[:-1]).astype(jnp.int32)])

    def tile(j, nv):
        return jnp.maximum(jnp.minimum(j, nv[0] - 1), 0)

    hbm = pl.BlockSpec(memory_space=pl.ANY)
    in_specs = [pl.BlockSpec((MOE_TILE, half), lambda j, te, nv, fi: (tile(j, nv), 0)), hbm, hbm, hbm]
    args = [tile_expert, n_valid, first, x_part, w_gate, w_up, w_down]
    aliases = {}
    if y_prev is not None:
        in_specs.append(hbm)
        args.append(y_prev)
        aliases = {len(args) - 1: 0}
    return pl.pallas_call(
        functools.partial(_experts_packed_kernel, layer=layer),
        grid_spec=pltpu.PrefetchScalarGridSpec(
            num_scalar_prefetch=3, grid=(tiles_part,), in_specs=in_specs,
            out_specs=pl.BlockSpec((MOE_TILE, half), lambda j, te, nv, fi: (part * tiles_part + tile(j, nv), 0)),
            scratch_shapes=_weight_scratch(d, dff, W_SLOTS)),
        out_shape=jax.ShapeDtypeStruct((n_rows, half), jnp.int32),
        input_output_aliases=aliases,
        compiler_params=_params("arbitrary"),
        name="moe_experts",
    )(*args)


def _combine_packed_kernel(y0_ref, y1_ref, route_ref, x1_ref, mod_ref, o_ref):
    rt = route_ref[...]
    f = rt[:, 2:3] * _unpack_rows(y0_ref[...]) + rt[:, 3:4] * _unpack_rows(y1_ref[...])
    o_ref[...] = x1_ref[...] + mod_ref[5:6, :] * f


def moe_combine_packed(y_pairs, route, x1, mod, seq, tm=512):
    t, d = x1.shape
    nt = t // tm
    tiles_per_batch = seq // tm
    row = lambda i: (i, 0)
    return pl.pallas_call(
        _combine_packed_kernel,
        grid=(nt,),
        in_specs=[pl.BlockSpec((tm, d // 2), row), pl.BlockSpec((tm, d // 2), lambda i: (i + nt, 0)),
                  pl.BlockSpec((tm, LANES), row), pl.BlockSpec((tm, d), row),
                  pl.BlockSpec((None, 6, d), lambda i: (i // tiles_per_batch, 0, 0))],
        out_specs=pl.BlockSpec((tm, d), row),
        out_shape=jax.ShapeDtypeStruct((t, d), F32),
        compiler_params=_params("parallel"),
        name="moe_combine",
    )(y_pairs, y_pairs, route, x1, mod)


def ffn_moe_sc(x1, h_packed, route, meta, counts, mod, w_gate, w_up, w_down, layer, seq):
    n_exp = w_gate.shape[1]
    pos0, pos1, src, tile_expert, n_valid = _moe_layout(meta, counts[0, :n_exp].astype(jnp.int32), n_exp)
    n_rows = src.shape[0]
    rows_part = n_rows // MOE_PARTS
    tiles_part = rows_part // MOE_TILE
    assert rows_part * MOE_PARTS == n_rows and tiles_part * MOE_TILE == rows_part
    h_parts = [sc_gather_rows(h_packed, src[p * rows_part:(p + 1) * rows_part]) for p in range(MOE_PARTS)]
    y_sorted = None
    for p in range(MOE_PARTS):
        n_valid_part = jnp.clip(n_valid - p * tiles_part, 0, tiles_part)
        y_sorted = moe_experts_packed(h_parts[p], tile_expert[p * tiles_part:(p + 1) * tiles_part], n_valid_part,
                                      w_gate, w_up, w_down, layer, y_sorted, p, n_rows)
    y_pairs = sc_gather_rows(y_sorted, jnp.concatenate([pos0, pos1]))
    return moe_combine_packed(y_pairs, route, x1, mod, seq)


def _final_norm_kernel(x_ref, g_ref, o_ref):
    x = x_ref[...]
    o_ref[...] = (x * lax.rsqrt(jnp.mean(x * x, axis=-1, keepdims=True) + EPS)) * g_ref[...]


def final_norm(x, g, tm=512):
    t, d = x.shape
    return pl.pallas_call(
        _final_norm_kernel,
        grid=(t // tm,),
        in_specs=[pl.BlockSpec((tm, d), lambda i: (i, 0)), pl.BlockSpec((1, d), lambda i: (0, 0))],
        out_specs=pl.BlockSpec((tm, d), lambda i: (i, 0)),
        out_shape=jax.ShapeDtypeStruct((t, d), F32),
        compiler_params=_params("parallel"),
        name="final_norm",
    )(x, g)


def _rope_tables(positions, dim):
    rot = dim // ROPE_FRACTION
    half = rot // 2
    inv = 1.0 / (ROPE_THETA ** (np.arange(0, rot, 2, dtype=np.float32) / rot))
    ang = positions.reshape(-1).astype(F32)[:, None] * jnp.asarray(inv, F32)
    cos, sin = jnp.cos(ang), jnp.sin(ang)
    t = ang.shape[0]
    ones = jnp.ones((t, dim - rot), F32)
    zeros = lambda w: jnp.zeros((t, w), F32)
    reps = LANES // dim
    c = jnp.tile(jnp.concatenate([cos, cos, ones], axis=1), (1, reps))
    sa = jnp.tile(jnp.concatenate([-sin, zeros(dim - half)], axis=1), (1, reps))
    sb = jnp.tile(jnp.concatenate([zeros(half), sin, zeros(dim - rot)], axis=1), (1, reps))
    return c, sa, sb


def _relayout_w_in(w):
    pts = np.cumsum([0, 384, 64, 64, 256, 64, 4, 256, 256, 256, 384, 384, 384])
    (q_a, k_a, v_a, q_i, k_i, w_i, q_b, k_b, v_b, q_c, k_c, v_c) = [w[..., pts[j]:pts[j + 1]] for j in range(12)]
    w_i_pad = jnp.concatenate([w_i, jnp.zeros(w.shape[:-1] + (LANES - IDX_HEADS,), w.dtype)], axis=-1)
    return jnp.concatenate([q_a, k_a, k_a, v_a, v_a, q_i, k_i, k_i, w_i_pad,
                            q_b, k_b, v_b, q_c, k_c, v_c], axis=-1).astype(BF16)


def kernel(x, c, positions, w_in, w_out, diff_lambda, diff_subln, w_ada, b_ada, g_attn, g_ffn, w_ff_gate,
           w_ff_up, w_ff_down, w_router, w_exp_gate, w_exp_up, w_exp_down, g_final):
    batch, seq, d = x.shape
    depth = w_in.shape[0]
    t = batch * seq
    tabs = _rope_tables(positions, HEAD_DIM) + _rope_tables(positions, B_QK_DIM)
    mod_all = adaln_mod(c, w_ada, b_ada).reshape(depth, batch, 6, d)
    w_in_pad = _relayout_w_in(w_in)
    wo = w_out.astype(BF16)
    xf = x.reshape(t, d)
    for layer in range(depth):
        mod = mod_all[layer]
        lam_init = 0.8 - 0.6 * math.exp(-0.3 * layer)
        (qa, kka, vva, qi, kki, wi, qb, kb, vb, qc, kc, vc) = inproj(
            xf, g_attn[layer].reshape(1, d), mod, tabs, w_in_pad, layer, seq)
        ya = dsa_attention(qa, qi, wi, kka, vva, kki, batch, seq)
        g_sub2 = jnp.tile(diff_subln[layer], 2).reshape(1, LANES)
        yb = diff_attention(qb, kb, vb, diff_lambda[layer], g_sub2, lam_init, batch, seq)
        yc = moba_attention(qc, kc, vc, batch, seq)
        j = layer // 2
        gf = g_ffn[layer].reshape(1, d)
        if layer % 2 == 0:
            xf = outproj_ffn_dense(xf, ya, yb, yc, wo, layer, gf, mod,
                                   w_ff_gate[:, None], w_ff_up[:, None], w_ff_down[:, None], j, seq)
        else:
            wr = jnp.concatenate([w_router[j], jnp.zeros((d, LANES - N_EXPERTS), F32)], axis=1)
            wr_hi = wr.astype(BF16)
            wr = jnp.stack([wr_hi, (wr - wr_hi.astype(F32)).astype(BF16)])
            x1, h_packed, route, meta, counts = outproj_router(xf, ya, yb, yc, wo, layer, gf, mod, seq, wr)
            xf = ffn_moe_sc(x1, h_packed, route, meta, counts, mod, w_exp_gate, w_exp_up, w_exp_down, j, seq)
    return final_norm(xf, g_final.reshape(1, d)).reshape(batch, seq, d)
```

```python
import functools
import math

import jax
import jax.numpy as jnp
import numpy as np
from jax import lax
from jax.experimental import pallas as pl
from jax.experimental.pallas import tpu as pltpu
from jax.experimental.pallas import tpu_sc as plsc

F32 = jnp.float32
BF16 = jnp.bfloat16

HEAD_DIM = 64
A_HEADS = 6
IDX_HEADS = 4
B_HEADS = 4
B_QK_DIM = 32
C_HEADS = 6
A_WIDTH, IDX_WIDTH = A_HEADS * HEAD_DIM, IDX_HEADS * HEAD_DIM
B_WIDTH, C_WIDTH = B_HEADS * HEAD_DIM, C_HEADS * HEAD_DIM
DSA_TOPK_MAX = 256
MOBA_BLOCK = 256
MOBA_TOPK = 3
ROPE_THETA = 500000.0
ROPE_FRACTION = 4
SUBLN_EPS = 1e-5
EPS = 1e-6
N_EXPERTS = 8

LANES = 128
NEG = -1e30
INT_MIN = -2 ** 31
MIN_NORMAL_KEY = 0x00800000
MIN_NORMAL_F32 = float(np.float32(2.0 ** -126))
VMEM_LIMIT = 48 * 1024 * 1024
LOG2E = math.log2(math.e)

_G_QA, _G_KKA, _G_VVA, _G_QI, _G_KKI, _G_WI = (0, 384), (384, 512), (512, 640), (640, 896), (896, 1024), (1024, 1152)
_G_QB, _G_KB, _G_VB = (1152, 1408), (1408, 1664), (1664, 1920)
_G_QC, _G_KC, _G_VC = (1920, 2304), (2304, 2688), (2688, 3072)
D_IN_PAD = 3072


def _params(*sem):
    return pltpu.CompilerParams(dimension_semantics=sem, vmem_limit_bytes=VMEM_LIMIT)


def _dot(a, b):
    return jnp.dot(a, b, preferred_element_type=F32)


def _dot_nt(a, b):
    return lax.dot_general(a, b, (((1,), (1,)), ((), ())), preferred_element_type=F32)


def _adaln_kernel(c_ref, w_ref, b_ref, o_ref):
    c = c_ref[...]
    c_act = c * (1.0 / (1.0 + jnp.exp(-c)))
    o_ref[...] = jnp.dot(c_act, w_ref[...], preferred_element_type=F32,
                         precision=lax.Precision.HIGHEST) + b_ref[...]


def adaln_mod(c, w_ada, b_ada, tn=1536):
    depth, d, n = w_ada.shape
    b = c.shape[0]
    return pl.pallas_call(
        _adaln_kernel,
        grid=(depth, n // tn),
        in_specs=[pl.BlockSpec((b, d), lambda l, j: (0, 0)),
                  pl.BlockSpec((None, d, tn), lambda l, j: (l, 0, j)),
                  pl.BlockSpec((None, 1, tn), lambda l, j: (l, 0, j))],
        out_specs=pl.BlockSpec((None, b, tn), lambda l, j: (l, 0, j)),
        out_shape=jax.ShapeDtypeStruct((depth, b, n), F32),
        compiler_params=_params("parallel", "parallel"),
        name="adaln_mod",
    )(c, w_ada, b_ada.reshape(depth, 1, n))


def _norm_mod(x, g, shift, scale, eps=EPS):
    y = x * lax.rsqrt(jnp.mean(x * x, axis=-1, keepdims=True) + eps)
    return (y * g) * (1.0 + scale) + shift


def _rope_store(acc, o_ref, cos, sa, sb, half):
    for j in range(acc.shape[1] // LANES):
        a = acc[:, j * LANES:(j + 1) * LANES]
        r = a * cos + pltpu.roll(a, half, 1) * sb + pltpu.roll(a, LANES - half, 1) * sa
        o_ref[:, j * LANES:(j + 1) * LANES] = r.astype(o_ref.dtype)


def _inproj_kernel(x_ref, g_ref, mod_ref, c64_ref, sa64_ref, sb64_ref, c32_ref, sa32_ref, sb32_ref, w_ref,
                   qa_ref, kka_ref, vva_ref, qi_ref, kki_ref, wi_ref,
                   qb_ref, kb_ref, vb_ref, qc_ref, kc_ref, vc_ref):
    h = _norm_mod(x_ref[...], g_ref[...], mod_ref[0:1, :], mod_ref[1:2, :]).astype(BF16)
    c64, sa64, sb64 = c64_ref[...], sa64_ref[...], sb64_ref[...]
    c32, sa32, sb32 = c32_ref[...], sa32_ref[...], sb32_ref[...]

    def proj(cols):
        return _dot(h, w_ref[:, cols[0]:cols[1]])

    qk_scale = HEAD_DIM ** -0.5 * LOG2E
    _rope_store(proj(_G_QA), qa_ref, c64 * qk_scale, sa64 * qk_scale, sb64 * qk_scale, 8)
    _rope_store(proj(_G_KKA), kka_ref, c64, sa64, sb64, 8)
    vva_ref[...] = proj(_G_VVA).astype(vva_ref.dtype)
    _rope_store(proj(_G_QI), qi_ref, c64, sa64, sb64, 8)
    _rope_store(proj(_G_KKI), kki_ref, c64, sa64, sb64, 8)
    wi_ref[...] = proj(_G_WI) * (IDX_HEADS ** -0.5 * HEAD_DIM ** -0.5)
    b_scale = B_QK_DIM ** -0.5 * LOG2E
    _rope_store(proj(_G_QB), qb_ref, c32 * b_scale, sa32 * b_scale, sb32 * b_scale, 4)
    _rope_store(proj(_G_KB), kb_ref, c32, sa32, sb32, 4)
    vb_ref[...] = proj(_G_VB).astype(vb_ref.dtype)
    _rope_store(proj(_G_QC), qc_ref, c64 * qk_scale, sa64 * qk_scale, sb64 * qk_scale, 8)
    _rope_store(proj(_G_KC), kc_ref, c64, sa64, sb64, 8)
    vc_ref[...] = proj(_G_VC).astype(vc_ref.dtype)


def inproj(x, g, mod, tabs, w_pad, layer, seq, tm=512):
    t, d = x.shape
    tiles_per_batch = seq // tm
    row = lambda i: (i, 0)
    widths = [A_WIDTH, LANES, LANES, IDX_WIDTH, LANES, LANES, B_WIDTH, B_WIDTH, B_WIDTH, C_WIDTH, C_WIDTH, C_WIDTH]
    dtypes = [BF16, BF16, BF16, BF16, BF16, F32, BF16, BF16, BF16, BF16, BF16, BF16]
    return pl.pallas_call(
        _inproj_kernel,
        grid=(t // tm,),
        in_specs=[pl.BlockSpec((tm, d), row),
                  pl.BlockSpec((1, d), lambda i: (0, 0)),
                  pl.BlockSpec((None, 6, d), lambda i: (i // tiles_per_batch, 0, 0))]
                 + [pl.BlockSpec((tm, LANES), row)] * 6
                 + [pl.BlockSpec((None, d, D_IN_PAD), lambda i: (layer, 0, 0))],
        out_specs=[pl.BlockSpec((tm, w), row) for w in widths],
        out_shape=[jax.ShapeDtypeStruct((t, w), dt) for w, dt in zip(widths, dtypes)],
        compiler_params=_params("parallel"),
        name="inproj",
    )(x, g, mod, *tabs, w_pad)


def _init_stats(m_ref, l_ref, acc_ref):
    m_ref[...] = jnp.full(m_ref.shape, -jnp.inf, F32)
    l_ref[...] = jnp.zeros(l_ref.shape, F32)
    acc_ref[...] = jnp.zeros(acc_ref.shape, F32)


def _key_to_f32(k):
    return lax.bitcast_convert_type(jnp.where(k >= 0, k, k ^ 0x7FFFFFFF), F32)


def dsa_attention(qa, qi, wi, kka, vva, kki, batch, seq, tq=256):
    t = qa.shape[0]
    nq = seq // tq
    k_top = min(DSA_TOPK_MAX, seq // 4)
    qrow = lambda b, i: (b * nq + i, 0)
    full = lambda b, i: (b, 0)
    return pl.pallas_call(
        functools.partial(_dsa_kernel_t, tq=tq, k_top=k_top, seq=seq),
        grid=(batch, nq),
        in_specs=[pl.BlockSpec((tq, A_WIDTH), qrow), pl.BlockSpec((tq, IDX_WIDTH), qrow),
                  pl.BlockSpec((tq, LANES), qrow),
                  pl.BlockSpec((seq, LANES), full), pl.BlockSpec((seq, LANES), full),
                  pl.BlockSpec((seq, LANES), full)],
        out_specs=pl.BlockSpec((tq, A_WIDTH), qrow),
        out_shape=jax.ShapeDtypeStruct((t, A_WIDTH), BF16),
        scratch_shapes=[pltpu.VMEM((seq, tq), F32), pltpu.VMEM((seq, tq), jnp.int16),
                        pltpu.VMEM((seq, tq), jnp.int16), pltpu.VMEM((1, LANES, seq), BF16),
                        pltpu.VMEM((3, 2 * tq, LANES), BF16),
                        pltpu.VMEM((3, 1, 2 * tq), F32), pltpu.VMEM((3, 1, 2 * tq), F32),
                        pltpu.VMEM((3, LANES, 2 * tq), F32)],
        compiler_params=_params("parallel", "arbitrary"),
        name="dsa_attention",
    )(qa, qi, wi, kka, vva, kki)


def diff_attention(qb, kb, vb, lam_vec, g_sub2, lam_init, batch, seq, tq=256):
    t = qb.shape[0]
    nq = seq // tq
    qrow = lambda b, i: (b * nq + i, 0)
    full = lambda b, i: (b, 0)
    const = lambda b, i: (0, 0)
    return pl.pallas_call(
        functools.partial(_diff_kernel_t, tq=tq, lam_init=lam_init, seq=seq),
        grid=(batch, nq),
        in_specs=[pl.BlockSpec((tq, B_WIDTH), qrow), pl.BlockSpec((seq, B_WIDTH), full),
                  pl.BlockSpec((seq, B_WIDTH), full),
                  pl.BlockSpec((4, B_QK_DIM), const), pl.BlockSpec((1, LANES), const)],
        out_specs=pl.BlockSpec((tq, B_WIDTH), qrow),
        out_shape=jax.ShapeDtypeStruct((t, B_WIDTH), BF16),
        scratch_shapes=[pltpu.VMEM((2, LANES, seq), BF16), pltpu.VMEM((2, 4 * tq, LANES), BF16),
                        pltpu.VMEM((2, 1, 4 * tq), F32), pltpu.VMEM((2, 1, 4 * tq), F32),
                        pltpu.VMEM((2, LANES, 4 * tq), F32)],
        compiler_params=_params("parallel", "arbitrary"),
        name="diff_attention",
    )(qb, kb, vb, lam_vec, g_sub2)


def moba_attention(qc, kc, vc, batch, seq):
    tq = MOBA_BLOCK
    t = qc.shape[0]
    nb = seq // tq
    n_sel = min(MOBA_TOPK, nb - 1)
    nbp = 8
    assert seq % tq == 0 and nb <= nbp
    qrow = lambda b, i: (b * nb + i, 0)
    full = lambda b, i: (b, 0)
    return pl.pallas_call(
        functools.partial(_moba_kernel_t, tq=tq, nb=nb, n_sel=n_sel, seq=seq),
        grid=(batch, nb),
        in_specs=[pl.BlockSpec((tq, C_WIDTH), qrow), pl.BlockSpec((seq, C_WIDTH), full),
                  pl.BlockSpec((seq, C_WIDTH), full)],
        out_specs=pl.BlockSpec((tq, C_WIDTH), qrow),
        out_shape=jax.ShapeDtypeStruct((t, C_WIDTH), BF16),
        scratch_shapes=[pltpu.VMEM((nbp, C_WIDTH), F32), pltpu.VMEM((3, LANES, seq), BF16),
                        pltpu.VMEM((3, 2 * tq, LANES), BF16), pltpu.VMEM((3, nbp, 2 * tq), F32),
                        pltpu.VMEM((3, 1, 2 * tq), F32), pltpu.VMEM((3, 1, 2 * tq), F32),
                        pltpu.VMEM((3, LANES, 2 * tq), F32)],
        compiler_params=_params("parallel", "arbitrary"),
        name="moba_attention",
    )(qc, kc, vc)


def _attend_t(k, q_ref, g, v_t, m_ref, l_ref, acc_ref, mask=None):
    for j in range(q_ref.shape[1] // LANES):
        cols = slice(j * LANES, (j + 1) * LANES)
        s = _dot_nt(k, q_ref[g, cols, :])
        if mask is not None:
            s = mask(j, s)
        m_prev = m_ref[g, :, cols]
        m_new = jnp.maximum(m_prev, jnp.max(s, axis=0, keepdims=True))
        alpha = jnp.exp2(m_prev - m_new)
        p = jnp.exp2(s - m_new)
        l_ref[g, :, cols] = alpha * l_ref[g, :, cols] + jnp.sum(p, axis=0, keepdims=True)
        acc_ref[g, :, cols] = alpha * acc_ref[g, :, cols] + _dot(v_t, p.astype(BF16))
        m_ref[g, :, cols] = m_new


def _causal_mask_t(tk, tq):
    key = lax.broadcasted_iota(jnp.int32, (tk, LANES), 0)
    qry = lax.broadcasted_iota(jnp.int32, (tk, LANES), 1)

    def mask(j, s):
        return jnp.where(key <= qry + (j * LANES) % tq, s, NEG)
    return mask


def _transpose_values(v_ref, vt_ref, seq, tk):
    for g in range(vt_ref.shape[0]):
        for n in range(seq // tk):
            blk = v_ref[n * tk:(n + 1) * tk, g * LANES:(g + 1) * LANES].astype(F32)
            vt_ref[g, :, n * tk:(n + 1) * tk] = blk.T.astype(vt_ref.dtype)


def _head_pair_out(acc_t, l, tq):
    even = acc_t[0:HEAD_DIM, 0:tq] / l[:, 0:tq]
    odd = acc_t[HEAD_DIM:LANES, tq:2 * tq] / l[:, tq:2 * tq]
    return jnp.concatenate([even, odd], axis=0).T


def _dsa_kernel_t(qa_ref, qi_ref, wiq_ref, kka_ref, vva_ref, kki_ref, o_ref,
                  sc_ref, hi_ref, lo_ref, vt_ref, qs_ref, m_ref, l_ref, acc_ref, *, tq, k_top, seq):
    i = pl.program_id(1)
    nk = i + 1
    t0 = i * tq
    lo = lax.broadcasted_iota(jnp.int32, (tq, LANES), 1) < HEAD_DIM
    zero_b = jnp.zeros((tq, LANES), BF16)

    @pl.when(i == 0)
    def _():
        _transpose_values(vva_ref, vt_ref, seq, tq)

    def stack_heads(q):
        out = []
        for g in range(q.shape[1] // LANES):
            qg = q[:, g * LANES:(g + 1) * LANES]
            out += [jnp.where(lo, qg, zero_b), jnp.where(lo, zero_b, qg)]
        return out

    qa_stack = stack_heads(qa_ref[...])
    for g in range(3):
        qs_ref[g, 0:tq, :] = qa_stack[2 * g]
        qs_ref[g, tq:2 * tq, :] = qa_stack[2 * g + 1]
    qi_stack = jnp.concatenate(stack_heads(qi_ref[...]), axis=0)
    wi_t = wiq_ref[...].T

    def rows(c):
        return pl.ds(pl.multiple_of(c * tq, tq), tq)

    key_pos = lax.broadcasted_iota(jnp.int32, (tq, tq), 0)
    qry_pos = lax.broadcasted_iota(jnp.int32, (tq, tq), 1)

    def idx_body(c, carry):
        r = jnp.maximum(_dot_nt(kki_ref[rows(c), :], qi_stack), 0.0)
        s = wi_t[0:1, :] * r[:, 0:tq]
        for h in range(1, IDX_HEADS):
            s = s + wi_t[h:h + 1, :] * r[:, h * tq:(h + 1) * tq]
        causal = (c * tq + key_pos) <= (t0 + qry_pos)
        s = jnp.where(causal, s, -jnp.inf)
        s = jnp.where(jnp.abs(s) < MIN_NORMAL_F32, 0.0, s)
        sc_ref[rows(c), :] = s
        bits = lax.bitcast_convert_type(s, jnp.int32)
        key = jnp.where(bits >= 0, bits, bits ^ 0x7FFFFFFF)
        hi_ref[rows(c), :] = lax.shift_right_arithmetic(key, jnp.int32(16)).astype(jnp.int16)
        lo_ref[rows(c), :] = ((key & 0xFFFF) - 32768).astype(jnp.int16)
        return carry

    lax.fori_loop(0, nk, idx_body, 0)

    def count16(ref, cand, strict):
        c16 = cand.astype(jnp.int16)

        def body(c, acc):
            x = ref[rows(c), :]
            hit = jnp.where((x > c16) if strict else (x >= c16), jnp.int16(1), jnp.int16(0))
            for r in range(tq // 16):
                acc = acc + hit[r * 16:(r + 1) * 16]
            return acc
        acc = lax.fori_loop(0, nk, body, jnp.zeros((16, tq), jnp.int16))
        return jnp.sum(acc.astype(F32), axis=0, keepdims=True)

    def search16(ref, k_need):
        v0 = jnp.where(count16(ref, jnp.zeros((1, tq), jnp.int32), False) >= k_need, 0, -32768).astype(jnp.int32)

        def bisect(b, v):
            trial = v | lax.shift_left(jnp.int32(1), 14 - b)
            return jnp.where(count16(ref, trial, False) >= k_need, trial, v)
        return lax.fori_loop(0, 15, bisect, v0)

    kf = float(k_top)

    def search():
        hi_k = search16(hi_ref, kf)
        above = count16(hi_ref, hi_k, True)
        hi16 = hi_k.astype(jnp.int16)

        def keep_bucket(c, carry):
            lo_ref[rows(c), :] = jnp.where(hi_ref[rows(c), :] == hi16, lo_ref[rows(c), :], jnp.int16(-32768))
            return carry
        lax.fori_loop(0, nk, keep_bucket, 0)
        lo_k = search16(lo_ref, kf - above)
        n_gt = above + count16(lo_ref, lo_k, True)
        return lax.shift_left(hi_k, jnp.int32(16)) | (lo_k + 32768), kf - n_gt

    key, need = lax.cond(t0 + tq <= k_top,
                         lambda: (jnp.full((1, tq), INT_MIN, jnp.int32), jnp.full((1, tq), kf, F32)), search)
    thr = _key_to_f32(key)
    thr_next = _key_to_f32(jnp.where(key == 0, MIN_NORMAL_KEY, key + 1))
    all_sel = (t0 + lax.broadcasted_iota(jnp.int32, (1, tq), 1)) < k_top
    lower = (qry_pos <= key_pos).astype(BF16)

    _init_stats(m_ref, l_ref, acc_ref)

    def att_body(c, tie_run):
        s_idx = sc_ref[rows(c), :]
        ge = s_idx >= thr
        gt = s_idx >= thr_next
        tie = jnp.where(gt, 0.0, jnp.where(ge, 1.0, 0.0))
        prefix = _dot(lower, tie.astype(BF16)) + tie_run
        take = jnp.where(gt, 1.0, jnp.where(prefix <= need, tie, 0.0))
        take = jnp.where(all_sel, 1.0, take)
        take = jnp.where(s_idx > -jnp.inf, take, 0.0)

        def mask(j, s):
            q0 = (j * LANES) % tq
            return jnp.where(take[:, q0:q0 + LANES] > 0.5, s, NEG)

        kk = kka_ref[rows(c), :]
        vt = vt_ref[0, :, rows(c)]
        for g in range(3):
            _attend_t(kk, qs_ref, g, vt, m_ref, l_ref, acc_ref, mask)
        return tie_run + jnp.sum(tie, axis=0, keepdims=True)

    lax.fori_loop(0, nk, att_body, jnp.zeros((1, tq), F32))

    for g in range(3):
        o_ref[:, g * LANES:(g + 1) * LANES] = _head_pair_out(acc_ref[g], l_ref[g], tq).astype(o_ref.dtype)


def _diff_kernel_t(qb_ref, kb_ref, vb_ref, lam_ref, gsub_ref, o_ref,
                   vt_ref, qs_ref, m_ref, l_ref, acc_ref, *, tq, lam_init, seq):
    i = pl.program_id(1)
    lane = lax.broadcasted_iota(jnp.int32, (tq, LANES), 1)
    zero_b = jnp.zeros((tq, LANES), BF16)

    @pl.when(i == 0)
    def _():
        _transpose_values(vb_ref, vt_ref, seq, tq)

    qb = qb_ref[...]
    for g in range(2):
        qg = qb[:, g * LANES:(g + 1) * LANES]
        for j in range(4):
            qs_ref[g, j * tq:(j + 1) * tq, :] = jnp.where(lane // B_QK_DIM == j, qg, zero_b)
    _init_stats(m_ref, l_ref, acc_ref)

    def rows(c):
        return pl.ds(pl.multiple_of(c * tq, tq), tq)

    def step(c, masked):
        mask = _causal_mask_t(tq, tq) if masked else None
        for g in range(2):
            _attend_t(kb_ref[rows(c), g * LANES:(g + 1) * LANES], qs_ref, g, vt_ref[g, :, rows(c)],
                      m_ref, l_ref, acc_ref, mask)

    def body(c, carry):
        step(c, False)
        return carry

    lax.fori_loop(0, i, body, 0)
    step(i, True)

    lv = lam_ref[...]
    lam = (jnp.exp(jnp.sum(lv[0:1] * lv[1:2], axis=1, keepdims=True))
           - jnp.exp(jnp.sum(lv[2:3] * lv[3:4], axis=1, keepdims=True)) + lam_init)
    gsub = gsub_ref[...]
    for g in range(2):
        acc_t, l = acc_ref[g], l_ref[g]

        def prob(r0, j):
            return acc_t[r0:r0 + HEAD_DIM, j * tq:(j + 1) * tq] / l[:, j * tq:(j + 1) * tq]

        halves = []
        for r0, j in ((0, 0), (HEAD_DIM, 2)):
            o = prob(r0, j) - lam * prob(r0, j + 1)
            ms = jnp.mean(o * o, axis=0, keepdims=True)
            halves.append(o * lax.rsqrt(ms + SUBLN_EPS))
        y = (jnp.concatenate(halves, axis=0).T * gsub) * (1.0 - lam_init)
        o_ref[:, g * LANES:(g + 1) * LANES] = y.astype(o_ref.dtype)


def _moba_kernel_t(qc_ref, kc_ref, vc_ref, o_ref, kmean_ref, vt_ref, qs_ref, bias_ref, m_ref, l_ref, acc_ref,
                   *, tq, nb, n_sel, seq):
    i = pl.program_id(1)
    nbp = kmean_ref.shape[0]
    lane = lax.broadcasted_iota(jnp.int32, (tq, LANES), 1)
    lo = lane < HEAD_DIM
    zero_b = jnp.zeros((tq, LANES), BF16)

    @pl.when(i == 0)
    def _():
        _transpose_values(vc_ref, vt_ref, seq, tq)
        kmean_ref[...] = jnp.zeros(kmean_ref.shape, F32)
        for n in range(nb):
            kblk = kc_ref[n * tq:(n + 1) * tq, :].astype(F32)
            kmean_ref[n:n + 1, :] = jnp.mean(kblk, axis=0, keepdims=True)

    qc = qc_ref[...]
    sub = lax.broadcasted_iota(jnp.int32, (nbp, 2 * tq), 0)
    past = sub < i
    for g in range(3):
        qg = qc[:, g * LANES:(g + 1) * LANES]
        q2 = jnp.concatenate([jnp.where(lo, qg, zero_b), jnp.where(lo, zero_b, qg)], axis=0)
        km = kmean_ref[:, g * LANES:(g + 1) * LANES]
        km_hi = km.astype(BF16)
        gt = _dot_nt(km_hi, q2) + _dot_nt((km - km_hi.astype(F32)).astype(BF16), q2)
        rows_ = []
        for n in range(nbp):
            gn = gt[n:n + 1, :]
            beats = jnp.where(sub < n, jnp.where(gt >= gn, 1.0, 0.0), jnp.where(gt > gn, 1.0, 0.0))
            beats = jnp.where(sub == n, 0.0, jnp.where(past, beats, 0.0))
            rank = jnp.sum(beats, axis=0, keepdims=True)
            rows_.append(jnp.where(rank < n_sel, 0.0, NEG))
        bias_ref[g] = jnp.where(past, jnp.concatenate(rows_, axis=0), NEG)
        qs_ref[g] = q2
    _init_stats(m_ref, l_ref, acc_ref)

    def rows(c):
        return pl.ds(pl.multiple_of(c * tq, tq), tq)

    def body(c, carry):
        for g in range(3):
            bias = bias_ref[g, pl.ds(c, 1), :]

            def mask(j, s):
                return s + bias[:, j * LANES:(j + 1) * LANES]

            _attend_t(kc_ref[rows(c), g * LANES:(g + 1) * LANES], qs_ref, g, vt_ref[g, :, rows(c)],
                      m_ref, l_ref, acc_ref, mask)
        return carry

    lax.fori_loop(0, i, body, 0)

    causal = _causal_mask_t(tq, tq)
    for g in range(3):
        _attend_t(kc_ref[rows(i), g * LANES:(g + 1) * LANES], qs_ref, g, vt_ref[g, :, rows(i)],
                  m_ref, l_ref, acc_ref, causal)
        o_ref[:, g * LANES:(g + 1) * LANES] = _head_pair_out(acc_ref[g], l_ref[g], tq).astype(o_ref.dtype)


def _route(h, wr_ref, cw_ref, meta_ref, cnt_ref):
    hi = h.astype(BF16)
    lo = (h - hi.astype(F32)).astype(BF16)
    logits = _dot(hi, wr_ref[0]) + (_dot(lo, wr_ref[0]) + _dot(hi, wr_ref[1]))
    lane = lax.broadcasted_iota(jnp.int32, logits.shape, 1)
    lg = jnp.where(lane < N_EXPERTS, logits, -jnp.inf)
    v0 = jnp.max(lg, axis=1, keepdims=True)
    i0 = jnp.min(jnp.where(lg == v0, lane, LANES), axis=1, keepdims=True)
    lg1 = jnp.where(lane == i0, -jnp.inf, lg)
    v1 = jnp.max(lg1, axis=1, keepdims=True)
    i1 = jnp.min(jnp.where(lg1 == v1, lane, LANES), axis=1, keepdims=True)
    e1 = jnp.exp(v1 - v0)
    w0 = 1.0 / (1.0 + e1)
    tm = logits.shape[0]
    routed = jnp.where(lane == i0, 1.0, 0.0) + jnp.where(lane == i1, 1.0, 0.0)
    earlier_rows = (lax.broadcasted_iota(jnp.int32, (tm, tm), 1)
                    < lax.broadcasted_iota(jnp.int32, (tm, tm), 0)).astype(BF16)

    @pl.when(pl.program_id(0) == 0)
    def _():
        cnt_ref[...] = jnp.zeros(cnt_ref.shape, F32)

    before = _dot(earlier_rows, routed.astype(BF16)) + cnt_ref[0:1, :]
    r0 = jnp.sum(jnp.where(lane == i0, before, 0.0), axis=1, keepdims=True)
    r1 = jnp.sum(jnp.where(lane == i1, before, 0.0), axis=1, keepdims=True)
    cnt_ref[0:1, :] = cnt_ref[0:1, :] + jnp.sum(routed, axis=0, keepdims=True)
    cw = (jnp.where(lane == 0, i0.astype(F32), 0.0) + jnp.where(lane == 1, i1.astype(F32), 0.0)
          + jnp.where(lane == 2, w0, 0.0) + jnp.where(lane == 3, e1 * w0, 0.0)
          + jnp.where(lane == 4, r0, 0.0) + jnp.where(lane == 5, r1, 0.0))
    cw_ref[...] = cw
    meta_ref[...] = cw.T[0:8, :].astype(jnp.int32)


def _mixer_out(x_ref, ya_ref, yb_ref, yc_ref, wo_ref, g_ref, mod_ref):
    ab = A_WIDTH + B_WIDTH
    y = (_dot(ya_ref[...], wo_ref[0:A_WIDTH, :]) + _dot(yb_ref[...], wo_ref[A_WIDTH:ab, :])
         + _dot(yc_ref[...], wo_ref[ab:ab + C_WIDTH, :]))
    x1 = x_ref[...] + mod_ref[2:3, :] * y
    return x1, _norm_mod(x1, g_ref[...], mod_ref[3:4, :], mod_ref[4:5, :])


def _mixer_out_specs(tm, d, layer, tiles_per_batch):
    row = lambda i: (i, 0)
    return [pl.BlockSpec((tm, d), row), pl.BlockSpec((tm, A_WIDTH), row), pl.BlockSpec((tm, B_WIDTH), row),
            pl.BlockSpec((tm, C_WIDTH), row), pl.BlockSpec((None, d, d), lambda i: (layer, 0, 0)),
            pl.BlockSpec((1, d), lambda i: (0, 0)),
            pl.BlockSpec((None, 6, d), lambda i: (i // tiles_per_batch, 0, 0))]


def _outproj_kernel(x_ref, ya_ref, yb_ref, yc_ref, wo_ref, g_ref, mod_ref, wr_ref,
                    x1_ref, h_ref, cw_ref, meta_ref, cnt_ref):
    x1, h = _mixer_out(x_ref, ya_ref, yb_ref, yc_ref, wo_ref, g_ref, mod_ref)
    x1_ref[...] = x1
    h_ref[...] = _pack_rows(h)
    _route(h, wr_ref, cw_ref, meta_ref, cnt_ref)


def outproj_router(x, ya, yb, yc, wo, layer, g_ffn, mod, seq, w_router, tm=512):
    t, d = x.shape
    row = lambda i: (i, 0)
    return pl.pallas_call(
        _outproj_kernel,
        grid=(t // tm,),
        in_specs=_mixer_out_specs(tm, d, layer, seq // tm) + [pl.BlockSpec((2, d, LANES), lambda i: (0, 0, 0))],
        out_specs=[pl.BlockSpec((tm, d), row), pl.BlockSpec((tm, d // 2), row), pl.BlockSpec((tm, LANES), row),
                   pl.BlockSpec((8, tm), lambda i: (0, i)), pl.BlockSpec((8, LANES), lambda i: (0, 0))],
        out_shape=[jax.ShapeDtypeStruct((t, d), F32), jax.ShapeDtypeStruct((t, d // 2), jnp.int32),
                   jax.ShapeDtypeStruct((t, LANES), F32), jax.ShapeDtypeStruct((8, t), jnp.int32),
                   jax.ShapeDtypeStruct((8, LANES), F32)],
        compiler_params=_params("arbitrary"),
        name="outproj_router",
    )(x, ya, yb, yc, wo, g_ffn, mod, w_router)


W_ROWS_IN = 128
W_ROWS_DOWN = 352
FF_SPLIT = 2


def _load_weights_bf16(layer, e, w_hbms, w_bs, st_in, st_dn, sems):
    slots = st_in.shape[0]
    plan = []
    for w_hbm, w_b, rb, st, sem0 in ((w_hbms[0], w_bs[0], W_ROWS_IN, st_in, 0), (w_hbms[1], w_bs[1], W_ROWS_IN, st_in, 0),
                                     (w_hbms[2], w_bs[2], W_ROWS_DOWN, st_dn, slots)):
        assert w_b.shape[0] % rb == 0
        plan += [(w_hbm, w_b, r0, rb, st, sem0) for r0 in range(0, w_b.shape[0], rb)]
    uses = {0: 0, slots: 0}
    copies = []
    for w_hbm, w_b, r0, rb, st, sem0 in plan:
        slot = uses[sem0] % slots
        uses[sem0] += 1
        copies.append((pltpu.make_async_copy(w_hbm.at[layer, e, pl.ds(r0, rb), :], st.at[slot],
                                             sems.at[sem0 + slot]), st, slot, w_b, r0, rb))
    ahead = slots - 1
    for copy in copies[:ahead]:
        copy[0].start()
    for b, (copy, st, slot, w_b, r0, rb) in enumerate(copies):
        if b + ahead < len(copies):
            copies[b + ahead][0].start()
        copy.wait()
        w_b[r0:r0 + rb, :] = st[slot].astype(BF16)


def _swiglu_resident(x, wg_b, wu_b, wd_b):
    tf = wg_b.shape[1] // FF_SPLIT
    acc = None
    for f in range(FF_SPLIT):
        cols = slice(f * tf, (f + 1) * tf)
        a = _dot(x, wg_b[:, cols])
        u = _dot(x, wu_b[:, cols])
        act = (a * (1.0 / (1.0 + jnp.exp(-a)))) * u
        part = _dot(act.astype(BF16), wd_b[cols, :])
        acc = part if acc is None else acc + part
    return acc


def _weight_scratch(d, dff, slots):
    return [pltpu.VMEM((d, dff), BF16), pltpu.VMEM((d, dff), BF16), pltpu.VMEM((dff, d), BF16),
            pltpu.VMEM((slots, W_ROWS_IN, dff), F32), pltpu.VMEM((slots, W_ROWS_DOWN, d), F32),
            pltpu.SemaphoreType.DMA((2 * slots,))]


def _layer_out(x, gfin_ref, final):
    if not final:
        return x
    return (x * lax.rsqrt(jnp.mean(x * x, axis=-1, keepdims=True) + EPS)) * gfin_ref[...]


def _ffn_kernel(x_ref, ya_ref, yb_ref, yc_ref, wo_ref, g_ref, mod_ref, gfin_ref, wg_hbm, wu_hbm, wd_hbm, o_ref,
                wg_b, wu_b, wd_b, st_in, st_dn, sems, *, layer, final):
    @pl.when(pl.program_id(0) == 0)
    def _():
        _load_weights_bf16(layer, 0, (wg_hbm, wu_hbm, wd_hbm), (wg_b, wu_b, wd_b), st_in, st_dn, sems)

    x1, h = _mixer_out(x_ref, ya_ref, yb_ref, yc_ref, wo_ref, g_ref, mod_ref)
    out = x1 + mod_ref[5:6, :] * _swiglu_resident(h.astype(BF16), wg_b, wu_b, wd_b)
    o_ref[...] = _layer_out(out, gfin_ref, final)


def outproj_ffn_dense(x, ya, yb, yc, wo, layer, g_ffn, mod, g_final, final, w_gate, w_up, w_down, ff_layer, seq,
                      tm=512):
    t, d = x.shape
    hbm = pl.BlockSpec(memory_space=pl.ANY)
    return pl.pallas_call(
        functools.partial(_ffn_kernel, layer=ff_layer, final=final),
        grid=(t // tm,),
        in_specs=_mixer_out_specs(tm, d, layer, seq // tm) + [pl.BlockSpec((1, d), lambda i: (0, 0)), hbm, hbm, hbm],
        out_specs=pl.BlockSpec((tm, d), lambda i: (i, 0)),
        out_shape=jax.ShapeDtypeStruct((t, d), F32),
        scratch_shapes=_weight_scratch(d, w_gate.shape[3], 2),
        compiler_params=_params("arbitrary"),
        name="ffn_dense",
    )(x, ya, yb, yc, wo, g_ffn, mod, g_final, w_gate, w_up, w_down)


MOE_TILE = 512
MOE_PARTS = 3
SC_CORES, SC_SUBCORES = 2, 16
SC_ROWS = 64
HI16 = -65536


def _pack_rows(x):
    c = x.shape[1] // 2
    bits = lax.bitcast_convert_type(x.astype(jnp.bfloat16).astype(F32), jnp.int32)
    return lax.shift_right_logical(bits[:, :c], jnp.int32(16)) | (bits[:, c:] & jnp.int32(HI16))


def _unpack_rows(w):
    lo = lax.bitcast_convert_type(lax.shift_left(w, jnp.int32(16)), F32)
    hi = lax.bitcast_convert_type(w & jnp.int32(HI16), F32)
    return jnp.concatenate([lo, hi], axis=1)


def sc_gather_rows(table, idx):
    d = table.shape[1]
    b = idx.shape[0]
    per_worker = b // (SC_CORES * SC_SUBCORES)
    assert per_worker * SC_CORES * SC_SUBCORES == b and per_worker % (2 * SC_ROWS) == 0
    mesh = plsc.VectorSubcoreMesh(core_axis_name="c", subcore_axis_name="s")
    idx_buf = pltpu.VMEM((SC_ROWS,), jnp.int32)
    row_buf = pltpu.VMEM((SC_ROWS, d), table.dtype)

    @functools.partial(
        pl.kernel, mesh=mesh, out_type=jax.ShapeDtypeStruct((b, d), table.dtype),
        scratch_types=[idx_buf, idx_buf, row_buf, row_buf] + [pltpu.SemaphoreType.DMA] * 4,
        name="sc_gather_rows")
    def gather(table_hbm, idx_hbm, out_hbm, idx0, idx1, rows0, rows1, sem_g0, sem_g1, sem_w0, sem_w1):
        base = (lax.axis_index("s") * SC_CORES + lax.axis_index("c")) * per_worker

        @pl.loop(0, per_worker // (2 * SC_ROWS))
        def _(pair):
            off0 = pl.multiple_of(base + pair * (2 * SC_ROWS), SC_ROWS)
            off1 = pl.multiple_of(off0 + SC_ROWS, SC_ROWS)
            pltpu.sync_copy(idx_hbm.at[pl.ds(off0, SC_ROWS)], idx0)
            pltpu.sync_copy(idx_hbm.at[pl.ds(off1, SC_ROWS)], idx1)
            gather0 = pltpu.async_copy(table_hbm.at[idx0], rows0, sem_g0)
            gather1 = pltpu.async_copy(table_hbm.at[idx1], rows1, sem_g1)
            gather0.wait()
            write0 = pltpu.async_copy(rows0, out_hbm.at[pl.ds(off0, SC_ROWS)], sem_w0)
            gather1.wait()
            write1 = pltpu.async_copy(rows1, out_hbm.at[pl.ds(off1, SC_ROWS)], sem_w1)
            write0.wait()
            write1.wait()

    return gather(table, idx)


def _moe_layout(meta, counts, n_exp):
    t = meta.shape[1]
    e0, e1, rank0, rank1 = meta[0], meta[1], meta[4], meta[5]
    padded = (counts + MOE_TILE - 1) // MOE_TILE * MOE_TILE
    ends = jnp.cumsum(padded)
    offs = ends - padded
    pos0 = offs[e0] + rank0
    pos1 = offs[e1] + rank1
    n_rows = 2 * t + n_exp * MOE_TILE
    tok = jnp.arange(t, dtype=jnp.int32)
    src = jnp.zeros((n_rows,), jnp.int32).at[jnp.concatenate([pos0, pos1])].set(
        jnp.concatenate([tok, tok]), unique_indices=True, mode="promise_in_bounds")
    n_tiles = n_rows // MOE_TILE
    n_valid = ends[-1] // MOE_TILE
    tile_expert = jnp.sum(jnp.arange(n_tiles)[:, None] * MOE_TILE >= ends[None, :], axis=1)
    tile_expert = tile_expert[jnp.minimum(jnp.arange(n_tiles), n_valid - 1)].astype(jnp.int32)
    return pos0, pos1, src, tile_expert, n_valid.astype(jnp.int32).reshape(1)


W_SLOTS = 4


def _experts_packed_kernel(te_ref, nv_ref, first_ref, x_ref, wg_hbm, wu_hbm, wd_hbm, *rest, layer):
    o_ref, wg_b, wu_b, wd_b, st_in, st_dn, sems = rest[-7:]
    j = pl.program_id(0)
    valid = j < nv_ref[0]

    @pl.when(valid & (first_ref[j] == 1))
    def _():
        _load_weights_bf16(layer, te_ref[j], (wg_hbm, wu_hbm, wd_hbm), (wg_b, wu_b, wd_b), st_in, st_dn, sems)

    @pl.when(valid)
    def _():
        o_ref[...] = _pack_rows(_swiglu_resident(_unpack_rows(x_ref[...]).astype(BF16), wg_b, wu_b, wd_b))


def moe_experts_packed(x_part, tile_expert, n_valid, w_gate, w_up, w_down, layer, y_prev, part, n_rows):
    rows_part, half = x_part.shape
    d = 2 * half
    dff = w_gate.shape[3]
    tiles_part = rows_part // MOE_TILE
    first = jnp.concatenate([jnp.ones((1,), jnp.int32),
                             (tile_expert[1:] != tile_expert[:-1]).astype(jnp.int32)])

    def tile(j, nv):
        return jnp.maximum(jnp.minimum(j, nv[0] - 1), 0)

    hbm = pl.BlockSpec(memory_space=pl.ANY)
    in_specs = [pl.BlockSpec((MOE_TILE, half), lambda j, te, nv, fi: (tile(j, nv), 0)), hbm, hbm, hbm]
    args = [tile_expert, n_valid, first, x_part, w_gate, w_up, w_down]
    aliases = {}
    if y_prev is not None:
        in_specs.append(hbm)
        args.append(y_prev)
        aliases = {len(args) - 1: 0}
    return pl.pallas_call(
        functools.partial(_experts_packed_kernel, layer=layer),
        grid_spec=pltpu.PrefetchScalarGridSpec(
            num_scalar_prefetch=3, grid=(tiles_part,), in_specs=in_specs,
            out_specs=pl.BlockSpec((MOE_TILE, half), lambda j, te, nv, fi: (part * tiles_part + tile(j, nv), 0)),
            scratch_shapes=_weight_scratch(d, dff, W_SLOTS)),
        out_shape=jax.ShapeDtypeStruct((n_rows, half), jnp.int32),
        input_output_aliases=aliases,
        compiler_params=_params("arbitrary"),
        name="moe_experts",
    )(*args)


def _combine_packed_kernel(y0_ref, y1_ref, route_ref, x1_ref, mod_ref, gfin_ref, o_ref, *, final):
    rt = route_ref[...]
    f = rt[:, 2:3] * _unpack_rows(y0_ref[...]) + rt[:, 3:4] * _unpack_rows(y1_ref[...])
    o_ref[...] = _layer_out(x1_ref[...] + mod_ref[5:6, :] * f, gfin_ref, final)


def moe_combine_packed(y_pairs, route, x1, mod, g_final, final, seq, tm=512):
    t, d = x1.shape
    nt = t // tm
    tiles_per_batch = seq // tm
    row = lambda i: (i, 0)
    return pl.pallas_call(
        functools.partial(_combine_packed_kernel, final=final),
        grid=(nt,),
        in_specs=[pl.BlockSpec((tm, d // 2), row), pl.BlockSpec((tm, d // 2), lambda i: (i + nt, 0)),
                  pl.BlockSpec((tm, LANES), row), pl.BlockSpec((tm, d), row),
                  pl.BlockSpec((None, 6, d), lambda i: (i // tiles_per_batch, 0, 0)),
                  pl.BlockSpec((1, d), lambda i: (0, 0))],
        out_specs=pl.BlockSpec((tm, d), row),
        out_shape=jax.ShapeDtypeStruct((t, d), F32),
        compiler_params=_params("parallel"),
        name="moe_combine",
    )(y_pairs, y_pairs, route, x1, mod, g_final)


def ffn_moe_sc(x1, h_packed, route, meta, counts, mod, g_final, final, w_gate, w_up, w_down, layer, seq):
    n_exp = w_gate.shape[1]
    pos0, pos1, src, tile_expert, n_valid = _moe_layout(meta, counts[0, :n_exp].astype(jnp.int32), n_exp)
    n_rows = src.shape[0]
    rows_part = n_rows // MOE_PARTS
    tiles_part = rows_part // MOE_TILE
    assert rows_part * MOE_PARTS == n_rows and tiles_part * MOE_TILE == rows_part
    h_parts = [sc_gather_rows(h_packed, src[p * rows_part:(p + 1) * rows_part]) for p in range(MOE_PARTS)]
    y_sorted = None
    for p in range(MOE_PARTS):
        n_valid_part = jnp.clip(n_valid - p * tiles_part, 0, tiles_part)
        y_sorted = moe_experts_packed(h_parts[p], tile_expert[p * tiles_part:(p + 1) * tiles_part], n_valid_part,
                                      w_gate, w_up, w_down, layer, y_sorted, p, n_rows)
    y_pairs = sc_gather_rows(y_sorted, jnp.concatenate([pos0, pos1]))
    return moe_combine_packed(y_pairs, route, x1, mod, g_final, final, seq)


def _rope_tables(positions, dim):
    rot = dim // ROPE_FRACTION
    half = rot // 2
    inv = 1.0 / (ROPE_THETA ** (np.arange(0, rot, 2, dtype=np.float32) / rot))
    ang = positions.reshape(-1).astype(F32)[:, None] * jnp.asarray(inv, F32)
    cos, sin = jnp.cos(ang), jnp.sin(ang)
    t = ang.shape[0]
    ones = jnp.ones((t, dim - rot), F32)
    zeros = lambda w: jnp.zeros((t, w), F32)
    reps = LANES // dim
    c = jnp.tile(jnp.concatenate([cos, cos, ones], axis=1), (1, reps))
    sa = jnp.tile(jnp.concatenate([-sin, zeros(dim - half)], axis=1), (1, reps))
    sb = jnp.tile(jnp.concatenate([zeros(half), sin, zeros(dim - rot)], axis=1), (1, reps))
    return c, sa, sb


def _relayout_w_in(w):
    pts = np.cumsum([0, 384, 64, 64, 256, 64, 4, 256, 256, 256, 384, 384, 384])
    (q_a, k_a, v_a, q_i, k_i, w_i, q_b, k_b, v_b, q_c, k_c, v_c) = [w[..., pts[j]:pts[j + 1]] for j in range(12)]
    w_i_pad = jnp.concatenate([w_i, jnp.zeros(w.shape[:-1] + (LANES - IDX_HEADS,), w.dtype)], axis=-1)
    return jnp.concatenate([q_a, k_a, k_a, v_a, v_a, q_i, k_i, k_i, w_i_pad,
                            q_b, k_b, v_b, q_c, k_c, v_c], axis=-1).astype(BF16)


def kernel(x, c, positions, w_in, w_out, diff_lambda, diff_subln, w_ada, b_ada, g_attn, g_ffn, w_ff_gate,
           w_ff_up, w_ff_down, w_router, w_exp_gate, w_exp_up, w_exp_down, g_final):
    batch, seq, d = x.shape
    depth = w_in.shape[0]
    t = batch * seq
    tabs = _rope_tables(positions, HEAD_DIM) + _rope_tables(positions, B_QK_DIM)
    mod_all = adaln_mod(c, w_ada, b_ada).reshape(depth, batch, 6, d)
    w_in_pad = _relayout_w_in(w_in)
    wo = w_out.astype(BF16)
    g_fin = g_final.reshape(1, d)
    xf = x.reshape(t, d)
    for layer in range(depth):
        mod = mod_all[layer]
        lam_init = 0.8 - 0.6 * math.exp(-0.3 * layer)
        (qa, kka, vva, qi, kki, wi, qb, kb, vb, qc, kc, vc) = inproj(
            xf, g_attn[layer].reshape(1, d), mod, tabs, w_in_pad, layer, seq)
        ya = dsa_attention(qa, qi, wi, kka, vva, kki, batch, seq)
        g_sub2 = jnp.tile(diff_subln[layer], 2).reshape(1, LANES)
        yb = diff_attention(qb, kb, vb, diff_lambda[layer], g_sub2, lam_init, batch, seq)
        yc = moba_attention(qc, kc, vc, batch, seq)
        j = layer // 2
        gf = g_ffn[layer].reshape(1, d)
        final = layer == depth - 1
        if layer % 2 == 0:
            xf = outproj_ffn_dense(xf, ya, yb, yc, wo, layer, gf, mod, g_fin, final,
                                   w_ff_gate[:, None], w_ff_up[:, None], w_ff_down[:, None], j, seq)
        else:
            wr = jnp.concatenate([w_router[j], jnp.zeros((d, LANES - N_EXPERTS), F32)], axis=1)
            wr_hi = wr.astype(BF16)
            wr = jnp.stack([wr_hi, (wr - wr_hi.astype(F32)).astype(BF16)])
            x1, h_packed, route, meta, counts = outproj_router(xf, ya, yb, yc, wo, layer, gf, mod, seq, wr)
            xf = ffn_moe_sc(x1, h_packed, route, meta, counts, mod, g_fin, final,
                            w_exp_gate, w_exp_up, w_exp_down, j, seq)
    return xf.reshape(batch, seq, d)
```

```python
import functools
import math

import jax
import jax.numpy as jnp
import numpy as np
from jax import lax
from jax.experimental import pallas as pl
from jax.experimental.pallas import tpu as pltpu
from jax.experimental.pallas import tpu_sc as plsc

F32 = jnp.float32
BF16 = jnp.bfloat16

HEAD_DIM = 64
A_HEADS = 6
IDX_HEADS = 4
B_HEADS = 4
B_QK_DIM = 32
C_HEADS = 6
A_WIDTH, IDX_WIDTH = A_HEADS * HEAD_DIM, IDX_HEADS * HEAD_DIM
B_WIDTH, C_WIDTH = B_HEADS * HEAD_DIM, C_HEADS * HEAD_DIM
DSA_TOPK_MAX = 256
MOBA_BLOCK = 256
MOBA_TOPK = 3
ROPE_THETA = 500000.0
ROPE_FRACTION = 4
SUBLN_EPS = 1e-5
EPS = 1e-6
N_EXPERTS = 8

LANES = 128
NEG = -1e30
INT_MIN = -2 ** 31
MIN_NORMAL_KEY = 0x00800000
MIN_NORMAL_F32 = float(np.float32(2.0 ** -126))
VMEM_LIMIT = 48 * 1024 * 1024
LOG2E = math.log2(math.e)

_G_QA, _G_KKA, _G_VVA, _G_QI, _G_KKI, _G_WI = (0, 384), (384, 512), (512, 640), (640, 896), (896, 1024), (1024, 1152)
_G_QB, _G_KB, _G_VB = (1152, 1408), (1408, 1664), (1664, 1920)
_G_QC, _G_KC, _G_VC = (1920, 2304), (2304, 2688), (2688, 3072)
D_IN_PAD = 3072


def _params(*sem):
    return pltpu.CompilerParams(dimension_semantics=sem, vmem_limit_bytes=VMEM_LIMIT)


def _dot(a, b):
    return jnp.dot(a, b, preferred_element_type=F32)


def _dot_nt(a, b):
    return lax.dot_general(a, b, (((1,), (1,)), ((), ())), preferred_element_type=F32)


def _adaln_kernel(c_ref, w_ref, b_ref, o_ref):
    c = c_ref[...]
    c_act = c * (1.0 / (1.0 + jnp.exp(-c)))
    o_ref[...] = jnp.dot(c_act, w_ref[...], preferred_element_type=F32,
                         precision=lax.Precision.HIGHEST) + b_ref[...]


def adaln_mod(c, w_ada, b_ada, tn=1536):
    depth, d, n = w_ada.shape
    b = c.shape[0]
    return pl.pallas_call(
        _adaln_kernel,
        grid=(depth, n // tn),
        in_specs=[pl.BlockSpec((b, d), lambda l, j: (0, 0)),
                  pl.BlockSpec((None, d, tn), lambda l, j: (l, 0, j)),
                  pl.BlockSpec((None, 1, tn), lambda l, j: (l, 0, j))],
        out_specs=pl.BlockSpec((None, b, tn), lambda l, j: (l, 0, j)),
        out_shape=jax.ShapeDtypeStruct((depth, b, n), F32),
        compiler_params=_params("parallel", "parallel"),
        name="adaln_mod",
    )(c, w_ada, b_ada.reshape(depth, 1, n))


def _norm_mod(x, g, shift, scale, eps=EPS):
    y = x * lax.rsqrt(jnp.mean(x * x, axis=-1, keepdims=True) + eps)
    return (y * g) * (1.0 + scale) + shift


def _rope_store(acc, o_ref, cos, sa, sb, half):
    for j in range(acc.shape[1] // LANES):
        a = acc[:, j * LANES:(j + 1) * LANES]
        r = a * cos + pltpu.roll(a, half, 1) * sb + pltpu.roll(a, LANES - half, 1) * sa
        o_ref[:, j * LANES:(j + 1) * LANES] = r.astype(o_ref.dtype)


def _inproj_kernel(x_ref, g_ref, mod_ref, c64_ref, sa64_ref, sb64_ref, c32_ref, sa32_ref, sb32_ref, w_ref,
                   qa_ref, kka_ref, vva_ref, qi_ref, kki_ref, wi_ref,
                   qb_ref, kb_ref, vb_ref, qc_ref, kc_ref, vc_ref):
    h = _norm_mod(x_ref[...], g_ref[...], mod_ref[0:1, :], mod_ref[1:2, :]).astype(BF16)
    c64, sa64, sb64 = c64_ref[...], sa64_ref[...], sb64_ref[...]
    c32, sa32, sb32 = c32_ref[...], sa32_ref[...], sb32_ref[...]

    def proj(cols):
        return _dot(h, w_ref[:, cols[0]:cols[1]])

    qk_scale = HEAD_DIM ** -0.5 * LOG2E
    _rope_store(proj(_G_QA), qa_ref, c64 * qk_scale, sa64 * qk_scale, sb64 * qk_scale, 8)
    _rope_store(proj(_G_KKA), kka_ref, c64, sa64, sb64, 8)
    vva_ref[...] = proj(_G_VVA).astype(vva_ref.dtype)
    _rope_store(proj(_G_QI), qi_ref, c64, sa64, sb64, 8)
    _rope_store(proj(_G_KKI), kki_ref, c64, sa64, sb64, 8)
    wi_ref[...] = proj(_G_WI) * (IDX_HEADS ** -0.5 * HEAD_DIM ** -0.5)
    b_scale = B_QK_DIM ** -0.5 * LOG2E
    _rope_store(proj(_G_QB), qb_ref, c32 * b_scale, sa32 * b_scale, sb32 * b_scale, 4)
    _rope_store(proj(_G_KB), kb_ref, c32, sa32, sb32, 4)
    vb_ref[...] = proj(_G_VB).astype(vb_ref.dtype)
    _rope_store(proj(_G_QC), qc_ref, c64 * qk_scale, sa64 * qk_scale, sb64 * qk_scale, 8)
    _rope_store(proj(_G_KC), kc_ref, c64, sa64, sb64, 8)
    vc_ref[...] = proj(_G_VC).astype(vc_ref.dtype)


def inproj(x, g, mod, tabs, w_pad, layer, seq, tm=512):
    t, d = x.shape
    tiles_per_batch = seq // tm
    row = lambda i: (i, 0)
    widths = [A_WIDTH, LANES, LANES, IDX_WIDTH, LANES, LANES, B_WIDTH, B_WIDTH, B_WIDTH, C_WIDTH, C_WIDTH, C_WIDTH]
    dtypes = [BF16, BF16, BF16, BF16, BF16, F32, BF16, BF16, BF16, BF16, BF16, BF16]
    return pl.pallas_call(
        _inproj_kernel,
        grid=(t // tm,),
        in_specs=[pl.BlockSpec((tm, d), row),
                  pl.BlockSpec((1, d), lambda i: (0, 0)),
                  pl.BlockSpec((None, 6, d), lambda i: (i // tiles_per_batch, 0, 0))]
                 + [pl.BlockSpec((tm, LANES), row)] * 6
                 + [pl.BlockSpec((None, d, D_IN_PAD), lambda i: (layer, 0, 0))],
        out_specs=[pl.BlockSpec((tm, w), row) for w in widths],
        out_shape=[jax.ShapeDtypeStruct((t, w), dt) for w, dt in zip(widths, dtypes)],
        compiler_params=_params("parallel"),
        name="inproj",
    )(x, g, mod, *tabs, w_pad)


def _init_stats(m_ref, l_ref, acc_ref):
    m_ref[...] = jnp.full(m_ref.shape, -jnp.inf, F32)
    l_ref[...] = jnp.zeros(l_ref.shape, F32)
    acc_ref[...] = jnp.zeros(acc_ref.shape, F32)


def _key_to_f32(k):
    return lax.bitcast_convert_type(jnp.where(k >= 0, k, k ^ 0x7FFFFFFF), F32)


def dsa_attention(qa, qi, wi, kka, vva, kki, batch, seq, tq=256):
    t = qa.shape[0]
    nq = seq // tq
    k_top = min(DSA_TOPK_MAX, seq // 4)
    qrow = lambda b, i: (b * nq + i, 0)
    full = lambda b, i: (b, 0)
    return pl.pallas_call(
        functools.partial(_dsa_kernel_t, tq=tq, k_top=k_top, seq=seq),
        grid=(batch, nq),
        in_specs=[pl.BlockSpec((tq, A_WIDTH), qrow), pl.BlockSpec((tq, IDX_WIDTH), qrow),
                  pl.BlockSpec((tq, LANES), qrow),
                  pl.BlockSpec((seq, LANES), full), pl.BlockSpec((seq, LANES), full),
                  pl.BlockSpec((seq, LANES), full)],
        out_specs=pl.BlockSpec((tq, A_WIDTH), qrow),
        out_shape=jax.ShapeDtypeStruct((t, A_WIDTH), BF16),
        scratch_shapes=[pltpu.VMEM((seq, tq), F32), pltpu.VMEM((seq, tq), jnp.int16),
                        pltpu.VMEM((seq, tq), jnp.int16), pltpu.VMEM((1, LANES, seq), BF16),
                        pltpu.VMEM((3, 2 * tq, LANES), BF16),
                        pltpu.VMEM((3, 1, 2 * tq), F32), pltpu.VMEM((3, 1, 2 * tq), F32),
                        pltpu.VMEM((3, LANES, 2 * tq), F32)],
        compiler_params=_params("parallel", "arbitrary"),
        name="dsa_attention",
    )(qa, qi, wi, kka, vva, kki)


def diff_attention(qb, kb, vb, lam_vec, g_sub2, lam_init, batch, seq, tq=256):
    t = qb.shape[0]
    nq = seq // tq
    qrow = lambda b, i: (b * nq + i, 0)
    full = lambda b, i: (b, 0)
    const = lambda b, i: (0, 0)
    return pl.pallas_call(
        functools.partial(_diff_kernel_t, tq=tq, lam_init=lam_init, seq=seq),
        grid=(batch, nq),
        in_specs=[pl.BlockSpec((tq, B_WIDTH), qrow), pl.BlockSpec((seq, B_WIDTH), full),
                  pl.BlockSpec((seq, B_WIDTH), full),
                  pl.BlockSpec((4, B_QK_DIM), const), pl.BlockSpec((1, LANES), const)],
        out_specs=pl.BlockSpec((tq, B_WIDTH), qrow),
        out_shape=jax.ShapeDtypeStruct((t, B_WIDTH), BF16),
        scratch_shapes=[pltpu.VMEM((2, LANES, seq), BF16), pltpu.VMEM((2, 4 * tq, LANES), BF16),
                        pltpu.VMEM((2, 1, 4 * tq), F32), pltpu.VMEM((2, 1, 4 * tq), F32),
                        pltpu.VMEM((2, LANES, 4 * tq), F32)],
        compiler_params=_params("parallel", "arbitrary"),
        name="diff_attention",
    )(qb, kb, vb, lam_vec, g_sub2)


def moba_attention(qc, kc, vc, batch, seq):
    tq = MOBA_BLOCK
    t = qc.shape[0]
    nb = seq // tq
    n_sel = min(MOBA_TOPK, nb - 1)
    nbp = 8
    assert seq % tq == 0 and nb <= nbp
    qrow = lambda b, i: (b * nb + i, 0)
    full = lambda b, i: (b, 0)
    return pl.pallas_call(
        functools.partial(_moba_kernel_t, tq=tq, nb=nb, n_sel=n_sel, seq=seq),
        grid=(batch, nb),
        in_specs=[pl.BlockSpec((tq, C_WIDTH), qrow), pl.BlockSpec((seq, C_WIDTH), full),
                  pl.BlockSpec((seq, C_WIDTH), full)],
        out_specs=pl.BlockSpec((tq, C_WIDTH), qrow),
        out_shape=jax.ShapeDtypeStruct((t, C_WIDTH), BF16),
        scratch_shapes=[pltpu.VMEM((nbp, C_WIDTH), F32), pltpu.VMEM((3, LANES, seq), BF16),
                        pltpu.VMEM((3, 2 * tq, LANES), BF16), pltpu.VMEM((3, nbp, 2 * tq), F32),
                        pltpu.VMEM((3, 1, 2 * tq), F32), pltpu.VMEM((3, 1, 2 * tq), F32),
                        pltpu.VMEM((3, LANES, 2 * tq), F32)],
        compiler_params=_params("parallel", "arbitrary"),
        name="moba_attention",
    )(qc, kc, vc)


def _attend_t(k, q_ref, g, v_t, m_ref, l_ref, acc_ref, mask=None, scores_ahead=False):
    n_strips = q_ref.shape[1] // LANES

    def scores(j):
        cols = slice(j * LANES, (j + 1) * LANES)
        s = _dot_nt(k, q_ref[g, cols, :])
        if mask is not None:
            s = mask(j, s)
        m_prev = m_ref[g, :, cols]
        return s, m_prev, jnp.maximum(m_prev, jnp.max(s, axis=0, keepdims=True))

    ahead = scores(0) if scores_ahead else None
    for j in range(n_strips):
        cols = slice(j * LANES, (j + 1) * LANES)
        s, m_prev, m_new = ahead if scores_ahead else scores(j)
        if scores_ahead and j + 1 < n_strips:
            ahead = scores(j + 1)
        alpha = jnp.exp2(m_prev - m_new)
        p = jnp.exp2(s - m_new)
        l_ref[g, :, cols] = alpha * l_ref[g, :, cols] + jnp.sum(p, axis=0, keepdims=True)
        acc_ref[g, :, cols] = alpha * acc_ref[g, :, cols] + _dot(v_t, p.astype(BF16))
        m_ref[g, :, cols] = m_new


def _causal_mask_t(tk, tq):
    key = lax.broadcasted_iota(jnp.int32, (tk, LANES), 0)
    qry = lax.broadcasted_iota(jnp.int32, (tk, LANES), 1)

    def mask(j, s):
        return jnp.where(key <= qry + (j * LANES) % tq, s, NEG)
    return mask


def _transpose_values(v_ref, vt_ref, seq, tk):
    for g in range(vt_ref.shape[0]):
        for n in range(seq // tk):
            blk = v_ref[n * tk:(n + 1) * tk, g * LANES:(g + 1) * LANES].astype(F32)
            vt_ref[g, :, n * tk:(n + 1) * tk] = blk.T.astype(vt_ref.dtype)


def _head_pair_out(acc_t, l, tq):
    even = acc_t[0:HEAD_DIM, 0:tq] / l[:, 0:tq]
    odd = acc_t[HEAD_DIM:LANES, tq:2 * tq] / l[:, tq:2 * tq]
    return jnp.concatenate([even, odd], axis=0).T


def _dsa_kernel_t(qa_ref, qi_ref, wiq_ref, kka_ref, vva_ref, kki_ref, o_ref,
                  sc_ref, hi_ref, lo_ref, vt_ref, qs_ref, m_ref, l_ref, acc_ref, *, tq, k_top, seq):
    i = pl.program_id(1)
    nk = i + 1
    t0 = i * tq
    lo = lax.broadcasted_iota(jnp.int32, (tq, LANES), 1) < HEAD_DIM
    zero_b = jnp.zeros((tq, LANES), BF16)

    @pl.when(i == 0)
    def _():
        _transpose_values(vva_ref, vt_ref, seq, tq)

    def stack_heads(q):
        out = []
        for g in range(q.shape[1] // LANES):
            qg = q[:, g * LANES:(g + 1) * LANES]
            out += [jnp.where(lo, qg, zero_b), jnp.where(lo, zero_b, qg)]
        return out

    qa_stack = stack_heads(qa_ref[...])
    for g in range(3):
        qs_ref[g, 0:tq, :] = qa_stack[2 * g]
        qs_ref[g, tq:2 * tq, :] = qa_stack[2 * g + 1]
    qi_stack = jnp.concatenate(stack_heads(qi_ref[...]), axis=0)
    wi_t = wiq_ref[...].T

    def rows(c):
        return pl.ds(pl.multiple_of(c * tq, tq), tq)

    key_pos = lax.broadcasted_iota(jnp.int32, (tq, tq), 0)
    qry_pos = lax.broadcasted_iota(jnp.int32, (tq, tq), 1)

    def idx_body(c, carry):
        r = jnp.maximum(_dot_nt(kki_ref[rows(c), :], qi_stack), 0.0)
        s = wi_t[0:1, :] * r[:, 0:tq]
        for h in range(1, IDX_HEADS):
            s = s + wi_t[h:h + 1, :] * r[:, h * tq:(h + 1) * tq]
        causal = (c * tq + key_pos) <= (t0 + qry_pos)
        s = jnp.where(causal, s, -jnp.inf)
        s = jnp.where(jnp.abs(s) < MIN_NORMAL_F32, 0.0, s)
        sc_ref[rows(c), :] = s
        bits = lax.bitcast_convert_type(s, jnp.int32)
        key = jnp.where(bits >= 0, bits, bits ^ 0x7FFFFFFF)
        hi_ref[rows(c), :] = lax.shift_right_arithmetic(key, jnp.int32(16)).astype(jnp.int16)
        lo_ref[rows(c), :] = ((key & 0xFFFF) - 32768).astype(jnp.int16)
        return carry

    lax.fori_loop(0, nk, idx_body, 0)

    def count16(ref, cand, strict):
        c16 = cand.astype(jnp.int16)

        def body(c, acc):
            x = ref[rows(c), :]
            hit = jnp.where((x > c16) if strict else (x >= c16), jnp.int16(1), jnp.int16(0))
            for r in range(tq // 16):
                acc = acc + hit[r * 16:(r + 1) * 16]
            return acc
        acc = lax.fori_loop(0, nk, body, jnp.zeros((16, tq), jnp.int16))
        return jnp.sum(acc.astype(F32), axis=0, keepdims=True)

    def search16(ref, k_need):
        v0 = jnp.where(count16(ref, jnp.zeros((1, tq), jnp.int32), False) >= k_need, 0, -32768).astype(jnp.int32)

        def bisect(b, v):
            trial = v | lax.shift_left(jnp.int32(1), 14 - b)
            return jnp.where(count16(ref, trial, False) >= k_need, trial, v)
        return lax.fori_loop(0, 15, bisect, v0)

    kf = float(k_top)

    def search():
        hi_k = search16(hi_ref, kf)
        above = count16(hi_ref, hi_k, True)
        hi16 = hi_k.astype(jnp.int16)

        def keep_bucket(c, carry):
            lo_ref[rows(c), :] = jnp.where(hi_ref[rows(c), :] == hi16, lo_ref[rows(c), :], jnp.int16(-32768))
            return carry
        lax.fori_loop(0, nk, keep_bucket, 0)
        lo_k = search16(lo_ref, kf - above)
        n_gt = above + count16(lo_ref, lo_k, True)
        return lax.shift_left(hi_k, jnp.int32(16)) | (lo_k + 32768), kf - n_gt

    key, need = lax.cond(t0 + tq <= k_top,
                         lambda: (jnp.full((1, tq), INT_MIN, jnp.int32), jnp.full((1, tq), kf, F32)), search)
    thr = _key_to_f32(key)
    thr_next = _key_to_f32(jnp.where(key == 0, MIN_NORMAL_KEY, key + 1))
    all_sel = (t0 + lax.broadcasted_iota(jnp.int32, (1, tq), 1)) < k_top
    lower = (qry_pos <= key_pos).astype(BF16)

    _init_stats(m_ref, l_ref, acc_ref)

    def att_body(c, tie_run):
        s_idx = sc_ref[rows(c), :]
        ge = s_idx >= thr
        gt = s_idx >= thr_next
        tie = jnp.where(gt, 0.0, jnp.where(ge, 1.0, 0.0))
        prefix = _dot(lower, tie.astype(BF16)) + tie_run
        take = jnp.where(gt, 1.0, jnp.where(prefix <= need, tie, 0.0))
        take = jnp.where(all_sel, 1.0, take)
        take = jnp.where(s_idx > -jnp.inf, take, 0.0)

        def mask(j, s):
            q0 = (j * LANES) % tq
            return jnp.where(take[:, q0:q0 + LANES] > 0.5, s, NEG)

        kk = kka_ref[rows(c), :]
        vt = vt_ref[0, :, rows(c)]
        for g in range(3):
            _attend_t(kk, qs_ref, g, vt, m_ref, l_ref, acc_ref, mask)
        return tie_run + jnp.sum(tie, axis=0, keepdims=True)

    lax.fori_loop(0, nk, att_body, jnp.zeros((1, tq), F32))

    for g in range(3):
        o_ref[:, g * LANES:(g + 1) * LANES] = _head_pair_out(acc_ref[g], l_ref[g], tq).astype(o_ref.dtype)


def _diff_kernel_t(qb_ref, kb_ref, vb_ref, lam_ref, gsub_ref, o_ref,
                   vt_ref, qs_ref, m_ref, l_ref, acc_ref, *, tq, lam_init, seq):
    i = pl.program_id(1)
    lane = lax.broadcasted_iota(jnp.int32, (tq, LANES), 1)
    zero_b = jnp.zeros((tq, LANES), BF16)

    @pl.when(i == 0)
    def _():
        _transpose_values(vb_ref, vt_ref, seq, tq)

    qb = qb_ref[...]
    for g in range(2):
        qg = qb[:, g * LANES:(g + 1) * LANES]
        for j in range(4):
            qs_ref[g, j * tq:(j + 1) * tq, :] = jnp.where(lane // B_QK_DIM == j, qg, zero_b)
    _init_stats(m_ref, l_ref, acc_ref)

    def rows(c):
        return pl.ds(pl.multiple_of(c * tq, tq), tq)

    def step(c, masked):
        mask = _causal_mask_t(tq, tq) if masked else None
        for g in range(2):
            _attend_t(kb_ref[rows(c), g * LANES:(g + 1) * LANES], qs_ref, g, vt_ref[g, :, rows(c)],
                      m_ref, l_ref, acc_ref, mask, scores_ahead=True)

    def body(c, carry):
        step(c, False)
        return carry

    lax.fori_loop(0, i, body, 0)
    step(i, True)

    lv = lam_ref[...]
    lam = (jnp.exp(jnp.sum(lv[0:1] * lv[1:2], axis=1, keepdims=True))
           - jnp.exp(jnp.sum(lv[2:3] * lv[3:4], axis=1, keepdims=True)) + lam_init)
    gsub = gsub_ref[...]
    for g in range(2):
        acc_t, l = acc_ref[g], l_ref[g]

        def prob(r0, j):
            return acc_t[r0:r0 + HEAD_DIM, j * tq:(j + 1) * tq] / l[:, j * tq:(j + 1) * tq]

        halves = []
        for r0, j in ((0, 0), (HEAD_DIM, 2)):
            o = prob(r0, j) - lam * prob(r0, j + 1)
            ms = jnp.mean(o * o, axis=0, keepdims=True)
            halves.append(o * lax.rsqrt(ms + SUBLN_EPS))
        y = (jnp.concatenate(halves, axis=0).T * gsub) * (1.0 - lam_init)
        o_ref[:, g * LANES:(g + 1) * LANES] = y.astype(o_ref.dtype)


def _moba_kernel_t(qc_ref, kc_ref, vc_ref, o_ref, kmean_ref, vt_ref, qs_ref, bias_ref, m_ref, l_ref, acc_ref,
                   *, tq, nb, n_sel, seq):
    i = pl.program_id(1)
    nbp = kmean_ref.shape[0]
    lane = lax.broadcasted_iota(jnp.int32, (tq, LANES), 1)
    lo = lane < HEAD_DIM
    zero_b = jnp.zeros((tq, LANES), BF16)

    @pl.when(i == 0)
    def _():
        _transpose_values(vc_ref, vt_ref, seq, tq)
        kmean_ref[...] = jnp.zeros(kmean_ref.shape, F32)
        for n in range(nb):
            kblk = kc_ref[n * tq:(n + 1) * tq, :].astype(F32)
            kmean_ref[n:n + 1, :] = jnp.mean(kblk, axis=0, keepdims=True)

    qc = qc_ref[...]
    sub = lax.broadcasted_iota(jnp.int32, (nbp, 2 * tq), 0)
    past = sub < i
    for g in range(3):
        qg = qc[:, g * LANES:(g + 1) * LANES]
        q2 = jnp.concatenate([jnp.where(lo, qg, zero_b), jnp.where(lo, zero_b, qg)], axis=0)
        km = kmean_ref[:, g * LANES:(g + 1) * LANES]
        km_hi = km.astype(BF16)
        gt = _dot_nt(km_hi, q2) + _dot_nt((km - km_hi.astype(F32)).astype(BF16), q2)
        rows_ = []
        for n in range(nbp):
            gn = gt[n:n + 1, :]
            beats = jnp.where(sub < n, jnp.where(gt >= gn, 1.0, 0.0), jnp.where(gt > gn, 1.0, 0.0))
            beats = jnp.where(sub == n, 0.0, jnp.where(past, beats, 0.0))
            rank = jnp.sum(beats, axis=0, keepdims=True)
            rows_.append(jnp.where(rank < n_sel, 0.0, NEG))
        bias_ref[g] = jnp.where(past, jnp.concatenate(rows_, axis=0), NEG)
        qs_ref[g] = q2
    _init_stats(m_ref, l_ref, acc_ref)

    def rows(c):
        return pl.ds(pl.multiple_of(c * tq, tq), tq)

    def body(c, carry):
        for g in range(3):
            bias = bias_ref[g, pl.ds(c, 1), :]

            def mask(j, s):
                return s + bias[:, j * LANES:(j + 1) * LANES]

            _attend_t(kc_ref[rows(c), g * LANES:(g + 1) * LANES], qs_ref, g, vt_ref[g, :, rows(c)],
                      m_ref, l_ref, acc_ref, mask)
        return carry

    lax.fori_loop(0, i, body, 0)

    causal = _causal_mask_t(tq, tq)
    for g in range(3):
        _attend_t(kc_ref[rows(i), g * LANES:(g + 1) * LANES], qs_ref, g, vt_ref[g, :, rows(i)],
                  m_ref, l_ref, acc_ref, causal)
        o_ref[:, g * LANES:(g + 1) * LANES] = _head_pair_out(acc_ref[g], l_ref[g], tq).astype(o_ref.dtype)


def _route(h, wr_ref, cw_ref, meta_ref, cnt_ref):
    hi = h.astype(BF16)
    lo = (h - hi.astype(F32)).astype(BF16)
    logits = _dot(hi, wr_ref[0]) + (_dot(lo, wr_ref[0]) + _dot(hi, wr_ref[1]))
    lane = lax.broadcasted_iota(jnp.int32, logits.shape, 1)
    lg = jnp.where(lane < N_EXPERTS, logits, -jnp.inf)
    v0 = jnp.max(lg, axis=1, keepdims=True)
    i0 = jnp.min(jnp.where(lg == v0, lane, LANES), axis=1, keepdims=True)
    lg1 = jnp.where(lane == i0, -jnp.inf, lg)
    v1 = jnp.max(lg1, axis=1, keepdims=True)
    i1 = jnp.min(jnp.where(lg1 == v1, lane, LANES), axis=1, keepdims=True)
    e1 = jnp.exp(v1 - v0)
    w0 = 1.0 / (1.0 + e1)
    tm = logits.shape[0]
    routed = jnp.where(lane == i0, 1.0, 0.0) + jnp.where(lane == i1, 1.0, 0.0)
    earlier_rows = (lax.broadcasted_iota(jnp.int32, (tm, tm), 1)
                    < lax.broadcasted_iota(jnp.int32, (tm, tm), 0)).astype(BF16)

    @pl.when(pl.program_id(0) == 0)
    def _():
        cnt_ref[...] = jnp.zeros(cnt_ref.shape, F32)

    before = _dot(earlier_rows, routed.astype(BF16)) + cnt_ref[0:1, :]
    r0 = jnp.sum(jnp.where(lane == i0, before, 0.0), axis=1, keepdims=True)
    r1 = jnp.sum(jnp.where(lane == i1, before, 0.0), axis=1, keepdims=True)
    cnt_ref[0:1, :] = cnt_ref[0:1, :] + jnp.sum(routed, axis=0, keepdims=True)
    cw = (jnp.where(lane == 0, i0.astype(F32), 0.0) + jnp.where(lane == 1, i1.astype(F32), 0.0)
          + jnp.where(lane == 2, w0, 0.0) + jnp.where(lane == 3, e1 * w0, 0.0)
          + jnp.where(lane == 4, r0, 0.0) + jnp.where(lane == 5, r1, 0.0))
    cw_ref[...] = cw
    meta_ref[...] = cw.T[0:8, :].astype(jnp.int32)


def _mixer_out(x_ref, ya_ref, yb_ref, yc_ref, wo_ref, g_ref, mod_ref):
    ab = A_WIDTH + B_WIDTH
    y = (_dot(ya_ref[...], wo_ref[0:A_WIDTH, :]) + _dot(yb_ref[...], wo_ref[A_WIDTH:ab, :])
         + _dot(yc_ref[...], wo_ref[ab:ab + C_WIDTH, :]))
    x1 = x_ref[...] + mod_ref[2:3, :] * y
    return x1, _norm_mod(x1, g_ref[...], mod_ref[3:4, :], mod_ref[4:5, :])


def _mixer_out_specs(tm, d, layer, tiles_per_batch):
    row = lambda i: (i, 0)
    return [pl.BlockSpec((tm, d), row), pl.BlockSpec((tm, A_WIDTH), row), pl.BlockSpec((tm, B_WIDTH), row),
            pl.BlockSpec((tm, C_WIDTH), row), pl.BlockSpec((None, d, d), lambda i: (layer, 0, 0)),
            pl.BlockSpec((1, d), lambda i: (0, 0)),
            pl.BlockSpec((None, 6, d), lambda i: (i // tiles_per_batch, 0, 0))]


def _outproj_kernel(x_ref, ya_ref, yb_ref, yc_ref, wo_ref, g_ref, mod_ref, wr_ref,
                    x1_ref, h_ref, cw_ref, meta_ref, cnt_ref):
    x1, h = _mixer_out(x_ref, ya_ref, yb_ref, yc_ref, wo_ref, g_ref, mod_ref)
    x1_ref[...] = x1
    h_ref[...] = _pack_rows(h)
    _route(h, wr_ref, cw_ref, meta_ref, cnt_ref)


def outproj_router(x, ya, yb, yc, wo, layer, g_ffn, mod, seq, w_router, tm=512):
    t, d = x.shape
    row = lambda i: (i, 0)
    return pl.pallas_call(
        _outproj_kernel,
        grid=(t // tm,),
        in_specs=_mixer_out_specs(tm, d, layer, seq // tm) + [pl.BlockSpec((2, d, LANES), lambda i: (0, 0, 0))],
        out_specs=[pl.BlockSpec((tm, d), row), pl.BlockSpec((tm, d // 2), row), pl.BlockSpec((tm, LANES), row),
                   pl.BlockSpec((8, tm), lambda i: (0, i)), pl.BlockSpec((8, LANES), lambda i: (0, 0))],
        out_shape=[jax.ShapeDtypeStruct((t, d), F32), jax.ShapeDtypeStruct((t, d // 2), jnp.int32),
                   jax.ShapeDtypeStruct((t, LANES), F32), jax.ShapeDtypeStruct((8, t), jnp.int32),
                   jax.ShapeDtypeStruct((8, LANES), F32)],
        compiler_params=_params("arbitrary"),
        name="outproj_router",
    )(x, ya, yb, yc, wo, g_ffn, mod, w_router)


W_ROWS_IN = 128
W_ROWS_DOWN = 352
FF_SPLIT = 2


def _load_weights_bf16(layer, e, w_hbms, w_bs, st_in, st_dn, sems):
    slots = st_in.shape[0]
    plan = []
    for w_hbm, w_b, rb, st, sem0 in ((w_hbms[0], w_bs[0], W_ROWS_IN, st_in, 0), (w_hbms[1], w_bs[1], W_ROWS_IN, st_in, 0),
                                     (w_hbms[2], w_bs[2], W_ROWS_DOWN, st_dn, slots)):
        assert w_b.shape[0] % rb == 0
        plan += [(w_hbm, w_b, r0, rb, st, sem0) for r0 in range(0, w_b.shape[0], rb)]
    uses = {0: 0, slots: 0}
    copies = []
    for w_hbm, w_b, r0, rb, st, sem0 in plan:
        slot = uses[sem0] % slots
        uses[sem0] += 1
        copies.append((pltpu.make_async_copy(w_hbm.at[layer, e, pl.ds(r0, rb), :], st.at[slot],
                                             sems.at[sem0 + slot]), st, slot, w_b, r0, rb))
    ahead = slots - 1
    for copy in copies[:ahead]:
        copy[0].start()
    for b, (copy, st, slot, w_b, r0, rb) in enumerate(copies):
        if b + ahead < len(copies):
            copies[b + ahead][0].start()
        copy.wait()
        w_b[r0:r0 + rb, :] = st[slot].astype(BF16)


def _swiglu_resident(x, wg_b, wu_b, wd_b):
    tf = wg_b.shape[1] // FF_SPLIT
    acc = None
    for f in range(FF_SPLIT):
        cols = slice(f * tf, (f + 1) * tf)
        a = _dot(x, wg_b[:, cols])
        u = _dot(x, wu_b[:, cols])
        act = (a * (1.0 / (1.0 + jnp.exp(-a)))) * u
        part = _dot(act.astype(BF16), wd_b[cols, :])
        acc = part if acc is None else acc + part
    return acc


def _weight_scratch(d, dff, slots):
    return [pltpu.VMEM((d, dff), BF16), pltpu.VMEM((d, dff), BF16), pltpu.VMEM((dff, d), BF16),
            pltpu.VMEM((slots, W_ROWS_IN, dff), F32), pltpu.VMEM((slots, W_ROWS_DOWN, d), F32),
            pltpu.SemaphoreType.DMA((2 * slots,))]


def _layer_out(x, gfin_ref, final):
    if not final:
        return x
    return (x * lax.rsqrt(jnp.mean(x * x, axis=-1, keepdims=True) + EPS)) * gfin_ref[...]


def _ffn_kernel(x_ref, ya_ref, yb_ref, yc_ref, wo_ref, g_ref, mod_ref, gfin_ref, wg_hbm, wu_hbm, wd_hbm, o_ref,
                wg_b, wu_b, wd_b, st_in, st_dn, sems, *, layer, final):
    @pl.when(pl.program_id(0) == 0)
    def _():
        _load_weights_bf16(layer, 0, (wg_hbm, wu_hbm, wd_hbm), (wg_b, wu_b, wd_b), st_in, st_dn, sems)

    x1, h = _mixer_out(x_ref, ya_ref, yb_ref, yc_ref, wo_ref, g_ref, mod_ref)
    out = x1 + mod_ref[5:6, :] * _swiglu_resident(h.astype(BF16), wg_b, wu_b, wd_b)
    o_ref[...] = _layer_out(out, gfin_ref, final)


def outproj_ffn_dense(x, ya, yb, yc, wo, layer, g_ffn, mod, g_final, final, w_gate, w_up, w_down, ff_layer, seq,
                      tm=512):
    t, d = x.shape
    hbm = pl.BlockSpec(memory_space=pl.ANY)
    return pl.pallas_call(
        functools.partial(_ffn_kernel, layer=ff_layer, final=final),
        grid=(t // tm,),
        in_specs=_mixer_out_specs(tm, d, layer, seq // tm) + [pl.BlockSpec((1, d), lambda i: (0, 0)), hbm, hbm, hbm],
        out_specs=pl.BlockSpec((tm, d), lambda i: (i, 0)),
        out_shape=jax.ShapeDtypeStruct((t, d), F32),
        scratch_shapes=_weight_scratch(d, w_gate.shape[3], 2),
        compiler_params=_params("arbitrary"),
        name="ffn_dense",
    )(x, ya, yb, yc, wo, g_ffn, mod, g_final, w_gate, w_up, w_down)


MOE_TILE = 512
MOE_PARTS = 3
SC_CORES, SC_SUBCORES = 2, 16
SC_ROWS = 64
HI16 = -65536


def _pack_rows(x):
    c = x.shape[1] // 2
    bits = lax.bitcast_convert_type(x.astype(jnp.bfloat16).astype(F32), jnp.int32)
    return lax.shift_right_logical(bits[:, :c], jnp.int32(16)) | (bits[:, c:] & jnp.int32(HI16))


def _unpack_rows(w):
    lo = lax.bitcast_convert_type(lax.shift_left(w, jnp.int32(16)), F32)
    hi = lax.bitcast_convert_type(w & jnp.int32(HI16), F32)
    return jnp.concatenate([lo, hi], axis=1)


def sc_gather_rows(table, idx):
    d = table.shape[1]
    b = idx.shape[0]
    per_worker = b // (SC_CORES * SC_SUBCORES)
    assert per_worker * SC_CORES * SC_SUBCORES == b and per_worker % (2 * SC_ROWS) == 0
    mesh = plsc.VectorSubcoreMesh(core_axis_name="c", subcore_axis_name="s")
    idx_buf = pltpu.VMEM((SC_ROWS,), jnp.int32)
    row_buf = pltpu.VMEM((SC_ROWS, d), table.dtype)

    @functools.partial(
        pl.kernel, mesh=mesh, out_type=jax.ShapeDtypeStruct((b, d), table.dtype),
        scratch_types=[idx_buf, idx_buf, row_buf, row_buf] + [pltpu.SemaphoreType.DMA] * 4,
        name="sc_gather_rows")
    def gather(table_hbm, idx_hbm, out_hbm, idx0, idx1, rows0, rows1, sem_g0, sem_g1, sem_w0, sem_w1):
        base = (lax.axis_index("s") * SC_CORES + lax.axis_index("c")) * per_worker

        @pl.loop(0, per_worker // (2 * SC_ROWS))
        def _(pair):
            off0 = pl.multiple_of(base + pair * (2 * SC_ROWS), SC_ROWS)
            off1 = pl.multiple_of(off0 + SC_ROWS, SC_ROWS)
            pltpu.sync_copy(idx_hbm.at[pl.ds(off0, SC_ROWS)], idx0)
            pltpu.sync_copy(idx_hbm.at[pl.ds(off1, SC_ROWS)], idx1)
            gather0 = pltpu.async_copy(table_hbm.at[idx0], rows0, sem_g0)
            gather1 = pltpu.async_copy(table_hbm.at[idx1], rows1, sem_g1)
            gather0.wait()
            write0 = pltpu.async_copy(rows0, out_hbm.at[pl.ds(off0, SC_ROWS)], sem_w0)
            gather1.wait()
            write1 = pltpu.async_copy(rows1, out_hbm.at[pl.ds(off1, SC_ROWS)], sem_w1)
            write0.wait()
            write1.wait()

    return gather(table, idx)


def _moe_layout(meta, counts, n_exp):
    t = meta.shape[1]
    e0, e1, rank0, rank1 = meta[0], meta[1], meta[4], meta[5]
    padded = (counts + MOE_TILE - 1) // MOE_TILE * MOE_TILE
    ends = jnp.cumsum(padded)
    offs = ends - padded
    pos0 = offs[e0] + rank0
    pos1 = offs[e1] + rank1
    n_rows = 2 * t + n_exp * MOE_TILE
    tok = jnp.arange(t, dtype=jnp.int32)
    src = jnp.zeros((n_rows,), jnp.int32).at[jnp.concatenate([pos0, pos1])].set(
        jnp.concatenate([tok, tok]), unique_indices=True, mode="promise_in_bounds")
    n_tiles = n_rows // MOE_TILE
    n_valid = ends[-1] // MOE_TILE
    tile_expert = jnp.sum(jnp.arange(n_tiles)[:, None] * MOE_TILE >= ends[None, :], axis=1)
    tile_expert = tile_expert[jnp.minimum(jnp.arange(n_tiles), n_valid - 1)].astype(jnp.int32)
    return pos0, pos1, src, tile_expert, n_valid.astype(jnp.int32).reshape(1)


W_SLOTS = 4


def _experts_packed_kernel(te_ref, nv_ref, first_ref, x_ref, wg_hbm, wu_hbm, wd_hbm, *rest, layer):
    o_ref, wg_b, wu_b, wd_b, st_in, st_dn, sems = rest[-7:]
    j = pl.program_id(0)
    valid = j < nv_ref[0]

    @pl.when(valid & (first_ref[j] == 1))
    def _():
        _load_weights_bf16(layer, te_ref[j], (wg_hbm, wu_hbm, wd_hbm), (wg_b, wu_b, wd_b), st_in, st_dn, sems)

    @pl.when(valid)
    def _():
        o_ref[...] = _pack_rows(_swiglu_resident(_unpack_rows(x_ref[...]).astype(BF16), wg_b, wu_b, wd_b))


def moe_experts_packed(x_part, tile_expert, n_valid, w_gate, w_up, w_down, layer, y_prev, part, n_rows):
    rows_part, half = x_part.shape
    d = 2 * half
    dff = w_gate.shape[3]
    tiles_part = rows_part // MOE_TILE
    first = jnp.concatenate([jnp.ones((1,), jnp.int32),
                             (tile_expert[1:] != tile_expert[:-1]).astype(jnp.int32)])

    def tile(j, nv):
        return jnp.maximum(jnp.minimum(j, nv[0] - 1), 0)

    hbm = pl.BlockSpec(memory_space=pl.ANY)
    in_specs = [pl.BlockSpec((MOE_TILE, half), lambda j, te, nv, fi: (tile(j, nv), 0)), hbm, hbm, hbm]
    args = [tile_expert, n_valid, first, x_part, w_gate, w_up, w_down]
    aliases = {}
    if y_prev is not None:
        in_specs.append(hbm)
        args.append(y_prev)
        aliases = {len(args) - 1: 0}
    return pl.pallas_call(
        functools.partial(_experts_packed_kernel, layer=layer),
        grid_spec=pltpu.PrefetchScalarGridSpec(
            num_scalar_prefetch=3, grid=(tiles_part,), in_specs=in_specs,
            out_specs=pl.BlockSpec((MOE_TILE, half), lambda j, te, nv, fi: (part * tiles_part + tile(j, nv), 0)),
            scratch_shapes=_weight_scratch(d, dff, W_SLOTS)),
        out_shape=jax.ShapeDtypeStruct((n_rows, half), jnp.int32),
        input_output_aliases=aliases,
        compiler_params=_params("arbitrary"),
        name="moe_experts",
    )(*args)


def _combine_packed_kernel(y0_ref, y1_ref, route_ref, x1_ref, mod_ref, gfin_ref, o_ref, *, final):
    rt = route_ref[...]
    f = rt[:, 2:3] * _unpack_rows(y0_ref[...]) + rt[:, 3:4] * _unpack_rows(y1_ref[...])
    o_ref[...] = _layer_out(x1_ref[...] + mod_ref[5:6, :] * f, gfin_ref, final)


def moe_combine_packed(y_pairs, route, x1, mod, g_final, final, seq, tm=512):
    t, d = x1.shape
    nt = t // tm
    tiles_per_batch = seq // tm
    row = lambda i: (i, 0)
    return pl.pallas_call(
        functools.partial(_combine_packed_kernel, final=final),
        grid=(nt,),
        in_specs=[pl.BlockSpec((tm, d // 2), row), pl.BlockSpec((tm, d // 2), lambda i: (i + nt, 0)),
                  pl.BlockSpec((tm, LANES), row), pl.BlockSpec((tm, d), row),
                  pl.BlockSpec((None, 6, d), lambda i: (i // tiles_per_batch, 0, 0)),
                  pl.BlockSpec((1, d), lambda i: (0, 0))],
        out_specs=pl.BlockSpec((tm, d), row),
        out_shape=jax.ShapeDtypeStruct((t, d), F32),
        compiler_params=_params("parallel"),
        name="moe_combine",
    )(y_pairs, y_pairs, route, x1, mod, g_final)


def ffn_moe_sc(x1, h_packed, route, meta, counts, mod, g_final, final, w_gate, w_up, w_down, layer, seq):
    n_exp = w_gate.shape[1]
    pos0, pos1, src, tile_expert, n_valid = _moe_layout(meta, counts[0, :n_exp].astype(jnp.int32), n_exp)
    n_rows = src.shape[0]
    rows_part = n_rows // MOE_PARTS
    tiles_part = rows_part // MOE_TILE
    assert rows_part * MOE_PARTS == n_rows and tiles_part * MOE_TILE == rows_part
    h_parts = [sc_gather_rows(h_packed, src[p * rows_part:(p + 1) * rows_part]) for p in range(MOE_PARTS)]
    y_sorted = None
    for p in range(MOE_PARTS):
        n_valid_part = jnp.clip(n_valid - p * tiles_part, 0, tiles_part)
        y_sorted = moe_experts_packed(h_parts[p], tile_expert[p * tiles_part:(p + 1) * tiles_part], n_valid_part,
                                      w_gate, w_up, w_down, layer, y_sorted, p, n_rows)
    y_pairs = sc_gather_rows(y_sorted, jnp.concatenate([pos0, pos1]))
    return moe_combine_packed(y_pairs, route, x1, mod, g_final, final, seq)


def _rope_tables(positions, dim):
    rot = dim // ROPE_FRACTION
    half = rot // 2
    inv = 1.0 / (ROPE_THETA ** (np.arange(0, rot, 2, dtype=np.float32) / rot))
    ang = positions.reshape(-1).astype(F32)[:, None] * jnp.asarray(inv, F32)
    cos, sin = jnp.cos(ang), jnp.sin(ang)
    t = ang.shape[0]
    ones = jnp.ones((t, dim - rot), F32)
    zeros = lambda w: jnp.zeros((t, w), F32)
    reps = LANES // dim
    c = jnp.tile(jnp.concatenate([cos, cos, ones], axis=1), (1, reps))
    sa = jnp.tile(jnp.concatenate([-sin, zeros(dim - half)], axis=1), (1, reps))
    sb = jnp.tile(jnp.concatenate([zeros(half), sin, zeros(dim - rot)], axis=1), (1, reps))
    return c, sa, sb


def _relayout_w_in(w):
    pts = np.cumsum([0, 384, 64, 64, 256, 64, 4, 256, 256, 256, 384, 384, 384])
    (q_a, k_a, v_a, q_i, k_i, w_i, q_b, k_b, v_b, q_c, k_c, v_c) = [w[..., pts[j]:pts[j + 1]] for j in range(12)]
    w_i_pad = jnp.concatenate([w_i, jnp.zeros(w.shape[:-1] + (LANES - IDX_HEADS,), w.dtype)], axis=-1)
    return jnp.concatenate([q_a, k_a, k_a, v_a, v_a, q_i, k_i, k_i, w_i_pad,
                            q_b, k_b, v_b, q_c, k_c, v_c], axis=-1).astype(BF16)


def kernel(x, c, positions, w_in, w_out, diff_lambda, diff_subln, w_ada, b_ada, g_attn, g_ffn, w_ff_gate,
           w_ff_up, w_ff_down, w_router, w_exp_gate, w_exp_up, w_exp_down, g_final):
    batch, seq, d = x.shape
    depth = w_in.shape[0]
    t = batch * seq
    tabs = _rope_tables(positions, HEAD_DIM) + _rope_tables(positions, B_QK_DIM)
    mod_all = adaln_mod(c, w_ada, b_ada).reshape(depth, batch, 6, d)
    w_in_pad = _relayout_w_in(w_in)
    wo = w_out.astype(BF16)
    g_fin = g_final.reshape(1, d)
    xf = x.reshape(t, d)
    for layer in range(depth):
        mod = mod_all[layer]
        lam_init = 0.8 - 0.6 * math.exp(-0.3 * layer)
        (qa, kka, vva, qi, kki, wi, qb, kb, vb, qc, kc, vc) = inproj(
            xf, g_attn[layer].reshape(1, d), mod, tabs, w_in_pad, layer, seq)
        ya = dsa_attention(qa, qi, wi, kka, vva, kki, batch, seq)
        g_sub2 = jnp.tile(diff_subln[layer], 2).reshape(1, LANES)
        yb = diff_attention(qb, kb, vb, diff_lambda[layer], g_sub2, lam_init, batch, seq)
        yc = moba_attention(qc, kc, vc, batch, seq)
        j = layer // 2
        gf = g_ffn[layer].reshape(1, d)
        final = layer == depth - 1
        if layer % 2 == 0:
            xf = outproj_ffn_dense(xf, ya, yb, yc, wo, layer, gf, mod, g_fin, final,
                                   w_ff_gate[:, None], w_ff_up[:, None], w_ff_down[:, None], j, seq)
        else:
            wr = jnp.concatenate([w_router[j], jnp.zeros((d, LANES - N_EXPERTS), F32)], axis=1)
            wr_hi = wr.astype(BF16)
            wr = jnp.stack([wr_hi, (wr - wr_hi.astype(F32)).astype(BF16)])
            x1, h_packed, route, meta, counts = outproj_router(xf, ya, yb, yc, wo, layer, gf, mod, seq, wr)
            xf = ffn_moe_sc(x1, h_packed, route, meta, counts, mod, g_fin, final,
                            w_exp_gate, w_exp_up, w_exp_down, j, seq)
    return xf.reshape(batch, seq, d)
```

```python
import functools
import math

import jax
import jax.numpy as jnp
import numpy as np
from jax import lax
from jax.experimental import pallas as pl
from jax.experimental.pallas import tpu as pltpu
from jax.experimental.pallas import tpu_sc as plsc

F32 = jnp.float32
BF16 = jnp.bfloat16

HEAD_DIM = 64
A_HEADS = 6
IDX_HEADS = 4
B_HEADS = 4
B_QK_DIM = 32
C_HEADS = 6
A_WIDTH, IDX_WIDTH = A_HEADS * HEAD_DIM, IDX_HEADS * HEAD_DIM
B_WIDTH, C_WIDTH = B_HEADS * HEAD_DIM, C_HEADS * HEAD_DIM
DSA_TOPK_MAX = 256
MOBA_BLOCK = 256
MOBA_TOPK = 3
ROPE_THETA = 500000.0
ROPE_FRACTION = 4
SUBLN_EPS = 1e-5
EPS = 1e-6
N_EXPERTS = 8

LANES = 128
NEG = -1e30
INT_MIN = -2 ** 31
MIN_NORMAL_KEY = 0x00800000
MIN_NORMAL_F32 = float(np.float32(2.0 ** -126))
VMEM_LIMIT = 48 * 1024 * 1024
LOG2E = math.log2(math.e)

_G_QA, _G_KKA, _G_VVA, _G_QI, _G_KKI, _G_WI = (0, 384), (384, 512), (512, 640), (640, 896), (896, 1024), (1024, 1152)
_G_QB, _G_KB, _G_VB = (1152, 1408), (1408, 1664), (1664, 1920)
_G_QC, _G_KC, _G_VC = (1920, 2304), (2304, 2688), (2688, 3072)
D_IN_PAD = 3072


def _params(*sem):
    return pltpu.CompilerParams(dimension_semantics=sem, vmem_limit_bytes=VMEM_LIMIT)


def _dot(a, b):
    return jnp.dot(a, b, preferred_element_type=F32)


def _dot_nt(a, b):
    return lax.dot_general(a, b, (((1,), (1,)), ((), ())), preferred_element_type=F32)


def _adaln_kernel(c_ref, w_ref, b_ref, o_ref):
    c = c_ref[...]
    c_act = c * (1.0 / (1.0 + jnp.exp(-c)))
    o_ref[...] = jnp.dot(c_act, w_ref[...], preferred_element_type=F32,
                         precision=lax.Precision.HIGHEST) + b_ref[...]


def adaln_mod(c, w_ada, b_ada, tn=1536):
    depth, d, n = w_ada.shape
    b = c.shape[0]
    return pl.pallas_call(
        _adaln_kernel,
        grid=(depth, n // tn),
        in_specs=[pl.BlockSpec((b, d), lambda l, j: (0, 0)),
                  pl.BlockSpec((None, d, tn), lambda l, j: (l, 0, j)),
                  pl.BlockSpec((None, 1, tn), lambda l, j: (l, 0, j))],
        out_specs=pl.BlockSpec((None, b, tn), lambda l, j: (l, 0, j)),
        out_shape=jax.ShapeDtypeStruct((depth, b, n), F32),
        compiler_params=_params("parallel", "parallel"),
        name="adaln_mod",
    )(c, w_ada, b_ada.reshape(depth, 1, n))


def _norm_mod(x, g, shift, scale, eps=EPS):
    y = x * lax.rsqrt(jnp.mean(x * x, axis=-1, keepdims=True) + eps)
    return (y * g) * (1.0 + scale) + shift


def _rope_store(acc, o_ref, cos, sa, sb, half):
    for j in range(acc.shape[1] // LANES):
        a = acc[:, j * LANES:(j + 1) * LANES]
        r = a * cos + pltpu.roll(a, half, 1) * sb + pltpu.roll(a, LANES - half, 1) * sa
        o_ref[:, j * LANES:(j + 1) * LANES] = r.astype(o_ref.dtype)


def _inproj_kernel(x_ref, g_ref, mod_ref, c64_ref, sa64_ref, sb64_ref, c32_ref, sa32_ref, sb32_ref, w_ref,
                   qa_ref, kka_ref, vva_ref, qi_ref, kki_ref, wi_ref,
                   qb_ref, kb_ref, vb_ref, qc_ref, kc_ref, vc_ref):
    h = _norm_mod(x_ref[...], g_ref[...], mod_ref[0:1, :], mod_ref[1:2, :]).astype(BF16)
    c64, sa64, sb64 = c64_ref[...], sa64_ref[...], sb64_ref[...]
    c32, sa32, sb32 = c32_ref[...], sa32_ref[...], sb32_ref[...]

    def proj(cols):
        return _dot(h, w_ref[:, cols[0]:cols[1]])

    qk_scale = HEAD_DIM ** -0.5 * LOG2E
    _rope_store(proj(_G_QA), qa_ref, c64 * qk_scale, sa64 * qk_scale, sb64 * qk_scale, 8)
    _rope_store(proj(_G_KKA), kka_ref, c64, sa64, sb64, 8)
    vva_ref[...] = proj(_G_VVA).astype(vva_ref.dtype)
    _rope_store(proj(_G_QI), qi_ref, c64, sa64, sb64, 8)
    _rope_store(proj(_G_KKI), kki_ref, c64, sa64, sb64, 8)
    wi_ref[...] = proj(_G_WI) * (IDX_HEADS ** -0.5 * HEAD_DIM ** -0.5)
    b_scale = B_QK_DIM ** -0.5 * LOG2E
    _rope_store(proj(_G_QB), qb_ref, c32 * b_scale, sa32 * b_scale, sb32 * b_scale, 4)
    _rope_store(proj(_G_KB), kb_ref, c32, sa32, sb32, 4)
    vb_ref[...] = proj(_G_VB).astype(vb_ref.dtype)
    _rope_store(proj(_G_QC), qc_ref, c64 * qk_scale, sa64 * qk_scale, sb64 * qk_scale, 8)
    _rope_store(proj(_G_KC), kc_ref, c64, sa64, sb64, 8)
    vc_ref[...] = proj(_G_VC).astype(vc_ref.dtype)


def inproj(x, g, mod, tabs, w_pad, layer, seq, tm=512):
    t, d = x.shape
    tiles_per_batch = seq // tm
    row = lambda i: (i, 0)
    widths = [A_WIDTH, LANES, LANES, IDX_WIDTH, LANES, LANES, B_WIDTH, B_WIDTH, B_WIDTH, C_WIDTH, C_WIDTH, C_WIDTH]
    dtypes = [BF16, BF16, BF16, BF16, BF16, F32, BF16, BF16, BF16, BF16, BF16, BF16]
    return pl.pallas_call(
        _inproj_kernel,
        grid=(t // tm,),
        in_specs=[pl.BlockSpec((tm, d), row),
                  pl.BlockSpec((1, d), lambda i: (0, 0)),
                  pl.BlockSpec((None, 6, d), lambda i: (i // tiles_per_batch, 0, 0))]
                 + [pl.BlockSpec((tm, LANES), row)] * 6
                 + [pl.BlockSpec((None, d, D_IN_PAD), lambda i: (layer, 0, 0))],
        out_specs=[pl.BlockSpec((tm, w), row) for w in widths],
        out_shape=[jax.ShapeDtypeStruct((t, w), dt) for w, dt in zip(widths, dtypes)],
        compiler_params=_params("parallel"),
        name="inproj",
    )(x, g, mod, *tabs, w_pad)


def _init_stats(m_ref, l_ref, acc_ref):
    m_ref[...] = jnp.full(m_ref.shape, -jnp.inf, F32)
    l_ref[...] = jnp.zeros(l_ref.shape, F32)
    acc_ref[...] = jnp.zeros(acc_ref.shape, F32)


def _key_to_f32(k):
    return lax.bitcast_convert_type(jnp.where(k >= 0, k, k ^ 0x7FFFFFFF), F32)


def dsa_attention(qa, qi, wi, kka, vva, kki, batch, seq, tq=256):
    t = qa.shape[0]
    nq = seq // tq
    k_top = min(DSA_TOPK_MAX, seq // 4)
    qrow = lambda b, i: (b * nq + i, 0)
    full = lambda b, i: (b, 0)
    return pl.pallas_call(
        functools.partial(_dsa_kernel_t, tq=tq, k_top=k_top, seq=seq),
        grid=(batch, nq),
        in_specs=[pl.BlockSpec((tq, A_WIDTH), qrow), pl.BlockSpec((tq, IDX_WIDTH), qrow),
                  pl.BlockSpec((tq, LANES), qrow),
                  pl.BlockSpec((seq, LANES), full), pl.BlockSpec((seq, LANES), full),
                  pl.BlockSpec((seq, LANES), full)],
        out_specs=pl.BlockSpec((tq, A_WIDTH), qrow),
        out_shape=jax.ShapeDtypeStruct((t, A_WIDTH), BF16),
        scratch_shapes=[pltpu.VMEM((seq, tq), F32), pltpu.VMEM((seq, tq), jnp.int16),
                        pltpu.VMEM((seq, tq), jnp.int16), pltpu.VMEM((1, LANES, seq), BF16),
                        pltpu.VMEM((3, 2 * tq, LANES), BF16),
                        pltpu.VMEM((3, 1, 2 * tq), F32), pltpu.VMEM((3, 1, 2 * tq), F32),
                        pltpu.VMEM((3, LANES, 2 * tq), F32)],
        compiler_params=_params("parallel", "arbitrary"),
        name="dsa_attention",
    )(qa, qi, wi, kka, vva, kki)


def diff_attention(qb, kb, vb, lam_vec, g_sub2, lam_init, batch, seq, tq=256):
    t = qb.shape[0]
    nq = seq // tq
    qrow = lambda b, i: (b * nq + i, 0)
    full = lambda b, i: (b, 0)
    const = lambda b, i: (0, 0)
    return pl.pallas_call(
        functools.partial(_diff_kernel_t, tq=tq, lam_init=lam_init, seq=seq),
        grid=(batch, nq),
        in_specs=[pl.BlockSpec((tq, B_WIDTH), qrow), pl.BlockSpec((seq, B_WIDTH), full),
                  pl.BlockSpec((seq, B_WIDTH), full),
                  pl.BlockSpec((4, B_QK_DIM), const), pl.BlockSpec((1, LANES), const)],
        out_specs=pl.BlockSpec((tq, B_WIDTH), qrow),
        out_shape=jax.ShapeDtypeStruct((t, B_WIDTH), BF16),
        scratch_shapes=[pltpu.VMEM((2, LANES, seq), BF16), pltpu.VMEM((2, 4 * tq, LANES), BF16),
                        pltpu.VMEM((2, 1, 4 * tq), F32), pltpu.VMEM((2, 1, 4 * tq), F32),
                        pltpu.VMEM((2, LANES, 4 * tq), F32)],
        compiler_params=_params("parallel", "arbitrary"),
        name="diff_attention",
    )(qb, kb, vb, lam_vec, g_sub2)


def moba_attention(qc, kc, vc, batch, seq):
    tq = MOBA_BLOCK
    t = qc.shape[0]
    nb = seq // tq
    n_sel = min(MOBA_TOPK, nb - 1)
    nbp = 8
    assert seq % tq == 0 and nb <= nbp
    qrow = lambda b, i: (b * nb + i, 0)
    full = lambda b, i: (b, 0)
    return pl.pallas_call(
        functools.partial(_moba_kernel_t, tq=tq, nb=nb, n_sel=n_sel, seq=seq),
        grid=(batch, nb),
        in_specs=[pl.BlockSpec((tq, C_WIDTH), qrow), pl.BlockSpec((seq, C_WIDTH), full),
                  pl.BlockSpec((seq, C_WIDTH), full)],
        out_specs=pl.BlockSpec((tq, C_WIDTH), qrow),
        out_shape=jax.ShapeDtypeStruct((t, C_WIDTH), BF16),
        scratch_shapes=[pltpu.VMEM((nbp, C_WIDTH), F32), pltpu.VMEM((3, LANES, seq), BF16),
                        pltpu.VMEM((3, 2 * tq, LANES), BF16), pltpu.VMEM((3, nbp, 2 * tq), F32),
                        pltpu.VMEM((3, 1, 2 * tq), F32), pltpu.VMEM((3, 1, 2 * tq), F32),
                        pltpu.VMEM((3, LANES, 2 * tq), F32)],
        compiler_params=_params("parallel", "arbitrary"),
        name="moba_attention",
    )(qc, kc, vc)


def _attend_t(k, q_ref, g, v_t, m_ref, l_ref, acc_ref, mask=None):
    for j in range(q_ref.shape[1] // LANES):
        cols = slice(j * LANES, (j + 1) * LANES)
        s = _dot_nt(k, q_ref[g, cols, :])
        if mask is not None:
            s = mask(j, s)
        m_prev = m_ref[g, :, cols]
        m_new = jnp.maximum(m_prev, jnp.max(s, axis=0, keepdims=True))
        alpha = jnp.exp2(m_prev - m_new)
        p = jnp.exp2(s - m_new)
        l_ref[g, :, cols] = alpha * l_ref[g, :, cols] + jnp.sum(p, axis=0, keepdims=True)
        acc_ref[g, :, cols] = alpha * acc_ref[g, :, cols] + _dot(v_t, p.astype(BF16))
        m_ref[g, :, cols] = m_new


def _causal_mask_t(tk, tq):
    key = lax.broadcasted_iota(jnp.int32, (tk, LANES), 0)
    qry = lax.broadcasted_iota(jnp.int32, (tk, LANES), 1)

    def mask(j, s):
        return jnp.where(key <= qry + (j * LANES) % tq, s, NEG)
    return mask


def _transpose_values(v_ref, vt_ref, seq, tk):
    for g in range(vt_ref.shape[0]):
        for n in range(seq // tk):
            blk = v_ref[n * tk:(n + 1) * tk, g * LANES:(g + 1) * LANES].astype(F32)
            vt_ref[g, :, n * tk:(n + 1) * tk] = blk.T.astype(vt_ref.dtype)


def _head_pair_out(acc_t, l, tq):
    even = acc_t[0:HEAD_DIM, 0:tq] / l[:, 0:tq]
    odd = acc_t[HEAD_DIM:LANES, tq:2 * tq] / l[:, tq:2 * tq]
    return jnp.concatenate([even, odd], axis=0).T


def _dsa_kernel_t(qa_ref, qi_ref, wiq_ref, kka_ref, vva_ref, kki_ref, o_ref,
                  sc_ref, hi_ref, lo_ref, vt_ref, qs_ref, m_ref, l_ref, acc_ref, *, tq, k_top, seq):
    i = pl.program_id(1)
    nk = i + 1
    t0 = i * tq
    lo = lax.broadcasted_iota(jnp.int32, (tq, LANES), 1) < HEAD_DIM
    zero_b = jnp.zeros((tq, LANES), BF16)

    @pl.when(i == 0)
    def _():
        _transpose_values(vva_ref, vt_ref, seq, tq)

    def stack_heads(q):
        out = []
        for g in range(q.shape[1] // LANES):
            qg = q[:, g * LANES:(g + 1) * LANES]
            out += [jnp.where(lo, qg, zero_b), jnp.where(lo, zero_b, qg)]
        return out

    qa_stack = stack_heads(qa_ref[...])
    for g in range(3):
        qs_ref[g, 0:tq, :] = qa_stack[2 * g]
        qs_ref[g, tq:2 * tq, :] = qa_stack[2 * g + 1]
    qi_stack = jnp.concatenate(stack_heads(qi_ref[...]), axis=0)
    wi_t = wiq_ref[...].T

    def rows(c):
        return pl.ds(pl.multiple_of(c * tq, tq), tq)

    key_pos = lax.broadcasted_iota(jnp.int32, (tq, tq), 0)
    qry_pos = lax.broadcasted_iota(jnp.int32, (tq, tq), 1)

    def idx_body(c, carry):
        r = jnp.maximum(_dot_nt(kki_ref[rows(c), :], qi_stack), 0.0)
        s = wi_t[0:1, :] * r[:, 0:tq]
        for h in range(1, IDX_HEADS):
            s = s + wi_t[h:h + 1, :] * r[:, h * tq:(h + 1) * tq]
        causal = (c * tq + key_pos) <= (t0 + qry_pos)
        s = jnp.where(causal, s, -jnp.inf)
        s = jnp.where(jnp.abs(s) < MIN_NORMAL_F32, 0.0, s)
        sc_ref[rows(c), :] = s
        bits = lax.bitcast_convert_type(s, jnp.int32)
        key = jnp.where(bits >= 0, bits, bits ^ 0x7FFFFFFF)
        hi_ref[rows(c), :] = lax.shift_right_arithmetic(key, jnp.int32(16)).astype(jnp.int16)
        lo_ref[rows(c), :] = ((key & 0xFFFF) - 32768).astype(jnp.int16)
        return carry

    lax.fori_loop(0, nk, idx_body, 0)

    def count16(ref, cand, strict):
        c16 = cand.astype(jnp.int16)

        def body(c, acc):
            x = ref[rows(c), :]
            hit = jnp.where((x > c16) if strict else (x >= c16), jnp.int16(1), jnp.int16(0))
            for r in range(tq // 16):
                acc = acc + hit[r * 16:(r + 1) * 16]
            return acc
        acc = lax.fori_loop(0, nk, body, jnp.zeros((16, tq), jnp.int16))
        return jnp.sum(acc.astype(F32), axis=0, keepdims=True)

    def search16(ref, k_need):
        v0 = jnp.where(count16(ref, jnp.zeros((1, tq), jnp.int32), False) >= k_need, 0, -32768).astype(jnp.int32)

        def bisect(b, v):
            trial = v | lax.shift_left(jnp.int32(1), 14 - b)
            return jnp.where(count16(ref, trial, False) >= k_need, trial, v)
        return lax.fori_loop(0, 15, bisect, v0)

    kf = float(k_top)

    def search():
        hi_k = search16(hi_ref, kf)
        above = count16(hi_ref, hi_k, True)
        hi16 = hi_k.astype(jnp.int16)

        def keep_bucket(c, carry):
            lo_ref[rows(c), :] = jnp.where(hi_ref[rows(c), :] == hi16, lo_ref[rows(c), :], jnp.int16(-32768))
            return carry
        lax.fori_loop(0, nk, keep_bucket, 0)
        lo_k = search16(lo_ref, kf - above)
        n_gt = above + count16(lo_ref, lo_k, True)
        return lax.shift_left(hi_k, jnp.int32(16)) | (lo_k + 32768), kf - n_gt

    key, need = lax.cond(t0 + tq <= k_top,
                         lambda: (jnp.full((1, tq), INT_MIN, jnp.int32), jnp.full((1, tq), kf, F32)), search)
    thr = _key_to_f32(key)
    thr_next = _key_to_f32(jnp.where(key == 0, MIN_NORMAL_KEY, key + 1))
    all_sel = (t0 + lax.broadcasted_iota(jnp.int32, (1, tq), 1)) < k_top
    lower = (qry_pos <= key_pos).astype(BF16)

    _init_stats(m_ref, l_ref, acc_ref)

    def att_body(c, tie_run):
        s_idx = sc_ref[rows(c), :]
        ge = s_idx >= thr
        gt = s_idx >= thr_next
        tie = jnp.where(gt, 0.0, jnp.where(ge, 1.0, 0.0))
        prefix = _dot(lower, tie.astype(BF16)) + tie_run
        take = jnp.where(gt, 1.0, jnp.where(prefix <= need, tie, 0.0))
        take = jnp.where(all_sel, 1.0, take)
        take = jnp.where(s_idx > -jnp.inf, take, 0.0)

        def mask(j, s):
            q0 = (j * LANES) % tq
            return jnp.where(take[:, q0:q0 + LANES] > 0.5, s, NEG)

        kk = kka_ref[rows(c), :]
        vt = vt_ref[0, :, rows(c)]
        for g in range(3):
            _attend_t(kk, qs_ref, g, vt, m_ref, l_ref, acc_ref, mask)
        return tie_run + jnp.sum(tie, axis=0, keepdims=True)

    lax.fori_loop(0, nk, att_body, jnp.zeros((1, tq), F32))

    for g in range(3):
        o_ref[:, g * LANES:(g + 1) * LANES] = _head_pair_out(acc_ref[g], l_ref[g], tq).astype(o_ref.dtype)


def _diff_kernel_t(qb_ref, kb_ref, vb_ref, lam_ref, gsub_ref, o_ref,
                   vt_ref, qs_ref, m_ref, l_ref, acc_ref, *, tq, lam_init, seq):
    i = pl.program_id(1)
    lane = lax.broadcasted_iota(jnp.int32, (tq, LANES), 1)
    zero_b = jnp.zeros((tq, LANES), BF16)

    @pl.when(i == 0)
    def _():
        _transpose_values(vb_ref, vt_ref, seq, tq)

    qb = qb_ref[...]
    for g in range(2):
        qg = qb[:, g * LANES:(g + 1) * LANES]
        for j in range(4):
            qs_ref[g, j * tq:(j + 1) * tq, :] = jnp.where(lane // B_QK_DIM == j, qg, zero_b)
    _init_stats(m_ref, l_ref, acc_ref)

    def rows(c):
        return pl.ds(pl.multiple_of(c * tq, tq), tq)

    def step(c, masked):
        mask = _causal_mask_t(tq, tq) if masked else None
        for g in range(2):
            _attend_t(kb_ref[rows(c), g * LANES:(g + 1) * LANES], qs_ref, g, vt_ref[g, :, rows(c)],
                      m_ref, l_ref, acc_ref, mask)

    def body(c, carry):
        step(c, False)
        return carry

    lax.fori_loop(0, i, body, 0)
    step(i, True)

    lv = lam_ref[...]
    lam = (jnp.exp(jnp.sum(lv[0:1] * lv[1:2], axis=1, keepdims=True))
           - jnp.exp(jnp.sum(lv[2:3] * lv[3:4], axis=1, keepdims=True)) + lam_init)
    gsub = gsub_ref[...]
    for g in range(2):
        acc_t, l = acc_ref[g], l_ref[g]

        def prob(r0, j):
            return acc_t[r0:r0 + HEAD_DIM, j * tq:(j + 1) * tq] / l[:, j * tq:(j + 1) * tq]

        halves = []
        for r0, j in ((0, 0), (HEAD_DIM, 2)):
            o = prob(r0, j) - lam * prob(r0, j + 1)
            ms = jnp.mean(o * o, axis=0, keepdims=True)
            halves.append(o * lax.rsqrt(ms + SUBLN_EPS))
        y = (jnp.concatenate(halves, axis=0).T * gsub) * (1.0 - lam_init)
        o_ref[:, g * LANES:(g + 1) * LANES] = y.astype(o_ref.dtype)


def _moba_kernel_t(qc_ref, kc_ref, vc_ref, o_ref, kmean_ref, vt_ref, qs_ref, bias_ref, m_ref, l_ref, acc_ref,
                   *, tq, nb, n_sel, seq):
    i = pl.program_id(1)
    nbp = kmean_ref.shape[0]
    lane = lax.broadcasted_iota(jnp.int32, (tq, LANES), 1)
    lo = lane < HEAD_DIM
    zero_b = jnp.zeros((tq, LANES), BF16)

    @pl.when(i == 0)
    def _():
        _transpose_values(vc_ref, vt_ref, seq, tq)
        kmean_ref[...] = jnp.zeros(kmean_ref.shape, F32)
        for n in range(nb):
            kblk = kc_ref[n * tq:(n + 1) * tq, :].astype(F32)
            kmean_ref[n:n + 1, :] = jnp.mean(kblk, axis=0, keepdims=True)

    qc = qc_ref[...]
    sub = lax.broadcasted_iota(jnp.int32, (nbp, 2 * tq), 0)
    past = sub < i
    for g in range(3):
        qg = qc[:, g * LANES:(g + 1) * LANES]
        q2 = jnp.concatenate([jnp.where(lo, qg, zero_b), jnp.where(lo, zero_b, qg)], axis=0)
        km = kmean_ref[:, g * LANES:(g + 1) * LANES]
        km_hi = km.astype(BF16)
        gt = _dot_nt(km_hi, q2) + _dot_nt((km - km_hi.astype(F32)).astype(BF16), q2)
        rows_ = []
        for n in range(nbp):
            gn = gt[n:n + 1, :]
            beats = jnp.where(sub < n, jnp.where(gt >= gn, 1.0, 0.0), jnp.where(gt > gn, 1.0, 0.0))
            beats = jnp.where(sub == n, 0.0, jnp.where(past, beats, 0.0))
            rank = jnp.sum(beats, axis=0, keepdims=True)
            rows_.append(jnp.where(rank < n_sel, 0.0, NEG))
        bias_ref[g] = jnp.where(past, jnp.concatenate(rows_, axis=0), NEG)
        qs_ref[g] = q2
    _init_stats(m_ref, l_ref, acc_ref)

    def rows(c):
        return pl.ds(pl.multiple_of(c * tq, tq), tq)

    def body(c, carry):
        for g in range(3):
            bias = bias_ref[g, pl.ds(c, 1), :]

            def mask(j, s):
                return s + bias[:, j * LANES:(j + 1) * LANES]

            _attend_t(kc_ref[rows(c), g * LANES:(g + 1) * LANES], qs_ref, g, vt_ref[g, :, rows(c)],
                      m_ref, l_ref, acc_ref, mask)
        return carry

    lax.fori_loop(0, i, body, 0)

    causal = _causal_mask_t(tq, tq)
    for g in range(3):
        _attend_t(kc_ref[rows(i), g * LANES:(g + 1) * LANES], qs_ref, g, vt_ref[g, :, rows(i)],
                  m_ref, l_ref, acc_ref, causal)
        o_ref[:, g * LANES:(g + 1) * LANES] = _head_pair_out(acc_ref[g], l_ref[g], tq).astype(o_ref.dtype)


def _route(h, wr_ref, cw_ref, meta_ref, cnt_ref):
    hi = h.astype(BF16)
    lo = (h - hi.astype(F32)).astype(BF16)
    logits = _dot(hi, wr_ref[0]) + (_dot(lo, wr_ref[0]) + _dot(hi, wr_ref[1]))
    lane = lax.broadcasted_iota(jnp.int32, logits.shape, 1)
    lg = jnp.where(lane < N_EXPERTS, logits, -jnp.inf)
    v0 = jnp.max(lg, axis=1, keepdims=True)
    i0 = jnp.min(jnp.where(lg == v0, lane, LANES), axis=1, keepdims=True)
    lg1 = jnp.where(lane == i0, -jnp.inf, lg)
    v1 = jnp.max(lg1, axis=1, keepdims=True)
    i1 = jnp.min(jnp.where(lg1 == v1, lane, LANES), axis=1, keepdims=True)
    e1 = jnp.exp(v1 - v0)
    w0 = 1.0 / (1.0 + e1)
    tm = logits.shape[0]
    routed = jnp.where(lane == i0, 1.0, 0.0) + jnp.where(lane == i1, 1.0, 0.0)
    earlier_rows = (lax.broadcasted_iota(jnp.int32, (tm, tm), 1)
                    < lax.broadcasted_iota(jnp.int32, (tm, tm), 0)).astype(BF16)

    @pl.when(pl.program_id(0) == 0)
    def _():
        cnt_ref[...] = jnp.zeros(cnt_ref.shape, F32)

    before = _dot(earlier_rows, routed.astype(BF16)) + cnt_ref[0:1, :]
    r0 = jnp.sum(jnp.where(lane == i0, before, 0.0), axis=1, keepdims=True)
    r1 = jnp.sum(jnp.where(lane == i1, before, 0.0), axis=1, keepdims=True)
    cnt_ref[0:1, :] = cnt_ref[0:1, :] + jnp.sum(routed, axis=0, keepdims=True)
    cw = (jnp.where(lane == 0, i0.astype(F32), 0.0) + jnp.where(lane == 1, i1.astype(F32), 0.0)
          + jnp.where(lane == 2, w0, 0.0) + jnp.where(lane == 3, e1 * w0, 0.0)
          + jnp.where(lane == 4, r0, 0.0) + jnp.where(lane == 5, r1, 0.0))
    cw_ref[...] = cw
    meta_ref[...] = cw.T[0:8, :].astype(jnp.int32)


def _mixer_out(x_ref, ya_ref, yb_ref, yc_ref, wo_ref, g_ref, mod_ref):
    ab = A_WIDTH + B_WIDTH
    y = (_dot(ya_ref[...], wo_ref[0:A_WIDTH, :]) + _dot(yb_ref[...], wo_ref[A_WIDTH:ab, :])
         + _dot(yc_ref[...], wo_ref[ab:ab + C_WIDTH, :]))
    x1 = x_ref[...] + mod_ref[2:3, :] * y
    return x1, _norm_mod(x1, g_ref[...], mod_ref[3:4, :], mod_ref[4:5, :])


def _mixer_out_specs(tm, d, layer, tiles_per_batch):
    row = lambda i: (i, 0)
    return [pl.BlockSpec((tm, d), row), pl.BlockSpec((tm, A_WIDTH), row), pl.BlockSpec((tm, B_WIDTH), row),
            pl.BlockSpec((tm, C_WIDTH), row), pl.BlockSpec((None, d, d), lambda i: (layer, 0, 0)),
            pl.BlockSpec((1, d), lambda i: (0, 0)),
            pl.BlockSpec((None, 6, d), lambda i: (i // tiles_per_batch, 0, 0))]


def _outproj_kernel(x_ref, ya_ref, yb_ref, yc_ref, wo_ref, g_ref, mod_ref, wr_ref,
                    x1_ref, h_ref, cw_ref, meta_ref, cnt_ref):
    x1, h = _mixer_out(x_ref, ya_ref, yb_ref, yc_ref, wo_ref, g_ref, mod_ref)
    x1_ref[...] = x1
    h_ref[...] = _pack_rows(h)
    _route(h, wr_ref, cw_ref, meta_ref, cnt_ref)


def outproj_router(x, ya, yb, yc, wo, layer, g_ffn, mod, seq, w_router, tm=512):
    t, d = x.shape
    row = lambda i: (i, 0)
    return pl.pallas_call(
        _outproj_kernel,
        grid=(t // tm,),
        in_specs=_mixer_out_specs(tm, d, layer, seq // tm) + [pl.BlockSpec((2, d, LANES), lambda i: (0, 0, 0))],
        out_specs=[pl.BlockSpec((tm, d), row), pl.BlockSpec((tm, d // 2), row), pl.BlockSpec((tm, LANES), row),
                   pl.BlockSpec((8, tm), lambda i: (0, i)), pl.BlockSpec((8, LANES), lambda i: (0, 0))],
        out_shape=[jax.ShapeDtypeStruct((t, d), F32), jax.ShapeDtypeStruct((t, d // 2), jnp.int32),
                   jax.ShapeDtypeStruct((t, LANES), F32), jax.ShapeDtypeStruct((8, t), jnp.int32),
                   jax.ShapeDtypeStruct((8, LANES), F32)],
        compiler_params=_params("arbitrary"),
        name="outproj_router",
    )(x, ya, yb, yc, wo, g_ffn, mod, w_router)


W_ROWS_IN = 128
W_ROWS_DOWN = 352
FF_SPLIT = 2


def _load_weights_bf16(layer, e, w_hbms, w_bs, st_in, st_dn, sems):
    slots = st_in.shape[0]
    plan = []
    for w_hbm, w_b, rb, st, sem0 in ((w_hbms[0], w_bs[0], W_ROWS_IN, st_in, 0), (w_hbms[1], w_bs[1], W_ROWS_IN, st_in, 0),
                                     (w_hbms[2], w_bs[2], W_ROWS_DOWN, st_dn, slots)):
        assert w_b.shape[0] % rb == 0
        plan += [(w_hbm, w_b, r0, rb, st, sem0) for r0 in range(0, w_b.shape[0], rb)]
    uses = {0: 0, slots: 0}
    copies = []
    for w_hbm, w_b, r0, rb, st, sem0 in plan:
        slot = uses[sem0] % slots
        uses[sem0] += 1
        copies.append((pltpu.make_async_copy(w_hbm.at[layer, e, pl.ds(r0, rb), :], st.at[slot],
                                             sems.at[sem0 + slot]), st, slot, w_b, r0, rb))
    ahead = slots - 1
    for b, copy in enumerate(copies[:ahead]):
        copy[0].start(priority=b % 2)
    for b, (copy, st, slot, w_b, r0, rb) in enumerate(copies):
        if b + ahead < len(copies):
            copies[b + ahead][0].start(priority=(b + ahead) % 2)
        copy.wait()
        w_b[r0:r0 + rb, :] = st[slot].astype(BF16)


def _swiglu_resident(x, wg_b, wu_b, wd_b):
    tf = wg_b.shape[1] // FF_SPLIT
    acc = None
    for f in range(FF_SPLIT):
        cols = slice(f * tf, (f + 1) * tf)
        a = _dot(x, wg_b[:, cols])
        u = _dot(x, wu_b[:, cols])
        act = (a * (1.0 / (1.0 + jnp.exp(-a)))) * u
        part = _dot(act.astype(BF16), wd_b[cols, :])
        acc = part if acc is None else acc + part
    return acc


def _weight_scratch(d, dff, slots):
    return [pltpu.VMEM((d, dff), BF16), pltpu.VMEM((d, dff), BF16), pltpu.VMEM((dff, d), BF16),
            pltpu.VMEM((slots, W_ROWS_IN, dff), F32), pltpu.VMEM((slots, W_ROWS_DOWN, d), F32),
            pltpu.SemaphoreType.DMA((2 * slots,))]


def _layer_out(x, gfin_ref, final):
    if not final:
        return x
    return (x * lax.rsqrt(jnp.mean(x * x, axis=-1, keepdims=True) + EPS)) * gfin_ref[...]


def _ffn_kernel(x_ref, ya_ref, yb_ref, yc_ref, wo_ref, g_ref, mod_ref, gfin_ref, wg_hbm, wu_hbm, wd_hbm, o_ref,
                wg_b, wu_b, wd_b, st_in, st_dn, sems, *, layer, final):
    @pl.when(pl.program_id(0) == 0)
    def _():
        _load_weights_bf16(layer, 0, (wg_hbm, wu_hbm, wd_hbm), (wg_b, wu_b, wd_b), st_in, st_dn, sems)

    x1, h = _mixer_out(x_ref, ya_ref, yb_ref, yc_ref, wo_ref, g_ref, mod_ref)
    out = x1 + mod_ref[5:6, :] * _swiglu_resident(h.astype(BF16), wg_b, wu_b, wd_b)
    o_ref[...] = _layer_out(out, gfin_ref, final)


def outproj_ffn_dense(x, ya, yb, yc, wo, layer, g_ffn, mod, g_final, final, w_gate, w_up, w_down, ff_layer, seq,
                      tm=512):
    t, d = x.shape
    hbm = pl.BlockSpec(memory_space=pl.ANY)
    return pl.pallas_call(
        functools.partial(_ffn_kernel, layer=ff_layer, final=final),
        grid=(t // tm,),
        in_specs=_mixer_out_specs(tm, d, layer, seq // tm) + [pl.BlockSpec((1, d), lambda i: (0, 0)), hbm, hbm, hbm],
        out_specs=pl.BlockSpec((tm, d), lambda i: (i, 0)),
        out_shape=jax.ShapeDtypeStruct((t, d), F32),
        scratch_shapes=_weight_scratch(d, w_gate.shape[3], 2),
        compiler_params=_params("arbitrary"),
        name="ffn_dense",
    )(x, ya, yb, yc, wo, g_ffn, mod, g_final, w_gate, w_up, w_down)


MOE_TILE = 512
MOE_PARTS = 3
SC_CORES, SC_SUBCORES = 2, 16
SC_ROWS = 64
HI16 = -65536


def _pack_rows(x):
    c = x.shape[1] // 2
    bits = lax.bitcast_convert_type(x.astype(jnp.bfloat16).astype(F32), jnp.int32)
    return lax.shift_right_logical(bits[:, :c], jnp.int32(16)) | (bits[:, c:] & jnp.int32(HI16))


def _unpack_rows(w):
    lo = lax.bitcast_convert_type(lax.shift_left(w, jnp.int32(16)), F32)
    hi = lax.bitcast_convert_type(w & jnp.int32(HI16), F32)
    return jnp.concatenate([lo, hi], axis=1)


def sc_gather_rows(table, idx):
    d = table.shape[1]
    b = idx.shape[0]
    per_worker = b // (SC_CORES * SC_SUBCORES)
    assert per_worker * SC_CORES * SC_SUBCORES == b and per_worker % (2 * SC_ROWS) == 0
    mesh = plsc.VectorSubcoreMesh(core_axis_name="c", subcore_axis_name="s")
    idx_buf = pltpu.VMEM((SC_ROWS,), jnp.int32)
    row_buf = pltpu.VMEM((SC_ROWS, d), table.dtype)

    @functools.partial(
        pl.kernel, mesh=mesh, out_type=jax.ShapeDtypeStruct((b, d), table.dtype),
        scratch_types=[idx_buf, idx_buf, row_buf, row_buf] + [pltpu.SemaphoreType.DMA] * 4,
        name="sc_gather_rows")
    def gather(table_hbm, idx_hbm, out_hbm, idx0, idx1, rows0, rows1, sem_g0, sem_g1, sem_w0, sem_w1):
        base = (lax.axis_index("s") * SC_CORES + lax.axis_index("c")) * per_worker

        @pl.loop(0, per_worker // (2 * SC_ROWS))
        def _(pair):
            off0 = pl.multiple_of(base + pair * (2 * SC_ROWS), SC_ROWS)
            off1 = pl.multiple_of(off0 + SC_ROWS, SC_ROWS)
            pltpu.sync_copy(idx_hbm.at[pl.ds(off0, SC_ROWS)], idx0)
            pltpu.sync_copy(idx_hbm.at[pl.ds(off1, SC_ROWS)], idx1)
            gather0 = pltpu.async_copy(table_hbm.at[idx0], rows0, sem_g0)
            gather1 = pltpu.async_copy(table_hbm.at[idx1], rows1, sem_g1)
            gather0.wait()
            write0 = pltpu.async_copy(rows0, out_hbm.at[pl.ds(off0, SC_ROWS)], sem_w0)
            gather1.wait()
            write1 = pltpu.async_copy(rows1, out_hbm.at[pl.ds(off1, SC_ROWS)], sem_w1)
            write0.wait()
            write1.wait()

    return gather(table, idx)


def _moe_layout(meta, counts, n_exp):
    t = meta.shape[1]
    e0, e1, rank0, rank1 = meta[0], meta[1], meta[4], meta[5]
    padded = (counts + MOE_TILE - 1) // MOE_TILE * MOE_TILE
    ends = jnp.cumsum(padded)
    offs = ends - padded
    pos0 = offs[e0] + rank0
    pos1 = offs[e1] + rank1
    n_rows = 2 * t + n_exp * MOE_TILE
    tok = jnp.arange(t, dtype=jnp.int32)
    src = jnp.zeros((n_rows,), jnp.int32).at[jnp.concatenate([pos0, pos1])].set(
        jnp.concatenate([tok, tok]), unique_indices=True, mode="promise_in_bounds")
    n_tiles = n_rows // MOE_TILE
    n_valid = ends[-1] // MOE_TILE
    tile_expert = jnp.sum(jnp.arange(n_tiles)[:, None] * MOE_TILE >= ends[None, :], axis=1)
    tile_expert = tile_expert[jnp.minimum(jnp.arange(n_tiles), n_valid - 1)].astype(jnp.int32)
    return pos0, pos1, src, tile_expert, n_valid.astype(jnp.int32).reshape(1)


W_SLOTS = 4


def _experts_packed_kernel(te_ref, nv_ref, first_ref, x_ref, wg_hbm, wu_hbm, wd_hbm, *rest, layer):
    o_ref, wg_b, wu_b, wd_b, st_in, st_dn, sems = rest[-7:]
    j = pl.program_id(0)
    valid = j < nv_ref[0]

    @pl.when(valid & (first_ref[j] == 1))
    def _():
        _load_weights_bf16(layer, te_ref[j], (wg_hbm, wu_hbm, wd_hbm), (wg_b, wu_b, wd_b), st_in, st_dn, sems)

    @pl.when(valid)
    def _():
        o_ref[...] = _pack_rows(_swiglu_resident(_unpack_rows(x_ref[...]).astype(BF16), wg_b, wu_b, wd_b))


def moe_experts_packed(x_part, tile_expert, n_valid, w_gate, w_up, w_down, layer, y_prev, part, n_rows):
    rows_part, half = x_part.shape
    d = 2 * half
    dff = w_gate.shape[3]
    tiles_part = rows_part // MOE_TILE
    first = jnp.concatenate([jnp.ones((1,), jnp.int32),
                             (tile_expert[1:] != tile_expert[:-1]).astype(jnp.int32)])

    def tile(j, nv):
        return jnp.maximum(jnp.minimum(j, nv[0] - 1), 0)

    hbm = pl.BlockSpec(memory_space=pl.ANY)
    in_specs = [pl.BlockSpec((MOE_TILE, half), lambda j, te, nv, fi: (tile(j, nv), 0)), hbm, hbm, hbm]
    args = [tile_expert, n_valid, first, x_part, w_gate, w_up, w_down]
    aliases = {}
    if y_prev is not None:
        in_specs.append(hbm)
        args.append(y_prev)
        aliases = {len(args) - 1: 0}
    return pl.pallas_call(
        functools.partial(_experts_packed_kernel, layer=layer),
        grid_spec=pltpu.PrefetchScalarGridSpec(
            num_scalar_prefetch=3, grid=(tiles_part,), in_specs=in_specs,
            out_specs=pl.BlockSpec((MOE_TILE, half), lambda j, te, nv, fi: (part * tiles_part + tile(j, nv), 0)),
            scratch_shapes=_weight_scratch(d, dff, W_SLOTS)),
        out_shape=jax.ShapeDtypeStruct((n_rows, half), jnp.int32),
        input_output_aliases=aliases,
        compiler_params=_params("arbitrary"),
        name="moe_experts",
    )(*args)


def _combine_packed_kernel(y0_ref, y1_ref, route_ref, x1_ref, mod_ref, gfin_ref, o_ref, *, final):
    rt = route_ref[...]
    f = rt[:, 2:3] * _unpack_rows(y0_ref[...]) + rt[:, 3:4] * _unpack_rows(y1_ref[...])
    o_ref[...] = _layer_out(x1_ref[...] + mod_ref[5:6, :] * f, gfin_ref, final)


def moe_combine_packed(y_pairs, route, x1, mod, g_final, final, seq, tm=512):
    t, d = x1.shape
    nt = t // tm
    tiles_per_batch = seq // tm
    row = lambda i: (i, 0)
    return pl.pallas_call(
        functools.partial(_combine_packed_kernel, final=final),
        grid=(nt,),
        in_specs=[pl.BlockSpec((tm, d // 2), row), pl.BlockSpec((tm, d // 2), lambda i: (i + nt, 0)),
                  pl.BlockSpec((tm, LANES), row), pl.BlockSpec((tm, d), row),
                  pl.BlockSpec((None, 6, d), lambda i: (i // tiles_per_batch, 0, 0)),
                  pl.BlockSpec((1, d), lambda i: (0, 0))],
        out_specs=pl.BlockSpec((tm, d), row),
        out_shape=jax.ShapeDtypeStruct((t, d), F32),
        compiler_params=_params("parallel"),
        name="moe_combine",
    )(y_pairs, y_pairs, route, x1, mod, g_final)


def ffn_moe_sc(x1, h_packed, route, meta, counts, mod, g_final, final, w_gate, w_up, w_down, layer, seq):
    n_exp = w_gate.shape[1]
    pos0, pos1, src, tile_expert, n_valid = _moe_layout(meta, counts[0, :n_exp].astype(jnp.int32), n_exp)
    n_rows = src.shape[0]
    rows_part = n_rows // MOE_PARTS
    tiles_part = rows_part // MOE_TILE
    assert rows_part * MOE_PARTS == n_rows and tiles_part * MOE_TILE == rows_part
    h_parts = [sc_gather_rows(h_packed, src[p * rows_part:(p + 1) * rows_part]) for p in range(MOE_PARTS)]
    y_sorted = None
    for p in range(MOE_PARTS):
        n_valid_part = jnp.clip(n_valid - p * tiles_part, 0, tiles_part)
        y_sorted = moe_experts_packed(h_parts[p], tile_expert[p * tiles_part:(p + 1) * tiles_part], n_valid_part,
                                      w_gate, w_up, w_down, layer, y_sorted, p, n_rows)
    y_pairs = sc_gather_rows(y_sorted, jnp.concatenate([pos0, pos1]))
    return moe_combine_packed(y_pairs, route, x1, mod, g_final, final, seq)


def _rope_tables(positions, dim):
    rot = dim // ROPE_FRACTION
    half = rot // 2
    inv = 1.0 / (ROPE_THETA ** (np.arange(0, rot, 2, dtype=np.float32) / rot))
    ang = positions.reshape(-1).astype(F32)[:, None] * jnp.asarray(inv, F32)
    cos, sin = jnp.cos(ang), jnp.sin(ang)
    t = ang.shape[0]
    ones = jnp.ones((t, dim - rot), F32)
    zeros = lambda w: jnp.zeros((t, w), F32)
    reps = LANES // dim
    c = jnp.tile(jnp.concatenate([cos, cos, ones], axis=1), (1, reps))
    sa = jnp.tile(jnp.concatenate([-sin, zeros(dim - half)], axis=1), (1, reps))
    sb = jnp.tile(jnp.concatenate([zeros(half), sin, zeros(dim - rot)], axis=1), (1, reps))
    return c, sa, sb


def _relayout_w_in(w):
    pts = np.cumsum([0, 384, 64, 64, 256, 64, 4, 256, 256, 256, 384, 384, 384])
    (q_a, k_a, v_a, q_i, k_i, w_i, q_b, k_b, v_b, q_c, k_c, v_c) = [w[..., pts[j]:pts[j + 1]] for j in range(12)]
    w_i_pad = jnp.concatenate([w_i, jnp.zeros(w.shape[:-1] + (LANES - IDX_HEADS,), w.dtype)], axis=-1)
    return jnp.concatenate([q_a, k_a, k_a, v_a, v_a, q_i, k_i, k_i, w_i_pad,
                            q_b, k_b, v_b, q_c, k_c, v_c], axis=-1).astype(BF16)


def kernel(x, c, positions, w_in, w_out, diff_lambda, diff_subln, w_ada, b_ada, g_attn, g_ffn, w_ff_gate,
           w_ff_up, w_ff_down, w_router, w_exp_gate, w_exp_up, w_exp_down, g_final):
    batch, seq, d = x.shape
    depth = w_in.shape[0]
    t = batch * seq
    tabs = _rope_tables(positions, HEAD_DIM) + _rope_tables(positions, B_QK_DIM)
    mod_all = adaln_mod(c, w_ada, b_ada).reshape(depth, batch, 6, d)
    w_in_pad = _relayout_w_in(w_in)
    wo = w_out.astype(BF16)
    g_fin = g_final.reshape(1, d)
    xf = x.reshape(t, d)
    for layer in range(depth):
        mod = mod_all[layer]
        lam_init = 0.8 - 0.6 * math.exp(-0.3 * layer)
        (qa, kka, vva, qi, kki, wi, qb, kb, vb, qc, kc, vc) = inproj(
            xf, g_attn[layer].reshape(1, d), mod, tabs, w_in_pad, layer, seq)
        ya = dsa_attention(qa, qi, wi, kka, vva, kki, batch, seq)
        g_sub2 = jnp.tile(diff_subln[layer], 2).reshape(1, LANES)
        yb = diff_attention(qb, kb, vb, diff_lambda[layer], g_sub2, lam_init, batch, seq)
        yc = moba_attention(qc, kc, vc, batch, seq)
        j = layer // 2
        gf = g_ffn[layer].reshape(1, d)
        final = layer == depth - 1
        if layer % 2 == 0:
            xf = outproj_ffn_dense(xf, ya, yb, yc, wo, layer, gf, mod, g_fin, final,
                                   w_ff_gate[:, None], w_ff_up[:, None], w_ff_down[:, None], j, seq)
        else:
            wr = jnp.concatenate([w_router[j], jnp.zeros((d, LANES - N_EXPERTS), F32)], axis=1)
            wr_hi = wr.astype(BF16)
            wr = jnp.stack([wr_hi, (wr - wr_hi.astype(F32)).astype(BF16)])
            x1, h_packed, route, meta, counts = outproj_router(xf, ya, yb, yc, wo, layer, gf, mod, seq, wr)
            xf = ffn_moe_sc(x1, h_packed, route, meta, counts, mod, g_fin, final,
                            w_exp_gate, w_exp_up, w_exp_down, j, seq)
    return xf.reshape(batch, seq, d)
```

```python
import functools
import math

import jax
import jax.numpy as jnp
import numpy as np
from jax import lax
from jax.experimental import pallas as pl
from jax.experimental.pallas import tpu as pltpu
from jax.experimental.pallas import tpu_sc as plsc

F32 = jnp.float32
BF16 = jnp.bfloat16

HEAD_DIM = 64
A_HEADS = 6
IDX_HEADS = 4
B_HEADS = 4
B_QK_DIM = 32
C_HEADS = 6
A_WIDTH, IDX_WIDTH = A_HEADS * HEAD_DIM, IDX_HEADS * HEAD_DIM
B_WIDTH, C_WIDTH = B_HEADS * HEAD_DIM, C_HEADS * HEAD_DIM
DSA_TOPK_MAX = 256
MOBA_BLOCK = 256
MOBA_TOPK = 3
ROPE_THETA = 500000.0
ROPE_FRACTION = 4
SUBLN_EPS = 1e-5
EPS = 1e-6
N_EXPERTS = 8

LANES = 128
NEG = -1e30
INT_MIN = -2 ** 31
MIN_NORMAL_KEY = 0x00800000
MIN_NORMAL_F32 = float(np.float32(2.0 ** -126))
VMEM_LIMIT = 48 * 1024 * 1024
LOG2E = math.log2(math.e)

_G_QA, _G_KKA, _G_VVA, _G_QI, _G_KKI, _G_WI = (0, 384), (384, 512), (512, 640), (640, 896), (896, 1024), (1024, 1152)
_G_QB, _G_KB, _G_VB = (1152, 1408), (1408, 1664), (1664, 1920)
_G_QC, _G_KC, _G_VC = (1920, 2304), (2304, 2688), (2688, 3072)
D_IN_PAD = 3072


def _params(*sem):
    return pltpu.CompilerParams(dimension_semantics=sem, vmem_limit_bytes=VMEM_LIMIT)


def _dot(a, b):
    return jnp.dot(a, b, preferred_element_type=F32)


def _dot_nt(a, b):
    return lax.dot_general(a, b, (((1,), (1,)), ((), ())), preferred_element_type=F32)


def _adaln_kernel(c_ref, w_ref, b_ref, o_ref):
    c = c_ref[...]
    c_act = c * (1.0 / (1.0 + jnp.exp(-c)))
    o_ref[...] = jnp.dot(c_act, w_ref[...], preferred_element_type=F32,
                         precision=lax.Precision.HIGHEST) + b_ref[...]


def adaln_mod(c, w_ada, b_ada, tn=1536):
    depth, d, n = w_ada.shape
    b = c.shape[0]
    return pl.pallas_call(
        _adaln_kernel,
        grid=(depth, n // tn),
        in_specs=[pl.BlockSpec((b, d), lambda l, j: (0, 0)),
                  pl.BlockSpec((None, d, tn), lambda l, j: (l, 0, j)),
                  pl.BlockSpec((None, 1, tn), lambda l, j: (l, 0, j))],
        out_specs=pl.BlockSpec((None, b, tn), lambda l, j: (l, 0, j)),
        out_shape=jax.ShapeDtypeStruct((depth, b, n), F32),
        compiler_params=_params("parallel", "parallel"),
        name="adaln_mod",
    )(c, w_ada, b_ada.reshape(depth, 1, n))


def _norm_mod(x, g, shift, scale, eps=EPS):
    y = x * lax.rsqrt(jnp.mean(x * x, axis=-1, keepdims=True) + eps)
    return (y * g) * (1.0 + scale) + shift


def _rope_store(acc, o_ref, cos, sa, sb, half):
    for j in range(acc.shape[1] // LANES):
        a = acc[:, j * LANES:(j + 1) * LANES]
        r = a * cos + pltpu.roll(a, half, 1) * sb + pltpu.roll(a, LANES - half, 1) * sa
        o_ref[:, j * LANES:(j + 1) * LANES] = r.astype(o_ref.dtype)


def _inproj_kernel(x_ref, g_ref, mod_ref, c64_ref, sa64_ref, sb64_ref, c32_ref, sa32_ref, sb32_ref, w_ref,
                   qa_ref, kka_ref, vva_ref, qi_ref, kki_ref, wi_ref,
                   qb_ref, kb_ref, vb_ref, qc_ref, kc_ref, vc_ref):
    h = _norm_mod(x_ref[...], g_ref[...], mod_ref[0:1, :], mod_ref[1:2, :]).astype(BF16)
    c64, sa64, sb64 = c64_ref[...], sa64_ref[...], sb64_ref[...]
    c32, sa32, sb32 = c32_ref[...], sa32_ref[...], sb32_ref[...]

    def proj(cols):
        return _dot(h, w_ref[:, cols[0]:cols[1]])

    qk_scale = HEAD_DIM ** -0.5 * LOG2E
    _rope_store(proj(_G_QA), qa_ref, c64 * qk_scale, sa64 * qk_scale, sb64 * qk_scale, 8)
    _rope_store(proj(_G_KKA), kka_ref, c64, sa64, sb64, 8)
    vva_ref[...] = proj(_G_VVA).astype(vva_ref.dtype)
    _rope_store(proj(_G_QI), qi_ref, c64, sa64, sb64, 8)
    _rope_store(proj(_G_KKI), kki_ref, c64, sa64, sb64, 8)
    wi_ref[...] = proj(_G_WI) * (IDX_HEADS ** -0.5 * HEAD_DIM ** -0.5)
    b_scale = B_QK_DIM ** -0.5 * LOG2E
    _rope_store(proj(_G_QB), qb_ref, c32 * b_scale, sa32 * b_scale, sb32 * b_scale, 4)
    _rope_store(proj(_G_KB), kb_ref, c32, sa32, sb32, 4)
    vb_ref[...] = proj(_G_VB).astype(vb_ref.dtype)
    _rope_store(proj(_G_QC), qc_ref, c64 * qk_scale, sa64 * qk_scale, sb64 * qk_scale, 8)
    _rope_store(proj(_G_KC), kc_ref, c64, sa64, sb64, 8)
    vc_ref[...] = proj(_G_VC).astype(vc_ref.dtype)


def inproj(x, g, mod, tabs, w_pad, layer, seq, tm=512):
    t, d = x.shape
    tiles_per_batch = seq // tm
    row = lambda i: (i, 0)
    widths = [A_WIDTH, LANES, LANES, IDX_WIDTH, LANES, LANES, B_WIDTH, B_WIDTH, B_WIDTH, C_WIDTH, C_WIDTH, C_WIDTH]
    dtypes = [BF16, BF16, BF16, BF16, BF16, F32, BF16, BF16, BF16, BF16, BF16, BF16]
    return pl.pallas_call(
        _inproj_kernel,
        grid=(t // tm,),
        in_specs=[pl.BlockSpec((tm, d), row),
                  pl.BlockSpec((1, d), lambda i: (0, 0)),
                  pl.BlockSpec((None, 6, d), lambda i: (i // tiles_per_batch, 0, 0))]
                 + [pl.BlockSpec((tm, LANES), row)] * 6
                 + [pl.BlockSpec((None, d, D_IN_PAD), lambda i: (layer, 0, 0))],
        out_specs=[pl.BlockSpec((tm, w), row) for w in widths],
        out_shape=[jax.ShapeDtypeStruct((t, w), dt) for w, dt in zip(widths, dtypes)],
        compiler_params=_params("parallel"),
        name="inproj",
    )(x, g, mod, *tabs, w_pad)


def _init_stats(m_ref, l_ref, acc_ref):
    m_ref[...] = jnp.full(m_ref.shape, -jnp.inf, F32)
    l_ref[...] = jnp.zeros(l_ref.shape, F32)
    acc_ref[...] = jnp.zeros(acc_ref.shape, F32)


def _key_to_f32(k):
    return lax.bitcast_convert_type(jnp.where(k >= 0, k, k ^ 0x7FFFFFFF), F32)


def dsa_attention(qa, qi, wi, kka, vva, kki, batch, seq, tq=256):
    t = qa.shape[0]
    nq = seq // tq
    k_top = min(DSA_TOPK_MAX, seq // 4)
    qrow = lambda b, i: (b * nq + i, 0)
    full = lambda b, i: (b, 0)
    return pl.pallas_call(
        functools.partial(_dsa_kernel_t, tq=tq, k_top=k_top, seq=seq),
        grid=(batch, nq),
        in_specs=[pl.BlockSpec((tq, A_WIDTH), qrow), pl.BlockSpec((tq, IDX_WIDTH), qrow),
                  pl.BlockSpec((tq, LANES), qrow),
                  pl.BlockSpec((seq, LANES), full), pl.BlockSpec((seq, LANES), full),
                  pl.BlockSpec((seq, LANES), full)],
        out_specs=pl.BlockSpec((tq, A_WIDTH), qrow),
        out_shape=jax.ShapeDtypeStruct((t, A_WIDTH), BF16),
        scratch_shapes=[pltpu.VMEM((seq, tq), F32), pltpu.VMEM((seq, tq), jnp.int16),
                        pltpu.VMEM((seq, tq), jnp.int16), pltpu.VMEM((1, LANES, seq), BF16),
                        pltpu.VMEM((3, 2 * tq, LANES), BF16),
                        pltpu.VMEM((3, 1, 2 * tq), F32), pltpu.VMEM((3, 1, 2 * tq), F32),
                        pltpu.VMEM((3, LANES, 2 * tq), F32)],
        compiler_params=_params("parallel", "arbitrary"),
        name="dsa_attention",
    )(qa, qi, wi, kka, vva, kki)


def diff_attention(qb, kb, vb, lam_vec, g_sub2, lam_init, batch, seq, tq=256):
    t = qb.shape[0]
    nq = seq // tq
    qrow = lambda b, i: (b * nq + i, 0)
    full = lambda b, i: (b, 0)
    const = lambda b, i: (0, 0)
    return pl.pallas_call(
        functools.partial(_diff_kernel_t, tq=tq, lam_init=lam_init, seq=seq),
        grid=(batch, nq),
        in_specs=[pl.BlockSpec((tq, B_WIDTH), qrow), pl.BlockSpec((seq, B_WIDTH), full),
                  pl.BlockSpec((seq, B_WIDTH), full),
                  pl.BlockSpec((4, B_QK_DIM), const), pl.BlockSpec((1, LANES), const)],
        out_specs=pl.BlockSpec((tq, B_WIDTH), qrow),
        out_shape=jax.ShapeDtypeStruct((t, B_WIDTH), BF16),
        scratch_shapes=[pltpu.VMEM((2, LANES, seq), BF16), pltpu.VMEM((2, 4 * tq, LANES), BF16),
                        pltpu.VMEM((2, 1, 4 * tq), F32), pltpu.VMEM((2, 1, 4 * tq), F32),
                        pltpu.VMEM((2, LANES, 4 * tq), F32)],
        compiler_params=_params("parallel", "arbitrary"),
        name="diff_attention",
    )(qb, kb, vb, lam_vec, g_sub2)


def moba_attention(qc, kc, vc, batch, seq):
    tq = MOBA_BLOCK
    t = qc.shape[0]
    nb = seq // tq
    n_sel = min(MOBA_TOPK, nb - 1)
    nbp = 8
    assert seq % tq == 0 and nb <= nbp
    qrow = lambda b, i: (b * nb + i, 0)
    full = lambda b, i: (b, 0)
    return pl.pallas_call(
        functools.partial(_moba_kernel_t, tq=tq, nb=nb, n_sel=n_sel, seq=seq),
        grid=(batch, nb),
        in_specs=[pl.BlockSpec((tq, C_WIDTH), qrow), pl.BlockSpec((seq, C_WIDTH), full),
                  pl.BlockSpec((seq, C_WIDTH), full)],
        out_specs=pl.BlockSpec((tq, C_WIDTH), qrow),
        out_shape=jax.ShapeDtypeStruct((t, C_WIDTH), BF16),
        scratch_shapes=[pltpu.VMEM((nbp, C_WIDTH), F32), pltpu.VMEM((3, LANES, seq), BF16),
                        pltpu.VMEM((3, 2 * tq, LANES), BF16), pltpu.VMEM((3, nbp, 2 * tq), F32),
                        pltpu.VMEM((3, 1, 2 * tq), F32), pltpu.VMEM((3, 1, 2 * tq), F32),
                        pltpu.VMEM((3, LANES, 2 * tq), F32)],
        compiler_params=_params("parallel", "arbitrary"),
        name="moba_attention",
    )(qc, kc, vc)


def _attend_t(k, q_ref, g, v_t, m_ref, l_ref, acc_ref, mask=None):
    for j in range(q_ref.shape[1] // LANES):
        cols = slice(j * LANES, (j + 1) * LANES)
        s = _dot_nt(k, q_ref[g, cols, :])
        if mask is not None:
            s = mask(j, s)
        m_prev = m_ref[g, :, cols]
        m_new = jnp.maximum(m_prev, jnp.max(s, axis=0, keepdims=True))
        alpha = jnp.exp2(m_prev - m_new)
        p = jnp.exp2(s - m_new)
        l_ref[g, :, cols] = alpha * l_ref[g, :, cols] + jnp.sum(p, axis=0, keepdims=True)
        acc_ref[g, :, cols] = alpha * acc_ref[g, :, cols] + _dot(v_t, p.astype(BF16))
        m_ref[g, :, cols] = m_new


def _causal_mask_t(tk, tq):
    key = lax.broadcasted_iota(jnp.int32, (tk, LANES), 0)
    qry = lax.broadcasted_iota(jnp.int32, (tk, LANES), 1)

    def mask(j, s):
        return jnp.where(key <= qry + (j * LANES) % tq, s, NEG)
    return mask


def _transpose_values(v_ref, vt_ref, seq, tk):
    for g in range(vt_ref.shape[0]):
        for n in range(seq // tk):
            blk = v_ref[n * tk:(n + 1) * tk, g * LANES:(g + 1) * LANES].astype(F32)
            vt_ref[g, :, n * tk:(n + 1) * tk] = blk.T.astype(vt_ref.dtype)


def _head_pair_out(acc_t, l, tq):
    even = acc_t[0:HEAD_DIM, 0:tq] / l[:, 0:tq]
    odd = acc_t[HEAD_DIM:LANES, tq:2 * tq] / l[:, tq:2 * tq]
    return jnp.concatenate([even, odd], axis=0).T


def _dsa_kernel_t(qa_ref, qi_ref, wiq_ref, kka_ref, vva_ref, kki_ref, o_ref,
                  sc_ref, hi_ref, lo_ref, vt_ref, qs_ref, m_ref, l_ref, acc_ref, *, tq, k_top, seq):
    i = pl.program_id(1)
    nk = i + 1
    t0 = i * tq
    lo = lax.broadcasted_iota(jnp.int32, (tq, LANES), 1) < HEAD_DIM
    zero_b = jnp.zeros((tq, LANES), BF16)

    @pl.when(i == 0)
    def _():
        _transpose_values(vva_ref, vt_ref, seq, tq)

    def stack_heads(q):
        out = []
        for g in range(q.shape[1] // LANES):
            qg = q[:, g * LANES:(g + 1) * LANES]
            out += [jnp.where(lo, qg, zero_b), jnp.where(lo, zero_b, qg)]
        return out

    qa_stack = stack_heads(qa_ref[...])
    for g in range(3):
        qs_ref[g, 0:tq, :] = qa_stack[2 * g]
        qs_ref[g, tq:2 * tq, :] = qa_stack[2 * g + 1]
    qi_stack = jnp.concatenate(stack_heads(qi_ref[...]), axis=0)
    wi_t = wiq_ref[...].T

    def rows(c):
        return pl.ds(pl.multiple_of(c * tq, tq), tq)

    key_pos = lax.broadcasted_iota(jnp.int32, (tq, tq), 0)
    qry_pos = lax.broadcasted_iota(jnp.int32, (tq, tq), 1)

    def idx_body(c, carry):
        r = jnp.maximum(_dot_nt(kki_ref[rows(c), :], qi_stack), 0.0)
        s = wi_t[0:1, :] * r[:, 0:tq]
        for h in range(1, IDX_HEADS):
            s = s + wi_t[h:h + 1, :] * r[:, h * tq:(h + 1) * tq]
        causal = (c * tq + key_pos) <= (t0 + qry_pos)
        s = jnp.where(causal, s, -jnp.inf)
        s = jnp.where(jnp.abs(s) < MIN_NORMAL_F32, 0.0, s)
        sc_ref[rows(c), :] = s
        bits = lax.bitcast_convert_type(s, jnp.int32)
        key = jnp.where(bits >= 0, bits, bits ^ 0x7FFFFFFF)
        hi_ref[rows(c), :] = lax.shift_right_arithmetic(key, jnp.int32(16)).astype(jnp.int16)
        lo_ref[rows(c), :] = ((key & 0xFFFF) - 32768).astype(jnp.int16)
        return carry

    lax.fori_loop(0, nk, idx_body, 0)

    def count16(ref, cand, strict):
        c16 = cand.astype(jnp.int16)

        def body(c, acc):
            x = ref[rows(c), :]
            hit = jnp.where((x > c16) if strict else (x >= c16), jnp.int16(1), jnp.int16(0))
            for r in range(tq // 16):
                acc = acc + hit[r * 16:(r + 1) * 16]
            return acc
        acc = lax.fori_loop(0, nk, body, jnp.zeros((16, tq), jnp.int16))
        return jnp.sum(acc.astype(F32), axis=0, keepdims=True)

    def search16(ref, k_need):
        v0 = jnp.where(count16(ref, jnp.zeros((1, tq), jnp.int32), False) >= k_need, 0, -32768).astype(jnp.int32)

        def bisect(b, v):
            trial = v | lax.shift_left(jnp.int32(1), 14 - b)
            return jnp.where(count16(ref, trial, False) >= k_need, trial, v)
        return lax.fori_loop(0, 15, bisect, v0)

    kf = float(k_top)

    def search():
        hi_k = search16(hi_ref, kf)
        above = count16(hi_ref, hi_k, True)
        hi16 = hi_k.astype(jnp.int16)

        def keep_bucket(c, carry):
            lo_ref[rows(c), :] = jnp.where(hi_ref[rows(c), :] == hi16, lo_ref[rows(c), :], jnp.int16(-32768))
            return carry
        lax.fori_loop(0, nk, keep_bucket, 0)
        lo_k = search16(lo_ref, kf - above)
        n_gt = above + count16(lo_ref, lo_k, True)
        return lax.shift_left(hi_k, jnp.int32(16)) | (lo_k + 32768), kf - n_gt

    key, need = lax.cond(t0 + tq <= k_top,
                         lambda: (jnp.full((1, tq), INT_MIN, jnp.int32), jnp.full((1, tq), kf, F32)), search)
    thr = _key_to_f32(key)
    thr_next = _key_to_f32(jnp.where(key == 0, MIN_NORMAL_KEY, key + 1))
    all_sel = (t0 + lax.broadcasted_iota(jnp.int32, (1, tq), 1)) < k_top
    lower = (qry_pos <= key_pos).astype(BF16)

    _init_stats(m_ref, l_ref, acc_ref)

    def att_body(c, tie_run):
        s_idx = sc_ref[rows(c), :]
        ge = s_idx >= thr
        gt = s_idx >= thr_next
        tie = jnp.where(gt, 0.0, jnp.where(ge, 1.0, 0.0))
        prefix = _dot(lower, tie.astype(BF16)) + tie_run
        take = jnp.where(gt, 1.0, jnp.where(prefix <= need, tie, 0.0))
        take = jnp.where(all_sel, 1.0, take)
        take = jnp.where(s_idx > -jnp.inf, take, 0.0)

        def mask(j, s):
            q0 = (j * LANES) % tq
            return jnp.where(take[:, q0:q0 + LANES] > 0.5, s, NEG)

        kk = kka_ref[rows(c), :]
        vt = vt_ref[0, :, rows(c)]
        for g in range(3):
            _attend_t(kk, qs_ref, g, vt, m_ref, l_ref, acc_ref, mask)
        return tie_run + jnp.sum(tie, axis=0, keepdims=True)

    lax.fori_loop(0, nk, att_body, jnp.zeros((1, tq), F32))

    for g in range(3):
        o_ref[:, g * LANES:(g + 1) * LANES] = _head_pair_out(acc_ref[g], l_ref[g], tq).astype(o_ref.dtype)


def _diff_kernel_t(qb_ref, kb_ref, vb_ref, lam_ref, gsub_ref, o_ref,
                   vt_ref, qs_ref, m_ref, l_ref, acc_ref, *, tq, lam_init, seq):
    i = pl.program_id(1)
    lane = lax.broadcasted_iota(jnp.int32, (tq, LANES), 1)
    zero_b = jnp.zeros((tq, LANES), BF16)

    @pl.when(i == 0)
    def _():
        _transpose_values(vb_ref, vt_ref, seq, tq)

    qb = qb_ref[...]
    for g in range(2):
        qg = qb[:, g * LANES:(g + 1) * LANES]
        for j in range(4):
            qs_ref[g, j * tq:(j + 1) * tq, :] = jnp.where(lane // B_QK_DIM == j, qg, zero_b)
    _init_stats(m_ref, l_ref, acc_ref)

    def rows(c):
        return pl.ds(pl.multiple_of(c * tq, tq), tq)

    def step(c, masked):
        mask = _causal_mask_t(tq, tq) if masked else None
        for g in range(2):
            _attend_t(kb_ref[rows(c), g * LANES:(g + 1) * LANES], qs_ref, g, vt_ref[g, :, rows(c)],
                      m_ref, l_ref, acc_ref, mask)

    def body(c, carry):
        step(c, False)
        return carry

    lax.fori_loop(0, i, body, 0)
    step(i, True)

    lv = lam_ref[...]
    lam = (jnp.exp(jnp.sum(lv[0:1] * lv[1:2], axis=1, keepdims=True))
           - jnp.exp(jnp.sum(lv[2:3] * lv[3:4], axis=1, keepdims=True)) + lam_init)
    gsub = gsub_ref[...]
    for g in range(2):
        acc_t, l = acc_ref[g], l_ref[g]

        def prob(r0, j):
            return acc_t[r0:r0 + HEAD_DIM, j * tq:(j + 1) * tq] / l[:, j * tq:(j + 1) * tq]

        halves = []
        for r0, j in ((0, 0), (HEAD_DIM, 2)):
            o = prob(r0, j) - lam * prob(r0, j + 1)
            ms = jnp.mean(o * o, axis=0, keepdims=True)
            halves.append(o * lax.rsqrt(ms + SUBLN_EPS))
        y = (jnp.concatenate(halves, axis=0).T * gsub) * (1.0 - lam_init)
        o_ref[:, g * LANES:(g + 1) * LANES] = y.astype(o_ref.dtype)


def _moba_kernel_t(qc_ref, kc_ref, vc_ref, o_ref, kmean_ref, vt_ref, qs_ref, bias_ref, m_ref, l_ref, acc_ref,
                   *, tq, nb, n_sel, seq):
    i = pl.program_id(1)
    nbp = kmean_ref.shape[0]
    lane = lax.broadcasted_iota(jnp.int32, (tq, LANES), 1)
    lo = lane < HEAD_DIM
    zero_b = jnp.zeros((tq, LANES), BF16)

    @pl.when(i == 0)
    def _():
        _transpose_values(vc_ref, vt_ref, seq, tq)
        kmean_ref[...] = jnp.zeros(kmean_ref.shape, F32)
        for n in range(nb):
            kblk = kc_ref[n * tq:(n + 1) * tq, :].astype(F32)
            kmean_ref[n:n + 1, :] = jnp.mean(kblk, axis=0, keepdims=True)

    qc = qc_ref[...]
    sub = lax.broadcasted_iota(jnp.int32, (nbp, 2 * tq), 0)
    past = sub < i
    for g in range(3):
        qg = qc[:, g * LANES:(g + 1) * LANES]
        q2 = jnp.concatenate([jnp.where(lo, qg, zero_b), jnp.where(lo, zero_b, qg)], axis=0)
        km = kmean_ref[:, g * LANES:(g + 1) * LANES]
        km_hi = km.astype(BF16)
        gt = _dot_nt(km_hi, q2) + _dot_nt((km - km_hi.astype(F32)).astype(BF16), q2)
        rows_ = []
        for n in range(nbp):
            gn = gt[n:n + 1, :]
            beats = jnp.where(sub < n, jnp.where(gt >= gn, 1.0, 0.0), jnp.where(gt > gn, 1.0, 0.0))
            beats = jnp.where(sub == n, 0.0, jnp.where(past, beats, 0.0))
            rank = jnp.sum(beats, axis=0, keepdims=True)
            rows_.append(jnp.where(rank < n_sel, 0.0, NEG))
        bias_ref[g] = jnp.where(past, jnp.concatenate(rows_, axis=0), NEG)
        qs_ref[g] = q2
    _init_stats(m_ref, l_ref, acc_ref)

    def rows(c):
        return pl.ds(pl.multiple_of(c * tq, tq), tq)

    def body(c, carry):
        for g in range(3):
            bias = bias_ref[g, pl.ds(c, 1), :]

            def mask(j, s):
                return s + bias[:, j * LANES:(j + 1) * LANES]

            _attend_t(kc_ref[rows(c), g * LANES:(g + 1) * LANES], qs_ref, g, vt_ref[g, :, rows(c)],
                      m_ref, l_ref, acc_ref, mask)
        return carry

    lax.fori_loop(0, i, body, 0)

    causal = _causal_mask_t(tq, tq)
    for g in range(3):
        _attend_t(kc_ref[rows(i), g * LANES:(g + 1) * LANES], qs_ref, g, vt_ref[g, :, rows(i)],
                  m_ref, l_ref, acc_ref, causal)
        o_ref[:, g * LANES:(g + 1) * LANES] = _head_pair_out(acc_ref[g], l_ref[g], tq).astype(o_ref.dtype)


def _route(h, wr_ref, cw_ref, meta_ref, cnt_ref):
    hi = h.astype(BF16)
    lo = (h - hi.astype(F32)).astype(BF16)
    logits = _dot(hi, wr_ref[0]) + (_dot(lo, wr_ref[0]) + _dot(hi, wr_ref[1]))
    lane = lax.broadcasted_iota(jnp.int32, logits.shape, 1)
    lg = jnp.where(lane < N_EXPERTS, logits, -jnp.inf)
    v0 = jnp.max(lg, axis=1, keepdims=True)
    i0 = jnp.min(jnp.where(lg == v0, lane, LANES), axis=1, keepdims=True)
    lg1 = jnp.where(lane == i0, -jnp.inf, lg)
    v1 = jnp.max(lg1, axis=1, keepdims=True)
    i1 = jnp.min(jnp.where(lg1 == v1, lane, LANES), axis=1, keepdims=True)
    e1 = jnp.exp(v1 - v0)
    w0 = 1.0 / (1.0 + e1)
    tm = logits.shape[0]
    routed = jnp.where(lane == i0, 1.0, 0.0) + jnp.where(lane == i1, 1.0, 0.0)
    earlier_rows = (lax.broadcasted_iota(jnp.int32, (tm, tm), 1)
                    < lax.broadcasted_iota(jnp.int32, (tm, tm), 0)).astype(BF16)

    @pl.when(pl.program_id(0) == 0)
    def _():
        cnt_ref[...] = jnp.zeros(cnt_ref.shape, F32)

    before = _dot(earlier_rows, routed.astype(BF16)) + cnt_ref[0:1, :]
    r0 = jnp.sum(jnp.where(lane == i0, before, 0.0), axis=1, keepdims=True)
    r1 = jnp.sum(jnp.where(lane == i1, before, 0.0), axis=1, keepdims=True)
    cnt_ref[0:1, :] = cnt_ref[0:1, :] + jnp.sum(routed, axis=0, keepdims=True)
    cw = (jnp.where(lane == 0, i0.astype(F32), 0.0) + jnp.where(lane == 1, i1.astype(F32), 0.0)
          + jnp.where(lane == 2, w0, 0.0) + jnp.where(lane == 3, e1 * w0, 0.0)
          + jnp.where(lane == 4, r0, 0.0) + jnp.where(lane == 5, r1, 0.0))
    cw_ref[...] = cw
    meta_ref[...] = cw.T[0:8, :].astype(jnp.int32)


def _mixer_out(x_ref, ya_ref, yb_ref, yc_ref, wo_ref, g_ref, mod_ref):
    ab = A_WIDTH + B_WIDTH
    y = (_dot(ya_ref[...], wo_ref[0:A_WIDTH, :]) + _dot(yb_ref[...], wo_ref[A_WIDTH:ab, :])
         + _dot(yc_ref[...], wo_ref[ab:ab + C_WIDTH, :]))
    x1 = x_ref[...] + mod_ref[2:3, :] * y
    return x1, _norm_mod(x1, g_ref[...], mod_ref[3:4, :], mod_ref[4:5, :])


def _mixer_out_specs(tm, d, layer, tiles_per_batch):
    row = lambda i: (i, 0)
    return [pl.BlockSpec((tm, d), row), pl.BlockSpec((tm, A_WIDTH), row), pl.BlockSpec((tm, B_WIDTH), row),
            pl.BlockSpec((tm, C_WIDTH), row), pl.BlockSpec((None, d, d), lambda i: (layer, 0, 0)),
            pl.BlockSpec((1, d), lambda i: (0, 0)),
            pl.BlockSpec((None, 6, d), lambda i: (i // tiles_per_batch, 0, 0))]


def _outproj_kernel(x_ref, ya_ref, yb_ref, yc_ref, wo_ref, g_ref, mod_ref, wr_ref,
                    x1_ref, h_ref, cw_ref, meta_ref, cnt_ref):
    x1, h = _mixer_out(x_ref, ya_ref, yb_ref, yc_ref, wo_ref, g_ref, mod_ref)
    x1_ref[...] = x1
    h_ref[...] = _pack_rows(h)
    _route(h, wr_ref, cw_ref, meta_ref, cnt_ref)


def outproj_router(x, ya, yb, yc, wo, layer, g_ffn, mod, seq, w_router, tm=512):
    t, d = x.shape
    row = lambda i: (i, 0)
    return pl.pallas_call(
        _outproj_kernel,
        grid=(t // tm,),
        in_specs=_mixer_out_specs(tm, d, layer, seq // tm) + [pl.BlockSpec((2, d, LANES), lambda i: (0, 0, 0))],
        out_specs=[pl.BlockSpec((tm, d), row), pl.BlockSpec((tm, d // 2), row), pl.BlockSpec((tm, LANES), row),
                   pl.BlockSpec((8, tm), lambda i: (0, i)), pl.BlockSpec((8, LANES), lambda i: (0, 0))],
        out_shape=[jax.ShapeDtypeStruct((t, d), F32), jax.ShapeDtypeStruct((t, d // 2), jnp.int32),
                   jax.ShapeDtypeStruct((t, LANES), F32), jax.ShapeDtypeStruct((8, t), jnp.int32),
                   jax.ShapeDtypeStruct((8, LANES), F32)],
        compiler_params=_params("arbitrary"),
        name="outproj_router",
    )(x, ya, yb, yc, wo, g_ffn, mod, w_router)


W_ROWS_IN = 128
W_ROWS_DOWN = 352
FF_SPLIT = 2


def _load_weights_bf16(layer, e, w_hbms, w_bs, st_in, st_dn, sems):
    slots = st_in.shape[0]
    plan = []
    for w_hbm, w_b, rb, st, sem0 in ((w_hbms[0], w_bs[0], W_ROWS_IN, st_in, 0), (w_hbms[1], w_bs[1], W_ROWS_IN, st_in, 0),
                                     (w_hbms[2], w_bs[2], W_ROWS_DOWN, st_dn, slots)):
        assert w_b.shape[0] % rb == 0
        plan += [(w_hbm, w_b, r0, rb, st, sem0) for r0 in range(0, w_b.shape[0], rb)]
    uses = {0: 0, slots: 0}
    copies = []
    for w_hbm, w_b, r0, rb, st, sem0 in plan:
        slot = uses[sem0] % slots
        uses[sem0] += 1
        copies.append((pltpu.make_async_copy(w_hbm.at[layer, e, pl.ds(r0, rb), :], st.at[slot],
                                             sems.at[sem0 + slot]), st, slot, w_b, r0, rb))
    ahead = slots - 1
    for copy in copies[:ahead]:
        copy[0].start()
    for b, (copy, st, slot, w_b, r0, rb) in enumerate(copies):
        if b + ahead < len(copies):
            copies[b + ahead][0].start()
        copy.wait()
        w_b[r0:r0 + rb, :] = st[slot].astype(BF16)


def _swiglu_resident(x, wg_b, wu_b, wd_b):
    tf = wg_b.shape[1] // FF_SPLIT
    acc = None
    for f in range(FF_SPLIT):
        cols = slice(f * tf, (f + 1) * tf)
        a = _dot(x, wg_b[:, cols])
        u = _dot(x, wu_b[:, cols])
        act = (a * (1.0 / (1.0 + jnp.exp(-a)))) * u
        part = _dot(act.astype(BF16), wd_b[cols, :])
        acc = part if acc is None else acc + part
    return acc


def _weight_scratch(d, dff, slots):
    return [pltpu.VMEM((d, dff), BF16), pltpu.VMEM((d, dff), BF16), pltpu.VMEM((dff, d), BF16),
            pltpu.VMEM((slots, W_ROWS_IN, dff), F32), pltpu.VMEM((slots, W_ROWS_DOWN, d), F32),
            pltpu.SemaphoreType.DMA((2 * slots,))]


def _layer_out(x, gfin_ref, final):
    if not final:
        return x
    return (x * lax.rsqrt(jnp.mean(x * x, axis=-1, keepdims=True) + EPS)) * gfin_ref[...]


def _ffn_kernel(x_ref, ya_ref, yb_ref, yc_ref, wo_ref, g_ref, mod_ref, gfin_ref, wg_hbm, wu_hbm, wd_hbm, o_ref,
                wg_b, wu_b, wd_b, st_in, st_dn, sems, *, layer, final):
    @pl.when(pl.program_id(0) == 0)
    def _():
        _load_weights_bf16(layer, 0, (wg_hbm, wu_hbm, wd_hbm), (wg_b, wu_b, wd_b), st_in, st_dn, sems)

    x1, h = _mixer_out(x_ref, ya_ref, yb_ref, yc_ref, wo_ref, g_ref, mod_ref)
    out = x1 + mod_ref[5:6, :] * _swiglu_resident(h.astype(BF16), wg_b, wu_b, wd_b)
    o_ref[...] = _layer_out(out, gfin_ref, final)


def outproj_ffn_dense(x, ya, yb, yc, wo, layer, g_ffn, mod, g_final, final, w_gate, w_up, w_down, ff_layer, seq,
                      tm=512):
    t, d = x.shape
    hbm = pl.BlockSpec(memory_space=pl.ANY)
    return pl.pallas_call(
        functools.partial(_ffn_kernel, layer=ff_layer, final=final),
        grid=(t // tm,),
        in_specs=_mixer_out_specs(tm, d, layer, seq // tm) + [pl.BlockSpec((1, d), lambda i: (0, 0)), hbm, hbm, hbm],
        out_specs=pl.BlockSpec((tm, d), lambda i: (i, 0)),
        out_shape=jax.ShapeDtypeStruct((t, d), F32),
        scratch_shapes=_weight_scratch(d, w_gate.shape[3], 2),
        compiler_params=_params("arbitrary"),
        name="ffn_dense",
    )(x, ya, yb, yc, wo, g_ffn, mod, g_final, w_gate, w_up, w_down)


MOE_TILE = 512
MOE_PARTS = 3
SC_CORES, SC_SUBCORES = 2, 16
SC_ROWS = 64
HI16 = -65536


def _pack_rows(x):
    c = x.shape[1] // 2
    bits = lax.bitcast_convert_type(x.astype(jnp.bfloat16).astype(F32), jnp.int32)
    return lax.shift_right_logical(bits[:, :c], jnp.int32(16)) | (bits[:, c:] & jnp.int32(HI16))


def _unpack_rows(w):
    lo = lax.bitcast_convert_type(lax.shift_left(w, jnp.int32(16)), F32)
    hi = lax.bitcast_convert_type(w & jnp.int32(HI16), F32)
    return jnp.concatenate([lo, hi], axis=1)


def sc_gather_rows(table, idx):
    d = table.shape[1]
    b = idx.shape[0]
    per_worker = b // (SC_CORES * SC_SUBCORES)
    assert per_worker * SC_CORES * SC_SUBCORES == b and per_worker % (2 * SC_ROWS) == 0
    mesh = plsc.VectorSubcoreMesh(core_axis_name="c", subcore_axis_name="s")
    idx_buf = pltpu.VMEM((SC_ROWS,), jnp.int32)
    row_buf = pltpu.VMEM((SC_ROWS, d), table.dtype)

    @functools.partial(
        pl.kernel, mesh=mesh, out_type=jax.ShapeDtypeStruct((b, d), table.dtype),
        scratch_types=[idx_buf, idx_buf, row_buf, row_buf] + [pltpu.SemaphoreType.DMA] * 4,
        name="sc_gather_rows")
    def gather(table_hbm, idx_hbm, out_hbm, idx0, idx1, rows0, rows1, sem_g0, sem_g1, sem_w0, sem_w1):
        base = (lax.axis_index("s") * SC_CORES + lax.axis_index("c")) * per_worker

        @pl.loop(0, per_worker // (2 * SC_ROWS))
        def _(pair):
            off0 = pl.multiple_of(base + pair * (2 * SC_ROWS), SC_ROWS)
            off1 = pl.multiple_of(off0 + SC_ROWS, SC_ROWS)
            pltpu.sync_copy(idx_hbm.at[pl.ds(off0, SC_ROWS)], idx0)
            pltpu.sync_copy(idx_hbm.at[pl.ds(off1, SC_ROWS)], idx1)
            gather0 = pltpu.async_copy(table_hbm.at[idx0], rows0, sem_g0)
            gather1 = pltpu.async_copy(table_hbm.at[idx1], rows1, sem_g1)
            gather0.wait()
            write0 = pltpu.async_copy(rows0, out_hbm.at[pl.ds(off0, SC_ROWS)], sem_w0)
            gather1.wait()
            write1 = pltpu.async_copy(rows1, out_hbm.at[pl.ds(off1, SC_ROWS)], sem_w1)
            write0.wait()
            write1.wait()

    return gather(table, idx)


def _moe_layout(meta, counts, n_exp):
    t = meta.shape[1]
    e0, e1, rank0, rank1 = meta[0], meta[1], meta[4], meta[5]
    padded = (counts + MOE_TILE - 1) // MOE_TILE * MOE_TILE
    ends = jnp.cumsum(padded)
    offs = ends - padded
    pos0 = offs[e0] + rank0
    pos1 = offs[e1] + rank1
    n_rows = 2 * t + n_exp * MOE_TILE
    tok = jnp.arange(t, dtype=jnp.int32)
    src = jnp.zeros((n_rows,), jnp.int32).at[jnp.concatenate([pos0, pos1])].set(
        jnp.concatenate([tok, tok]), unique_indices=True, mode="promise_in_bounds")
    n_tiles = n_rows // MOE_TILE
    n_valid = ends[-1] // MOE_TILE
    tile_expert = jnp.sum(jnp.arange(n_tiles)[:, None] * MOE_TILE >= ends[None, :], axis=1)
    tile_expert = tile_expert[jnp.minimum(jnp.arange(n_tiles), n_valid - 1)].astype(jnp.int32)
    return pos0, pos1, src, tile_expert, n_valid.astype(jnp.int32).reshape(1)


W_SLOTS = 4


def _experts_packed_kernel(te_ref, nv_ref, first_ref, x_ref, wg_hbm, wu_hbm, wd_hbm, *rest, layer):
    o_ref, wg_b, wu_b, wd_b, st_in, st_dn, sems = rest[-7:]
    j = pl.program_id(0)
    valid = j < nv_ref[0]

    @pl.when(valid & (first_ref[j] == 1))
    def _():
        _load_weights_bf16(layer, te_ref[j], (wg_hbm, wu_hbm, wd_hbm), (wg_b, wu_b, wd_b), st_in, st_dn, sems)

    @pl.when(valid)
    def _():
        o_ref[...] = _pack_rows(_swiglu_resident(_unpack_rows(x_ref[...]).astype(BF16), wg_b, wu_b, wd_b))


def moe_experts_packed(x_part, tile_expert, n_valid, w_gate, w_up, w_down, layer, y_prev, tile0, n_rows):
    rows_part, half = x_part.shape
    d = 2 * half
    dff = w_gate.shape[3]
    tiles_part = rows_part // MOE_TILE
    first = jnp.concatenate([jnp.ones((1,), jnp.int32),
                             (tile_expert[1:] != tile_expert[:-1]).astype(jnp.int32)])

    def tile(j, nv):
        return jnp.maximum(jnp.minimum(j, nv[0] - 1), 0)

    hbm = pl.BlockSpec(memory_space=pl.ANY)
    in_specs = [pl.BlockSpec((MOE_TILE, half), lambda j, te, nv, fi: (tile(j, nv), 0)), hbm, hbm, hbm]
    args = [tile_expert, n_valid, first, x_part, w_gate, w_up, w_down]
    aliases = {}
    if y_prev is not None:
        in_specs.append(hbm)
        args.append(y_prev)
        aliases = {len(args) - 1: 0}
    return pl.pallas_call(
        functools.partial(_experts_packed_kernel, layer=layer),
        grid_spec=pltpu.PrefetchScalarGridSpec(
            num_scalar_prefetch=3, grid=(tiles_part,), in_specs=in_specs,
            out_specs=pl.BlockSpec((MOE_TILE, half), lambda j, te, nv, fi: (tile0 + tile(j, nv), 0)),
            scratch_shapes=_weight_scratch(d, dff, W_SLOTS)),
        out_shape=jax.ShapeDtypeStruct((n_rows, half), jnp.int32),
        input_output_aliases=aliases,
        compiler_params=_params("arbitrary"),
        name="moe_experts",
    )(*args)


def _combine_packed_kernel(y0_ref, y1_ref, route_ref, x1_ref, mod_ref, gfin_ref, o_ref, *, final):
    rt = route_ref[...]
    f = rt[:, 2:3] * _unpack_rows(y0_ref[...]) + rt[:, 3:4] * _unpack_rows(y1_ref[...])
    o_ref[...] = _layer_out(x1_ref[...] + mod_ref[5:6, :] * f, gfin_ref, final)


def moe_combine_packed(y_pairs, route, x1, mod, g_final, final, seq, tm=512):
    t, d = x1.shape
    nt = t // tm
    tiles_per_batch = seq // tm
    row = lambda i: (i, 0)
    return pl.pallas_call(
        functools.partial(_combine_packed_kernel, final=final),
        grid=(nt,),
        in_specs=[pl.BlockSpec((tm, d // 2), row), pl.BlockSpec((tm, d // 2), lambda i: (i + nt, 0)),
                  pl.BlockSpec((tm, LANES), row), pl.BlockSpec((tm, d), row),
                  pl.BlockSpec((None, 6, d), lambda i: (i // tiles_per_batch, 0, 0)),
                  pl.BlockSpec((1, d), lambda i: (0, 0))],
        out_specs=pl.BlockSpec((tm, d), row),
        out_shape=jax.ShapeDtypeStruct((t, d), F32),
        compiler_params=_params("parallel"),
        name="moe_combine",
    )(y_pairs, y_pairs, route, x1, mod, g_final)


def ffn_moe_sc(x1, h_packed, route, meta, counts, mod, g_final, final, w_gate, w_up, w_down, layer, seq):
    n_exp = w_gate.shape[1]
    pos0, pos1, src, tile_expert, n_valid = _moe_layout(meta, counts[0, :n_exp].astype(jnp.int32), n_exp)
    n_rows = src.shape[0]
    unit = 2 * SC_ROWS * SC_CORES * SC_SUBCORES // MOE_TILE
    n_units = n_rows // (unit * MOE_TILE)
    assert n_units * unit * MOE_TILE == n_rows and n_units >= MOE_PARTS
    cuts = [0, max(1, (2 * n_units) // 9), max(2, (5 * n_units) // 9), n_units]
    bounds = [c * unit for c in cuts]
    h_parts = [sc_gather_rows(h_packed, src[bounds[p] * MOE_TILE:bounds[p + 1] * MOE_TILE])
               for p in range(MOE_PARTS)]
    y_sorted = None
    for p in range(MOE_PARTS):
        t0, t1 = bounds[p], bounds[p + 1]
        n_valid_part = jnp.clip(n_valid - t0, 0, t1 - t0)
        y_sorted = moe_experts_packed(h_parts[p], tile_expert[t0:t1], n_valid_part,
                                      w_gate, w_up, w_down, layer, y_sorted, t0, n_rows)
    y_pairs = sc_gather_rows(y_sorted, jnp.concatenate([pos0, pos1]))
    return moe_combine_packed(y_pairs, route, x1, mod, g_final, final, seq)


def _rope_tables(positions, dim):
    rot = dim // ROPE_FRACTION
    half = rot // 2
    inv = 1.0 / (ROPE_THETA ** (np.arange(0, rot, 2, dtype=np.float32) / rot))
    ang = positions.reshape(-1).astype(F32)[:, None] * jnp.asarray(inv, F32)
    cos, sin = jnp.cos(ang), jnp.sin(ang)
    t = ang.shape[0]
    ones = jnp.ones((t, dim - rot), F32)
    zeros = lambda w: jnp.zeros((t, w), F32)
    reps = LANES // dim
    c = jnp.tile(jnp.concatenate([cos, cos, ones], axis=1), (1, reps))
    sa = jnp.tile(jnp.concatenate([-sin, zeros(dim - half)], axis=1), (1, reps))
    sb = jnp.tile(jnp.concatenate([zeros(half), sin, zeros(dim - rot)], axis=1), (1, reps))
    return c, sa, sb


def _relayout_w_in(w):
    pts = np.cumsum([0, 384, 64, 64, 256, 64, 4, 256, 256, 256, 384, 384, 384])
    (q_a, k_a, v_a, q_i, k_i, w_i, q_b, k_b, v_b, q_c, k_c, v_c) = [w[..., pts[j]:pts[j + 1]] for j in range(12)]
    w_i_pad = jnp.concatenate([w_i, jnp.zeros(w.shape[:-1] + (LANES - IDX_HEADS,), w.dtype)], axis=-1)
    return jnp.concatenate([q_a, k_a, k_a, v_a, v_a, q_i, k_i, k_i, w_i_pad,
                            q_b, k_b, v_b, q_c, k_c, v_c], axis=-1).astype(BF16)


def kernel(x, c, positions, w_in, w_out, diff_lambda, diff_subln, w_ada, b_ada, g_attn, g_ffn, w_ff_gate,
           w_ff_up, w_ff_down, w_router, w_exp_gate, w_exp_up, w_exp_down, g_final):
    batch, seq, d = x.shape
    depth = w_in.shape[0]
    t = batch * seq
    tabs = _rope_tables(positions, HEAD_DIM) + _rope_tables(positions, B_QK_DIM)
    mod_all = adaln_mod(c, w_ada, b_ada).reshape(depth, batch, 6, d)
    w_in_pad = _relayout_w_in(w_in)
    wo = w_out.astype(BF16)
    g_fin = g_final.reshape(1, d)
    xf = x.reshape(t, d)
    for layer in range(depth):
        mod = mod_all[layer]
        lam_init = 0.8 - 0.6 * math.exp(-0.3 * layer)
        (qa, kka, vva, qi, kki, wi, qb, kb, vb, qc, kc, vc) = inproj(
            xf, g_attn[layer].reshape(1, d), mod, tabs, w_in_pad, layer, seq)
        ya = dsa_attention(qa, qi, wi, kka, vva, kki, batch, seq)
        g_sub2 = jnp.tile(diff_subln[layer], 2).reshape(1, LANES)
        yb = diff_attention(qb, kb, vb, diff_lambda[layer], g_sub2, lam_init, batch, seq)
        yc = moba_attention(qc, kc, vc, batch, seq)
        j = layer // 2
        gf = g_ffn[layer].reshape(1, d)
        final = layer == depth - 1
        if layer % 2 == 0:
            xf = outproj_ffn_dense(xf, ya, yb, yc, wo, layer, gf, mod, g_fin, final,
                                   w_ff_gate[:, None], w_ff_up[:, None], w_ff_down[:, None], j, seq)
        else:
            wr = jnp.concatenate([w_router[j], jnp.zeros((d, LANES - N_EXPERTS), F32)], axis=1)
            wr_hi = wr.astype(BF16)
            wr = jnp.stack([wr_hi, (wr - wr_hi.astype(F32)).astype(BF16)])
            x1, h_packed, route, meta, counts = outproj_router(xf, ya, yb, yc, wo, layer, gf, mod, seq, wr)
            xf = ffn_moe_sc(x1, h_packed, route, meta, counts, mod, g_fin, final,
                            w_exp_gate, w_exp_up, w_exp_down, j, seq)
    return xf.reshape(batch, seq, d)
```
